```python
import math
import jax, jax.numpy as jnp
from jax import lax
import numpy as np

D_MODEL = 1024
BATCH = 4
SEQ = 4096
DEPTH = 1

ATT_HEADS = 8
ATT_KV_HEADS = 2
ATT_HEAD_DIM = 64
ATT_GROUP = ATT_HEADS // ATT_KV_HEADS
ATT_Q_DIM = ATT_HEADS * ATT_HEAD_DIM
ATT_KV_DIM = ATT_KV_HEADS * ATT_HEAD_DIM
WINDOW = 128
ATT_BLOCK = 128
ROPE_DIM = ATT_HEAD_DIM // 4
ROPE_THETA = 500000.0

GLA_HEADS = 4
GLA_KEY_DIM = D_MODEL // 2
GLA_VAL_DIM = D_MODEL
GLA_DK = GLA_KEY_DIM // GLA_HEADS
GLA_DV = GLA_VAL_DIM // GLA_HEADS
GLA_GATE_RANK = 16
GLA_GATE_NORM = 16.0
GLA_CHUNK = 64

N_EXPERTS = 16
EXPERT_FF = D_MODEL
CAPACITY_FACTOR = 2

PLE_DIM = 256

EPS = 1e-6

IN_SIZES = (ATT_Q_DIM, ATT_KV_DIM, ATT_KV_DIM,
            GLA_KEY_DIM, GLA_KEY_DIM, GLA_VAL_DIM, GLA_VAL_DIM,
            GLA_GATE_RANK, GLA_GATE_RANK, D_MODEL, D_MODEL)
IN_DIM = ATT_Q_DIM + 2 * ATT_KV_DIM + 2 * GLA_KEY_DIM + 2 * GLA_VAL_DIM + 2 * GLA_GATE_RANK + 2 * D_MODEL

kernel_name = "hybrid_swa_gla_ecmoe_block"


def rmsnorm(x, g):
    xf = x.astype(jnp.float32)
    y = xf * lax.rsqrt(jnp.mean(xf * xf, axis=-1, keepdims=True) + EPS)
    return (y * g.astype(jnp.float32)).astype(x.dtype)


def split_columns(proj):
    parts, start = [], 0
    for size in IN_SIZES:
        parts.append(proj[..., start:start + size])
        start += size
    return parts


def partial_rope(t, positions):
    half = ROPE_DIM // 2
    inv_freq = ROPE_THETA ** (-jnp.arange(0, ROPE_DIM, 2, dtype=jnp.float32) / ROPE_DIM)
    ang = positions.astype(jnp.float32)[..., None] * inv_freq
    cos = jnp.cos(ang)[:, :, None, :].astype(t.dtype)
    sin = jnp.sin(ang)[:, :, None, :].astype(t.dtype)
    t1 = t[..., :half]
    t2 = t[..., half:ROPE_DIM]
    rot = jnp.concatenate([t1 * cos - t2 * sin, t2 * cos + t1 * sin], axis=-1)
    return jnp.concatenate([rot, t[..., ROPE_DIM:]], axis=-1)


def windowed_gqa(q, k, v, sink):
    B, S = q.shape[0], q.shape[1]
    nb = S // ATT_BLOCK
    qb = q.reshape(B, nb, ATT_BLOCK, ATT_KV_HEADS, ATT_GROUP, ATT_HEAD_DIM)

    def band(t):
        tp = jnp.pad(t, ((0, 0), (ATT_BLOCK, ATT_BLOCK), (0, 0), (0, 0)))
        tp = tp.reshape(B, nb + 2, ATT_BLOCK, ATT_KV_HEADS, ATT_HEAD_DIM)
        return jnp.concatenate([tp[:, :-2], tp[:, 1:-1], tp[:, 2:]], axis=2)

    kb, vb = band(k), band(v)
    scores = jnp.einsum('bnqhgd,bnkhd->bnhgqk', qb, kb).astype(jnp.float32) * (ATT_HEAD_DIM ** -0.5)
    blk = jnp.arange(nb)[:, None, None]
    qi = blk * ATT_BLOCK + jnp.arange(ATT_BLOCK)[None, :, None]
    kj = (blk - 1) * ATT_BLOCK + jnp.arange(3 * ATT_BLOCK)[None, None, :]
    valid = (jnp.abs(qi - kj) <= WINDOW) & (kj >= 0) & (kj < S)
    scores = jnp.where(valid[None, :, None, None], scores, -jnp.inf)
    sink_b = sink.astype(jnp.float32).reshape(1, 1, ATT_KV_HEADS, ATT_GROUP, 1, 1)
    m = jnp.maximum(jnp.max(scores, axis=-1, keepdims=True), sink_b)
    e = jnp.exp(scores - m)
    probs = e / (jnp.sum(e, axis=-1, keepdims=True) + jnp.exp(sink_b - m))
    out = jnp.einsum('bnhgqk,bnkhd->bnqhgd', probs.astype(v.dtype), vb)
    return out.reshape(B, S, ATT_Q_DIM)


def gla_chunked(q, k, v, log_a, strict):
    B, S, H, dk = q.shape
    dv = v.shape[-1]
    nc = S // GLA_CHUNK

    def chunk(t):
        return t.reshape(B, nc, GLA_CHUNK, H, t.shape[-1])

    qc, kc, vc, ac = chunk(q), chunk(k), chunk(v), chunk(log_a)
    b = jnp.cumsum(ac, axis=2)
    g = b[:, :, -1]
    q_t = qc * jnp.exp(b)
    k_t = kc * jnp.exp(-b)
    k_end = kc * jnp.exp(g[:, :, None] - b)
    attn = jnp.einsum('bclhd,bcmhd->bchlm', q_t, k_t)
    mask = jnp.tril(jnp.ones((GLA_CHUNK, GLA_CHUNK), dtype=bool), -1 if strict else 0)
    attn = jnp.where(mask, attn, 0.0)
    o_intra = jnp.einsum('bchlm,bcmhe->bclhe', attn, vc)
    kv = jnp.einsum('bclhd,bclhe->bchde', k_end, vc)
    decay = jnp.exp(g)

    def step(state, xs):
        dec, kv_c = xs
        return dec[..., None] * state + kv_c, state

    s0 = jnp.zeros((B, H, dk, dv), dtype=jnp.float32)
    _, s_prev = lax.scan(step, s0, (jnp.moveaxis(decay, 1, 0), jnp.moveaxis(kv, 1, 0)))
    s_prev = jnp.moveaxis(s_prev, 0, 1)
    o_inter = jnp.einsum('bclhd,bchde->bclhe', q_t, s_prev)
    return (o_intra + o_inter).reshape(B, S, H, dv)


def gla_branch(gq, gk, gv, gr, z_f, z_b, up_f, bias_f, up_b, bias_b, norm_gain):
    B, S = gq.shape[0], gq.shape[1]
    f32 = jnp.float32
    q = gq.astype(f32).reshape(B, S, GLA_HEADS, GLA_DK) * (GLA_DK ** -0.5)
    k = gk.astype(f32).reshape(B, S, GLA_HEADS, GLA_DK)
    v = gv.astype(f32).reshape(B, S, GLA_HEADS, GLA_DV)
    la_f = (jax.nn.log_sigmoid(z_f.astype(f32) @ up_f.astype(f32) + bias_f.astype(f32)) / GLA_GATE_NORM).reshape(B, S, GLA_HEADS, GLA_DK)
    la_b = (jax.nn.log_sigmoid(z_b.astype(f32) @ up_b.astype(f32) + bias_b.astype(f32)) / GLA_GATE_NORM).reshape(B, S, GLA_HEADS, GLA_DK)
    o_f = gla_chunked(q, k, v, la_f, strict=False)
    flip = lambda t: jnp.flip(t, axis=1)
    o_b = flip(gla_chunked(flip(q), flip(k), flip(v), flip(la_b), strict=True))
    o = o_f + o_b
    o = o * lax.rsqrt(jnp.mean(o * o, axis=-1, keepdims=True) + EPS) * norm_gain.astype(f32).reshape(GLA_HEADS, GLA_DV)
    o = o.reshape(B, S, GLA_VAL_DIM) * jax.nn.silu(gr.astype(f32))
    return o.astype(gq.dtype)


def expert_choice_ffn(xn, w_router, w_gate, w_up, w_down):
    B, S, D = xn.shape
    cap = CAPACITY_FACTOR * S // N_EXPERTS
    aff = jax.nn.softmax(jnp.einsum('bsd,de->bse', xn, w_router).astype(jnp.float32), axis=-1)
    top_aff, top_idx = lax.top_k(jnp.swapaxes(aff, 1, 2), cap)
    bidx = jnp.arange(B)[:, None, None]
    xg = xn[bidx, top_idx]
    hid = jax.nn.silu(jnp.einsum('becd,edf->becf', xg, w_gate)) * jnp.einsum('becd,edf->becf', xg, w_up)
    y = jnp.einsum('becf,efd->becd', hid, w_down) * top_aff[..., None].astype(xn.dtype)
    return jnp.zeros((B, S, D), dtype=y.dtype).at[bidx, top_idx].add(y)


def setup_inputs(seed: int = 0) -> dict:
    key = jax.random.key(seed)
    ks = jax.random.split(key, 32)
    f32 = jnp.float32
    L, D = DEPTH, D_MODEL

    def w(k, shape, fan_in):
        return jax.random.normal(k, shape, f32) * (fan_in ** -0.5)

    def gain(k, shape):
        return 1.0 + 0.1 * jax.random.normal(k, shape, f32)

    x = jax.random.normal(ks[0], (BATCH, SEQ, D), f32)
    p = jax.random.normal(ks[1], (L, BATCH, SEQ, PLE_DIM), f32)
    offsets = jax.random.randint(ks[2], (BATCH, 1), 0, 1024, dtype=jnp.int32)
    positions = jnp.arange(SEQ, dtype=jnp.int32)[None, :] + offsets
    return {
        "x": x,
        "p": p,
        "positions": positions,
        "norm_mix": gain(ks[3], (L, D)),
        "w_in": w(ks[4], (L, D, IN_DIM), D),
        "gla_gate_up_fwd": w(ks[5], (L, GLA_GATE_RANK, GLA_KEY_DIM), GLA_GATE_RANK),
        "gla_gate_bias_fwd": 1.0 + 0.5 * jax.random.normal(ks[6], (L, GLA_KEY_DIM), f32),
        "gla_gate_up_bwd": w(ks[7], (L, GLA_GATE_RANK, GLA_KEY_DIM), GLA_GATE_RANK),
        "gla_gate_bias_bwd": 1.0 + 0.5 * jax.random.normal(ks[8], (L, GLA_KEY_DIM), f32),
        "attn_sink": 0.5 * jax.random.normal(ks[9], (L, ATT_HEADS), f32),
        "gla_norm": gain(ks[10], (L, GLA_VAL_DIM)),
        "w_branch_attn": w(ks[11], (L, ATT_Q_DIM, D), ATT_Q_DIM),
        "w_branch_gla": w(ks[12], (L, GLA_VAL_DIM, D), GLA_VAL_DIM),
        "w_out": w(ks[13], (L, D, D), D),
        "norm_ffn": gain(ks[14], (L, D)),
        "w_router": w(ks[15], (L, D, N_EXPERTS), D),
        "w_exp_gate": w(ks[16], (L, N_EXPERTS, D, EXPERT_FF), D),
        "w_exp_up": w(ks[17], (L, N_EXPERTS, D, EXPERT_FF), D),
        "w_exp_down": w(ks[18], (L, N_EXPERTS, EXPERT_FF, D), EXPERT_FF),
        "norm_ple": gain(ks[19], (L, D)),
        "w_ple_gate": w(ks[20], (L, D, D), D),
        "w_ple": w(ks[21], (L, PLE_DIM, D), PLE_DIM),
        "norm_final": gain(ks[22], (D,)),
    }


def reference(x, p, positions, norm_mix, w_in, gla_gate_up_fwd, gla_gate_bias_fwd,
              gla_gate_up_bwd, gla_gate_bias_bwd, attn_sink, gla_norm, w_branch_attn,
              w_branch_gla, w_out, norm_ffn, w_router, w_exp_gate, w_exp_up, w_exp_down,
              norm_ple, w_ple_gate, w_ple, norm_final):
    B, S, _ = x.shape
    h = x
    for l in range(DEPTH):
        a = rmsnorm(h, norm_mix[l])
        proj = a @ w_in[l]
        aq, ak, av, gq, gk, gv, gr, z_f, z_b, g_att, g_gla = split_columns(proj)
        aq = partial_rope(aq.reshape(B, S, ATT_HEADS, ATT_HEAD_DIM), positions)
        ak = partial_rope(ak.reshape(B, S, ATT_KV_HEADS, ATT_HEAD_DIM), positions)
        av = av.reshape(B, S, ATT_KV_HEADS, ATT_HEAD_DIM)
        y_att = windowed_gqa(aq, ak, av, attn_sink[l]) @ w_branch_attn[l]
        y_gla = gla_branch(gq, gk, gv, gr, z_f, z_b,
                           gla_gate_up_fwd[l], gla_gate_bias_fwd[l],
                           gla_gate_up_bwd[l], gla_gate_bias_bwd[l], gla_norm[l]) @ w_branch_gla[l]
        merged = jax.nn.sigmoid(g_att) * y_att + jax.nn.sigmoid(g_gla) * y_gla
        h = h + merged @ w_out[l]
        h = h + expert_choice_ffn(rmsnorm(h, norm_ffn[l]), w_router[l],
                                  w_exp_gate[l], w_exp_up[l], w_exp_down[l])
        ple_gate = jax.nn.sigmoid(rmsnorm(h, norm_ple[l]) @ w_ple_gate[l])
        h = h + ple_gate * (p[l] @ w_ple[l])
    return rmsnorm(h, norm_final)
```

```python
import functools
import math

import jax
import jax.numpy as jnp
from jax import lax
from jax.experimental import pallas as pl
from jax.experimental.pallas import tpu as pltpu

D_MODEL = 1024
ATT_HEADS = 8
ATT_KV_HEADS = 2
ATT_HEAD_DIM = 64
ATT_GROUP = ATT_HEADS // ATT_KV_HEADS
ATT_Q_DIM = ATT_HEADS * ATT_HEAD_DIM
ATT_KV_DIM = ATT_KV_HEADS * ATT_HEAD_DIM
WINDOW = 128
ROPE_DIM = ATT_HEAD_DIM // 4
ROPE_THETA = 500000.0
GLA_HEADS = 4
GLA_KEY_DIM = D_MODEL // 2
GLA_VAL_DIM = D_MODEL
GLA_DK = GLA_KEY_DIM // GLA_HEADS
GLA_DV = GLA_VAL_DIM // GLA_HEADS
GLA_GATE_RANK = 16
GLA_GATE_NORM = 16.0
GLA_CHUNK = 64
N_EXPERTS = 16
EXPERT_FF = D_MODEL
CAPACITY_FACTOR = 2
PLE_DIM = 256
EPS = 1e-6

LANES = 128
MIB = 1024 * 1024
BF16 = jnp.bfloat16
F32 = jnp.float32
HI = lax.Precision.HIGHEST

NT_DIMS = (((1,), (1,)), ((), ()))
TN_DIMS = (((0,), (0,)), ((), ()))


def _cparams(sem, vmem_mib):
    return pltpu.CompilerParams(dimension_semantics=sem, vmem_limit_bytes=vmem_mib * MIB)


def _full(shape):
    n = len(shape)
    return pl.BlockSpec(shape, lambda *_: (0,) * n)


def _rms(x, gain):
    ms = jnp.mean(x * x, axis=-1, keepdims=True)
    return x * lax.rsqrt(ms + EPS) * gain


def _sigmoid(x):
    return 1.0 / (1.0 + jnp.exp(-x))


def _rope(t, cos_t, sin_t, first_half):
    fwd = pltpu.roll(t, LANES - ROPE_DIM // 2, axis=1)
    bwd = pltpu.roll(t, ROPE_DIM // 2, axis=1)
    return t * cos_t + jnp.where(first_half, fwd, bwd) * sin_t


def _in_proj_kernel(x_ref, pos_ref, invf_ref, gain_ref, wqkv_ref, wgqk_ref, wgv_ref, wgr_ref,
                    wz_ref, wgate_ref,
                    q_ref, k_ref, v_ref, gq_ref, gk_ref, gv_ref, gr_ref, z_ref, sga_ref, sgg_ref):
    a = _rms(x_ref[...], gain_ref[...]).astype(BF16)

    lane = lax.broadcasted_iota(jnp.int32, (1, LANES), 1)
    d = lane % ATT_HEAD_DIM
    first_half = d < ROPE_DIM // 2
    in_rope = d < ROPE_DIM
    ang = pos_ref[...].astype(F32) * invf_ref[...]
    cos_t = jnp.where(in_rope, jnp.cos(ang), 1.0)
    sin_t = jnp.where(first_half, -jnp.sin(ang), jnp.where(in_rope, jnp.sin(ang), 0.0))

    qkv = jnp.dot(a, wqkv_ref[...], preferred_element_type=F32)
    scale = ATT_HEAD_DIM ** -0.5
    for g in range(ATT_Q_DIM // LANES):
        t = qkv[:, g * LANES:(g + 1) * LANES]
        q_ref[:, g * LANES:(g + 1) * LANES] = (_rope(t, cos_t, sin_t, first_half) * scale).astype(BF16)
    k_ref[...] = _rope(qkv[:, ATT_Q_DIM:ATT_Q_DIM + ATT_KV_DIM], cos_t, sin_t, first_half).astype(BF16)
    v_ref[...] = qkv[:, ATT_Q_DIM + ATT_KV_DIM:].astype(BF16)

    gqk = jnp.dot(a, wgqk_ref[...], preferred_element_type=F32)
    gq_ref[...] = (gqk[:, :GLA_KEY_DIM] * (GLA_DK ** -0.5)).astype(BF16)
    gk_ref[...] = gqk[:, GLA_KEY_DIM:].astype(BF16)
    gv_ref[...] = jnp.dot(a, wgv_ref[...], preferred_element_type=F32).astype(BF16)
    gr_ref[...] = jnp.dot(a, wgr_ref[...], preferred_element_type=F32).astype(BF16)
    z_ref[...] = jnp.dot(a, wz_ref[...], preferred_element_type=F32)
    gates = jnp.dot(a, wgate_ref[...], preferred_element_type=F32)
    sga_ref[...] = _sigmoid(gates[:, :D_MODEL]).astype(BF16)
    sgg_ref[...] = _sigmoid(gates[:, D_MODEL:]).astype(BF16)


def _in_proj(x2, pos2, invf, gain, wqkv, wgqk, wgv, wgr, wz, wgate, tm):
    T = x2.shape[0]
    row = lambda n: pl.BlockSpec((tm, n), lambda i: (i, 0))
    out_widths = (ATT_Q_DIM, ATT_KV_DIM, ATT_KV_DIM, GLA_KEY_DIM, GLA_KEY_DIM, GLA_VAL_DIM,
                  GLA_VAL_DIM, 2 * GLA_GATE_RANK, D_MODEL, D_MODEL)
    out_dtypes = (BF16,) * 7 + (F32, BF16, BF16)
    return pl.pallas_call(
        _in_proj_kernel,
        grid=(T // tm,),
        in_specs=[row(D_MODEL), row(1), _full(invf.shape), _full(gain.shape), _full(wqkv.shape),
                  _full(wgqk.shape), _full(wgv.shape), _full(wgr.shape), _full(wz.shape),
                  _full(wgate.shape)],
        out_specs=[row(n) for n in out_widths],
        out_shape=[jax.ShapeDtypeStruct((T, n), dt) for n, dt in zip(out_widths, out_dtypes)],
        compiler_params=_cparams(("parallel",), 56),
        name="in_proj",
    )(x2, pos2, invf, gain, wqkv, wgqk, wgv, wgr, wz, wgate)


def _swa_kernel(sink_ref, q_ref, k_ref, v_ref, o_ref, *, tq, seq):
    blk = WINDOW
    span = 3 * blk
    n = pl.program_id(1)
    for sb in range(tq // blk):
        q0 = n * tq + sb * blk
        start = pl.multiple_of(jnp.clip(q0 - blk, 0, seq - span), blk)
        kw = k_ref[0, pl.ds(start, span), :]
        vw = v_ref[0, pl.ds(start, span), :]
        qi = q0 + lax.broadcasted_iota(jnp.int32, (blk, span), 0)
        kj = start + lax.broadcasted_iota(jnp.int32, (blk, span), 1)
        valid = jnp.abs(qi - kj) <= WINDOW
        for h in range(ATT_HEADS):
            g = h // ATT_GROUP
            qh = q_ref[0, sb * blk:(sb + 1) * blk, h * ATT_HEAD_DIM:(h + 1) * ATT_HEAD_DIM]
            kg = kw[:, g * ATT_HEAD_DIM:(g + 1) * ATT_HEAD_DIM]
            vg = vw[:, g * ATT_HEAD_DIM:(g + 1) * ATT_HEAD_DIM]
            s = lax.dot_general(qh, kg, NT_DIMS, preferred_element_type=F32)
            s = jnp.where(valid, s, -jnp.inf)
            sink = sink_ref[h]
            m = jnp.maximum(jnp.max(s, axis=-1, keepdims=True), sink)
            e = jnp.exp(s - m)
            den = jnp.sum(e, axis=-1, keepdims=True) + jnp.exp(sink - m)
            o = jnp.dot(e.astype(BF16), vg, preferred_element_type=F32) / den
            o_ref[0, sb * blk:(sb + 1) * blk, h * ATT_HEAD_DIM:(h + 1) * ATT_HEAD_DIM] = o.astype(BF16)


def _swa(sink, q, k, v, tq):
    B, S, _ = q.shape
    return pl.pallas_call(
        functools.partial(_swa_kernel, tq=tq, seq=S),
        grid=(B, S // tq),
        in_specs=[pl.BlockSpec(memory_space=pltpu.SMEM),
                  pl.BlockSpec((1, tq, ATT_Q_DIM), lambda b, n: (b, n, 0)),
                  pl.BlockSpec((1, S, ATT_KV_DIM), lambda b, n: (b, 0, 0)),
                  pl.BlockSpec((1, S, ATT_KV_DIM), lambda b, n: (b, 0, 0))],
        out_specs=pl.BlockSpec((1, tq, ATT_Q_DIM), lambda b, n: (b, n, 0)),
        out_shape=jax.ShapeDtypeStruct((B, S, ATT_Q_DIM), BF16),
        compiler_params=_cparams(("parallel", "parallel"), 32),
        name="swa",
    )(sink, q, k, v)


def _log_sigmoid(u):
    return jnp.minimum(u, 0.0) - jnp.log1p(jnp.exp(-jnp.abs(u)))


def _gla_kernel(z_ref, q_ref, k_ref, v_ref, r_ref, upf_ref, biasf_ref, upb_ref, biasb_ref, gain_ref,
                o_ref, cf_ref, cb_ref, st_ref, sf_ref, sb_ref, *, seq):
    L = GLA_CHUNK
    nc = seq // L
    grp = 4 * L

    row = lax.broadcasted_iota(jnp.int32, (grp, grp), 0)
    col = lax.broadcasted_iota(jnp.int32, (grp, grp), 1)
    same = (row // L) == (col // L)
    tri_lo = jnp.where(same & (col <= row), 1.0, 0.0).astype(F32)
    tri_up = jnp.where(same & (col >= row), 1.0, 0.0).astype(F32)

    def cum_body(i, carry):
        r0 = pl.multiple_of(i * grp, grp)
        z = z_ref[0, pl.ds(r0, grp), :]
        uf = jnp.dot(z, upf_ref[...], preferred_element_type=F32, precision=HI) + biasf_ref[...]
        ub = jnp.dot(z, upb_ref[...], preferred_element_type=F32, precision=HI) + biasb_ref[...]
        laf = _log_sigmoid(uf) / GLA_GATE_NORM
        lab = _log_sigmoid(ub) / GLA_GATE_NORM
        cf_ref[pl.ds(r0, grp), :] = jnp.dot(tri_lo, laf, preferred_element_type=F32, precision=HI)
        cb_ref[pl.ds(r0, grp), :] = jnp.dot(tri_up, lab, preferred_element_type=F32, precision=HI)
        return carry

    lax.fori_loop(0, seq // grp, cum_body, 0)

    sf_ref[...] = jnp.zeros_like(sf_ref)
    sb_ref[...] = jnp.zeros_like(sb_ref)

    def state_body(i, carry):
        rf = pl.multiple_of(i * L, L)
        cf = cf_ref[pl.ds(rf, L), :]
        gf = cf_ref[pl.ds(rf + L - 1, 1), :]
        kf = k_ref[0, pl.ds(rf, L), :].astype(F32)
        kend = (kf * jnp.exp(gf - cf)).astype(BF16)
        kv = lax.dot_general(v_ref[0, pl.ds(rf, L), :], kend, TN_DIMS, preferred_element_type=F32)
        st_ref[i, :, 0:GLA_DK] = sf_ref[...].astype(BF16)
        sf_ref[...] = sf_ref[...] * jnp.exp(gf) + kv

        j = nc - 1 - i
        rb = pl.multiple_of(j * L, L)
        cb = cb_ref[pl.ds(rb, L), :]
        gb = cb_ref[pl.ds(rb, 1), :]
        kb = k_ref[0, pl.ds(rb, L), :].astype(F32)
        kend_b = (kb * jnp.exp(gb - cb)).astype(BF16)
        kv_b = lax.dot_general(v_ref[0, pl.ds(rb, L), :], kend_b, TN_DIMS, preferred_element_type=F32)
        st_ref[j, :, GLA_DK:2 * GLA_DK] = sb_ref[...].astype(BF16)
        sb_ref[...] = sb_ref[...] * jnp.exp(gb) + kv_b
        return carry

    lax.fori_loop(0, nc, state_body, 0)

    lrow = lax.broadcasted_iota(jnp.int32, (L, L), 0)
    lcol = lax.broadcasted_iota(jnp.int32, (L, L), 1)
    causal = lrow >= lcol

    def out_body(c, carry):
        r0 = pl.multiple_of(c * L, L)
        q = q_ref[0, pl.ds(r0, L), :].astype(F32)
        k = k_ref[0, pl.ds(r0, L), :].astype(F32)
        v = v_ref[0, pl.ds(r0, L), :]
        cf = cf_ref[pl.ds(r0, L), :]
        cb = cb_ref[pl.ds(r0, L), :]
        qf = (q * jnp.exp(cf)).astype(BF16)
        kf = (k * jnp.exp(-cf)).astype(BF16)
        qb = (q * jnp.exp(cb)).astype(BF16)
        kb = (k * jnp.exp(-cb)).astype(BF16)
        af = lax.dot_general(qf, kf, NT_DIMS, preferred_element_type=F32)
        ab = lax.dot_general(qb, kb, NT_DIMS, preferred_element_type=F32)
        attn = jnp.where(causal, af, ab).astype(BF16)
        o = jnp.dot(attn, v, preferred_element_type=F32)
        o = o + lax.dot_general(jnp.concatenate([qf, qb], axis=1), st_ref[c], NT_DIMS,
                                preferred_element_type=F32)
        o = _rms(o, gain_ref[...])
        r = r_ref[0, pl.ds(r0, L), :].astype(F32)
        o_ref[0, pl.ds(r0, L), :] = (o * (r * _sigmoid(r))).astype(BF16)
        return carry

    lax.fori_loop(0, nc, out_body, 0)


def _gla(z, gq, gk, gv, gr, upf, biasf, upb, biasb, gain):
    B, S, _ = gq.shape
    nc = S // GLA_CHUNK
    seq_blk = lambda n: pl.BlockSpec((1, S, n), lambda b, h: (b, 0, h))
    head_blk = lambda r, n: pl.BlockSpec((r, n), lambda b, h: (0, h))
    return pl.pallas_call(
        functools.partial(_gla_kernel, seq=S),
        grid=(B, GLA_HEADS),
        in_specs=[pl.BlockSpec((1, S, 2 * GLA_GATE_RANK), lambda b, h: (b, 0, 0)),
                  seq_blk(GLA_DK), seq_blk(GLA_DK), seq_blk(GLA_DV), seq_blk(GLA_DV),
                  head_blk(2 * GLA_GATE_RANK, GLA_DK), head_blk(1, GLA_DK),
                  head_blk(2 * GLA_GATE_RANK, GLA_DK), head_blk(1, GLA_DK),
                  head_blk(1, GLA_DV)],
        out_specs=seq_blk(GLA_DV),
        out_shape=jax.ShapeDtypeStruct((B, S, GLA_VAL_DIM), BF16),
        scratch_shapes=[pltpu.VMEM((S, GLA_DK), F32), pltpu.VMEM((S, GLA_DK), F32),
                        pltpu.VMEM((nc, GLA_DV, 2 * GLA_DK), BF16),
                        pltpu.VMEM((GLA_DV, GLA_DK), F32), pltpu.VMEM((GLA_DV, GLA_DK), F32)],
        compiler_params=_cparams(("parallel", "parallel"), 48),
        name="gla",
    )(z, gq, gk, gv, gr, upf, biasf, upb, biasb, gain)


def _mix_out_kernel(x_ref, a_ref, g_ref, sga_ref, sgg_ref, wa_ref, wb_ref, wo_ref, gain_ref, wr_ref,
                    h_ref, xn_ref, aff_ref):
    y_att = jnp.dot(a_ref[...], wa_ref[...], preferred_element_type=F32)
    y_gla = jnp.dot(g_ref[...], wb_ref[...], preferred_element_type=F32)
    merged = sga_ref[...].astype(F32) * y_att + sgg_ref[...].astype(F32) * y_gla
    h = x_ref[...] + jnp.dot(merged.astype(BF16), wo_ref[...], preferred_element_type=F32)
    h_ref[...] = h
    xn = _rms(h, gain_ref[...])
    xn_ref[...] = xn
    hi = xn.astype(BF16)
    lo = (xn - hi.astype(F32)).astype(BF16)
    part = jnp.dot(jnp.concatenate([hi, lo], axis=1), wr_ref[...], preferred_element_type=F32)
    logits = part[:, :N_EXPERTS] + part[:, N_EXPERTS:]
    m = jnp.max(logits, axis=-1, keepdims=True)
    e = jnp.exp(logits - m)
    aff_ref[...] = e / jnp.sum(e, axis=-1, keepdims=True)


def _mix_out(x2, a, g, sga, sgg, wa, wb, wo, gain, wr, tm):
    T = x2.shape[0]
    row = lambda n: pl.BlockSpec((tm, n), lambda i: (i, 0))
    return pl.pallas_call(
        _mix_out_kernel,
        grid=(T // tm,),
        in_specs=[row(D_MODEL), row(ATT_Q_DIM), row(GLA_VAL_DIM), row(D_MODEL), row(D_MODEL),
                  _full(wa.shape), _full(wb.shape), _full(wo.shape), _full(gain.shape), _full(wr.shape)],
        out_specs=[row(D_MODEL), row(D_MODEL), row(N_EXPERTS)],
        out_shape=[jax.ShapeDtypeStruct((T, D_MODEL), F32), jax.ShapeDtypeStruct((T, D_MODEL), F32),
                   jax.ShapeDtypeStruct((T, N_EXPERTS), F32)],
        compiler_params=_cparams(("parallel",), 48),
        name="mix_out",
    )(x2, a, g, sga, sgg, wa, wb, wo, gain, wr)


def _route_kernel(aff_ref, idx_ref, cum_ref, *, cap, seq):
    E = N_EXPERTS
    bits = lax.bitcast_convert_type(aff_ref[0], jnp.int32)
    count = lambda mask: jnp.sum(mask.astype(jnp.int32), axis=1, keepdims=True)

    def thr_body(t, thr):
        cand = thr | jnp.left_shift(jnp.int32(1), 30 - t)
        return jnp.where(count(bits >= cand) >= cap, cand, thr)

    thr = lax.fori_loop(0, 31, thr_body, jnp.zeros((E, 1), jnp.int32))
    above = bits > thr
    tie = bits == thr
    need = cap - count(above)

    pos = lax.broadcasted_iota(jnp.int32, (E, seq), 1)

    def tie_body(t, last):
        cand = last | jnp.left_shift(jnp.int32(1), (seq.bit_length() - 2) - t)
        return jnp.where(count(tie & (pos < cand)) < need, cand, last)

    last = lax.fori_loop(0, seq.bit_length() - 1, tie_body, jnp.zeros((E, 1), jnp.int32))
    sel = (above | (tie & (pos <= last))).astype(BF16)

    tri = (lax.broadcasted_iota(jnp.int32, (LANES, LANES), 0)
           <= lax.broadcasted_iota(jnp.int32, (LANES, LANES), 1)).astype(BF16)
    carry = jnp.zeros((E, 1), F32)
    for t in range(seq // LANES):
        local = jnp.dot(sel[:, t * LANES:(t + 1) * LANES], tri, preferred_element_type=F32) + carry
        cum_ref[:, t * LANES:(t + 1) * LANES] = local
        carry = local[:, LANES - 1:LANES]

    slot = lax.broadcasted_iota(jnp.int32, (cap, LANES), 0).astype(F32)
    for e in range(E):
        def cnt_body(t, acc):
            c = cum_ref[e:e + 1, pl.ds(pl.multiple_of(t * LANES, LANES), LANES)]
            return acc + jnp.where(c <= slot, 1.0, 0.0)

        acc = lax.fori_loop(0, seq // LANES, cnt_body, jnp.zeros((cap, LANES), F32))
        idx_ref[0, :, e:e + 1] = jnp.sum(acc, axis=1, keepdims=True).astype(jnp.int32)


def _route(aff_t, cap):
    B, E, S = aff_t.shape
    return pl.pallas_call(
        functools.partial(_route_kernel, cap=cap, seq=S),
        grid=(B,),
        in_specs=[pl.BlockSpec((1, E, S), lambda b: (b, 0, 0))],
        out_specs=pl.BlockSpec((1, cap, E), lambda b: (b, 0, 0)),
        out_shape=jax.ShapeDtypeStruct((B, cap, E), jnp.int32),
        scratch_shapes=[pltpu.VMEM((E, S), F32)],
        compiler_params=_cparams(("parallel",), 32),
        name="route",
    )(aff_t)


def _gather_kernel(idx_ref, xn_ref, aff_ref, xg_ref, wg_ref, xs_ref, ws_ref, *, cap):
    b = pl.program_id(0)
    e = pl.program_id(1)
    base = (b * N_EXPERTS + e) * cap

    def body(i, carry):
        t = idx_ref[base + i]
        xs_ref[pl.ds(i, 1), :] = xn_ref[0, pl.ds(t, 1), :]
        ws_ref[pl.ds(i, 1), :] = aff_ref[0, pl.ds(t, 1), :]
        return carry

    lax.fori_loop(0, cap, body, 0, unroll=8)
    xg_ref[0, 0] = xs_ref[...].astype(BF16)
    lane = lax.broadcasted_iota(jnp.int32, (cap, N_EXPERTS), 1)
    wg_ref[0, 0] = jnp.sum(jnp.where(lane == e, ws_ref[...], 0.0), axis=1, keepdims=True)


def _gather(idx_flat, xn, aff, cap):
    B, S, D = xn.shape
    E = N_EXPERTS
    grid_spec = pltpu.PrefetchScalarGridSpec(
        num_scalar_prefetch=1,
        grid=(B, E),
        in_specs=[pl.BlockSpec((1, S, D), lambda b, e, idx: (b, 0, 0)),
                  pl.BlockSpec((1, S, E), lambda b, e, idx: (b, 0, 0))],
        out_specs=[pl.BlockSpec((1, 1, cap, D), lambda b, e, idx: (b, e, 0, 0)),
                   pl.BlockSpec((1, 1, cap, 1), lambda b, e, idx: (b, e, 0, 0))],
        scratch_shapes=[pltpu.VMEM((cap, D), F32), pltpu.VMEM((cap, E), F32)],
    )
    return pl.pallas_call(
        functools.partial(_gather_kernel, cap=cap),
        grid_spec=grid_spec,
        out_shape=[jax.ShapeDtypeStruct((B, E, cap, D), BF16),
                   jax.ShapeDtypeStruct((B, E, cap, 1), F32)],
        compiler_params=_cparams(("arbitrary", "arbitrary"), 48),
        name="gather",
    )(idx_flat, xn, aff)


def _ffn_kernel(xg_ref, wg_ref, w1_ref, w2_ref, w3_ref, y_ref):
    xg = xg_ref[0, 0]
    gate = jnp.dot(xg, w1_ref[0], preferred_element_type=F32)
    up = jnp.dot(xg, w2_ref[0], preferred_element_type=F32)
    hid = (gate * _sigmoid(gate) * up).astype(BF16)
    y_ref[0, 0] = jnp.dot(hid, w3_ref[0], preferred_element_type=F32) * wg_ref[0, 0]


def _ffn(xg, wg, w1, w2, w3):
    B, E, C, D = xg.shape
    F = w1.shape[-1]
    return pl.pallas_call(
        _ffn_kernel,
        grid=(E, B),
        in_specs=[pl.BlockSpec((1, 1, C, D), lambda e, b: (b, e, 0, 0)),
                  pl.BlockSpec((1, 1, C, 1), lambda e, b: (b, e, 0, 0)),
                  pl.BlockSpec((1, D, F), lambda e, b: (e, 0, 0)),
                  pl.BlockSpec((1, D, F), lambda e, b: (e, 0, 0)),
                  pl.BlockSpec((1, F, D), lambda e, b: (e, 0, 0))],
        out_specs=pl.BlockSpec((1, 1, C, D), lambda e, b: (b, e, 0, 0)),
        out_shape=jax.ShapeDtypeStruct((B, E, C, D), F32),
        compiler_params=_cparams(("parallel", "parallel"), 48),
        name="ffn",
    )(xg, wg, w1, w2, w3)


def _scatter_kernel(idx_ref, y_ref, o_ref, *, cap):
    b = pl.program_id(0)
    e = pl.program_id(1)
    base = (b * N_EXPERTS + e) * cap

    @pl.when(e == 0)
    def _():
        o_ref[...] = jnp.zeros_like(o_ref)

    def body(i, carry):
        t = idx_ref[base + i]
        o_ref[0, pl.ds(t, 1), :] = o_ref[0, pl.ds(t, 1), :] + y_ref[0, 0, pl.ds(i, 1), :]
        return carry

    lax.fori_loop(0, cap, body, 0, unroll=8)


def _scatter(idx_flat, y, seq):
    B, E, C, D = y.shape
    grid_spec = pltpu.PrefetchScalarGridSpec(
        num_scalar_prefetch=1,
        grid=(B, E),
        in_specs=[pl.BlockSpec((1, 1, C, D), lambda b, e, idx: (b, e, 0, 0))],
        out_specs=pl.BlockSpec((1, seq, D), lambda b, e, idx: (b, 0, 0)),
    )
    return pl.pallas_call(
        functools.partial(_scatter_kernel, cap=C),
        grid_spec=grid_spec,
        out_shape=jax.ShapeDtypeStruct((B, seq, D), F32),
        compiler_params=_cparams(("arbitrary", "arbitrary"), 48),
        name="scatter",
    )(idx_flat, y)


def _ple_out_kernel(h_ref, moe_ref, p_ref, gple_ref, wpg_ref, wple_ref, gfin_ref, o_ref):
    h = h_ref[...] + moe_ref[...]
    n = _rms(h, gple_ref[...]).astype(BF16)
    gate = _sigmoid(jnp.dot(n, wpg_ref[...], preferred_element_type=F32))
    emb = jnp.dot(p_ref[...].astype(BF16), wple_ref[...], preferred_element_type=F32)
    o_ref[...] = _rms(h + gate * emb, gfin_ref[...])


def _ple_out(h, moe, p2, gple, wpg, wple, gfin, tm):
    T = h.shape[0]
    row = lambda n: pl.BlockSpec((tm, n), lambda i: (i, 0))
    return pl.pallas_call(
        _ple_out_kernel,
        grid=(T // tm,),
        in_specs=[row(D_MODEL), row(D_MODEL), row(PLE_DIM), _full(gple.shape), _full(wpg.shape),
                  _full(wple.shape), _full(gfin.shape)],
        out_specs=row(D_MODEL),
        out_shape=jax.ShapeDtypeStruct((T, D_MODEL), F32),
        compiler_params=_cparams(("parallel",), 48),
        name="ple_out",
    )(h, moe, p2, gple, wpg, wple, gfin)


def kernel(x, p, positions, norm_mix, w_in, gla_gate_up_fwd, gla_gate_bias_fwd, gla_gate_up_bwd, gla_gate_bias_bwd, attn_sink, gla_norm, w_branch_attn, w_branch_gla, w_out, norm_ffn, w_router, w_exp_gate, w_exp_up, w_exp_down, norm_ple, w_ple_gate, w_ple, norm_final):
    B, S, D = x.shape
    T = B * S
    depth = w_in.shape[0]
    cap = CAPACITY_FACTOR * S // N_EXPERTS
    R = GLA_GATE_RANK

    pos2 = positions.reshape(T, 1)
    inv_freq = ROPE_THETA ** (-jnp.arange(0, ROPE_DIM, 2, dtype=F32) / ROPE_DIM)
    invf = jnp.tile(inv_freq, LANES // (ROPE_DIM // 2)).reshape(1, LANES)

    h = x.reshape(T, D)
    for l in range(depth):
        o = 0
        cols = {}
        for name, n in (("qkv", ATT_Q_DIM + 2 * ATT_KV_DIM), ("gqk", 2 * GLA_KEY_DIM), ("gv", GLA_VAL_DIM),
                        ("gr", GLA_VAL_DIM), ("z", 2 * R), ("gate", 2 * D_MODEL)):
            cols[name] = w_in[l][:, o:o + n].astype(BF16)
            o += n
        zeros = jnp.zeros((R, GLA_KEY_DIM), F32)
        upf = jnp.concatenate([gla_gate_up_fwd[l], zeros], axis=0)
        upb = jnp.concatenate([zeros, gla_gate_up_bwd[l]], axis=0)
        wr = w_router[l]
        wr_hi = wr.astype(BF16)
        wr_lo = (wr - wr_hi.astype(F32)).astype(BF16)
        wr2 = jnp.concatenate([jnp.concatenate([wr_hi, wr_lo], axis=1),
                               jnp.concatenate([wr_hi, jnp.zeros_like(wr_lo)], axis=1)], axis=0)

        q, k, v, gq, gk, gv, gr, z, sga, sgg = _in_proj(
            h, pos2, invf, norm_mix[l].reshape(1, D), cols["qkv"], cols["gqk"], cols["gv"], cols["gr"],
            cols["z"], cols["gate"], tm=256)

        att = _swa(attn_sink[l], q.reshape(B, S, -1), k.reshape(B, S, -1), v.reshape(B, S, -1), tq=512)
        gla = _gla(z.reshape(B, S, -1), gq.reshape(B, S, -1), gk.reshape(B, S, -1), gv.reshape(B, S, -1),
                   gr.reshape(B, S, -1), upf, gla_gate_bias_fwd[l].reshape(1, -1), upb,
                   gla_gate_bias_bwd[l].reshape(1, -1), gla_norm[l].reshape(1, -1))

        h1, xn, aff = _mix_out(h, att.reshape(T, -1), gla.reshape(T, -1), sga, sgg,
                               w_branch_attn[l].astype(BF16), w_branch_gla[l].astype(BF16),
                               w_out[l].astype(BF16), norm_ffn[l].reshape(1, D), wr2, tm=256)

        aff3 = aff.reshape(B, S, N_EXPERTS)
        idx = _route(jnp.swapaxes(aff3, 1, 2), cap)
        idx_flat = jnp.swapaxes(idx, 1, 2).reshape(-1)
        xg, wg = _gather(idx_flat, xn.reshape(B, S, D), aff3, cap)
        y = _ffn(xg, wg, w_exp_gate[l].astype(BF16), w_exp_up[l].astype(BF16), w_exp_down[l].astype(BF16))
        moe = _scatter(idx_flat, y, S)

        last = l == depth - 1
        gfin = norm_final.reshape(1, D)
        assert last, "the final norm is fused into the last layer's PLE kernel"
        h = _ple_out(h1, moe.reshape(T, D), p[l].reshape(T, PLE_DIM), norm_ple[l].reshape(1, D),
                     w_ple_gate[l].astype(BF16), w_ple[l].astype(BF16), gfin, tm=256)
    return h.reshape(B, S, D)
```

```python
import functools
import math

import jax
import jax.numpy as jnp
from jax import lax
from jax.experimental import pallas as pl
from jax.experimental.pallas import tpu as pltpu

D_MODEL = 1024
ATT_HEADS = 8
ATT_KV_HEADS = 2
ATT_HEAD_DIM = 64
ATT_GROUP = ATT_HEADS // ATT_KV_HEADS
ATT_Q_DIM = ATT_HEADS * ATT_HEAD_DIM
ATT_KV_DIM = ATT_KV_HEADS * ATT_HEAD_DIM
WINDOW = 128
ROPE_DIM = ATT_HEAD_DIM // 4
ROPE_THETA = 500000.0
GLA_HEADS = 4
GLA_KEY_DIM = D_MODEL // 2
GLA_VAL_DIM = D_MODEL
GLA_DK = GLA_KEY_DIM // GLA_HEADS
GLA_DV = GLA_VAL_DIM // GLA_HEADS
GLA_GATE_RANK = 16
GLA_GATE_NORM = 16.0
GLA_CHUNK = 64
N_EXPERTS = 16
EXPERT_FF = D_MODEL
CAPACITY_FACTOR = 2
PLE_DIM = 256
EPS = 1e-6

LANES = 128
MIB = 1024 * 1024
BF16 = jnp.bfloat16
F32 = jnp.float32
HI = lax.Precision.HIGHEST

NT_DIMS = (((1,), (1,)), ((), ()))
TN_DIMS = (((0,), (0,)), ((), ()))


def _cparams(sem, vmem_mib):
    return pltpu.CompilerParams(dimension_semantics=sem, vmem_limit_bytes=vmem_mib * MIB)


def _full(shape):
    n = len(shape)
    return pl.BlockSpec(shape, lambda *_: (0,) * n)


def _rms(x, gain):
    ms = jnp.mean(x * x, axis=-1, keepdims=True)
    return x * lax.rsqrt(ms + EPS) * gain


def _sigmoid(x):
    return 1.0 / (1.0 + jnp.exp(-x))


def _rope(t, cos_t, sin_t, first_half):
    fwd = pltpu.roll(t, LANES - ROPE_DIM // 2, axis=1)
    bwd = pltpu.roll(t, ROPE_DIM // 2, axis=1)
    return t * cos_t + jnp.where(first_half, fwd, bwd) * sin_t


def _in_proj_kernel(x_ref, pos_ref, invf_ref, gain_ref, wqkv_ref, wgqk_ref, wgv_ref, wgr_ref,
                    wz_ref, wgate_ref,
                    q_ref, k_ref, v_ref, gq_ref, gk_ref, gv_ref, gr_ref, z_ref, sga_ref, sgg_ref):
    a = _rms(x_ref[...], gain_ref[...]).astype(BF16)

    lane = lax.broadcasted_iota(jnp.int32, (1, LANES), 1)
    d = lane % ATT_HEAD_DIM
    first_half = d < ROPE_DIM // 2
    in_rope = d < ROPE_DIM
    ang = pos_ref[...].astype(F32) * invf_ref[...]
    cos_t = jnp.where(in_rope, jnp.cos(ang), 1.0)
    sin_t = jnp.where(first_half, -jnp.sin(ang), jnp.where(in_rope, jnp.sin(ang), 0.0))

    qkv = jnp.dot(a, wqkv_ref[...], preferred_element_type=F32)
    scale = ATT_HEAD_DIM ** -0.5
    for g in range(ATT_Q_DIM // LANES):
        t = qkv[:, g * LANES:(g + 1) * LANES]
        q_ref[:, g * LANES:(g + 1) * LANES] = (_rope(t, cos_t, sin_t, first_half) * scale).astype(BF16)
    k_ref[...] = _rope(qkv[:, ATT_Q_DIM:ATT_Q_DIM + ATT_KV_DIM], cos_t, sin_t, first_half).astype(BF16)
    v_ref[...] = qkv[:, ATT_Q_DIM + ATT_KV_DIM:].astype(BF16)

    gqk = jnp.dot(a, wgqk_ref[...], preferred_element_type=F32)
    gq_ref[...] = (gqk[:, :GLA_KEY_DIM] * (GLA_DK ** -0.5)).astype(BF16)
    gk_ref[...] = gqk[:, GLA_KEY_DIM:].astype(BF16)
    gv_ref[...] = jnp.dot(a, wgv_ref[...], preferred_element_type=F32).astype(BF16)
    gr_ref[...] = jnp.dot(a, wgr_ref[...], preferred_element_type=F32).astype(BF16)
    z_ref[...] = jnp.dot(a, wz_ref[...], preferred_element_type=F32)
    gates = jnp.dot(a, wgate_ref[...], preferred_element_type=F32)
    sga_ref[...] = _sigmoid(gates[:, :D_MODEL]).astype(BF16)
    sgg_ref[...] = _sigmoid(gates[:, D_MODEL:]).astype(BF16)


def _in_proj(x2, pos2, invf, gain, wqkv, wgqk, wgv, wgr, wz, wgate, tm):
    T = x2.shape[0]
    row = lambda n: pl.BlockSpec((tm, n), lambda i: (i, 0))
    out_widths = (ATT_Q_DIM, ATT_KV_DIM, ATT_KV_DIM, GLA_KEY_DIM, GLA_KEY_DIM, GLA_VAL_DIM,
                  GLA_VAL_DIM, 2 * GLA_GATE_RANK, D_MODEL, D_MODEL)
    out_dtypes = (BF16,) * 7 + (F32, BF16, BF16)
    return pl.pallas_call(
        _in_proj_kernel,
        grid=(T // tm,),
        in_specs=[row(D_MODEL), row(1), _full(invf.shape), _full(gain.shape), _full(wqkv.shape),
                  _full(wgqk.shape), _full(wgv.shape), _full(wgr.shape), _full(wz.shape),
                  _full(wgate.shape)],
        out_specs=[row(n) for n in out_widths],
        out_shape=[jax.ShapeDtypeStruct((T, n), dt) for n, dt in zip(out_widths, out_dtypes)],
        compiler_params=_cparams(("parallel",), 56),
        name="in_proj",
    )(x2, pos2, invf, gain, wqkv, wgqk, wgv, wgr, wz, wgate)


def _swa_kernel(sink_ref, q_ref, k_ref, v_ref, o_ref, *, tq, seq):
    blk = WINDOW
    span = 3 * blk
    n = pl.program_id(1)
    for sb in range(tq // blk):
        q0 = n * tq + sb * blk
        start = pl.multiple_of(jnp.clip(q0 - blk, 0, seq - span), blk)
        kw = k_ref[0, pl.ds(start, span), :]
        vw = v_ref[0, pl.ds(start, span), :]
        qi = q0 + lax.broadcasted_iota(jnp.int32, (blk, span), 0)
        kj = start + lax.broadcasted_iota(jnp.int32, (blk, span), 1)
        valid = jnp.abs(qi - kj) <= WINDOW
        for h in range(ATT_HEADS):
            g = h // ATT_GROUP
            qh = q_ref[0, sb * blk:(sb + 1) * blk, h * ATT_HEAD_DIM:(h + 1) * ATT_HEAD_DIM]
            kg = kw[:, g * ATT_HEAD_DIM:(g + 1) * ATT_HEAD_DIM]
            vg = vw[:, g * ATT_HEAD_DIM:(g + 1) * ATT_HEAD_DIM]
            s = lax.dot_general(qh, kg, NT_DIMS, preferred_element_type=F32)
            s = jnp.where(valid, s, -jnp.inf)
            sink = sink_ref[h]
            m = jnp.maximum(jnp.max(s, axis=-1, keepdims=True), sink)
            e = jnp.exp(s - m)
            den = jnp.sum(e, axis=-1, keepdims=True) + jnp.exp(sink - m)
            o = jnp.dot(e.astype(BF16), vg, preferred_element_type=F32) / den
            o_ref[0, sb * blk:(sb + 1) * blk, h * ATT_HEAD_DIM:(h + 1) * ATT_HEAD_DIM] = o.astype(BF16)


def _swa(sink, q, k, v, tq):
    B, S, _ = q.shape
    return pl.pallas_call(
        functools.partial(_swa_kernel, tq=tq, seq=S),
        grid=(B, S // tq),
        in_specs=[pl.BlockSpec(memory_space=pltpu.SMEM),
                  pl.BlockSpec((1, tq, ATT_Q_DIM), lambda b, n: (b, n, 0)),
                  pl.BlockSpec((1, S, ATT_KV_DIM), lambda b, n: (b, 0, 0)),
                  pl.BlockSpec((1, S, ATT_KV_DIM), lambda b, n: (b, 0, 0))],
        out_specs=pl.BlockSpec((1, tq, ATT_Q_DIM), lambda b, n: (b, n, 0)),
        out_shape=jax.ShapeDtypeStruct((B, S, ATT_Q_DIM), BF16),
        compiler_params=_cparams(("parallel", "parallel"), 32),
        name="swa",
    )(sink, q, k, v)


def _log_sigmoid(u):
    return jnp.minimum(u, 0.0) - jnp.log(1.0 + jnp.exp(-jnp.abs(u)))


def _split2(x):
    hi = x.astype(BF16)
    return hi, (x - hi.astype(F32)).astype(BF16)


def _gla_kernel(z_ref, q_ref, k_ref, v_ref, r_ref, uph_ref, upl_ref, bias_ref, gain_ref,
                o_ref, cf_ref, cb_ref, st_ref, sf_ref, sb_ref, *, seq, unroll):
    L = GLA_CHUNK
    nc = seq // L
    grp = 4 * L

    row = lax.broadcasted_iota(jnp.int32, (grp, grp), 0)
    col = lax.broadcasted_iota(jnp.int32, (grp, grp), 1)
    same = (row // L) == (col // L)
    tri_lo = jnp.where(same & (col <= row), 1.0, 0.0).astype(BF16)
    tri_up = jnp.where(same & (col >= row), 1.0, 0.0).astype(BF16)
    mm = functools.partial(jnp.dot, preferred_element_type=F32)

    def cum_body(i, carry):
        r0 = pl.multiple_of(i * grp, grp)
        zh, zl = _split2(z_ref[0, pl.ds(r0, grp), :])
        u = mm(zh, uph_ref[...]) + mm(zl, uph_ref[...]) + mm(zh, upl_ref[...]) + bias_ref[...]
        la = _log_sigmoid(u) * (1.0 / GLA_GATE_NORM)
        fh, fl = _split2(la[:, :GLA_DK])
        bh, bl = _split2(la[:, GLA_DK:])
        cf_ref[pl.ds(r0, grp), :] = mm(tri_lo, fh) + mm(tri_lo, fl)
        cb_ref[pl.ds(r0, grp), :] = mm(tri_up, bh) + mm(tri_up, bl)
        return carry

    lax.fori_loop(0, seq // grp, cum_body, 0, unroll=2)

    sf_ref[...] = jnp.zeros_like(sf_ref)
    sb_ref[...] = jnp.zeros_like(sb_ref)

    def state_body(i, carry):
        rf = pl.multiple_of(i * L, L)
        cf = cf_ref[pl.ds(rf, L), :]
        gf = cf_ref[pl.ds(rf + L - 1, 1), :]
        kf = k_ref[0, pl.ds(rf, L), :].astype(F32)
        kend = (kf * jnp.exp(gf - cf)).astype(BF16)
        kv = lax.dot_general(v_ref[0, pl.ds(rf, L), :], kend, TN_DIMS, preferred_element_type=F32)
        st_ref[i, :, 0:GLA_DK] = sf_ref[...].astype(BF16)
        sf_ref[...] = sf_ref[...] * jnp.exp(gf) + kv

        j = nc - 1 - i
        rb = pl.multiple_of(j * L, L)
        cb = cb_ref[pl.ds(rb, L), :]
        gb = cb_ref[pl.ds(rb, 1), :]
        kb = k_ref[0, pl.ds(rb, L), :].astype(F32)
        kend_b = (kb * jnp.exp(gb - cb)).astype(BF16)
        kv_b = lax.dot_general(v_ref[0, pl.ds(rb, L), :], kend_b, TN_DIMS, preferred_element_type=F32)
        st_ref[j, :, GLA_DK:2 * GLA_DK] = sb_ref[...].astype(BF16)
        sb_ref[...] = sb_ref[...] * jnp.exp(gb) + kv_b
        return carry

    lax.fori_loop(0, nc, state_body, 0, unroll=unroll)

    lrow = lax.broadcasted_iota(jnp.int32, (L, L), 0)
    lcol = lax.broadcasted_iota(jnp.int32, (L, L), 1)
    causal = lrow >= lcol

    def out_body(c, carry):
        r0 = pl.multiple_of(c * L, L)
        q = q_ref[0, pl.ds(r0, L), :].astype(F32)
        k = k_ref[0, pl.ds(r0, L), :].astype(F32)
        v = v_ref[0, pl.ds(r0, L), :]
        cf = cf_ref[pl.ds(r0, L), :]
        cb = cb_ref[pl.ds(r0, L), :]
        qf = (q * jnp.exp(cf)).astype(BF16)
        kf = (k * jnp.exp(-cf)).astype(BF16)
        qb = (q * jnp.exp(cb)).astype(BF16)
        kb = (k * jnp.exp(-cb)).astype(BF16)
        af = lax.dot_general(qf, kf, NT_DIMS, preferred_element_type=F32)
        ab = lax.dot_general(qb, kb, NT_DIMS, preferred_element_type=F32)
        attn = jnp.where(causal, af, ab).astype(BF16)
        o = jnp.dot(attn, v, preferred_element_type=F32)
        o = o + lax.dot_general(jnp.concatenate([qf, qb], axis=1), st_ref[c], NT_DIMS,
                                preferred_element_type=F32)
        o = _rms(o, gain_ref[...])
        r = r_ref[0, pl.ds(r0, L), :].astype(F32)
        o_ref[0, pl.ds(r0, L), :] = (o * (r * _sigmoid(r))).astype(BF16)
        return carry

    lax.fori_loop(0, nc, out_body, 0, unroll=unroll)


def _gla(z, gq, gk, gv, gr, uph, upl, bias, gain, unroll):
    B, S, _ = gq.shape
    nc = S // GLA_CHUNK
    seq_blk = lambda n: pl.BlockSpec((1, S, n), lambda b, h: (b, 0, h))
    head_blk = lambda r, n: pl.BlockSpec((None, r, n), lambda b, h: (h, 0, 0))
    return pl.pallas_call(
        functools.partial(_gla_kernel, seq=S, unroll=unroll),
        grid=(B, GLA_HEADS),
        in_specs=[pl.BlockSpec((1, S, 2 * GLA_GATE_RANK), lambda b, h: (b, 0, 0)),
                  seq_blk(GLA_DK), seq_blk(GLA_DK), seq_blk(GLA_DV), seq_blk(GLA_DV),
                  head_blk(2 * GLA_GATE_RANK, 2 * GLA_DK), head_blk(2 * GLA_GATE_RANK, 2 * GLA_DK),
                  head_blk(1, 2 * GLA_DK),
                  pl.BlockSpec((1, GLA_DV), lambda b, h: (0, h))],
        out_specs=seq_blk(GLA_DV),
        out_shape=jax.ShapeDtypeStruct((B, S, GLA_VAL_DIM), BF16),
        scratch_shapes=[pltpu.VMEM((S, GLA_DK), F32), pltpu.VMEM((S, GLA_DK), F32),
                        pltpu.VMEM((nc, GLA_DV, 2 * GLA_DK), BF16),
                        pltpu.VMEM((GLA_DV, GLA_DK), F32), pltpu.VMEM((GLA_DV, GLA_DK), F32)],
        compiler_params=_cparams(("parallel", "parallel"), 48),
        name="gla",
    )(z, gq, gk, gv, gr, uph, upl, bias, gain)


def _mix_out_kernel(x_ref, a_ref, g_ref, sga_ref, sgg_ref, wa_ref, wb_ref, wo_ref, gain_ref, wr_ref,
                    h_ref, xn_ref, aff_ref):
    y_att = jnp.dot(a_ref[...], wa_ref[...], preferred_element_type=F32)
    y_gla = jnp.dot(g_ref[...], wb_ref[...], preferred_element_type=F32)
    merged = sga_ref[...].astype(F32) * y_att + sgg_ref[...].astype(F32) * y_gla
    h = x_ref[...] + jnp.dot(merged.astype(BF16), wo_ref[...], preferred_element_type=F32)
    h_ref[...] = h
    xn = _rms(h, gain_ref[...])
    xn_ref[...] = xn
    hi = xn.astype(BF16)
    lo = (xn - hi.astype(F32)).astype(BF16)
    part = jnp.dot(jnp.concatenate([hi, lo], axis=1), wr_ref[...], preferred_element_type=F32)
    logits = part[:, :N_EXPERTS] + part[:, N_EXPERTS:]
    m = jnp.max(logits, axis=-1, keepdims=True)
    e = jnp.exp(logits - m)
    aff_ref[...] = e / jnp.sum(e, axis=-1, keepdims=True)


def _mix_out(x2, a, g, sga, sgg, wa, wb, wo, gain, wr, tm):
    T = x2.shape[0]
    row = lambda n: pl.BlockSpec((tm, n), lambda i: (i, 0))
    return pl.pallas_call(
        _mix_out_kernel,
        grid=(T // tm,),
        in_specs=[row(D_MODEL), row(ATT_Q_DIM), row(GLA_VAL_DIM), row(D_MODEL), row(D_MODEL),
                  _full(wa.shape), _full(wb.shape), _full(wo.shape), _full(gain.shape), _full(wr.shape)],
        out_specs=[row(D_MODEL), row(D_MODEL), row(N_EXPERTS)],
        out_shape=[jax.ShapeDtypeStruct((T, D_MODEL), F32), jax.ShapeDtypeStruct((T, D_MODEL), F32),
                   jax.ShapeDtypeStruct((T, N_EXPERTS), F32)],
        compiler_params=_cparams(("parallel",), 48),
        name="mix_out",
    )(x2, a, g, sga, sgg, wa, wb, wo, gain, wr)


def _route_kernel(aff_ref, idx_ref, cum_ref, *, cap, seq):
    E = N_EXPERTS
    bits = lax.bitcast_convert_type(aff_ref[0], jnp.int32)
    count = lambda mask: jnp.sum(mask.astype(jnp.int32), axis=1, keepdims=True)

    def thr_body(t, thr):
        cand = thr | jnp.left_shift(jnp.int32(1), 30 - t)
        return jnp.where(count(bits >= cand) >= cap, cand, thr)

    thr = lax.fori_loop(0, 31, thr_body, jnp.zeros((E, 1), jnp.int32))
    above = bits > thr
    tie = bits == thr
    need = cap - count(above)

    pos = lax.broadcasted_iota(jnp.int32, (E, seq), 1)

    def tie_body(t, last):
        cand = last | jnp.left_shift(jnp.int32(1), (seq.bit_length() - 2) - t)
        return jnp.where(count(tie & (pos < cand)) < need, cand, last)

    last = lax.fori_loop(0, seq.bit_length() - 1, tie_body, jnp.zeros((E, 1), jnp.int32))
    sel = (above | (tie & (pos <= last))).astype(BF16)

    tri = (lax.broadcasted_iota(jnp.int32, (LANES, LANES), 0)
           <= lax.broadcasted_iota(jnp.int32, (LANES, LANES), 1)).astype(BF16)
    carry = jnp.zeros((E, 1), F32)
    for t in range(seq // LANES):
        local = jnp.dot(sel[:, t * LANES:(t + 1) * LANES], tri, preferred_element_type=F32) + carry
        cum_ref[:, t * LANES:(t + 1) * LANES] = local
        carry = local[:, LANES - 1:LANES]

    slot = lax.broadcasted_iota(jnp.int32, (cap, LANES), 0).astype(F32)
    for e in range(E):
        def cnt_body(t, acc):
            c = cum_ref[e:e + 1, pl.ds(pl.multiple_of(t * LANES, LANES), LANES)]
            return acc + jnp.where(c <= slot, 1.0, 0.0)

        acc = lax.fori_loop(0, seq // LANES, cnt_body, jnp.zeros((cap, LANES), F32))
        idx_ref[0, :, e:e + 1] = jnp.sum(acc, axis=1, keepdims=True).astype(jnp.int32)


def _route(aff_t, cap):
    B, E, S = aff_t.shape
    return pl.pallas_call(
        functools.partial(_route_kernel, cap=cap, seq=S),
        grid=(B,),
        in_specs=[pl.BlockSpec((1, E, S), lambda b: (b, 0, 0))],
        out_specs=pl.BlockSpec((1, cap, E), lambda b: (b, 0, 0)),
        out_shape=jax.ShapeDtypeStruct((B, cap, E), jnp.int32),
        scratch_shapes=[pltpu.VMEM((E, S), F32)],
        compiler_params=_cparams(("parallel",), 32),
        name="route",
    )(aff_t)


def _gather_kernel(idx_ref, xn_ref, aff_ref, xg_ref, wg_ref, xs_ref, ws_ref, *, cap):
    b = pl.program_id(0)
    e = pl.program_id(1)
    base = (b * N_EXPERTS + e) * cap

    def body(i, carry):
        t = idx_ref[base + i]
        xs_ref[pl.ds(i, 1), :] = xn_ref[0, pl.ds(t, 1), :]
        ws_ref[pl.ds(i, 1), :] = aff_ref[0, pl.ds(t, 1), :]
        return carry

    lax.fori_loop(0, cap, body, 0, unroll=8)
    xg_ref[0, 0] = xs_ref[...].astype(BF16)
    lane = lax.broadcasted_iota(jnp.int32, (cap, N_EXPERTS), 1)
    wg_ref[0, 0] = jnp.sum(jnp.where(lane == e, ws_ref[...], 0.0), axis=1, keepdims=True)


def _gather(idx_flat, xn, aff, cap):
    B, S, D = xn.shape
    E = N_EXPERTS
    grid_spec = pltpu.PrefetchScalarGridSpec(
        num_scalar_prefetch=1,
        grid=(B, E),
        in_specs=[pl.BlockSpec((1, S, D), lambda b, e, idx: (b, 0, 0)),
                  pl.BlockSpec((1, S, E), lambda b, e, idx: (b, 0, 0))],
        out_specs=[pl.BlockSpec((1, 1, cap, D), lambda b, e, idx: (b, e, 0, 0)),
                   pl.BlockSpec((1, 1, cap, 1), lambda b, e, idx: (b, e, 0, 0))],
        scratch_shapes=[pltpu.VMEM((cap, D), F32), pltpu.VMEM((cap, E), F32)],
    )
    return pl.pallas_call(
        functools.partial(_gather_kernel, cap=cap),
        grid_spec=grid_spec,
        out_shape=[jax.ShapeDtypeStruct((B, E, cap, D), BF16),
                   jax.ShapeDtypeStruct((B, E, cap, 1), F32)],
        compiler_params=_cparams(("arbitrary", "arbitrary"), 48),
        name="gather",
    )(idx_flat, xn, aff)


def _ffn_kernel(xg_ref, wg_ref, w1_ref, w2_ref, w3_ref, y_ref):
    xg = xg_ref[0, 0]
    gate = jnp.dot(xg, w1_ref[0], preferred_element_type=F32)
    up = jnp.dot(xg, w2_ref[0], preferred_element_type=F32)
    hid = (gate * _sigmoid(gate) * up).astype(BF16)
    y_ref[0, 0] = jnp.dot(hid, w3_ref[0], preferred_element_type=F32) * wg_ref[0, 0]


def _ffn(xg, wg, w1, w2, w3):
    B, E, C, D = xg.shape
    F = w1.shape[-1]
    return pl.pallas_call(
        _ffn_kernel,
        grid=(E, B),
        in_specs=[pl.BlockSpec((1, 1, C, D), lambda e, b: (b, e, 0, 0)),
                  pl.BlockSpec((1, 1, C, 1), lambda e, b: (b, e, 0, 0)),
                  pl.BlockSpec((1, D, F), lambda e, b: (e, 0, 0)),
                  pl.BlockSpec((1, D, F), lambda e, b: (e, 0, 0)),
                  pl.BlockSpec((1, F, D), lambda e, b: (e, 0, 0))],
        out_specs=pl.BlockSpec((1, 1, C, D), lambda e, b: (b, e, 0, 0)),
        out_shape=jax.ShapeDtypeStruct((B, E, C, D), F32),
        compiler_params=_cparams(("parallel", "parallel"), 48),
        name="ffn",
    )(xg, wg, w1, w2, w3)


def _scatter_kernel(idx_ref, y_ref, o_ref, *, cap):
    b = pl.program_id(0)
    e = pl.program_id(1)
    base = (b * N_EXPERTS + e) * cap

    @pl.when(e == 0)
    def _():
        o_ref[...] = jnp.zeros_like(o_ref)

    def body(i, carry):
        t = idx_ref[base + i]
        o_ref[0, pl.ds(t, 1), :] = o_ref[0, pl.ds(t, 1), :] + y_ref[0, 0, pl.ds(i, 1), :]
        return carry

    lax.fori_loop(0, cap, body, 0, unroll=8)


def _scatter(idx_flat, y, seq):
    B, E, C, D = y.shape
    grid_spec = pltpu.PrefetchScalarGridSpec(
        num_scalar_prefetch=1,
        grid=(B, E),
        in_specs=[pl.BlockSpec((1, 1, C, D), lambda b, e, idx: (b, e, 0, 0))],
        out_specs=pl.BlockSpec((1, seq, D), lambda b, e, idx: (b, 0, 0)),
    )
    return pl.pallas_call(
        functools.partial(_scatter_kernel, cap=C),
        grid_spec=grid_spec,
        out_shape=jax.ShapeDtypeStruct((B, seq, D), F32),
        compiler_params=_cparams(("arbitrary", "arbitrary"), 48),
        name="scatter",
    )(idx_flat, y)


def _ple_out_kernel(h_ref, moe_ref, p_ref, gple_ref, wpg_ref, wple_ref, gfin_ref, o_ref):
    h = h_ref[...] + moe_ref[...]
    n = _rms(h, gple_ref[...]).astype(BF16)
    gate = _sigmoid(jnp.dot(n, wpg_ref[...], preferred_element_type=F32))
    emb = jnp.dot(p_ref[...].astype(BF16), wple_ref[...], preferred_element_type=F32)
    o_ref[...] = _rms(h + gate * emb, gfin_ref[...])


def _ple_out(h, moe, p2, gple, wpg, wple, gfin, tm):
    T = h.shape[0]
    row = lambda n: pl.BlockSpec((tm, n), lambda i: (i, 0))
    return pl.pallas_call(
        _ple_out_kernel,
        grid=(T // tm,),
        in_specs=[row(D_MODEL), row(D_MODEL), row(PLE_DIM), _full(gple.shape), _full(wpg.shape),
                  _full(wple.shape), _full(gfin.shape)],
        out_specs=row(D_MODEL),
        out_shape=jax.ShapeDtypeStruct((T, D_MODEL), F32),
        compiler_params=_cparams(("parallel",), 48),
        name="ple_out",
    )(h, moe, p2, gple, wpg, wple, gfin)


def kernel(x, p, positions, norm_mix, w_in, gla_gate_up_fwd, gla_gate_bias_fwd, gla_gate_up_bwd, gla_gate_bias_bwd, attn_sink, gla_norm, w_branch_attn, w_branch_gla, w_out, norm_ffn, w_router, w_exp_gate, w_exp_up, w_exp_down, norm_ple, w_ple_gate, w_ple, norm_final):
    B, S, D = x.shape
    T = B * S
    depth = w_in.shape[0]
    cap = CAPACITY_FACTOR * S // N_EXPERTS
    R = GLA_GATE_RANK

    pos2 = positions.reshape(T, 1)
    inv_freq = ROPE_THETA ** (-jnp.arange(0, ROPE_DIM, 2, dtype=F32) / ROPE_DIM)
    invf = jnp.tile(inv_freq, LANES // (ROPE_DIM // 2)).reshape(1, LANES)

    h = x.reshape(T, D)
    for l in range(depth):
        o = 0
        cols = {}
        for name, n in (("qkv", ATT_Q_DIM + 2 * ATT_KV_DIM), ("gqk", 2 * GLA_KEY_DIM), ("gv", GLA_VAL_DIM),
                        ("gr", GLA_VAL_DIM), ("z", 2 * R), ("gate", 2 * D_MODEL)):
            cols[name] = w_in[l][:, o:o + n].astype(BF16)
            o += n
        per_head = lambda w: w.reshape(-1, GLA_HEADS, GLA_DK).swapaxes(0, 1)
        upf, upb = per_head(gla_gate_up_fwd[l]), per_head(gla_gate_up_bwd[l])
        up = jnp.concatenate([jnp.concatenate([upf, jnp.zeros_like(upf)], axis=2),
                              jnp.concatenate([jnp.zeros_like(upb), upb], axis=2)], axis=1)
        up_hi = up.astype(BF16)
        up_lo = (up - up_hi.astype(F32)).astype(BF16)
        gbias = jnp.concatenate([per_head(gla_gate_bias_fwd[l]), per_head(gla_gate_bias_bwd[l])], axis=2)
        wr = w_router[l]
        wr_hi = wr.astype(BF16)
        wr_lo = (wr - wr_hi.astype(F32)).astype(BF16)
        wr2 = jnp.concatenate([jnp.concatenate([wr_hi, wr_lo], axis=1),
                               jnp.concatenate([wr_hi, jnp.zeros_like(wr_lo)], axis=1)], axis=0)

        q, k, v, gq, gk, gv, gr, z, sga, sgg = _in_proj(
            h, pos2, invf, norm_mix[l].reshape(1, D), cols["qkv"], cols["gqk"], cols["gv"], cols["gr"],
            cols["z"], cols["gate"], tm=256)

        att = _swa(attn_sink[l], q.reshape(B, S, -1), k.reshape(B, S, -1), v.reshape(B, S, -1), tq=512)
        gla = _gla(z.reshape(B, S, -1), gq.reshape(B, S, -1), gk.reshape(B, S, -1), gv.reshape(B, S, -1),
                   gr.reshape(B, S, -1), up_hi, up_lo, gbias, gla_norm[l].reshape(1, -1), unroll=4)

        h1, xn, aff = _mix_out(h, att.reshape(T, -1), gla.reshape(T, -1), sga, sgg,
                               w_branch_attn[l].astype(BF16), w_branch_gla[l].astype(BF16),
                               w_out[l].astype(BF16), norm_ffn[l].reshape(1, D), wr2, tm=256)

        aff3 = aff.reshape(B, S, N_EXPERTS)
        idx = _route(jnp.swapaxes(aff3, 1, 2), cap)
        idx_flat = jnp.swapaxes(idx, 1, 2).reshape(-1)
        xg, wg = _gather(idx_flat, xn.reshape(B, S, D), aff3, cap)
        y = _ffn(xg, wg, w_exp_gate[l].astype(BF16), w_exp_up[l].astype(BF16), w_exp_down[l].astype(BF16))
        moe = _scatter(idx_flat, y, S)

        last = l == depth - 1
        gfin = norm_final.reshape(1, D)
        assert last, "the final norm is fused into the last layer's PLE kernel"
        h = _ple_out(h1, moe.reshape(T, D), p[l].reshape(T, PLE_DIM), norm_ple[l].reshape(1, D),
                     w_ple_gate[l].astype(BF16), w_ple[l].astype(BF16), gfin, tm=256)
    return h.reshape(B, S, D)
```

```python
import functools
import math

import jax
import jax.numpy as jnp
from jax import lax
from jax.experimental import pallas as pl
from jax.experimental.pallas import tpu as pltpu

D_MODEL = 1024
ATT_HEADS = 8
ATT_KV_HEADS = 2
ATT_HEAD_DIM = 64
ATT_GROUP = ATT_HEADS // ATT_KV_HEADS
ATT_Q_DIM = ATT_HEADS * ATT_HEAD_DIM
ATT_KV_DIM = ATT_KV_HEADS * ATT_HEAD_DIM
WINDOW = 128
ROPE_DIM = ATT_HEAD_DIM // 4
ROPE_THETA = 500000.0
GLA_HEADS = 4
GLA_KEY_DIM = D_MODEL // 2
GLA_VAL_DIM = D_MODEL
GLA_DK = GLA_KEY_DIM // GLA_HEADS
GLA_DV = GLA_VAL_DIM // GLA_HEADS
GLA_GATE_RANK = 16
GLA_GATE_NORM = 16.0
GLA_CHUNK = 64
N_EXPERTS = 16
EXPERT_FF = D_MODEL
CAPACITY_FACTOR = 2
PLE_DIM = 256
EPS = 1e-6

LANES = 128
MIB = 1024 * 1024
BF16 = jnp.bfloat16
F32 = jnp.float32
LOG2E = math.log2(math.e)

NT_DIMS = (((1,), (1,)), ((), ()))
TN_DIMS = (((0,), (0,)), ((), ()))


def _cparams(sem, vmem_mib):
    return pltpu.CompilerParams(dimension_semantics=sem, vmem_limit_bytes=vmem_mib * MIB)


def _full(shape):
    n = len(shape)
    return pl.BlockSpec(shape, lambda *_: (0,) * n)


def _rms(x, gain):
    ms = jnp.mean(x * x, axis=-1, keepdims=True)
    return x * lax.rsqrt(ms + EPS) * gain


def _sigmoid(x):
    return 1.0 / (1.0 + jnp.exp(-x))


def _rope(t, cos_t, sin_t, first_half):
    fwd = pltpu.roll(t, LANES - ROPE_DIM // 2, axis=1)
    bwd = pltpu.roll(t, ROPE_DIM // 2, axis=1)
    return t * cos_t + jnp.where(first_half, fwd, bwd) * sin_t


def _in_proj_kernel(x_ref, posc_ref, posr_ref, invfl_ref, invfc_ref, gain_ref, wqt_ref, wk_ref, wvt_ref,
                    wgqk_ref, wgv_ref, wgr_ref, wz_ref, wgate_ref,
                    qt_ref, k0_ref, k1_ref, vt_ref, gq_ref, gk_ref, gv_ref, gr_ref, z_ref, sga_ref, sgg_ref):
    a = _rms(x_ref[...], gain_ref[...]).astype(BF16)
    half = ROPE_DIM // 2

    qt = lax.dot_general(wqt_ref[...], a, NT_DIMS, preferred_element_type=F32)
    ang_t = invfc_ref[...] * posr_ref[...].astype(F32)
    cos_r, sin_r = jnp.cos(ang_t), jnp.sin(ang_t)
    qscale = ATT_HEAD_DIM ** -0.5 * LOG2E
    for h in range(ATT_HEADS):
        r0 = h * ATT_HEAD_DIM
        t1, t2 = qt[r0:r0 + half], qt[r0 + half:r0 + ROPE_DIM]
        qt_ref[r0:r0 + half, :] = ((t1 * cos_r - t2 * sin_r) * qscale).astype(BF16)
        qt_ref[r0 + half:r0 + ROPE_DIM, :] = ((t2 * cos_r + t1 * sin_r) * qscale).astype(BF16)
        qt_ref[r0 + ROPE_DIM:r0 + ATT_HEAD_DIM, :] = (qt[r0 + ROPE_DIM:r0 + ATT_HEAD_DIM] * qscale).astype(BF16)

    lane = lax.broadcasted_iota(jnp.int32, (1, LANES), 1)
    d = lane % ATT_HEAD_DIM
    first_half = d < half
    in_rope = d < ROPE_DIM
    ang = posc_ref[...].astype(F32) * invfl_ref[...]
    cos_t = jnp.where(in_rope, jnp.cos(ang), 1.0)
    sin_t = jnp.where(first_half, -jnp.sin(ang), jnp.where(in_rope, jnp.sin(ang), 0.0))
    k = _rope(jnp.dot(a, wk_ref[...], preferred_element_type=F32), cos_t, sin_t, first_half).astype(BF16)
    k0_ref[...] = k[:, :ATT_HEAD_DIM]
    k1_ref[...] = k[:, ATT_HEAD_DIM:]
    vt_ref[...] = lax.dot_general(wvt_ref[...], a, NT_DIMS, preferred_element_type=F32).astype(BF16)

    gqk = jnp.dot(a, wgqk_ref[...], preferred_element_type=F32)
    gq_ref[...] = (gqk[:, :GLA_KEY_DIM] * (GLA_DK ** -0.5)).astype(BF16)
    gk_ref[...] = gqk[:, GLA_KEY_DIM:].astype(BF16)
    gv_ref[...] = jnp.dot(a, wgv_ref[...], preferred_element_type=F32).astype(BF16)
    gr_ref[...] = jnp.dot(a, wgr_ref[...], preferred_element_type=F32).astype(BF16)
    z_ref[...] = jnp.dot(a, wz_ref[...], preferred_element_type=F32)
    gates = jnp.dot(a, wgate_ref[...], preferred_element_type=F32)
    sga_ref[...] = _sigmoid(gates[:, :D_MODEL]).astype(BF16)
    sgg_ref[...] = _sigmoid(gates[:, D_MODEL:]).astype(BF16)


def _in_proj(x2, posc, posr, invfl, invfc, gain, wqt, wk, wvt, wgqk, wgv, wgr, wz, wgate, tm):
    T = x2.shape[0]
    row = lambda n: pl.BlockSpec((tm, n), lambda i: (i, 0))
    col = lambda n: pl.BlockSpec((n, tm), lambda i: (0, i))
    row_widths = (ATT_HEAD_DIM, ATT_HEAD_DIM, None, GLA_KEY_DIM, GLA_KEY_DIM, GLA_VAL_DIM,
                  GLA_VAL_DIM, 2 * GLA_GATE_RANK, D_MODEL, D_MODEL)
    row_dtypes = (BF16,) * 7 + (F32, BF16, BF16)
    out_specs = [col(ATT_Q_DIM)]
    out_shape = [jax.ShapeDtypeStruct((ATT_Q_DIM, T), BF16)]
    for n, dt in zip(row_widths, row_dtypes):
        if n is None:
            out_specs.append(col(ATT_KV_DIM))
            out_shape.append(jax.ShapeDtypeStruct((ATT_KV_DIM, T), BF16))
        else:
            out_specs.append(row(n))
            out_shape.append(jax.ShapeDtypeStruct((T, n), dt))
    consts = (invfl, invfc, gain, wqt, wk, wvt, wgqk, wgv, wgr, wz, wgate)
    return pl.pallas_call(
        _in_proj_kernel,
        grid=(T // tm,),
        in_specs=[row(D_MODEL), row(1), col(1)] + [_full(c.shape) for c in consts],
        out_specs=out_specs,
        out_shape=out_shape,
        compiler_params=_cparams(("parallel",), 56),
        name="in_proj",
    )(x2, posc, posr, *consts)


def _swa_kernel(sink_ref, qt_ref, k0_ref, k1_ref, vt_ref, o_ref, *, tq, seq):
    blk = WINDOW
    span = 3 * blk
    hd = ATT_HEAD_DIM
    n = pl.program_id(1)
    ones = jnp.ones((16, span), BF16)
    for sb in range(tq // blk):
        q0 = n * tq + sb * blk
        start = pl.multiple_of(jnp.clip(q0 - blk, 0, seq - span), blk)
        kj = start + lax.broadcasted_iota(jnp.int32, (span, blk), 0)
        qi = q0 + lax.broadcasted_iota(jnp.int32, (span, blk), 1)
        valid = jnp.abs(qi - kj) <= WINDOW
        outs = []
        for g, k_ref in enumerate((k0_ref, k1_ref)):
            kw = k_ref[0, pl.ds(start, span), :]
            vaug = jnp.concatenate([vt_ref[g * hd:(g + 1) * hd, pl.ds(start, span)], ones], axis=0)
            heads = range(g * ATT_GROUP, (g + 1) * ATT_GROUP)
            qs = jnp.concatenate([qt_ref[h * hd:(h + 1) * hd, sb * blk:(sb + 1) * blk] for h in heads], axis=1)
            s_all = jnp.dot(kw, qs, preferred_element_type=F32)
            for i, h in enumerate(heads):
                s = jnp.where(valid, s_all[:, i * blk:(i + 1) * blk], -jnp.inf)
                sink = sink_ref[h] * LOG2E
                m = jnp.maximum(jnp.max(s, axis=0, keepdims=True), sink)
                e = jnp.exp2(s - m).astype(BF16)
                r = jnp.dot(vaug, e, preferred_element_type=F32)
                den = r[hd:hd + 1] + jnp.exp2(sink - m)
                outs.append(r[:hd] / den)
        for pr in range(ATT_HEADS // 2):
            pair = jnp.concatenate([outs[2 * pr], outs[2 * pr + 1]], axis=0)
            o_ref[0, sb * blk:(sb + 1) * blk, pr * 2 * hd:(pr + 1) * 2 * hd] = pair.T.astype(BF16)


def _swa(sink, qt, k0, k1, vt, batch, tq):
    S = k0.shape[1]
    nq = S // tq
    kspec = pl.BlockSpec((1, S, ATT_HEAD_DIM), lambda b, n: (b, 0, 0))
    return pl.pallas_call(
        functools.partial(_swa_kernel, tq=tq, seq=S),
        grid=(batch, nq),
        in_specs=[pl.BlockSpec(memory_space=pltpu.SMEM),
                  pl.BlockSpec((ATT_Q_DIM, tq), lambda b, n: (0, b * nq + n)),
                  kspec, kspec,
                  pl.BlockSpec((ATT_KV_DIM, S), lambda b, n: (0, b))],
        out_specs=pl.BlockSpec((1, tq, ATT_Q_DIM), lambda b, n: (b, n, 0)),
        out_shape=jax.ShapeDtypeStruct((batch, S, ATT_Q_DIM), BF16),
        compiler_params=_cparams(("parallel", "parallel"), 32),
        name="swa",
    )(sink, qt, k0, k1, vt)


def _log_sigmoid(u):
    return jnp.minimum(u, 0.0) - jnp.log(1.0 + jnp.exp(-jnp.abs(u)))


def _split2(x):
    hi = x.astype(BF16)
    return hi, (x - hi.astype(F32)).astype(BF16)


def _gla_kernel(z_ref, q_ref, k_ref, v_ref, r_ref, uph_ref, upl_ref, bias_ref, gain_ref,
                o_ref, cf_ref, cb_ref, st_ref, sf_ref, sb_ref, *, seq, unroll):
    L = GLA_CHUNK
    nc = seq // L
    grp = 4 * L

    row = lax.broadcasted_iota(jnp.int32, (grp, grp), 0)
    col = lax.broadcasted_iota(jnp.int32, (grp, grp), 1)
    same = (row // L) == (col // L)
    tri_lo = jnp.where(same & (col <= row), 1.0, 0.0).astype(BF16)
    tri_up = jnp.where(same & (col >= row), 1.0, 0.0).astype(BF16)
    mm = functools.partial(jnp.dot, preferred_element_type=F32)

    def cum_body(i, carry):
        r0 = pl.multiple_of(i * grp, grp)
        zh, zl = _split2(z_ref[0, pl.ds(r0, grp), :])
        u = mm(zh, uph_ref[...]) + mm(zl, uph_ref[...]) + mm(zh, upl_ref[...]) + bias_ref[...]
        la = _log_sigmoid(u) * (1.0 / GLA_GATE_NORM)
        fh, fl = _split2(la[:, :GLA_DK])
        bh, bl = _split2(la[:, GLA_DK:])
        cf_ref[pl.ds(r0, grp), :] = mm(tri_lo, fh) + mm(tri_lo, fl)
        cb_ref[pl.ds(r0, grp), :] = mm(tri_up, bh) + mm(tri_up, bl)
        return carry

    lax.fori_loop(0, seq // grp, cum_body, 0, unroll=2)

    sf_ref[...] = jnp.zeros_like(sf_ref)
    sb_ref[...] = jnp.zeros_like(sb_ref)

    def state_body(i, carry):
        rf = pl.multiple_of(i * L, L)
        cf = cf_ref[pl.ds(rf, L), :]
        gf = cf_ref[pl.ds(rf + L - 1, 1), :]
        kf = k_ref[0, pl.ds(rf, L), :].astype(F32)
        kend = (kf * jnp.exp(gf - cf)).astype(BF16)
        kv = lax.dot_general(v_ref[0, pl.ds(rf, L), :], kend, TN_DIMS, preferred_element_type=F32)
        st_ref[i, :, 0:GLA_DK] = sf_ref[...].astype(BF16)
        sf_ref[...] = sf_ref[...] * jnp.exp(gf) + kv

        j = nc - 1 - i
        rb = pl.multiple_of(j * L, L)
        cb = cb_ref[pl.ds(rb, L), :]
        gb = cb_ref[pl.ds(rb, 1), :]
        kb = k_ref[0, pl.ds(rb, L), :].astype(F32)
        kend_b = (kb * jnp.exp(gb - cb)).astype(BF16)
        kv_b = lax.dot_general(v_ref[0, pl.ds(rb, L), :], kend_b, TN_DIMS, preferred_element_type=F32)
        st_ref[j, :, GLA_DK:2 * GLA_DK] = sb_ref[...].astype(BF16)
        sb_ref[...] = sb_ref[...] * jnp.exp(gb) + kv_b
        return carry

    lax.fori_loop(0, nc, state_body, 0, unroll=unroll)

    lrow = lax.broadcasted_iota(jnp.int32, (L, L), 0)
    lcol = lax.broadcasted_iota(jnp.int32, (L, L), 1)
    causal = lrow >= lcol

    def out_body(c, carry):
        r0 = pl.multiple_of(c * L, L)
        q = q_ref[0, pl.ds(r0, L), :].astype(F32)
        k = k_ref[0, pl.ds(r0, L), :].astype(F32)
        v = v_ref[0, pl.ds(r0, L), :]
        cf = cf_ref[pl.ds(r0, L), :]
        cb = cb_ref[pl.ds(r0, L), :]
        qf = (q * jnp.exp(cf)).astype(BF16)
        kf = (k * jnp.exp(-cf)).astype(BF16)
        qb = (q * jnp.exp(cb)).astype(BF16)
        kb = (k * jnp.exp(-cb)).astype(BF16)
        af = lax.dot_general(qf, kf, NT_DIMS, preferred_element_type=F32)
        ab = lax.dot_general(qb, kb, NT_DIMS, preferred_element_type=F32)
        attn = jnp.where(causal, af, ab).astype(BF16)
        o = jnp.dot(attn, v, preferred_element_type=F32)
        o = o + lax.dot_general(jnp.concatenate([qf, qb], axis=1), st_ref[c], NT_DIMS,
                                preferred_element_type=F32)
        o = _rms(o, gain_ref[...])
        r = r_ref[0, pl.ds(r0, L), :].astype(F32)
        o_ref[0, pl.ds(r0, L), :] = (o * (r * _sigmoid(r))).astype(BF16)
        return carry

    lax.fori_loop(0, nc, out_body, 0, unroll=unroll)


def _gla(z, gq, gk, gv, gr, uph, upl, bias, gain, unroll):
    B, S, _ = gq.shape
    nc = S // GLA_CHUNK
    seq_blk = lambda n: pl.BlockSpec((1, S, n), lambda b, h: (b, 0, h))
    head_blk = lambda r, n: pl.BlockSpec((None, r, n), lambda b, h: (h, 0, 0))
    return pl.pallas_call(
        functools.partial(_gla_kernel, seq=S, unroll=unroll),
        grid=(B, GLA_HEADS),
        in_specs=[pl.BlockSpec((1, S, 2 * GLA_GATE_RANK), lambda b, h: (b, 0, 0)),
                  seq_blk(GLA_DK), seq_blk(GLA_DK), seq_blk(GLA_DV), seq_blk(GLA_DV),
                  head_blk(2 * GLA_GATE_RANK, 2 * GLA_DK), head_blk(2 * GLA_GATE_RANK, 2 * GLA_DK),
                  head_blk(1, 2 * GLA_DK),
                  pl.BlockSpec((1, GLA_DV), lambda b, h: (0, h))],
        out_specs=seq_blk(GLA_DV),
        out_shape=jax.ShapeDtypeStruct((B, S, GLA_VAL_DIM), BF16),
        scratch_shapes=[pltpu.VMEM((S, GLA_DK), F32), pltpu.VMEM((S, GLA_DK), F32),
                        pltpu.VMEM((nc, GLA_DV, 2 * GLA_DK), BF16),
                        pltpu.VMEM((GLA_DV, GLA_DK), F32), pltpu.VMEM((GLA_DV, GLA_DK), F32)],
        compiler_params=_cparams(("parallel", "parallel"), 48),
        name="gla",
    )(z, gq, gk, gv, gr, uph, upl, bias, gain)


def _mix_out_kernel(x_ref, a_ref, g_ref, sga_ref, sgg_ref, wa_ref, wb_ref, wo_ref, gain_ref, wr_ref,
                    h_ref, xn_ref, aff_ref):
    y_att = jnp.dot(a_ref[...], wa_ref[...], preferred_element_type=F32)
    y_gla = jnp.dot(g_ref[...], wb_ref[...], preferred_element_type=F32)
    merged = sga_ref[...].astype(F32) * y_att + sgg_ref[...].astype(F32) * y_gla
    h = x_ref[...] + jnp.dot(merged.astype(BF16), wo_ref[...], preferred_element_type=F32)
    h_ref[...] = h
    xn = _rms(h, gain_ref[...])
    xn_ref[...] = xn
    hi = xn.astype(BF16)
    lo = (xn - hi.astype(F32)).astype(BF16)
    part = jnp.dot(jnp.concatenate([hi, lo], axis=1), wr_ref[...], preferred_element_type=F32)
    logits = part[:, :N_EXPERTS] + part[:, N_EXPERTS:]
    m = jnp.max(logits, axis=-1, keepdims=True)
    e = jnp.exp(logits - m)
    aff_ref[...] = e / jnp.sum(e, axis=-1, keepdims=True)


def _mix_out(x2, a, g, sga, sgg, wa, wb, wo, gain, wr, tm):
    T = x2.shape[0]
    row = lambda n: pl.BlockSpec((tm, n), lambda i: (i, 0))
    return pl.pallas_call(
        _mix_out_kernel,
        grid=(T // tm,),
        in_specs=[row(D_MODEL), row(ATT_Q_DIM), row(GLA_VAL_DIM), row(D_MODEL), row(D_MODEL),
                  _full(wa.shape), _full(wb.shape), _full(wo.shape), _full(gain.shape), _full(wr.shape)],
        out_specs=[row(D_MODEL), row(D_MODEL), row(N_EXPERTS)],
        out_shape=[jax.ShapeDtypeStruct((T, D_MODEL), F32), jax.ShapeDtypeStruct((T, D_MODEL), F32),
                   jax.ShapeDtypeStruct((T, N_EXPERTS), F32)],
        compiler_params=_cparams(("parallel",), 48),
        name="mix_out",
    )(x2, a, g, sga, sgg, wa, wb, wo, gain, wr)


def _route_kernel(aff_ref, idx_ref, cum_ref, *, cap, seq):
    E = N_EXPERTS
    bits = lax.bitcast_convert_type(aff_ref[0], jnp.int32)
    count = lambda mask: jnp.sum(mask.astype(jnp.int32), axis=1, keepdims=True)

    def thr_body(t, thr):
        cand = thr | jnp.left_shift(jnp.int32(1), 30 - t)
        return jnp.where(count(bits >= cand) >= cap, cand, thr)

    thr = lax.fori_loop(0, 31, thr_body, jnp.zeros((E, 1), jnp.int32))
    above = bits > thr
    tie = bits == thr
    need = cap - count(above)

    pos = lax.broadcasted_iota(jnp.int32, (E, seq), 1)

    def tie_body(t, last):
        cand = last | jnp.left_shift(jnp.int32(1), (seq.bit_length() - 2) - t)
        return jnp.where(count(tie & (pos < cand)) < need, cand, last)

    last = lax.fori_loop(0, seq.bit_length() - 1, tie_body, jnp.zeros((E, 1), jnp.int32))
    sel = (above | (tie & (pos <= last))).astype(BF16)

    tri = (lax.broadcasted_iota(jnp.int32, (LANES, LANES), 0)
           <= lax.broadcasted_iota(jnp.int32, (LANES, LANES), 1)).astype(BF16)
    carry = jnp.zeros((E, 1), F32)
    for t in range(seq // LANES):
        local = jnp.dot(sel[:, t * LANES:(t + 1) * LANES], tri, preferred_element_type=F32) + carry
        cum_ref[:, t * LANES:(t + 1) * LANES] = local
        carry = local[:, LANES - 1:LANES]

    slot = lax.broadcasted_iota(jnp.int32, (cap, LANES), 0).astype(F32)
    for e in range(E):
        def cnt_body(t, acc):
            c = cum_ref[e:e + 1, pl.ds(pl.multiple_of(t * LANES, LANES), LANES)]
            return acc + jnp.where(c <= slot, 1.0, 0.0)

        acc = lax.fori_loop(0, seq // LANES, cnt_body, jnp.zeros((cap, LANES), F32))
        idx_ref[0, :, e:e + 1] = jnp.sum(acc, axis=1, keepdims=True).astype(jnp.int32)


def _route(aff_t, cap):
    B, E, S = aff_t.shape
    return pl.pallas_call(
        functools.partial(_route_kernel, cap=cap, seq=S),
        grid=(B,),
        in_specs=[pl.BlockSpec((1, E, S), lambda b: (b, 0, 0))],
        out_specs=pl.BlockSpec((1, cap, E), lambda b: (b, 0, 0)),
        out_shape=jax.ShapeDtypeStruct((B, cap, E), jnp.int32),
        scratch_shapes=[pltpu.VMEM((E, S), F32)],
        compiler_params=_cparams(("parallel",), 32),
        name="route",
    )(aff_t)


def _gather_kernel(idx_ref, xn_ref, aff_ref, xg_ref, wg_ref, xs_ref, ws_ref, *, cap):
    b = pl.program_id(0)
    e = pl.program_id(1)
    base = (b * N_EXPERTS + e) * cap

    def body(i, carry):
        t = idx_ref[base + i]
        xs_ref[pl.ds(i, 1), :] = xn_ref[0, pl.ds(t, 1), :]
        ws_ref[pl.ds(i, 1), :] = aff_ref[0, pl.ds(t, 1), :]
        return carry

    lax.fori_loop(0, cap, body, 0, unroll=8)
    xg_ref[0, 0] = xs_ref[...].astype(BF16)
    lane = lax.broadcasted_iota(jnp.int32, (cap, N_EXPERTS), 1)
    wg_ref[0, 0] = jnp.sum(jnp.where(lane == e, ws_ref[...], 0.0), axis=1, keepdims=True)


def _gather(idx_flat, xn, aff, cap):
    B, S, D = xn.shape
    E = N_EXPERTS
    grid_spec = pltpu.PrefetchScalarGridSpec(
        num_scalar_prefetch=1,
        grid=(B, E),
        in_specs=[pl.BlockSpec((1, S, D), lambda b, e, idx: (b, 0, 0)),
                  pl.BlockSpec((1, S, E), lambda b, e, idx: (b, 0, 0))],
        out_specs=[pl.BlockSpec((1, 1, cap, D), lambda b, e, idx: (b, e, 0, 0)),
                   pl.BlockSpec((1, 1, cap, 1), lambda b, e, idx: (b, e, 0, 0))],
        scratch_shapes=[pltpu.VMEM((cap, D), F32), pltpu.VMEM((cap, E), F32)],
    )
    return pl.pallas_call(
        functools.partial(_gather_kernel, cap=cap),
        grid_spec=grid_spec,
        out_shape=[jax.ShapeDtypeStruct((B, E, cap, D), BF16),
                   jax.ShapeDtypeStruct((B, E, cap, 1), F32)],
        compiler_params=_cparams(("arbitrary", "arbitrary"), 48),
        name="gather",
    )(idx_flat, xn, aff)


def _ffn_kernel(xg_ref, wg_ref, w1_ref, w2_ref, w3_ref, y_ref):
    xg = xg_ref[0, 0]
    gate = jnp.dot(xg, w1_ref[0], preferred_element_type=F32)
    up = jnp.dot(xg, w2_ref[0], preferred_element_type=F32)
    hid = (gate * _sigmoid(gate) * up).astype(BF16)
    y_ref[0, 0] = jnp.dot(hid, w3_ref[0], preferred_element_type=F32) * wg_ref[0, 0]


def _ffn(xg, wg, w1, w2, w3):
    B, E, C, D = xg.shape
    F = w1.shape[-1]
    return pl.pallas_call(
        _ffn_kernel,
        grid=(E, B),
        in_specs=[pl.BlockSpec((1, 1, C, D), lambda e, b: (b, e, 0, 0)),
                  pl.BlockSpec((1, 1, C, 1), lambda e, b: (b, e, 0, 0)),
                  pl.BlockSpec((1, D, F), lambda e, b: (e, 0, 0)),
                  pl.BlockSpec((1, D, F), lambda e, b: (e, 0, 0)),
                  pl.BlockSpec((1, F, D), lambda e, b: (e, 0, 0))],
        out_specs=pl.BlockSpec((1, 1, C, D), lambda e, b: (b, e, 0, 0)),
        out_shape=jax.ShapeDtypeStruct((B, E, C, D), F32),
        compiler_params=_cparams(("parallel", "parallel"), 48),
        name="ffn",
    )(xg, wg, w1, w2, w3)


def _scatter_kernel(idx_ref, y_ref, o_ref, *, cap):
    b = pl.program_id(0)
    e = pl.program_id(1)
    base = (b * N_EXPERTS + e) * cap

    @pl.when(e == 0)
    def _():
        o_ref[...] = jnp.zeros_like(o_ref)

    def body(i, carry):
        t = idx_ref[base + i]
        o_ref[0, pl.ds(t, 1), :] = o_ref[0, pl.ds(t, 1), :] + y_ref[0, 0, pl.ds(i, 1), :]
        return carry

    lax.fori_loop(0, cap, body, 0, unroll=8)


def _scatter(idx_flat, y, seq):
    B, E, C, D = y.shape
    grid_spec = pltpu.PrefetchScalarGridSpec(
        num_scalar_prefetch=1,
        grid=(B, E),
        in_specs=[pl.BlockSpec((1, 1, C, D), lambda b, e, idx: (b, e, 0, 0))],
        out_specs=pl.BlockSpec((1, seq, D), lambda b, e, idx: (b, 0, 0)),
    )
    return pl.pallas_call(
        functools.partial(_scatter_kernel, cap=C),
        grid_spec=grid_spec,
        out_shape=jax.ShapeDtypeStruct((B, seq, D), F32),
        compiler_params=_cparams(("arbitrary", "arbitrary"), 48),
        name="scatter",
    )(idx_flat, y)


def _ple_out_kernel(h_ref, moe_ref, p_ref, gple_ref, wpg_ref, wple_ref, gfin_ref, o_ref):
    h = h_ref[...] + moe_ref[...]
    n = _rms(h, gple_ref[...]).astype(BF16)
    gate = _sigmoid(jnp.dot(n, wpg_ref[...], preferred_element_type=F32))
    emb = jnp.dot(p_ref[...].astype(BF16), wple_ref[...], preferred_element_type=F32)
    o_ref[...] = _rms(h + gate * emb, gfin_ref[...])


def _ple_out(h, moe, p2, gple, wpg, wple, gfin, tm):
    T = h.shape[0]
    row = lambda n: pl.BlockSpec((tm, n), lambda i: (i, 0))
    return pl.pallas_call(
        _ple_out_kernel,
        grid=(T // tm,),
        in_specs=[row(D_MODEL), row(D_MODEL), row(PLE_DIM), _full(gple.shape), _full(wpg.shape),
                  _full(wple.shape), _full(gfin.shape)],
        out_specs=row(D_MODEL),
        out_shape=jax.ShapeDtypeStruct((T, D_MODEL), F32),
        compiler_params=_cparams(("parallel",), 48),
        name="ple_out",
    )(h, moe, p2, gple, wpg, wple, gfin)


def kernel(x, p, positions, norm_mix, w_in, gla_gate_up_fwd, gla_gate_bias_fwd, gla_gate_up_bwd, gla_gate_bias_bwd, attn_sink, gla_norm, w_branch_attn, w_branch_gla, w_out, norm_ffn, w_router, w_exp_gate, w_exp_up, w_exp_down, norm_ple, w_ple_gate, w_ple, norm_final):
    B, S, D = x.shape
    T = B * S
    depth = w_in.shape[0]
    cap = CAPACITY_FACTOR * S // N_EXPERTS
    R = GLA_GATE_RANK

    posc = positions.reshape(T, 1)
    posr = positions.reshape(1, T)
    inv_freq = ROPE_THETA ** (-jnp.arange(0, ROPE_DIM, 2, dtype=F32) / ROPE_DIM)
    invfl = jnp.tile(inv_freq, LANES // (ROPE_DIM // 2)).reshape(1, LANES)
    invfc = inv_freq.reshape(ROPE_DIM // 2, 1)

    h = x.reshape(T, D)
    for l in range(depth):
        o = 0
        cols = {}
        for name, n in (("q", ATT_Q_DIM), ("k", ATT_KV_DIM), ("v", ATT_KV_DIM), ("gqk", 2 * GLA_KEY_DIM),
                        ("gv", GLA_VAL_DIM), ("gr", GLA_VAL_DIM), ("z", 2 * R), ("gate", 2 * D_MODEL)):
            cols[name] = w_in[l][:, o:o + n].astype(BF16)
            o += n
        per_head = lambda w: w.reshape(-1, GLA_HEADS, GLA_DK).swapaxes(0, 1)
        upf, upb = per_head(gla_gate_up_fwd[l]), per_head(gla_gate_up_bwd[l])
        up = jnp.concatenate([jnp.concatenate([upf, jnp.zeros_like(upf)], axis=2),
                              jnp.concatenate([jnp.zeros_like(upb), upb], axis=2)], axis=1)
        up_hi = up.astype(BF16)
        up_lo = (up - up_hi.astype(F32)).astype(BF16)
        gbias = jnp.concatenate([per_head(gla_gate_bias_fwd[l]), per_head(gla_gate_bias_bwd[l])], axis=2)
        wr = w_router[l]
        wr_hi = wr.astype(BF16)
        wr_lo = (wr - wr_hi.astype(F32)).astype(BF16)
        wr2 = jnp.concatenate([jnp.concatenate([wr_hi, wr_lo], axis=1),
                               jnp.concatenate([wr_hi, jnp.zeros_like(wr_lo)], axis=1)], axis=0)

        qt, k0, k1, vt, gq, gk, gv, gr, z, sga, sgg = _in_proj(
            h, posc, posr, invfl, invfc, norm_mix[l].reshape(1, D), cols["q"].T, cols["k"], cols["v"].T,
            cols["gqk"], cols["gv"], cols["gr"], cols["z"], cols["gate"], tm=256)

        att = _swa(attn_sink[l], qt, k0.reshape(B, S, -1), k1.reshape(B, S, -1), vt, batch=B, tq=512)
        gla = _gla(z.reshape(B, S, -1), gq.reshape(B, S, -1), gk.reshape(B, S, -1), gv.reshape(B, S, -1),
                   gr.reshape(B, S, -1), up_hi, up_lo, gbias, gla_norm[l].reshape(1, -1), unroll=4)

        h1, xn, aff = _mix_out(h, att.reshape(T, -1), gla.reshape(T, -1), sga, sgg,
                               w_branch_attn[l].astype(BF16), w_branch_gla[l].astype(BF16),
                               w_out[l].astype(BF16), norm_ffn[l].reshape(1, D), wr2, tm=256)

        aff3 = aff.reshape(B, S, N_EXPERTS)
        idx = _route(jnp.swapaxes(aff3, 1, 2), cap)
        idx_flat = jnp.swapaxes(idx, 1, 2).reshape(-1)
        xg, wg = _gather(idx_flat, xn.reshape(B, S, D), aff3, cap)
        y = _ffn(xg, wg, w_exp_gate[l].astype(BF16), w_exp_up[l].astype(BF16), w_exp_down[l].astype(BF16))
        moe = _scatter(idx_flat, y, S)

        last = l == depth - 1
        gfin = norm_final.reshape(1, D)
        assert last, "the final norm is fused into the last layer's PLE kernel"
        h = _ple_out(h1, moe.reshape(T, D), p[l].reshape(T, PLE_DIM), norm_ple[l].reshape(1, D),
                     w_ple_gate[l].astype(BF16), w_ple[l].astype(BF16), gfin, tm=256)
    return h.reshape(B, S, D)
```

```python
import functools
import math

import jax
import jax.numpy as jnp
from jax import lax
from jax.experimental import pallas as pl
from jax.experimental.pallas import tpu as pltpu

D_MODEL = 1024
ATT_HEADS = 8
ATT_KV_HEADS = 2
ATT_HEAD_DIM = 64
ATT_GROUP = ATT_HEADS // ATT_KV_HEADS
ATT_Q_DIM = ATT_HEADS * ATT_HEAD_DIM
ATT_KV_DIM = ATT_KV_HEADS * ATT_HEAD_DIM
WINDOW = 128
ROPE_DIM = ATT_HEAD_DIM // 4
ROPE_THETA = 500000.0
GLA_HEADS = 4
GLA_KEY_DIM = D_MODEL // 2
GLA_VAL_DIM = D_MODEL
GLA_DK = GLA_KEY_DIM // GLA_HEADS
GLA_DV = GLA_VAL_DIM // GLA_HEADS
GLA_GATE_RANK = 16
GLA_GATE_NORM = 16.0
GLA_CHUNK = 64
N_EXPERTS = 16
EXPERT_FF = D_MODEL
CAPACITY_FACTOR = 2
PLE_DIM = 256
EPS = 1e-6

LANES = 128
MIB = 1024 * 1024
BF16 = jnp.bfloat16
F32 = jnp.float32
LOG2E = math.log2(math.e)

NT_DIMS = (((1,), (1,)), ((), ()))
TN_DIMS = (((0,), (0,)), ((), ()))


def _cparams(sem, vmem_mib):
    return pltpu.CompilerParams(dimension_semantics=sem, vmem_limit_bytes=vmem_mib * MIB)


def _full(shape):
    n = len(shape)
    return pl.BlockSpec(shape, lambda *_: (0,) * n)


def _rms(x, gain):
    ms = jnp.mean(x * x, axis=-1, keepdims=True)
    return x * lax.rsqrt(ms + EPS) * gain


def _sigmoid(x):
    return 1.0 / (1.0 + jnp.exp(-x))


SUBLANES = 8
ROW_TILES = D_MODEL // LANES


def _store_token_tiles(ref2d, x):
    rows = x.shape[0]
    for j in range(ROW_TILES):
        ref2d[pl.ds(j, rows, stride=ROW_TILES), :] = x[:, j * LANES:(j + 1) * LANES]


def _load_token_tiles(ref2d, rows):
    return jnp.concatenate([ref2d[pl.ds(j, rows, stride=ROW_TILES), :] for j in range(ROW_TILES)], axis=1)


def _rope(t, cos_t, sin_t, first_half):
    fwd = pltpu.roll(t, LANES - ROPE_DIM // 2, axis=1)
    bwd = pltpu.roll(t, ROPE_DIM // 2, axis=1)
    return t * cos_t + jnp.where(first_half, fwd, bwd) * sin_t


def _in_proj_kernel(x_ref, posc_ref, posr_ref, invfl_ref, invfc_ref, gain_ref, wqt_ref, wk_ref, wvt_ref,
                    wgqk_ref, wgv_ref, wgr_ref, wz_ref, wgate_ref,
                    qt_ref, k0_ref, k1_ref, vt_ref, gq_ref, gk_ref, gv_ref, gr_ref, z_ref, sga_ref, sgg_ref):
    a = _rms(x_ref[...], gain_ref[...]).astype(BF16)
    half = ROPE_DIM // 2

    qt = lax.dot_general(wqt_ref[...], a, NT_DIMS, preferred_element_type=F32)
    ang_t = invfc_ref[...] * posr_ref[...].astype(F32)
    cos_r, sin_r = jnp.cos(ang_t), jnp.sin(ang_t)
    qscale = ATT_HEAD_DIM ** -0.5 * LOG2E
    for h in range(ATT_HEADS):
        r0 = h * ATT_HEAD_DIM
        t1, t2 = qt[r0:r0 + half], qt[r0 + half:r0 + ROPE_DIM]
        qt_ref[r0:r0 + half, :] = ((t1 * cos_r - t2 * sin_r) * qscale).astype(BF16)
        qt_ref[r0 + half:r0 + ROPE_DIM, :] = ((t2 * cos_r + t1 * sin_r) * qscale).astype(BF16)
        qt_ref[r0 + ROPE_DIM:r0 + ATT_HEAD_DIM, :] = (qt[r0 + ROPE_DIM:r0 + ATT_HEAD_DIM] * qscale).astype(BF16)

    lane = lax.broadcasted_iota(jnp.int32, (1, LANES), 1)
    d = lane % ATT_HEAD_DIM
    first_half = d < half
    in_rope = d < ROPE_DIM
    ang = posc_ref[...].astype(F32) * invfl_ref[...]
    cos_t = jnp.where(in_rope, jnp.cos(ang), 1.0)
    sin_t = jnp.where(first_half, -jnp.sin(ang), jnp.where(in_rope, jnp.sin(ang), 0.0))
    k = _rope(jnp.dot(a, wk_ref[...], preferred_element_type=F32), cos_t, sin_t, first_half).astype(BF16)
    k0_ref[...] = k[:, :ATT_HEAD_DIM]
    k1_ref[...] = k[:, ATT_HEAD_DIM:]
    vt_ref[...] = lax.dot_general(wvt_ref[...], a, NT_DIMS, preferred_element_type=F32).astype(BF16)

    gqk = jnp.dot(a, wgqk_ref[...], preferred_element_type=F32)
    gq_ref[...] = (gqk[:, :GLA_KEY_DIM] * (GLA_DK ** -0.5)).astype(BF16)
    gk_ref[...] = gqk[:, GLA_KEY_DIM:].astype(BF16)
    gv_ref[...] = jnp.dot(a, wgv_ref[...], preferred_element_type=F32).astype(BF16)
    gr_ref[...] = jnp.dot(a, wgr_ref[...], preferred_element_type=F32).astype(BF16)
    z_ref[...] = jnp.dot(a, wz_ref[...], preferred_element_type=F32)
    gates = jnp.dot(a, wgate_ref[...], preferred_element_type=F32)
    sga_ref[...] = _sigmoid(gates[:, :D_MODEL]).astype(BF16)
    sgg_ref[...] = _sigmoid(gates[:, D_MODEL:]).astype(BF16)


def _in_proj(x2, posc, posr, invfl, invfc, gain, wqt, wk, wvt, wgqk, wgv, wgr, wz, wgate, tm):
    T = x2.shape[0]
    row = lambda n: pl.BlockSpec((tm, n), lambda i: (i, 0))
    col = lambda n: pl.BlockSpec((n, tm), lambda i: (0, i))
    row_widths = (ATT_HEAD_DIM, ATT_HEAD_DIM, None, GLA_KEY_DIM, GLA_KEY_DIM, GLA_VAL_DIM,
                  GLA_VAL_DIM, 2 * GLA_GATE_RANK, D_MODEL, D_MODEL)
    row_dtypes = (BF16,) * 7 + (F32, BF16, BF16)
    out_specs = [col(ATT_Q_DIM)]
    out_shape = [jax.ShapeDtypeStruct((ATT_Q_DIM, T), BF16)]
    for n, dt in zip(row_widths, row_dtypes):
        if n is None:
            out_specs.append(col(ATT_KV_DIM))
            out_shape.append(jax.ShapeDtypeStruct((ATT_KV_DIM, T), BF16))
        else:
            out_specs.append(row(n))
            out_shape.append(jax.ShapeDtypeStruct((T, n), dt))
    consts = (invfl, invfc, gain, wqt, wk, wvt, wgqk, wgv, wgr, wz, wgate)
    return pl.pallas_call(
        _in_proj_kernel,
        grid=(T // tm,),
        in_specs=[row(D_MODEL), row(1), col(1)] + [_full(c.shape) for c in consts],
        out_specs=out_specs,
        out_shape=out_shape,
        compiler_params=_cparams(("parallel",), 56),
        name="in_proj",
    )(x2, posc, posr, *consts)


def _swa_kernel(sink_ref, qt_ref, k0_ref, k1_ref, vt_ref, o_ref, *, tq, seq):
    blk = WINDOW
    span = 3 * blk
    hd = ATT_HEAD_DIM
    n = pl.program_id(1)
    ones = jnp.ones((16, span), BF16)
    for sb in range(tq // blk):
        q0 = n * tq + sb * blk
        start = pl.multiple_of(jnp.clip(q0 - blk, 0, seq - span), blk)
        kj = start + lax.broadcasted_iota(jnp.int32, (span, blk), 0)
        qi = q0 + lax.broadcasted_iota(jnp.int32, (span, blk), 1)
        valid = jnp.abs(qi - kj) <= WINDOW
        outs = []
        for g, k_ref in enumerate((k0_ref, k1_ref)):
            kw = k_ref[0, pl.ds(start, span), :]
            vaug = jnp.concatenate([vt_ref[g * hd:(g + 1) * hd, pl.ds(start, span)], ones], axis=0)
            heads = range(g * ATT_GROUP, (g + 1) * ATT_GROUP)
            qs = jnp.concatenate([qt_ref[h * hd:(h + 1) * hd, sb * blk:(sb + 1) * blk] for h in heads], axis=1)
            s_all = jnp.dot(kw, qs, preferred_element_type=F32)
            for i, h in enumerate(heads):
                s = jnp.where(valid, s_all[:, i * blk:(i + 1) * blk], -jnp.inf)
                sink = sink_ref[h] * LOG2E
                m = jnp.maximum(jnp.max(s, axis=0, keepdims=True), sink)
                e = jnp.exp2(s - m).astype(BF16)
                r = jnp.dot(vaug, e, preferred_element_type=F32)
                den = r[hd:hd + 1] + jnp.exp2(sink - m)
                outs.append(r[:hd] / den)
        for pr in range(ATT_HEADS // 2):
            pair = jnp.concatenate([outs[2 * pr], outs[2 * pr + 1]], axis=0)
            o_ref[0, sb * blk:(sb + 1) * blk, pr * 2 * hd:(pr + 1) * 2 * hd] = pair.T.astype(BF16)


def _swa(sink, qt, k0, k1, vt, batch, tq):
    S = k0.shape[1]
    nq = S // tq
    kspec = pl.BlockSpec((1, S, ATT_HEAD_DIM), lambda b, n: (b, 0, 0))
    return pl.pallas_call(
        functools.partial(_swa_kernel, tq=tq, seq=S),
        grid=(batch, nq),
        in_specs=[pl.BlockSpec(memory_space=pltpu.SMEM),
                  pl.BlockSpec((ATT_Q_DIM, tq), lambda b, n: (0, b * nq + n)),
                  kspec, kspec,
                  pl.BlockSpec((ATT_KV_DIM, S), lambda b, n: (0, b))],
        out_specs=pl.BlockSpec((1, tq, ATT_Q_DIM), lambda b, n: (b, n, 0)),
        out_shape=jax.ShapeDtypeStruct((batch, S, ATT_Q_DIM), BF16),
        compiler_params=_cparams(("parallel", "parallel"), 32),
        name="swa",
    )(sink, qt, k0, k1, vt)


def _log_sigmoid(u):
    return jnp.minimum(u, 0.0) - jnp.log(1.0 + jnp.exp(-jnp.abs(u)))


def _split2(x):
    hi = x.astype(BF16)
    return hi, (x - hi.astype(F32)).astype(BF16)


def _gla_kernel(z_ref, q_ref, k_ref, v_ref, r_ref, uph_ref, upl_ref, bias_ref, gain_ref,
                o_ref, cf_ref, cb_ref, st_ref, sf_ref, sb_ref, *, seq, unroll):
    L = GLA_CHUNK
    nc = seq // L
    grp = 4 * L

    row = lax.broadcasted_iota(jnp.int32, (grp, grp), 0)
    col = lax.broadcasted_iota(jnp.int32, (grp, grp), 1)
    same = (row // L) == (col // L)
    tri_lo = jnp.where(same & (col <= row), 1.0, 0.0).astype(BF16)
    tri_up = jnp.where(same & (col >= row), 1.0, 0.0).astype(BF16)
    mm = functools.partial(jnp.dot, preferred_element_type=F32)

    def cum_body(i, carry):
        r0 = pl.multiple_of(i * grp, grp)
        zh, zl = _split2(z_ref[0, pl.ds(r0, grp), :])
        u = mm(zh, uph_ref[...]) + mm(zl, uph_ref[...]) + mm(zh, upl_ref[...]) + bias_ref[...]
        la = _log_sigmoid(u) * (1.0 / GLA_GATE_NORM)
        fh, fl = _split2(la[:, :GLA_DK])
        bh, bl = _split2(la[:, GLA_DK:])
        cf_ref[pl.ds(r0, grp), :] = mm(tri_lo, fh) + mm(tri_lo, fl)
        cb_ref[pl.ds(r0, grp), :] = mm(tri_up, bh) + mm(tri_up, bl)
        return carry

    lax.fori_loop(0, seq // grp, cum_body, 0, unroll=2)

    sf_ref[...] = jnp.zeros_like(sf_ref)
    sb_ref[...] = jnp.zeros_like(sb_ref)

    def state_body(i, carry):
        rf = pl.multiple_of(i * L, L)
        cf = cf_ref[pl.ds(rf, L), :]
        gf = cf_ref[pl.ds(rf + L - 1, 1), :]
        kf = k_ref[0, pl.ds(rf, L), :].astype(F32)
        kend = (kf * jnp.exp(gf - cf)).astype(BF16)
        kv = lax.dot_general(v_ref[0, pl.ds(rf, L), :], kend, TN_DIMS, preferred_element_type=F32)
        st_ref[i, :, 0:GLA_DK] = sf_ref[...].astype(BF16)
        sf_ref[...] = sf_ref[...] * jnp.exp(gf) + kv

        j = nc - 1 - i
        rb = pl.multiple_of(j * L, L)
        cb = cb_ref[pl.ds(rb, L), :]
        gb = cb_ref[pl.ds(rb, 1), :]
        kb = k_ref[0, pl.ds(rb, L), :].astype(F32)
        kend_b = (kb * jnp.exp(gb - cb)).astype(BF16)
        kv_b = lax.dot_general(v_ref[0, pl.ds(rb, L), :], kend_b, TN_DIMS, preferred_element_type=F32)
        st_ref[j, :, GLA_DK:2 * GLA_DK] = sb_ref[...].astype(BF16)
        sb_ref[...] = sb_ref[...] * jnp.exp(gb) + kv_b
        return carry

    lax.fori_loop(0, nc, state_body, 0, unroll=unroll)

    lrow = lax.broadcasted_iota(jnp.int32, (L, L), 0)
    lcol = lax.broadcasted_iota(jnp.int32, (L, L), 1)
    causal = lrow >= lcol

    def out_body(c, carry):
        r0 = pl.multiple_of(c * L, L)
        q = q_ref[0, pl.ds(r0, L), :].astype(F32)
        k = k_ref[0, pl.ds(r0, L), :].astype(F32)
        v = v_ref[0, pl.ds(r0, L), :]
        cf = cf_ref[pl.ds(r0, L), :]
        cb = cb_ref[pl.ds(r0, L), :]
        qf = (q * jnp.exp(cf)).astype(BF16)
        kf = (k * jnp.exp(-cf)).astype(BF16)
        qb = (q * jnp.exp(cb)).astype(BF16)
        kb = (k * jnp.exp(-cb)).astype(BF16)
        af = lax.dot_general(qf, kf, NT_DIMS, preferred_element_type=F32)
        ab = lax.dot_general(qb, kb, NT_DIMS, preferred_element_type=F32)
        attn = jnp.where(causal, af, ab).astype(BF16)
        o = jnp.dot(attn, v, preferred_element_type=F32)
        o = o + lax.dot_general(jnp.concatenate([qf, qb], axis=1), st_ref[c], NT_DIMS,
                                preferred_element_type=F32)
        o = _rms(o, gain_ref[...])
        r = r_ref[0, pl.ds(r0, L), :].astype(F32)
        o_ref[0, pl.ds(r0, L), :] = (o * (r * _sigmoid(r))).astype(BF16)
        return carry

    lax.fori_loop(0, nc, out_body, 0, unroll=unroll)


def _gla(z, gq, gk, gv, gr, uph, upl, bias, gain, unroll):
    B, S, _ = gq.shape
    nc = S // GLA_CHUNK
    seq_blk = lambda n: pl.BlockSpec((1, S, n), lambda b, h: (b, 0, h))
    head_blk = lambda r, n: pl.BlockSpec((None, r, n), lambda b, h: (h, 0, 0))
    return pl.pallas_call(
        functools.partial(_gla_kernel, seq=S, unroll=unroll),
        grid=(B, GLA_HEADS),
        in_specs=[pl.BlockSpec((1, S, 2 * GLA_GATE_RANK), lambda b, h: (b, 0, 0)),
                  seq_blk(GLA_DK), seq_blk(GLA_DK), seq_blk(GLA_DV), seq_blk(GLA_DV),
                  head_blk(2 * GLA_GATE_RANK, 2 * GLA_DK), head_blk(2 * GLA_GATE_RANK, 2 * GLA_DK),
                  head_blk(1, 2 * GLA_DK),
                  pl.BlockSpec((1, GLA_DV), lambda b, h: (0, h))],
        out_specs=seq_blk(GLA_DV),
        out_shape=jax.ShapeDtypeStruct((B, S, GLA_VAL_DIM), BF16),
        scratch_shapes=[pltpu.VMEM((S, GLA_DK), F32), pltpu.VMEM((S, GLA_DK), F32),
                        pltpu.VMEM((nc, GLA_DV, 2 * GLA_DK), BF16),
                        pltpu.VMEM((GLA_DV, GLA_DK), F32), pltpu.VMEM((GLA_DV, GLA_DK), F32)],
        compiler_params=_cparams(("parallel", "parallel"), 48),
        name="gla",
    )(z, gq, gk, gv, gr, uph, upl, bias, gain)


def _mix_out_kernel(x_ref, a_ref, g_ref, sga_ref, sgg_ref, wa_ref, wb_ref, wo_ref, gain_ref, wr_ref,
                    h_ref, xn_ref, aff_ref):
    y_att = jnp.dot(a_ref[...], wa_ref[...], preferred_element_type=F32)
    y_gla = jnp.dot(g_ref[...], wb_ref[...], preferred_element_type=F32)
    merged = sga_ref[...].astype(F32) * y_att + sgg_ref[...].astype(F32) * y_gla
    h = x_ref[...] + jnp.dot(merged.astype(BF16), wo_ref[...], preferred_element_type=F32)
    h_ref[...] = h
    xn = _rms(h, gain_ref[...])
    _store_token_tiles(xn_ref, xn)
    hi = xn.astype(BF16)
    lo = (xn - hi.astype(F32)).astype(BF16)
    part = jnp.dot(jnp.concatenate([hi, lo], axis=1), wr_ref[...], preferred_element_type=F32)
    logits = part[:, :N_EXPERTS] + part[:, N_EXPERTS:]
    m = jnp.max(logits, axis=-1, keepdims=True)
    e = jnp.exp(logits - m)
    aff_ref[...] = e / jnp.sum(e, axis=-1, keepdims=True)


def _mix_out(x2, a, g, sga, sgg, wa, wb, wo, gain, wr, tm):
    T = x2.shape[0]
    row = lambda n: pl.BlockSpec((tm, n), lambda i: (i, 0))
    return pl.pallas_call(
        _mix_out_kernel,
        grid=(T // tm,),
        in_specs=[row(D_MODEL), row(ATT_Q_DIM), row(GLA_VAL_DIM), row(D_MODEL), row(D_MODEL),
                  _full(wa.shape), _full(wb.shape), _full(wo.shape), _full(gain.shape), _full(wr.shape)],
        out_specs=[row(D_MODEL), pl.BlockSpec((tm * ROW_TILES, LANES), lambda i: (i, 0)), row(N_EXPERTS)],
        out_shape=[jax.ShapeDtypeStruct((T, D_MODEL), F32), jax.ShapeDtypeStruct((T * ROW_TILES, LANES), F32),
                   jax.ShapeDtypeStruct((T, N_EXPERTS), F32)],
        compiler_params=_cparams(("parallel",), 48),
        name="mix_out",
    )(x2, a, g, sga, sgg, wa, wb, wo, gain, wr)


def _route_kernel(aff_ref, idx_ref, cum_ref, *, cap, seq):
    E = N_EXPERTS
    bits = lax.bitcast_convert_type(aff_ref[0], jnp.int32)
    count = lambda mask: jnp.sum(mask.astype(jnp.int32), axis=1, keepdims=True)

    def thr_body(t, thr):
        cand = thr | jnp.left_shift(jnp.int32(1), 30 - t)
        return jnp.where(count(bits >= cand) >= cap, cand, thr)

    thr = lax.fori_loop(0, 31, thr_body, jnp.zeros((E, 1), jnp.int32))
    above = bits > thr
    tie = bits == thr
    need = cap - count(above)

    pos = lax.broadcasted_iota(jnp.int32, (E, seq), 1)

    def tie_body(t, last):
        cand = last | jnp.left_shift(jnp.int32(1), (seq.bit_length() - 2) - t)
        return jnp.where(count(tie & (pos < cand)) < need, cand, last)

    last = lax.fori_loop(0, seq.bit_length() - 1, tie_body, jnp.zeros((E, 1), jnp.int32))
    sel = (above | (tie & (pos <= last))).astype(BF16)

    tri = (lax.broadcasted_iota(jnp.int32, (LANES, LANES), 0)
           <= lax.broadcasted_iota(jnp.int32, (LANES, LANES), 1)).astype(BF16)
    carry = jnp.zeros((E, 1), F32)
    for t in range(seq // LANES):
        local = jnp.dot(sel[:, t * LANES:(t + 1) * LANES], tri, preferred_element_type=F32) + carry
        cum_ref[:, t * LANES:(t + 1) * LANES] = local
        carry = local[:, LANES - 1:LANES]

    slot = lax.broadcasted_iota(jnp.int32, (cap, LANES), 0).astype(F32)
    for e in range(E):
        def cnt_body(t, acc):
            c = cum_ref[e:e + 1, pl.ds(pl.multiple_of(t * LANES, LANES), LANES)]
            return acc + jnp.where(c <= slot, 1.0, 0.0)

        acc = lax.fori_loop(0, seq // LANES, cnt_body, jnp.zeros((cap, LANES), F32))
        idx_ref[0, :, e:e + 1] = jnp.sum(acc, axis=1, keepdims=True).astype(jnp.int32)


def _route(aff_t, cap):
    B, E, S = aff_t.shape
    return pl.pallas_call(
        functools.partial(_route_kernel, cap=cap, seq=S),
        grid=(B,),
        in_specs=[pl.BlockSpec((1, E, S), lambda b: (b, 0, 0))],
        out_specs=pl.BlockSpec((1, cap, E), lambda b: (b, 0, 0)),
        out_shape=jax.ShapeDtypeStruct((B, cap, E), jnp.int32),
        scratch_shapes=[pltpu.VMEM((E, S), F32)],
        compiler_params=_cparams(("parallel",), 32),
        name="route",
    )(aff_t)


def _gather_kernel(idx_ref, xn_ref, aff_ref, xg_ref, wg_ref, xs_ref, ws_ref, *, cap):
    b = pl.program_id(0)
    e = pl.program_id(1)
    base = (b * N_EXPERTS + e) * cap

    def body(i, carry):
        t = idx_ref[base + i]
        xs_ref[pl.ds(pl.multiple_of(i * SUBLANES, SUBLANES), SUBLANES), :] = xn_ref[0, t]
        ws_ref[pl.ds(i, 1), :] = aff_ref[0, pl.ds(t, 1), :]
        return carry

    lax.fori_loop(0, cap, body, 0, unroll=8)
    xg_ref[0, 0] = _load_token_tiles(xs_ref, cap).astype(BF16)
    lane = lax.broadcasted_iota(jnp.int32, (cap, N_EXPERTS), 1)
    wg_ref[0, 0] = jnp.sum(jnp.where(lane == e, ws_ref[...], 0.0), axis=1, keepdims=True)


def _gather(idx_flat, xn_tiles, aff, cap):
    B, S = xn_tiles.shape[:2]
    E = N_EXPERTS
    grid_spec = pltpu.PrefetchScalarGridSpec(
        num_scalar_prefetch=1,
        grid=(B, E),
        in_specs=[pl.BlockSpec((1, S, SUBLANES, LANES), lambda b, e, idx: (b, 0, 0, 0)),
                  pl.BlockSpec((1, S, E), lambda b, e, idx: (b, 0, 0))],
        out_specs=[pl.BlockSpec((1, 1, cap, D_MODEL), lambda b, e, idx: (b, e, 0, 0)),
                   pl.BlockSpec((1, 1, cap, 1), lambda b, e, idx: (b, e, 0, 0))],
        scratch_shapes=[pltpu.VMEM((cap * SUBLANES, LANES), F32), pltpu.VMEM((cap, E), F32)],
    )
    return pl.pallas_call(
        functools.partial(_gather_kernel, cap=cap),
        grid_spec=grid_spec,
        out_shape=[jax.ShapeDtypeStruct((B, E, cap, D_MODEL), BF16),
                   jax.ShapeDtypeStruct((B, E, cap, 1), F32)],
        compiler_params=_cparams(("arbitrary", "arbitrary"), 48),
        name="gather",
    )(idx_flat, xn_tiles, aff)


def _ffn_kernel(xg_ref, wg_ref, w1_ref, w2_ref, w3_ref, y_ref):
    xg = xg_ref[0, 0]
    gate = jnp.dot(xg, w1_ref[0], preferred_element_type=F32)
    up = jnp.dot(xg, w2_ref[0], preferred_element_type=F32)
    hid = (gate * _sigmoid(gate) * up).astype(BF16)
    y = jnp.dot(hid, w3_ref[0], preferred_element_type=F32) * wg_ref[0, 0]
    _store_token_tiles(y_ref.at[0, 0], y)


def _ffn(xg, wg, w1, w2, w3):
    B, E, C, D = xg.shape
    F = w1.shape[-1]
    return pl.pallas_call(
        _ffn_kernel,
        grid=(E, B),
        in_specs=[pl.BlockSpec((1, 1, C, D), lambda e, b: (b, e, 0, 0)),
                  pl.BlockSpec((1, 1, C, 1), lambda e, b: (b, e, 0, 0)),
                  pl.BlockSpec((1, D, F), lambda e, b: (e, 0, 0)),
                  pl.BlockSpec((1, D, F), lambda e, b: (e, 0, 0)),
                  pl.BlockSpec((1, F, D), lambda e, b: (e, 0, 0))],
        out_specs=pl.BlockSpec((1, 1, C * ROW_TILES, LANES), lambda e, b: (b, e, 0, 0)),
        out_shape=jax.ShapeDtypeStruct((B, E, C * ROW_TILES, LANES), F32),
        compiler_params=_cparams(("parallel", "parallel"), 48),
        name="ffn",
    )(xg, wg, w1, w2, w3)


SCATTER_BATCH = 16


def _scatter_kernel(idx_ref, y_ref, o_ref, *, cap):
    b = pl.program_id(0)
    e = pl.program_id(1)
    base = (b * N_EXPERTS + e) * cap

    @pl.when(e == 0)
    def _():
        o_ref[...] = jnp.zeros_like(o_ref)

    def body(blk, carry):
        i0 = blk * SCATTER_BATCH
        ts = [idx_ref[base + i0 + u] for u in range(SCATTER_BATCH)]
        new = [o_ref[0, ts[u]] + y_ref[0, 0, i0 + u] for u in range(SCATTER_BATCH)]
        for u in range(SCATTER_BATCH):
            o_ref[0, ts[u]] = new[u]
        return carry

    lax.fori_loop(0, cap // SCATTER_BATCH, body, 0)


def _scatter(idx_flat, y_tiles, seq):
    B, E, C = y_tiles.shape[:3]
    grid_spec = pltpu.PrefetchScalarGridSpec(
        num_scalar_prefetch=1,
        grid=(B, E),
        in_specs=[pl.BlockSpec((1, 1, C, SUBLANES, LANES), lambda b, e, idx: (b, e, 0, 0, 0))],
        out_specs=pl.BlockSpec((1, seq, SUBLANES, LANES), lambda b, e, idx: (b, 0, 0, 0)),
    )
    return pl.pallas_call(
        functools.partial(_scatter_kernel, cap=C),
        grid_spec=grid_spec,
        out_shape=jax.ShapeDtypeStruct((B, seq, SUBLANES, LANES), F32),
        compiler_params=_cparams(("arbitrary", "arbitrary"), 48),
        name="scatter",
    )(idx_flat, y_tiles)


def _ple_out_kernel(h_ref, moe_ref, p_ref, gple_ref, wpg_ref, wple_ref, gfin_ref, o_ref):
    h = h_ref[...] + _load_token_tiles(moe_ref, h_ref.shape[0])
    n = _rms(h, gple_ref[...]).astype(BF16)
    gate = _sigmoid(jnp.dot(n, wpg_ref[...], preferred_element_type=F32))
    emb = jnp.dot(p_ref[...].astype(BF16), wple_ref[...], preferred_element_type=F32)
    o_ref[...] = _rms(h + gate * emb, gfin_ref[...])


def _ple_out(h, moe, p2, gple, wpg, wple, gfin, tm):
    T = h.shape[0]
    row = lambda n: pl.BlockSpec((tm, n), lambda i: (i, 0))
    return pl.pallas_call(
        _ple_out_kernel,
        grid=(T // tm,),
        in_specs=[row(D_MODEL), pl.BlockSpec((tm * ROW_TILES, LANES), lambda i: (i, 0)), row(PLE_DIM),
                  _full(gple.shape), _full(wpg.shape), _full(wple.shape), _full(gfin.shape)],
        out_specs=row(D_MODEL),
        out_shape=jax.ShapeDtypeStruct((T, D_MODEL), F32),
        compiler_params=_cparams(("parallel",), 48),
        name="ple_out",
    )(h, moe, p2, gple, wpg, wple, gfin)


def kernel(x, p, positions, norm_mix, w_in, gla_gate_up_fwd, gla_gate_bias_fwd, gla_gate_up_bwd, gla_gate_bias_bwd, attn_sink, gla_norm, w_branch_attn, w_branch_gla, w_out, norm_ffn, w_router, w_exp_gate, w_exp_up, w_exp_down, norm_ple, w_ple_gate, w_ple, norm_final):
    B, S, D = x.shape
    T = B * S
    depth = w_in.shape[0]
    cap = CAPACITY_FACTOR * S // N_EXPERTS
    R = GLA_GATE_RANK

    posc = positions.reshape(T, 1)
    posr = positions.reshape(1, T)
    inv_freq = ROPE_THETA ** (-jnp.arange(0, ROPE_DIM, 2, dtype=F32) / ROPE_DIM)
    invfl = jnp.tile(inv_freq, LANES // (ROPE_DIM // 2)).reshape(1, LANES)
    invfc = inv_freq.reshape(ROPE_DIM // 2, 1)

    h = x.reshape(T, D)
    for l in range(depth):
        o = 0
        cols = {}
        for name, n in (("q", ATT_Q_DIM), ("k", ATT_KV_DIM), ("v", ATT_KV_DIM), ("gqk", 2 * GLA_KEY_DIM),
                        ("gv", GLA_VAL_DIM), ("gr", GLA_VAL_DIM), ("z", 2 * R), ("gate", 2 * D_MODEL)):
            cols[name] = w_in[l][:, o:o + n].astype(BF16)
            o += n
        per_head = lambda w: w.reshape(-1, GLA_HEADS, GLA_DK).swapaxes(0, 1)
        upf, upb = per_head(gla_gate_up_fwd[l]), per_head(gla_gate_up_bwd[l])
        up = jnp.concatenate([jnp.concatenate([upf, jnp.zeros_like(upf)], axis=2),
                              jnp.concatenate([jnp.zeros_like(upb), upb], axis=2)], axis=1)
        up_hi = up.astype(BF16)
        up_lo = (up - up_hi.astype(F32)).astype(BF16)
        gbias = jnp.concatenate([per_head(gla_gate_bias_fwd[l]), per_head(gla_gate_bias_bwd[l])], axis=2)
        wr = w_router[l]
        wr_hi = wr.astype(BF16)
        wr_lo = (wr - wr_hi.astype(F32)).astype(BF16)
        wr2 = jnp.concatenate([jnp.concatenate([wr_hi, wr_lo], axis=1),
                               jnp.concatenate([wr_hi, jnp.zeros_like(wr_lo)], axis=1)], axis=0)

        qt, k0, k1, vt, gq, gk, gv, gr, z, sga, sgg = _in_proj(
            h, posc, posr, invfl, invfc, norm_mix[l].reshape(1, D), cols["q"].T, cols["k"], cols["v"].T,
            cols["gqk"], cols["gv"], cols["gr"], cols["z"], cols["gate"], tm=256)

        att = _swa(attn_sink[l], qt, k0.reshape(B, S, -1), k1.reshape(B, S, -1), vt, batch=B, tq=512)
        gla = _gla(z.reshape(B, S, -1), gq.reshape(B, S, -1), gk.reshape(B, S, -1), gv.reshape(B, S, -1),
                   gr.reshape(B, S, -1), up_hi, up_lo, gbias, gla_norm[l].reshape(1, -1), unroll=4)

        h1, xn, aff = _mix_out(h, att.reshape(T, -1), gla.reshape(T, -1), sga, sgg,
                               w_branch_attn[l].astype(BF16), w_branch_gla[l].astype(BF16),
                               w_out[l].astype(BF16), norm_ffn[l].reshape(1, D), wr2, tm=256)

        aff3 = aff.reshape(B, S, N_EXPERTS)
        idx = _route(jnp.swapaxes(aff3, 1, 2), cap)
        idx_flat = jnp.swapaxes(idx, 1, 2).reshape(-1)
        xg, wg = _gather(idx_flat, xn.reshape(B, S, SUBLANES, LANES), aff3, cap)
        y = _ffn(xg, wg, w_exp_gate[l].astype(BF16), w_exp_up[l].astype(BF16), w_exp_down[l].astype(BF16))
        moe = _scatter(idx_flat, y.reshape(B, N_EXPERTS, cap, SUBLANES, LANES), S)

        last = l == depth - 1
        gfin = norm_final.reshape(1, D)
        assert last, "the final norm is fused into the last layer's PLE kernel"
        h = _ple_out(h1, moe.reshape(T * ROW_TILES, LANES), p[l].reshape(T, PLE_DIM), norm_ple[l].reshape(1, D),
                     w_ple_gate[l].astype(BF16), w_ple[l].astype(BF16), gfin, tm=256)
    return h.reshape(B, S, D)
```

```python
import functools
import math

import jax
import jax.numpy as jnp
from jax import lax
from jax.experimental import pallas as pl
from jax.experimental.pallas import tpu as pltpu

D_MODEL = 1024
ATT_HEADS = 8
ATT_KV_HEADS = 2
ATT_HEAD_DIM = 64
ATT_GROUP = ATT_HEADS // ATT_KV_HEADS
ATT_Q_DIM = ATT_HEADS * ATT_HEAD_DIM
ATT_KV_DIM = ATT_KV_HEADS * ATT_HEAD_DIM
WINDOW = 128
ROPE_DIM = ATT_HEAD_DIM // 4
ROPE_THETA = 500000.0
GLA_HEADS = 4
GLA_KEY_DIM = D_MODEL // 2
GLA_VAL_DIM = D_MODEL
GLA_DK = GLA_KEY_DIM // GLA_HEADS
GLA_DV = GLA_VAL_DIM // GLA_HEADS
GLA_GATE_RANK = 16
GLA_GATE_NORM = 16.0
GLA_CHUNK = 64
N_EXPERTS = 16
EXPERT_FF = D_MODEL
CAPACITY_FACTOR = 2
PLE_DIM = 256
EPS = 1e-6

LANES = 128
MIB = 1024 * 1024
BF16 = jnp.bfloat16
F32 = jnp.float32
LOG2E = math.log2(math.e)

NT_DIMS = (((1,), (1,)), ((), ()))
TN_DIMS = (((0,), (0,)), ((), ()))


def _cparams(sem, vmem_mib):
    return pltpu.CompilerParams(dimension_semantics=sem, vmem_limit_bytes=vmem_mib * MIB)


def _full(shape):
    n = len(shape)
    return pl.BlockSpec(shape, lambda *_: (0,) * n)


def _rms(x, gain):
    ms = jnp.mean(x * x, axis=-1, keepdims=True)
    return x * lax.rsqrt(ms + EPS) * gain


def _sigmoid(x):
    return 1.0 / (1.0 + jnp.exp(-x))


SUBLANES = 8
ROW_TILES = D_MODEL // LANES


def _store_token_tiles(ref2d, x):
    rows = x.shape[0]
    for j in range(ROW_TILES):
        ref2d[pl.ds(j, rows, stride=ROW_TILES), :] = x[:, j * LANES:(j + 1) * LANES]


def _load_token_tiles(ref2d, rows):
    return jnp.concatenate([ref2d[pl.ds(j, rows, stride=ROW_TILES), :] for j in range(ROW_TILES)], axis=1)


def _rope(t, cos_t, sin_t, first_half):
    fwd = pltpu.roll(t, LANES - ROPE_DIM // 2, axis=1)
    bwd = pltpu.roll(t, ROPE_DIM // 2, axis=1)
    return t * cos_t + jnp.where(first_half, fwd, bwd) * sin_t


def _in_proj_kernel(x_ref, posc_ref, posr_ref, invfl_ref, invfc_ref, gain_ref, wqt_ref, wk_ref, wvt_ref,
                    wgqk_ref, wgv_ref, wgr_ref, wz_ref, wgate_ref,
                    qt_ref, k0_ref, k1_ref, vt_ref, gq_ref, gk_ref, gv_ref, gr_ref, z_ref, sga_ref, sgg_ref):
    a = _rms(x_ref[...], gain_ref[...]).astype(BF16)
    half = ROPE_DIM // 2

    qt = lax.dot_general(wqt_ref[...], a, NT_DIMS, preferred_element_type=F32)
    ang_t = invfc_ref[...] * posr_ref[...].astype(F32)
    cos_r, sin_r = jnp.cos(ang_t), jnp.sin(ang_t)
    qscale = ATT_HEAD_DIM ** -0.5 * LOG2E
    for h in range(ATT_HEADS):
        r0 = h * ATT_HEAD_DIM
        t1, t2 = qt[r0:r0 + half], qt[r0 + half:r0 + ROPE_DIM]
        qt_ref[r0:r0 + half, :] = ((t1 * cos_r - t2 * sin_r) * qscale).astype(BF16)
        qt_ref[r0 + half:r0 + ROPE_DIM, :] = ((t2 * cos_r + t1 * sin_r) * qscale).astype(BF16)
        qt_ref[r0 + ROPE_DIM:r0 + ATT_HEAD_DIM, :] = (qt[r0 + ROPE_DIM:r0 + ATT_HEAD_DIM] * qscale).astype(BF16)

    lane = lax.broadcasted_iota(jnp.int32, (1, LANES), 1)
    d = lane % ATT_HEAD_DIM
    first_half = d < half
    in_rope = d < ROPE_DIM
    ang = posc_ref[...].astype(F32) * invfl_ref[...]
    cos_t = jnp.where(in_rope, jnp.cos(ang), 1.0)
    sin_t = jnp.where(first_half, -jnp.sin(ang), jnp.where(in_rope, jnp.sin(ang), 0.0))
    k = _rope(jnp.dot(a, wk_ref[...], preferred_element_type=F32), cos_t, sin_t, first_half).astype(BF16)
    k0_ref[...] = k[:, :ATT_HEAD_DIM]
    k1_ref[...] = k[:, ATT_HEAD_DIM:]
    vt_ref[...] = lax.dot_general(wvt_ref[...], a, NT_DIMS, preferred_element_type=F32).astype(BF16)

    gqk = jnp.dot(a, wgqk_ref[...], preferred_element_type=F32)
    gq_ref[...] = (gqk[:, :GLA_KEY_DIM] * (GLA_DK ** -0.5)).astype(BF16)
    gk_ref[...] = gqk[:, GLA_KEY_DIM:].astype(BF16)
    gv_ref[...] = jnp.dot(a, wgv_ref[...], preferred_element_type=F32).astype(BF16)
    gr_ref[...] = jnp.dot(a, wgr_ref[...], preferred_element_type=F32).astype(BF16)
    z_ref[...] = jnp.dot(a, wz_ref[...], preferred_element_type=F32)
    gates = jnp.dot(a, wgate_ref[...], preferred_element_type=F32)
    sga_ref[...] = _sigmoid(gates[:, :D_MODEL]).astype(BF16)
    sgg_ref[...] = _sigmoid(gates[:, D_MODEL:]).astype(BF16)


def _in_proj(x2, posc, posr, invfl, invfc, gain, wqt, wk, wvt, wgqk, wgv, wgr, wz, wgate, tm):
    T = x2.shape[0]
    row = lambda n: pl.BlockSpec((tm, n), lambda i: (i, 0))
    col = lambda n: pl.BlockSpec((n, tm), lambda i: (0, i))
    row_widths = (ATT_HEAD_DIM, ATT_HEAD_DIM, None, GLA_KEY_DIM, GLA_KEY_DIM, GLA_VAL_DIM,
                  GLA_VAL_DIM, wz.shape[1], D_MODEL, D_MODEL)
    row_dtypes = (BF16,) * 7 + (F32, BF16, BF16)
    out_specs = [col(ATT_Q_DIM)]
    out_shape = [jax.ShapeDtypeStruct((ATT_Q_DIM, T), BF16)]
    for n, dt in zip(row_widths, row_dtypes):
        if n is None:
            out_specs.append(col(ATT_KV_DIM))
            out_shape.append(jax.ShapeDtypeStruct((ATT_KV_DIM, T), BF16))
        else:
            out_specs.append(row(n))
            out_shape.append(jax.ShapeDtypeStruct((T, n), dt))
    consts = (invfl, invfc, gain, wqt, wk, wvt, wgqk, wgv, wgr, wz, wgate)
    return pl.pallas_call(
        _in_proj_kernel,
        grid=(T // tm,),
        in_specs=[row(D_MODEL), row(1), col(1)] + [_full(c.shape) for c in consts],
        out_specs=out_specs,
        out_shape=out_shape,
        compiler_params=_cparams(("parallel",), 56),
        name="in_proj",
    )(x2, posc, posr, *consts)


def _swa_kernel(sink_ref, qt_ref, k0_ref, k1_ref, vt_ref, o_ref, *, tq, seq):
    blk = WINDOW
    span = 3 * blk
    hd = ATT_HEAD_DIM
    n = pl.program_id(1)
    ones = jnp.ones((16, span), BF16)
    for sb in range(tq // blk):
        q0 = n * tq + sb * blk
        start = pl.multiple_of(jnp.clip(q0 - blk, 0, seq - span), blk)
        kj = start + lax.broadcasted_iota(jnp.int32, (span, blk), 0)
        qi = q0 + lax.broadcasted_iota(jnp.int32, (span, blk), 1)
        valid = jnp.abs(qi - kj) <= WINDOW
        outs = []
        for g, k_ref in enumerate((k0_ref, k1_ref)):
            kw = k_ref[0, pl.ds(start, span), :]
            vaug = jnp.concatenate([vt_ref[g * hd:(g + 1) * hd, pl.ds(start, span)], ones], axis=0)
            heads = range(g * ATT_GROUP, (g + 1) * ATT_GROUP)
            qs = jnp.concatenate([qt_ref[h * hd:(h + 1) * hd, sb * blk:(sb + 1) * blk] for h in heads], axis=1)
            s_all = jnp.dot(kw, qs, preferred_element_type=F32)
            for i, h in enumerate(heads):
                s = jnp.where(valid, s_all[:, i * blk:(i + 1) * blk], -jnp.inf)
                sink = sink_ref[h] * LOG2E
                m = jnp.maximum(jnp.max(s, axis=0, keepdims=True), sink)
                e = jnp.exp2(s - m).astype(BF16)
                r = jnp.dot(vaug, e, preferred_element_type=F32)
                den = r[hd:hd + 1] + jnp.exp2(sink - m)
                outs.append(r[:hd] / den)
        for pr in range(ATT_HEADS // 2):
            pair = jnp.concatenate([outs[2 * pr], outs[2 * pr + 1]], axis=0)
            o_ref[0, sb * blk:(sb + 1) * blk, pr * 2 * hd:(pr + 1) * 2 * hd] = pair.T.astype(BF16)


def _swa(sink, qt, k0, k1, vt, batch, tq):
    S = k0.shape[1]
    nq = S // tq
    kspec = pl.BlockSpec((1, S, ATT_HEAD_DIM), lambda b, n: (b, 0, 0))
    return pl.pallas_call(
        functools.partial(_swa_kernel, tq=tq, seq=S),
        grid=(batch, nq),
        in_specs=[pl.BlockSpec(memory_space=pltpu.SMEM),
                  pl.BlockSpec((ATT_Q_DIM, tq), lambda b, n: (0, b * nq + n)),
                  kspec, kspec,
                  pl.BlockSpec((ATT_KV_DIM, S), lambda b, n: (0, b))],
        out_specs=pl.BlockSpec((1, tq, ATT_Q_DIM), lambda b, n: (b, n, 0)),
        out_shape=jax.ShapeDtypeStruct((batch, S, ATT_Q_DIM), BF16),
        compiler_params=_cparams(("parallel", "parallel"), 32),
        name="swa",
    )(sink, qt, k0, k1, vt)


def _log_sigmoid(u):
    return jnp.minimum(u, 0.0) - jnp.log(1.0 + jnp.exp(-jnp.abs(u)))


def _split2(x):
    hi = x.astype(BF16)
    return hi, (x - hi.astype(F32)).astype(BF16)


GLA_WAYS = 4


def _gla_kernel(z_ref, q_ref, k_ref, v_ref, r_ref, upw_ref, bias_ref, gain_ref,
                o_ref, cf_ref, cb_ref, kef_ref, keb_ref, st_ref, s_ref, *, seq):
    L = GLA_CHUNK
    R2 = 2 * GLA_GATE_RANK
    nc = seq // L
    grp = 4 * L
    cpg = grp // L
    dk = GLA_DK
    mm = functools.partial(jnp.dot, preferred_element_type=F32)
    nt = functools.partial(lax.dot_general, dimension_numbers=NT_DIMS, preferred_element_type=F32)

    row = lax.broadcasted_iota(jnp.int32, (grp, grp), 0)
    col = lax.broadcasted_iota(jnp.int32, (grp, grp), 1)
    same = (row // L) == (col // L)
    fwd_mask = same & (col <= row)
    bwd_mask = same & (col > row)
    tri_lo = jnp.where(fwd_mask, 1.0, 0.0).astype(BF16)
    tri_up = jnp.where(same & (col >= row), 1.0, 0.0).astype(BF16)
    lane = lax.broadcasted_iota(jnp.int32, (1, LANES), 1)
    use_lo = (lane >= R2) & (lane < 2 * R2)

    def group_starts(i):
        return [pl.multiple_of((i * GLA_WAYS + w) * grp, grp) for w in range(GLA_WAYS)]

    def cum_body(i, carry):
        r0s = group_starts(i)
        zs = [_split2(z_ref[0, pl.ds(r0, grp), :]) for r0 in r0s]
        us = [mm(jnp.where(use_lo, zl, zh), upw_ref[...]) + bias_ref[...] for zh, zl in zs]
        las = [_split2(_log_sigmoid(u) * (1.0 / GLA_GATE_NORM)) for u in us]
        cfxs = [mm(tri_lo, jnp.concatenate([lh[:, :dk], ll[:, :dk]], axis=1)) for lh, ll in las]
        cbxs = [mm(tri_up, jnp.concatenate([lh[:, dk:], ll[:, dk:]], axis=1)) for lh, ll in las]
        for r0, cfx, cbx in zip(r0s, cfxs, cbxs):
            cf = cfx[:, :dk] + cfx[:, dk:]
            cb = cbx[:, :dk] + cbx[:, dk:]
            cf_ref[pl.ds(r0, grp), :] = cf
            cb_ref[pl.ds(r0, grp), :] = cb
            k = k_ref[0, pl.ds(r0, grp), :].astype(F32)
            for c in range(cpg):
                sl = slice(c * L, (c + 1) * L)
                gf = cf[(c + 1) * L - 1:(c + 1) * L]
                gb = cb[c * L:c * L + 1]
                kef_ref[pl.ds(r0 + c * L, L), :] = (k[sl] * jnp.exp(gf - cf[sl])).astype(BF16)
                keb_ref[pl.ds(r0 + c * L, L), :] = (k[sl] * jnp.exp(gb - cb[sl])).astype(BF16)
        return carry

    lax.fori_loop(0, seq // (grp * GLA_WAYS), cum_body, 0)

    s_ref[...] = jnp.zeros_like(s_ref)
    zero_k = jnp.zeros((L, dk), BF16)

    def state_body(i, carry):
        j = nc - 1 - i
        rf = pl.multiple_of(i * L, L)
        rb = pl.multiple_of(j * L, L)
        vcat = jnp.concatenate([v_ref[0, pl.ds(rf, L), :], v_ref[0, pl.ds(rb, L), :]], axis=0)
        kblk = jnp.concatenate([jnp.concatenate([kef_ref[pl.ds(rf, L), :], zero_k], axis=1),
                                jnp.concatenate([zero_k, keb_ref[pl.ds(rb, L), :]], axis=1)], axis=0)
        kv = lax.dot_general(vcat, kblk, TN_DIMS, preferred_element_type=F32)
        decay = jnp.exp(jnp.concatenate([cf_ref[pl.ds(rf + L - 1, 1), :], cb_ref[pl.ds(rb, 1), :]], axis=1))
        s = s_ref[...]
        st_ref[i, :, 0:dk] = s[:, :dk].astype(BF16)
        st_ref[j, :, dk:2 * dk] = s[:, dk:].astype(BF16)
        s_ref[...] = s * decay + kv
        return carry

    lax.fori_loop(0, nc, state_body, 0, unroll=16)

    def out_body(i, carry):
        r0s = group_starts(i)
        ops = []
        for r0 in r0s:
            q = q_ref[0, pl.ds(r0, grp), :].astype(F32)
            k = k_ref[0, pl.ds(r0, grp), :].astype(F32)
            cf = cf_ref[pl.ds(r0, grp), :]
            cb = cb_ref[pl.ds(r0, grp), :]
            ops.append(((q * jnp.exp(cf)).astype(BF16), (k * jnp.exp(-cf)).astype(BF16),
                        (q * jnp.exp(cb)).astype(BF16), (k * jnp.exp(-cb)).astype(BF16)))
        scores = [(nt(qf, kf), nt(qb, kb)) for qf, kf, qb, kb in ops]
        attns = [jnp.where(fwd_mask, af, jnp.where(bwd_mask, ab, 0.0)).astype(BF16) for af, ab in scores]
        outs = []
        for r0, attn, (qf, _, qb, _) in zip(r0s, attns, ops):
            c0 = r0 // L
            qcat = jnp.concatenate([qf, qb], axis=1)
            inter = jnp.concatenate([nt(qcat[c * L:(c + 1) * L], st_ref[c0 + c]) for c in range(cpg)], axis=0)
            outs.append(mm(attn, v_ref[0, pl.ds(r0, grp), :]) + inter)
        for r0, o in zip(r0s, outs):
            r = r_ref[0, pl.ds(r0, grp), :].astype(F32)
            o_ref[0, pl.ds(r0, grp), :] = (_rms(o, gain_ref[...]) * (r * _sigmoid(r))).astype(BF16)
        return carry

    lax.fori_loop(0, seq // (grp * GLA_WAYS), out_body, 0)


def _gla(z, gq, gk, gv, gr, upw, bias, gain):
    B, S, _ = gq.shape
    nc = S // GLA_CHUNK
    seq_blk = lambda n: pl.BlockSpec((1, S, n), lambda b, h: (b, 0, h))
    head_blk = lambda r, n: pl.BlockSpec((None, r, n), lambda b, h: (h, 0, 0))
    return pl.pallas_call(
        functools.partial(_gla_kernel, seq=S),
        grid=(B, GLA_HEADS),
        in_specs=[pl.BlockSpec((1, S, LANES), lambda b, h: (b, 0, 0)),
                  seq_blk(GLA_DK), seq_blk(GLA_DK), seq_blk(GLA_DV), seq_blk(GLA_DV),
                  head_blk(LANES, 2 * GLA_DK), head_blk(1, 2 * GLA_DK),
                  pl.BlockSpec((1, GLA_DV), lambda b, h: (0, h))],
        out_specs=seq_blk(GLA_DV),
        out_shape=jax.ShapeDtypeStruct((B, S, GLA_VAL_DIM), BF16),
        scratch_shapes=[pltpu.VMEM((S, GLA_DK), F32), pltpu.VMEM((S, GLA_DK), F32),
                        pltpu.VMEM((S, GLA_DK), BF16), pltpu.VMEM((S, GLA_DK), BF16),
                        pltpu.VMEM((nc, GLA_DV, 2 * GLA_DK), BF16),
                        pltpu.VMEM((GLA_DV, 2 * GLA_DK), F32)],
        compiler_params=_cparams(("parallel", "parallel"), 48),
        name="gla",
    )(z, gq, gk, gv, gr, upw, bias, gain)


def _mix_out_kernel(x_ref, a_ref, g_ref, sga_ref, sgg_ref, wa_ref, wb_ref, wo_ref, gain_ref, wr_ref,
                    h_ref, xn_ref, aff_ref):
    y_att = jnp.dot(a_ref[...], wa_ref[...], preferred_element_type=F32)
    y_gla = jnp.dot(g_ref[...], wb_ref[...], preferred_element_type=F32)
    merged = sga_ref[...].astype(F32) * y_att + sgg_ref[...].astype(F32) * y_gla
    h = x_ref[...] + jnp.dot(merged.astype(BF16), wo_ref[...], preferred_element_type=F32)
    h_ref[...] = h
    xn = _rms(h, gain_ref[...])
    _store_token_tiles(xn_ref, xn)
    hi = xn.astype(BF16)
    lo = (xn - hi.astype(F32)).astype(BF16)
    part = jnp.dot(jnp.concatenate([hi, lo], axis=1), wr_ref[...], preferred_element_type=F32)
    logits = part[:, :N_EXPERTS] + part[:, N_EXPERTS:]
    m = jnp.max(logits, axis=-1, keepdims=True)
    e = jnp.exp(logits - m)
    aff_ref[...] = e / jnp.sum(e, axis=-1, keepdims=True)


def _mix_out(x2, a, g, sga, sgg, wa, wb, wo, gain, wr, tm):
    T = x2.shape[0]
    row = lambda n: pl.BlockSpec((tm, n), lambda i: (i, 0))
    return pl.pallas_call(
        _mix_out_kernel,
        grid=(T // tm,),
        in_specs=[row(D_MODEL), row(ATT_Q_DIM), row(GLA_VAL_DIM), row(D_MODEL), row(D_MODEL),
                  _full(wa.shape), _full(wb.shape), _full(wo.shape), _full(gain.shape), _full(wr.shape)],
        out_specs=[row(D_MODEL), pl.BlockSpec((tm * ROW_TILES, LANES), lambda i: (i, 0)), row(N_EXPERTS)],
        out_shape=[jax.ShapeDtypeStruct((T, D_MODEL), F32), jax.ShapeDtypeStruct((T * ROW_TILES, LANES), F32),
                   jax.ShapeDtypeStruct((T, N_EXPERTS), F32)],
        compiler_params=_cparams(("parallel",), 48),
        name="mix_out",
    )(x2, a, g, sga, sgg, wa, wb, wo, gain, wr)


def _route_kernel(aff_ref, idx_ref, cum_ref, *, cap, seq):
    E = N_EXPERTS
    bits = lax.bitcast_convert_type(aff_ref[0], jnp.int32)
    count = lambda mask: jnp.sum(mask.astype(jnp.int32), axis=1, keepdims=True)

    def thr_body(t, thr):
        cand = thr | jnp.left_shift(jnp.int32(1), 30 - t)
        return jnp.where(count(bits >= cand) >= cap, cand, thr)

    thr = lax.fori_loop(0, 31, thr_body, jnp.zeros((E, 1), jnp.int32))
    above = bits > thr
    tie = bits == thr
    need = cap - count(above)

    pos = lax.broadcasted_iota(jnp.int32, (E, seq), 1)

    def tie_body(t, last):
        cand = last | jnp.left_shift(jnp.int32(1), (seq.bit_length() - 2) - t)
        return jnp.where(count(tie & (pos < cand)) < need, cand, last)

    last = lax.fori_loop(0, seq.bit_length() - 1, tie_body, jnp.zeros((E, 1), jnp.int32))
    sel = (above | (tie & (pos <= last))).astype(BF16)

    tri = (lax.broadcasted_iota(jnp.int32, (LANES, LANES), 0)
           <= lax.broadcasted_iota(jnp.int32, (LANES, LANES), 1)).astype(BF16)
    carry = jnp.zeros((E, 1), F32)
    for t in range(seq // LANES):
        local = jnp.dot(sel[:, t * LANES:(t + 1) * LANES], tri, preferred_element_type=F32) + carry
        cum_ref[:, t * LANES:(t + 1) * LANES] = local
        carry = local[:, LANES - 1:LANES]

    slot = lax.broadcasted_iota(jnp.int32, (cap, LANES), 0).astype(F32)
    for e in range(E):
        def cnt_body(t, acc):
            c = cum_ref[e:e + 1, pl.ds(pl.multiple_of(t * LANES, LANES), LANES)]
            return acc + jnp.where(c <= slot, 1.0, 0.0)

        acc = lax.fori_loop(0, seq // LANES, cnt_body, jnp.zeros((cap, LANES), F32))
        idx_ref[0, :, e:e + 1] = jnp.sum(acc, axis=1, keepdims=True).astype(jnp.int32)


def _route(aff_t, cap):
    B, E, S = aff_t.shape
    return pl.pallas_call(
        functools.partial(_route_kernel, cap=cap, seq=S),
        grid=(B,),
        in_specs=[pl.BlockSpec((1, E, S), lambda b: (b, 0, 0))],
        out_specs=pl.BlockSpec((1, cap, E), lambda b: (b, 0, 0)),
        out_shape=jax.ShapeDtypeStruct((B, cap, E), jnp.int32),
        scratch_shapes=[pltpu.VMEM((E, S), F32)],
        compiler_params=_cparams(("parallel",), 32),
        name="route",
    )(aff_t)


def _gather_kernel(idx_ref, xn_ref, aff_ref, xg_ref, wg_ref, xs_ref, ws_ref, *, cap):
    b = pl.program_id(0)
    e = pl.program_id(1)
    base = (b * N_EXPERTS + e) * cap

    def body(i, carry):
        t = idx_ref[base + i]
        xs_ref[pl.ds(pl.multiple_of(i * SUBLANES, SUBLANES), SUBLANES), :] = xn_ref[0, t]
        ws_ref[pl.ds(i, 1), :] = aff_ref[0, pl.ds(t, 1), :]
        return carry

    lax.fori_loop(0, cap, body, 0, unroll=8)
    xg_ref[0, 0] = _load_token_tiles(xs_ref, cap).astype(BF16)
    lane = lax.broadcasted_iota(jnp.int32, (cap, N_EXPERTS), 1)
    wg_ref[0, 0] = jnp.sum(jnp.where(lane == e, ws_ref[...], 0.0), axis=1, keepdims=True)


def _gather(idx_flat, xn_tiles, aff, cap):
    B, S = xn_tiles.shape[:2]
    E = N_EXPERTS
    grid_spec = pltpu.PrefetchScalarGridSpec(
        num_scalar_prefetch=1,
        grid=(B, E),
        in_specs=[pl.BlockSpec((1, S, SUBLANES, LANES), lambda b, e, idx: (b, 0, 0, 0)),
                  pl.BlockSpec((1, S, E), lambda b, e, idx: (b, 0, 0))],
        out_specs=[pl.BlockSpec((1, 1, cap, D_MODEL), lambda b, e, idx: (b, e, 0, 0)),
                   pl.BlockSpec((1, 1, cap, 1), lambda b, e, idx: (b, e, 0, 0))],
        scratch_shapes=[pltpu.VMEM((cap * SUBLANES, LANES), F32), pltpu.VMEM((cap, E), F32)],
    )
    return pl.pallas_call(
        functools.partial(_gather_kernel, cap=cap),
        grid_spec=grid_spec,
        out_shape=[jax.ShapeDtypeStruct((B, E, cap, D_MODEL), BF16),
                   jax.ShapeDtypeStruct((B, E, cap, 1), F32)],
        compiler_params=_cparams(("arbitrary", "arbitrary"), 48),
        name="gather",
    )(idx_flat, xn_tiles, aff)


def _ffn_kernel(xg_ref, wg_ref, w1_ref, w2_ref, w3_ref, y_ref):
    xg = xg_ref[0, 0]
    gate = jnp.dot(xg, w1_ref[0], preferred_element_type=F32)
    up = jnp.dot(xg, w2_ref[0], preferred_element_type=F32)
    hid = (gate * _sigmoid(gate) * up).astype(BF16)
    y = jnp.dot(hid, w3_ref[0], preferred_element_type=F32) * wg_ref[0, 0]
    _store_token_tiles(y_ref.at[0, 0], y)


def _ffn(xg, wg, w1, w2, w3):
    B, E, C, D = xg.shape
    F = w1.shape[-1]
    return pl.pallas_call(
        _ffn_kernel,
        grid=(E, B),
        in_specs=[pl.BlockSpec((1, 1, C, D), lambda e, b: (b, e, 0, 0)),
                  pl.BlockSpec((1, 1, C, 1), lambda e, b: (b, e, 0, 0)),
                  pl.BlockSpec((1, D, F), lambda e, b: (e, 0, 0)),
                  pl.BlockSpec((1, D, F), lambda e, b: (e, 0, 0)),
                  pl.BlockSpec((1, F, D), lambda e, b: (e, 0, 0))],
        out_specs=pl.BlockSpec((1, 1, C * ROW_TILES, LANES), lambda e, b: (b, e, 0, 0)),
        out_shape=jax.ShapeDtypeStruct((B, E, C * ROW_TILES, LANES), F32),
        compiler_params=_cparams(("parallel", "parallel"), 48),
        name="ffn",
    )(xg, wg, w1, w2, w3)


SCATTER_BATCH = 16


def _scatter_kernel(idx_ref, y_ref, o_ref, *, cap):
    b = pl.program_id(0)
    e = pl.program_id(1)
    base = (b * N_EXPERTS + e) * cap

    @pl.when(e == 0)
    def _():
        o_ref[...] = jnp.zeros_like(o_ref)

    def body(blk, carry):
        i0 = blk * SCATTER_BATCH
        ts = [idx_ref[base + i0 + u] for u in range(SCATTER_BATCH)]
        new = [o_ref[0, ts[u]] + y_ref[0, 0, i0 + u] for u in range(SCATTER_BATCH)]
        for u in range(SCATTER_BATCH):
            o_ref[0, ts[u]] = new[u]
        return carry

    lax.fori_loop(0, cap // SCATTER_BATCH, body, 0)


def _scatter(idx_flat, y_tiles, seq):
    B, E, C = y_tiles.shape[:3]
    grid_spec = pltpu.PrefetchScalarGridSpec(
        num_scalar_prefetch=1,
        grid=(B, E),
        in_specs=[pl.BlockSpec((1, 1, C, SUBLANES, LANES), lambda b, e, idx: (b, e, 0, 0, 0))],
        out_specs=pl.BlockSpec((1, seq, SUBLANES, LANES), lambda b, e, idx: (b, 0, 0, 0)),
    )
    return pl.pallas_call(
        functools.partial(_scatter_kernel, cap=C),
        grid_spec=grid_spec,
        out_shape=jax.ShapeDtypeStruct((B, seq, SUBLANES, LANES), F32),
        compiler_params=_cparams(("arbitrary", "arbitrary"), 48),
        name="scatter",
    )(idx_flat, y_tiles)


def _ple_out_kernel(h_ref, moe_ref, p_ref, gple_ref, wpg_ref, wple_ref, gfin_ref, o_ref):
    h = h_ref[...] + _load_token_tiles(moe_ref, h_ref.shape[0])
    n = _rms(h, gple_ref[...]).astype(BF16)
    gate = _sigmoid(jnp.dot(n, wpg_ref[...], preferred_element_type=F32))
    emb = jnp.dot(p_ref[...].astype(BF16), wple_ref[...], preferred_element_type=F32)
    o_ref[...] = _rms(h + gate * emb, gfin_ref[...])


def _ple_out(h, moe, p2, gple, wpg, wple, gfin, tm):
    T = h.shape[0]
    row = lambda n: pl.BlockSpec((tm, n), lambda i: (i, 0))
    return pl.pallas_call(
        _ple_out_kernel,
        grid=(T // tm,),
        in_specs=[row(D_MODEL), pl.BlockSpec((tm * ROW_TILES, LANES), lambda i: (i, 0)), row(PLE_DIM),
                  _full(gple.shape), _full(wpg.shape), _full(wple.shape), _full(gfin.shape)],
        out_specs=row(D_MODEL),
        out_shape=jax.ShapeDtypeStruct((T, D_MODEL), F32),
        compiler_params=_cparams(("parallel",), 48),
        name="ple_out",
    )(h, moe, p2, gple, wpg, wple, gfin)


def kernel(x, p, positions, norm_mix, w_in, gla_gate_up_fwd, gla_gate_bias_fwd, gla_gate_up_bwd, gla_gate_bias_bwd, attn_sink, gla_norm, w_branch_attn, w_branch_gla, w_out, norm_ffn, w_router, w_exp_gate, w_exp_up, w_exp_down, norm_ple, w_ple_gate, w_ple, norm_final):
    B, S, D = x.shape
    T = B * S
    depth = w_in.shape[0]
    cap = CAPACITY_FACTOR * S // N_EXPERTS
    R = GLA_GATE_RANK

    posc = positions.reshape(T, 1)
    posr = positions.reshape(1, T)
    inv_freq = ROPE_THETA ** (-jnp.arange(0, ROPE_DIM, 2, dtype=F32) / ROPE_DIM)
    invfl = jnp.tile(inv_freq, LANES // (ROPE_DIM // 2)).reshape(1, LANES)
    invfc = inv_freq.reshape(ROPE_DIM // 2, 1)

    h = x.reshape(T, D)
    for l in range(depth):
        o = 0
        cols = {}
        for name, n in (("q", ATT_Q_DIM), ("k", ATT_KV_DIM), ("v", ATT_KV_DIM), ("gqk", 2 * GLA_KEY_DIM),
                        ("gv", GLA_VAL_DIM), ("gr", GLA_VAL_DIM), ("z", 2 * R), ("gate", 2 * D_MODEL)):
            cols[name] = w_in[l][:, o:o + n].astype(BF16)
            o += n
        per_head = lambda w: w.reshape(-1, GLA_HEADS, GLA_DK).swapaxes(0, 1)
        upf, upb = per_head(gla_gate_up_fwd[l]), per_head(gla_gate_up_bwd[l])
        up = jnp.concatenate([jnp.concatenate([upf, jnp.zeros_like(upf)], axis=2),
                              jnp.concatenate([jnp.zeros_like(upb), upb], axis=2)], axis=1)
        up_hi = up.astype(BF16)
        up_lo = (up - up_hi.astype(F32)).astype(BF16)
        upw = jnp.concatenate([up_hi, up_hi, up_lo, jnp.zeros_like(up_lo)], axis=1)
        wz4 = jnp.tile(cols["z"], (1, LANES // (2 * R)))
        gbias = jnp.concatenate([per_head(gla_gate_bias_fwd[l]), per_head(gla_gate_bias_bwd[l])], axis=2)
        wr = w_router[l]
        wr_hi = wr.astype(BF16)
        wr_lo = (wr - wr_hi.astype(F32)).astype(BF16)
        wr2 = jnp.concatenate([jnp.concatenate([wr_hi, wr_lo], axis=1),
                               jnp.concatenate([wr_hi, jnp.zeros_like(wr_lo)], axis=1)], axis=0)

        qt, k0, k1, vt, gq, gk, gv, gr, z, sga, sgg = _in_proj(
            h, posc, posr, invfl, invfc, norm_mix[l].reshape(1, D), cols["q"].T, cols["k"], cols["v"].T,
            cols["gqk"], cols["gv"], cols["gr"], wz4, cols["gate"], tm=256)

        att = _swa(attn_sink[l], qt, k0.reshape(B, S, -1), k1.reshape(B, S, -1), vt, batch=B, tq=512)
        gla = _gla(z.reshape(B, S, -1), gq.reshape(B, S, -1), gk.reshape(B, S, -1), gv.reshape(B, S, -1),
                   gr.reshape(B, S, -1), upw, gbias, gla_norm[l].reshape(1, -1))

        h1, xn, aff = _mix_out(h, att.reshape(T, -1), gla.reshape(T, -1), sga, sgg,
                               w_branch_attn[l].astype(BF16), w_branch_gla[l].astype(BF16),
                               w_out[l].astype(BF16), norm_ffn[l].reshape(1, D), wr2, tm=256)

        aff3 = aff.reshape(B, S, N_EXPERTS)
        idx = _route(jnp.swapaxes(aff3, 1, 2), cap)
        idx_flat = jnp.swapaxes(idx, 1, 2).reshape(-1)
        xg, wg = _gather(idx_flat, xn.reshape(B, S, SUBLANES, LANES), aff3, cap)
        y = _ffn(xg, wg, w_exp_gate[l].astype(BF16), w_exp_up[l].astype(BF16), w_exp_down[l].astype(BF16))
        moe = _scatter(idx_flat, y.reshape(B, N_EXPERTS, cap, SUBLANES, LANES), S)

        last = l == depth - 1
        gfin = norm_final.reshape(1, D)
        assert last, "the final norm is fused into the last layer's PLE kernel"
        h = _ple_out(h1, moe.reshape(T * ROW_TILES, LANES), p[l].reshape(T, PLE_DIM), norm_ple[l].reshape(1, D),
                     w_ple_gate[l].astype(BF16), w_ple[l].astype(BF16), gfin, tm=256)
    return h.reshape(B, S, D)
```

```python
import functools
import math

import jax
import jax.numpy as jnp
from jax import lax
from jax.experimental import pallas as pl
from jax.experimental.pallas import tpu as pltpu

D_MODEL = 1024
ATT_HEADS = 8
ATT_KV_HEADS = 2
ATT_HEAD_DIM = 64
ATT_GROUP = ATT_HEADS // ATT_KV_HEADS
ATT_Q_DIM = ATT_HEADS * ATT_HEAD_DIM
ATT_KV_DIM = ATT_KV_HEADS * ATT_HEAD_DIM
WINDOW = 128
ROPE_DIM = ATT_HEAD_DIM // 4
ROPE_THETA = 500000.0
GLA_HEADS = 4
GLA_KEY_DIM = D_MODEL // 2
GLA_VAL_DIM = D_MODEL
GLA_DK = GLA_KEY_DIM // GLA_HEADS
GLA_DV = GLA_VAL_DIM // GLA_HEADS
GLA_GATE_RANK = 16
GLA_GATE_NORM = 16.0
GLA_CHUNK = 64
N_EXPERTS = 16
EXPERT_FF = D_MODEL
CAPACITY_FACTOR = 2
PLE_DIM = 256
EPS = 1e-6

LANES = 128
MIB = 1024 * 1024
BF16 = jnp.bfloat16
F32 = jnp.float32
LOG2E = math.log2(math.e)

NT_DIMS = (((1,), (1,)), ((), ()))
TN_DIMS = (((0,), (0,)), ((), ()))


def _cparams(sem, vmem_mib):
    return pltpu.CompilerParams(dimension_semantics=sem, vmem_limit_bytes=vmem_mib * MIB)


def _full(shape):
    n = len(shape)
    return pl.BlockSpec(shape, lambda *_: (0,) * n)


def _rms(x, gain):
    ms = jnp.mean(x * x, axis=-1, keepdims=True)
    return x * lax.rsqrt(ms + EPS) * gain


def _sigmoid(x):
    return 1.0 / (1.0 + jnp.exp(-x))


SUBLANES = 8
ROW_TILES = D_MODEL // LANES


def _store_token_tiles(ref2d, x):
    rows = x.shape[0]
    for j in range(ROW_TILES):
        ref2d[pl.ds(j, rows, stride=ROW_TILES), :] = x[:, j * LANES:(j + 1) * LANES]


def _load_token_tiles(ref2d, rows):
    return jnp.concatenate([ref2d[pl.ds(j, rows, stride=ROW_TILES), :] for j in range(ROW_TILES)], axis=1)


def _rope(t, cos_t, sin_t, first_half):
    fwd = pltpu.roll(t, LANES - ROPE_DIM // 2, axis=1)
    bwd = pltpu.roll(t, ROPE_DIM // 2, axis=1)
    return t * cos_t + jnp.where(first_half, fwd, bwd) * sin_t


def _in_proj_kernel(x_ref, posc_ref, posr_ref, invfl_ref, invfc_ref, gain_ref, wqt_ref, wk_ref, wvt_ref,
                    wgqk_ref, wgv_ref, wgr_ref, wz_ref, wgate_ref,
                    qt_ref, k0_ref, k1_ref, vt_ref, gq_ref, gk_ref, gv_ref, gr_ref, z_ref, sga_ref, sgg_ref):
    a = _rms(x_ref[...], gain_ref[...]).astype(BF16)
    half = ROPE_DIM // 2

    qt = lax.dot_general(wqt_ref[...], a, NT_DIMS, preferred_element_type=F32)
    ang_t = invfc_ref[...] * posr_ref[...].astype(F32)
    cos_r, sin_r = jnp.cos(ang_t), jnp.sin(ang_t)
    qscale = ATT_HEAD_DIM ** -0.5 * LOG2E
    for h in range(ATT_HEADS):
        r0 = h * ATT_HEAD_DIM
        t1, t2 = qt[r0:r0 + half], qt[r0 + half:r0 + ROPE_DIM]
        qt_ref[r0:r0 + half, :] = ((t1 * cos_r - t2 * sin_r) * qscale).astype(BF16)
        qt_ref[r0 + half:r0 + ROPE_DIM, :] = ((t2 * cos_r + t1 * sin_r) * qscale).astype(BF16)
        qt_ref[r0 + ROPE_DIM:r0 + ATT_HEAD_DIM, :] = (qt[r0 + ROPE_DIM:r0 + ATT_HEAD_DIM] * qscale).astype(BF16)

    lane = lax.broadcasted_iota(jnp.int32, (1, LANES), 1)
    d = lane % ATT_HEAD_DIM
    first_half = d < half
    in_rope = d < ROPE_DIM
    ang = posc_ref[...].astype(F32) * invfl_ref[...]
    cos_t = jnp.where(in_rope, jnp.cos(ang), 1.0)
    sin_t = jnp.where(first_half, -jnp.sin(ang), jnp.where(in_rope, jnp.sin(ang), 0.0))
    k = _rope(jnp.dot(a, wk_ref[...], preferred_element_type=F32), cos_t, sin_t, first_half).astype(BF16)
    k0_ref[...] = k[:, :ATT_HEAD_DIM]
    k1_ref[...] = k[:, ATT_HEAD_DIM:]
    vt_ref[...] = lax.dot_general(wvt_ref[...], a, NT_DIMS, preferred_element_type=F32).astype(BF16)

    gqk = jnp.dot(a, wgqk_ref[...], preferred_element_type=F32)
    gq_ref[...] = (gqk[:, :GLA_KEY_DIM] * (GLA_DK ** -0.5)).astype(BF16)
    gk_ref[...] = gqk[:, GLA_KEY_DIM:].astype(BF16)
    gv_ref[...] = jnp.dot(a, wgv_ref[...], preferred_element_type=F32).astype(BF16)
    gr_ref[...] = jnp.dot(a, wgr_ref[...], preferred_element_type=F32).astype(BF16)
    z_ref[...] = jnp.dot(a, wz_ref[...], preferred_element_type=F32)
    gates = jnp.dot(a, wgate_ref[...], preferred_element_type=F32)
    sga_ref[...] = _sigmoid(gates[:, :D_MODEL]).astype(BF16)
    sgg_ref[...] = _sigmoid(gates[:, D_MODEL:]).astype(BF16)


def _in_proj(x2, posc, posr, invfl, invfc, gain, wqt, wk, wvt, wgqk, wgv, wgr, wz, wgate, tm):
    T = x2.shape[0]
    row = lambda n: pl.BlockSpec((tm, n), lambda i: (i, 0))
    col = lambda n: pl.BlockSpec((n, tm), lambda i: (0, i))
    row_widths = (ATT_HEAD_DIM, ATT_HEAD_DIM, None, GLA_KEY_DIM, GLA_KEY_DIM, GLA_VAL_DIM,
                  GLA_VAL_DIM, wz.shape[1], D_MODEL, D_MODEL)
    row_dtypes = (BF16,) * 7 + (F32, BF16, BF16)
    out_specs = [col(ATT_Q_DIM)]
    out_shape = [jax.ShapeDtypeStruct((ATT_Q_DIM, T), BF16)]
    for n, dt in zip(row_widths, row_dtypes):
        if n is None:
            out_specs.append(col(ATT_KV_DIM))
            out_shape.append(jax.ShapeDtypeStruct((ATT_KV_DIM, T), BF16))
        else:
            out_specs.append(row(n))
            out_shape.append(jax.ShapeDtypeStruct((T, n), dt))
    consts = (invfl, invfc, gain, wqt, wk, wvt, wgqk, wgv, wgr, wz, wgate)
    return pl.pallas_call(
        _in_proj_kernel,
        grid=(T // tm,),
        in_specs=[row(D_MODEL), row(1), col(1)] + [_full(c.shape) for c in consts],
        out_specs=out_specs,
        out_shape=out_shape,
        compiler_params=_cparams(("parallel",), 56),
        name="in_proj",
    )(x2, posc, posr, *consts)


def _swa_kernel(sink_ref, qt_ref, k0_ref, k1_ref, vt_ref, o_ref, *, tq, seq):
    blk = WINDOW
    span = 3 * blk
    hd = ATT_HEAD_DIM
    n = pl.program_id(1)
    ones = jnp.ones((16, span), BF16)
    for sb in range(tq // blk):
        q0 = n * tq + sb * blk
        start = pl.multiple_of(jnp.clip(q0 - blk, 0, seq - span), blk)
        kj = start + lax.broadcasted_iota(jnp.int32, (span, blk), 0)
        qi = q0 + lax.broadcasted_iota(jnp.int32, (span, blk), 1)
        valid = jnp.abs(qi - kj) <= WINDOW
        outs = []
        for g, k_ref in enumerate((k0_ref, k1_ref)):
            kw = k_ref[0, pl.ds(start, span), :]
            vaug = jnp.concatenate([vt_ref[g * hd:(g + 1) * hd, pl.ds(start, span)], ones], axis=0)
            heads = range(g * ATT_GROUP, (g + 1) * ATT_GROUP)
            qs = jnp.concatenate([qt_ref[h * hd:(h + 1) * hd, sb * blk:(sb + 1) * blk] for h in heads], axis=1)
            s_all = jnp.dot(kw, qs, preferred_element_type=F32)
            for i, h in enumerate(heads):
                s = jnp.where(valid, s_all[:, i * blk:(i + 1) * blk], -jnp.inf)
                sink = sink_ref[h] * LOG2E
                m = jnp.maximum(jnp.max(s, axis=0, keepdims=True), sink)
                e = jnp.exp2(s - m).astype(BF16)
                r = jnp.dot(vaug, e, preferred_element_type=F32)
                den = r[hd:hd + 1] + jnp.exp2(sink - m)
                outs.append(r[:hd] / den)
        for pr in range(ATT_HEADS // 2):
            pair = jnp.concatenate([outs[2 * pr], outs[2 * pr + 1]], axis=0)
            o_ref[0, sb * blk:(sb + 1) * blk, pr * 2 * hd:(pr + 1) * 2 * hd] = pair.T.astype(BF16)


def _swa(sink, qt, k0, k1, vt, batch, tq):
    S = k0.shape[1]
    nq = S // tq
    kspec = pl.BlockSpec((1, S, ATT_HEAD_DIM), lambda b, n: (b, 0, 0))
    return pl.pallas_call(
        functools.partial(_swa_kernel, tq=tq, seq=S),
        grid=(batch, nq),
        in_specs=[pl.BlockSpec(memory_space=pltpu.SMEM),
                  pl.BlockSpec((ATT_Q_DIM, tq), lambda b, n: (0, b * nq + n)),
                  kspec, kspec,
                  pl.BlockSpec((ATT_KV_DIM, S), lambda b, n: (0, b))],
        out_specs=pl.BlockSpec((1, tq, ATT_Q_DIM), lambda b, n: (b, n, 0)),
        out_shape=jax.ShapeDtypeStruct((batch, S, ATT_Q_DIM), BF16),
        compiler_params=_cparams(("parallel", "parallel"), 32),
        name="swa",
    )(sink, qt, k0, k1, vt)


def _log_sigmoid(u):
    return jnp.minimum(u, 0.0) - jnp.log(1.0 + jnp.exp(-jnp.abs(u)))


def _split2(x):
    hi = x.astype(BF16)
    return hi, (x - hi.astype(F32)).astype(BF16)


GLA_WAYS = 4


def _gla_kernel(z_ref, q_ref, k_ref, v_ref, r_ref, upw_ref, bias_ref, gain_ref,
                o_ref, cf_ref, cb_ref, kef_ref, keb_ref, st_ref, s_ref, *, seq):
    L = GLA_CHUNK
    R2 = 2 * GLA_GATE_RANK
    nc = seq // L
    grp = 4 * L
    cpg = grp // L
    dk = GLA_DK
    mm = functools.partial(jnp.dot, preferred_element_type=F32)
    nt = functools.partial(lax.dot_general, dimension_numbers=NT_DIMS, preferred_element_type=F32)

    row = lax.broadcasted_iota(jnp.int32, (grp, grp), 0)
    col = lax.broadcasted_iota(jnp.int32, (grp, grp), 1)
    same = (row // L) == (col // L)
    fwd_mask = same & (col <= row)
    bwd_mask = same & (col > row)
    tri_lo = jnp.where(fwd_mask, 1.0, 0.0).astype(BF16)
    tri_up = jnp.where(same & (col >= row), 1.0, 0.0).astype(BF16)
    lane = lax.broadcasted_iota(jnp.int32, (1, LANES), 1)
    use_lo = (lane >= R2) & (lane < 2 * R2)

    def group_starts(i):
        return [pl.multiple_of((i * GLA_WAYS + w) * grp, grp) for w in range(GLA_WAYS)]

    def cum_body(i, carry):
        r0s = group_starts(i)
        zs = [_split2(z_ref[0, pl.ds(r0, grp), :]) for r0 in r0s]
        us = [mm(jnp.where(use_lo, zl, zh), upw_ref[...]) + bias_ref[...] for zh, zl in zs]
        las = [_split2(_log_sigmoid(u) * (1.0 / GLA_GATE_NORM)) for u in us]
        cfxs = [mm(tri_lo, jnp.concatenate([lh[:, :dk], ll[:, :dk]], axis=1)) for lh, ll in las]
        cbxs = [mm(tri_up, jnp.concatenate([lh[:, dk:], ll[:, dk:]], axis=1)) for lh, ll in las]
        for r0, cfx, cbx in zip(r0s, cfxs, cbxs):
            cf = cfx[:, :dk] + cfx[:, dk:]
            cb = cbx[:, :dk] + cbx[:, dk:]
            cf_ref[pl.ds(r0, grp), :] = cf
            cb_ref[pl.ds(r0, grp), :] = cb
            k = k_ref[0, pl.ds(r0, grp), :].astype(F32)
            for c in range(cpg):
                sl = slice(c * L, (c + 1) * L)
                gf = cf[(c + 1) * L - 1:(c + 1) * L]
                gb = cb[c * L:c * L + 1]
                kef_ref[pl.ds(r0 + c * L, L), :] = (k[sl] * jnp.exp(gf - cf[sl])).astype(BF16)
                keb_ref[pl.ds(r0 + c * L, L), :] = (k[sl] * jnp.exp(gb - cb[sl])).astype(BF16)
        return carry

    lax.fori_loop(0, seq // (grp * GLA_WAYS), cum_body, 0)

    s_ref[...] = jnp.zeros_like(s_ref)
    zero_k = jnp.zeros((L, dk), BF16)

    def state_body(i, carry):
        j = nc - 1 - i
        rf = pl.multiple_of(i * L, L)
        rb = pl.multiple_of(j * L, L)
        vcat = jnp.concatenate([v_ref[0, pl.ds(rf, L), :], v_ref[0, pl.ds(rb, L), :]], axis=0)
        kblk = jnp.concatenate([jnp.concatenate([kef_ref[pl.ds(rf, L), :], zero_k], axis=1),
                                jnp.concatenate([zero_k, keb_ref[pl.ds(rb, L), :]], axis=1)], axis=0)
        kv = lax.dot_general(vcat, kblk, TN_DIMS, preferred_element_type=F32)
        decay = jnp.exp(jnp.concatenate([cf_ref[pl.ds(rf + L - 1, 1), :], cb_ref[pl.ds(rb, 1), :]], axis=1))
        s = s_ref[...]
        st_ref[i, :, 0:dk] = s[:, :dk].astype(BF16)
        st_ref[j, :, dk:2 * dk] = s[:, dk:].astype(BF16)
        s_ref[...] = s * decay + kv
        return carry

    lax.fori_loop(0, nc, state_body, 0, unroll=16)

    def out_body(i, carry):
        r0s = group_starts(i)
        ops = []
        for r0 in r0s:
            q = q_ref[0, pl.ds(r0, grp), :].astype(F32)
            k = k_ref[0, pl.ds(r0, grp), :].astype(F32)
            cf = cf_ref[pl.ds(r0, grp), :]
            cb = cb_ref[pl.ds(r0, grp), :]
            ops.append(((q * jnp.exp(cf)).astype(BF16), (k * jnp.exp(-cf)).astype(BF16),
                        (q * jnp.exp(cb)).astype(BF16), (k * jnp.exp(-cb)).astype(BF16)))
        scores = [(nt(qf, kf), nt(qb, kb)) for qf, kf, qb, kb in ops]
        attns = [jnp.where(fwd_mask, af, jnp.where(bwd_mask, ab, 0.0)).astype(BF16) for af, ab in scores]
        outs = []
        for r0, attn, (qf, _, qb, _) in zip(r0s, attns, ops):
            c0 = r0 // L
            qcat = jnp.concatenate([qf, qb], axis=1)
            inter = jnp.concatenate([nt(qcat[c * L:(c + 1) * L], st_ref[c0 + c]) for c in range(cpg)], axis=0)
            outs.append(mm(attn, v_ref[0, pl.ds(r0, grp), :]) + inter)
        for r0, o in zip(r0s, outs):
            r = r_ref[0, pl.ds(r0, grp), :].astype(F32)
            o_ref[0, pl.ds(r0, grp), :] = (_rms(o, gain_ref[...]) * (r * _sigmoid(r))).astype(BF16)
        return carry

    lax.fori_loop(0, seq // (grp * GLA_WAYS), out_body, 0)


def _gla(z, gq, gk, gv, gr, upw, bias, gain):
    B, S, _ = gq.shape
    nc = S // GLA_CHUNK
    seq_blk = lambda n: pl.BlockSpec((1, S, n), lambda b, h: (b, 0, h))
    head_blk = lambda r, n: pl.BlockSpec((None, r, n), lambda b, h: (h, 0, 0))
    return pl.pallas_call(
        functools.partial(_gla_kernel, seq=S),
        grid=(B, GLA_HEADS),
        in_specs=[pl.BlockSpec((1, S, LANES), lambda b, h: (b, 0, 0)),
                  seq_blk(GLA_DK), seq_blk(GLA_DK), seq_blk(GLA_DV), seq_blk(GLA_DV),
                  head_blk(LANES, 2 * GLA_DK), head_blk(1, 2 * GLA_DK),
                  pl.BlockSpec((1, GLA_DV), lambda b, h: (0, h))],
        out_specs=seq_blk(GLA_DV),
        out_shape=jax.ShapeDtypeStruct((B, S, GLA_VAL_DIM), BF16),
        scratch_shapes=[pltpu.VMEM((S, GLA_DK), F32), pltpu.VMEM((S, GLA_DK), F32),
                        pltpu.VMEM((S, GLA_DK), BF16), pltpu.VMEM((S, GLA_DK), BF16),
                        pltpu.VMEM((nc, GLA_DV, 2 * GLA_DK), BF16),
                        pltpu.VMEM((GLA_DV, 2 * GLA_DK), F32)],
        compiler_params=_cparams(("parallel", "parallel"), 48),
        name="gla",
    )(z, gq, gk, gv, gr, upw, bias, gain)


def _mix_out_kernel(x_ref, a_ref, g_ref, sga_ref, sgg_ref, wa_ref, wb_ref, wo_ref, gain_ref, wr_ref,
                    h_ref, xn_ref, aff_ref):
    y_att = jnp.dot(a_ref[...], wa_ref[...], preferred_element_type=F32)
    y_gla = jnp.dot(g_ref[...], wb_ref[...], preferred_element_type=F32)
    merged = sga_ref[...].astype(F32) * y_att + sgg_ref[...].astype(F32) * y_gla
    h = x_ref[...] + jnp.dot(merged.astype(BF16), wo_ref[...], preferred_element_type=F32)
    h_ref[...] = h
    xn = _rms(h, gain_ref[...])
    _store_token_tiles(xn_ref, xn)
    hi = xn.astype(BF16)
    lo = (xn - hi.astype(F32)).astype(BF16)
    part = jnp.dot(jnp.concatenate([hi, lo], axis=1), wr_ref[...], preferred_element_type=F32)
    logits = part[:, :N_EXPERTS] + part[:, N_EXPERTS:]
    m = jnp.max(logits, axis=-1, keepdims=True)
    e = jnp.exp(logits - m)
    aff_ref[...] = e / jnp.sum(e, axis=-1, keepdims=True)


def _mix_out(x2, a, g, sga, sgg, wa, wb, wo, gain, wr, tm):
    T = x2.shape[0]
    row = lambda n: pl.BlockSpec((tm, n), lambda i: (i, 0))
    return pl.pallas_call(
        _mix_out_kernel,
        grid=(T // tm,),
        in_specs=[row(D_MODEL), row(ATT_Q_DIM), row(GLA_VAL_DIM), row(D_MODEL), row(D_MODEL),
                  _full(wa.shape), _full(wb.shape), _full(wo.shape), _full(gain.shape), _full(wr.shape)],
        out_specs=[row(D_MODEL), pl.BlockSpec((tm * ROW_TILES, LANES), lambda i: (i, 0)), row(N_EXPERTS)],
        out_shape=[jax.ShapeDtypeStruct((T, D_MODEL), F32), jax.ShapeDtypeStruct((T * ROW_TILES, LANES), F32),
                   jax.ShapeDtypeStruct((T, N_EXPERTS), F32)],
        compiler_params=_cparams(("parallel",), 48),
        name="mix_out",
    )(x2, a, g, sga, sgg, wa, wb, wo, gain, wr)


ROUTE_WAYS = 4


def _route_kernel(aff_ref, idx_ref, cum_ref, *, cap, seq):
    E = N_EXPERTS
    aff = aff_ref[0]
    count = lambda mask: jnp.sum(mask.astype(jnp.int32), axis=1, keepdims=True)
    as_float = lambda pattern: lax.bitcast_convert_type(pattern, F32)

    def thr_body(t, pattern):
        cand = pattern | jnp.left_shift(jnp.int32(1), 30 - t)
        return jnp.where(count(aff >= as_float(cand)) >= cap, cand, pattern)

    thr = as_float(lax.fori_loop(0, 31, thr_body, jnp.zeros((E, 1), jnp.int32)))
    above = aff > thr
    tie = aff == thr
    need = cap - count(above)

    pos = lax.broadcasted_iota(jnp.int32, (E, seq), 1)

    def tie_body(t, last):
        cand = last | jnp.left_shift(jnp.int32(1), (seq.bit_length() - 2) - t)
        return jnp.where(count(tie & (pos < cand)) < need, cand, last)

    last = lax.fori_loop(0, seq.bit_length() - 1, tie_body, jnp.zeros((E, 1), jnp.int32))
    sel = (above | (tie & (pos <= last))).astype(BF16)

    nt = seq // LANES
    lrow = lax.broadcasted_iota(jnp.int32, (LANES, LANES), 0)
    lcol = lax.broadcasted_iota(jnp.int32, (LANES, LANES), 1)
    tri = (lrow <= lcol).astype(BF16)
    mm = functools.partial(jnp.dot, preferred_element_type=F32)
    for t in range(nt):
        cum_ref[t * E:(t + 1) * E, :] = mm(sel[:, t * LANES:(t + 1) * LANES], tri)
    tile_of = (lax.broadcasted_iota(jnp.int32, (seq, LANES), 0) // LANES
               == lax.broadcasted_iota(jnp.int32, (seq, LANES), 1)).astype(BF16)
    per_tile = mm(sel, tile_of)
    lane = lax.broadcasted_iota(jnp.int32, (1, LANES), 1)
    far = jnp.float32(2 * seq)
    t_end = jnp.where(lane < nt, mm(per_tile.astype(BF16), tri), far)
    t_start = jnp.where(lane < nt, t_end - per_tile, far)
    pad = jnp.zeros((LANES - E, LANES), F32)
    t_start_cols = jnp.concatenate([jnp.where(lane < nt, t_start, 0.0), pad], axis=0).T

    slot = lax.broadcasted_iota(jnp.int32, (cap, LANES), 0).astype(F32)
    ones = jnp.ones((LANES, LANES), BF16)
    zrows = jnp.zeros((LANES - nt, 2 * LANES), F32)
    for e0 in range(0, E, ROUTE_WAYS):
        es = range(e0, e0 + ROUTE_WAYS)
        tiles, picks, whole = [], [], []
        for e in es:
            absc = cum_ref[pl.ds(e, nt, stride=E), :] + t_start_cols[0:nt, e:e + 1]
            hi = jnp.where(absc >= 256.0, 1.0, 0.0) + jnp.where(absc >= 512.0, 1.0, 0.0)
            lo = absc - 256.0 * hi
            tiles.append(jnp.concatenate([jnp.concatenate([lo, hi], axis=1), zrows], axis=0).astype(BF16))
            done = jnp.where(t_end[e:e + 1] <= slot, 1.0, 0.0)
            whole.append(done)
            picks.append((jnp.where(t_start[e:e + 1] <= slot, 1.0, 0.0) - done).astype(BF16))
        rows = [mm(p, w) for p, w in zip(picks, tiles)]
        votes = [(jnp.where(r[:, :LANES] + 256.0 * r[:, LANES:] <= slot, 1.0, 0.0) + float(LANES) * d).astype(BF16)
                 for r, d in zip(rows, whole)]
        for e, v in zip(es, votes):
            idx_ref[0, :, e:e + 1] = mm(v, ones)[:, e:e + 1].astype(jnp.int32)


def _route(aff_t, cap):
    B, E, S = aff_t.shape
    return pl.pallas_call(
        functools.partial(_route_kernel, cap=cap, seq=S),
        grid=(B,),
        in_specs=[pl.BlockSpec((1, E, S), lambda b: (b, 0, 0))],
        out_specs=pl.BlockSpec((1, cap, E), lambda b: (b, 0, 0)),
        out_shape=jax.ShapeDtypeStruct((B, cap, E), jnp.int32),
        scratch_shapes=[pltpu.VMEM((S // LANES * E, LANES), F32)],
        compiler_params=_cparams(("parallel",), 32),
        name="route",
    )(aff_t)


def _gather_kernel(idx_ref, xn_ref, aff_ref, xg_ref, wg_ref, xs_ref, ws_ref, *, cap):
    b = pl.program_id(0)
    e = pl.program_id(1)
    base = (b * N_EXPERTS + e) * cap

    def body(i, carry):
        t = idx_ref[base + i]
        xs_ref[pl.ds(pl.multiple_of(i * SUBLANES, SUBLANES), SUBLANES), :] = xn_ref[0, t]
        ws_ref[pl.ds(i, 1), :] = aff_ref[0, pl.ds(t, 1), :]
        return carry

    lax.fori_loop(0, cap, body, 0, unroll=8)
    xg_ref[0, 0] = _load_token_tiles(xs_ref, cap).astype(BF16)
    lane = lax.broadcasted_iota(jnp.int32, (cap, N_EXPERTS), 1)
    wg_ref[0, 0] = jnp.sum(jnp.where(lane == e, ws_ref[...], 0.0), axis=1, keepdims=True)


def _gather(idx_flat, xn_tiles, aff, cap):
    B, S = xn_tiles.shape[:2]
    E = N_EXPERTS
    grid_spec = pltpu.PrefetchScalarGridSpec(
        num_scalar_prefetch=1,
        grid=(B, E),
        in_specs=[pl.BlockSpec((1, S, SUBLANES, LANES), lambda b, e, idx: (b, 0, 0, 0)),
                  pl.BlockSpec((1, S, E), lambda b, e, idx: (b, 0, 0))],
        out_specs=[pl.BlockSpec((1, 1, cap, D_MODEL), lambda b, e, idx: (b, e, 0, 0)),
                   pl.BlockSpec((1, 1, cap, 1), lambda b, e, idx: (b, e, 0, 0))],
        scratch_shapes=[pltpu.VMEM((cap * SUBLANES, LANES), F32), pltpu.VMEM((cap, E), F32)],
    )
    return pl.pallas_call(
        functools.partial(_gather_kernel, cap=cap),
        grid_spec=grid_spec,
        out_shape=[jax.ShapeDtypeStruct((B, E, cap, D_MODEL), BF16),
                   jax.ShapeDtypeStruct((B, E, cap, 1), F32)],
        compiler_params=_cparams(("arbitrary", "arbitrary"), 48),
        name="gather",
    )(idx_flat, xn_tiles, aff)


def _ffn_kernel(xg_ref, wg_ref, w1_ref, w2_ref, w3_ref, y_ref):
    xg = xg_ref[0, 0]
    gate = jnp.dot(xg, w1_ref[0], preferred_element_type=F32)
    up = jnp.dot(xg, w2_ref[0], preferred_element_type=F32)
    hid = (gate * _sigmoid(gate) * up).astype(BF16)
    y = jnp.dot(hid, w3_ref[0], preferred_element_type=F32) * wg_ref[0, 0]
    _store_token_tiles(y_ref.at[0, 0], y)


def _ffn(xg, wg, w1, w2, w3):
    B, E, C, D = xg.shape
    F = w1.shape[-1]
    return pl.pallas_call(
        _ffn_kernel,
        grid=(E, B),
        in_specs=[pl.BlockSpec((1, 1, C, D), lambda e, b: (b, e, 0, 0)),
                  pl.BlockSpec((1, 1, C, 1), lambda e, b: (b, e, 0, 0)),
                  pl.BlockSpec((1, D, F), lambda e, b: (e, 0, 0)),
                  pl.BlockSpec((1, D, F), lambda e, b: (e, 0, 0)),
                  pl.BlockSpec((1, F, D), lambda e, b: (e, 0, 0))],
        out_specs=pl.BlockSpec((1, 1, C * ROW_TILES, LANES), lambda e, b: (b, e, 0, 0)),
        out_shape=jax.ShapeDtypeStruct((B, E, C * ROW_TILES, LANES), F32),
        compiler_params=_cparams(("parallel", "parallel"), 48),
        name="ffn",
    )(xg, wg, w1, w2, w3)


SCATTER_BATCH = 16


def _scatter_kernel(idx_ref, y_ref, o_ref, *, cap):
    b = pl.program_id(0)
    e = pl.program_id(1)
    base = (b * N_EXPERTS + e) * cap

    @pl.when(e == 0)
    def _():
        o_ref[...] = jnp.zeros_like(o_ref)

    def body(blk, carry):
        i0 = blk * SCATTER_BATCH
        ts = [idx_ref[base + i0 + u] for u in range(SCATTER_BATCH)]
        new = [o_ref[0, ts[u]] + y_ref[0, 0, i0 + u] for u in range(SCATTER_BATCH)]
        for u in range(SCATTER_BATCH):
            o_ref[0, ts[u]] = new[u]
        return carry

    lax.fori_loop(0, cap // SCATTER_BATCH, body, 0)


def _scatter(idx_flat, y_tiles, seq):
    B, E, C = y_tiles.shape[:3]
    grid_spec = pltpu.PrefetchScalarGridSpec(
        num_scalar_prefetch=1,
        grid=(B, E),
        in_specs=[pl.BlockSpec((1, 1, C, SUBLANES, LANES), lambda b, e, idx: (b, e, 0, 0, 0))],
        out_specs=pl.BlockSpec((1, seq, SUBLANES, LANES), lambda b, e, idx: (b, 0, 0, 0)),
    )
    return pl.pallas_call(
        functools.partial(_scatter_kernel, cap=C),
        grid_spec=grid_spec,
        out_shape=jax.ShapeDtypeStruct((B, seq, SUBLANES, LANES), F32),
        compiler_params=_cparams(("arbitrary", "arbitrary"), 48),
        name="scatter",
    )(idx_flat, y_tiles)


def _ple_out_kernel(h_ref, moe_ref, p_ref, gple_ref, wpg_ref, wple_ref, gfin_ref, o_ref):
    h = h_ref[...] + _load_token_tiles(moe_ref, h_ref.shape[0])
    n = _rms(h, gple_ref[...]).astype(BF16)
    gate = _sigmoid(jnp.dot(n, wpg_ref[...], preferred_element_type=F32))
    emb = jnp.dot(p_ref[...].astype(BF16), wple_ref[...], preferred_element_type=F32)
    o_ref[...] = _rms(h + gate * emb, gfin_ref[...])


def _ple_out(h, moe, p2, gple, wpg, wple, gfin, tm):
    T = h.shape[0]
    row = lambda n: pl.BlockSpec((tm, n), lambda i: (i, 0))
    return pl.pallas_call(
        _ple_out_kernel,
        grid=(T // tm,),
        in_specs=[row(D_MODEL), pl.BlockSpec((tm * ROW_TILES, LANES), lambda i: (i, 0)), row(PLE_DIM),
                  _full(gple.shape), _full(wpg.shape), _full(wple.shape), _full(gfin.shape)],
        out_specs=row(D_MODEL),
        out_shape=jax.ShapeDtypeStruct((T, D_MODEL), F32),
        compiler_params=_cparams(("parallel",), 48),
        name="ple_out",
    )(h, moe, p2, gple, wpg, wple, gfin)


def kernel(x, p, positions, norm_mix, w_in, gla_gate_up_fwd, gla_gate_bias_fwd, gla_gate_up_bwd, gla_gate_bias_bwd, attn_sink, gla_norm, w_branch_attn, w_branch_gla, w_out, norm_ffn, w_router, w_exp_gate, w_exp_up, w_exp_down, norm_ple, w_ple_gate, w_ple, norm_final):
    B, S, D = x.shape
    T = B * S
    depth = w_in.shape[0]
    cap = CAPACITY_FACTOR * S // N_EXPERTS
    R = GLA_GATE_RANK

    posc = positions.reshape(T, 1)
    posr = positions.reshape(1, T)
    inv_freq = ROPE_THETA ** (-jnp.arange(0, ROPE_DIM, 2, dtype=F32) / ROPE_DIM)
    invfl = jnp.tile(inv_freq, LANES // (ROPE_DIM // 2)).reshape(1, LANES)
    invfc = inv_freq.reshape(ROPE_DIM // 2, 1)

    h = x.reshape(T, D)
    for l in range(depth):
        o = 0
        cols = {}
        for name, n in (("q", ATT_Q_DIM), ("k", ATT_KV_DIM), ("v", ATT_KV_DIM), ("gqk", 2 * GLA_KEY_DIM),
                        ("gv", GLA_VAL_DIM), ("gr", GLA_VAL_DIM), ("z", 2 * R), ("gate", 2 * D_MODEL)):
            cols[name] = w_in[l][:, o:o + n].astype(BF16)
            o += n
        per_head = lambda w: w.reshape(-1, GLA_HEADS, GLA_DK).swapaxes(0, 1)
        upf, upb = per_head(gla_gate_up_fwd[l]), per_head(gla_gate_up_bwd[l])
        up = jnp.concatenate([jnp.concatenate([upf, jnp.zeros_like(upf)], axis=2),
                              jnp.concatenate([jnp.zeros_like(upb), upb], axis=2)], axis=1)
        up_hi = up.astype(BF16)
        up_lo = (up - up_hi.astype(F32)).astype(BF16)
        upw = jnp.concatenate([up_hi, up_hi, up_lo, jnp.zeros_like(up_lo)], axis=1)
        wz4 = jnp.tile(cols["z"], (1, LANES // (2 * R)))
        gbias = jnp.concatenate([per_head(gla_gate_bias_fwd[l]), per_head(gla_gate_bias_bwd[l])], axis=2)
        wr = w_router[l]
        wr_hi = wr.astype(BF16)
        wr_lo = (wr - wr_hi.astype(F32)).astype(BF16)
        wr2 = jnp.concatenate([jnp.concatenate([wr_hi, wr_lo], axis=1),
                               jnp.concatenate([wr_hi, jnp.zeros_like(wr_lo)], axis=1)], axis=0)

        qt, k0, k1, vt, gq, gk, gv, gr, z, sga, sgg = _in_proj(
            h, posc, posr, invfl, invfc, norm_mix[l].reshape(1, D), cols["q"].T, cols["k"], cols["v"].T,
            cols["gqk"], cols["gv"], cols["gr"], wz4, cols["gate"], tm=256)

        att = _swa(attn_sink[l], qt, k0.reshape(B, S, -1), k1.reshape(B, S, -1), vt, batch=B, tq=512)
        gla = _gla(z.reshape(B, S, -1), gq.reshape(B, S, -1), gk.reshape(B, S, -1), gv.reshape(B, S, -1),
                   gr.reshape(B, S, -1), upw, gbias, gla_norm[l].reshape(1, -1))

        h1, xn, aff = _mix_out(h, att.reshape(T, -1), gla.reshape(T, -1), sga, sgg,
                               w_branch_attn[l].astype(BF16), w_branch_gla[l].astype(BF16),
                               w_out[l].astype(BF16), norm_ffn[l].reshape(1, D), wr2, tm=256)

        aff3 = aff.reshape(B, S, N_EXPERTS)
        idx = _route(jnp.swapaxes(aff3, 1, 2), cap)
        idx_flat = jnp.swapaxes(idx, 1, 2).reshape(-1)
        xg, wg = _gather(idx_flat, xn.reshape(B, S, SUBLANES, LANES), aff3, cap)
        y = _ffn(xg, wg, w_exp_gate[l].astype(BF16), w_exp_up[l].astype(BF16), w_exp_down[l].astype(BF16))
        moe = _scatter(idx_flat, y.reshape(B, N_EXPERTS, cap, SUBLANES, LANES), S)

        last = l == depth - 1
        gfin = norm_final.reshape(1, D)
        assert last, "the final norm is fused into the last layer's PLE kernel"
        h = _ple_out(h1, moe.reshape(T * ROW_TILES, LANES), p[l].reshape(T, PLE_DIM), norm_ple[l].reshape(1, D),
                     w_ple_gate[l].astype(BF16), w_ple[l].astype(BF16), gfin, tm=256)
    return h.reshape(B, S, D)
```

```python
import functools
import math

import jax
import jax.numpy as jnp
from jax import lax
from jax.experimental import pallas as pl
from jax.experimental.pallas import tpu as pltpu

D_MODEL = 1024
ATT_HEADS = 8
ATT_KV_HEADS = 2
ATT_HEAD_DIM = 64
ATT_GROUP = ATT_HEADS // ATT_KV_HEADS
ATT_Q_DIM = ATT_HEADS * ATT_HEAD_DIM
ATT_KV_DIM = ATT_KV_HEADS * ATT_HEAD_DIM
WINDOW = 128
ROPE_DIM = ATT_HEAD_DIM // 4
ROPE_THETA = 500000.0
GLA_HEADS = 4
GLA_KEY_DIM = D_MODEL // 2
GLA_VAL_DIM = D_MODEL
GLA_DK = GLA_KEY_DIM // GLA_HEADS
GLA_DV = GLA_VAL_DIM // GLA_HEADS
GLA_GATE_RANK = 16
GLA_GATE_NORM = 16.0
GLA_CHUNK = 64
N_EXPERTS = 16
EXPERT_FF = D_MODEL
CAPACITY_FACTOR = 2
PLE_DIM = 256
EPS = 1e-6

LANES = 128
MIB = 1024 * 1024
BF16 = jnp.bfloat16
F32 = jnp.float32
LOG2E = math.log2(math.e)

NT_DIMS = (((1,), (1,)), ((), ()))
TN_DIMS = (((0,), (0,)), ((), ()))


def _cparams(sem, vmem_mib):
    return pltpu.CompilerParams(dimension_semantics=sem, vmem_limit_bytes=vmem_mib * MIB)


def _full(shape):
    n = len(shape)
    return pl.BlockSpec(shape, lambda *_: (0,) * n)


def _rms(x, gain):
    ms = jnp.mean(x * x, axis=-1, keepdims=True)
    return x * lax.rsqrt(ms + EPS) * gain


def _sigmoid(x):
    return 0.5 * jnp.tanh(0.5 * x) + 0.5


SUBLANES = 8
ROW_TILES = D_MODEL // LANES


def _store_token_tiles(ref2d, x, first_row=0):
    rows = x.shape[0]
    for j in range(ROW_TILES):
        ref2d[pl.ds(first_row * ROW_TILES + j, rows, stride=ROW_TILES), :] = x[:, j * LANES:(j + 1) * LANES]


def _load_token_tiles(ref2d, rows, first_row=0):
    return jnp.concatenate([ref2d[pl.ds(first_row * ROW_TILES + j, rows, stride=ROW_TILES), :]
                            for j in range(ROW_TILES)], axis=1)


ROW_WAYS = 2


def _sub_blocks(tile_rows):
    sub = tile_rows // ROW_WAYS
    return sub, [w * sub for w in range(ROW_WAYS)]


def _rope(t, cos_t, sin_t, first_half):
    fwd = pltpu.roll(t, LANES - ROPE_DIM // 2, axis=1)
    bwd = pltpu.roll(t, ROPE_DIM // 2, axis=1)
    return t * cos_t + jnp.where(first_half, fwd, bwd) * sin_t


def _in_proj_kernel(x_ref, posc_ref, posr_ref, invfl_ref, invfc_ref, gain_ref, wqt_ref, wk_ref, wvt_ref,
                    wgqk_ref, wgv_ref, wgr_ref, wz_ref, wgate_ref,
                    qt_ref, k0_ref, k1_ref, vt_ref, gq_ref, gk_ref, gv_ref, gr_ref, z_ref, sga_ref, sgg_ref):
    a = _rms(x_ref[...], gain_ref[...]).astype(BF16)
    half = ROPE_DIM // 2

    qt = lax.dot_general(wqt_ref[...], a, NT_DIMS, preferred_element_type=F32)
    ang_t = invfc_ref[...] * posr_ref[...].astype(F32)
    cos_r, sin_r = jnp.cos(ang_t), jnp.sin(ang_t)
    qscale = ATT_HEAD_DIM ** -0.5 * LOG2E
    for h in range(ATT_HEADS):
        r0 = h * ATT_HEAD_DIM
        t1, t2 = qt[r0:r0 + half], qt[r0 + half:r0 + ROPE_DIM]
        qt_ref[r0:r0 + half, :] = ((t1 * cos_r - t2 * sin_r) * qscale).astype(BF16)
        qt_ref[r0 + half:r0 + ROPE_DIM, :] = ((t2 * cos_r + t1 * sin_r) * qscale).astype(BF16)
        qt_ref[r0 + ROPE_DIM:r0 + ATT_HEAD_DIM, :] = (qt[r0 + ROPE_DIM:r0 + ATT_HEAD_DIM] * qscale).astype(BF16)

    lane = lax.broadcasted_iota(jnp.int32, (1, LANES), 1)
    d = lane % ATT_HEAD_DIM
    first_half = d < half
    in_rope = d < ROPE_DIM
    ang = posc_ref[...].astype(F32) * invfl_ref[...]
    cos_t = jnp.where(in_rope, jnp.cos(ang), 1.0)
    sin_t = jnp.where(first_half, -jnp.sin(ang), jnp.where(in_rope, jnp.sin(ang), 0.0))
    k = _rope(jnp.dot(a, wk_ref[...], preferred_element_type=F32), cos_t, sin_t, first_half).astype(BF16)
    k0_ref[...] = k[:, :ATT_HEAD_DIM]
    k1_ref[...] = k[:, ATT_HEAD_DIM:]
    vt_ref[...] = lax.dot_general(wvt_ref[...], a, NT_DIMS, preferred_element_type=F32).astype(BF16)

    gqk = jnp.dot(a, wgqk_ref[...], preferred_element_type=F32)
    gq_ref[...] = (gqk[:, :GLA_KEY_DIM] * (GLA_DK ** -0.5)).astype(BF16)
    gk_ref[...] = gqk[:, GLA_KEY_DIM:].astype(BF16)
    gv_ref[...] = jnp.dot(a, wgv_ref[...], preferred_element_type=F32).astype(BF16)
    gr_ref[...] = jnp.dot(a, wgr_ref[...], preferred_element_type=F32).astype(BF16)
    z_ref[...] = jnp.dot(a, wz_ref[...], preferred_element_type=F32)
    gates = jnp.dot(a, wgate_ref[...], preferred_element_type=F32)
    sga_ref[...] = _sigmoid(gates[:, :D_MODEL]).astype(BF16)
    sgg_ref[...] = _sigmoid(gates[:, D_MODEL:]).astype(BF16)


def _in_proj(x2, posc, posr, invfl, invfc, gain, wqt, wk, wvt, wgqk, wgv, wgr, wz, wgate, tm):
    T = x2.shape[0]
    row = lambda n: pl.BlockSpec((tm, n), lambda i: (i, 0))
    col = lambda n: pl.BlockSpec((n, tm), lambda i: (0, i))
    row_widths = (ATT_HEAD_DIM, ATT_HEAD_DIM, None, GLA_KEY_DIM, GLA_KEY_DIM, GLA_VAL_DIM,
                  GLA_VAL_DIM, wz.shape[1], D_MODEL, D_MODEL)
    row_dtypes = (BF16,) * 7 + (F32, BF16, BF16)
    out_specs = [col(ATT_Q_DIM)]
    out_shape = [jax.ShapeDtypeStruct((ATT_Q_DIM, T), BF16)]
    for n, dt in zip(row_widths, row_dtypes):
        if n is None:
            out_specs.append(col(ATT_KV_DIM))
            out_shape.append(jax.ShapeDtypeStruct((ATT_KV_DIM, T), BF16))
        else:
            out_specs.append(row(n))
            out_shape.append(jax.ShapeDtypeStruct((T, n), dt))
    consts = (invfl, invfc, gain, wqt, wk, wvt, wgqk, wgv, wgr, wz, wgate)
    return pl.pallas_call(
        _in_proj_kernel,
        grid=(T // tm,),
        in_specs=[row(D_MODEL), row(1), col(1)] + [_full(c.shape) for c in consts],
        out_specs=out_specs,
        out_shape=out_shape,
        compiler_params=_cparams(("parallel",), 56),
        name="in_proj",
    )(x2, posc, posr, *consts)


def _swa_kernel(sink_ref, qt_ref, k0_ref, k1_ref, vt_ref, o_ref, *, tq, seq):
    blk = WINDOW
    span = 3 * blk
    hd = ATT_HEAD_DIM
    n = pl.program_id(1)
    ones = jnp.ones((16, span), BF16)
    for sb in range(tq // blk):
        q0 = n * tq + sb * blk
        start = pl.multiple_of(jnp.clip(q0 - blk, 0, seq - span), blk)
        kj = start + lax.broadcasted_iota(jnp.int32, (span, blk), 0)
        qi = q0 + lax.broadcasted_iota(jnp.int32, (span, blk), 1)
        valid = jnp.abs(qi - kj) <= WINDOW
        outs = []
        for g, k_ref in enumerate((k0_ref, k1_ref)):
            kw = k_ref[0, pl.ds(start, span), :]
            vaug = jnp.concatenate([vt_ref[g * hd:(g + 1) * hd, pl.ds(start, span)], ones], axis=0)
            heads = range(g * ATT_GROUP, (g + 1) * ATT_GROUP)
            qs = jnp.concatenate([qt_ref[h * hd:(h + 1) * hd, sb * blk:(sb + 1) * blk] for h in heads], axis=1)
            s_all = jnp.dot(kw, qs, preferred_element_type=F32)
            for i, h in enumerate(heads):
                s = jnp.where(valid, s_all[:, i * blk:(i + 1) * blk], -jnp.inf)
                sink = sink_ref[h] * LOG2E
                m = jnp.maximum(jnp.max(s, axis=0, keepdims=True), sink)
                e = jnp.exp2(s - m).astype(BF16)
                r = jnp.dot(vaug, e, preferred_element_type=F32)
                den = r[hd:hd + 1] + jnp.exp2(sink - m)
                outs.append(r[:hd] / den)
        for pr in range(ATT_HEADS // 2):
            pair = jnp.concatenate([outs[2 * pr], outs[2 * pr + 1]], axis=0)
            o_ref[0, sb * blk:(sb + 1) * blk, pr * 2 * hd:(pr + 1) * 2 * hd] = pair.T.astype(BF16)


def _swa(sink, qt, k0, k1, vt, batch, tq):
    S = k0.shape[1]
    nq = S // tq
    kspec = pl.BlockSpec((1, S, ATT_HEAD_DIM), lambda b, n: (b, 0, 0))
    return pl.pallas_call(
        functools.partial(_swa_kernel, tq=tq, seq=S),
        grid=(batch, nq),
        in_specs=[pl.BlockSpec(memory_space=pltpu.SMEM),
                  pl.BlockSpec((ATT_Q_DIM, tq), lambda b, n: (0, b * nq + n)),
                  kspec, kspec,
                  pl.BlockSpec((ATT_KV_DIM, S), lambda b, n: (0, b))],
        out_specs=pl.BlockSpec((1, tq, ATT_Q_DIM), lambda b, n: (b, n, 0)),
        out_shape=jax.ShapeDtypeStruct((batch, S, ATT_Q_DIM), BF16),
        compiler_params=_cparams(("parallel", "parallel"), 32),
        name="swa",
    )(sink, qt, k0, k1, vt)


def _log_sigmoid(u):
    return jnp.minimum(u, 0.0) - jnp.log(1.0 + jnp.exp(-jnp.abs(u)))


def _split2(x):
    hi = x.astype(BF16)
    return hi, (x - hi.astype(F32)).astype(BF16)


GLA_WAYS = 4


def _gla_kernel(z_ref, q_ref, k_ref, v_ref, r_ref, upw_ref, bias_ref, gain_ref,
                o_ref, cf_ref, cb_ref, kef_ref, keb_ref, st_ref, s_ref, *, seq):
    L = GLA_CHUNK
    R2 = 2 * GLA_GATE_RANK
    nc = seq // L
    grp = 4 * L
    cpg = grp // L
    dk = GLA_DK
    mm = functools.partial(jnp.dot, preferred_element_type=F32)
    nt = functools.partial(lax.dot_general, dimension_numbers=NT_DIMS, preferred_element_type=F32)

    row = lax.broadcasted_iota(jnp.int32, (grp, grp), 0)
    col = lax.broadcasted_iota(jnp.int32, (grp, grp), 1)
    same = (row // L) == (col // L)
    fwd_mask = same & (col <= row)
    bwd_mask = same & (col > row)
    tri_lo = jnp.where(fwd_mask, 1.0, 0.0).astype(BF16)
    tri_up = jnp.where(same & (col >= row), 1.0, 0.0).astype(BF16)
    lane = lax.broadcasted_iota(jnp.int32, (1, LANES), 1)
    use_lo = (lane >= R2) & (lane < 2 * R2)

    def group_starts(i):
        return [pl.multiple_of((i * GLA_WAYS + w) * grp, grp) for w in range(GLA_WAYS)]

    def cum_body(i, carry):
        r0s = group_starts(i)
        zs = [_split2(z_ref[0, pl.ds(r0, grp), :]) for r0 in r0s]
        us = [mm(jnp.where(use_lo, zl, zh), upw_ref[...]) + bias_ref[...] for zh, zl in zs]
        las = [_split2(_log_sigmoid(u) * (1.0 / GLA_GATE_NORM)) for u in us]
        cfxs = [mm(tri_lo, jnp.concatenate([lh[:, :dk], ll[:, :dk]], axis=1)) for lh, ll in las]
        cbxs = [mm(tri_up, jnp.concatenate([lh[:, dk:], ll[:, dk:]], axis=1)) for lh, ll in las]
        for r0, cfx, cbx in zip(r0s, cfxs, cbxs):
            cf = cfx[:, :dk] + cfx[:, dk:]
            cb = cbx[:, :dk] + cbx[:, dk:]
            cf_ref[pl.ds(r0, grp), :] = cf
            cb_ref[pl.ds(r0, grp), :] = cb
            k = k_ref[0, pl.ds(r0, grp), :].astype(F32)
            for c in range(cpg):
                sl = slice(c * L, (c + 1) * L)
                gf = cf[(c + 1) * L - 1:(c + 1) * L]
                gb = cb[c * L:c * L + 1]
                kef_ref[pl.ds(r0 + c * L, L), :] = (k[sl] * jnp.exp(gf - cf[sl])).astype(BF16)
                keb_ref[pl.ds(r0 + c * L, L), :] = (k[sl] * jnp.exp(gb - cb[sl])).astype(BF16)
        return carry

    lax.fori_loop(0, seq // (grp * GLA_WAYS), cum_body, 0)

    s_ref[...] = jnp.zeros_like(s_ref)
    zero_k = jnp.zeros((L, dk), BF16)

    def state_body(i, carry):
        j = nc - 1 - i
        rf = pl.multiple_of(i * L, L)
        rb = pl.multiple_of(j * L, L)
        vcat = jnp.concatenate([v_ref[0, pl.ds(rf, L), :], v_ref[0, pl.ds(rb, L), :]], axis=0)
        kblk = jnp.concatenate([jnp.concatenate([kef_ref[pl.ds(rf, L), :], zero_k], axis=1),
                                jnp.concatenate([zero_k, keb_ref[pl.ds(rb, L), :]], axis=1)], axis=0)
        kv = lax.dot_general(vcat, kblk, TN_DIMS, preferred_element_type=F32)
        decay = jnp.exp(jnp.concatenate([cf_ref[pl.ds(rf + L - 1, 1), :], cb_ref[pl.ds(rb, 1), :]], axis=1))
        s = s_ref[...]
        st_ref[i, :, 0:dk] = s[:, :dk].astype(BF16)
        st_ref[j, :, dk:2 * dk] = s[:, dk:].astype(BF16)
        s_ref[...] = s * decay + kv
        return carry

    lax.fori_loop(0, nc, state_body, 0, unroll=16)

    def out_body(i, carry):
        r0s = group_starts(i)
        ops = []
        for r0 in r0s:
            q = q_ref[0, pl.ds(r0, grp), :].astype(F32)
            k = k_ref[0, pl.ds(r0, grp), :].astype(F32)
            cf = cf_ref[pl.ds(r0, grp), :]
            cb = cb_ref[pl.ds(r0, grp), :]
            ops.append(((q * jnp.exp(cf)).astype(BF16), (k * jnp.exp(-cf)).astype(BF16),
                        (q * jnp.exp(cb)).astype(BF16), (k * jnp.exp(-cb)).astype(BF16)))
        scores = [(nt(qf, kf), nt(qb, kb)) for qf, kf, qb, kb in ops]
        attns = [jnp.where(fwd_mask, af, jnp.where(bwd_mask, ab, 0.0)).astype(BF16) for af, ab in scores]
        outs = []
        for r0, attn, (qf, _, qb, _) in zip(r0s, attns, ops):
            c0 = r0 // L
            qcat = jnp.concatenate([qf, qb], axis=1)
            inter = jnp.concatenate([nt(qcat[c * L:(c + 1) * L], st_ref[c0 + c]) for c in range(cpg)], axis=0)
            outs.append(mm(attn, v_ref[0, pl.ds(r0, grp), :]) + inter)
        for r0, o in zip(r0s, outs):
            r = r_ref[0, pl.ds(r0, grp), :].astype(F32)
            o_ref[0, pl.ds(r0, grp), :] = (_rms(o, gain_ref[...]) * (r * _sigmoid(r))).astype(BF16)
        return carry

    lax.fori_loop(0, seq // (grp * GLA_WAYS), out_body, 0)


def _gla(z, gq, gk, gv, gr, upw, bias, gain):
    B, S, _ = gq.shape
    nc = S // GLA_CHUNK
    seq_blk = lambda n: pl.BlockSpec((1, S, n), lambda b, h: (b, 0, h))
    head_blk = lambda r, n: pl.BlockSpec((None, r, n), lambda b, h: (h, 0, 0))
    return pl.pallas_call(
        functools.partial(_gla_kernel, seq=S),
        grid=(B, GLA_HEADS),
        in_specs=[pl.BlockSpec((1, S, LANES), lambda b, h: (b, 0, 0)),
                  seq_blk(GLA_DK), seq_blk(GLA_DK), seq_blk(GLA_DV), seq_blk(GLA_DV),
                  head_blk(LANES, 2 * GLA_DK), head_blk(1, 2 * GLA_DK),
                  pl.BlockSpec((1, GLA_DV), lambda b, h: (0, h))],
        out_specs=seq_blk(GLA_DV),
        out_shape=jax.ShapeDtypeStruct((B, S, GLA_VAL_DIM), BF16),
        scratch_shapes=[pltpu.VMEM((S, GLA_DK), F32), pltpu.VMEM((S, GLA_DK), F32),
                        pltpu.VMEM((S, GLA_DK), BF16), pltpu.VMEM((S, GLA_DK), BF16),
                        pltpu.VMEM((nc, GLA_DV, 2 * GLA_DK), BF16),
                        pltpu.VMEM((GLA_DV, 2 * GLA_DK), F32)],
        compiler_params=_cparams(("parallel", "parallel"), 48),
        name="gla",
    )(z, gq, gk, gv, gr, upw, bias, gain)


def _mix_out_kernel(x_ref, a_ref, g_ref, sga_ref, sgg_ref, wa_ref, wb_ref, wo_ref, gain_ref, wr_ref,
                    h_ref, xn_ref, aff_ref):
    sub, starts = _sub_blocks(x_ref.shape[0])
    mm = functools.partial(jnp.dot, preferred_element_type=F32)
    blk = lambda ref, r0: ref[r0:r0 + sub, :]
    y_att = [mm(blk(a_ref, r0), wa_ref[...]) for r0 in starts]
    y_gla = [mm(blk(g_ref, r0), wb_ref[...]) for r0 in starts]
    merged = [(blk(sga_ref, r0).astype(F32) * ya + blk(sgg_ref, r0).astype(F32) * yg).astype(BF16)
              for r0, ya, yg in zip(starts, y_att, y_gla)]
    hs = [blk(x_ref, r0) + mm(m, wo_ref[...]) for r0, m in zip(starts, merged)]
    xns = [_rms(h, gain_ref[...]) for h in hs]
    for r0, h, xn in zip(starts, hs, xns):
        h_ref[r0:r0 + sub, :] = h
        _store_token_tiles(xn_ref, xn, r0)
    his = [xn.astype(BF16) for xn in xns]
    parts = [mm(jnp.concatenate([hi, (xn - hi.astype(F32)).astype(BF16)], axis=1), wr_ref[...])
             for xn, hi in zip(xns, his)]
    for r0, part in zip(starts, parts):
        logits = part[:, :N_EXPERTS] + part[:, N_EXPERTS:]
        e = jnp.exp(logits - jnp.max(logits, axis=-1, keepdims=True))
        aff_ref[r0:r0 + sub, :] = e / jnp.sum(e, axis=-1, keepdims=True)


def _mix_out(x2, a, g, sga, sgg, wa, wb, wo, gain, wr, tm):
    T = x2.shape[0]
    row = lambda n: pl.BlockSpec((tm, n), lambda i: (i, 0))
    return pl.pallas_call(
        _mix_out_kernel,
        grid=(T // tm,),
        in_specs=[row(D_MODEL), row(ATT_Q_DIM), row(GLA_VAL_DIM), row(D_MODEL), row(D_MODEL),
                  _full(wa.shape), _full(wb.shape), _full(wo.shape), _full(gain.shape), _full(wr.shape)],
        out_specs=[row(D_MODEL), pl.BlockSpec((tm * ROW_TILES, LANES), lambda i: (i, 0)), row(N_EXPERTS)],
        out_shape=[jax.ShapeDtypeStruct((T, D_MODEL), F32), jax.ShapeDtypeStruct((T * ROW_TILES, LANES), F32),
                   jax.ShapeDtypeStruct((T, N_EXPERTS), F32)],
        compiler_params=_cparams(("parallel",), 48),
        name="mix_out",
    )(x2, a, g, sga, sgg, wa, wb, wo, gain, wr)


ROUTE_WAYS = 4


def _route_kernel(aff_ref, idx_ref, cum_ref, *, cap, seq):
    E = N_EXPERTS
    aff = aff_ref[0]
    count = lambda mask: jnp.sum(mask.astype(jnp.int32), axis=1, keepdims=True)
    as_float = lambda pattern: lax.bitcast_convert_type(pattern, F32)

    def thr_body(t, pattern):
        cand = pattern | jnp.left_shift(jnp.int32(1), 30 - t)
        return jnp.where(count(aff >= as_float(cand)) >= cap, cand, pattern)

    thr = as_float(lax.fori_loop(0, 31, thr_body, jnp.zeros((E, 1), jnp.int32)))
    above = aff > thr
    tie = aff == thr
    need = cap - count(above)

    pos = lax.broadcasted_iota(jnp.int32, (E, seq), 1)

    def tie_body(t, last):
        cand = last | jnp.left_shift(jnp.int32(1), (seq.bit_length() - 2) - t)
        return jnp.where(count(tie & (pos < cand)) < need, cand, last)

    last = lax.fori_loop(0, seq.bit_length() - 1, tie_body, jnp.zeros((E, 1), jnp.int32))
    sel = (above | (tie & (pos <= last))).astype(BF16)

    nt = seq // LANES
    lrow = lax.broadcasted_iota(jnp.int32, (LANES, LANES), 0)
    lcol = lax.broadcasted_iota(jnp.int32, (LANES, LANES), 1)
    tri = (lrow <= lcol).astype(BF16)
    mm = functools.partial(jnp.dot, preferred_element_type=F32)
    for t in range(nt):
        cum_ref[t * E:(t + 1) * E, :] = mm(sel[:, t * LANES:(t + 1) * LANES], tri)
    tile_of = (lax.broadcasted_iota(jnp.int32, (seq, LANES), 0) // LANES
               == lax.broadcasted_iota(jnp.int32, (seq, LANES), 1)).astype(BF16)
    per_tile = mm(sel, tile_of)
    lane = lax.broadcasted_iota(jnp.int32, (1, LANES), 1)
    far = jnp.float32(2 * seq)
    t_end = jnp.where(lane < nt, mm(per_tile.astype(BF16), tri), far)
    t_start = jnp.where(lane < nt, t_end - per_tile, far)
    pad = jnp.zeros((LANES - E, LANES), F32)
    t_start_cols = jnp.concatenate([jnp.where(lane < nt, t_start, 0.0), pad], axis=0).T

    slot = lax.broadcasted_iota(jnp.int32, (cap, LANES), 0).astype(F32)
    ones = jnp.ones((LANES, LANES), BF16)
    zrows = jnp.zeros((LANES - nt, 2 * LANES), F32)
    for e0 in range(0, E, ROUTE_WAYS):
        es = range(e0, e0 + ROUTE_WAYS)
        tiles, picks, whole = [], [], []
        for e in es:
            absc = cum_ref[pl.ds(e, nt, stride=E), :] + t_start_cols[0:nt, e:e + 1]
            hi = jnp.where(absc >= 256.0, 1.0, 0.0) + jnp.where(absc >= 512.0, 1.0, 0.0)
            lo = absc - 256.0 * hi
            tiles.append(jnp.concatenate([jnp.concatenate([lo, hi], axis=1), zrows], axis=0).astype(BF16))
            done = jnp.where(t_end[e:e + 1] <= slot, 1.0, 0.0)
            whole.append(done)
            picks.append((jnp.where(t_start[e:e + 1] <= slot, 1.0, 0.0) - done).astype(BF16))
        rows = [mm(p, w) for p, w in zip(picks, tiles)]
        votes = [(jnp.where(r[:, :LANES] + 256.0 * r[:, LANES:] <= slot, 1.0, 0.0) + float(LANES) * d).astype(BF16)
                 for r, d in zip(rows, whole)]
        for e, v in zip(es, votes):
            idx_ref[0, :, e:e + 1] = mm(v, ones)[:, e:e + 1].astype(jnp.int32)


def _route(aff_t, cap):
    B, E, S = aff_t.shape
    return pl.pallas_call(
        functools.partial(_route_kernel, cap=cap, seq=S),
        grid=(B,),
        in_specs=[pl.BlockSpec((1, E, S), lambda b: (b, 0, 0))],
        out_specs=pl.BlockSpec((1, cap, E), lambda b: (b, 0, 0)),
        out_shape=jax.ShapeDtypeStruct((B, cap, E), jnp.int32),
        scratch_shapes=[pltpu.VMEM((S // LANES * E, LANES), F32)],
        compiler_params=_cparams(("parallel",), 32),
        name="route",
    )(aff_t)


def _gather_kernel(idx_ref, xn_ref, aff_ref, xg_ref, wg_ref, xs_ref, ws_ref, *, cap):
    b = pl.program_id(0)
    e = pl.program_id(1)
    base = (b * N_EXPERTS + e) * cap

    def body(i, carry):
        t = idx_ref[base + i]
        xs_ref[pl.ds(pl.multiple_of(i * SUBLANES, SUBLANES), SUBLANES), :] = xn_ref[0, t]
        ws_ref[pl.ds(i, 1), :] = aff_ref[0, pl.ds(t, 1), :]
        return carry

    lax.fori_loop(0, cap, body, 0, unroll=8)
    xg_ref[0, 0] = _load_token_tiles(xs_ref, cap).astype(BF16)
    lane = lax.broadcasted_iota(jnp.int32, (cap, N_EXPERTS), 1)
    wg_ref[0, 0] = jnp.sum(jnp.where(lane == e, ws_ref[...], 0.0), axis=1, keepdims=True)


def _gather(idx_flat, xn_tiles, aff, cap):
    B, S = xn_tiles.shape[:2]
    E = N_EXPERTS
    grid_spec = pltpu.PrefetchScalarGridSpec(
        num_scalar_prefetch=1,
        grid=(B, E),
        in_specs=[pl.BlockSpec((1, S, SUBLANES, LANES), lambda b, e, idx: (b, 0, 0, 0)),
                  pl.BlockSpec((1, S, E), lambda b, e, idx: (b, 0, 0))],
        out_specs=[pl.BlockSpec((1, 1, cap, D_MODEL), lambda b, e, idx: (b, e, 0, 0)),
                   pl.BlockSpec((1, 1, cap, 1), lambda b, e, idx: (b, e, 0, 0))],
        scratch_shapes=[pltpu.VMEM((cap * SUBLANES, LANES), F32), pltpu.VMEM((cap, E), F32)],
    )
    return pl.pallas_call(
        functools.partial(_gather_kernel, cap=cap),
        grid_spec=grid_spec,
        out_shape=[jax.ShapeDtypeStruct((B, E, cap, D_MODEL), BF16),
                   jax.ShapeDtypeStruct((B, E, cap, 1), F32)],
        compiler_params=_cparams(("arbitrary", "arbitrary"), 48),
        name="gather",
    )(idx_flat, xn_tiles, aff)


def _ffn_kernel(xg_ref, wg_ref, w1_ref, w2_ref, w3_ref, y_ref, b1_ref, b2_ref, b3_ref):
    @pl.when(pl.program_id(1) == 0)
    def _():
        b1_ref[...] = w1_ref[0].astype(BF16)
        b2_ref[...] = w2_ref[0].astype(BF16)
        b3_ref[...] = w3_ref[0].astype(BF16)

    xg = xg_ref[0, 0]
    gate = jnp.dot(xg, b1_ref[...], preferred_element_type=F32)
    up = jnp.dot(xg, b2_ref[...], preferred_element_type=F32)
    hid = (gate * _sigmoid(gate) * up).astype(BF16)
    y = jnp.dot(hid, b3_ref[...], preferred_element_type=F32) * wg_ref[0, 0]
    _store_token_tiles(y_ref.at[0, 0], y)


def _ffn(xg, wg, w1, w2, w3):
    B, E, C, D = xg.shape
    F = w1.shape[-1]
    return pl.pallas_call(
        _ffn_kernel,
        grid=(E, B),
        in_specs=[pl.BlockSpec((1, 1, C, D), lambda e, b: (b, e, 0, 0)),
                  pl.BlockSpec((1, 1, C, 1), lambda e, b: (b, e, 0, 0)),
                  pl.BlockSpec((1, D, F), lambda e, b: (e, 0, 0)),
                  pl.BlockSpec((1, D, F), lambda e, b: (e, 0, 0)),
                  pl.BlockSpec((1, F, D), lambda e, b: (e, 0, 0))],
        out_specs=pl.BlockSpec((1, 1, C * ROW_TILES, LANES), lambda e, b: (b, e, 0, 0)),
        out_shape=jax.ShapeDtypeStruct((B, E, C * ROW_TILES, LANES), F32),
        scratch_shapes=[pltpu.VMEM((D, F), BF16), pltpu.VMEM((D, F), BF16), pltpu.VMEM((F, D), BF16)],
        compiler_params=_cparams(("arbitrary", "arbitrary"), 56),
        name="ffn",
    )(xg, wg, w1, w2, w3)


SCATTER_BATCH = 16


def _scatter_kernel(idx_ref, y_ref, o_ref, *, cap):
    b = pl.program_id(0)
    e = pl.program_id(1)
    base = (b * N_EXPERTS + e) * cap

    @pl.when(e == 0)
    def _():
        o_ref[...] = jnp.zeros_like(o_ref)

    def body(blk, carry):
        i0 = blk * SCATTER_BATCH
        ts = [idx_ref[base + i0 + u] for u in range(SCATTER_BATCH)]
        new = [o_ref[0, ts[u]] + y_ref[0, 0, i0 + u] for u in range(SCATTER_BATCH)]
        for u in range(SCATTER_BATCH):
            o_ref[0, ts[u]] = new[u]
        return carry

    lax.fori_loop(0, cap // SCATTER_BATCH, body, 0)


def _scatter(idx_flat, y_tiles, seq):
    B, E, C = y_tiles.shape[:3]
    grid_spec = pltpu.PrefetchScalarGridSpec(
        num_scalar_prefetch=1,
        grid=(B, E),
        in_specs=[pl.BlockSpec((1, 1, C, SUBLANES, LANES), lambda b, e, idx: (b, e, 0, 0, 0))],
        out_specs=pl.BlockSpec((1, seq, SUBLANES, LANES), lambda b, e, idx: (b, 0, 0, 0)),
    )
    return pl.pallas_call(
        functools.partial(_scatter_kernel, cap=C),
        grid_spec=grid_spec,
        out_shape=jax.ShapeDtypeStruct((B, seq, SUBLANES, LANES), F32),
        compiler_params=_cparams(("arbitrary", "arbitrary"), 48),
        name="scatter",
    )(idx_flat, y_tiles)


def _ple_out_kernel(h_ref, moe_ref, p_ref, gple_ref, wpg_ref, wple_ref, gfin_ref, o_ref):
    sub, starts = _sub_blocks(h_ref.shape[0])
    mm = functools.partial(jnp.dot, preferred_element_type=F32)
    hs = [h_ref[r0:r0 + sub, :] + _load_token_tiles(moe_ref, sub, r0) for r0 in starts]
    ns = [_rms(h, gple_ref[...]).astype(BF16) for h in hs]
    gates = [_sigmoid(mm(n, wpg_ref[...])) for n in ns]
    embs = [mm(p_ref[r0:r0 + sub, :].astype(BF16), wple_ref[...]) for r0 in starts]
    for r0, h, gate, emb in zip(starts, hs, gates, embs):
        o_ref[r0:r0 + sub, :] = _rms(h + gate * emb, gfin_ref[...])


def _ple_out(h, moe, p2, gple, wpg, wple, gfin, tm):
    T = h.shape[0]
    row = lambda n: pl.BlockSpec((tm, n), lambda i: (i, 0))
    return pl.pallas_call(
        _ple_out_kernel,
        grid=(T // tm,),
        in_specs=[row(D_MODEL), pl.BlockSpec((tm * ROW_TILES, LANES), lambda i: (i, 0)), row(PLE_DIM),
                  _full(gple.shape), _full(wpg.shape), _full(wple.shape), _full(gfin.shape)],
        out_specs=row(D_MODEL),
        out_shape=jax.ShapeDtypeStruct((T, D_MODEL), F32),
        compiler_params=_cparams(("parallel",), 48),
        name="ple_out",
    )(h, moe, p2, gple, wpg, wple, gfin)


def kernel(x, p, positions, norm_mix, w_in, gla_gate_up_fwd, gla_gate_bias_fwd, gla_gate_up_bwd, gla_gate_bias_bwd, attn_sink, gla_norm, w_branch_attn, w_branch_gla, w_out, norm_ffn, w_router, w_exp_gate, w_exp_up, w_exp_down, norm_ple, w_ple_gate, w_ple, norm_final):
    B, S, D = x.shape
    T = B * S
    depth = w_in.shape[0]
    cap = CAPACITY_FACTOR * S // N_EXPERTS
    R = GLA_GATE_RANK

    posc = positions.reshape(T, 1)
    posr = positions.reshape(1, T)
    inv_freq = ROPE_THETA ** (-jnp.arange(0, ROPE_DIM, 2, dtype=F32) / ROPE_DIM)
    invfl = jnp.tile(inv_freq, LANES // (ROPE_DIM // 2)).reshape(1, LANES)
    invfc = inv_freq.reshape(ROPE_DIM // 2, 1)

    h = x.reshape(T, D)
    for l in range(depth):
        o = 0
        cols = {}
        for name, n in (("q", ATT_Q_DIM), ("k", ATT_KV_DIM), ("v", ATT_KV_DIM), ("gqk", 2 * GLA_KEY_DIM),
                        ("gv", GLA_VAL_DIM), ("gr", GLA_VAL_DIM), ("z", 2 * R), ("gate", 2 * D_MODEL)):
            cols[name] = w_in[l][:, o:o + n].astype(BF16)
            o += n
        per_head = lambda w: w.reshape(-1, GLA_HEADS, GLA_DK).swapaxes(0, 1)
        upf, upb = per_head(gla_gate_up_fwd[l]), per_head(gla_gate_up_bwd[l])
        up = jnp.concatenate([jnp.concatenate([upf, jnp.zeros_like(upf)], axis=2),
                              jnp.concatenate([jnp.zeros_like(upb), upb], axis=2)], axis=1)
        up_hi = up.astype(BF16)
        up_lo = (up - up_hi.astype(F32)).astype(BF16)
        upw = jnp.concatenate([up_hi, up_hi, up_lo, jnp.zeros_like(up_lo)], axis=1)
        wz4 = jnp.tile(cols["z"], (1, LANES // (2 * R)))
        gbias = jnp.concatenate([per_head(gla_gate_bias_fwd[l]), per_head(gla_gate_bias_bwd[l])], axis=2)
        wr = w_router[l]
        wr_hi = wr.astype(BF16)
        wr_lo = (wr - wr_hi.astype(F32)).astype(BF16)
        wr2 = jnp.concatenate([jnp.concatenate([wr_hi, wr_lo], axis=1),
                               jnp.concatenate([wr_hi, jnp.zeros_like(wr_lo)], axis=1)], axis=0)

        qt, k0, k1, vt, gq, gk, gv, gr, z, sga, sgg = _in_proj(
            h, posc, posr, invfl, invfc, norm_mix[l].reshape(1, D), cols["q"].T, cols["k"], cols["v"].T,
            cols["gqk"], cols["gv"], cols["gr"], wz4, cols["gate"], tm=256)

        att = _swa(attn_sink[l], qt, k0.reshape(B, S, -1), k1.reshape(B, S, -1), vt, batch=B, tq=512)
        gla = _gla(z.reshape(B, S, -1), gq.reshape(B, S, -1), gk.reshape(B, S, -1), gv.reshape(B, S, -1),
                   gr.reshape(B, S, -1), upw, gbias, gla_norm[l].reshape(1, -1))

        h1, xn, aff = _mix_out(h, att.reshape(T, -1), gla.reshape(T, -1), sga, sgg,
                               w_branch_attn[l].astype(BF16), w_branch_gla[l].astype(BF16),
                               w_out[l].astype(BF16), norm_ffn[l].reshape(1, D), wr2, tm=512)

        aff3 = aff.reshape(B, S, N_EXPERTS)
        idx = _route(jnp.swapaxes(aff3, 1, 2), cap)
        idx_flat = jnp.swapaxes(idx, 1, 2).reshape(-1)
        xg, wg = _gather(idx_flat, xn.reshape(B, S, SUBLANES, LANES), aff3, cap)
        y = _ffn(xg, wg, w_exp_gate[l], w_exp_up[l], w_exp_down[l])
        moe = _scatter(idx_flat, y.reshape(B, N_EXPERTS, cap, SUBLANES, LANES), S)

        last = l == depth - 1
        gfin = norm_final.reshape(1, D)
        assert last, "the final norm is fused into the last layer's PLE kernel"
        h = _ple_out(h1, moe.reshape(T * ROW_TILES, LANES), p[l].reshape(T, PLE_DIM), norm_ple[l].reshape(1, D),
                     w_ple_gate[l].astype(BF16), w_ple[l].astype(BF16), gfin, tm=512)
    return h.reshape(B, S, D)
```

```python
import functools
import math

import jax
import jax.numpy as jnp
from jax import lax
from jax.experimental import pallas as pl
from jax.experimental.pallas import tpu as pltpu

D_MODEL = 1024
ATT_HEADS = 8
ATT_KV_HEADS = 2
ATT_HEAD_DIM = 64
ATT_GROUP = ATT_HEADS // ATT_KV_HEADS
ATT_Q_DIM = ATT_HEADS * ATT_HEAD_DIM
ATT_KV_DIM = ATT_KV_HEADS * ATT_HEAD_DIM
WINDOW = 128
ROPE_DIM = ATT_HEAD_DIM // 4
ROPE_THETA = 500000.0
GLA_HEADS = 4
GLA_KEY_DIM = D_MODEL // 2
GLA_VAL_DIM = D_MODEL
GLA_DK = GLA_KEY_DIM // GLA_HEADS
GLA_DV = GLA_VAL_DIM // GLA_HEADS
GLA_GATE_RANK = 16
GLA_GATE_NORM = 16.0
GLA_CHUNK = 64
N_EXPERTS = 16
EXPERT_FF = D_MODEL
CAPACITY_FACTOR = 2
PLE_DIM = 256
EPS = 1e-6

LANES = 128
MIB = 1024 * 1024
BF16 = jnp.bfloat16
F32 = jnp.float32
LOG2E = math.log2(math.e)

NT_DIMS = (((1,), (1,)), ((), ()))
TN_DIMS = (((0,), (0,)), ((), ()))


def _cparams(sem, vmem_mib):
    return pltpu.CompilerParams(dimension_semantics=sem, vmem_limit_bytes=vmem_mib * MIB)


def _full(shape):
    n = len(shape)
    return pl.BlockSpec(shape, lambda *_: (0,) * n)


def _rms(x, gain):
    ms = jnp.mean(x * x, axis=-1, keepdims=True)
    return x * lax.rsqrt(ms + EPS) * gain


def _sigmoid(x):
    return 0.5 * jnp.tanh(0.5 * x) + 0.5


SUBLANES = 8
ROW_TILES = D_MODEL // LANES


def _store_token_tiles(ref2d, x, first_row=0):
    rows = x.shape[0]
    for j in range(ROW_TILES):
        ref2d[pl.ds(first_row * ROW_TILES + j, rows, stride=ROW_TILES), :] = x[:, j * LANES:(j + 1) * LANES]


def _load_token_tiles(ref2d, rows, first_row=0):
    return jnp.concatenate([ref2d[pl.ds(first_row * ROW_TILES + j, rows, stride=ROW_TILES), :]
                            for j in range(ROW_TILES)], axis=1)


ROW_WAYS = 2


def _sub_blocks(tile_rows):
    sub = tile_rows // ROW_WAYS
    return sub, [w * sub for w in range(ROW_WAYS)]


def _rope(t, cos_t, sin_t, first_half):
    fwd = pltpu.roll(t, LANES - ROPE_DIM // 2, axis=1)
    bwd = pltpu.roll(t, ROPE_DIM // 2, axis=1)
    return t * cos_t + jnp.where(first_half, fwd, bwd) * sin_t


def _in_proj_kernel(x_ref, posc_ref, posr_ref, invfl_ref, invfc_ref, gain_ref, wqt_ref, wk_ref, wvt_ref,
                    wgqk_ref, wgv_ref, wgr_ref, wz_ref, wgate_ref,
                    qt_ref, k0_ref, k1_ref, vt_ref, gq_ref, gk_ref, gv_ref, gr_ref, z_ref, sga_ref, sgg_ref):
    a = _rms(x_ref[...], gain_ref[...]).astype(BF16)
    half = ROPE_DIM // 2

    qt = lax.dot_general(wqt_ref[...], a, NT_DIMS, preferred_element_type=F32)
    ang_t = invfc_ref[...] * posr_ref[...].astype(F32)
    cos_r, sin_r = jnp.cos(ang_t), jnp.sin(ang_t)
    qscale = ATT_HEAD_DIM ** -0.5 * LOG2E
    for h in range(ATT_HEADS):
        r0 = h * ATT_HEAD_DIM
        t1, t2 = qt[r0:r0 + half], qt[r0 + half:r0 + ROPE_DIM]
        qt_ref[r0:r0 + half, :] = ((t1 * cos_r - t2 * sin_r) * qscale).astype(BF16)
        qt_ref[r0 + half:r0 + ROPE_DIM, :] = ((t2 * cos_r + t1 * sin_r) * qscale).astype(BF16)
        qt_ref[r0 + ROPE_DIM:r0 + ATT_HEAD_DIM, :] = (qt[r0 + ROPE_DIM:r0 + ATT_HEAD_DIM] * qscale).astype(BF16)

    lane = lax.broadcasted_iota(jnp.int32, (1, LANES), 1)
    d = lane % ATT_HEAD_DIM
    first_half = d < half
    in_rope = d < ROPE_DIM
    ang = posc_ref[...].astype(F32) * invfl_ref[...]
    cos_t = jnp.where(in_rope, jnp.cos(ang), 1.0)
    sin_t = jnp.where(first_half, -jnp.sin(ang), jnp.where(in_rope, jnp.sin(ang), 0.0))
    k = _rope(jnp.dot(a, wk_ref[...], preferred_element_type=F32), cos_t, sin_t, first_half).astype(BF16)
    k0_ref[...] = k[:, :ATT_HEAD_DIM]
    k1_ref[...] = k[:, ATT_HEAD_DIM:]
    vt_ref[...] = lax.dot_general(wvt_ref[...], a, NT_DIMS, preferred_element_type=F32).astype(BF16)

    gqk = jnp.dot(a, wgqk_ref[...], preferred_element_type=F32)
    gq_ref[...] = (gqk[:, :GLA_KEY_DIM] * (GLA_DK ** -0.5)).astype(BF16)
    gk_ref[...] = gqk[:, GLA_KEY_DIM:].astype(BF16)
    gv_ref[...] = jnp.dot(a, wgv_ref[...], preferred_element_type=F32).astype(BF16)
    gr_ref[...] = jnp.dot(a, wgr_ref[...], preferred_element_type=F32).astype(BF16)
    z_ref[...] = jnp.dot(a, wz_ref[...], preferred_element_type=F32)
    gates = jnp.dot(a, wgate_ref[...], preferred_element_type=F32)
    sga_ref[...] = _sigmoid(gates[:, :D_MODEL]).astype(BF16)
    sgg_ref[...] = _sigmoid(gates[:, D_MODEL:]).astype(BF16)


def _in_proj(x2, posc, posr, invfl, invfc, gain, wqt, wk, wvt, wgqk, wgv, wgr, wz, wgate, tm):
    T = x2.shape[0]
    row = lambda n: pl.BlockSpec((tm, n), lambda i: (i, 0))
    col = lambda n: pl.BlockSpec((n, tm), lambda i: (0, i))
    row_widths = (ATT_HEAD_DIM, ATT_HEAD_DIM, None, GLA_KEY_DIM, GLA_KEY_DIM, GLA_VAL_DIM,
                  GLA_VAL_DIM, wz.shape[1], D_MODEL, D_MODEL)
    row_dtypes = (BF16,) * 7 + (F32, BF16, BF16)
    out_specs = [col(ATT_Q_DIM)]
    out_shape = [jax.ShapeDtypeStruct((ATT_Q_DIM, T), BF16)]
    for n, dt in zip(row_widths, row_dtypes):
        if n is None:
            out_specs.append(col(ATT_KV_DIM))
            out_shape.append(jax.ShapeDtypeStruct((ATT_KV_DIM, T), BF16))
        else:
            out_specs.append(row(n))
            out_shape.append(jax.ShapeDtypeStruct((T, n), dt))
    consts = (invfl, invfc, gain, wqt, wk, wvt, wgqk, wgv, wgr, wz, wgate)
    return pl.pallas_call(
        _in_proj_kernel,
        grid=(T // tm,),
        in_specs=[row(D_MODEL), row(1), col(1)] + [_full(c.shape) for c in consts],
        out_specs=out_specs,
        out_shape=out_shape,
        compiler_params=_cparams(("parallel",), 56),
        name="in_proj",
    )(x2, posc, posr, *consts)


def _swa_kernel(sink_ref, qt_ref, k0_ref, k1_ref, vt_ref, o_ref, *, tq, seq):
    blk = WINDOW
    span = 3 * blk
    hd = ATT_HEAD_DIM
    n = pl.program_id(1)
    ones = jnp.ones((16, span), BF16)
    for sb in range(tq // blk):
        q0 = n * tq + sb * blk
        start = pl.multiple_of(jnp.clip(q0 - blk, 0, seq - span), blk)
        kj = start + lax.broadcasted_iota(jnp.int32, (span, blk), 0)
        qi = q0 + lax.broadcasted_iota(jnp.int32, (span, blk), 1)
        valid = jnp.abs(qi - kj) <= WINDOW
        outs = []
        for g, k_ref in enumerate((k0_ref, k1_ref)):
            kw = k_ref[0, pl.ds(start, span), :]
            vaug = jnp.concatenate([vt_ref[g * hd:(g + 1) * hd, pl.ds(start, span)], ones], axis=0)
            heads = range(g * ATT_GROUP, (g + 1) * ATT_GROUP)
            qs = jnp.concatenate([qt_ref[h * hd:(h + 1) * hd, sb * blk:(sb + 1) * blk] for h in heads], axis=1)
            s_all = jnp.dot(kw, qs, preferred_element_type=F32)
            for i, h in enumerate(heads):
                s = jnp.where(valid, s_all[:, i * blk:(i + 1) * blk], -jnp.inf)
                sink = sink_ref[h] * LOG2E
                m = jnp.maximum(jnp.max(s, axis=0, keepdims=True), sink)
                e = jnp.exp2(s - m).astype(BF16)
                r = jnp.dot(vaug, e, preferred_element_type=F32)
                den = r[hd:hd + 1] + jnp.exp2(sink - m)
                outs.append(r[:hd] / den)
        for pr in range(ATT_HEADS // 2):
            pair = jnp.concatenate([outs[2 * pr], outs[2 * pr + 1]], axis=0)
            o_ref[0, sb * blk:(sb + 1) * blk, pr * 2 * hd:(pr + 1) * 2 * hd] = pair.T.astype(BF16)


def _swa(sink, qt, k0, k1, vt, batch, tq):
    S = k0.shape[1]
    nq = S // tq
    kspec = pl.BlockSpec((1, S, ATT_HEAD_DIM), lambda b, n: (b, 0, 0))
    return pl.pallas_call(
        functools.partial(_swa_kernel, tq=tq, seq=S),
        grid=(batch, nq),
        in_specs=[pl.BlockSpec(memory_space=pltpu.SMEM),
                  pl.BlockSpec((ATT_Q_DIM, tq), lambda b, n: (0, b * nq + n)),
                  kspec, kspec,
                  pl.BlockSpec((ATT_KV_DIM, S), lambda b, n: (0, b))],
        out_specs=pl.BlockSpec((1, tq, ATT_Q_DIM), lambda b, n: (b, n, 0)),
        out_shape=jax.ShapeDtypeStruct((batch, S, ATT_Q_DIM), BF16),
        compiler_params=_cparams(("parallel", "parallel"), 32),
        name="swa",
    )(sink, qt, k0, k1, vt)


def _log_sigmoid(u):
    return jnp.minimum(u, 0.0) - jnp.log(1.0 + jnp.exp(-jnp.abs(u)))


def _split2(x):
    hi = x.astype(BF16)
    return hi, (x - hi.astype(F32)).astype(BF16)


GLA_WAYS = 4


def _gla_kernel(z_ref, q_ref, k_ref, v_ref, r_ref, upw_ref, bias_ref, gain_ref,
                o_ref, cf_ref, cb_ref, kef_ref, keb_ref, st_ref, s_ref, *, seq):
    L = GLA_CHUNK
    R2 = 2 * GLA_GATE_RANK
    nc = seq // L
    grp = 4 * L
    cpg = grp // L
    dk = GLA_DK
    mm = functools.partial(jnp.dot, preferred_element_type=F32)
    nt = functools.partial(lax.dot_general, dimension_numbers=NT_DIMS, preferred_element_type=F32)

    row = lax.broadcasted_iota(jnp.int32, (grp, grp), 0)
    col = lax.broadcasted_iota(jnp.int32, (grp, grp), 1)
    same = (row // L) == (col // L)
    fwd_mask = same & (col <= row)
    bwd_mask = same & (col > row)
    tri_lo = jnp.where(fwd_mask, 1.0, 0.0).astype(BF16)
    tri_up = jnp.where(same & (col >= row), 1.0, 0.0).astype(BF16)
    lane = lax.broadcasted_iota(jnp.int32, (1, LANES), 1)
    use_lo = (lane >= R2) & (lane < 2 * R2)

    def group_starts(i):
        return [pl.multiple_of((i * GLA_WAYS + w) * grp, grp) for w in range(GLA_WAYS)]

    def cum_body(i, carry):
        r0s = group_starts(i)
        zs = [_split2(z_ref[0, pl.ds(r0, grp), :]) for r0 in r0s]
        us = [mm(jnp.where(use_lo, zl, zh), upw_ref[...]) + bias_ref[...] for zh, zl in zs]
        las = [_split2(_log_sigmoid(u) * (1.0 / GLA_GATE_NORM)) for u in us]
        cfxs = [mm(tri_lo, jnp.concatenate([lh[:, :dk], ll[:, :dk]], axis=1)) for lh, ll in las]
        cbxs = [mm(tri_up, jnp.concatenate([lh[:, dk:], ll[:, dk:]], axis=1)) for lh, ll in las]
        for r0, cfx, cbx in zip(r0s, cfxs, cbxs):
            cf = cfx[:, :dk] + cfx[:, dk:]
            cb = cbx[:, :dk] + cbx[:, dk:]
            cf_ref[pl.ds(r0, grp), :] = cf
            cb_ref[pl.ds(r0, grp), :] = cb
            k = k_ref[0, pl.ds(r0, grp), :].astype(F32)
            for c in range(cpg):
                sl = slice(c * L, (c + 1) * L)
                gf = cf[(c + 1) * L - 1:(c + 1) * L]
                gb = cb[c * L:c * L + 1]
                kef_ref[pl.ds(r0 + c * L, L), :] = (k[sl] * jnp.exp(gf - cf[sl])).astype(BF16)
                keb_ref[pl.ds(r0 + c * L, L), :] = (k[sl] * jnp.exp(gb - cb[sl])).astype(BF16)
        return carry

    lax.fori_loop(0, seq // (grp * GLA_WAYS), cum_body, 0)

    s_ref[...] = jnp.zeros_like(s_ref)
    zero_k = jnp.zeros((L, dk), BF16)

    def state_body(i, carry):
        j = nc - 1 - i
        rf = pl.multiple_of(i * L, L)
        rb = pl.multiple_of(j * L, L)
        vcat = jnp.concatenate([v_ref[0, pl.ds(rf, L), :], v_ref[0, pl.ds(rb, L), :]], axis=0)
        kblk = jnp.concatenate([jnp.concatenate([kef_ref[pl.ds(rf, L), :], zero_k], axis=1),
                                jnp.concatenate([zero_k, keb_ref[pl.ds(rb, L), :]], axis=1)], axis=0)
        kv = lax.dot_general(vcat, kblk, TN_DIMS, preferred_element_type=F32)
        decay = jnp.exp(jnp.concatenate([cf_ref[pl.ds(rf + L - 1, 1), :], cb_ref[pl.ds(rb, 1), :]], axis=1))
        s = s_ref[...]
        st_ref[i, :, 0:dk] = s[:, :dk].astype(BF16)
        st_ref[j, :, dk:2 * dk] = s[:, dk:].astype(BF16)
        s_ref[...] = s * decay + kv
        return carry

    lax.fori_loop(0, nc, state_body, 0, unroll=16)

    def out_body(i, carry):
        r0s = group_starts(i)
        ops = []
        for r0 in r0s:
            q = q_ref[0, pl.ds(r0, grp), :].astype(F32)
            k = k_ref[0, pl.ds(r0, grp), :].astype(F32)
            cf = cf_ref[pl.ds(r0, grp), :]
            cb = cb_ref[pl.ds(r0, grp), :]
            ops.append(((q * jnp.exp(cf)).astype(BF16), (k * jnp.exp(-cf)).astype(BF16),
                        (q * jnp.exp(cb)).astype(BF16), (k * jnp.exp(-cb)).astype(BF16)))
        scores = [(nt(qf, kf), nt(qb, kb)) for qf, kf, qb, kb in ops]
        attns = [jnp.where(fwd_mask, af, jnp.where(bwd_mask, ab, 0.0)).astype(BF16) for af, ab in scores]
        outs = []
        for r0, attn, (qf, _, qb, _) in zip(r0s, attns, ops):
            c0 = r0 // L
            qcat = jnp.concatenate([qf, qb], axis=1)
            inter = jnp.concatenate([nt(qcat[c * L:(c + 1) * L], st_ref[c0 + c]) for c in range(cpg)], axis=0)
            outs.append(mm(attn, v_ref[0, pl.ds(r0, grp), :]) + inter)
        for r0, o in zip(r0s, outs):
            r = r_ref[0, pl.ds(r0, grp), :].astype(F32)
            o_ref[0, pl.ds(r0, grp), :] = (_rms(o, gain_ref[...]) * (r * _sigmoid(r))).astype(BF16)
        return carry

    lax.fori_loop(0, seq // (grp * GLA_WAYS), out_body, 0)


def _gla(z, gq, gk, gv, gr, upw, bias, gain):
    B, S, _ = gq.shape
    nc = S // GLA_CHUNK
    seq_blk = lambda n: pl.BlockSpec((1, S, n), lambda b, h: (b, 0, h))
    head_blk = lambda r, n: pl.BlockSpec((None, r, n), lambda b, h: (h, 0, 0))
    return pl.pallas_call(
        functools.partial(_gla_kernel, seq=S),
        grid=(B, GLA_HEADS),
        in_specs=[pl.BlockSpec((1, S, LANES), lambda b, h: (b, 0, 0)),
                  seq_blk(GLA_DK), seq_blk(GLA_DK), seq_blk(GLA_DV), seq_blk(GLA_DV),
                  head_blk(LANES, 2 * GLA_DK), head_blk(1, 2 * GLA_DK),
                  pl.BlockSpec((1, GLA_DV), lambda b, h: (0, h))],
        out_specs=seq_blk(GLA_DV),
        out_shape=jax.ShapeDtypeStruct((B, S, GLA_VAL_DIM), BF16),
        scratch_shapes=[pltpu.VMEM((S, GLA_DK), F32), pltpu.VMEM((S, GLA_DK), F32),
                        pltpu.VMEM((S, GLA_DK), BF16), pltpu.VMEM((S, GLA_DK), BF16),
                        pltpu.VMEM((nc, GLA_DV, 2 * GLA_DK), BF16),
                        pltpu.VMEM((GLA_DV, 2 * GLA_DK), F32)],
        compiler_params=_cparams(("parallel", "parallel"), 48),
        name="gla",
    )(z, gq, gk, gv, gr, upw, bias, gain)


def _mix_out_kernel(x_ref, a_ref, g_ref, sga_ref, sgg_ref, wa_ref, wb_ref, wo_ref, gain_ref, wr_ref,
                    h_ref, xn_ref, aff_ref):
    sub, starts = _sub_blocks(x_ref.shape[0])
    mm = functools.partial(jnp.dot, preferred_element_type=F32)
    blk = lambda ref, r0: ref[r0:r0 + sub, :]
    y_att = [mm(blk(a_ref, r0), wa_ref[...]) for r0 in starts]
    y_gla = [mm(blk(g_ref, r0), wb_ref[...]) for r0 in starts]
    merged = [(blk(sga_ref, r0).astype(F32) * ya + blk(sgg_ref, r0).astype(F32) * yg).astype(BF16)
              for r0, ya, yg in zip(starts, y_att, y_gla)]
    hs = [blk(x_ref, r0) + mm(m, wo_ref[...]) for r0, m in zip(starts, merged)]
    xns = [_rms(h, gain_ref[...]) for h in hs]
    for r0, h, xn in zip(starts, hs, xns):
        h_ref[r0:r0 + sub, :] = h
        _store_token_tiles(xn_ref, xn, r0)
    his = [xn.astype(BF16) for xn in xns]
    parts = [mm(jnp.concatenate([hi, (xn - hi.astype(F32)).astype(BF16)], axis=1), wr_ref[...])
             for xn, hi in zip(xns, his)]
    for r0, part in zip(starts, parts):
        logits = part[:, :N_EXPERTS] + part[:, N_EXPERTS:]
        e = jnp.exp(logits - jnp.max(logits, axis=-1, keepdims=True))
        aff_ref[r0:r0 + sub, :] = e / jnp.sum(e, axis=-1, keepdims=True)


def _mix_out(x2, a, g, sga, sgg, wa, wb, wo, gain, wr, tm):
    T = x2.shape[0]
    row = lambda n: pl.BlockSpec((tm, n), lambda i: (i, 0))
    return pl.pallas_call(
        _mix_out_kernel,
        grid=(T // tm,),
        in_specs=[row(D_MODEL), row(ATT_Q_DIM), row(GLA_VAL_DIM), row(D_MODEL), row(D_MODEL),
                  _full(wa.shape), _full(wb.shape), _full(wo.shape), _full(gain.shape), _full(wr.shape)],
        out_specs=[row(D_MODEL), pl.BlockSpec((tm * ROW_TILES, LANES), lambda i: (i, 0)), row(N_EXPERTS)],
        out_shape=[jax.ShapeDtypeStruct((T, D_MODEL), F32), jax.ShapeDtypeStruct((T * ROW_TILES, LANES), F32),
                   jax.ShapeDtypeStruct((T, N_EXPERTS), F32)],
        compiler_params=_cparams(("parallel",), 48),
        name="mix_out",
    )(x2, a, g, sga, sgg, wa, wb, wo, gain, wr)


ROUTE_WAYS = 4


def _route_kernel(aff_ref, idx_ref, cum_ref, *, cap, seq):
    E = N_EXPERTS
    aff = aff_ref[0]
    count = lambda mask: jnp.sum(mask.astype(jnp.int32), axis=1, keepdims=True)
    as_float = lambda pattern: lax.bitcast_convert_type(pattern, F32)

    def thr_body(t, pattern):
        cand = pattern | jnp.left_shift(jnp.int32(1), 30 - t)
        return jnp.where(count(aff >= as_float(cand)) >= cap, cand, pattern)

    thr = as_float(lax.fori_loop(0, 31, thr_body, jnp.zeros((E, 1), jnp.int32)))
    above = aff > thr
    tie = aff == thr
    need = cap - count(above)

    pos = lax.broadcasted_iota(jnp.int32, (E, seq), 1)

    def tie_body(t, last):
        cand = last | jnp.left_shift(jnp.int32(1), (seq.bit_length() - 2) - t)
        return jnp.where(count(tie & (pos < cand)) < need, cand, last)

    last = lax.fori_loop(0, seq.bit_length() - 1, tie_body, jnp.zeros((E, 1), jnp.int32))
    sel = (above | (tie & (pos <= last))).astype(BF16)

    nt = seq // LANES
    lrow = lax.broadcasted_iota(jnp.int32, (LANES, LANES), 0)
    lcol = lax.broadcasted_iota(jnp.int32, (LANES, LANES), 1)
    tri = (lrow <= lcol).astype(BF16)
    mm = functools.partial(jnp.dot, preferred_element_type=F32)
    for t in range(nt):
        cum_ref[t * E:(t + 1) * E, :] = mm(sel[:, t * LANES:(t + 1) * LANES], tri)
    tile_of = (lax.broadcasted_iota(jnp.int32, (seq, LANES), 0) // LANES
               == lax.broadcasted_iota(jnp.int32, (seq, LANES), 1)).astype(BF16)
    per_tile = mm(sel, tile_of)
    lane = lax.broadcasted_iota(jnp.int32, (1, LANES), 1)
    far = jnp.float32(2 * seq)
    t_end = jnp.where(lane < nt, mm(per_tile.astype(BF16), tri), far)
    t_start = jnp.where(lane < nt, t_end - per_tile, far)
    pad = jnp.zeros((LANES - E, LANES), F32)
    t_start_cols = jnp.concatenate([jnp.where(lane < nt, t_start, 0.0), pad], axis=0).T

    slot = lax.broadcasted_iota(jnp.int32, (cap, LANES), 0).astype(F32)
    ones = jnp.ones((LANES, LANES), BF16)
    zrows = jnp.zeros((LANES - nt, 2 * LANES), F32)
    for e0 in range(0, E, ROUTE_WAYS):
        es = range(e0, e0 + ROUTE_WAYS)
        tiles, picks, whole = [], [], []
        for e in es:
            absc = cum_ref[pl.ds(e, nt, stride=E), :] + t_start_cols[0:nt, e:e + 1]
            hi = jnp.where(absc >= 256.0, 1.0, 0.0) + jnp.where(absc >= 512.0, 1.0, 0.0)
            lo = absc - 256.0 * hi
            tiles.append(jnp.concatenate([jnp.concatenate([lo, hi], axis=1), zrows], axis=0).astype(BF16))
            done = jnp.where(t_end[e:e + 1] <= slot, 1.0, 0.0)
            whole.append(done)
            picks.append((jnp.where(t_start[e:e + 1] <= slot, 1.0, 0.0) - done).astype(BF16))
        rows = [mm(p, w) for p, w in zip(picks, tiles)]
        votes = [(jnp.where(r[:, :LANES] + 256.0 * r[:, LANES:] <= slot, 1.0, 0.0) + float(LANES) * d).astype(BF16)
                 for r, d in zip(rows, whole)]
        for e, v in zip(es, votes):
            idx_ref[0, :, e:e + 1] = mm(v, ones)[:, e:e + 1].astype(jnp.int32)


def _route(aff_t, cap):
    B, E, S = aff_t.shape
    return pl.pallas_call(
        functools.partial(_route_kernel, cap=cap, seq=S),
        grid=(B,),
        in_specs=[pl.BlockSpec((1, E, S), lambda b: (b, 0, 0))],
        out_specs=pl.BlockSpec((1, cap, E), lambda b: (b, 0, 0)),
        out_shape=jax.ShapeDtypeStruct((B, cap, E), jnp.int32),
        scratch_shapes=[pltpu.VMEM((S // LANES * E, LANES), F32)],
        compiler_params=_cparams(("parallel",), 32),
        name="route",
    )(aff_t)


def _gather_kernel(idx_ref, xn_ref, aff_ref, xg_ref, wg_ref, xs_ref, ws_ref, *, cap):
    b = pl.program_id(0)
    e = pl.program_id(1)
    base = (b * N_EXPERTS + e) * cap

    def body(i, carry):
        t = idx_ref[base + i]
        xs_ref[pl.ds(pl.multiple_of(i * SUBLANES, SUBLANES), SUBLANES), :] = xn_ref[0, t]
        ws_ref[pl.ds(i, 1), :] = aff_ref[0, pl.ds(t, 1), :]
        return carry

    lax.fori_loop(0, cap, body, 0, unroll=8)
    xg_ref[0, 0] = _load_token_tiles(xs_ref, cap).astype(BF16)
    lane = lax.broadcasted_iota(jnp.int32, (cap, N_EXPERTS), 1)
    wg_ref[0, 0] = jnp.sum(jnp.where(lane == e, ws_ref[...], 0.0), axis=1, keepdims=True)


def _gather(idx_flat, xn_tiles, aff, cap):
    B, S = xn_tiles.shape[:2]
    E = N_EXPERTS
    grid_spec = pltpu.PrefetchScalarGridSpec(
        num_scalar_prefetch=1,
        grid=(B, E),
        in_specs=[pl.BlockSpec((1, S, SUBLANES, LANES), lambda b, e, idx: (b, 0, 0, 0)),
                  pl.BlockSpec((1, S, E), lambda b, e, idx: (b, 0, 0))],
        out_specs=[pl.BlockSpec((1, 1, cap, D_MODEL), lambda b, e, idx: (b, e, 0, 0)),
                   pl.BlockSpec((1, 1, cap, 1), lambda b, e, idx: (b, e, 0, 0))],
        scratch_shapes=[pltpu.VMEM((cap * SUBLANES, LANES), F32), pltpu.VMEM((cap, E), F32)],
    )
    return pl.pallas_call(
        functools.partial(_gather_kernel, cap=cap),
        grid_spec=grid_spec,
        out_shape=[jax.ShapeDtypeStruct((B, E, cap, D_MODEL), BF16),
                   jax.ShapeDtypeStruct((B, E, cap, 1), F32)],
        compiler_params=_cparams(("arbitrary", "arbitrary"), 48),
        name="gather",
    )(idx_flat, xn_tiles, aff)


def _ffn_kernel(xg_ref, wg_ref, w1_ref, w2_ref, w3_ref, y_ref, b1_ref, b2_ref, b3_ref):
    @pl.when(pl.program_id(1) == 0)
    def _():
        b1_ref[...] = w1_ref[0].astype(BF16)
        b2_ref[...] = w2_ref[0].astype(BF16)
        b3_ref[...] = w3_ref[0].astype(BF16)

    for i in range(xg_ref.shape[0]):
        xg = xg_ref[i, 0]
        gate = jnp.dot(xg, b1_ref[...], preferred_element_type=F32)
        up = jnp.dot(xg, b2_ref[...], preferred_element_type=F32)
        hid = (gate * _sigmoid(gate) * up).astype(BF16)
        y = jnp.dot(hid, b3_ref[...], preferred_element_type=F32) * wg_ref[i, 0]
        _store_token_tiles(y_ref.at[i, 0], y)


FFN_SEQS_PER_STEP = 2


def _ffn(xg, wg, w1, w2, w3):
    B, E, C, D = xg.shape
    F = w1.shape[-1]
    n = FFN_SEQS_PER_STEP
    return pl.pallas_call(
        _ffn_kernel,
        grid=(E, B // n),
        in_specs=[pl.BlockSpec((n, 1, C, D), lambda e, b: (b, e, 0, 0)),
                  pl.BlockSpec((n, 1, C, 1), lambda e, b: (b, e, 0, 0)),
                  pl.BlockSpec((1, D, F), lambda e, b: (e, 0, 0)),
                  pl.BlockSpec((1, D, F), lambda e, b: (e, 0, 0)),
                  pl.BlockSpec((1, F, D), lambda e, b: (e, 0, 0))],
        out_specs=pl.BlockSpec((n, 1, C * ROW_TILES, LANES), lambda e, b: (b, e, 0, 0)),
        out_shape=jax.ShapeDtypeStruct((B, E, C * ROW_TILES, LANES), F32),
        scratch_shapes=[pltpu.VMEM((D, F), BF16), pltpu.VMEM((D, F), BF16), pltpu.VMEM((F, D), BF16)],
        compiler_params=_cparams(("arbitrary", "arbitrary"), 56),
        name="ffn",
    )(xg, wg, w1, w2, w3)


SCATTER_BATCH = 16


def _scatter_kernel(idx_ref, y_ref, o_ref, *, cap):
    b = pl.program_id(0)
    e = pl.program_id(1)
    base = (b * N_EXPERTS + e) * cap

    @pl.when(e == 0)
    def _():
        o_ref[...] = jnp.zeros_like(o_ref)

    def body(blk, carry):
        i0 = blk * SCATTER_BATCH
        ts = [idx_ref[base + i0 + u] for u in range(SCATTER_BATCH)]
        new = [o_ref[0, ts[u]] + y_ref[0, 0, i0 + u] for u in range(SCATTER_BATCH)]
        for u in range(SCATTER_BATCH):
            o_ref[0, ts[u]] = new[u]
        return carry

    lax.fori_loop(0, cap // SCATTER_BATCH, body, 0)


def _scatter(idx_flat, y_tiles, seq):
    B, E, C = y_tiles.shape[:3]
    grid_spec = pltpu.PrefetchScalarGridSpec(
        num_scalar_prefetch=1,
        grid=(B, E),
        in_specs=[pl.BlockSpec((1, 1, C, SUBLANES, LANES), lambda b, e, idx: (b, e, 0, 0, 0))],
        out_specs=pl.BlockSpec((1, seq, SUBLANES, LANES), lambda b, e, idx: (b, 0, 0, 0)),
    )
    return pl.pallas_call(
        functools.partial(_scatter_kernel, cap=C),
        grid_spec=grid_spec,
        out_shape=jax.ShapeDtypeStruct((B, seq, SUBLANES, LANES), F32),
        compiler_params=_cparams(("arbitrary", "arbitrary"), 48),
        name="scatter",
    )(idx_flat, y_tiles)


def _ple_out_kernel(h_ref, moe_ref, p_ref, gple_ref, wpg_ref, wple_ref, gfin_ref, o_ref):
    sub, starts = _sub_blocks(h_ref.shape[0])
    mm = functools.partial(jnp.dot, preferred_element_type=F32)
    hs = [h_ref[r0:r0 + sub, :] + _load_token_tiles(moe_ref, sub, r0) for r0 in starts]
    ns = [_rms(h, gple_ref[...]).astype(BF16) for h in hs]
    gates = [_sigmoid(mm(n, wpg_ref[...])) for n in ns]
    embs = [mm(p_ref[r0:r0 + sub, :].astype(BF16), wple_ref[...]) for r0 in starts]
    for r0, h, gate, emb in zip(starts, hs, gates, embs):
        o_ref[r0:r0 + sub, :] = _rms(h + gate * emb, gfin_ref[...])


def _ple_out(h, moe, p2, gple, wpg, wple, gfin, tm):
    T = h.shape[0]
    row = lambda n: pl.BlockSpec((tm, n), lambda i: (i, 0))
    return pl.pallas_call(
        _ple_out_kernel,
        grid=(T // tm,),
        in_specs=[row(D_MODEL), pl.BlockSpec((tm * ROW_TILES, LANES), lambda i: (i, 0)), row(PLE_DIM),
                  _full(gple.shape), _full(wpg.shape), _full(wple.shape), _full(gfin.shape)],
        out_specs=row(D_MODEL),
        out_shape=jax.ShapeDtypeStruct((T, D_MODEL), F32),
        compiler_params=_cparams(("parallel",), 48),
        name="ple_out",
    )(h, moe, p2, gple, wpg, wple, gfin)


def kernel(x, p, positions, norm_mix, w_in, gla_gate_up_fwd, gla_gate_bias_fwd, gla_gate_up_bwd, gla_gate_bias_bwd, attn_sink, gla_norm, w_branch_attn, w_branch_gla, w_out, norm_ffn, w_router, w_exp_gate, w_exp_up, w_exp_down, norm_ple, w_ple_gate, w_ple, norm_final):
    B, S, D = x.shape
    T = B * S
    depth = w_in.shape[0]
    cap = CAPACITY_FACTOR * S // N_EXPERTS
    R = GLA_GATE_RANK

    posc = positions.reshape(T, 1)
    posr = positions.reshape(1, T)
    inv_freq = ROPE_THETA ** (-jnp.arange(0, ROPE_DIM, 2, dtype=F32) / ROPE_DIM)
    invfl = jnp.tile(inv_freq, LANES // (ROPE_DIM // 2)).reshape(1, LANES)
    invfc = inv_freq.reshape(ROPE_DIM // 2, 1)

    h = x.reshape(T, D)
    for l in range(depth):
        o = 0
        cols = {}
        for name, n in (("q", ATT_Q_DIM), ("k", ATT_KV_DIM), ("v", ATT_KV_DIM), ("gqk", 2 * GLA_KEY_DIM),
                        ("gv", GLA_VAL_DIM), ("gr", GLA_VAL_DIM), ("z", 2 * R), ("gate", 2 * D_MODEL)):
            cols[name] = w_in[l][:, o:o + n].astype(BF16)
            o += n
        per_head = lambda w: w.reshape(-1, GLA_HEADS, GLA_DK).swapaxes(0, 1)
        upf, upb = per_head(gla_gate_up_fwd[l]), per_head(gla_gate_up_bwd[l])
        up = jnp.concatenate([jnp.concatenate([upf, jnp.zeros_like(upf)], axis=2),
                              jnp.concatenate([jnp.zeros_like(upb), upb], axis=2)], axis=1)
        up_hi = up.astype(BF16)
        up_lo = (up - up_hi.astype(F32)).astype(BF16)
        upw = jnp.concatenate([up_hi, up_hi, up_lo, jnp.zeros_like(up_lo)], axis=1)
        wz4 = jnp.tile(cols["z"], (1, LANES // (2 * R)))
        gbias = jnp.concatenate([per_head(gla_gate_bias_fwd[l]), per_head(gla_gate_bias_bwd[l])], axis=2)
        wr = w_router[l]
        wr_hi = wr.astype(BF16)
        wr_lo = (wr - wr_hi.astype(F32)).astype(BF16)
        wr2 = jnp.concatenate([jnp.concatenate([wr_hi, wr_lo], axis=1),
                               jnp.concatenate([wr_hi, jnp.zeros_like(wr_lo)], axis=1)], axis=0)

        qt, k0, k1, vt, gq, gk, gv, gr, z, sga, sgg = _in_proj(
            h, posc, posr, invfl, invfc, norm_mix[l].reshape(1, D), cols["q"].T, cols["k"], cols["v"].T,
            cols["gqk"], cols["gv"], cols["gr"], wz4, cols["gate"], tm=256)

        att = _swa(attn_sink[l], qt, k0.reshape(B, S, -1), k1.reshape(B, S, -1), vt, batch=B, tq=512)
        gla = _gla(z.reshape(B, S, -1), gq.reshape(B, S, -1), gk.reshape(B, S, -1), gv.reshape(B, S, -1),
                   gr.reshape(B, S, -1), upw, gbias, gla_norm[l].reshape(1, -1))

        h1, xn, aff = _mix_out(h, att.reshape(T, -1), gla.reshape(T, -1), sga, sgg,
                               w_branch_attn[l].astype(BF16), w_branch_gla[l].astype(BF16),
                               w_out[l].astype(BF16), norm_ffn[l].reshape(1, D), wr2, tm=512)

        aff3 = aff.reshape(B, S, N_EXPERTS)
        idx = _route(jnp.swapaxes(aff3, 1, 2), cap)
        idx_flat = jnp.swapaxes(idx, 1, 2).reshape(-1)
        xg, wg = _gather(idx_flat, xn.reshape(B, S, SUBLANES, LANES), aff3, cap)
        y = _ffn(xg, wg, w_exp_gate[l], w_exp_up[l], w_exp_down[l])
        moe = _scatter(idx_flat, y.reshape(B, N_EXPERTS, cap, SUBLANES, LANES), S)

        last = l == depth - 1
        gfin = norm_final.reshape(1, D)
        assert last, "the final norm is fused into the last layer's PLE kernel"
        h = _ple_out(h1, moe.reshape(T * ROW_TILES, LANES), p[l].reshape(T, PLE_DIM), norm_ple[l].reshape(1, D),
                     w_ple_gate[l].astype(BF16), w_ple[l].astype(BF16), gfin, tm=512)
    return h.reshape(B, S, D)
```

```python
import functools
import math

import jax
import jax.numpy as jnp
from jax import lax
from jax.experimental import pallas as pl
from jax.experimental.pallas import tpu as pltpu

D_MODEL = 1024
ATT_HEADS = 8
ATT_KV_HEADS = 2
ATT_HEAD_DIM = 64
ATT_GROUP = ATT_HEADS // ATT_KV_HEADS
ATT_Q_DIM = ATT_HEADS * ATT_HEAD_DIM
ATT_KV_DIM = ATT_KV_HEADS * ATT_HEAD_DIM
WINDOW = 128
ROPE_DIM = ATT_HEAD_DIM // 4
ROPE_THETA = 500000.0
GLA_HEADS = 4
GLA_KEY_DIM = D_MODEL // 2
GLA_VAL_DIM = D_MODEL
GLA_DK = GLA_KEY_DIM // GLA_HEADS
GLA_DV = GLA_VAL_DIM // GLA_HEADS
GLA_GATE_RANK = 16
GLA_GATE_NORM = 16.0
GLA_CHUNK = 64
N_EXPERTS = 16
EXPERT_FF = D_MODEL
CAPACITY_FACTOR = 2
PLE_DIM = 256
EPS = 1e-6

LANES = 128
MIB = 1024 * 1024
BF16 = jnp.bfloat16
F32 = jnp.float32
LOG2E = math.log2(math.e)

NT_DIMS = (((1,), (1,)), ((), ()))
TN_DIMS = (((0,), (0,)), ((), ()))


def _cparams(sem, vmem_mib):
    return pltpu.CompilerParams(dimension_semantics=sem, vmem_limit_bytes=vmem_mib * MIB)


def _full(shape):
    n = len(shape)
    return pl.BlockSpec(shape, lambda *_: (0,) * n)


def _rms(x, gain):
    ms = jnp.mean(x * x, axis=-1, keepdims=True)
    return x * lax.rsqrt(ms + EPS) * gain


def _sigmoid(x):
    return 0.5 * jnp.tanh(0.5 * x) + 0.5


SUBLANES = 8
ROW_TILES = D_MODEL // LANES


def _store_token_tiles(ref2d, x, first_row=0):
    rows = x.shape[0]
    for j in range(ROW_TILES):
        ref2d[pl.ds(first_row * ROW_TILES + j, rows, stride=ROW_TILES), :] = x[:, j * LANES:(j + 1) * LANES]


def _load_token_tiles(ref2d, rows, first_row=0):
    return jnp.concatenate([ref2d[pl.ds(first_row * ROW_TILES + j, rows, stride=ROW_TILES), :]
                            for j in range(ROW_TILES)], axis=1)


ROW_WAYS = 2


def _sub_blocks(tile_rows):
    sub = tile_rows // ROW_WAYS
    return sub, [w * sub for w in range(ROW_WAYS)]


def _rope(t, cos_t, sin_t, first_half):
    fwd = pltpu.roll(t, LANES - ROPE_DIM // 2, axis=1)
    bwd = pltpu.roll(t, ROPE_DIM // 2, axis=1)
    return t * cos_t + jnp.where(first_half, fwd, bwd) * sin_t


def _in_proj_kernel(x_ref, posc_ref, posr_ref, invfl_ref, invfc_ref, gain_ref, wqt_ref, wk_ref, wvt_ref,
                    wgqk_ref, wgv_ref, wgr_ref, wz_ref, wgate_ref,
                    qt_ref, k0_ref, k1_ref, vt_ref, gq_ref, gk_ref, gv_ref, gr_ref, z_ref, sga_ref, sgg_ref):
    a = _rms(x_ref[...], gain_ref[...]).astype(BF16)
    half = ROPE_DIM // 2

    qt = lax.dot_general(wqt_ref[...], a, NT_DIMS, preferred_element_type=F32)
    ang_t = invfc_ref[...] * posr_ref[...].astype(F32)
    cos_r, sin_r = jnp.cos(ang_t), jnp.sin(ang_t)
    qscale = ATT_HEAD_DIM ** -0.5 * LOG2E
    for h in range(ATT_HEADS):
        r0 = h * ATT_HEAD_DIM
        t1, t2 = qt[r0:r0 + half], qt[r0 + half:r0 + ROPE_DIM]
        qt_ref[r0:r0 + half, :] = ((t1 * cos_r - t2 * sin_r) * qscale).astype(BF16)
        qt_ref[r0 + half:r0 + ROPE_DIM, :] = ((t2 * cos_r + t1 * sin_r) * qscale).astype(BF16)
        qt_ref[r0 + ROPE_DIM:r0 + ATT_HEAD_DIM, :] = (qt[r0 + ROPE_DIM:r0 + ATT_HEAD_DIM] * qscale).astype(BF16)

    lane = lax.broadcasted_iota(jnp.int32, (1, LANES), 1)
    d = lane % ATT_HEAD_DIM
    first_half = d < half
    in_rope = d < ROPE_DIM
    ang = posc_ref[...].astype(F32) * invfl_ref[...]
    cos_t = jnp.where(in_rope, jnp.cos(ang), 1.0)
    sin_t = jnp.where(first_half, -jnp.sin(ang), jnp.where(in_rope, jnp.sin(ang), 0.0))
    k = _rope(jnp.dot(a, wk_ref[...], preferred_element_type=F32), cos_t, sin_t, first_half).astype(BF16)
    k0_ref[...] = k[:, :ATT_HEAD_DIM]
    k1_ref[...] = k[:, ATT_HEAD_DIM:]
    vt_ref[...] = lax.dot_general(wvt_ref[...], a, NT_DIMS, preferred_element_type=F32).astype(BF16)

    gqk = jnp.dot(a, wgqk_ref[...], preferred_element_type=F32)
    gq_ref[...] = (gqk[:, :GLA_KEY_DIM] * (GLA_DK ** -0.5)).astype(BF16)
    gk_ref[...] = gqk[:, GLA_KEY_DIM:].astype(BF16)
    gv_ref[...] = jnp.dot(a, wgv_ref[...], preferred_element_type=F32).astype(BF16)
    gr_ref[...] = jnp.dot(a, wgr_ref[...], preferred_element_type=F32).astype(BF16)
    z_ref[...] = jnp.dot(a, wz_ref[...], preferred_element_type=F32)
    gates = jnp.dot(a, wgate_ref[...], preferred_element_type=F32)
    sga_ref[...] = _sigmoid(gates[:, :D_MODEL]).astype(BF16)
    sgg_ref[...] = _sigmoid(gates[:, D_MODEL:]).astype(BF16)


def _in_proj(x2, posc, posr, invfl, invfc, gain, wqt, wk, wvt, wgqk, wgv, wgr, wz, wgate, tm):
    T = x2.shape[0]
    row = lambda n: pl.BlockSpec((tm, n), lambda i: (i, 0))
    col = lambda n: pl.BlockSpec((n, tm), lambda i: (0, i))
    row_widths = (ATT_HEAD_DIM, ATT_HEAD_DIM, None, GLA_KEY_DIM, GLA_KEY_DIM, GLA_VAL_DIM,
                  GLA_VAL_DIM, wz.shape[1], D_MODEL, D_MODEL)
    row_dtypes = (BF16,) * 7 + (F32, BF16, BF16)
    out_specs = [col(ATT_Q_DIM)]
    out_shape = [jax.ShapeDtypeStruct((ATT_Q_DIM, T), BF16)]
    for n, dt in zip(row_widths, row_dtypes):
        if n is None:
            out_specs.append(col(ATT_KV_DIM))
            out_shape.append(jax.ShapeDtypeStruct((ATT_KV_DIM, T), BF16))
        else:
            out_specs.append(row(n))
            out_shape.append(jax.ShapeDtypeStruct((T, n), dt))
    consts = (invfl, invfc, gain, wqt, wk, wvt, wgqk, wgv, wgr, wz, wgate)
    return pl.pallas_call(
        _in_proj_kernel,
        grid=(T // tm,),
        in_specs=[row(D_MODEL), row(1), col(1)] + [_full(c.shape) for c in consts],
        out_specs=out_specs,
        out_shape=out_shape,
        compiler_params=_cparams(("parallel",), 56),
        name="in_proj",
    )(x2, posc, posr, *consts)


def _swa_kernel(sink_ref, qt_ref, k0_ref, k1_ref, vt_ref, o_ref, *, tq, seq):
    blk = WINDOW
    span = 3 * blk
    hd = ATT_HEAD_DIM
    n = pl.program_id(1)
    ones = jnp.ones((16, span), BF16)
    for sb in range(tq // blk):
        q0 = n * tq + sb * blk
        start = pl.multiple_of(jnp.clip(q0 - blk, 0, seq - span), blk)
        kj = start + lax.broadcasted_iota(jnp.int32, (span, blk), 0)
        qi = q0 + lax.broadcasted_iota(jnp.int32, (span, blk), 1)
        valid = jnp.abs(qi - kj) <= WINDOW
        outs = []
        for g, k_ref in enumerate((k0_ref, k1_ref)):
            kw = k_ref[0, pl.ds(start, span), :]
            vaug = jnp.concatenate([vt_ref[g * hd:(g + 1) * hd, pl.ds(start, span)], ones], axis=0)
            heads = range(g * ATT_GROUP, (g + 1) * ATT_GROUP)
            qs = jnp.concatenate([qt_ref[h * hd:(h + 1) * hd, sb * blk:(sb + 1) * blk] for h in heads], axis=1)
            s_all = jnp.dot(kw, qs, preferred_element_type=F32)
            for i, h in enumerate(heads):
                s = jnp.where(valid, s_all[:, i * blk:(i + 1) * blk], -jnp.inf)
                sink = sink_ref[h] * LOG2E
                m = jnp.maximum(jnp.max(s, axis=0, keepdims=True), sink)
                e = jnp.exp2(s - m).astype(BF16)
                r = jnp.dot(vaug, e, preferred_element_type=F32)
                den = r[hd:hd + 1] + jnp.exp2(sink - m)
                outs.append(r[:hd] / den)
        for pr in range(ATT_HEADS // 2):
            pair = jnp.concatenate([outs[2 * pr], outs[2 * pr + 1]], axis=0)
            o_ref[0, sb * blk:(sb + 1) * blk, pr * 2 * hd:(pr + 1) * 2 * hd] = pair.T.astype(BF16)


def _swa(sink, qt, k0, k1, vt, batch, tq):
    S = k0.shape[1]
    nq = S // tq
    kspec = pl.BlockSpec((1, S, ATT_HEAD_DIM), lambda b, n: (b, 0, 0))
    return pl.pallas_call(
        functools.partial(_swa_kernel, tq=tq, seq=S),
        grid=(batch, nq),
        in_specs=[pl.BlockSpec(memory_space=pltpu.SMEM),
                  pl.BlockSpec((ATT_Q_DIM, tq), lambda b, n: (0, b * nq + n)),
                  kspec, kspec,
                  pl.BlockSpec((ATT_KV_DIM, S), lambda b, n: (0, b))],
        out_specs=pl.BlockSpec((1, tq, ATT_Q_DIM), lambda b, n: (b, n, 0)),
        out_shape=jax.ShapeDtypeStruct((batch, S, ATT_Q_DIM), BF16),
        compiler_params=_cparams(("parallel", "parallel"), 32),
        name="swa",
    )(sink, qt, k0, k1, vt)


def _log_sigmoid(u):
    return jnp.minimum(u, 0.0) - jnp.log(1.0 + jnp.exp(-jnp.abs(u)))


def _split2(x):
    hi = x.astype(BF16)
    return hi, (x - hi.astype(F32)).astype(BF16)


GLA_WAYS = 4


def _gla_kernel(z_ref, q_ref, k_ref, v_ref, r_ref, upw_ref, bias_ref, gain_ref,
                o_ref, cf_ref, cb_ref, kef_ref, keb_ref, st_ref, s_ref, *, seq):
    L = GLA_CHUNK
    R2 = 2 * GLA_GATE_RANK
    nc = seq // L
    grp = 4 * L
    cpg = grp // L
    dk = GLA_DK
    mm = functools.partial(jnp.dot, preferred_element_type=F32)
    nt = functools.partial(lax.dot_general, dimension_numbers=NT_DIMS, preferred_element_type=F32)

    row = lax.broadcasted_iota(jnp.int32, (grp, grp), 0)
    col = lax.broadcasted_iota(jnp.int32, (grp, grp), 1)
    same = (row // L) == (col // L)
    fwd_mask = same & (col <= row)
    bwd_mask = same & (col > row)
    tri_lo = jnp.where(fwd_mask, 1.0, 0.0).astype(BF16)
    tri_up = jnp.where(same & (col >= row), 1.0, 0.0).astype(BF16)
    lane = lax.broadcasted_iota(jnp.int32, (1, LANES), 1)
    use_lo = (lane >= R2) & (lane < 2 * R2)

    def group_starts(i):
        return [pl.multiple_of((i * GLA_WAYS + w) * grp, grp) for w in range(GLA_WAYS)]

    def cum_body(i, carry):
        r0s = group_starts(i)
        zs = [_split2(z_ref[0, pl.ds(r0, grp), :]) for r0 in r0s]
        us = [mm(jnp.where(use_lo, zl, zh), upw_ref[...]) + bias_ref[...] for zh, zl in zs]
        las = [_split2(_log_sigmoid(u) * (1.0 / GLA_GATE_NORM)) for u in us]
        cfxs = [mm(tri_lo, jnp.concatenate([lh[:, :dk], ll[:, :dk]], axis=1)) for lh, ll in las]
        cbxs = [mm(tri_up, jnp.concatenate([lh[:, dk:], ll[:, dk:]], axis=1)) for lh, ll in las]
        for r0, cfx, cbx in zip(r0s, cfxs, cbxs):
            cf = cfx[:, :dk] + cfx[:, dk:]
            cb = cbx[:, :dk] + cbx[:, dk:]
            cf_ref[pl.ds(r0, grp), :] = cf
            cb_ref[pl.ds(r0, grp), :] = cb
            k = k_ref[0, pl.ds(r0, grp), :].astype(F32)
            for c in range(cpg):
                sl = slice(c * L, (c + 1) * L)
                gf = cf[(c + 1) * L - 1:(c + 1) * L]
                gb = cb[c * L:c * L + 1]
                kef_ref[pl.ds(r0 + c * L, L), :] = (k[sl] * jnp.exp(gf - cf[sl])).astype(BF16)
                keb_ref[pl.ds(r0 + c * L, L), :] = (k[sl] * jnp.exp(gb - cb[sl])).astype(BF16)
        return carry

    lax.fori_loop(0, seq // (grp * GLA_WAYS), cum_body, 0)

    s_ref[...] = jnp.zeros_like(s_ref)
    zero_k = jnp.zeros((L, dk), BF16)

    def state_body(i, carry):
        j = nc - 1 - i
        rf = pl.multiple_of(i * L, L)
        rb = pl.multiple_of(j * L, L)
        vcat = jnp.concatenate([v_ref[0, pl.ds(rf, L), :], v_ref[0, pl.ds(rb, L), :]], axis=0)
        kblk = jnp.concatenate([jnp.concatenate([kef_ref[pl.ds(rf, L), :], zero_k], axis=1),
                                jnp.concatenate([zero_k, keb_ref[pl.ds(rb, L), :]], axis=1)], axis=0)
        kv = lax.dot_general(vcat, kblk, TN_DIMS, preferred_element_type=F32)
        decay = jnp.exp(jnp.concatenate([cf_ref[pl.ds(rf + L - 1, 1), :], cb_ref[pl.ds(rb, 1), :]], axis=1))
        s = s_ref[...]
        st_ref[i, :, 0:dk] = s[:, :dk].astype(BF16)
        st_ref[j, :, dk:2 * dk] = s[:, dk:].astype(BF16)
        s_ref[...] = s * decay + kv
        return carry

    lax.fori_loop(0, nc, state_body, 0, unroll=16)

    def out_body(i, carry):
        r0s = group_starts(i)
        ops = []
        for r0 in r0s:
            q = q_ref[0, pl.ds(r0, grp), :].astype(F32)
            k = k_ref[0, pl.ds(r0, grp), :].astype(F32)
            cf = cf_ref[pl.ds(r0, grp), :]
            cb = cb_ref[pl.ds(r0, grp), :]
            ops.append(((q * jnp.exp(cf)).astype(BF16), (k * jnp.exp(-cf)).astype(BF16),
                        (q * jnp.exp(cb)).astype(BF16), (k * jnp.exp(-cb)).astype(BF16)))
        scores = [(nt(qf, kf), nt(qb, kb)) for qf, kf, qb, kb in ops]
        attns = [jnp.where(fwd_mask, af, jnp.where(bwd_mask, ab, 0.0)).astype(BF16) for af, ab in scores]
        outs = []
        for r0, attn, (qf, _, qb, _) in zip(r0s, attns, ops):
            c0 = r0 // L
            qcat = jnp.concatenate([qf, qb], axis=1)
            inter = jnp.concatenate([nt(qcat[c * L:(c + 1) * L], st_ref[c0 + c]) for c in range(cpg)], axis=0)
            outs.append(mm(attn, v_ref[0, pl.ds(r0, grp), :]) + inter)
        for r0, o in zip(r0s, outs):
            r = r_ref[0, pl.ds(r0, grp), :].astype(F32)
            o_ref[0, pl.ds(r0, grp), :] = (_rms(o, gain_ref[...]) * (r * _sigmoid(r))).astype(BF16)
        return carry

    lax.fori_loop(0, seq // (grp * GLA_WAYS), out_body, 0)


def _gla(z, gq, gk, gv, gr, upw, bias, gain):
    B, S, _ = gq.shape
    nc = S // GLA_CHUNK
    seq_blk = lambda n: pl.BlockSpec((1, S, n), lambda b, h: (b, 0, h))
    head_blk = lambda r, n: pl.BlockSpec((None, r, n), lambda b, h: (h, 0, 0))
    return pl.pallas_call(
        functools.partial(_gla_kernel, seq=S),
        grid=(B, GLA_HEADS),
        in_specs=[pl.BlockSpec((1, S, LANES), lambda b, h: (b, 0, 0)),
                  seq_blk(GLA_DK), seq_blk(GLA_DK), seq_blk(GLA_DV), seq_blk(GLA_DV),
                  head_blk(LANES, 2 * GLA_DK), head_blk(1, 2 * GLA_DK),
                  pl.BlockSpec((1, GLA_DV), lambda b, h: (0, h))],
        out_specs=seq_blk(GLA_DV),
        out_shape=jax.ShapeDtypeStruct((B, S, GLA_VAL_DIM), BF16),
        scratch_shapes=[pltpu.VMEM((S, GLA_DK), F32), pltpu.VMEM((S, GLA_DK), F32),
                        pltpu.VMEM((S, GLA_DK), BF16), pltpu.VMEM((S, GLA_DK), BF16),
                        pltpu.VMEM((nc, GLA_DV, 2 * GLA_DK), BF16),
                        pltpu.VMEM((GLA_DV, 2 * GLA_DK), F32)],
        compiler_params=_cparams(("parallel", "parallel"), 48),
        name="gla",
    )(z, gq, gk, gv, gr, upw, bias, gain)


def _mix_out_kernel(x_ref, a_ref, g_ref, sga_ref, sgg_ref, wa_ref, wb_ref, wo_ref, gain_ref, wr_ref,
                    h_ref, xn_ref, aff_ref):
    sub, starts = _sub_blocks(x_ref.shape[0])
    mm = functools.partial(jnp.dot, preferred_element_type=F32)
    blk = lambda ref, r0: ref[r0:r0 + sub, :]
    y_att = [mm(blk(a_ref, r0), wa_ref[...]) for r0 in starts]
    y_gla = [mm(blk(g_ref, r0), wb_ref[...]) for r0 in starts]
    merged = [(blk(sga_ref, r0).astype(F32) * ya + blk(sgg_ref, r0).astype(F32) * yg).astype(BF16)
              for r0, ya, yg in zip(starts, y_att, y_gla)]
    hs = [blk(x_ref, r0) + mm(m, wo_ref[...]) for r0, m in zip(starts, merged)]
    xns = [_rms(h, gain_ref[...]) for h in hs]
    for r0, h, xn in zip(starts, hs, xns):
        h_ref[r0:r0 + sub, :] = h
        _store_token_tiles(xn_ref, xn, r0)
    his = [xn.astype(BF16) for xn in xns]
    parts = [mm(jnp.concatenate([hi, (xn - hi.astype(F32)).astype(BF16)], axis=1), wr_ref[...])
             for xn, hi in zip(xns, his)]
    for r0, part in zip(starts, parts):
        logits = part[:, :N_EXPERTS] + part[:, N_EXPERTS:]
        e = jnp.exp(logits - jnp.max(logits, axis=-1, keepdims=True))
        aff_ref[r0:r0 + sub, :] = e / jnp.sum(e, axis=-1, keepdims=True)


def _mix_out(x2, a, g, sga, sgg, wa, wb, wo, gain, wr, tm):
    T = x2.shape[0]
    row = lambda n: pl.BlockSpec((tm, n), lambda i: (i, 0))
    return pl.pallas_call(
        _mix_out_kernel,
        grid=(T // tm,),
        in_specs=[row(D_MODEL), row(ATT_Q_DIM), row(GLA_VAL_DIM), row(D_MODEL), row(D_MODEL),
                  _full(wa.shape), _full(wb.shape), _full(wo.shape), _full(gain.shape), _full(wr.shape)],
        out_specs=[row(D_MODEL), pl.BlockSpec((tm * ROW_TILES, LANES), lambda i: (i, 0)), row(N_EXPERTS)],
        out_shape=[jax.ShapeDtypeStruct((T, D_MODEL), F32), jax.ShapeDtypeStruct((T * ROW_TILES, LANES), F32),
                   jax.ShapeDtypeStruct((T, N_EXPERTS), F32)],
        compiler_params=_cparams(("parallel",), 48),
        name="mix_out",
    )(x2, a, g, sga, sgg, wa, wb, wo, gain, wr)


ROUTE_WAYS = 4


def _route_kernel(aff_ref, idx_ref, cum_ref, *, cap, seq):
    E = N_EXPERTS
    aff = aff_ref[0]
    count = lambda mask: jnp.sum(mask.astype(jnp.int32), axis=1, keepdims=True)
    as_float = lambda pattern: lax.bitcast_convert_type(pattern, F32)

    def thr_body(t, pattern):
        cand = pattern | jnp.left_shift(jnp.int32(1), 30 - t)
        return jnp.where(count(aff >= as_float(cand)) >= cap, cand, pattern)

    thr = as_float(lax.fori_loop(0, 31, thr_body, jnp.zeros((E, 1), jnp.int32)))
    above = aff > thr
    tie = aff == thr
    need = cap - count(above)

    pos = lax.broadcasted_iota(jnp.int32, (E, seq), 1)

    def tie_body(t, last):
        cand = last | jnp.left_shift(jnp.int32(1), (seq.bit_length() - 2) - t)
        return jnp.where(count(tie & (pos < cand)) < need, cand, last)

    last = lax.fori_loop(0, seq.bit_length() - 1, tie_body, jnp.zeros((E, 1), jnp.int32))
    sel = (above | (tie & (pos <= last))).astype(BF16)

    nt = seq // LANES
    lrow = lax.broadcasted_iota(jnp.int32, (LANES, LANES), 0)
    lcol = lax.broadcasted_iota(jnp.int32, (LANES, LANES), 1)
    tri = (lrow <= lcol).astype(BF16)
    mm = functools.partial(jnp.dot, preferred_element_type=F32)
    for t in range(nt):
        cum_ref[t * E:(t + 1) * E, :] = mm(sel[:, t * LANES:(t + 1) * LANES], tri)
    tile_of = (lax.broadcasted_iota(jnp.int32, (seq, LANES), 0) // LANES
               == lax.broadcasted_iota(jnp.int32, (seq, LANES), 1)).astype(BF16)
    per_tile = mm(sel, tile_of)
    lane = lax.broadcasted_iota(jnp.int32, (1, LANES), 1)
    far = jnp.float32(2 * seq)
    t_end = jnp.where(lane < nt, mm(per_tile.astype(BF16), tri), far)
    t_start = jnp.where(lane < nt, t_end - per_tile, far)
    pad = jnp.zeros((LANES - E, LANES), F32)
    t_start_cols = jnp.concatenate([jnp.where(lane < nt, t_start, 0.0), pad], axis=0).T

    slot = lax.broadcasted_iota(jnp.int32, (cap, LANES), 0).astype(F32)
    ones = jnp.ones((LANES, LANES), BF16)
    zrows = jnp.zeros((LANES - nt, 2 * LANES), F32)
    for e0 in range(0, E, ROUTE_WAYS):
        es = range(e0, e0 + ROUTE_WAYS)
        tiles, picks, whole = [], [], []
        for e in es:
            absc = cum_ref[pl.ds(e, nt, stride=E), :] + t_start_cols[0:nt, e:e + 1]
            hi = jnp.where(absc >= 256.0, 1.0, 0.0) + jnp.where(absc >= 512.0, 1.0, 0.0)
            lo = absc - 256.0 * hi
            tiles.append(jnp.concatenate([jnp.concatenate([lo, hi], axis=1), zrows], axis=0).astype(BF16))
            done = jnp.where(t_end[e:e + 1] <= slot, 1.0, 0.0)
            whole.append(done)
            picks.append((jnp.where(t_start[e:e + 1] <= slot, 1.0, 0.0) - done).astype(BF16))
        rows = [mm(p, w) for p, w in zip(picks, tiles)]
        votes = [(jnp.where(r[:, :LANES] + 256.0 * r[:, LANES:] <= slot, 1.0, 0.0) + float(LANES) * d).astype(BF16)
                 for r, d in zip(rows, whole)]
        for e, v in zip(es, votes):
            idx_ref[0, :, e:e + 1] = mm(v, ones)[:, e:e + 1].astype(jnp.int32)


def _route(aff_t, cap):
    B, E, S = aff_t.shape
    return pl.pallas_call(
        functools.partial(_route_kernel, cap=cap, seq=S),
        grid=(B,),
        in_specs=[pl.BlockSpec((1, E, S), lambda b: (b, 0, 0))],
        out_specs=pl.BlockSpec((1, cap, E), lambda b: (b, 0, 0)),
        out_shape=jax.ShapeDtypeStruct((B, cap, E), jnp.int32),
        scratch_shapes=[pltpu.VMEM((S // LANES * E, LANES), F32)],
        compiler_params=_cparams(("parallel",), 32),
        name="route",
    )(aff_t)


def _gather_kernel(idx_ref, xn_ref, aff_ref, xg_ref, wg_ref, ws_ref, *, cap):
    e = pl.program_id(1)

    for i in range(cap):
        t = idx_ref[0, 0, i]
        src = pl.multiple_of(t * SUBLANES, SUBLANES)
        xg_ref[0, 0, i * SUBLANES:(i + 1) * SUBLANES, :] = xn_ref[0, pl.ds(src, SUBLANES), :]
        ws_ref[i:i + 1, :] = aff_ref[0, pl.ds(t, 1), :]
    lane = lax.broadcasted_iota(jnp.int32, (cap, N_EXPERTS), 1)
    wg_ref[0, 0] = jnp.sum(jnp.where(lane == e, ws_ref[...], 0.0), axis=1, keepdims=True)


def _gather(idx, xn_tiles, aff, cap):
    B, S, E = aff.shape
    return pl.pallas_call(
        functools.partial(_gather_kernel, cap=cap),
        grid=(B, E),
        in_specs=[pl.BlockSpec((1, 1, cap), lambda b, e: (b * E + e, 0, 0), memory_space=pltpu.SMEM),
                  pl.BlockSpec((1, S * SUBLANES, LANES), lambda b, e: (b, 0, 0)),
                  pl.BlockSpec((1, S, E), lambda b, e: (b, 0, 0))],
        out_specs=[pl.BlockSpec((1, 1, cap * SUBLANES, LANES), lambda b, e: (b, e, 0, 0)),
                   pl.BlockSpec((1, 1, cap, 1), lambda b, e: (b, e, 0, 0))],
        out_shape=[jax.ShapeDtypeStruct((B, E, cap * SUBLANES, LANES), F32),
                   jax.ShapeDtypeStruct((B, E, cap, 1), F32)],
        scratch_shapes=[pltpu.VMEM((cap, E), F32)],
        compiler_params=_cparams(("arbitrary", "arbitrary"), 48),
        name="gather",
    )(idx.reshape(B * E, 1, cap), xn_tiles, aff)


def _ffn_kernel(xg_ref, wg_ref, w1_ref, w2_ref, w3_ref, y_ref, b1_ref, b2_ref, b3_ref):
    @pl.when(pl.program_id(1) == 0)
    def _():
        b1_ref[...] = w1_ref[0].astype(BF16)
        b2_ref[...] = w2_ref[0].astype(BF16)
        b3_ref[...] = w3_ref[0].astype(BF16)

    rows = wg_ref.shape[2]
    for i in range(xg_ref.shape[0]):
        xg = _load_token_tiles(xg_ref.at[i, 0], rows).astype(BF16)
        gate = jnp.dot(xg, b1_ref[...], preferred_element_type=F32)
        up = jnp.dot(xg, b2_ref[...], preferred_element_type=F32)
        hid = (gate * _sigmoid(gate) * up).astype(BF16)
        y = jnp.dot(hid, b3_ref[...], preferred_element_type=F32) * wg_ref[i, 0]
        _store_token_tiles(y_ref.at[i, 0], y)


FFN_SEQS_PER_STEP = 2


def _ffn(xg, wg, w1, w2, w3):
    B, E, C, _ = wg.shape
    _, D, F = w1.shape
    n = FFN_SEQS_PER_STEP
    return pl.pallas_call(
        _ffn_kernel,
        grid=(E, B // n),
        in_specs=[pl.BlockSpec((n, 1, C * ROW_TILES, LANES), lambda e, b: (b, e, 0, 0)),
                  pl.BlockSpec((n, 1, C, 1), lambda e, b: (b, e, 0, 0)),
                  pl.BlockSpec((1, D, F), lambda e, b: (e, 0, 0)),
                  pl.BlockSpec((1, D, F), lambda e, b: (e, 0, 0)),
                  pl.BlockSpec((1, F, D), lambda e, b: (e, 0, 0))],
        out_specs=pl.BlockSpec((n, 1, C * ROW_TILES, LANES), lambda e, b: (b, e, 0, 0)),
        out_shape=jax.ShapeDtypeStruct((B, E, C * ROW_TILES, LANES), F32),
        scratch_shapes=[pltpu.VMEM((D, F), BF16), pltpu.VMEM((D, F), BF16), pltpu.VMEM((F, D), BF16)],
        compiler_params=_cparams(("arbitrary", "arbitrary"), 60),
        name="ffn",
    )(xg, wg, w1, w2, w3)


SCATTER_BATCH = 16


def _scatter_kernel(idx_ref, y_ref, o_ref, *, cap):
    @pl.when(pl.program_id(1) == 0)
    def _():
        o_ref[...] = jnp.zeros_like(o_ref)

    for i0 in range(0, cap, SCATTER_BATCH):
        slots = range(i0, i0 + SCATTER_BATCH)
        rows = [pl.ds(pl.multiple_of(idx_ref[0, 0, i] * SUBLANES, SUBLANES), SUBLANES) for i in slots]
        new = [o_ref[0, r, :] + y_ref[0, 0, i * SUBLANES:(i + 1) * SUBLANES, :] for r, i in zip(rows, slots)]
        for r, v in zip(rows, new):
            o_ref[0, r, :] = v


def _scatter(idx, y_tiles, seq):
    B, E, cap = idx.shape
    return pl.pallas_call(
        functools.partial(_scatter_kernel, cap=cap),
        grid=(B, E),
        in_specs=[pl.BlockSpec((1, 1, cap), lambda b, e: (b * E + e, 0, 0), memory_space=pltpu.SMEM),
                  pl.BlockSpec((1, 1, cap * SUBLANES, LANES), lambda b, e: (b, e, 0, 0))],
        out_specs=pl.BlockSpec((1, seq * SUBLANES, LANES), lambda b, e: (b, 0, 0)),
        out_shape=jax.ShapeDtypeStruct((B, seq * SUBLANES, LANES), F32),
        compiler_params=_cparams(("arbitrary", "arbitrary"), 48),
        name="scatter",
    )(idx.reshape(B * E, 1, cap), y_tiles)


def _ple_out_kernel(h_ref, moe_ref, p_ref, gple_ref, wpg_ref, wple_ref, gfin_ref, o_ref):
    sub, starts = _sub_blocks(h_ref.shape[0])
    mm = functools.partial(jnp.dot, preferred_element_type=F32)
    hs = [h_ref[r0:r0 + sub, :] + _load_token_tiles(moe_ref, sub, r0) for r0 in starts]
    ns = [_rms(h, gple_ref[...]).astype(BF16) for h in hs]
    gates = [_sigmoid(mm(n, wpg_ref[...])) for n in ns]
    embs = [mm(p_ref[r0:r0 + sub, :].astype(BF16), wple_ref[...]) for r0 in starts]
    for r0, h, gate, emb in zip(starts, hs, gates, embs):
        o_ref[r0:r0 + sub, :] = _rms(h + gate * emb, gfin_ref[...])


def _ple_out(h, moe, p2, gple, wpg, wple, gfin, tm):
    T = h.shape[0]
    row = lambda n: pl.BlockSpec((tm, n), lambda i: (i, 0))
    return pl.pallas_call(
        _ple_out_kernel,
        grid=(T // tm,),
        in_specs=[row(D_MODEL), pl.BlockSpec((tm * ROW_TILES, LANES), lambda i: (i, 0)), row(PLE_DIM),
                  _full(gple.shape), _full(wpg.shape), _full(wple.shape), _full(gfin.shape)],
        out_specs=row(D_MODEL),
        out_shape=jax.ShapeDtypeStruct((T, D_MODEL), F32),
        compiler_params=_cparams(("parallel",), 48),
        name="ple_out",
    )(h, moe, p2, gple, wpg, wple, gfin)


def kernel(x, p, positions, norm_mix, w_in, gla_gate_up_fwd, gla_gate_bias_fwd, gla_gate_up_bwd, gla_gate_bias_bwd, attn_sink, gla_norm, w_branch_attn, w_branch_gla, w_out, norm_ffn, w_router, w_exp_gate, w_exp_up, w_exp_down, norm_ple, w_ple_gate, w_ple, norm_final):
    B, S, D = x.shape
    T = B * S
    depth = w_in.shape[0]
    cap = CAPACITY_FACTOR * S // N_EXPERTS
    R = GLA_GATE_RANK

    posc = positions.reshape(T, 1)
    posr = positions.reshape(1, T)
    inv_freq = ROPE_THETA ** (-jnp.arange(0, ROPE_DIM, 2, dtype=F32) / ROPE_DIM)
    invfl = jnp.tile(inv_freq, LANES // (ROPE_DIM // 2)).reshape(1, LANES)
    invfc = inv_freq.reshape(ROPE_DIM // 2, 1)

    h = x.reshape(T, D)
    for l in range(depth):
        o = 0
        cols = {}
        for name, n in (("q", ATT_Q_DIM), ("k", ATT_KV_DIM), ("v", ATT_KV_DIM), ("gqk", 2 * GLA_KEY_DIM),
                        ("gv", GLA_VAL_DIM), ("gr", GLA_VAL_DIM), ("z", 2 * R), ("gate", 2 * D_MODEL)):
            cols[name] = w_in[l][:, o:o + n].astype(BF16)
            o += n
        per_head = lambda w: w.reshape(-1, GLA_HEADS, GLA_DK).swapaxes(0, 1)
        upf, upb = per_head(gla_gate_up_fwd[l]), per_head(gla_gate_up_bwd[l])
        up = jnp.concatenate([jnp.concatenate([upf, jnp.zeros_like(upf)], axis=2),
                              jnp.concatenate([jnp.zeros_like(upb), upb], axis=2)], axis=1)
        up_hi = up.astype(BF16)
        up_lo = (up - up_hi.astype(F32)).astype(BF16)
        upw = jnp.concatenate([up_hi, up_hi, up_lo, jnp.zeros_like(up_lo)], axis=1)
        wz4 = jnp.tile(cols["z"], (1, LANES // (2 * R)))
        gbias = jnp.concatenate([per_head(gla_gate_bias_fwd[l]), per_head(gla_gate_bias_bwd[l])], axis=2)
        wr = w_router[l]
        wr_hi = wr.astype(BF16)
        wr_lo = (wr - wr_hi.astype(F32)).astype(BF16)
        wr2 = jnp.concatenate([jnp.concatenate([wr_hi, wr_lo], axis=1),
                               jnp.concatenate([wr_hi, jnp.zeros_like(wr_lo)], axis=1)], axis=0)

        qt, k0, k1, vt, gq, gk, gv, gr, z, sga, sgg = _in_proj(
            h, posc, posr, invfl, invfc, norm_mix[l].reshape(1, D), cols["q"].T, cols["k"], cols["v"].T,
            cols["gqk"], cols["gv"], cols["gr"], wz4, cols["gate"], tm=256)

        att = _swa(attn_sink[l], qt, k0.reshape(B, S, -1), k1.reshape(B, S, -1), vt, batch=B, tq=512)
        gla = _gla(z.reshape(B, S, -1), gq.reshape(B, S, -1), gk.reshape(B, S, -1), gv.reshape(B, S, -1),
                   gr.reshape(B, S, -1), upw, gbias, gla_norm[l].reshape(1, -1))

        h1, xn, aff = _mix_out(h, att.reshape(T, -1), gla.reshape(T, -1), sga, sgg,
                               w_branch_attn[l].astype(BF16), w_branch_gla[l].astype(BF16),
                               w_out[l].astype(BF16), norm_ffn[l].reshape(1, D), wr2, tm=512)

        aff3 = aff.reshape(B, S, N_EXPERTS)
        idx = _route(jnp.swapaxes(aff3, 1, 2), cap)
        idx = jnp.swapaxes(idx, 1, 2)
        xg, wg = _gather(idx, xn.reshape(B, S * SUBLANES, LANES), aff3, cap)
        y = _ffn(xg, wg, w_exp_gate[l], w_exp_up[l], w_exp_down[l])
        moe = _scatter(idx, y, S)

        last = l == depth - 1
        gfin = norm_final.reshape(1, D)
        assert last, "the final norm is fused into the last layer's PLE kernel"
        h = _ple_out(h1, moe.reshape(T * ROW_TILES, LANES), p[l].reshape(T, PLE_DIM), norm_ple[l].reshape(1, D),
                     w_ple_gate[l].astype(BF16), w_ple[l].astype(BF16), gfin, tm=512)
    return h.reshape(B, S, D)
```

```python
import functools
import math

import jax
import jax.numpy as jnp
from jax import lax
from jax.experimental import pallas as pl
from jax.experimental.pallas import tpu as pltpu

D_MODEL = 1024
ATT_HEADS = 8
ATT_KV_HEADS = 2
ATT_HEAD_DIM = 64
ATT_GROUP = ATT_HEADS // ATT_KV_HEADS
ATT_Q_DIM = ATT_HEADS * ATT_HEAD_DIM
ATT_KV_DIM = ATT_KV_HEADS * ATT_HEAD_DIM
WINDOW = 128
ROPE_DIM = ATT_HEAD_DIM // 4
ROPE_THETA = 500000.0
GLA_HEADS = 4
GLA_KEY_DIM = D_MODEL // 2
GLA_VAL_DIM = D_MODEL
GLA_DK = GLA_KEY_DIM // GLA_HEADS
GLA_DV = GLA_VAL_DIM // GLA_HEADS
GLA_GATE_RANK = 16
GLA_GATE_NORM = 16.0
GLA_CHUNK = 64
N_EXPERTS = 16
EXPERT_FF = D_MODEL
CAPACITY_FACTOR = 2
PLE_DIM = 256
EPS = 1e-6

LANES = 128
MIB = 1024 * 1024
BF16 = jnp.bfloat16
F32 = jnp.float32
LOG2E = math.log2(math.e)

NT_DIMS = (((1,), (1,)), ((), ()))
TN_DIMS = (((0,), (0,)), ((), ()))


def _cparams(sem, vmem_mib):
    return pltpu.CompilerParams(dimension_semantics=sem, vmem_limit_bytes=vmem_mib * MIB)


def _full(shape):
    n = len(shape)
    return pl.BlockSpec(shape, lambda *_: (0,) * n)


def _rms(x, gain):
    ms = jnp.mean(x * x, axis=-1, keepdims=True)
    return x * lax.rsqrt(ms + EPS) * gain


def _sigmoid(x):
    return 0.5 * jnp.tanh(0.5 * x) + 0.5


SUBLANES = 8
ROW_TILES = D_MODEL // LANES


def _store_token_tiles(ref2d, x, first_row=0):
    rows = x.shape[0]
    for j in range(ROW_TILES):
        ref2d[pl.ds(first_row * ROW_TILES + j, rows, stride=ROW_TILES), :] = x[:, j * LANES:(j + 1) * LANES]


def _load_token_tiles(ref2d, rows, first_row=0):
    return jnp.concatenate([ref2d[pl.ds(first_row * ROW_TILES + j, rows, stride=ROW_TILES), :]
                            for j in range(ROW_TILES)], axis=1)


ROW_WAYS = 2


def _sub_blocks(tile_rows):
    sub = tile_rows // ROW_WAYS
    return sub, [w * sub for w in range(ROW_WAYS)]


def _rope(t, cos_t, sin_t, first_half):
    fwd = pltpu.roll(t, LANES - ROPE_DIM // 2, axis=1)
    bwd = pltpu.roll(t, ROPE_DIM // 2, axis=1)
    return t * cos_t + jnp.where(first_half, fwd, bwd) * sin_t


def _in_proj_kernel(x_ref, posc_ref, posr_ref, invfl_ref, invfc_ref, gain_ref, wqt_ref, wkz_ref, wvt_ref,
                    wgqk_ref, wgv_ref, wgr_ref, wgate_ref,
                    qt_ref, k0_ref, k1_ref, vt_ref, gq_ref, gk_ref, gv_ref, gr_ref, z_ref, sga_ref, sgg_ref):
    a = _rms(x_ref[...], gain_ref[...]).astype(BF16)
    half = ROPE_DIM // 2

    qt = lax.dot_general(wqt_ref[...], a, NT_DIMS, preferred_element_type=F32)
    ang_t = invfc_ref[...] * posr_ref[...].astype(F32)
    cos_r, sin_r = jnp.cos(ang_t), jnp.sin(ang_t)
    qscale = ATT_HEAD_DIM ** -0.5 * LOG2E
    for h in range(ATT_HEADS):
        r0 = h * ATT_HEAD_DIM
        t1, t2 = qt[r0:r0 + half], qt[r0 + half:r0 + ROPE_DIM]
        qt_ref[r0:r0 + half, :] = ((t1 * cos_r - t2 * sin_r) * qscale).astype(BF16)
        qt_ref[r0 + half:r0 + ROPE_DIM, :] = ((t2 * cos_r + t1 * sin_r) * qscale).astype(BF16)
        qt_ref[r0 + ROPE_DIM:r0 + ATT_HEAD_DIM, :] = (qt[r0 + ROPE_DIM:r0 + ATT_HEAD_DIM] * qscale).astype(BF16)

    lane = lax.broadcasted_iota(jnp.int32, (1, LANES), 1)
    d = lane % ATT_HEAD_DIM
    first_half = d < half
    in_rope = d < ROPE_DIM
    ang = posc_ref[...].astype(F32) * invfl_ref[...]
    cos_t = jnp.where(in_rope, jnp.cos(ang), 1.0)
    sin_t = jnp.where(first_half, -jnp.sin(ang), jnp.where(in_rope, jnp.sin(ang), 0.0))
    kz = jnp.dot(a, wkz_ref[...], preferred_element_type=F32)
    z_ref[...] = kz[:, ATT_KV_DIM:]
    k = _rope(kz[:, :ATT_KV_DIM], cos_t, sin_t, first_half).astype(BF16)
    k0_ref[...] = k[:, :ATT_HEAD_DIM]
    k1_ref[...] = k[:, ATT_HEAD_DIM:]
    vt_ref[...] = lax.dot_general(wvt_ref[...], a, NT_DIMS, preferred_element_type=F32).astype(BF16)

    gqk = jnp.dot(a, wgqk_ref[...], preferred_element_type=F32)
    gq_ref[...] = (gqk[:, :GLA_KEY_DIM] * (GLA_DK ** -0.5)).astype(BF16)
    gk_ref[...] = gqk[:, GLA_KEY_DIM:].astype(BF16)
    gv_ref[...] = jnp.dot(a, wgv_ref[...], preferred_element_type=F32).astype(BF16)
    gr_ref[...] = jnp.dot(a, wgr_ref[...], preferred_element_type=F32).astype(BF16)
    gates = jnp.dot(a, wgate_ref[...], preferred_element_type=F32)
    sga_ref[...] = _sigmoid(gates[:, :D_MODEL]).astype(BF16)
    sgg_ref[...] = _sigmoid(gates[:, D_MODEL:]).astype(BF16)


def _in_proj(x2, posc, posr, invfl, invfc, gain, wqt, wkz, wvt, wgqk, wgv, wgr, wgate, tm):
    T = x2.shape[0]
    row = lambda n: pl.BlockSpec((tm, n), lambda i: (i, 0))
    col = lambda n: pl.BlockSpec((n, tm), lambda i: (0, i))
    row_widths = (ATT_HEAD_DIM, ATT_HEAD_DIM, None, GLA_KEY_DIM, GLA_KEY_DIM, GLA_VAL_DIM,
                  GLA_VAL_DIM, wkz.shape[1] - ATT_KV_DIM, D_MODEL, D_MODEL)
    row_dtypes = (BF16,) * 7 + (F32, BF16, BF16)
    out_specs = [col(ATT_Q_DIM)]
    out_shape = [jax.ShapeDtypeStruct((ATT_Q_DIM, T), BF16)]
    for n, dt in zip(row_widths, row_dtypes):
        if n is None:
            out_specs.append(col(ATT_KV_DIM))
            out_shape.append(jax.ShapeDtypeStruct((ATT_KV_DIM, T), BF16))
        else:
            out_specs.append(row(n))
            out_shape.append(jax.ShapeDtypeStruct((T, n), dt))
    consts = (invfl, invfc, gain, wqt, wkz, wvt, wgqk, wgv, wgr, wgate)
    return pl.pallas_call(
        _in_proj_kernel,
        grid=(T // tm,),
        in_specs=[row(D_MODEL), row(1), col(1)] + [_full(c.shape) for c in consts],
        out_specs=out_specs,
        out_shape=out_shape,
        compiler_params=_cparams(("parallel",), 56),
        name="in_proj",
    )(x2, posc, posr, *consts)


def _swa_kernel(sink_ref, qt_ref, k0_ref, k1_ref, vt_ref, o_ref, *, tq, seq):
    blk = WINDOW
    span = 3 * blk
    hd = ATT_HEAD_DIM
    n = pl.program_id(1)
    ones = jnp.ones((16, span), BF16)
    kv_refs = (k0_ref, k1_ref)

    def window_start(sb):
        return pl.multiple_of(jnp.clip(n * tq + (sb - 1) * blk, 0, seq - span), blk)

    def scores(sb, g):
        kw = kv_refs[g][0, pl.ds(window_start(sb), span), :]
        heads = range(g * ATT_GROUP, (g + 1) * ATT_GROUP)
        qs = jnp.concatenate([qt_ref[h * hd:(h + 1) * hd, sb * blk:(sb + 1) * blk] for h in heads], axis=1)
        return jnp.dot(kw, qs, preferred_element_type=F32)

    work = [(sb, g) for sb in range(tq // blk) for g in range(ATT_KV_HEADS)]
    s_next = scores(*work[0])
    outs = []
    for step, (sb, g) in enumerate(work):
        s_all = s_next
        if step + 1 < len(work):
            s_next = scores(*work[step + 1])
        q0 = n * tq + sb * blk
        start = window_start(sb)
        kj = start + lax.broadcasted_iota(jnp.int32, (span, blk), 0)
        qi = q0 + lax.broadcasted_iota(jnp.int32, (span, blk), 1)
        valid = jnp.abs(qi - kj) <= WINDOW
        vaug = jnp.concatenate([vt_ref[g * hd:(g + 1) * hd, pl.ds(start, span)], ones], axis=0)
        sinks = [sink_ref[g * ATT_GROUP + i] * LOG2E for i in range(ATT_GROUP)]
        ss = [jnp.where(valid, s_all[:, i * blk:(i + 1) * blk], -jnp.inf) for i in range(ATT_GROUP)]
        ms = [jnp.maximum(jnp.max(s, axis=0, keepdims=True), sink) for s, sink in zip(ss, sinks)]
        rs = [jnp.dot(vaug, jnp.exp2(s - m).astype(BF16), preferred_element_type=F32) for s, m in zip(ss, ms)]
        outs += [r[:hd] / (r[hd:hd + 1] + jnp.exp2(sink - m)) for r, m, sink in zip(rs, ms, sinks)]
        if g == ATT_KV_HEADS - 1:
            for pr in range(ATT_HEADS // 2):
                pair = jnp.concatenate([outs[2 * pr], outs[2 * pr + 1]], axis=0)
                o_ref[0, sb * blk:(sb + 1) * blk, pr * 2 * hd:(pr + 1) * 2 * hd] = pair.T.astype(BF16)
            outs = []


def _swa(sink, qt, k0, k1, vt, batch, tq):
    S = k0.shape[1]
    nq = S // tq
    kspec = pl.BlockSpec((1, S, ATT_HEAD_DIM), lambda b, n: (b, 0, 0))
    return pl.pallas_call(
        functools.partial(_swa_kernel, tq=tq, seq=S),
        grid=(batch, nq),
        in_specs=[pl.BlockSpec(memory_space=pltpu.SMEM),
                  pl.BlockSpec((ATT_Q_DIM, tq), lambda b, n: (0, b * nq + n)),
                  kspec, kspec,
                  pl.BlockSpec((ATT_KV_DIM, S), lambda b, n: (0, b))],
        out_specs=pl.BlockSpec((1, tq, ATT_Q_DIM), lambda b, n: (b, n, 0)),
        out_shape=jax.ShapeDtypeStruct((batch, S, ATT_Q_DIM), BF16),
        compiler_params=_cparams(("parallel", "parallel"), 32),
        name="swa",
    )(sink, qt, k0, k1, vt)


def _log_sigmoid(u):
    return jnp.minimum(u, 0.0) - jnp.log(1.0 + jnp.exp(-jnp.abs(u)))


def _split2(x):
    hi = x.astype(BF16)
    return hi, (x - hi.astype(F32)).astype(BF16)


GLA_WAYS = 4


def _gla_kernel(z_ref, q_ref, k_ref, v_ref, r_ref, upw_ref, bias_ref, gain_ref,
                o_ref, cf_ref, cb_ref, kef_ref, keb_ref, st_ref, s_ref, *, seq):
    L = GLA_CHUNK
    R2 = 2 * GLA_GATE_RANK
    nc = seq // L
    grp = 4 * L
    cpg = grp // L
    dk = GLA_DK
    mm = functools.partial(jnp.dot, preferred_element_type=F32)
    nt = functools.partial(lax.dot_general, dimension_numbers=NT_DIMS, preferred_element_type=F32)

    row = lax.broadcasted_iota(jnp.int32, (grp, grp), 0)
    col = lax.broadcasted_iota(jnp.int32, (grp, grp), 1)
    same = (row // L) == (col // L)
    fwd_mask = same & (col <= row)
    bwd_mask = same & (col > row)
    tri_lo = jnp.where(fwd_mask, 1.0, 0.0).astype(BF16)
    tri_up = jnp.where(same & (col >= row), 1.0, 0.0).astype(BF16)
    lane = lax.broadcasted_iota(jnp.int32, (1, LANES), 1)
    use_lo = (lane >= R2) & (lane < 2 * R2)

    def group_starts(i):
        return [pl.multiple_of((i * GLA_WAYS + w) * grp, grp) for w in range(GLA_WAYS)]

    def cum_body(i, carry):
        r0s = group_starts(i)
        zs = [_split2(z_ref[0, pl.ds(r0, grp), :]) for r0 in r0s]
        us = [mm(jnp.where(use_lo, zl, zh), upw_ref[...]) + bias_ref[...] for zh, zl in zs]
        las = [_split2(_log_sigmoid(u) * (1.0 / GLA_GATE_NORM)) for u in us]
        cfxs = [mm(tri_lo, jnp.concatenate([lh[:, :dk], ll[:, :dk]], axis=1)) for lh, ll in las]
        cbxs = [mm(tri_up, jnp.concatenate([lh[:, dk:], ll[:, dk:]], axis=1)) for lh, ll in las]
        for r0, cfx, cbx in zip(r0s, cfxs, cbxs):
            cf = cfx[:, :dk] + cfx[:, dk:]
            cb = cbx[:, :dk] + cbx[:, dk:]
            cf_ref[pl.ds(r0, grp), :] = cf
            cb_ref[pl.ds(r0, grp), :] = cb
            k = k_ref[0, pl.ds(r0, grp), :].astype(F32)
            for c in range(cpg):
                sl = slice(c * L, (c + 1) * L)
                gf = cf[(c + 1) * L - 1:(c + 1) * L]
                gb = cb[c * L:c * L + 1]
                kef_ref[pl.ds(r0 + c * L, L), :] = (k[sl] * jnp.exp(gf - cf[sl])).astype(BF16)
                keb_ref[pl.ds(r0 + c * L, L), :] = (k[sl] * jnp.exp(gb - cb[sl])).astype(BF16)
        return carry

    lax.fori_loop(0, seq // (grp * GLA_WAYS), cum_body, 0)

    s_ref[...] = jnp.zeros_like(s_ref)
    zero_k = jnp.zeros((L, dk), BF16)

    def state_body(i, carry):
        j = nc - 1 - i
        rf = pl.multiple_of(i * L, L)
        rb = pl.multiple_of(j * L, L)
        vcat = jnp.concatenate([v_ref[0, pl.ds(rf, L), :], v_ref[0, pl.ds(rb, L), :]], axis=0)
        kblk = jnp.concatenate([jnp.concatenate([kef_ref[pl.ds(rf, L), :], zero_k], axis=1),
                                jnp.concatenate([zero_k, keb_ref[pl.ds(rb, L), :]], axis=1)], axis=0)
        kv = lax.dot_general(vcat, kblk, TN_DIMS, preferred_element_type=F32)
        decay = jnp.exp(jnp.concatenate([cf_ref[pl.ds(rf + L - 1, 1), :], cb_ref[pl.ds(rb, 1), :]], axis=1))
        s = s_ref[...]
        st_ref[i, :, 0:dk] = s[:, :dk].astype(BF16)
        st_ref[j, :, dk:2 * dk] = s[:, dk:].astype(BF16)
        s_ref[...] = s * decay + kv
        return carry

    lax.fori_loop(0, nc, state_body, 0, unroll=16)

    def out_body(i, carry):
        r0s = group_starts(i)
        ops = []
        for r0 in r0s:
            q = q_ref[0, pl.ds(r0, grp), :].astype(F32)
            k = k_ref[0, pl.ds(r0, grp), :].astype(F32)
            cf = cf_ref[pl.ds(r0, grp), :]
            cb = cb_ref[pl.ds(r0, grp), :]
            ops.append(((q * jnp.exp(cf)).astype(BF16), (k * jnp.exp(-cf)).astype(BF16),
                        (q * jnp.exp(cb)).astype(BF16), (k * jnp.exp(-cb)).astype(BF16)))
        scores = [(nt(qf, kf), nt(qb, kb)) for qf, kf, qb, kb in ops]
        attns = [jnp.where(fwd_mask, af, jnp.where(bwd_mask, ab, 0.0)).astype(BF16) for af, ab in scores]
        outs = []
        for r0, attn, (qf, _, qb, _) in zip(r0s, attns, ops):
            c0 = r0 // L
            qcat = jnp.concatenate([qf, qb], axis=1)
            inter = jnp.concatenate([nt(qcat[c * L:(c + 1) * L], st_ref[c0 + c]) for c in range(cpg)], axis=0)
            outs.append(mm(attn, v_ref[0, pl.ds(r0, grp), :]) + inter)
        for r0, o in zip(r0s, outs):
            r = r_ref[0, pl.ds(r0, grp), :].astype(F32)
            o_ref[0, pl.ds(r0, grp), :] = (_rms(o, gain_ref[...]) * (r * _sigmoid(r))).astype(BF16)
        return carry

    lax.fori_loop(0, seq // (grp * GLA_WAYS), out_body, 0)


def _gla(z, gq, gk, gv, gr, upw, bias, gain):
    B, S, _ = gq.shape
    nc = S // GLA_CHUNK
    seq_blk = lambda n: pl.BlockSpec((1, S, n), lambda b, h: (b, 0, h))
    head_blk = lambda r, n: pl.BlockSpec((None, r, n), lambda b, h: (h, 0, 0))
    return pl.pallas_call(
        functools.partial(_gla_kernel, seq=S),
        grid=(B, GLA_HEADS),
        in_specs=[pl.BlockSpec((1, S, LANES), lambda b, h: (b, 0, 0)),
                  seq_blk(GLA_DK), seq_blk(GLA_DK), seq_blk(GLA_DV), seq_blk(GLA_DV),
                  head_blk(LANES, 2 * GLA_DK), head_blk(1, 2 * GLA_DK),
                  pl.BlockSpec((1, GLA_DV), lambda b, h: (0, h))],
        out_specs=seq_blk(GLA_DV),
        out_shape=jax.ShapeDtypeStruct((B, S, GLA_VAL_DIM), BF16),
        scratch_shapes=[pltpu.VMEM((S, GLA_DK), F32), pltpu.VMEM((S, GLA_DK), F32),
                        pltpu.VMEM((S, GLA_DK), BF16), pltpu.VMEM((S, GLA_DK), BF16),
                        pltpu.VMEM((nc, GLA_DV, 2 * GLA_DK), BF16),
                        pltpu.VMEM((GLA_DV, 2 * GLA_DK), F32)],
        compiler_params=_cparams(("parallel", "parallel"), 48),
        name="gla",
    )(z, gq, gk, gv, gr, upw, bias, gain)


def _mix_out_kernel(x_ref, a_ref, g_ref, sga_ref, sgg_ref, wa_ref, wb_ref, wo_ref, gain_ref, wr_ref,
                    h_ref, xn_ref, aff_ref):
    sub, starts = _sub_blocks(x_ref.shape[0])
    mm = functools.partial(jnp.dot, preferred_element_type=F32)
    blk = lambda ref, r0: ref[r0:r0 + sub, :]
    y_att = [mm(blk(a_ref, r0), wa_ref[...]) for r0 in starts]
    y_gla = [mm(blk(g_ref, r0), wb_ref[...]) for r0 in starts]
    merged = [(blk(sga_ref, r0).astype(F32) * ya + blk(sgg_ref, r0).astype(F32) * yg).astype(BF16)
              for r0, ya, yg in zip(starts, y_att, y_gla)]
    hs = [blk(x_ref, r0) + mm(m, wo_ref[...]) for r0, m in zip(starts, merged)]
    xns = [_rms(h, gain_ref[...]) for h in hs]
    for r0, h, xn in zip(starts, hs, xns):
        h_ref[r0:r0 + sub, :] = h
        _store_token_tiles(xn_ref, xn, r0)
    his = [xn.astype(BF16) for xn in xns]
    parts = [mm(jnp.concatenate([hi, (xn - hi.astype(F32)).astype(BF16)], axis=1), wr_ref[...])
             for xn, hi in zip(xns, his)]
    for r0, part in zip(starts, parts):
        logits = part[:, :N_EXPERTS] + part[:, N_EXPERTS:]
        e = jnp.exp(logits - jnp.max(logits, axis=-1, keepdims=True))
        aff_ref[r0:r0 + sub, :] = e / jnp.sum(e, axis=-1, keepdims=True)


def _mix_out(x2, a, g, sga, sgg, wa, wb, wo, gain, wr, tm):
    T = x2.shape[0]
    row = lambda n: pl.BlockSpec((tm, n), lambda i: (i, 0))
    return pl.pallas_call(
        _mix_out_kernel,
        grid=(T // tm,),
        in_specs=[row(D_MODEL), row(ATT_Q_DIM), row(GLA_VAL_DIM), row(D_MODEL), row(D_MODEL),
                  _full(wa.shape), _full(wb.shape), _full(wo.shape), _full(gain.shape), _full(wr.shape)],
        out_specs=[row(D_MODEL), pl.BlockSpec((tm * ROW_TILES, LANES), lambda i: (i, 0)), row(N_EXPERTS)],
        out_shape=[jax.ShapeDtypeStruct((T, D_MODEL), F32), jax.ShapeDtypeStruct((T * ROW_TILES, LANES), F32),
                   jax.ShapeDtypeStruct((T, N_EXPERTS), F32)],
        compiler_params=_cparams(("parallel",), 48),
        name="mix_out",
    )(x2, a, g, sga, sgg, wa, wb, wo, gain, wr)


ROUTE_WAYS = 4


def _route_kernel(aff_ref, idx_ref, cum_ref, *, cap, seq):
    E = N_EXPERTS
    aff = aff_ref[0]
    count = lambda mask: jnp.sum(mask.astype(jnp.int32), axis=1, keepdims=True)
    as_float = lambda pattern: lax.bitcast_convert_type(pattern, F32)

    def thr_body(t, pattern):
        cand = pattern | jnp.left_shift(jnp.int32(1), 30 - t)
        return jnp.where(count(aff >= as_float(cand)) >= cap, cand, pattern)

    thr = as_float(lax.fori_loop(0, 31, thr_body, jnp.zeros((E, 1), jnp.int32)))
    above = aff > thr
    tie = aff == thr
    need = cap - count(above)

    pos = lax.broadcasted_iota(jnp.int32, (E, seq), 1)

    def tie_body(t, last):
        cand = last | jnp.left_shift(jnp.int32(1), (seq.bit_length() - 2) - t)
        return jnp.where(count(tie & (pos < cand)) < need, cand, last)

    last = lax.fori_loop(0, seq.bit_length() - 1, tie_body, jnp.zeros((E, 1), jnp.int32))
    sel = (above | (tie & (pos <= last))).astype(BF16)

    nt = seq // LANES
    lrow = lax.broadcasted_iota(jnp.int32, (LANES, LANES), 0)
    lcol = lax.broadcasted_iota(jnp.int32, (LANES, LANES), 1)
    tri = (lrow <= lcol).astype(BF16)
    mm = functools.partial(jnp.dot, preferred_element_type=F32)
    for t in range(nt):
        cum_ref[t * E:(t + 1) * E, :] = mm(sel[:, t * LANES:(t + 1) * LANES], tri)
    tile_of = (lax.broadcasted_iota(jnp.int32, (seq, LANES), 0) // LANES
               == lax.broadcasted_iota(jnp.int32, (seq, LANES), 1)).astype(BF16)
    per_tile = mm(sel, tile_of)
    lane = lax.broadcasted_iota(jnp.int32, (1, LANES), 1)
    far = jnp.float32(2 * seq)
    t_end = jnp.where(lane < nt, mm(per_tile.astype(BF16), tri), far)
    t_start = jnp.where(lane < nt, t_end - per_tile, far)
    pad = jnp.zeros((LANES - E, LANES), F32)
    t_start_cols = jnp.concatenate([jnp.where(lane < nt, t_start, 0.0), pad], axis=0).T

    slot = lax.broadcasted_iota(jnp.int32, (cap, LANES), 0).astype(F32)
    ones = jnp.ones((LANES, LANES), BF16)
    zrows = jnp.zeros((LANES - nt, 2 * LANES), F32)
    for e0 in range(0, E, ROUTE_WAYS):
        es = range(e0, e0 + ROUTE_WAYS)
        tiles, picks, whole = [], [], []
        for e in es:
            absc = cum_ref[pl.ds(e, nt, stride=E), :] + t_start_cols[0:nt, e:e + 1]
            hi = jnp.where(absc >= 256.0, 1.0, 0.0) + jnp.where(absc >= 512.0, 1.0, 0.0)
            lo = absc - 256.0 * hi
            tiles.append(jnp.concatenate([jnp.concatenate([lo, hi], axis=1), zrows], axis=0).astype(BF16))
            done = jnp.where(t_end[e:e + 1] <= slot, 1.0, 0.0)
            whole.append(done)
            picks.append((jnp.where(t_start[e:e + 1] <= slot, 1.0, 0.0) - done).astype(BF16))
        rows = [mm(p, w) for p, w in zip(picks, tiles)]
        votes = [(jnp.where(r[:, :LANES] + 256.0 * r[:, LANES:] <= slot, 1.0, 0.0) + float(LANES) * d).astype(BF16)
                 for r, d in zip(rows, whole)]
        for e, v in zip(es, votes):
            idx_ref[0, :, e:e + 1] = mm(v, ones)[:, e:e + 1].astype(jnp.int32)


def _route(aff_t, cap):
    B, E, S = aff_t.shape
    return pl.pallas_call(
        functools.partial(_route_kernel, cap=cap, seq=S),
        grid=(B,),
        in_specs=[pl.BlockSpec((1, E, S), lambda b: (b, 0, 0))],
        out_specs=pl.BlockSpec((1, cap, E), lambda b: (b, 0, 0)),
        out_shape=jax.ShapeDtypeStruct((B, cap, E), jnp.int32),
        scratch_shapes=[pltpu.VMEM((S // LANES * E, LANES), F32)],
        compiler_params=_cparams(("parallel",), 32),
        name="route",
    )(aff_t)


def _gather_kernel(idx_ref, xn_ref, aff_ref, xg_ref, wg_ref, ws_ref, *, cap):
    e = pl.program_id(1)

    for i in range(cap):
        t = idx_ref[0, 0, i]
        src = pl.multiple_of(t * SUBLANES, SUBLANES)
        xg_ref[0, 0, i * SUBLANES:(i + 1) * SUBLANES, :] = xn_ref[0, pl.ds(src, SUBLANES), :]
        ws_ref[i:i + 1, :] = aff_ref[0, pl.ds(t, 1), :]
    lane = lax.broadcasted_iota(jnp.int32, (cap, N_EXPERTS), 1)
    wg_ref[0, 0] = jnp.sum(jnp.where(lane == e, ws_ref[...], 0.0), axis=1, keepdims=True)


def _gather(idx, xn_tiles, aff, cap):
    B, S, E = aff.shape
    return pl.pallas_call(
        functools.partial(_gather_kernel, cap=cap),
        grid=(B, E),
        in_specs=[pl.BlockSpec((1, 1, cap), lambda b, e: (b * E + e, 0, 0), memory_space=pltpu.SMEM),
                  pl.BlockSpec((1, S * SUBLANES, LANES), lambda b, e: (b, 0, 0)),
                  pl.BlockSpec((1, S, E), lambda b, e: (b, 0, 0))],
        out_specs=[pl.BlockSpec((1, 1, cap * SUBLANES, LANES), lambda b, e: (b, e, 0, 0)),
                   pl.BlockSpec((1, 1, cap, 1), lambda b, e: (b, e, 0, 0))],
        out_shape=[jax.ShapeDtypeStruct((B, E, cap * SUBLANES, LANES), F32),
                   jax.ShapeDtypeStruct((B, E, cap, 1), F32)],
        scratch_shapes=[pltpu.VMEM((cap, E), F32)],
        compiler_params=_cparams(("arbitrary", "arbitrary"), 48),
        name="gather",
    )(idx.reshape(B * E, 1, cap), xn_tiles, aff)


def _ffn_kernel(xg_ref, wg_ref, w1_ref, w2_ref, w3_ref, y_ref, b1_ref, b2_ref, b3_ref):
    @pl.when(pl.program_id(1) == 0)
    def _():
        b1_ref[...] = w1_ref[0].astype(BF16)
        b2_ref[...] = w2_ref[0].astype(BF16)
        b3_ref[...] = w3_ref[0].astype(BF16)

    rows = wg_ref.shape[2]
    for i in range(xg_ref.shape[0]):
        xg = _load_token_tiles(xg_ref.at[i, 0], rows).astype(BF16)
        gate = jnp.dot(xg, b1_ref[...], preferred_element_type=F32)
        up = jnp.dot(xg, b2_ref[...], preferred_element_type=F32)
        hid = (gate * _sigmoid(gate) * up).astype(BF16)
        y = jnp.dot(hid, b3_ref[...], preferred_element_type=F32) * wg_ref[i, 0]
        _store_token_tiles(y_ref.at[i, 0], y)


FFN_SEQS_PER_STEP = 2


def _ffn(xg, wg, w1, w2, w3):
    B, E, C, _ = wg.shape
    _, D, F = w1.shape
    n = FFN_SEQS_PER_STEP
    return pl.pallas_call(
        _ffn_kernel,
        grid=(E, B // n),
        in_specs=[pl.BlockSpec((n, 1, C * ROW_TILES, LANES), lambda e, b: (b, e, 0, 0)),
                  pl.BlockSpec((n, 1, C, 1), lambda e, b: (b, e, 0, 0)),
                  pl.BlockSpec((1, D, F), lambda e, b: (e, 0, 0)),
                  pl.BlockSpec((1, D, F), lambda e, b: (e, 0, 0)),
                  pl.BlockSpec((1, F, D), lambda e, b: (e, 0, 0))],
        out_specs=pl.BlockSpec((n, 1, C * ROW_TILES, LANES), lambda e, b: (b, e, 0, 0)),
        out_shape=jax.ShapeDtypeStruct((B, E, C * ROW_TILES, LANES), F32),
        scratch_shapes=[pltpu.VMEM((D, F), BF16), pltpu.VMEM((D, F), BF16), pltpu.VMEM((F, D), BF16)],
        compiler_params=_cparams(("arbitrary", "arbitrary"), 60),
        name="ffn",
    )(xg, wg, w1, w2, w3)


SCATTER_BATCH = 16


def _scatter_kernel(idx_ref, y_ref, o_ref, *, cap):
    @pl.when(pl.program_id(1) == 0)
    def _():
        o_ref[...] = jnp.zeros_like(o_ref)

    for i0 in range(0, cap, SCATTER_BATCH):
        slots = range(i0, i0 + SCATTER_BATCH)
        rows = [pl.ds(pl.multiple_of(idx_ref[0, 0, i] * SUBLANES, SUBLANES), SUBLANES) for i in slots]
        new = [o_ref[0, r, :] + y_ref[0, 0, i * SUBLANES:(i + 1) * SUBLANES, :] for r, i in zip(rows, slots)]
        for r, v in zip(rows, new):
            o_ref[0, r, :] = v


def _scatter(idx, y_tiles, seq):
    B, E, cap = idx.shape
    return pl.pallas_call(
        functools.partial(_scatter_kernel, cap=cap),
        grid=(B, E),
        in_specs=[pl.BlockSpec((1, 1, cap), lambda b, e: (b * E + e, 0, 0), memory_space=pltpu.SMEM),
                  pl.BlockSpec((1, 1, cap * SUBLANES, LANES), lambda b, e: (b, e, 0, 0))],
        out_specs=pl.BlockSpec((1, seq * SUBLANES, LANES), lambda b, e: (b, 0, 0)),
        out_shape=jax.ShapeDtypeStruct((B, seq * SUBLANES, LANES), F32),
        compiler_params=_cparams(("arbitrary", "arbitrary"), 48),
        name="scatter",
    )(idx.reshape(B * E, 1, cap), y_tiles)


def _ple_out_kernel(h_ref, moe_ref, p_ref, gple_ref, wpg_ref, wple_ref, gfin_ref, o_ref):
    sub, starts = _sub_blocks(h_ref.shape[0])
    mm = functools.partial(jnp.dot, preferred_element_type=F32)
    hs = [h_ref[r0:r0 + sub, :] + _load_token_tiles(moe_ref, sub, r0) for r0 in starts]
    ns = [_rms(h, gple_ref[...]).astype(BF16) for h in hs]
    gates = [_sigmoid(mm(n, wpg_ref[...])) for n in ns]
    embs = [mm(p_ref[r0:r0 + sub, :].astype(BF16), wple_ref[...]) for r0 in starts]
    for r0, h, gate, emb in zip(starts, hs, gates, embs):
        o_ref[r0:r0 + sub, :] = _rms(h + gate * emb, gfin_ref[...])


def _ple_out(h, moe, p2, gple, wpg, wple, gfin, tm):
    T = h.shape[0]
    row = lambda n: pl.BlockSpec((tm, n), lambda i: (i, 0))
    return pl.pallas_call(
        _ple_out_kernel,
        grid=(T // tm,),
        in_specs=[row(D_MODEL), pl.BlockSpec((tm * ROW_TILES, LANES), lambda i: (i, 0)), row(PLE_DIM),
                  _full(gple.shape), _full(wpg.shape), _full(wple.shape), _full(gfin.shape)],
        out_specs=row(D_MODEL),
        out_shape=jax.ShapeDtypeStruct((T, D_MODEL), F32),
        compiler_params=_cparams(("parallel",), 48),
        name="ple_out",
    )(h, moe, p2, gple, wpg, wple, gfin)


def kernel(x, p, positions, norm_mix, w_in, gla_gate_up_fwd, gla_gate_bias_fwd, gla_gate_up_bwd, gla_gate_bias_bwd, attn_sink, gla_norm, w_branch_attn, w_branch_gla, w_out, norm_ffn, w_router, w_exp_gate, w_exp_up, w_exp_down, norm_ple, w_ple_gate, w_ple, norm_final):
    B, S, D = x.shape
    T = B * S
    depth = w_in.shape[0]
    cap = CAPACITY_FACTOR * S // N_EXPERTS
    R = GLA_GATE_RANK

    posc = positions.reshape(T, 1)
    posr = positions.reshape(1, T)
    inv_freq = ROPE_THETA ** (-jnp.arange(0, ROPE_DIM, 2, dtype=F32) / ROPE_DIM)
    invfl = jnp.tile(inv_freq, LANES // (ROPE_DIM // 2)).reshape(1, LANES)
    invfc = inv_freq.reshape(ROPE_DIM // 2, 1)

    h = x.reshape(T, D)
    for l in range(depth):
        o = 0
        cols = {}
        for name, n in (("q", ATT_Q_DIM), ("k", ATT_KV_DIM), ("v", ATT_KV_DIM), ("gqk", 2 * GLA_KEY_DIM),
                        ("gv", GLA_VAL_DIM), ("gr", GLA_VAL_DIM), ("z", 2 * R), ("gate", 2 * D_MODEL)):
            cols[name] = w_in[l][:, o:o + n].astype(BF16)
            o += n
        per_head = lambda w: w.reshape(-1, GLA_HEADS, GLA_DK).swapaxes(0, 1)
        upf, upb = per_head(gla_gate_up_fwd[l]), per_head(gla_gate_up_bwd[l])
        up = jnp.concatenate([jnp.concatenate([upf, jnp.zeros_like(upf)], axis=2),
                              jnp.concatenate([jnp.zeros_like(upb), upb], axis=2)], axis=1)
        up_hi = up.astype(BF16)
        up_lo = (up - up_hi.astype(F32)).astype(BF16)
        upw = jnp.concatenate([up_hi, up_hi, up_lo, jnp.zeros_like(up_lo)], axis=1)
        wz4 = jnp.tile(cols["z"], (1, LANES // (2 * R)))
        gbias = jnp.concatenate([per_head(gla_gate_bias_fwd[l]), per_head(gla_gate_bias_bwd[l])], axis=2)
        wr = w_router[l]
        wr_hi = wr.astype(BF16)
        wr_lo = (wr - wr_hi.astype(F32)).astype(BF16)
        wr2 = jnp.concatenate([jnp.concatenate([wr_hi, wr_lo], axis=1),
                               jnp.concatenate([wr_hi, jnp.zeros_like(wr_lo)], axis=1)], axis=0)

        qt, k0, k1, vt, gq, gk, gv, gr, z, sga, sgg = _in_proj(
            h, posc, posr, invfl, invfc, norm_mix[l].reshape(1, D), cols["q"].T,
            jnp.concatenate([cols["k"], wz4], axis=1), cols["v"].T,
            cols["gqk"], cols["gv"], cols["gr"], cols["gate"], tm=512)

        att = _swa(attn_sink[l], qt, k0.reshape(B, S, -1), k1.reshape(B, S, -1), vt, batch=B, tq=512)
        gla = _gla(z.reshape(B, S, -1), gq.reshape(B, S, -1), gk.reshape(B, S, -1), gv.reshape(B, S, -1),
                   gr.reshape(B, S, -1), upw, gbias, gla_norm[l].reshape(1, -1))

        h1, xn, aff = _mix_out(h, att.reshape(T, -1), gla.reshape(T, -1), sga, sgg,
                               w_branch_attn[l].astype(BF16), w_branch_gla[l].astype(BF16),
                               w_out[l].astype(BF16), norm_ffn[l].reshape(1, D), wr2, tm=512)

        aff3 = aff.reshape(B, S, N_EXPERTS)
        idx = _route(jnp.swapaxes(aff3, 1, 2), cap)
        idx = jnp.swapaxes(idx, 1, 2)
        xg, wg = _gather(idx, xn.reshape(B, S * SUBLANES, LANES), aff3, cap)
        y = _ffn(xg, wg, w_exp_gate[l], w_exp_up[l], w_exp_down[l])
        moe = _scatter(idx, y, S)

        last = l == depth - 1
        gfin = norm_final.reshape(1, D)
        assert last, "the final norm is fused into the last layer's PLE kernel"
        h = _ple_out(h1, moe.reshape(T * ROW_TILES, LANES), p[l].reshape(T, PLE_DIM), norm_ple[l].reshape(1, D),
                     w_ple_gate[l].astype(BF16), w_ple[l].astype(BF16), gfin, tm=512)
    return h.reshape(B, S, D)
```

```python
import functools
import math

import jax
import jax.numpy as jnp
from jax import lax
from jax.experimental import pallas as pl
from jax.experimental.pallas import tpu as pltpu

D_MODEL = 1024
ATT_HEADS = 8
ATT_KV_HEADS = 2
ATT_HEAD_DIM = 64
ATT_GROUP = ATT_HEADS // ATT_KV_HEADS
ATT_Q_DIM = ATT_HEADS * ATT_HEAD_DIM
ATT_KV_DIM = ATT_KV_HEADS * ATT_HEAD_DIM
WINDOW = 128
ROPE_DIM = ATT_HEAD_DIM // 4
ROPE_THETA = 500000.0
GLA_HEADS = 4
GLA_KEY_DIM = D_MODEL // 2
GLA_VAL_DIM = D_MODEL
GLA_DK = GLA_KEY_DIM // GLA_HEADS
GLA_DV = GLA_VAL_DIM // GLA_HEADS
GLA_GATE_RANK = 16
GLA_GATE_NORM = 16.0
GLA_CHUNK = 64
N_EXPERTS = 16
EXPERT_FF = D_MODEL
CAPACITY_FACTOR = 2
PLE_DIM = 256
EPS = 1e-6

LANES = 128
MIB = 1024 * 1024
BF16 = jnp.bfloat16
F32 = jnp.float32
LOG2E = math.log2(math.e)

NT_DIMS = (((1,), (1,)), ((), ()))
TN_DIMS = (((0,), (0,)), ((), ()))


def _cparams(sem, vmem_mib):
    return pltpu.CompilerParams(dimension_semantics=sem, vmem_limit_bytes=vmem_mib * MIB)


def _full(shape):
    n = len(shape)
    return pl.BlockSpec(shape, lambda *_: (0,) * n)


def _rms(x, gain):
    ms = jnp.mean(x * x, axis=-1, keepdims=True)
    return x * lax.rsqrt(ms + EPS) * gain


def _sigmoid(x):
    return 0.5 * jnp.tanh(0.5 * x) + 0.5


SUBLANES = 8
ROW_TILES = D_MODEL // LANES


def _store_token_tiles(ref2d, x, first_row=0):
    rows = x.shape[0]
    for j in range(ROW_TILES):
        ref2d[pl.ds(first_row * ROW_TILES + j, rows, stride=ROW_TILES), :] = x[:, j * LANES:(j + 1) * LANES]


def _load_token_tiles(ref2d, rows, first_row=0):
    return jnp.concatenate([ref2d[pl.ds(first_row * ROW_TILES + j, rows, stride=ROW_TILES), :]
                            for j in range(ROW_TILES)], axis=1)


ROW_WAYS = 2


def _sub_blocks(tile_rows):
    sub = tile_rows // ROW_WAYS
    return sub, [w * sub for w in range(ROW_WAYS)]


def _rope(t, cos_t, sin_t, first_half):
    fwd = pltpu.roll(t, LANES - ROPE_DIM // 2, axis=1)
    bwd = pltpu.roll(t, ROPE_DIM // 2, axis=1)
    return t * cos_t + jnp.where(first_half, fwd, bwd) * sin_t


def _in_proj_kernel(x_ref, posc_ref, posr_ref, invfl_ref, invfc_ref, gain_ref, wqt_ref, wkz_ref, wvt_ref,
                    wgqk_ref, wgv_ref, wgr_ref, wgate_ref,
                    qt_ref, k0_ref, k1_ref, vt_ref, gq_ref, gk_ref, gv_ref, gr_ref, z_ref, sga_ref, sgg_ref):
    a = _rms(x_ref[...], gain_ref[...]).astype(BF16)
    half = ROPE_DIM // 2

    qt = lax.dot_general(wqt_ref[...], a, NT_DIMS, preferred_element_type=F32)
    ang_t = invfc_ref[...] * posr_ref[...].astype(F32)
    cos_r, sin_r = jnp.cos(ang_t), jnp.sin(ang_t)
    qscale = ATT_HEAD_DIM ** -0.5 * LOG2E
    for h in range(ATT_HEADS):
        r0 = h * ATT_HEAD_DIM
        t1, t2 = qt[r0:r0 + half], qt[r0 + half:r0 + ROPE_DIM]
        qt_ref[r0:r0 + half, :] = ((t1 * cos_r - t2 * sin_r) * qscale).astype(BF16)
        qt_ref[r0 + half:r0 + ROPE_DIM, :] = ((t2 * cos_r + t1 * sin_r) * qscale).astype(BF16)
        qt_ref[r0 + ROPE_DIM:r0 + ATT_HEAD_DIM, :] = (qt[r0 + ROPE_DIM:r0 + ATT_HEAD_DIM] * qscale).astype(BF16)

    lane = lax.broadcasted_iota(jnp.int32, (1, LANES), 1)
    d = lane % ATT_HEAD_DIM
    first_half = d < half
    in_rope = d < ROPE_DIM
    ang = posc_ref[...].astype(F32) * invfl_ref[...]
    cos_t = jnp.where(in_rope, jnp.cos(ang), 1.0)
    sin_t = jnp.where(first_half, -jnp.sin(ang), jnp.where(in_rope, jnp.sin(ang), 0.0))
    kz = jnp.dot(a, wkz_ref[...], preferred_element_type=F32)
    z_ref[...] = kz[:, ATT_KV_DIM:]
    k = _rope(kz[:, :ATT_KV_DIM], cos_t, sin_t, first_half).astype(BF16)
    k0_ref[...] = k[:, :ATT_HEAD_DIM]
    k1_ref[...] = k[:, ATT_HEAD_DIM:]
    vt_ref[...] = lax.dot_general(wvt_ref[...], a, NT_DIMS, preferred_element_type=F32).astype(BF16)

    gqk = jnp.dot(a, wgqk_ref[...], preferred_element_type=F32)
    gq_ref[...] = (gqk[:, :GLA_KEY_DIM] * (GLA_DK ** -0.5)).astype(BF16)
    gk_ref[...] = gqk[:, GLA_KEY_DIM:].astype(BF16)
    gv_ref[...] = jnp.dot(a, wgv_ref[...], preferred_element_type=F32).astype(BF16)
    gr_ref[...] = jnp.dot(a, wgr_ref[...], preferred_element_type=F32).astype(BF16)
    gates = jnp.dot(a, wgate_ref[...], preferred_element_type=F32)
    sga_ref[...] = _sigmoid(gates[:, :D_MODEL]).astype(BF16)
    sgg_ref[...] = _sigmoid(gates[:, D_MODEL:]).astype(BF16)


def _in_proj(x2, posc, posr, invfl, invfc, gain, wqt, wkz, wvt, wgqk, wgv, wgr, wgate, tm):
    T = x2.shape[0]
    row = lambda n: pl.BlockSpec((tm, n), lambda i: (i, 0))
    col = lambda n: pl.BlockSpec((n, tm), lambda i: (0, i))
    row_widths = (ATT_HEAD_DIM, ATT_HEAD_DIM, None, GLA_KEY_DIM, GLA_KEY_DIM, GLA_VAL_DIM,
                  GLA_VAL_DIM, wkz.shape[1] - ATT_KV_DIM, D_MODEL, D_MODEL)
    row_dtypes = (BF16,) * 7 + (F32, BF16, BF16)
    out_specs = [col(ATT_Q_DIM)]
    out_shape = [jax.ShapeDtypeStruct((ATT_Q_DIM, T), BF16)]
    for n, dt in zip(row_widths, row_dtypes):
        if n is None:
            out_specs.append(col(ATT_KV_DIM))
            out_shape.append(jax.ShapeDtypeStruct((ATT_KV_DIM, T), BF16))
        else:
            out_specs.append(row(n))
            out_shape.append(jax.ShapeDtypeStruct((T, n), dt))
    consts = (invfl, invfc, gain, wqt, wkz, wvt, wgqk, wgv, wgr, wgate)
    return pl.pallas_call(
        _in_proj_kernel,
        grid=(T // tm,),
        in_specs=[row(D_MODEL), row(1), col(1)] + [_full(c.shape) for c in consts],
        out_specs=out_specs,
        out_shape=out_shape,
        compiler_params=_cparams(("parallel",), 56),
        name="in_proj",
    )(x2, posc, posr, *consts)


def _swa_kernel(sink_ref, qt_ref, k0_ref, k1_ref, vt_ref, o_ref, *, tq, seq):
    blk = WINDOW
    span = 3 * blk
    hd = ATT_HEAD_DIM
    n = pl.program_id(1)
    ones = jnp.ones((16, span), BF16)
    kv_refs = (k0_ref, k1_ref)

    def window_start(sb):
        return pl.multiple_of(jnp.clip(n * tq + (sb - 1) * blk, 0, seq - span), blk)

    def scores(sb, g):
        kw = kv_refs[g][0, pl.ds(window_start(sb), span), :]
        heads = range(g * ATT_GROUP, (g + 1) * ATT_GROUP)
        qs = jnp.concatenate([qt_ref[h * hd:(h + 1) * hd, sb * blk:(sb + 1) * blk] for h in heads], axis=1)
        return jnp.dot(kw, qs, preferred_element_type=F32)

    work = [(sb, g) for sb in range(tq // blk) for g in range(ATT_KV_HEADS)]
    s_next = scores(*work[0])
    outs = []
    for step, (sb, g) in enumerate(work):
        s_all = s_next
        if step + 1 < len(work):
            s_next = scores(*work[step + 1])
        q0 = n * tq + sb * blk
        start = window_start(sb)
        kj = start + lax.broadcasted_iota(jnp.int32, (span, blk), 0)
        qi = q0 + lax.broadcasted_iota(jnp.int32, (span, blk), 1)
        valid = jnp.abs(qi - kj) <= WINDOW
        vaug = jnp.concatenate([vt_ref[g * hd:(g + 1) * hd, pl.ds(start, span)], ones], axis=0)
        sinks = [sink_ref[g * ATT_GROUP + i] * LOG2E for i in range(ATT_GROUP)]
        ss = [jnp.where(valid, s_all[:, i * blk:(i + 1) * blk], -jnp.inf) for i in range(ATT_GROUP)]
        ms = [jnp.maximum(jnp.max(s, axis=0, keepdims=True), sink) for s, sink in zip(ss, sinks)]
        rs = [jnp.dot(vaug, jnp.exp2(s - m).astype(BF16), preferred_element_type=F32) for s, m in zip(ss, ms)]
        outs += [r[:hd] / (r[hd:hd + 1] + jnp.exp2(sink - m)) for r, m, sink in zip(rs, ms, sinks)]
        if g == ATT_KV_HEADS - 1:
            for pr in range(ATT_HEADS // 2):
                pair = jnp.concatenate([outs[2 * pr], outs[2 * pr + 1]], axis=0)
                o_ref[0, sb * blk:(sb + 1) * blk, pr * 2 * hd:(pr + 1) * 2 * hd] = pair.T.astype(BF16)
            outs = []


def _swa(sink, qt, k0, k1, vt, batch, tq):
    S = k0.shape[1]
    nq = S // tq
    kspec = pl.BlockSpec((1, S, ATT_HEAD_DIM), lambda b, n: (b, 0, 0))
    return pl.pallas_call(
        functools.partial(_swa_kernel, tq=tq, seq=S),
        grid=(batch, nq),
        in_specs=[pl.BlockSpec(memory_space=pltpu.SMEM),
                  pl.BlockSpec((ATT_Q_DIM, tq), lambda b, n: (0, b * nq + n)),
                  kspec, kspec,
                  pl.BlockSpec((ATT_KV_DIM, S), lambda b, n: (0, b))],
        out_specs=pl.BlockSpec((1, tq, ATT_Q_DIM), lambda b, n: (b, n, 0)),
        out_shape=jax.ShapeDtypeStruct((batch, S, ATT_Q_DIM), BF16),
        compiler_params=_cparams(("parallel", "parallel"), 32),
        name="swa",
    )(sink, qt, k0, k1, vt)


def _log_sigmoid(u):
    return jnp.minimum(u, 0.0) - jnp.log(1.0 + jnp.exp(-jnp.abs(u)))


def _split2(x):
    hi = x.astype(BF16)
    return hi, (x - hi.astype(F32)).astype(BF16)


GLA_WAYS = 4


def _gla_kernel(z_ref, q_ref, k_ref, v_ref, r_ref, upw_ref, bias_ref, gain_ref,
                o_ref, cf_ref, cb_ref, kef_ref, keb_ref, st_ref, s_ref, *, seq):
    L = GLA_CHUNK
    R2 = 2 * GLA_GATE_RANK
    nc = seq // L
    grp = 4 * L
    cpg = grp // L
    dk = GLA_DK
    mm = functools.partial(jnp.dot, preferred_element_type=F32)
    nt = functools.partial(lax.dot_general, dimension_numbers=NT_DIMS, preferred_element_type=F32)

    row = lax.broadcasted_iota(jnp.int32, (grp, grp), 0)
    col = lax.broadcasted_iota(jnp.int32, (grp, grp), 1)
    same = (row // L) == (col // L)
    fwd_mask = same & (col <= row)
    bwd_mask = same & (col > row)
    tri_lo = jnp.where(fwd_mask, 1.0, 0.0).astype(BF16)
    tri_up = jnp.where(same & (col >= row), 1.0, 0.0).astype(BF16)
    lane = lax.broadcasted_iota(jnp.int32, (1, LANES), 1)
    use_lo = (lane >= R2) & (lane < 2 * R2)

    def group_starts(i):
        return [pl.multiple_of((i * GLA_WAYS + w) * grp, grp) for w in range(GLA_WAYS)]

    def cum_body(i, carry):
        r0s = group_starts(i)
        zs = [_split2(z_ref[0, pl.ds(r0, grp), :]) for r0 in r0s]
        us = [mm(jnp.where(use_lo, zl, zh), upw_ref[...]) + bias_ref[...] for zh, zl in zs]
        las = [_split2(_log_sigmoid(u) * (1.0 / GLA_GATE_NORM)) for u in us]
        cfxs = [mm(tri_lo, jnp.concatenate([lh[:, :dk], ll[:, :dk]], axis=1)) for lh, ll in las]
        cbxs = [mm(tri_up, jnp.concatenate([lh[:, dk:], ll[:, dk:]], axis=1)) for lh, ll in las]
        for r0, cfx, cbx in zip(r0s, cfxs, cbxs):
            cf = cfx[:, :dk] + cfx[:, dk:]
            cb = cbx[:, :dk] + cbx[:, dk:]
            cf_ref[pl.ds(r0, grp), :] = cf
            cb_ref[pl.ds(r0, grp), :] = cb
            k = k_ref[0, pl.ds(r0, grp), :].astype(F32)
            for c in range(cpg):
                sl = slice(c * L, (c + 1) * L)
                gf = cf[(c + 1) * L - 1:(c + 1) * L]
                gb = cb[c * L:c * L + 1]
                kef_ref[pl.ds(r0 + c * L, L), :] = (k[sl] * jnp.exp(gf - cf[sl])).astype(BF16)
                keb_ref[pl.ds(r0 + c * L, L), :] = (k[sl] * jnp.exp(gb - cb[sl])).astype(BF16)
        return carry

    lax.fori_loop(0, seq // (grp * GLA_WAYS), cum_body, 0)

    s_ref[...] = jnp.zeros_like(s_ref)
    zero_k = jnp.zeros((L, dk), BF16)

    def state_body(i, carry):
        j = nc - 1 - i
        rf = pl.multiple_of(i * L, L)
        rb = pl.multiple_of(j * L, L)
        vcat = jnp.concatenate([v_ref[0, pl.ds(rf, L), :], v_ref[0, pl.ds(rb, L), :]], axis=0)
        kblk = jnp.concatenate([jnp.concatenate([kef_ref[pl.ds(rf, L), :], zero_k], axis=1),
                                jnp.concatenate([zero_k, keb_ref[pl.ds(rb, L), :]], axis=1)], axis=0)
        kv = lax.dot_general(vcat, kblk, TN_DIMS, preferred_element_type=F32)
        decay = jnp.exp(jnp.concatenate([cf_ref[pl.ds(rf + L - 1, 1), :], cb_ref[pl.ds(rb, 1), :]], axis=1))
        s = s_ref[...]
        st_ref[i, :, 0:dk] = s[:, :dk].astype(BF16)
        st_ref[j, :, dk:2 * dk] = s[:, dk:].astype(BF16)
        s_ref[...] = s * decay + kv
        return carry

    lax.fori_loop(0, nc, state_body, 0, unroll=16)

    def out_body(i, carry):
        r0s = group_starts(i)
        ops = []
        for r0 in r0s:
            q = q_ref[0, pl.ds(r0, grp), :].astype(F32)
            k = k_ref[0, pl.ds(r0, grp), :].astype(F32)
            cf = cf_ref[pl.ds(r0, grp), :]
            cb = cb_ref[pl.ds(r0, grp), :]
            ops.append(((q * jnp.exp(cf)).astype(BF16), (k * jnp.exp(-cf)).astype(BF16),
                        (q * jnp.exp(cb)).astype(BF16), (k * jnp.exp(-cb)).astype(BF16)))
        scores = [(nt(qf, kf), nt(qb, kb)) for qf, kf, qb, kb in ops]
        attns = [jnp.where(fwd_mask, af, jnp.where(bwd_mask, ab, 0.0)).astype(BF16) for af, ab in scores]
        outs = []
        for r0, attn, (qf, _, qb, _) in zip(r0s, attns, ops):
            c0 = r0 // L
            qcat = jnp.concatenate([qf, qb], axis=1)
            inter = jnp.concatenate([nt(qcat[c * L:(c + 1) * L], st_ref[c0 + c]) for c in range(cpg)], axis=0)
            outs.append(mm(attn, v_ref[0, pl.ds(r0, grp), :]) + inter)
        for r0, o in zip(r0s, outs):
            r = r_ref[0, pl.ds(r0, grp), :].astype(F32)
            o_ref[0, pl.ds(r0, grp), :] = (_rms(o, gain_ref[...]) * (r * _sigmoid(r))).astype(BF16)
        return carry

    lax.fori_loop(0, seq // (grp * GLA_WAYS), out_body, 0)


def _gla(z, gq, gk, gv, gr, upw, bias, gain):
    B, S, _ = gq.shape
    nc = S // GLA_CHUNK
    seq_blk = lambda n: pl.BlockSpec((1, S, n), lambda b, h: (b, 0, h))
    head_blk = lambda r, n: pl.BlockSpec((None, r, n), lambda b, h: (h, 0, 0))
    return pl.pallas_call(
        functools.partial(_gla_kernel, seq=S),
        grid=(B, GLA_HEADS),
        in_specs=[pl.BlockSpec((1, S, LANES), lambda b, h: (b, 0, 0)),
                  seq_blk(GLA_DK), seq_blk(GLA_DK), seq_blk(GLA_DV), seq_blk(GLA_DV),
                  head_blk(LANES, 2 * GLA_DK), head_blk(1, 2 * GLA_DK),
                  pl.BlockSpec((1, GLA_DV), lambda b, h: (0, h))],
        out_specs=seq_blk(GLA_DV),
        out_shape=jax.ShapeDtypeStruct((B, S, GLA_VAL_DIM), BF16),
        scratch_shapes=[pltpu.VMEM((S, GLA_DK), F32), pltpu.VMEM((S, GLA_DK), F32),
                        pltpu.VMEM((S, GLA_DK), BF16), pltpu.VMEM((S, GLA_DK), BF16),
                        pltpu.VMEM((nc, GLA_DV, 2 * GLA_DK), BF16),
                        pltpu.VMEM((GLA_DV, 2 * GLA_DK), F32)],
        compiler_params=_cparams(("parallel", "parallel"), 48),
        name="gla",
    )(z, gq, gk, gv, gr, upw, bias, gain)


def _mix_out_kernel(x_ref, a_ref, g_ref, sga_ref, sgg_ref, wa_ref, wb_ref, wo_ref, gain_ref, wr_ref,
                    h_ref, xn_ref, aff_ref, afft_ref):
    sub, starts = _sub_blocks(x_ref.shape[0])
    mm = functools.partial(jnp.dot, preferred_element_type=F32)
    blk = lambda ref, r0: ref[r0:r0 + sub, :]
    y_att = [mm(blk(a_ref, r0), wa_ref[...]) for r0 in starts]
    y_gla = [mm(blk(g_ref, r0), wb_ref[...]) for r0 in starts]
    merged = [(blk(sga_ref, r0).astype(F32) * ya + blk(sgg_ref, r0).astype(F32) * yg).astype(BF16)
              for r0, ya, yg in zip(starts, y_att, y_gla)]
    hs = [blk(x_ref, r0) + mm(m, wo_ref[...]) for r0, m in zip(starts, merged)]
    xns = [_rms(h, gain_ref[...]) for h in hs]
    for r0, h, xn in zip(starts, hs, xns):
        h_ref[r0:r0 + sub, :] = h
        _store_token_tiles(xn_ref, xn, r0)
    his = [xn.astype(BF16) for xn in xns]
    parts = [lax.dot_general(wr_ref[...], jnp.concatenate([hi, (xn - hi.astype(F32)).astype(BF16)], axis=1),
                             NT_DIMS, preferred_element_type=F32) for xn, hi in zip(xns, his)]
    pad = jnp.zeros((LANES - N_EXPERTS, sub), F32)
    for r0, part in zip(starts, parts):
        logits = part[:N_EXPERTS] + part[N_EXPERTS:]
        e = jnp.exp(logits - jnp.max(logits, axis=0, keepdims=True))
        aff_t = e / jnp.sum(e, axis=0, keepdims=True)
        afft_ref[:, r0:r0 + sub] = aff_t
        aff_ref[r0:r0 + sub, :] = jnp.concatenate([aff_t, pad], axis=0).T[:, :N_EXPERTS]


def _mix_out(x2, a, g, sga, sgg, wa, wb, wo, gain, wr, tm):
    T = x2.shape[0]
    row = lambda n: pl.BlockSpec((tm, n), lambda i: (i, 0))
    return pl.pallas_call(
        _mix_out_kernel,
        grid=(T // tm,),
        in_specs=[row(D_MODEL), row(ATT_Q_DIM), row(GLA_VAL_DIM), row(D_MODEL), row(D_MODEL),
                  _full(wa.shape), _full(wb.shape), _full(wo.shape), _full(gain.shape), _full(wr.shape)],
        out_specs=[row(D_MODEL), pl.BlockSpec((tm * ROW_TILES, LANES), lambda i: (i, 0)), row(N_EXPERTS),
                   pl.BlockSpec((N_EXPERTS, tm), lambda i: (0, i))],
        out_shape=[jax.ShapeDtypeStruct((T, D_MODEL), F32), jax.ShapeDtypeStruct((T * ROW_TILES, LANES), F32),
                   jax.ShapeDtypeStruct((T, N_EXPERTS), F32), jax.ShapeDtypeStruct((N_EXPERTS, T), F32)],
        compiler_params=_cparams(("parallel",), 48),
        name="mix_out",
    )(x2, a, g, sga, sgg, wa, wb, wo, gain, wr)


ROUTE_WAYS = 4


def _route_kernel(aff_ref, idx_ref, cum_ref, *, cap, seq):
    E = N_EXPERTS
    aff = aff_ref[...]
    count = lambda mask: jnp.sum(mask.astype(jnp.int32), axis=1, keepdims=True)
    as_float = lambda pattern: lax.bitcast_convert_type(pattern, F32)

    def thr_body(t, pattern):
        cand = pattern | jnp.left_shift(jnp.int32(1), 30 - t)
        return jnp.where(count(aff >= as_float(cand)) >= cap, cand, pattern)

    thr = as_float(lax.fori_loop(0, 31, thr_body, jnp.zeros((E, 1), jnp.int32)))
    above = aff > thr
    tie = aff == thr
    need = cap - count(above)

    pos = lax.broadcasted_iota(jnp.int32, (E, seq), 1)

    def tie_body(t, last):
        cand = last | jnp.left_shift(jnp.int32(1), (seq.bit_length() - 2) - t)
        return jnp.where(count(tie & (pos < cand)) < need, cand, last)

    last = lax.fori_loop(0, seq.bit_length() - 1, tie_body, jnp.zeros((E, 1), jnp.int32))
    sel = (above | (tie & (pos <= last))).astype(BF16)

    nt = seq // LANES
    lrow = lax.broadcasted_iota(jnp.int32, (LANES, LANES), 0)
    lcol = lax.broadcasted_iota(jnp.int32, (LANES, LANES), 1)
    tri = (lrow <= lcol).astype(BF16)
    mm = functools.partial(jnp.dot, preferred_element_type=F32)
    for t in range(nt):
        cum_ref[t * E:(t + 1) * E, :] = mm(sel[:, t * LANES:(t + 1) * LANES], tri)
    tile_of = (lax.broadcasted_iota(jnp.int32, (seq, LANES), 0) // LANES
               == lax.broadcasted_iota(jnp.int32, (seq, LANES), 1)).astype(BF16)
    per_tile = mm(sel, tile_of)
    lane = lax.broadcasted_iota(jnp.int32, (1, LANES), 1)
    far = jnp.float32(2 * seq)
    t_end = jnp.where(lane < nt, mm(per_tile.astype(BF16), tri), far)
    t_start = jnp.where(lane < nt, t_end - per_tile, far)
    pad = jnp.zeros((LANES - E, LANES), F32)
    t_start_cols = jnp.concatenate([jnp.where(lane < nt, t_start, 0.0), pad], axis=0).T

    slot = lax.broadcasted_iota(jnp.int32, (cap, LANES), 0).astype(F32)
    ones = jnp.ones((LANES, LANES), BF16)
    zrows = jnp.zeros((LANES - nt, 2 * LANES), F32)
    for e0 in range(0, E, ROUTE_WAYS):
        es = range(e0, e0 + ROUTE_WAYS)
        tiles, picks, whole = [], [], []
        for e in es:
            absc = cum_ref[pl.ds(e, nt, stride=E), :] + t_start_cols[0:nt, e:e + 1]
            hi = jnp.where(absc >= 256.0, 1.0, 0.0) + jnp.where(absc >= 512.0, 1.0, 0.0)
            lo = absc - 256.0 * hi
            tiles.append(jnp.concatenate([jnp.concatenate([lo, hi], axis=1), zrows], axis=0).astype(BF16))
            done = jnp.where(t_end[e:e + 1] <= slot, 1.0, 0.0)
            whole.append(done)
            picks.append((jnp.where(t_start[e:e + 1] <= slot, 1.0, 0.0) - done).astype(BF16))
        rows = [mm(p, w) for p, w in zip(picks, tiles)]
        votes = [(jnp.where(r[:, :LANES] + 256.0 * r[:, LANES:] <= slot, 1.0, 0.0) + float(LANES) * d).astype(BF16)
                 for r, d in zip(rows, whole)]
        for e, v in zip(es, votes):
            idx_ref[0, :, e:e + 1] = mm(v, ones)[:, e:e + 1].astype(jnp.int32)


def _route(aff_t, batch, cap):
    E, T = aff_t.shape
    B, S = batch, T // batch
    return pl.pallas_call(
        functools.partial(_route_kernel, cap=cap, seq=S),
        grid=(B,),
        in_specs=[pl.BlockSpec((E, S), lambda b: (0, b))],
        out_specs=pl.BlockSpec((1, cap, E), lambda b: (b, 0, 0)),
        out_shape=jax.ShapeDtypeStruct((B, cap, E), jnp.int32),
        scratch_shapes=[pltpu.VMEM((S // LANES * E, LANES), F32)],
        compiler_params=_cparams(("parallel",), 32),
        name="route",
    )(aff_t)


def _gather_kernel(idx_ref, xn_ref, aff_ref, xg_ref, wg_ref, ws_ref, *, cap):
    e = pl.program_id(1)

    for i in range(cap):
        t = idx_ref[0, 0, i]
        src = pl.multiple_of(t * SUBLANES, SUBLANES)
        xg_ref[0, 0, i * SUBLANES:(i + 1) * SUBLANES, :] = xn_ref[0, pl.ds(src, SUBLANES), :]
        ws_ref[i:i + 1, :] = aff_ref[0, pl.ds(t, 1), :]
    lane = lax.broadcasted_iota(jnp.int32, (cap, N_EXPERTS), 1)
    wg_ref[0, 0] = jnp.sum(jnp.where(lane == e, ws_ref[...], 0.0), axis=1, keepdims=True)


def _gather(idx, xn_tiles, aff, cap):
    B, S, E = aff.shape
    return pl.pallas_call(
        functools.partial(_gather_kernel, cap=cap),
        grid=(B, E),
        in_specs=[pl.BlockSpec((1, 1, cap), lambda b, e: (b * E + e, 0, 0), memory_space=pltpu.SMEM),
                  pl.BlockSpec((1, S * SUBLANES, LANES), lambda b, e: (b, 0, 0)),
                  pl.BlockSpec((1, S, E), lambda b, e: (b, 0, 0))],
        out_specs=[pl.BlockSpec((1, 1, cap * SUBLANES, LANES), lambda b, e: (b, e, 0, 0)),
                   pl.BlockSpec((1, 1, cap, 1), lambda b, e: (b, e, 0, 0))],
        out_shape=[jax.ShapeDtypeStruct((B, E, cap * SUBLANES, LANES), F32),
                   jax.ShapeDtypeStruct((B, E, cap, 1), F32)],
        scratch_shapes=[pltpu.VMEM((cap, E), F32)],
        compiler_params=_cparams(("arbitrary", "arbitrary"), 48),
        name="gather",
    )(idx.reshape(B * E, 1, cap), xn_tiles, aff)


def _ffn_kernel(xg_ref, wg_ref, w1_ref, w2_ref, w3_ref, y_ref, b1_ref, b2_ref, b3_ref):
    @pl.when(pl.program_id(1) == 0)
    def _():
        b1_ref[...] = w1_ref[0].astype(BF16)
        b2_ref[...] = w2_ref[0].astype(BF16)
        b3_ref[...] = w3_ref[0].astype(BF16)

    rows = wg_ref.shape[2]
    for i in range(xg_ref.shape[0]):
        xg = _load_token_tiles(xg_ref.at[i, 0], rows).astype(BF16)
        gate = jnp.dot(xg, b1_ref[...], preferred_element_type=F32)
        up = jnp.dot(xg, b2_ref[...], preferred_element_type=F32)
        hid = (gate * _sigmoid(gate) * up).astype(BF16)
        y = jnp.dot(hid, b3_ref[...], preferred_element_type=F32) * wg_ref[i, 0]
        _store_token_tiles(y_ref.at[i, 0], y)


FFN_SEQS_PER_STEP = 2


def _ffn(xg, wg, w1, w2, w3):
    B, E, C, _ = wg.shape
    _, D, F = w1.shape
    n = FFN_SEQS_PER_STEP
    return pl.pallas_call(
        _ffn_kernel,
        grid=(E, B // n),
        in_specs=[pl.BlockSpec((n, 1, C * ROW_TILES, LANES), lambda e, b: (b, e, 0, 0)),
                  pl.BlockSpec((n, 1, C, 1), lambda e, b: (b, e, 0, 0)),
                  pl.BlockSpec((1, D, F), lambda e, b: (e, 0, 0)),
                  pl.BlockSpec((1, D, F), lambda e, b: (e, 0, 0)),
                  pl.BlockSpec((1, F, D), lambda e, b: (e, 0, 0))],
        out_specs=pl.BlockSpec((n, 1, C * ROW_TILES, LANES), lambda e, b: (b, e, 0, 0)),
        out_shape=jax.ShapeDtypeStruct((B, E, C * ROW_TILES, LANES), F32),
        scratch_shapes=[pltpu.VMEM((D, F), BF16), pltpu.VMEM((D, F), BF16), pltpu.VMEM((F, D), BF16)],
        compiler_params=_cparams(("arbitrary", "arbitrary"), 60),
        name="ffn",
    )(xg, wg, w1, w2, w3)


SCATTER_BATCH = 16


def _scatter_kernel(idx_ref, y_ref, o_ref, *, cap):
    @pl.when(pl.program_id(1) == 0)
    def _():
        o_ref[...] = jnp.zeros_like(o_ref)

    for i0 in range(0, cap, SCATTER_BATCH):
        slots = range(i0, i0 + SCATTER_BATCH)
        rows = [pl.ds(pl.multiple_of(idx_ref[0, 0, i] * SUBLANES, SUBLANES), SUBLANES) for i in slots]
        new = [o_ref[0, r, :] + y_ref[0, 0, i * SUBLANES:(i + 1) * SUBLANES, :] for r, i in zip(rows, slots)]
        for r, v in zip(rows, new):
            o_ref[0, r, :] = v


def _scatter(idx, y_tiles, seq):
    B, E, cap = idx.shape
    return pl.pallas_call(
        functools.partial(_scatter_kernel, cap=cap),
        grid=(B, E),
        in_specs=[pl.BlockSpec((1, 1, cap), lambda b, e: (b * E + e, 0, 0), memory_space=pltpu.SMEM),
                  pl.BlockSpec((1, 1, cap * SUBLANES, LANES), lambda b, e: (b, e, 0, 0))],
        out_specs=pl.BlockSpec((1, seq * SUBLANES, LANES), lambda b, e: (b, 0, 0)),
        out_shape=jax.ShapeDtypeStruct((B, seq * SUBLANES, LANES), F32),
        compiler_params=_cparams(("arbitrary", "arbitrary"), 48),
        name="scatter",
    )(idx.reshape(B * E, 1, cap), y_tiles)


def _ple_out_kernel(h_ref, moe_ref, p_ref, gple_ref, wpg_ref, wple_ref, gfin_ref, o_ref):
    sub, starts = _sub_blocks(h_ref.shape[0])
    mm = functools.partial(jnp.dot, preferred_element_type=F32)
    hs = [h_ref[r0:r0 + sub, :] + _load_token_tiles(moe_ref, sub, r0) for r0 in starts]
    ns = [_rms(h, gple_ref[...]).astype(BF16) for h in hs]
    gates = [_sigmoid(mm(n, wpg_ref[...])) for n in ns]
    embs = [mm(p_ref[r0:r0 + sub, :].astype(BF16), wple_ref[...]) for r0 in starts]
    for r0, h, gate, emb in zip(starts, hs, gates, embs):
        o_ref[r0:r0 + sub, :] = _rms(h + gate * emb, gfin_ref[...])


def _ple_out(h, moe, p2, gple, wpg, wple, gfin, tm):
    T = h.shape[0]
    row = lambda n: pl.BlockSpec((tm, n), lambda i: (i, 0))
    return pl.pallas_call(
        _ple_out_kernel,
        grid=(T // tm,),
        in_specs=[row(D_MODEL), pl.BlockSpec((tm * ROW_TILES, LANES), lambda i: (i, 0)), row(PLE_DIM),
                  _full(gple.shape), _full(wpg.shape), _full(wple.shape), _full(gfin.shape)],
        out_specs=row(D_MODEL),
        out_shape=jax.ShapeDtypeStruct((T, D_MODEL), F32),
        compiler_params=_cparams(("parallel",), 48),
        name="ple_out",
    )(h, moe, p2, gple, wpg, wple, gfin)


def kernel(x, p, positions, norm_mix, w_in, gla_gate_up_fwd, gla_gate_bias_fwd, gla_gate_up_bwd, gla_gate_bias_bwd, attn_sink, gla_norm, w_branch_attn, w_branch_gla, w_out, norm_ffn, w_router, w_exp_gate, w_exp_up, w_exp_down, norm_ple, w_ple_gate, w_ple, norm_final):
    B, S, D = x.shape
    T = B * S
    depth = w_in.shape[0]
    cap = CAPACITY_FACTOR * S // N_EXPERTS
    R = GLA_GATE_RANK

    posc = positions.reshape(T, 1)
    posr = positions.reshape(1, T)
    inv_freq = ROPE_THETA ** (-jnp.arange(0, ROPE_DIM, 2, dtype=F32) / ROPE_DIM)
    invfl = jnp.tile(inv_freq, LANES // (ROPE_DIM // 2)).reshape(1, LANES)
    invfc = inv_freq.reshape(ROPE_DIM // 2, 1)

    h = x.reshape(T, D)
    for l in range(depth):
        o = 0
        cols = {}
        for name, n in (("q", ATT_Q_DIM), ("k", ATT_KV_DIM), ("v", ATT_KV_DIM), ("gqk", 2 * GLA_KEY_DIM),
                        ("gv", GLA_VAL_DIM), ("gr", GLA_VAL_DIM), ("z", 2 * R), ("gate", 2 * D_MODEL)):
            cols[name] = w_in[l][:, o:o + n].astype(BF16)
            o += n
        per_head = lambda w: w.reshape(-1, GLA_HEADS, GLA_DK).swapaxes(0, 1)
        upf, upb = per_head(gla_gate_up_fwd[l]), per_head(gla_gate_up_bwd[l])
        up = jnp.concatenate([jnp.concatenate([upf, jnp.zeros_like(upf)], axis=2),
                              jnp.concatenate([jnp.zeros_like(upb), upb], axis=2)], axis=1)
        up_hi = up.astype(BF16)
        up_lo = (up - up_hi.astype(F32)).astype(BF16)
        upw = jnp.concatenate([up_hi, up_hi, up_lo, jnp.zeros_like(up_lo)], axis=1)
        wz4 = jnp.tile(cols["z"], (1, LANES // (2 * R)))
        gbias = jnp.concatenate([per_head(gla_gate_bias_fwd[l]), per_head(gla_gate_bias_bwd[l])], axis=2)
        wr = w_router[l]
        wr_hi = wr.astype(BF16)
        wr_lo = (wr - wr_hi.astype(F32)).astype(BF16)
        wr2 = jnp.concatenate([jnp.concatenate([wr_hi, wr_lo], axis=1),
                               jnp.concatenate([wr_hi, jnp.zeros_like(wr_lo)], axis=1)], axis=0).T

        qt, k0, k1, vt, gq, gk, gv, gr, z, sga, sgg = _in_proj(
            h, posc, posr, invfl, invfc, norm_mix[l].reshape(1, D), cols["q"].T,
            jnp.concatenate([cols["k"], wz4], axis=1), cols["v"].T,
            cols["gqk"], cols["gv"], cols["gr"], cols["gate"], tm=512)

        att = _swa(attn_sink[l], qt, k0.reshape(B, S, -1), k1.reshape(B, S, -1), vt, batch=B, tq=512)
        gla = _gla(z.reshape(B, S, -1), gq.reshape(B, S, -1), gk.reshape(B, S, -1), gv.reshape(B, S, -1),
                   gr.reshape(B, S, -1), upw, gbias, gla_norm[l].reshape(1, -1))

        h1, xn, aff, aff_t = _mix_out(h, att.reshape(T, -1), gla.reshape(T, -1), sga, sgg,
                               w_branch_attn[l].astype(BF16), w_branch_gla[l].astype(BF16),
                               w_out[l].astype(BF16), norm_ffn[l].reshape(1, D), wr2, tm=512)

        aff3 = aff.reshape(B, S, N_EXPERTS)
        idx = _route(aff_t, B, cap)
        idx = jnp.swapaxes(idx, 1, 2)
        xg, wg = _gather(idx, xn.reshape(B, S * SUBLANES, LANES), aff3, cap)
        y = _ffn(xg, wg, w_exp_gate[l], w_exp_up[l], w_exp_down[l])
        moe = _scatter(idx, y, S)

        last = l == depth - 1
        gfin = norm_final.reshape(1, D)
        assert last, "the final norm is fused into the last layer's PLE kernel"
        h = _ple_out(h1, moe.reshape(T * ROW_TILES, LANES), p[l].reshape(T, PLE_DIM), norm_ple[l].reshape(1, D),
                     w_ple_gate[l].astype(BF16), w_ple[l].astype(BF16), gfin, tm=512)
    return h.reshape(B, S, D)
```

```python
import functools
import math

import jax
import jax.numpy as jnp
from jax import lax
from jax.experimental import pallas as pl
from jax.experimental.pallas import tpu as pltpu

D_MODEL = 1024
ATT_HEADS = 8
ATT_KV_HEADS = 2
ATT_HEAD_DIM = 64
ATT_GROUP = ATT_HEADS // ATT_KV_HEADS
ATT_Q_DIM = ATT_HEADS * ATT_HEAD_DIM
ATT_KV_DIM = ATT_KV_HEADS * ATT_HEAD_DIM
WINDOW = 128
ROPE_DIM = ATT_HEAD_DIM // 4
ROPE_THETA = 500000.0
GLA_HEADS = 4
GLA_KEY_DIM = D_MODEL // 2
GLA_VAL_DIM = D_MODEL
GLA_DK = GLA_KEY_DIM // GLA_HEADS
GLA_DV = GLA_VAL_DIM // GLA_HEADS
GLA_GATE_RANK = 16
GLA_GATE_NORM = 16.0
GLA_CHUNK = 64
N_EXPERTS = 16
EXPERT_FF = D_MODEL
CAPACITY_FACTOR = 2
PLE_DIM = 256
EPS = 1e-6

LANES = 128
MIB = 1024 * 1024
BF16 = jnp.bfloat16
F32 = jnp.float32
LOG2E = math.log2(math.e)

NT_DIMS = (((1,), (1,)), ((), ()))
TN_DIMS = (((0,), (0,)), ((), ()))


def _cparams(sem, vmem_mib):
    return pltpu.CompilerParams(dimension_semantics=sem, vmem_limit_bytes=vmem_mib * MIB)


def _full(shape):
    n = len(shape)
    return pl.BlockSpec(shape, lambda *_: (0,) * n)


def _rms(x, gain):
    ms = jnp.mean(x * x, axis=-1, keepdims=True)
    return x * lax.rsqrt(ms + EPS) * gain


def _sigmoid(x):
    return 0.5 * jnp.tanh(0.5 * x) + 0.5


SUBLANES = 8
ROW_TILES = D_MODEL // LANES


def _store_token_tiles(ref2d, x, first_row=0):
    rows = x.shape[0]
    for j in range(ROW_TILES):
        ref2d[pl.ds(first_row * ROW_TILES + j, rows, stride=ROW_TILES), :] = x[:, j * LANES:(j + 1) * LANES]


def _load_token_tiles(ref2d, rows, first_row=0):
    return jnp.concatenate([ref2d[pl.ds(first_row * ROW_TILES + j, rows, stride=ROW_TILES), :]
                            for j in range(ROW_TILES)], axis=1)


SUB_ROWS = 256


def _sub_blocks(tile_rows):
    return SUB_ROWS, list(range(0, tile_rows, SUB_ROWS))


def _rope(t, cos_t, sin_t, first_half):
    fwd = pltpu.roll(t, LANES - ROPE_DIM // 2, axis=1)
    bwd = pltpu.roll(t, ROPE_DIM // 2, axis=1)
    return t * cos_t + jnp.where(first_half, fwd, bwd) * sin_t


def _in_proj_kernel(x_ref, posc_ref, posr_ref, invfl_ref, invfc_ref, gain_ref, wqt_ref, wkz_ref, wvt_ref,
                    wgqk_ref, wgv_ref, wgr_ref, wgate_ref,
                    qt_ref, k0_ref, k1_ref, vt_ref, gq_ref, gk_ref, gv_ref, gr_ref, z_ref, sga_ref, sgg_ref):
    a = _rms(x_ref[...], gain_ref[...]).astype(BF16)
    half = ROPE_DIM // 2

    qt = lax.dot_general(wqt_ref[...], a, NT_DIMS, preferred_element_type=F32)
    ang_t = invfc_ref[...] * posr_ref[...].astype(F32)
    cos_r, sin_r = jnp.cos(ang_t), jnp.sin(ang_t)
    qscale = ATT_HEAD_DIM ** -0.5 * LOG2E
    for h in range(ATT_HEADS):
        r0 = h * ATT_HEAD_DIM
        t1, t2 = qt[r0:r0 + half], qt[r0 + half:r0 + ROPE_DIM]
        qt_ref[r0:r0 + half, :] = ((t1 * cos_r - t2 * sin_r) * qscale).astype(BF16)
        qt_ref[r0 + half:r0 + ROPE_DIM, :] = ((t2 * cos_r + t1 * sin_r) * qscale).astype(BF16)
        qt_ref[r0 + ROPE_DIM:r0 + ATT_HEAD_DIM, :] = (qt[r0 + ROPE_DIM:r0 + ATT_HEAD_DIM] * qscale).astype(BF16)

    lane = lax.broadcasted_iota(jnp.int32, (1, LANES), 1)
    d = lane % ATT_HEAD_DIM
    first_half = d < half
    in_rope = d < ROPE_DIM
    ang = posc_ref[...].astype(F32) * invfl_ref[...]
    cos_t = jnp.where(in_rope, jnp.cos(ang), 1.0)
    sin_t = jnp.where(first_half, -jnp.sin(ang), jnp.where(in_rope, jnp.sin(ang), 0.0))
    kz = jnp.dot(a, wkz_ref[...], preferred_element_type=F32)
    z_ref[...] = kz[:, ATT_KV_DIM:]
    k = _rope(kz[:, :ATT_KV_DIM], cos_t, sin_t, first_half).astype(BF16)
    k0_ref[...] = k[:, :ATT_HEAD_DIM]
    k1_ref[...] = k[:, ATT_HEAD_DIM:]
    vt_ref[...] = lax.dot_general(wvt_ref[...], a, NT_DIMS, preferred_element_type=F32).astype(BF16)

    gqk = jnp.dot(a, wgqk_ref[...], preferred_element_type=F32)
    gq_ref[...] = (gqk[:, :GLA_KEY_DIM] * (GLA_DK ** -0.5)).astype(BF16)
    gk_ref[...] = gqk[:, GLA_KEY_DIM:].astype(BF16)
    gv_ref[...] = jnp.dot(a, wgv_ref[...], preferred_element_type=F32).astype(BF16)
    gr = jnp.dot(a, wgr_ref[...], preferred_element_type=F32)
    gr_ref[...] = (gr * _sigmoid(gr)).astype(BF16)
    gates = jnp.dot(a, wgate_ref[...], preferred_element_type=F32)
    sga_ref[...] = _sigmoid(gates[:, :D_MODEL]).astype(BF16)
    sgg_ref[...] = _sigmoid(gates[:, D_MODEL:]).astype(BF16)


def _in_proj(x2, posc, posr, invfl, invfc, gain, wqt, wkz, wvt, wgqk, wgv, wgr, wgate, tm):
    T = x2.shape[0]
    row = lambda n: pl.BlockSpec((tm, n), lambda i: (i, 0))
    col = lambda n: pl.BlockSpec((n, tm), lambda i: (0, i))
    row_widths = (ATT_HEAD_DIM, ATT_HEAD_DIM, None, GLA_KEY_DIM, GLA_KEY_DIM, GLA_VAL_DIM,
                  GLA_VAL_DIM, wkz.shape[1] - ATT_KV_DIM, D_MODEL, D_MODEL)
    row_dtypes = (BF16,) * 7 + (F32, BF16, BF16)
    out_specs = [col(ATT_Q_DIM)]
    out_shape = [jax.ShapeDtypeStruct((ATT_Q_DIM, T), BF16)]
    for n, dt in zip(row_widths, row_dtypes):
        if n is None:
            out_specs.append(col(ATT_KV_DIM))
            out_shape.append(jax.ShapeDtypeStruct((ATT_KV_DIM, T), BF16))
        else:
            out_specs.append(row(n))
            out_shape.append(jax.ShapeDtypeStruct((T, n), dt))
    consts = (invfl, invfc, gain, wqt, wkz, wvt, wgqk, wgv, wgr, wgate)
    return pl.pallas_call(
        _in_proj_kernel,
        grid=(T // tm,),
        in_specs=[row(D_MODEL), row(1), col(1)] + [_full(c.shape) for c in consts],
        out_specs=out_specs,
        out_shape=out_shape,
        compiler_params=_cparams(("parallel",), 56),
        name="in_proj",
    )(x2, posc, posr, *consts)


def _swa_kernel(sink_ref, qt_ref, k0_ref, k1_ref, vt_ref, o_ref, *, tq, seq):
    blk = WINDOW
    span = 3 * blk
    hd = ATT_HEAD_DIM
    n = pl.program_id(1)
    ones = jnp.ones((16, span), BF16)
    kv_refs = (k0_ref, k1_ref)

    def window_start(sb):
        return pl.multiple_of(jnp.clip(n * tq + (sb - 1) * blk, 0, seq - span), blk)

    def scores(sb, g):
        kw = kv_refs[g][0, pl.ds(window_start(sb), span), :]
        heads = range(g * ATT_GROUP, (g + 1) * ATT_GROUP)
        qs = jnp.concatenate([qt_ref[h * hd:(h + 1) * hd, sb * blk:(sb + 1) * blk] for h in heads], axis=1)
        return jnp.dot(kw, qs, preferred_element_type=F32)

    work = [(sb, g) for sb in range(tq // blk) for g in range(ATT_KV_HEADS)]
    s_next = scores(*work[0])
    outs = []
    for step, (sb, g) in enumerate(work):
        s_all = s_next
        if step + 1 < len(work):
            s_next = scores(*work[step + 1])
        q0 = n * tq + sb * blk
        start = window_start(sb)
        kj = start + lax.broadcasted_iota(jnp.int32, (span, blk), 0)
        qi = q0 + lax.broadcasted_iota(jnp.int32, (span, blk), 1)
        valid = jnp.abs(qi - kj) <= WINDOW
        vaug = jnp.concatenate([vt_ref[g * hd:(g + 1) * hd, pl.ds(start, span)], ones], axis=0)
        sinks = [sink_ref[g * ATT_GROUP + i] * LOG2E for i in range(ATT_GROUP)]
        ss = [jnp.where(valid, s_all[:, i * blk:(i + 1) * blk], -jnp.inf) for i in range(ATT_GROUP)]
        ms = [jnp.maximum(jnp.max(s, axis=0, keepdims=True), sink) for s, sink in zip(ss, sinks)]
        rs = [jnp.dot(vaug, jnp.exp2(s - m).astype(BF16), preferred_element_type=F32) for s, m in zip(ss, ms)]
        outs += [r[:hd] / (r[hd:hd + 1] + jnp.exp2(sink - m)) for r, m, sink in zip(rs, ms, sinks)]
        if g == ATT_KV_HEADS - 1:
            for pr in range(ATT_HEADS // 2):
                pair = jnp.concatenate([outs[2 * pr], outs[2 * pr + 1]], axis=0)
                o_ref[0, sb * blk:(sb + 1) * blk, pr * 2 * hd:(pr + 1) * 2 * hd] = pair.T.astype(BF16)
            outs = []


def _swa(sink, qt, k0, k1, vt, batch, tq):
    S = k0.shape[1]
    nq = S // tq
    kspec = pl.BlockSpec((1, S, ATT_HEAD_DIM), lambda b, n: (b, 0, 0))
    return pl.pallas_call(
        functools.partial(_swa_kernel, tq=tq, seq=S),
        grid=(batch, nq),
        in_specs=[pl.BlockSpec(memory_space=pltpu.SMEM),
                  pl.BlockSpec((ATT_Q_DIM, tq), lambda b, n: (0, b * nq + n)),
                  kspec, kspec,
                  pl.BlockSpec((ATT_KV_DIM, S), lambda b, n: (0, b))],
        out_specs=pl.BlockSpec((1, tq, ATT_Q_DIM), lambda b, n: (b, n, 0)),
        out_shape=jax.ShapeDtypeStruct((batch, S, ATT_Q_DIM), BF16),
        compiler_params=_cparams(("parallel", "parallel"), 32),
        name="swa",
    )(sink, qt, k0, k1, vt)


def _log2_sigmoid(u):
    return jnp.minimum(u, 0.0) * LOG2E - jnp.log2(1.0 + jnp.exp2(jnp.abs(u) * -LOG2E))


def _split2(x):
    hi = x.astype(BF16)
    return hi, (x - hi.astype(F32)).astype(BF16)


GLA_WAYS = 4


def _gla_kernel(z_ref, q_ref, k_ref, v_ref, r_ref, upw_ref, bias_ref, gain_ref,
                o_ref, cf_ref, cb_ref, kef_ref, keb_ref, st_ref, s_ref, *, seq):
    L = GLA_CHUNK
    R2 = 2 * GLA_GATE_RANK
    nc = seq // L
    grp = 4 * L
    cpg = grp // L
    dk = GLA_DK
    mm = functools.partial(jnp.dot, preferred_element_type=F32)
    nt = functools.partial(lax.dot_general, dimension_numbers=NT_DIMS, preferred_element_type=F32)

    row = lax.broadcasted_iota(jnp.int32, (grp, grp), 0)
    col = lax.broadcasted_iota(jnp.int32, (grp, grp), 1)
    same = (row // L) == (col // L)
    fwd_mask = same & (col <= row)
    bwd_mask = same & (col > row)
    tri_lo = jnp.where(fwd_mask, 1.0, 0.0).astype(BF16)
    tri_up = jnp.where(same & (col >= row), 1.0, 0.0).astype(BF16)
    lane = lax.broadcasted_iota(jnp.int32, (1, LANES), 1)
    use_lo = (lane >= R2) & (lane < 2 * R2)

    def group_starts(i):
        return [pl.multiple_of((i * GLA_WAYS + w) * grp, grp) for w in range(GLA_WAYS)]

    def cum_body(i, carry):
        r0s = group_starts(i)
        zs = [_split2(z_ref[0, pl.ds(r0, grp), :]) for r0 in r0s]
        us = [mm(jnp.where(use_lo, zl, zh), upw_ref[...]) + bias_ref[...] for zh, zl in zs]
        las = [_split2(_log2_sigmoid(u) * (1.0 / GLA_GATE_NORM)) for u in us]
        cfxs = [mm(tri_lo, jnp.concatenate([lh[:, :dk], ll[:, :dk]], axis=1)) for lh, ll in las]
        cbxs = [mm(tri_up, jnp.concatenate([lh[:, dk:], ll[:, dk:]], axis=1)) for lh, ll in las]
        for r0, cfx, cbx in zip(r0s, cfxs, cbxs):
            cf = cfx[:, :dk] + cfx[:, dk:]
            cb = cbx[:, :dk] + cbx[:, dk:]
            cf_ref[pl.ds(r0, grp), :] = cf
            cb_ref[pl.ds(r0, grp), :] = cb
            k = k_ref[0, pl.ds(r0, grp), :].astype(F32)
            for c in range(cpg):
                sl = slice(c * L, (c + 1) * L)
                gf = cf[(c + 1) * L - 1:(c + 1) * L]
                gb = cb[c * L:c * L + 1]
                kef_ref[pl.ds(r0 + c * L, L), :] = (k[sl] * jnp.exp2(gf - cf[sl])).astype(BF16)
                keb_ref[pl.ds(r0 + c * L, L), :] = (k[sl] * jnp.exp2(gb - cb[sl])).astype(BF16)
        return carry

    lax.fori_loop(0, seq // (grp * GLA_WAYS), cum_body, 0)

    s_ref[...] = jnp.zeros_like(s_ref)
    zero_k = jnp.zeros((L, dk), BF16)

    def state_body(i, carry):
        j = nc - 1 - i
        rf = pl.multiple_of(i * L, L)
        rb = pl.multiple_of(j * L, L)
        vcat = jnp.concatenate([v_ref[0, pl.ds(rf, L), :], v_ref[0, pl.ds(rb, L), :]], axis=0)
        kblk = jnp.concatenate([jnp.concatenate([kef_ref[pl.ds(rf, L), :], zero_k], axis=1),
                                jnp.concatenate([zero_k, keb_ref[pl.ds(rb, L), :]], axis=1)], axis=0)
        kv = lax.dot_general(vcat, kblk, TN_DIMS, preferred_element_type=F32)
        decay = jnp.exp2(jnp.concatenate([cf_ref[pl.ds(rf + L - 1, 1), :], cb_ref[pl.ds(rb, 1), :]], axis=1))
        s = s_ref[...]
        st_ref[i, :, 0:dk] = s[:, :dk].astype(BF16)
        st_ref[j, :, dk:2 * dk] = s[:, dk:].astype(BF16)
        s_ref[...] = s * decay + kv
        return carry

    lax.fori_loop(0, nc, state_body, 0, unroll=16)

    def out_body(i, carry):
        r0s = group_starts(i)
        ops = []
        for r0 in r0s:
            q = q_ref[0, pl.ds(r0, grp), :].astype(F32)
            k = k_ref[0, pl.ds(r0, grp), :].astype(F32)
            cf = cf_ref[pl.ds(r0, grp), :]
            cb = cb_ref[pl.ds(r0, grp), :]
            ops.append(((q * jnp.exp2(cf)).astype(BF16), (k * jnp.exp2(-cf)).astype(BF16),
                        (q * jnp.exp2(cb)).astype(BF16), (k * jnp.exp2(-cb)).astype(BF16)))
        scores = [(nt(qf, kf), nt(qb, kb)) for qf, kf, qb, kb in ops]
        attns = [jnp.where(fwd_mask, af, jnp.where(bwd_mask, ab, 0.0)).astype(BF16) for af, ab in scores]
        outs = []
        for r0, attn, (qf, _, qb, _) in zip(r0s, attns, ops):
            c0 = r0 // L
            qcat = jnp.concatenate([qf, qb], axis=1)
            inter = jnp.concatenate([nt(qcat[c * L:(c + 1) * L], st_ref[c0 + c]) for c in range(cpg)], axis=0)
            outs.append(mm(attn, v_ref[0, pl.ds(r0, grp), :]) + inter)
        for r0, o in zip(r0s, outs):
            o_ref[0, pl.ds(r0, grp), :] = (_rms(o, gain_ref[...]) * r_ref[0, pl.ds(r0, grp), :].astype(F32)).astype(BF16)
        return carry

    lax.fori_loop(0, seq // (grp * GLA_WAYS), out_body, 0)


def _gla(z, gq, gk, gv, gr, upw, bias, gain):
    B, S, _ = gq.shape
    nc = S // GLA_CHUNK
    seq_blk = lambda n: pl.BlockSpec((1, S, n), lambda b, h: (b, 0, h))
    head_blk = lambda r, n: pl.BlockSpec((None, r, n), lambda b, h: (h, 0, 0))
    return pl.pallas_call(
        functools.partial(_gla_kernel, seq=S),
        grid=(B, GLA_HEADS),
        in_specs=[pl.BlockSpec((1, S, LANES), lambda b, h: (b, 0, 0)),
                  seq_blk(GLA_DK), seq_blk(GLA_DK), seq_blk(GLA_DV), seq_blk(GLA_DV),
                  head_blk(LANES, 2 * GLA_DK), head_blk(1, 2 * GLA_DK),
                  pl.BlockSpec((1, GLA_DV), lambda b, h: (0, h))],
        out_specs=seq_blk(GLA_DV),
        out_shape=jax.ShapeDtypeStruct((B, S, GLA_VAL_DIM), BF16),
        scratch_shapes=[pltpu.VMEM((S, GLA_DK), F32), pltpu.VMEM((S, GLA_DK), F32),
                        pltpu.VMEM((S, GLA_DK), BF16), pltpu.VMEM((S, GLA_DK), BF16),
                        pltpu.VMEM((nc, GLA_DV, 2 * GLA_DK), BF16),
                        pltpu.VMEM((GLA_DV, 2 * GLA_DK), F32)],
        compiler_params=_cparams(("parallel", "parallel"), 48),
        name="gla",
    )(z, gq, gk, gv, gr, upw, bias, gain)


def _mix_out_kernel(x_ref, a_ref, g_ref, sga_ref, sgg_ref, wa_ref, wb_ref, wo_ref, gain_ref, wr_ref,
                    h_ref, xn_ref, aff_ref, afft_ref):
    sub, starts = _sub_blocks(x_ref.shape[0])
    mm = functools.partial(jnp.dot, preferred_element_type=F32)
    blk = lambda ref, r0: ref[r0:r0 + sub, :]
    y_att = [mm(blk(a_ref, r0), wa_ref[...]) for r0 in starts]
    y_gla = [mm(blk(g_ref, r0), wb_ref[...]) for r0 in starts]
    merged = [(blk(sga_ref, r0).astype(F32) * ya + blk(sgg_ref, r0).astype(F32) * yg).astype(BF16)
              for r0, ya, yg in zip(starts, y_att, y_gla)]
    hs = [blk(x_ref, r0) + mm(m, wo_ref[...]) for r0, m in zip(starts, merged)]
    xns = [_rms(h, gain_ref[...]) for h in hs]
    for r0, h, xn in zip(starts, hs, xns):
        h_ref[r0:r0 + sub, :] = h
        _store_token_tiles(xn_ref, xn, r0)
    his = [xn.astype(BF16) for xn in xns]
    parts = [lax.dot_general(wr_ref[...], jnp.concatenate([hi, (xn - hi.astype(F32)).astype(BF16)], axis=1),
                             NT_DIMS, preferred_element_type=F32) for xn, hi in zip(xns, his)]
    pad = jnp.zeros((LANES - N_EXPERTS, sub), F32)
    for r0, part in zip(starts, parts):
        logits = part[:N_EXPERTS] + part[N_EXPERTS:]
        e = jnp.exp(logits - jnp.max(logits, axis=0, keepdims=True))
        aff_t = e / jnp.sum(e, axis=0, keepdims=True)
        afft_ref[:, r0:r0 + sub] = aff_t
        aff_ref[r0:r0 + sub, :] = jnp.concatenate([aff_t, pad], axis=0).T[:, :N_EXPERTS]


def _mix_out(x2, a, g, sga, sgg, wa, wb, wo, gain, wr, tm):
    T = x2.shape[0]
    row = lambda n: pl.BlockSpec((tm, n), lambda i: (i, 0))
    return pl.pallas_call(
        _mix_out_kernel,
        grid=(T // tm,),
        in_specs=[row(D_MODEL), row(ATT_Q_DIM), row(GLA_VAL_DIM), row(D_MODEL), row(D_MODEL),
                  _full(wa.shape), _full(wb.shape), _full(wo.shape), _full(gain.shape), _full(wr.shape)],
        out_specs=[row(D_MODEL), pl.BlockSpec((tm * ROW_TILES, LANES), lambda i: (i, 0)), row(N_EXPERTS),
                   pl.BlockSpec((N_EXPERTS, tm), lambda i: (0, i))],
        out_shape=[jax.ShapeDtypeStruct((T, D_MODEL), F32), jax.ShapeDtypeStruct((T * ROW_TILES, LANES), F32),
                   jax.ShapeDtypeStruct((T, N_EXPERTS), F32), jax.ShapeDtypeStruct((N_EXPERTS, T), F32)],
        compiler_params=_cparams(("parallel",), 48),
        name="mix_out",
    )(x2, a, g, sga, sgg, wa, wb, wo, gain, wr)


ROUTE_WAYS = 4


def _route_kernel(aff_ref, idx_ref, cum_ref, *, cap, seq):
    E = N_EXPERTS
    aff = aff_ref[...]
    count = lambda mask: jnp.sum(mask.astype(jnp.int32), axis=1, keepdims=True)
    as_float = lambda pattern: lax.bitcast_convert_type(pattern, F32)

    def thr_body(t, pattern):
        cand = pattern | jnp.left_shift(jnp.int32(1), 30 - t)
        return jnp.where(count(aff >= as_float(cand)) >= cap, cand, pattern)

    thr = as_float(lax.fori_loop(0, 31, thr_body, jnp.zeros((E, 1), jnp.int32)))
    above = aff > thr
    tie = aff == thr
    need = cap - count(above)

    pos = lax.broadcasted_iota(jnp.int32, (E, seq), 1)

    def tie_body(t, last):
        cand = last | jnp.left_shift(jnp.int32(1), (seq.bit_length() - 2) - t)
        return jnp.where(count(tie & (pos < cand)) < need, cand, last)

    last = lax.fori_loop(0, seq.bit_length() - 1, tie_body, jnp.zeros((E, 1), jnp.int32))
    sel = (above | (tie & (pos <= last))).astype(BF16)

    nt = seq // LANES
    lrow = lax.broadcasted_iota(jnp.int32, (LANES, LANES), 0)
    lcol = lax.broadcasted_iota(jnp.int32, (LANES, LANES), 1)
    tri = (lrow <= lcol).astype(BF16)
    mm = functools.partial(jnp.dot, preferred_element_type=F32)
    for t in range(nt):
        cum_ref[t * E:(t + 1) * E, :] = mm(sel[:, t * LANES:(t + 1) * LANES], tri)
    tile_of = (lax.broadcasted_iota(jnp.int32, (seq, LANES), 0) // LANES
               == lax.broadcasted_iota(jnp.int32, (seq, LANES), 1)).astype(BF16)
    per_tile = mm(sel, tile_of)
    lane = lax.broadcasted_iota(jnp.int32, (1, LANES), 1)
    far = jnp.float32(2 * seq)
    t_end = jnp.where(lane < nt, mm(per_tile.astype(BF16), tri), far)
    t_start = jnp.where(lane < nt, t_end - per_tile, far)
    pad = jnp.zeros((LANES - E, LANES), F32)
    t_start_cols = jnp.concatenate([jnp.where(lane < nt, t_start, 0.0), pad], axis=0).T

    slot = lax.broadcasted_iota(jnp.int32, (cap, LANES), 0).astype(F32)
    ones = jnp.ones((LANES, LANES), BF16)
    zrows = jnp.zeros((LANES - nt, 2 * LANES), F32)
    for e0 in range(0, E, ROUTE_WAYS):
        es = range(e0, e0 + ROUTE_WAYS)
        tiles, picks, whole = [], [], []
        for e in es:
            absc = cum_ref[pl.ds(e, nt, stride=E), :] + t_start_cols[0:nt, e:e + 1]
            hi = jnp.where(absc >= 256.0, 1.0, 0.0) + jnp.where(absc >= 512.0, 1.0, 0.0)
            lo = absc - 256.0 * hi
            tiles.append(jnp.concatenate([jnp.concatenate([lo, hi], axis=1), zrows], axis=0).astype(BF16))
            done = jnp.where(t_end[e:e + 1] <= slot, 1.0, 0.0)
            whole.append(done)
            picks.append((jnp.where(t_start[e:e + 1] <= slot, 1.0, 0.0) - done).astype(BF16))
        rows = [mm(p, w) for p, w in zip(picks, tiles)]
        votes = [(jnp.where(r[:, :LANES] + 256.0 * r[:, LANES:] <= slot, 1.0, 0.0) + float(LANES) * d).astype(BF16)
                 for r, d in zip(rows, whole)]
        for e, v in zip(es, votes):
            idx_ref[0, :, e:e + 1] = mm(v, ones)[:, e:e + 1].astype(jnp.int32)


def _route(aff_t, batch, cap):
    E, T = aff_t.shape
    B, S = batch, T // batch
    return pl.pallas_call(
        functools.partial(_route_kernel, cap=cap, seq=S),
        grid=(B,),
        in_specs=[pl.BlockSpec((E, S), lambda b: (0, b))],
        out_specs=pl.BlockSpec((1, cap, E), lambda b: (b, 0, 0)),
        out_shape=jax.ShapeDtypeStruct((B, cap, E), jnp.int32),
        scratch_shapes=[pltpu.VMEM((S // LANES * E, LANES), F32)],
        compiler_params=_cparams(("parallel",), 32),
        name="route",
    )(aff_t)


def _gather_kernel(idx_ref, xn_ref, aff_ref, xg_ref, wg_ref, ws_ref, *, cap):
    e = pl.program_id(1)

    for i in range(cap):
        t = idx_ref[0, 0, i]
        src = pl.multiple_of(t * SUBLANES, SUBLANES)
        xg_ref[0, 0, i * SUBLANES:(i + 1) * SUBLANES, :] = xn_ref[0, pl.ds(src, SUBLANES), :]
        ws_ref[i:i + 1, :] = aff_ref[0, pl.ds(t, 1), :]
    lane = lax.broadcasted_iota(jnp.int32, (cap, N_EXPERTS), 1)
    wg_ref[0, 0] = jnp.sum(jnp.where(lane == e, ws_ref[...], 0.0), axis=1, keepdims=True)


def _gather(idx, xn_tiles, aff, cap):
    B, S, E = aff.shape
    return pl.pallas_call(
        functools.partial(_gather_kernel, cap=cap),
        grid=(B, E),
        in_specs=[pl.BlockSpec((1, 1, cap), lambda b, e: (b * E + e, 0, 0), memory_space=pltpu.SMEM),
                  pl.BlockSpec((1, S * SUBLANES, LANES), lambda b, e: (b, 0, 0)),
                  pl.BlockSpec((1, S, E), lambda b, e: (b, 0, 0))],
        out_specs=[pl.BlockSpec((1, 1, cap * SUBLANES, LANES), lambda b, e: (b, e, 0, 0)),
                   pl.BlockSpec((1, 1, cap, 1), lambda b, e: (b, e, 0, 0))],
        out_shape=[jax.ShapeDtypeStruct((B, E, cap * SUBLANES, LANES), F32),
                   jax.ShapeDtypeStruct((B, E, cap, 1), F32)],
        scratch_shapes=[pltpu.VMEM((cap, E), F32)],
        compiler_params=_cparams(("arbitrary", "arbitrary"), 48),
        name="gather",
    )(idx.reshape(B * E, 1, cap), xn_tiles, aff)


def _ffn_kernel(xg_ref, wg_ref, w1_ref, w2_ref, w3_ref, y_ref, b1_ref, b2_ref, b3_ref):
    @pl.when(pl.program_id(1) == 0)
    def _():
        b1_ref[...] = w1_ref[0].astype(BF16)
        b2_ref[...] = w2_ref[0].astype(BF16)
        b3_ref[...] = w3_ref[0].astype(BF16)

    rows = wg_ref.shape[2]
    seqs = range(xg_ref.shape[0])
    mm = functools.partial(jnp.dot, preferred_element_type=F32)
    xgs = [_load_token_tiles(xg_ref.at[i, 0], rows).astype(BF16) for i in seqs]
    gates = [mm(xg, b1_ref[...]) for xg in xgs]
    ups = [mm(xg, b2_ref[...]) for xg in xgs]
    hids = [(gate * _sigmoid(gate) * up).astype(BF16) for gate, up in zip(gates, ups)]
    ys = [mm(hid, b3_ref[...]) * wg_ref[i, 0] for i, hid in zip(seqs, hids)]
    for i, y in zip(seqs, ys):
        _store_token_tiles(y_ref.at[i, 0], y)


FFN_SEQS_PER_STEP = 2


def _ffn(xg, wg, w1, w2, w3):
    B, E, C, _ = wg.shape
    _, D, F = w1.shape
    n = FFN_SEQS_PER_STEP
    return pl.pallas_call(
        _ffn_kernel,
        grid=(E, B // n),
        in_specs=[pl.BlockSpec((n, 1, C * ROW_TILES, LANES), lambda e, b: (b, e, 0, 0)),
                  pl.BlockSpec((n, 1, C, 1), lambda e, b: (b, e, 0, 0)),
                  pl.BlockSpec((1, D, F), lambda e, b: (e, 0, 0)),
                  pl.BlockSpec((1, D, F), lambda e, b: (e, 0, 0)),
                  pl.BlockSpec((1, F, D), lambda e, b: (e, 0, 0))],
        out_specs=pl.BlockSpec((n, 1, C * ROW_TILES, LANES), lambda e, b: (b, e, 0, 0)),
        out_shape=jax.ShapeDtypeStruct((B, E, C * ROW_TILES, LANES), F32),
        scratch_shapes=[pltpu.VMEM((D, F), BF16), pltpu.VMEM((D, F), BF16), pltpu.VMEM((F, D), BF16)],
        compiler_params=_cparams(("arbitrary", "arbitrary"), 60),
        name="ffn",
    )(xg, wg, w1, w2, w3)


SCATTER_BATCH = 16


def _scatter_kernel(idx_ref, y_ref, o_ref, *, cap):
    @pl.when(pl.program_id(1) == 0)
    def _():
        o_ref[...] = jnp.zeros_like(o_ref)

    for i0 in range(0, cap, SCATTER_BATCH):
        slots = range(i0, i0 + SCATTER_BATCH)
        rows = [pl.ds(pl.multiple_of(idx_ref[0, 0, i] * SUBLANES, SUBLANES), SUBLANES) for i in slots]
        new = [o_ref[0, r, :] + y_ref[0, 0, i * SUBLANES:(i + 1) * SUBLANES, :] for r, i in zip(rows, slots)]
        for r, v in zip(rows, new):
            o_ref[0, r, :] = v


def _scatter(idx, y_tiles, seq):
    B, E, cap = idx.shape
    return pl.pallas_call(
        functools.partial(_scatter_kernel, cap=cap),
        grid=(B, E),
        in_specs=[pl.BlockSpec((1, 1, cap), lambda b, e: (b * E + e, 0, 0), memory_space=pltpu.SMEM),
                  pl.BlockSpec((1, 1, cap * SUBLANES, LANES), lambda b, e: (b, e, 0, 0))],
        out_specs=pl.BlockSpec((1, seq * SUBLANES, LANES), lambda b, e: (b, 0, 0)),
        out_shape=jax.ShapeDtypeStruct((B, seq * SUBLANES, LANES), F32),
        compiler_params=_cparams(("arbitrary", "arbitrary"), 48),
        name="scatter",
    )(idx.reshape(B * E, 1, cap), y_tiles)


def _ple_out_kernel(h_ref, moe_ref, p_ref, gple_ref, wpg_ref, wple_ref, gfin_ref, o_ref):
    sub, starts = _sub_blocks(h_ref.shape[0])
    mm = functools.partial(jnp.dot, preferred_element_type=F32)
    hs = [h_ref[r0:r0 + sub, :] + _load_token_tiles(moe_ref, sub, r0) for r0 in starts]
    ns = [_rms(h, gple_ref[...]).astype(BF16) for h in hs]
    gates = [_sigmoid(mm(n, wpg_ref[...])) for n in ns]
    embs = [mm(p_ref[r0:r0 + sub, :].astype(BF16), wple_ref[...]) for r0 in starts]
    for r0, h, gate, emb in zip(starts, hs, gates, embs):
        o_ref[r0:r0 + sub, :] = _rms(h + gate * emb, gfin_ref[...])


def _ple_out(h, moe, p2, gple, wpg, wple, gfin, tm):
    T = h.shape[0]
    row = lambda n: pl.BlockSpec((tm, n), lambda i: (i, 0))
    return pl.pallas_call(
        _ple_out_kernel,
        grid=(T // tm,),
        in_specs=[row(D_MODEL), pl.BlockSpec((tm * ROW_TILES, LANES), lambda i: (i, 0)), row(PLE_DIM),
                  _full(gple.shape), _full(wpg.shape), _full(wple.shape), _full(gfin.shape)],
        out_specs=row(D_MODEL),
        out_shape=jax.ShapeDtypeStruct((T, D_MODEL), F32),
        compiler_params=_cparams(("parallel",), 48),
        name="ple_out",
    )(h, moe, p2, gple, wpg, wple, gfin)


def kernel(x, p, positions, norm_mix, w_in, gla_gate_up_fwd, gla_gate_bias_fwd, gla_gate_up_bwd, gla_gate_bias_bwd, attn_sink, gla_norm, w_branch_attn, w_branch_gla, w_out, norm_ffn, w_router, w_exp_gate, w_exp_up, w_exp_down, norm_ple, w_ple_gate, w_ple, norm_final):
    B, S, D = x.shape
    T = B * S
    depth = w_in.shape[0]
    cap = CAPACITY_FACTOR * S // N_EXPERTS
    R = GLA_GATE_RANK

    posc = positions.reshape(T, 1)
    posr = positions.reshape(1, T)
    inv_freq = ROPE_THETA ** (-jnp.arange(0, ROPE_DIM, 2, dtype=F32) / ROPE_DIM)
    invfl = jnp.tile(inv_freq, LANES // (ROPE_DIM // 2)).reshape(1, LANES)
    invfc = inv_freq.reshape(ROPE_DIM // 2, 1)

    h = x.reshape(T, D)
    for l in range(depth):
        o = 0
        cols = {}
        for name, n in (("q", ATT_Q_DIM), ("k", ATT_KV_DIM), ("v", ATT_KV_DIM), ("gqk", 2 * GLA_KEY_DIM),
                        ("gv", GLA_VAL_DIM), ("gr", GLA_VAL_DIM), ("z", 2 * R), ("gate", 2 * D_MODEL)):
            cols[name] = w_in[l][:, o:o + n].astype(BF16)
            o += n
        per_head = lambda w: w.reshape(-1, GLA_HEADS, GLA_DK).swapaxes(0, 1)
        upf, upb = per_head(gla_gate_up_fwd[l]), per_head(gla_gate_up_bwd[l])
        up = jnp.concatenate([jnp.concatenate([upf, jnp.zeros_like(upf)], axis=2),
                              jnp.concatenate([jnp.zeros_like(upb), upb], axis=2)], axis=1)
        up_hi = up.astype(BF16)
        up_lo = (up - up_hi.astype(F32)).astype(BF16)
        upw = jnp.concatenate([up_hi, up_hi, up_lo, jnp.zeros_like(up_lo)], axis=1)
        wz4 = jnp.tile(cols["z"], (1, LANES // (2 * R)))
        gbias = jnp.concatenate([per_head(gla_gate_bias_fwd[l]), per_head(gla_gate_bias_bwd[l])], axis=2)
        wr = w_router[l]
        wr_hi = wr.astype(BF16)
        wr_lo = (wr - wr_hi.astype(F32)).astype(BF16)
        wr2 = jnp.concatenate([jnp.concatenate([wr_hi, wr_lo], axis=1),
                               jnp.concatenate([wr_hi, jnp.zeros_like(wr_lo)], axis=1)], axis=0).T

        qt, k0, k1, vt, gq, gk, gv, gr, z, sga, sgg = _in_proj(
            h, posc, posr, invfl, invfc, norm_mix[l].reshape(1, D), cols["q"].T,
            jnp.concatenate([cols["k"], wz4], axis=1), cols["v"].T,
            cols["gqk"], cols["gv"], cols["gr"], cols["gate"], tm=512)

        att = _swa(attn_sink[l], qt, k0.reshape(B, S, -1), k1.reshape(B, S, -1), vt, batch=B, tq=512)
        gla = _gla(z.reshape(B, S, -1), gq.reshape(B, S, -1), gk.reshape(B, S, -1), gv.reshape(B, S, -1),
                   gr.reshape(B, S, -1), upw, gbias, gla_norm[l].reshape(1, -1))

        h1, xn, aff, aff_t = _mix_out(h, att.reshape(T, -1), gla.reshape(T, -1), sga, sgg,
                               w_branch_attn[l].astype(BF16), w_branch_gla[l].astype(BF16),
                               w_out[l].astype(BF16), norm_ffn[l].reshape(1, D), wr2, tm=512)

        aff3 = aff.reshape(B, S, N_EXPERTS)
        idx = _route(aff_t, B, cap)
        idx = jnp.swapaxes(idx, 1, 2)
        xg, wg = _gather(idx, xn.reshape(B, S * SUBLANES, LANES), aff3, cap)
        y = _ffn(xg, wg, w_exp_gate[l], w_exp_up[l], w_exp_down[l])
        moe = _scatter(idx, y, S)

        last = l == depth - 1
        gfin = norm_final.reshape(1, D)
        assert last, "the final norm is fused into the last layer's PLE kernel"
        h = _ple_out(h1, moe.reshape(T * ROW_TILES, LANES), p[l].reshape(T, PLE_DIM), norm_ple[l].reshape(1, D),
                     w_ple_gate[l].astype(BF16), w_ple[l].astype(BF16), gfin, tm=1024)
    return h.reshape(B, S, D)
```

```python
import functools
import math

import jax
import jax.numpy as jnp
from jax import lax
from jax.experimental import pallas as pl
from jax.experimental.pallas import tpu as pltpu

D_MODEL = 1024
ATT_HEADS = 8
ATT_KV_HEADS = 2
ATT_HEAD_DIM = 64
ATT_GROUP = ATT_HEADS // ATT_KV_HEADS
ATT_Q_DIM = ATT_HEADS * ATT_HEAD_DIM
ATT_KV_DIM = ATT_KV_HEADS * ATT_HEAD_DIM
WINDOW = 128
ROPE_DIM = ATT_HEAD_DIM // 4
ROPE_THETA = 500000.0
GLA_HEADS = 4
GLA_KEY_DIM = D_MODEL // 2
GLA_VAL_DIM = D_MODEL
GLA_DK = GLA_KEY_DIM // GLA_HEADS
GLA_DV = GLA_VAL_DIM // GLA_HEADS
GLA_GATE_RANK = 16
GLA_GATE_NORM = 16.0
GLA_CHUNK = 64
N_EXPERTS = 16
EXPERT_FF = D_MODEL
CAPACITY_FACTOR = 2
PLE_DIM = 256
EPS = 1e-6

LANES = 128
MIB = 1024 * 1024
BF16 = jnp.bfloat16
F32 = jnp.float32
LOG2E = math.log2(math.e)

NT_DIMS = (((1,), (1,)), ((), ()))
TN_DIMS = (((0,), (0,)), ((), ()))


def _cparams(sem, vmem_mib):
    return pltpu.CompilerParams(dimension_semantics=sem, vmem_limit_bytes=vmem_mib * MIB)


def _full(shape):
    n = len(shape)
    return pl.BlockSpec(shape, lambda *_: (0,) * n)


def _rms(x, gain):
    ms = jnp.mean(x * x, axis=-1, keepdims=True)
    return x * lax.rsqrt(ms + EPS) * gain


def _sigmoid(x):
    return 0.5 * jnp.tanh(0.5 * x) + 0.5


SUBLANES = 8
ROW_TILES = D_MODEL // LANES


def _store_token_tiles(ref2d, x, first_row=0):
    rows = x.shape[0]
    for j in range(ROW_TILES):
        ref2d[pl.ds(first_row * ROW_TILES + j, rows, stride=ROW_TILES), :] = x[:, j * LANES:(j + 1) * LANES]


def _load_token_tiles(ref2d, rows, first_row=0):
    return jnp.concatenate([ref2d[pl.ds(first_row * ROW_TILES + j, rows, stride=ROW_TILES), :]
                            for j in range(ROW_TILES)], axis=1)


SUB_ROWS = 256


def _sub_blocks(tile_rows):
    return SUB_ROWS, list(range(0, tile_rows, SUB_ROWS))


def _rope_rows(t, cos_r, sin_r, heads):
    half = ROPE_DIM // 2
    rows = []
    for h in range(heads):
        r0 = h * ATT_HEAD_DIM
        t1, t2 = t[r0:r0 + half], t[r0 + half:r0 + ROPE_DIM]
        rows += [t1 * cos_r - t2 * sin_r, t2 * cos_r + t1 * sin_r, t[r0 + ROPE_DIM:r0 + ATT_HEAD_DIM]]
    return jnp.concatenate(rows, axis=0)


def _in_proj_kernel(x_ref, posr_ref, invfc_ref, gain_ref, wqt_ref, wkz_ref, wvt_ref,
                    wgqk_ref, wgv_ref, wgr_ref, wgate_ref,
                    qt_ref, k0_ref, k1_ref, vt_ref, gq_ref, gk_ref, gv_ref, gr_ref, z_ref, sga_ref, sgg_ref):
    a = _rms(x_ref[...], gain_ref[...]).astype(BF16)
    ang_t = invfc_ref[...] * posr_ref[...].astype(F32)
    cos_r, sin_r = jnp.cos(ang_t), jnp.sin(ang_t)

    qt = lax.dot_general(wqt_ref[...], a, NT_DIMS, preferred_element_type=F32)
    qt_ref[...] = (_rope_rows(qt, cos_r, sin_r, ATT_HEADS) * (ATT_HEAD_DIM ** -0.5 * LOG2E)).astype(BF16)

    kz = jnp.dot(a, wkz_ref[...], preferred_element_type=F32)
    z_ref[...] = kz[:, ATT_KV_DIM:]
    k = _rope_rows(kz[:, :ATT_KV_DIM].T, cos_r, sin_r, ATT_KV_HEADS).T.astype(BF16)
    k0_ref[...] = k[:, :ATT_HEAD_DIM]
    k1_ref[...] = k[:, ATT_HEAD_DIM:]
    vt_ref[...] = lax.dot_general(wvt_ref[...], a, NT_DIMS, preferred_element_type=F32).astype(BF16)

    gqk = jnp.dot(a, wgqk_ref[...], preferred_element_type=F32)
    gq_ref[...] = (gqk[:, :GLA_KEY_DIM] * (GLA_DK ** -0.5)).astype(BF16)
    gk_ref[...] = gqk[:, GLA_KEY_DIM:].astype(BF16)
    gv_ref[...] = jnp.dot(a, wgv_ref[...], preferred_element_type=F32).astype(BF16)
    gr = jnp.dot(a, wgr_ref[...], preferred_element_type=F32)
    gr_ref[...] = (gr * _sigmoid(gr)).astype(BF16)
    gates = jnp.dot(a, wgate_ref[...], preferred_element_type=F32)
    sga_ref[...] = _sigmoid(gates[:, :D_MODEL]).astype(BF16)
    sgg_ref[...] = _sigmoid(gates[:, D_MODEL:]).astype(BF16)


def _in_proj(x2, posr, invfc, gain, wqt, wkz, wvt, wgqk, wgv, wgr, wgate, tm):
    T = x2.shape[0]
    row = lambda n: pl.BlockSpec((tm, n), lambda i: (i, 0))
    col = lambda n: pl.BlockSpec((n, tm), lambda i: (0, i))
    row_widths = (ATT_HEAD_DIM, ATT_HEAD_DIM, None, GLA_KEY_DIM, GLA_KEY_DIM, GLA_VAL_DIM,
                  GLA_VAL_DIM, wkz.shape[1] - ATT_KV_DIM, D_MODEL, D_MODEL)
    row_dtypes = (BF16,) * 7 + (F32, BF16, BF16)
    out_specs = [col(ATT_Q_DIM)]
    out_shape = [jax.ShapeDtypeStruct((ATT_Q_DIM, T), BF16)]
    for n, dt in zip(row_widths, row_dtypes):
        if n is None:
            out_specs.append(col(ATT_KV_DIM))
            out_shape.append(jax.ShapeDtypeStruct((ATT_KV_DIM, T), BF16))
        else:
            out_specs.append(row(n))
            out_shape.append(jax.ShapeDtypeStruct((T, n), dt))
    consts = (invfc, gain, wqt, wkz, wvt, wgqk, wgv, wgr, wgate)
    return pl.pallas_call(
        _in_proj_kernel,
        grid=(T // tm,),
        in_specs=[row(D_MODEL), col(1)] + [_full(c.shape) for c in consts],
        out_specs=out_specs,
        out_shape=out_shape,
        compiler_params=_cparams(("parallel",), 56),
        name="in_proj",
    )(x2, posr, *consts)


def _swa_kernel(sink_ref, qt_ref, k0_ref, k1_ref, vt_ref, o_ref, *, tq, seq):
    blk = WINDOW
    span = 3 * blk
    hd = ATT_HEAD_DIM
    n = pl.program_id(1)
    ones = jnp.ones((16, span), BF16)
    kv_refs = (k0_ref, k1_ref)

    def window_start(sb):
        return pl.multiple_of(jnp.clip(n * tq + (sb - 1) * blk, 0, seq - span), blk)

    def scores(sb, g):
        kw = kv_refs[g][0, pl.ds(window_start(sb), span), :]
        heads = range(g * ATT_GROUP, (g + 1) * ATT_GROUP)
        qs = jnp.concatenate([qt_ref[h * hd:(h + 1) * hd, sb * blk:(sb + 1) * blk] for h in heads], axis=1)
        return jnp.dot(kw, qs, preferred_element_type=F32)

    work = [(sb, g) for sb in range(tq // blk) for g in range(ATT_KV_HEADS)]
    s_next = scores(*work[0])
    outs = []
    for step, (sb, g) in enumerate(work):
        s_all = s_next
        if step + 1 < len(work):
            s_next = scores(*work[step + 1])
        q0 = n * tq + sb * blk
        start = window_start(sb)
        kj = start + lax.broadcasted_iota(jnp.int32, (span, blk), 0)
        qi = q0 + lax.broadcasted_iota(jnp.int32, (span, blk), 1)
        valid = jnp.abs(qi - kj) <= WINDOW
        vaug = jnp.concatenate([vt_ref[g * hd:(g + 1) * hd, pl.ds(start, span)], ones], axis=0)
        sinks = [sink_ref[g * ATT_GROUP + i] * LOG2E for i in range(ATT_GROUP)]
        ss = [jnp.where(valid, s_all[:, i * blk:(i + 1) * blk], -jnp.inf) for i in range(ATT_GROUP)]
        ms = [jnp.maximum(jnp.max(s, axis=0, keepdims=True), sink) for s, sink in zip(ss, sinks)]
        rs = [jnp.dot(vaug, jnp.exp2(s - m).astype(BF16), preferred_element_type=F32) for s, m in zip(ss, ms)]
        outs += [r[:hd] / (r[hd:hd + 1] + jnp.exp2(sink - m)) for r, m, sink in zip(rs, ms, sinks)]
        if g == ATT_KV_HEADS - 1:
            for pr in range(ATT_HEADS // 2):
                pair = jnp.concatenate([outs[2 * pr], outs[2 * pr + 1]], axis=0)
                o_ref[0, sb * blk:(sb + 1) * blk, pr * 2 * hd:(pr + 1) * 2 * hd] = pair.T.astype(BF16)
            outs = []


def _swa(sink, qt, k0, k1, vt, batch, tq):
    S = k0.shape[1]
    nq = S // tq
    kspec = pl.BlockSpec((1, S, ATT_HEAD_DIM), lambda b, n: (b, 0, 0))
    return pl.pallas_call(
        functools.partial(_swa_kernel, tq=tq, seq=S),
        grid=(batch, nq),
        in_specs=[pl.BlockSpec(memory_space=pltpu.SMEM),
                  pl.BlockSpec((ATT_Q_DIM, tq), lambda b, n: (0, b * nq + n)),
                  kspec, kspec,
                  pl.BlockSpec((ATT_KV_DIM, S), lambda b, n: (0, b))],
        out_specs=pl.BlockSpec((1, tq, ATT_Q_DIM), lambda b, n: (b, n, 0)),
        out_shape=jax.ShapeDtypeStruct((batch, S, ATT_Q_DIM), BF16),
        compiler_params=_cparams(("parallel", "parallel"), 32),
        name="swa",
    )(sink, qt, k0, k1, vt)


def _log2_sigmoid(u):
    return jnp.minimum(u, 0.0) * LOG2E - jnp.log2(1.0 + jnp.exp2(jnp.abs(u) * -LOG2E))


def _split2(x):
    hi = x.astype(BF16)
    return hi, (x - hi.astype(F32)).astype(BF16)


GLA_WAYS = 4


def _gla_kernel(z_ref, q_ref, k_ref, v_ref, r_ref, upw_ref, bias_ref, gain_ref,
                o_ref, cf_ref, cb_ref, kef_ref, keb_ref, st_ref, s_ref, *, seq):
    L = GLA_CHUNK
    R2 = 2 * GLA_GATE_RANK
    nc = seq // L
    grp = 4 * L
    cpg = grp // L
    dk = GLA_DK
    mm = functools.partial(jnp.dot, preferred_element_type=F32)
    nt = functools.partial(lax.dot_general, dimension_numbers=NT_DIMS, preferred_element_type=F32)

    row = lax.broadcasted_iota(jnp.int32, (grp, grp), 0)
    col = lax.broadcasted_iota(jnp.int32, (grp, grp), 1)
    same = (row // L) == (col // L)
    fwd_mask = same & (col <= row)
    bwd_mask = same & (col > row)
    tri_lo = jnp.where(fwd_mask, 1.0, 0.0).astype(BF16)
    tri_up = jnp.where(same & (col >= row), 1.0, 0.0).astype(BF16)
    lane = lax.broadcasted_iota(jnp.int32, (1, LANES), 1)
    use_lo = (lane >= R2) & (lane < 2 * R2)

    def group_starts(i):
        return [pl.multiple_of((i * GLA_WAYS + w) * grp, grp) for w in range(GLA_WAYS)]

    def cum_body(i, carry):
        r0s = group_starts(i)
        zs = [_split2(z_ref[0, pl.ds(r0, grp), :]) for r0 in r0s]
        us = [mm(jnp.where(use_lo, zl, zh), upw_ref[...]) + bias_ref[...] for zh, zl in zs]
        las = [_split2(_log2_sigmoid(u) * (1.0 / GLA_GATE_NORM)) for u in us]
        cfxs = [mm(tri_lo, jnp.concatenate([lh[:, :dk], ll[:, :dk]], axis=1)) for lh, ll in las]
        cbxs = [mm(tri_up, jnp.concatenate([lh[:, dk:], ll[:, dk:]], axis=1)) for lh, ll in las]
        for r0, cfx, cbx in zip(r0s, cfxs, cbxs):
            cf = cfx[:, :dk] + cfx[:, dk:]
            cb = cbx[:, :dk] + cbx[:, dk:]
            cf_ref[pl.ds(r0, grp), :] = cf
            cb_ref[pl.ds(r0, grp), :] = cb
            k = k_ref[0, pl.ds(r0, grp), :].astype(F32)
            for c in range(cpg):
                sl = slice(c * L, (c + 1) * L)
                gf = cf[(c + 1) * L - 1:(c + 1) * L]
                gb = cb[c * L:c * L + 1]
                kef_ref[pl.ds(r0 + c * L, L), :] = (k[sl] * jnp.exp2(gf - cf[sl])).astype(BF16)
                keb_ref[pl.ds(r0 + c * L, L), :] = (k[sl] * jnp.exp2(gb - cb[sl])).astype(BF16)
        return carry

    lax.fori_loop(0, seq // (grp * GLA_WAYS), cum_body, 0)

    s_ref[...] = jnp.zeros_like(s_ref)
    zero_k = jnp.zeros((L, dk), BF16)

    def state_body(i, carry):
        j = nc - 1 - i
        rf = pl.multiple_of(i * L, L)
        rb = pl.multiple_of(j * L, L)
        vcat = jnp.concatenate([v_ref[0, pl.ds(rf, L), :], v_ref[0, pl.ds(rb, L), :]], axis=0)
        kblk = jnp.concatenate([jnp.concatenate([kef_ref[pl.ds(rf, L), :], zero_k], axis=1),
                                jnp.concatenate([zero_k, keb_ref[pl.ds(rb, L), :]], axis=1)], axis=0)
        kv = lax.dot_general(vcat, kblk, TN_DIMS, preferred_element_type=F32)
        decay = jnp.exp2(jnp.concatenate([cf_ref[pl.ds(rf + L - 1, 1), :], cb_ref[pl.ds(rb, 1), :]], axis=1))
        s = s_ref[...]
        st_ref[i, :, 0:dk] = s[:, :dk].astype(BF16)
        st_ref[j, :, dk:2 * dk] = s[:, dk:].astype(BF16)
        s_ref[...] = s * decay + kv
        return carry

    lax.fori_loop(0, nc, state_body, 0, unroll=16)

    def out_body(i, carry):
        r0s = group_starts(i)
        ops = []
        for r0 in r0s:
            q = q_ref[0, pl.ds(r0, grp), :].astype(F32)
            k = k_ref[0, pl.ds(r0, grp), :].astype(F32)
            cf = cf_ref[pl.ds(r0, grp), :]
            cb = cb_ref[pl.ds(r0, grp), :]
            ops.append(((q * jnp.exp2(cf)).astype(BF16), (k * jnp.exp2(-cf)).astype(BF16),
                        (q * jnp.exp2(cb)).astype(BF16), (k * jnp.exp2(-cb)).astype(BF16)))
        scores = [(nt(qf, kf), nt(qb, kb)) for qf, kf, qb, kb in ops]
        attns = [jnp.where(fwd_mask, af, jnp.where(bwd_mask, ab, 0.0)).astype(BF16) for af, ab in scores]
        outs = []
        for r0, attn, (qf, _, qb, _) in zip(r0s, attns, ops):
            c0 = r0 // L
            qcat = jnp.concatenate([qf, qb], axis=1)
            inter = jnp.concatenate([nt(qcat[c * L:(c + 1) * L], st_ref[c0 + c]) for c in range(cpg)], axis=0)
            outs.append(mm(attn, v_ref[0, pl.ds(r0, grp), :]) + inter)
        for r0, o in zip(r0s, outs):
            o_ref[0, pl.ds(r0, grp), :] = (_rms(o, gain_ref[...]) * r_ref[0, pl.ds(r0, grp), :].astype(F32)).astype(BF16)
        return carry

    lax.fori_loop(0, seq // (grp * GLA_WAYS), out_body, 0)


def _gla(z, gq, gk, gv, gr, upw, bias, gain):
    B, S, _ = gq.shape
    nc = S // GLA_CHUNK
    seq_blk = lambda n: pl.BlockSpec((1, S, n), lambda b, h: (b, 0, h))
    head_blk = lambda r, n: pl.BlockSpec((None, r, n), lambda b, h: (h, 0, 0))
    return pl.pallas_call(
        functools.partial(_gla_kernel, seq=S),
        grid=(B, GLA_HEADS),
        in_specs=[pl.BlockSpec((1, S, LANES), lambda b, h: (b, 0, 0)),
                  seq_blk(GLA_DK), seq_blk(GLA_DK), seq_blk(GLA_DV), seq_blk(GLA_DV),
                  head_blk(LANES, 2 * GLA_DK), head_blk(1, 2 * GLA_DK),
                  pl.BlockSpec((1, GLA_DV), lambda b, h: (0, h))],
        out_specs=seq_blk(GLA_DV),
        out_shape=jax.ShapeDtypeStruct((B, S, GLA_VAL_DIM), BF16),
        scratch_shapes=[pltpu.VMEM((S, GLA_DK), F32), pltpu.VMEM((S, GLA_DK), F32),
                        pltpu.VMEM((S, GLA_DK), BF16), pltpu.VMEM((S, GLA_DK), BF16),
                        pltpu.VMEM((nc, GLA_DV, 2 * GLA_DK), BF16),
                        pltpu.VMEM((GLA_DV, 2 * GLA_DK), F32)],
        compiler_params=_cparams(("parallel", "parallel"), 48),
        name="gla",
    )(z, gq, gk, gv, gr, upw, bias, gain)


def _mix_out_kernel(x_ref, a_ref, g_ref, sga_ref, sgg_ref, wa_ref, wb_ref, wo_ref, gain_ref, wr_ref,
                    h_ref, xn_ref, aff_ref, afft_ref):
    sub, starts = _sub_blocks(x_ref.shape[0])
    mm = functools.partial(jnp.dot, preferred_element_type=F32)
    blk = lambda ref, r0: ref[r0:r0 + sub, :]
    y_att = [mm(blk(a_ref, r0), wa_ref[...]) for r0 in starts]
    y_gla = [mm(blk(g_ref, r0), wb_ref[...]) for r0 in starts]
    merged = [(blk(sga_ref, r0).astype(F32) * ya + blk(sgg_ref, r0).astype(F32) * yg).astype(BF16)
              for r0, ya, yg in zip(starts, y_att, y_gla)]
    hs = [blk(x_ref, r0) + mm(m, wo_ref[...]) for r0, m in zip(starts, merged)]
    xns = [_rms(h, gain_ref[...]) for h in hs]
    for r0, h, xn in zip(starts, hs, xns):
        h_ref[r0:r0 + sub, :] = h
        _store_token_tiles(xn_ref, xn, r0)
    his = [xn.astype(BF16) for xn in xns]
    parts = [lax.dot_general(wr_ref[...], jnp.concatenate([hi, (xn - hi.astype(F32)).astype(BF16)], axis=1),
                             NT_DIMS, preferred_element_type=F32) for xn, hi in zip(xns, his)]
    pad = jnp.zeros((LANES - N_EXPERTS, sub), F32)
    for r0, part in zip(starts, parts):
        logits = part[:N_EXPERTS] + part[N_EXPERTS:]
        e = jnp.exp(logits - jnp.max(logits, axis=0, keepdims=True))
        aff_t = e / jnp.sum(e, axis=0, keepdims=True)
        afft_ref[:, r0:r0 + sub] = aff_t
        aff_ref[r0:r0 + sub, :] = jnp.concatenate([aff_t, pad], axis=0).T[:, :N_EXPERTS]


def _mix_out(x2, a, g, sga, sgg, wa, wb, wo, gain, wr, tm):
    T = x2.shape[0]
    row = lambda n: pl.BlockSpec((tm, n), lambda i: (i, 0))
    return pl.pallas_call(
        _mix_out_kernel,
        grid=(T // tm,),
        in_specs=[row(D_MODEL), row(ATT_Q_DIM), row(GLA_VAL_DIM), row(D_MODEL), row(D_MODEL),
                  _full(wa.shape), _full(wb.shape), _full(wo.shape), _full(gain.shape), _full(wr.shape)],
        out_specs=[row(D_MODEL), pl.BlockSpec((tm * ROW_TILES, LANES), lambda i: (i, 0)), row(N_EXPERTS),
                   pl.BlockSpec((N_EXPERTS, tm), lambda i: (0, i))],
        out_shape=[jax.ShapeDtypeStruct((T, D_MODEL), F32), jax.ShapeDtypeStruct((T * ROW_TILES, LANES), F32),
                   jax.ShapeDtypeStruct((T, N_EXPERTS), F32), jax.ShapeDtypeStruct((N_EXPERTS, T), F32)],
        compiler_params=_cparams(("parallel",), 48),
        name="mix_out",
    )(x2, a, g, sga, sgg, wa, wb, wo, gain, wr)


ROUTE_WAYS = 4


def _route_kernel(aff_ref, idx_ref, cum_ref, *, cap, seq):
    E = N_EXPERTS
    aff = aff_ref[...]
    count = lambda mask: jnp.sum(mask.astype(jnp.int32), axis=1, keepdims=True)
    as_float = lambda pattern: lax.bitcast_convert_type(pattern, F32)

    def thr_body(t, pattern):
        cand = pattern | jnp.left_shift(jnp.int32(1), 30 - t)
        return jnp.where(count(aff >= as_float(cand)) >= cap, cand, pattern)

    thr = as_float(lax.fori_loop(0, 31, thr_body, jnp.zeros((E, 1), jnp.int32)))
    above = aff > thr
    tie = aff == thr
    need = cap - count(above)

    pos = lax.broadcasted_iota(jnp.int32, (E, seq), 1)

    def tie_body(t, last):
        cand = last | jnp.left_shift(jnp.int32(1), (seq.bit_length() - 2) - t)
        return jnp.where(count(tie & (pos < cand)) < need, cand, last)

    last = lax.fori_loop(0, seq.bit_length() - 1, tie_body, jnp.zeros((E, 1), jnp.int32))
    sel = (above | (tie & (pos <= last))).astype(BF16)

    nt = seq // LANES
    lrow = lax.broadcasted_iota(jnp.int32, (LANES, LANES), 0)
    lcol = lax.broadcasted_iota(jnp.int32, (LANES, LANES), 1)
    tri = (lrow <= lcol).astype(BF16)
    mm = functools.partial(jnp.dot, preferred_element_type=F32)
    for t in range(nt):
        cum_ref[t * E:(t + 1) * E, :] = mm(sel[:, t * LANES:(t + 1) * LANES], tri)
    tile_of = (lax.broadcasted_iota(jnp.int32, (seq, LANES), 0) // LANES
               == lax.broadcasted_iota(jnp.int32, (seq, LANES), 1)).astype(BF16)
    per_tile = mm(sel, tile_of)
    lane = lax.broadcasted_iota(jnp.int32, (1, LANES), 1)
    far = jnp.float32(2 * seq)
    t_end = jnp.where(lane < nt, mm(per_tile.astype(BF16), tri), far)
    t_start = jnp.where(lane < nt, t_end - per_tile, far)
    pad = jnp.zeros((LANES - E, LANES), F32)
    t_start_cols = jnp.concatenate([jnp.where(lane < nt, t_start, 0.0), pad], axis=0).T

    slot = lax.broadcasted_iota(jnp.int32, (cap, LANES), 0).astype(F32)
    ones = jnp.ones((LANES, LANES), BF16)
    zrows = jnp.zeros((LANES - nt, 2 * LANES), F32)
    for e0 in range(0, E, ROUTE_WAYS):
        es = range(e0, e0 + ROUTE_WAYS)
        tiles, picks, whole = [], [], []
        for e in es:
            absc = cum_ref[pl.ds(e, nt, stride=E), :] + t_start_cols[0:nt, e:e + 1]
            hi = jnp.where(absc >= 256.0, 1.0, 0.0) + jnp.where(absc >= 512.0, 1.0, 0.0)
            lo = absc - 256.0 * hi
            tiles.append(jnp.concatenate([jnp.concatenate([lo, hi], axis=1), zrows], axis=0).astype(BF16))
            done = jnp.where(t_end[e:e + 1] <= slot, 1.0, 0.0)
            whole.append(done)
            picks.append((jnp.where(t_start[e:e + 1] <= slot, 1.0, 0.0) - done).astype(BF16))
        rows = [mm(p, w) for p, w in zip(picks, tiles)]
        votes = [(jnp.where(r[:, :LANES] + 256.0 * r[:, LANES:] <= slot, 1.0, 0.0) + float(LANES) * d).astype(BF16)
                 for r, d in zip(rows, whole)]
        for e, v in zip(es, votes):
            idx_ref[0, :, e:e + 1] = mm(v, ones)[:, e:e + 1].astype(jnp.int32)


def _route(aff_t, batch, cap):
    E, T = aff_t.shape
    B, S = batch, T // batch
    return pl.pallas_call(
        functools.partial(_route_kernel, cap=cap, seq=S),
        grid=(B,),
        in_specs=[pl.BlockSpec((E, S), lambda b: (0, b))],
        out_specs=pl.BlockSpec((1, cap, E), lambda b: (b, 0, 0)),
        out_shape=jax.ShapeDtypeStruct((B, cap, E), jnp.int32),
        scratch_shapes=[pltpu.VMEM((S // LANES * E, LANES), F32)],
        compiler_params=_cparams(("parallel",), 32),
        name="route",
    )(aff_t)


def _gather_kernel(idx_ref, xn_ref, aff_ref, xg_ref, wg_ref, ws_ref, *, cap):
    e = pl.program_id(1)

    for i in range(cap):
        t = idx_ref[0, 0, i]
        src = pl.multiple_of(t * SUBLANES, SUBLANES)
        xg_ref[0, 0, i * SUBLANES:(i + 1) * SUBLANES, :] = xn_ref[0, pl.ds(src, SUBLANES), :]
        ws_ref[i:i + 1, :] = aff_ref[0, pl.ds(t, 1), :]
    lane = lax.broadcasted_iota(jnp.int32, (cap, N_EXPERTS), 1)
    wg_ref[0, 0] = jnp.sum(jnp.where(lane == e, ws_ref[...], 0.0), axis=1, keepdims=True)


def _gather(idx, xn_tiles, aff, cap):
    B, S, E = aff.shape
    return pl.pallas_call(
        functools.partial(_gather_kernel, cap=cap),
        grid=(B, E),
        in_specs=[pl.BlockSpec((1, 1, cap), lambda b, e: (b * E + e, 0, 0), memory_space=pltpu.SMEM),
                  pl.BlockSpec((1, S * SUBLANES, LANES), lambda b, e: (b, 0, 0)),
                  pl.BlockSpec((1, S, E), lambda b, e: (b, 0, 0))],
        out_specs=[pl.BlockSpec((1, 1, cap * SUBLANES, LANES), lambda b, e: (b, e, 0, 0)),
                   pl.BlockSpec((1, 1, cap, 1), lambda b, e: (b, e, 0, 0))],
        out_shape=[jax.ShapeDtypeStruct((B, E, cap * SUBLANES, LANES), F32),
                   jax.ShapeDtypeStruct((B, E, cap, 1), F32)],
        scratch_shapes=[pltpu.VMEM((cap, E), F32)],
        compiler_params=_cparams(("arbitrary", "arbitrary"), 48),
        name="gather",
    )(idx.reshape(B * E, 1, cap), xn_tiles, aff)


def _ffn_kernel(xg_ref, wg_ref, w1_ref, w2_ref, w3_ref, y_ref, b1_ref, b2_ref, b3_ref):
    @pl.when(pl.program_id(1) == 0)
    def _():
        b1_ref[...] = w1_ref[0].astype(BF16)
        b2_ref[...] = w2_ref[0].astype(BF16)
        b3_ref[...] = w3_ref[0].astype(BF16)

    rows = wg_ref.shape[2]
    seqs = range(xg_ref.shape[0])
    mm = functools.partial(jnp.dot, preferred_element_type=F32)
    xgs = [_load_token_tiles(xg_ref.at[i, 0], rows).astype(BF16) for i in seqs]
    gates = [mm(xg, b1_ref[...]) for xg in xgs]
    ups = [mm(xg, b2_ref[...]) for xg in xgs]
    hids = [(gate * _sigmoid(gate) * up).astype(BF16) for gate, up in zip(gates, ups)]
    ys = [mm(hid, b3_ref[...]) * wg_ref[i, 0] for i, hid in zip(seqs, hids)]
    for i, y in zip(seqs, ys):
        _store_token_tiles(y_ref.at[i, 0], y)


FFN_SEQS_PER_STEP = 2


def _ffn(xg, wg, w1, w2, w3):
    B, E, C, _ = wg.shape
    _, D, F = w1.shape
    n = FFN_SEQS_PER_STEP
    return pl.pallas_call(
        _ffn_kernel,
        grid=(E, B // n),
        in_specs=[pl.BlockSpec((n, 1, C * ROW_TILES, LANES), lambda e, b: (b, e, 0, 0)),
                  pl.BlockSpec((n, 1, C, 1), lambda e, b: (b, e, 0, 0)),
                  pl.BlockSpec((1, D, F), lambda e, b: (e, 0, 0)),
                  pl.BlockSpec((1, D, F), lambda e, b: (e, 0, 0)),
                  pl.BlockSpec((1, F, D), lambda e, b: (e, 0, 0))],
        out_specs=pl.BlockSpec((n, 1, C * ROW_TILES, LANES), lambda e, b: (b, e, 0, 0)),
        out_shape=jax.ShapeDtypeStruct((B, E, C * ROW_TILES, LANES), F32),
        scratch_shapes=[pltpu.VMEM((D, F), BF16), pltpu.VMEM((D, F), BF16), pltpu.VMEM((F, D), BF16)],
        compiler_params=_cparams(("arbitrary", "arbitrary"), 60),
        name="ffn",
    )(xg, wg, w1, w2, w3)


SCATTER_BATCH = 16


def _scatter_kernel(idx_ref, y_ref, o_ref, *, cap):
    @pl.when(pl.program_id(1) == 0)
    def _():
        o_ref[...] = jnp.zeros_like(o_ref)

    for i0 in range(0, cap, SCATTER_BATCH):
        slots = range(i0, i0 + SCATTER_BATCH)
        rows = [pl.ds(pl.multiple_of(idx_ref[0, 0, i] * SUBLANES, SUBLANES), SUBLANES) for i in slots]
        new = [o_ref[0, r, :] + y_ref[0, 0, i * SUBLANES:(i + 1) * SUBLANES, :] for r, i in zip(rows, slots)]
        for r, v in zip(rows, new):
            o_ref[0, r, :] = v


def _scatter(idx, y_tiles, seq):
    B, E, cap = idx.shape
    return pl.pallas_call(
        functools.partial(_scatter_kernel, cap=cap),
        grid=(B, E),
        in_specs=[pl.BlockSpec((1, 1, cap), lambda b, e: (b * E + e, 0, 0), memory_space=pltpu.SMEM),
                  pl.BlockSpec((1, 1, cap * SUBLANES, LANES), lambda b, e: (b, e, 0, 0))],
        out_specs=pl.BlockSpec((1, seq * SUBLANES, LANES), lambda b, e: (b, 0, 0)),
        out_shape=jax.ShapeDtypeStruct((B, seq * SUBLANES, LANES), F32),
        compiler_params=_cparams(("arbitrary", "arbitrary"), 48),
        name="scatter",
    )(idx.reshape(B * E, 1, cap), y_tiles)


def _ple_out_kernel(h_ref, moe_ref, p_ref, gple_ref, wpg_ref, wple_ref, gfin_ref, o_ref):
    sub, starts = _sub_blocks(h_ref.shape[0])
    mm = functools.partial(jnp.dot, preferred_element_type=F32)
    hs = [h_ref[r0:r0 + sub, :] + _load_token_tiles(moe_ref, sub, r0) for r0 in starts]
    ns = [_rms(h, gple_ref[...]).astype(BF16) for h in hs]
    gates = [_sigmoid(mm(n, wpg_ref[...])) for n in ns]
    embs = [mm(p_ref[r0:r0 + sub, :].astype(BF16), wple_ref[...]) for r0 in starts]
    for r0, h, gate, emb in zip(starts, hs, gates, embs):
        o_ref[r0:r0 + sub, :] = _rms(h + gate * emb, gfin_ref[...])


def _ple_out(h, moe, p2, gple, wpg, wple, gfin, tm):
    T = h.shape[0]
    row = lambda n: pl.BlockSpec((tm, n), lambda i: (i, 0))
    return pl.pallas_call(
        _ple_out_kernel,
        grid=(T // tm,),
        in_specs=[row(D_MODEL), pl.BlockSpec((tm * ROW_TILES, LANES), lambda i: (i, 0)), row(PLE_DIM),
                  _full(gple.shape), _full(wpg.shape), _full(wple.shape), _full(gfin.shape)],
        out_specs=row(D_MODEL),
        out_shape=jax.ShapeDtypeStruct((T, D_MODEL), F32),
        compiler_params=_cparams(("parallel",), 48),
        name="ple_out",
    )(h, moe, p2, gple, wpg, wple, gfin)


def kernel(x, p, positions, norm_mix, w_in, gla_gate_up_fwd, gla_gate_bias_fwd, gla_gate_up_bwd, gla_gate_bias_bwd, attn_sink, gla_norm, w_branch_attn, w_branch_gla, w_out, norm_ffn, w_router, w_exp_gate, w_exp_up, w_exp_down, norm_ple, w_ple_gate, w_ple, norm_final):
    B, S, D = x.shape
    T = B * S
    depth = w_in.shape[0]
    cap = CAPACITY_FACTOR * S // N_EXPERTS
    R = GLA_GATE_RANK

    posr = positions.reshape(1, T)
    inv_freq = ROPE_THETA ** (-jnp.arange(0, ROPE_DIM, 2, dtype=F32) / ROPE_DIM)
    invfc = inv_freq.reshape(ROPE_DIM // 2, 1)

    h = x.reshape(T, D)
    for l in range(depth):
        o = 0
        cols = {}
        for name, n in (("q", ATT_Q_DIM), ("k", ATT_KV_DIM), ("v", ATT_KV_DIM), ("gqk", 2 * GLA_KEY_DIM),
                        ("gv", GLA_VAL_DIM), ("gr", GLA_VAL_DIM), ("z", 2 * R), ("gate", 2 * D_MODEL)):
            cols[name] = w_in[l][:, o:o + n].astype(BF16)
            o += n
        per_head = lambda w: w.reshape(-1, GLA_HEADS, GLA_DK).swapaxes(0, 1)
        upf, upb = per_head(gla_gate_up_fwd[l]), per_head(gla_gate_up_bwd[l])
        up = jnp.concatenate([jnp.concatenate([upf, jnp.zeros_like(upf)], axis=2),
                              jnp.concatenate([jnp.zeros_like(upb), upb], axis=2)], axis=1)
        up_hi = up.astype(BF16)
        up_lo = (up - up_hi.astype(F32)).astype(BF16)
        upw = jnp.concatenate([up_hi, up_hi, up_lo, jnp.zeros_like(up_lo)], axis=1)
        wz4 = jnp.tile(cols["z"], (1, LANES // (2 * R)))
        gbias = jnp.concatenate([per_head(gla_gate_bias_fwd[l]), per_head(gla_gate_bias_bwd[l])], axis=2)
        wr = w_router[l]
        wr_hi = wr.astype(BF16)
        wr_lo = (wr - wr_hi.astype(F32)).astype(BF16)
        wr2 = jnp.concatenate([jnp.concatenate([wr_hi, wr_lo], axis=1),
                               jnp.concatenate([wr_hi, jnp.zeros_like(wr_lo)], axis=1)], axis=0).T

        qt, k0, k1, vt, gq, gk, gv, gr, z, sga, sgg = _in_proj(
            h, posr, invfc, norm_mix[l].reshape(1, D), cols["q"].T,
            jnp.concatenate([cols["k"], wz4], axis=1), cols["v"].T,
            cols["gqk"], cols["gv"], cols["gr"], cols["gate"], tm=512)

        att = _swa(attn_sink[l], qt, k0.reshape(B, S, -1), k1.reshape(B, S, -1), vt, batch=B, tq=512)
        gla = _gla(z.reshape(B, S, -1), gq.reshape(B, S, -1), gk.reshape(B, S, -1), gv.reshape(B, S, -1),
                   gr.reshape(B, S, -1), upw, gbias, gla_norm[l].reshape(1, -1))

        h1, xn, aff, aff_t = _mix_out(h, att.reshape(T, -1), gla.reshape(T, -1), sga, sgg,
                               w_branch_attn[l].astype(BF16), w_branch_gla[l].astype(BF16),
                               w_out[l].astype(BF16), norm_ffn[l].reshape(1, D), wr2, tm=512)

        aff3 = aff.reshape(B, S, N_EXPERTS)
        idx = _route(aff_t, B, cap)
        idx = jnp.swapaxes(idx, 1, 2)
        xg, wg = _gather(idx, xn.reshape(B, S * SUBLANES, LANES), aff3, cap)
        y = _ffn(xg, wg, w_exp_gate[l], w_exp_up[l], w_exp_down[l])
        moe = _scatter(idx, y, S)

        last = l == depth - 1
        gfin = norm_final.reshape(1, D)
        assert last, "the final norm is fused into the last layer's PLE kernel"
        h = _ple_out(h1, moe.reshape(T * ROW_TILES, LANES), p[l].reshape(T, PLE_DIM), norm_ple[l].reshape(1, D),
                     w_ple_gate[l].astype(BF16), w_ple[l].astype(BF16), gfin, tm=1024)
    return h.reshape(B, S, D)
```

```python
import functools
import math

import jax
import jax.numpy as jnp
from jax import lax
from jax.experimental import pallas as pl
from jax.experimental.pallas import tpu as pltpu

D_MODEL = 1024
ATT_HEADS = 8
ATT_KV_HEADS = 2
ATT_HEAD_DIM = 64
ATT_GROUP = ATT_HEADS // ATT_KV_HEADS
ATT_Q_DIM = ATT_HEADS * ATT_HEAD_DIM
ATT_KV_DIM = ATT_KV_HEADS * ATT_HEAD_DIM
WINDOW = 128
ROPE_DIM = ATT_HEAD_DIM // 4
ROPE_THETA = 500000.0
GLA_HEADS = 4
GLA_KEY_DIM = D_MODEL // 2
GLA_VAL_DIM = D_MODEL
GLA_DK = GLA_KEY_DIM // GLA_HEADS
GLA_DV = GLA_VAL_DIM // GLA_HEADS
GLA_GATE_RANK = 16
GLA_GATE_NORM = 16.0
GLA_CHUNK = 64
N_EXPERTS = 16
EXPERT_FF = D_MODEL
CAPACITY_FACTOR = 2
PLE_DIM = 256
EPS = 1e-6

LANES = 128
MIB = 1024 * 1024
BF16 = jnp.bfloat16
F32 = jnp.float32
LOG2E = math.log2(math.e)

NT_DIMS = (((1,), (1,)), ((), ()))
TN_DIMS = (((0,), (0,)), ((), ()))


def _cparams(sem, vmem_mib):
    return pltpu.CompilerParams(dimension_semantics=sem, vmem_limit_bytes=vmem_mib * MIB)


def _full(shape):
    n = len(shape)
    return pl.BlockSpec(shape, lambda *_: (0,) * n)


def _rms(x, gain):
    ms = jnp.mean(x * x, axis=-1, keepdims=True)
    return x * lax.rsqrt(ms + EPS) * gain


def _sigmoid(x):
    return 0.5 * jnp.tanh(0.5 * x) + 0.5


SUBLANES = 8
ROW_TILES = D_MODEL // LANES


def _store_token_tiles(ref2d, x, first_row=0):
    rows = x.shape[0]
    for j in range(ROW_TILES):
        ref2d[pl.ds(first_row * ROW_TILES + j, rows, stride=ROW_TILES), :] = x[:, j * LANES:(j + 1) * LANES]


def _load_token_tiles(ref2d, rows, first_row=0):
    return jnp.concatenate([ref2d[pl.ds(first_row * ROW_TILES + j, rows, stride=ROW_TILES), :]
                            for j in range(ROW_TILES)], axis=1)


SUB_ROWS = 256


def _sub_blocks(tile_rows):
    return SUB_ROWS, list(range(0, tile_rows, SUB_ROWS))


def _rope_rows(t, cos_r, sin_r, heads):
    half = ROPE_DIM // 2
    rows = []
    for h in range(heads):
        r0 = h * ATT_HEAD_DIM
        t1, t2 = t[r0:r0 + half], t[r0 + half:r0 + ROPE_DIM]
        rows += [t1 * cos_r - t2 * sin_r, t2 * cos_r + t1 * sin_r, t[r0 + ROPE_DIM:r0 + ATT_HEAD_DIM]]
    return jnp.concatenate(rows, axis=0)


def _in_proj_kernel(x_ref, posr_ref, invfc_ref, gain_ref, wt_ref, w_ref,
                    qt_ref, k0_ref, k1_ref, vt_ref, gq_ref, gk_ref, gv_ref, gr_ref, z_ref, sga_ref, sgg_ref):
    wqt_ref = wt_ref.at[0:ATT_Q_DIM]
    wvt_ref = wt_ref.at[ATT_Q_DIM:ATT_Q_DIM + ATT_KV_DIM]
    edges = [0, ATT_KV_DIM + LANES]
    for width in (2 * GLA_KEY_DIM, GLA_VAL_DIM, GLA_VAL_DIM, 2 * D_MODEL):
        edges.append(edges[-1] + width)
    wkz_ref, wgqk_ref, wgv_ref, wgr_ref, wgate_ref = (w_ref.at[:, a0:a1] for a0, a1 in zip(edges[:-1], edges[1:]))
    a = _rms(x_ref[...], gain_ref[...]).astype(BF16)
    ang_t = invfc_ref[...] * posr_ref[...].astype(F32)
    cos_r, sin_r = jnp.cos(ang_t), jnp.sin(ang_t)

    qt = lax.dot_general(wqt_ref[...], a, NT_DIMS, preferred_element_type=F32)
    qt_ref[...] = (_rope_rows(qt, cos_r, sin_r, ATT_HEADS) * (ATT_HEAD_DIM ** -0.5 * LOG2E)).astype(BF16)

    kz = jnp.dot(a, wkz_ref[...], preferred_element_type=F32)
    z_ref[...] = kz[:, ATT_KV_DIM:]
    k = _rope_rows(kz[:, :ATT_KV_DIM].T, cos_r, sin_r, ATT_KV_HEADS).T.astype(BF16)
    k0_ref[...] = k[:, :ATT_HEAD_DIM]
    k1_ref[...] = k[:, ATT_HEAD_DIM:]
    vt_ref[...] = lax.dot_general(wvt_ref[...], a, NT_DIMS, preferred_element_type=F32).astype(BF16)

    gqk = jnp.dot(a, wgqk_ref[...], preferred_element_type=F32)
    gq_ref[...] = (gqk[:, :GLA_KEY_DIM] * (GLA_DK ** -0.5)).astype(BF16)
    gk_ref[...] = gqk[:, GLA_KEY_DIM:].astype(BF16)
    gv_ref[...] = jnp.dot(a, wgv_ref[...], preferred_element_type=F32).astype(BF16)
    gr = jnp.dot(a, wgr_ref[...], preferred_element_type=F32)
    gr_ref[...] = (gr * _sigmoid(gr)).astype(BF16)
    gates = jnp.dot(a, wgate_ref[...], preferred_element_type=F32)
    sga_ref[...] = _sigmoid(gates[:, :D_MODEL]).astype(BF16)
    sgg_ref[...] = _sigmoid(gates[:, D_MODEL:]).astype(BF16)


def _in_proj(x2, posr, invfc, gain, wt, w, tm):
    T = x2.shape[0]
    row = lambda n: pl.BlockSpec((tm, n), lambda i: (i, 0))
    col = lambda n: pl.BlockSpec((n, tm), lambda i: (0, i))
    row_widths = (ATT_HEAD_DIM, ATT_HEAD_DIM, None, GLA_KEY_DIM, GLA_KEY_DIM, GLA_VAL_DIM,
                  GLA_VAL_DIM, LANES, D_MODEL, D_MODEL)
    row_dtypes = (BF16,) * 7 + (F32, BF16, BF16)
    out_specs = [col(ATT_Q_DIM)]
    out_shape = [jax.ShapeDtypeStruct((ATT_Q_DIM, T), BF16)]
    for n, dt in zip(row_widths, row_dtypes):
        if n is None:
            out_specs.append(col(ATT_KV_DIM))
            out_shape.append(jax.ShapeDtypeStruct((ATT_KV_DIM, T), BF16))
        else:
            out_specs.append(row(n))
            out_shape.append(jax.ShapeDtypeStruct((T, n), dt))
    consts = (invfc, gain, wt, w)
    return pl.pallas_call(
        _in_proj_kernel,
        grid=(T // tm,),
        in_specs=[row(D_MODEL), col(1)] + [_full(c.shape) for c in consts],
        out_specs=out_specs,
        out_shape=out_shape,
        compiler_params=_cparams(("parallel",), 56),
        name="in_proj",
    )(x2, posr, *consts)


def _swa_kernel(sink_ref, qt_ref, k0_ref, k1_ref, vt_ref, o_ref, *, tq, seq):
    blk = WINDOW
    span = 3 * blk
    hd = ATT_HEAD_DIM
    n = pl.program_id(1)
    ones = jnp.ones((16, span), BF16)
    kv_refs = (k0_ref, k1_ref)

    def window_start(sb):
        return pl.multiple_of(jnp.clip(n * tq + (sb - 1) * blk, 0, seq - span), blk)

    def scores(sb, g):
        kw = kv_refs[g][0, pl.ds(window_start(sb), span), :]
        heads = range(g * ATT_GROUP, (g + 1) * ATT_GROUP)
        qs = jnp.concatenate([qt_ref[h * hd:(h + 1) * hd, sb * blk:(sb + 1) * blk] for h in heads], axis=1)
        return jnp.dot(kw, qs, preferred_element_type=F32)

    work = [(sb, g) for sb in range(tq // blk) for g in range(ATT_KV_HEADS)]
    s_next = scores(*work[0])
    outs = []
    for step, (sb, g) in enumerate(work):
        s_all = s_next
        if step + 1 < len(work):
            s_next = scores(*work[step + 1])
        q0 = n * tq + sb * blk
        start = window_start(sb)
        kj = start + lax.broadcasted_iota(jnp.int32, (span, blk), 0)
        qi = q0 + lax.broadcasted_iota(jnp.int32, (span, blk), 1)
        valid = jnp.abs(qi - kj) <= WINDOW
        vaug = jnp.concatenate([vt_ref[g * hd:(g + 1) * hd, pl.ds(start, span)], ones], axis=0)
        sinks = [sink_ref[g * ATT_GROUP + i] * LOG2E for i in range(ATT_GROUP)]
        ss = [jnp.where(valid, s_all[:, i * blk:(i + 1) * blk], -jnp.inf) for i in range(ATT_GROUP)]
        ms = [jnp.maximum(jnp.max(s, axis=0, keepdims=True), sink) for s, sink in zip(ss, sinks)]
        rs = [jnp.dot(vaug, jnp.exp2(s - m).astype(BF16), preferred_element_type=F32) for s, m in zip(ss, ms)]
        outs += [r[:hd] / (r[hd:hd + 1] + jnp.exp2(sink - m)) for r, m, sink in zip(rs, ms, sinks)]
        if g == ATT_KV_HEADS - 1:
            for pr in range(ATT_HEADS // 2):
                pair = jnp.concatenate([outs[2 * pr], outs[2 * pr + 1]], axis=0)
                o_ref[0, sb * blk:(sb + 1) * blk, pr * 2 * hd:(pr + 1) * 2 * hd] = pair.T.astype(BF16)
            outs = []


def _swa(sink, qt, k0, k1, vt, batch, tq):
    S = k0.shape[1]
    nq = S // tq
    kspec = pl.BlockSpec((1, S, ATT_HEAD_DIM), lambda b, n: (b, 0, 0))
    return pl.pallas_call(
        functools.partial(_swa_kernel, tq=tq, seq=S),
        grid=(batch, nq),
        in_specs=[pl.BlockSpec(memory_space=pltpu.SMEM),
                  pl.BlockSpec((ATT_Q_DIM, tq), lambda b, n: (0, b * nq + n)),
                  kspec, kspec,
                  pl.BlockSpec((ATT_KV_DIM, S), lambda b, n: (0, b))],
        out_specs=pl.BlockSpec((1, tq, ATT_Q_DIM), lambda b, n: (b, n, 0)),
        out_shape=jax.ShapeDtypeStruct((batch, S, ATT_Q_DIM), BF16),
        compiler_params=_cparams(("parallel", "parallel"), 32),
        name="swa",
    )(sink, qt, k0, k1, vt)


def _log2_sigmoid(u):
    return jnp.minimum(u, 0.0) * LOG2E - jnp.log2(1.0 + jnp.exp2(jnp.abs(u) * -LOG2E))


def _split2(x):
    hi = x.astype(BF16)
    return hi, (x - hi.astype(F32)).astype(BF16)


GLA_WAYS = 4


def _gla_kernel(z_ref, q_ref, k_ref, v_ref, r_ref, upw_ref, bias_ref, gain_ref,
                o_ref, cf_ref, cb_ref, kef_ref, keb_ref, st_ref, s_ref, *, seq):
    L = GLA_CHUNK
    R2 = 2 * GLA_GATE_RANK
    nc = seq // L
    grp = 4 * L
    cpg = grp // L
    dk = GLA_DK
    mm = functools.partial(jnp.dot, preferred_element_type=F32)
    nt = functools.partial(lax.dot_general, dimension_numbers=NT_DIMS, preferred_element_type=F32)

    row = lax.broadcasted_iota(jnp.int32, (grp, grp), 0)
    col = lax.broadcasted_iota(jnp.int32, (grp, grp), 1)
    same = (row // L) == (col // L)
    fwd_mask = same & (col <= row)
    bwd_mask = same & (col > row)
    tri_lo = jnp.where(fwd_mask, 1.0, 0.0).astype(BF16)
    tri_up = jnp.where(same & (col >= row), 1.0, 0.0).astype(BF16)
    lane = lax.broadcasted_iota(jnp.int32, (1, LANES), 1)
    use_lo = (lane >= R2) & (lane < 2 * R2)

    def group_starts(i):
        return [pl.multiple_of((i * GLA_WAYS + w) * grp, grp) for w in range(GLA_WAYS)]

    def cum_body(i, carry):
        r0s = group_starts(i)
        zs = [_split2(z_ref[0, pl.ds(r0, grp), :]) for r0 in r0s]
        us = [mm(jnp.where(use_lo, zl, zh), upw_ref[...]) + bias_ref[...] for zh, zl in zs]
        las = [_split2(_log2_sigmoid(u) * (1.0 / GLA_GATE_NORM)) for u in us]
        cfxs = [mm(tri_lo, jnp.concatenate([lh[:, :dk], ll[:, :dk]], axis=1)) for lh, ll in las]
        cbxs = [mm(tri_up, jnp.concatenate([lh[:, dk:], ll[:, dk:]], axis=1)) for lh, ll in las]
        for r0, cfx, cbx in zip(r0s, cfxs, cbxs):
            cf = cfx[:, :dk] + cfx[:, dk:]
            cb = cbx[:, :dk] + cbx[:, dk:]
            cf_ref[pl.ds(r0, grp), :] = cf
            cb_ref[pl.ds(r0, grp), :] = cb
            k = k_ref[0, pl.ds(r0, grp), :].astype(F32)
            for c in range(cpg):
                sl = slice(c * L, (c + 1) * L)
                gf = cf[(c + 1) * L - 1:(c + 1) * L]
                gb = cb[c * L:c * L + 1]
                kef_ref[pl.ds(r0 + c * L, L), :] = (k[sl] * jnp.exp2(gf - cf[sl])).astype(BF16)
                keb_ref[pl.ds(r0 + c * L, L), :] = (k[sl] * jnp.exp2(gb - cb[sl])).astype(BF16)
        return carry

    lax.fori_loop(0, seq // (grp * GLA_WAYS), cum_body, 0)

    s_ref[...] = jnp.zeros_like(s_ref)
    zero_k = jnp.zeros((L, dk), BF16)

    def state_body(i, carry):
        j = nc - 1 - i
        rf = pl.multiple_of(i * L, L)
        rb = pl.multiple_of(j * L, L)
        vcat = jnp.concatenate([v_ref[0, pl.ds(rf, L), :], v_ref[0, pl.ds(rb, L), :]], axis=0)
        kblk = jnp.concatenate([jnp.concatenate([kef_ref[pl.ds(rf, L), :], zero_k], axis=1),
                                jnp.concatenate([zero_k, keb_ref[pl.ds(rb, L), :]], axis=1)], axis=0)
        kv = lax.dot_general(vcat, kblk, TN_DIMS, preferred_element_type=F32)
        decay = jnp.exp2(jnp.concatenate([cf_ref[pl.ds(rf + L - 1, 1), :], cb_ref[pl.ds(rb, 1), :]], axis=1))
        s = s_ref[...]
        st_ref[i, :, 0:dk] = s[:, :dk].astype(BF16)
        st_ref[j, :, dk:2 * dk] = s[:, dk:].astype(BF16)
        s_ref[...] = s * decay + kv
        return carry

    lax.fori_loop(0, nc, state_body, 0, unroll=16)

    def out_body(i, carry):
        r0s = group_starts(i)
        ops = []
        for r0 in r0s:
            q = q_ref[0, pl.ds(r0, grp), :].astype(F32)
            k = k_ref[0, pl.ds(r0, grp), :].astype(F32)
            cf = cf_ref[pl.ds(r0, grp), :]
            cb = cb_ref[pl.ds(r0, grp), :]
            ops.append(((q * jnp.exp2(cf)).astype(BF16), (k * jnp.exp2(-cf)).astype(BF16),
                        (q * jnp.exp2(cb)).astype(BF16), (k * jnp.exp2(-cb)).astype(BF16)))
        scores = [(nt(qf, kf), nt(qb, kb)) for qf, kf, qb, kb in ops]
        attns = [jnp.where(fwd_mask, af, jnp.where(bwd_mask, ab, 0.0)).astype(BF16) for af, ab in scores]
        outs = []
        for r0, attn, (qf, _, qb, _) in zip(r0s, attns, ops):
            c0 = r0 // L
            qcat = jnp.concatenate([qf, qb], axis=1)
            inter = jnp.concatenate([nt(qcat[c * L:(c + 1) * L], st_ref[c0 + c]) for c in range(cpg)], axis=0)
            outs.append(mm(attn, v_ref[0, pl.ds(r0, grp), :]) + inter)
        for r0, o in zip(r0s, outs):
            o_ref[0, pl.ds(r0, grp), :] = (_rms(o, gain_ref[...]) * r_ref[0, pl.ds(r0, grp), :].astype(F32)).astype(BF16)
        return carry

    lax.fori_loop(0, seq // (grp * GLA_WAYS), out_body, 0)


def _gla(z, gq, gk, gv, gr, upw, bias, gain):
    B, S, _ = gq.shape
    nc = S // GLA_CHUNK
    seq_blk = lambda n: pl.BlockSpec((1, S, n), lambda b, h: (b, 0, h))
    head_blk = lambda r, n: pl.BlockSpec((None, r, n), lambda b, h: (h, 0, 0))
    return pl.pallas_call(
        functools.partial(_gla_kernel, seq=S),
        grid=(B, GLA_HEADS),
        in_specs=[pl.BlockSpec((1, S, LANES), lambda b, h: (b, 0, 0)),
                  seq_blk(GLA_DK), seq_blk(GLA_DK), seq_blk(GLA_DV), seq_blk(GLA_DV),
                  head_blk(LANES, 2 * GLA_DK), head_blk(1, 2 * GLA_DK),
                  pl.BlockSpec((1, GLA_DV), lambda b, h: (0, h))],
        out_specs=seq_blk(GLA_DV),
        out_shape=jax.ShapeDtypeStruct((B, S, GLA_VAL_DIM), BF16),
        scratch_shapes=[pltpu.VMEM((S, GLA_DK), F32), pltpu.VMEM((S, GLA_DK), F32),
                        pltpu.VMEM((S, GLA_DK), BF16), pltpu.VMEM((S, GLA_DK), BF16),
                        pltpu.VMEM((nc, GLA_DV, 2 * GLA_DK), BF16),
                        pltpu.VMEM((GLA_DV, 2 * GLA_DK), F32)],
        compiler_params=_cparams(("parallel", "parallel"), 48),
        name="gla",
    )(z, gq, gk, gv, gr, upw, bias, gain)


def _mix_out_kernel(x_ref, a_ref, g_ref, sga_ref, sgg_ref, wa_ref, wb_ref, wo_ref, gain_ref, wr_ref,
                    h_ref, xn_ref, aff_ref, afft_ref):
    sub, starts = _sub_blocks(x_ref.shape[0])
    mm = functools.partial(jnp.dot, preferred_element_type=F32)
    blk = lambda ref, r0: ref[r0:r0 + sub, :]
    pad = jnp.zeros((LANES - N_EXPERTS, sub), F32)

    y_att = [mm(blk(a_ref, r0), wa_ref[...]) for r0 in starts]
    y_gla = [mm(blk(g_ref, r0), wb_ref[...]) for r0 in starts]
    merged = [(blk(sga_ref, r0).astype(F32) * ya + blk(sgg_ref, r0).astype(F32) * yg).astype(BF16)
              for r0, ya, yg in zip(starts, y_att, y_gla)]
    hs = [blk(x_ref, r0) + mm(m, wo_ref[...]) for r0, m in zip(starts, merged)]
    xns = [_rms(h, gain_ref[...]) for h in hs]
    for r0, h, xn in zip(starts, hs, xns):
        h_ref[r0:r0 + sub, :] = h
        _store_token_tiles(xn_ref, xn, r0)
    his = [xn.astype(BF16) for xn in xns]
    parts = [lax.dot_general(wr_ref[...], jnp.concatenate([hi, (xn - hi.astype(F32)).astype(BF16)], axis=1),
                             NT_DIMS, preferred_element_type=F32) for xn, hi in zip(xns, his)]
    for r0, part in zip(starts, parts):
        logits = part[:N_EXPERTS] + part[N_EXPERTS:]
        e = jnp.exp(logits - jnp.max(logits, axis=0, keepdims=True))
        aff_t = e / jnp.sum(e, axis=0, keepdims=True)
        afft_ref[:, r0:r0 + sub] = aff_t
        aff_ref[r0:r0 + sub, :] = jnp.concatenate([aff_t, pad], axis=0).T[:, :N_EXPERTS]


def _mix_out(x2, a, g, sga, sgg, wa, wb, wo, gain, wr, tm):
    T = x2.shape[0]
    row = lambda n: pl.BlockSpec((tm, n), lambda i: (i, 0))
    return pl.pallas_call(
        _mix_out_kernel,
        grid=(T // tm,),
        in_specs=[row(D_MODEL), row(ATT_Q_DIM), row(GLA_VAL_DIM), row(D_MODEL), row(D_MODEL),
                  _full(wa.shape), _full(wb.shape), _full(wo.shape), _full(gain.shape), _full(wr.shape)],
        out_specs=[row(D_MODEL), pl.BlockSpec((tm * ROW_TILES, LANES), lambda i: (i, 0)), row(N_EXPERTS),
                   pl.BlockSpec((N_EXPERTS, tm), lambda i: (0, i))],
        out_shape=[jax.ShapeDtypeStruct((T, D_MODEL), F32), jax.ShapeDtypeStruct((T * ROW_TILES, LANES), F32),
                   jax.ShapeDtypeStruct((T, N_EXPERTS), F32), jax.ShapeDtypeStruct((N_EXPERTS, T), F32)],
        compiler_params=_cparams(("parallel",), 48),
        name="mix_out",
    )(x2, a, g, sga, sgg, wa, wb, wo, gain, wr)


ROUTE_WAYS = 4


def _route_kernel(aff_ref, idx_ref, cum_ref, *, cap, seq):
    E = N_EXPERTS
    aff = aff_ref[...]
    count = lambda mask: jnp.sum(mask.astype(jnp.int32), axis=1, keepdims=True)
    as_float = lambda pattern: lax.bitcast_convert_type(pattern, F32)

    def thr_body(t, pattern):
        cand = pattern | jnp.left_shift(jnp.int32(1), 30 - t)
        return jnp.where(count(aff >= as_float(cand)) >= cap, cand, pattern)

    thr = as_float(lax.fori_loop(0, 31, thr_body, jnp.zeros((E, 1), jnp.int32)))
    above = aff > thr
    tie = aff == thr
    need = cap - count(above)

    pos = lax.broadcasted_iota(jnp.int32, (E, seq), 1)

    def tie_body(t, last):
        cand = last | jnp.left_shift(jnp.int32(1), (seq.bit_length() - 2) - t)
        return jnp.where(count(tie & (pos < cand)) < need, cand, last)

    last = lax.fori_loop(0, seq.bit_length() - 1, tie_body, jnp.zeros((E, 1), jnp.int32))
    sel = (above | (tie & (pos <= last))).astype(BF16)

    nt = seq // LANES
    lrow = lax.broadcasted_iota(jnp.int32, (LANES, LANES), 0)
    lcol = lax.broadcasted_iota(jnp.int32, (LANES, LANES), 1)
    tri = (lrow <= lcol).astype(BF16)
    mm = functools.partial(jnp.dot, preferred_element_type=F32)
    for t in range(nt):
        cum_ref[t * E:(t + 1) * E, :] = mm(sel[:, t * LANES:(t + 1) * LANES], tri)
    tile_of = (lax.broadcasted_iota(jnp.int32, (seq, LANES), 0) // LANES
               == lax.broadcasted_iota(jnp.int32, (seq, LANES), 1)).astype(BF16)
    per_tile = mm(sel, tile_of)
    lane = lax.broadcasted_iota(jnp.int32, (1, LANES), 1)
    far = jnp.float32(2 * seq)
    t_end = jnp.where(lane < nt, mm(per_tile.astype(BF16), tri), far)
    t_start = jnp.where(lane < nt, t_end - per_tile, far)
    pad = jnp.zeros((LANES - E, LANES), F32)
    t_start_cols = jnp.concatenate([jnp.where(lane < nt, t_start, 0.0), pad], axis=0).T

    slot = lax.broadcasted_iota(jnp.int32, (cap, LANES), 0).astype(F32)
    ones = jnp.ones((LANES, LANES), BF16)
    zrows = jnp.zeros((LANES - nt, 2 * LANES), F32)
    for e0 in range(0, E, ROUTE_WAYS):
        es = range(e0, e0 + ROUTE_WAYS)
        tiles, picks, whole = [], [], []
        for e in es:
            absc = cum_ref[pl.ds(e, nt, stride=E), :] + t_start_cols[0:nt, e:e + 1]
            hi = jnp.where(absc >= 256.0, 1.0, 0.0) + jnp.where(absc >= 512.0, 1.0, 0.0)
            lo = absc - 256.0 * hi
            tiles.append(jnp.concatenate([jnp.concatenate([lo, hi], axis=1), zrows], axis=0).astype(BF16))
            done = jnp.where(t_end[e:e + 1] <= slot, 1.0, 0.0)
            whole.append(done)
            picks.append((jnp.where(t_start[e:e + 1] <= slot, 1.0, 0.0) - done).astype(BF16))
        rows = [mm(p, w) for p, w in zip(picks, tiles)]
        votes = [(jnp.where(r[:, :LANES] + 256.0 * r[:, LANES:] <= slot, 1.0, 0.0) + float(LANES) * d).astype(BF16)
                 for r, d in zip(rows, whole)]
        for e, v in zip(es, votes):
            idx_ref[0, :, e:e + 1] = mm(v, ones)[:, e:e + 1].astype(jnp.int32)


def _route(aff_t, batch, cap):
    E, T = aff_t.shape
    B, S = batch, T // batch
    return pl.pallas_call(
        functools.partial(_route_kernel, cap=cap, seq=S),
        grid=(B,),
        in_specs=[pl.BlockSpec((E, S), lambda b: (0, b))],
        out_specs=pl.BlockSpec((1, cap, E), lambda b: (b, 0, 0)),
        out_shape=jax.ShapeDtypeStruct((B, cap, E), jnp.int32),
        scratch_shapes=[pltpu.VMEM((S // LANES * E, LANES), F32)],
        compiler_params=_cparams(("parallel",), 32),
        name="route",
    )(aff_t)


def _gather_kernel(idx_ref, xn_ref, aff_ref, xg_ref, wg_ref, ws_ref, *, cap):
    e = pl.program_id(1)

    for i in range(cap):
        t = idx_ref[0, 0, i]
        src = pl.multiple_of(t * SUBLANES, SUBLANES)
        xg_ref[0, 0, i * SUBLANES:(i + 1) * SUBLANES, :] = xn_ref[0, pl.ds(src, SUBLANES), :]
        ws_ref[i:i + 1, :] = aff_ref[0, pl.ds(t, 1), :]
    lane = lax.broadcasted_iota(jnp.int32, (cap, N_EXPERTS), 1)
    wg_ref[0, 0] = jnp.sum(jnp.where(lane == e, ws_ref[...], 0.0), axis=1, keepdims=True)


def _gather(idx, xn_tiles, aff, cap):
    B, S, E = aff.shape
    return pl.pallas_call(
        functools.partial(_gather_kernel, cap=cap),
        grid=(B, E),
        in_specs=[pl.BlockSpec((1, 1, cap), lambda b, e: (b * E + e, 0, 0), memory_space=pltpu.SMEM),
                  pl.BlockSpec((1, S * SUBLANES, LANES), lambda b, e: (b, 0, 0)),
                  pl.BlockSpec((1, S, E), lambda b, e: (b, 0, 0))],
        out_specs=[pl.BlockSpec((1, 1, cap * SUBLANES, LANES), lambda b, e: (b, e, 0, 0)),
                   pl.BlockSpec((1, 1, cap, 1), lambda b, e: (b, e, 0, 0))],
        out_shape=[jax.ShapeDtypeStruct((B, E, cap * SUBLANES, LANES), F32),
                   jax.ShapeDtypeStruct((B, E, cap, 1), F32)],
        scratch_shapes=[pltpu.VMEM((cap, E), F32)],
        compiler_params=_cparams(("arbitrary", "arbitrary"), 48),
        name="gather",
    )(idx.reshape(B * E, 1, cap), xn_tiles, aff)


def _ffn_kernel(xg_ref, wg_ref, w1_ref, w2_ref, w3_ref, y_ref, b1_ref, b2_ref, b3_ref):
    @pl.when(pl.program_id(1) == 0)
    def _():
        b1_ref[...] = w1_ref[0].astype(BF16)
        b2_ref[...] = w2_ref[0].astype(BF16)
        b3_ref[...] = w3_ref[0].astype(BF16)

    rows = wg_ref.shape[2]
    seqs = range(xg_ref.shape[0])
    mm = functools.partial(jnp.dot, preferred_element_type=F32)
    xgs = [_load_token_tiles(xg_ref.at[i, 0], rows).astype(BF16) for i in seqs]
    gates = [mm(xg, b1_ref[...]) for xg in xgs]
    ups = [mm(xg, b2_ref[...]) for xg in xgs]
    hids = [(gate * _sigmoid(gate) * up).astype(BF16) for gate, up in zip(gates, ups)]
    ys = [mm(hid, b3_ref[...]) * wg_ref[i, 0] for i, hid in zip(seqs, hids)]
    for i, y in zip(seqs, ys):
        _store_token_tiles(y_ref.at[i, 0], y)


FFN_SEQS_PER_STEP = 2


def _ffn(xg, wg, w1, w2, w3):
    B, E, C, _ = wg.shape
    _, D, F = w1.shape
    n = FFN_SEQS_PER_STEP
    return pl.pallas_call(
        _ffn_kernel,
        grid=(E, B // n),
        in_specs=[pl.BlockSpec((n, 1, C * ROW_TILES, LANES), lambda e, b: (b, e, 0, 0)),
                  pl.BlockSpec((n, 1, C, 1), lambda e, b: (b, e, 0, 0)),
                  pl.BlockSpec((1, D, F), lambda e, b: (e, 0, 0)),
                  pl.BlockSpec((1, D, F), lambda e, b: (e, 0, 0)),
                  pl.BlockSpec((1, F, D), lambda e, b: (e, 0, 0))],
        out_specs=pl.BlockSpec((n, 1, C * ROW_TILES, LANES), lambda e, b: (b, e, 0, 0)),
        out_shape=jax.ShapeDtypeStruct((B, E, C * ROW_TILES, LANES), F32),
        scratch_shapes=[pltpu.VMEM((D, F), BF16), pltpu.VMEM((D, F), BF16), pltpu.VMEM((F, D), BF16)],
        compiler_params=_cparams(("arbitrary", "arbitrary"), 60),
        name="ffn",
    )(xg, wg, w1, w2, w3)


SCATTER_BATCH = 16


def _scatter_kernel(idx_ref, y_ref, o_ref, *, cap):
    @pl.when(pl.program_id(1) == 0)
    def _():
        o_ref[...] = jnp.zeros_like(o_ref)

    for i0 in range(0, cap, SCATTER_BATCH):
        slots = range(i0, i0 + SCATTER_BATCH)
        rows = [pl.ds(pl.multiple_of(idx_ref[0, 0, i] * SUBLANES, SUBLANES), SUBLANES) for i in slots]
        new = [o_ref[0, r, :] + y_ref[0, 0, i * SUBLANES:(i + 1) * SUBLANES, :] for r, i in zip(rows, slots)]
        for r, v in zip(rows, new):
            o_ref[0, r, :] = v


def _scatter(idx, y_tiles, seq):
    B, E, cap = idx.shape
    return pl.pallas_call(
        functools.partial(_scatter_kernel, cap=cap),
        grid=(B, E),
        in_specs=[pl.BlockSpec((1, 1, cap), lambda b, e: (b * E + e, 0, 0), memory_space=pltpu.SMEM),
                  pl.BlockSpec((1, 1, cap * SUBLANES, LANES), lambda b, e: (b, e, 0, 0))],
        out_specs=pl.BlockSpec((1, seq * SUBLANES, LANES), lambda b, e: (b, 0, 0)),
        out_shape=jax.ShapeDtypeStruct((B, seq * SUBLANES, LANES), F32),
        compiler_params=_cparams(("arbitrary", "arbitrary"), 48),
        name="scatter",
    )(idx.reshape(B * E, 1, cap), y_tiles)


def _ple_out_kernel(h_ref, moe_ref, p_ref, gple_ref, wpg_ref, wple_ref, gfin_ref, o_ref):
    sub, starts = _sub_blocks(h_ref.shape[0])
    mm = functools.partial(jnp.dot, preferred_element_type=F32)
    hs = [h_ref[r0:r0 + sub, :] + _load_token_tiles(moe_ref, sub, r0) for r0 in starts]
    ns = [_rms(h, gple_ref[...]).astype(BF16) for h in hs]
    gates = [_sigmoid(mm(n, wpg_ref[...])) for n in ns]
    embs = [mm(p_ref[r0:r0 + sub, :].astype(BF16), wple_ref[...]) for r0 in starts]
    for r0, h, gate, emb in zip(starts, hs, gates, embs):
        o_ref[r0:r0 + sub, :] = _rms(h + gate * emb, gfin_ref[...])


def _ple_out(h, moe, p2, gple, wpg, wple, gfin, tm):
    T = h.shape[0]
    row = lambda n: pl.BlockSpec((tm, n), lambda i: (i, 0))
    return pl.pallas_call(
        _ple_out_kernel,
        grid=(T // tm,),
        in_specs=[row(D_MODEL), pl.BlockSpec((tm * ROW_TILES, LANES), lambda i: (i, 0)), row(PLE_DIM),
                  _full(gple.shape), _full(wpg.shape), _full(wple.shape), _full(gfin.shape)],
        out_specs=row(D_MODEL),
        out_shape=jax.ShapeDtypeStruct((T, D_MODEL), F32),
        compiler_params=_cparams(("parallel",), 48),
        name="ple_out",
    )(h, moe, p2, gple, wpg, wple, gfin)


def kernel(x, p, positions, norm_mix, w_in, gla_gate_up_fwd, gla_gate_bias_fwd, gla_gate_up_bwd, gla_gate_bias_bwd, attn_sink, gla_norm, w_branch_attn, w_branch_gla, w_out, norm_ffn, w_router, w_exp_gate, w_exp_up, w_exp_down, norm_ple, w_ple_gate, w_ple, norm_final):
    B, S, D = x.shape
    T = B * S
    depth = w_in.shape[0]
    cap = CAPACITY_FACTOR * S // N_EXPERTS
    R = GLA_GATE_RANK

    posr = positions.reshape(1, T)
    inv_freq = ROPE_THETA ** (-jnp.arange(0, ROPE_DIM, 2, dtype=F32) / ROPE_DIM)
    invfc = inv_freq.reshape(ROPE_DIM // 2, 1)

    h = x.reshape(T, D)
    for l in range(depth):
        o = 0
        cols = {}
        for name, n in (("q", ATT_Q_DIM), ("k", ATT_KV_DIM), ("v", ATT_KV_DIM), ("g", 2 * GLA_KEY_DIM + 2 * GLA_VAL_DIM),
                        ("z", 2 * R), ("gate", 2 * D_MODEL)):
            cols[name] = w_in[l][:, o:o + n]
            o += n
        w_t = jnp.concatenate([cols["q"], cols["v"]], axis=1).T.astype(BF16)
        w_rows = jnp.concatenate([cols["k"]] + [cols["z"]] * (LANES // (2 * R)) + [cols["g"], cols["gate"]],
                                 axis=1).astype(BF16)
        per_head = lambda w: w.reshape(-1, GLA_HEADS, GLA_DK).swapaxes(0, 1)
        upf, upb = per_head(gla_gate_up_fwd[l]), per_head(gla_gate_up_bwd[l])
        up = jnp.concatenate([jnp.concatenate([upf, jnp.zeros_like(upf)], axis=2),
                              jnp.concatenate([jnp.zeros_like(upb), upb], axis=2)], axis=1)
        up_hi = up.astype(BF16)
        up_lo = (up - up_hi.astype(F32)).astype(BF16)
        upw = jnp.concatenate([up_hi, up_hi, up_lo, jnp.zeros_like(up_lo)], axis=1)
        gbias = jnp.concatenate([per_head(gla_gate_bias_fwd[l]), per_head(gla_gate_bias_bwd[l])], axis=2)
        wr = w_router[l]
        wr_hi = wr.astype(BF16)
        wr_lo = (wr - wr_hi.astype(F32)).astype(BF16)
        wr2 = jnp.concatenate([jnp.concatenate([wr_hi, wr_lo], axis=1),
                               jnp.concatenate([wr_hi, jnp.zeros_like(wr_lo)], axis=1)], axis=0).T

        qt, k0, k1, vt, gq, gk, gv, gr, z, sga, sgg = _in_proj(
            h, posr, invfc, norm_mix[l].reshape(1, D), w_t, w_rows, tm=512)

        att = _swa(attn_sink[l], qt, k0.reshape(B, S, -1), k1.reshape(B, S, -1), vt, batch=B, tq=512)
        gla = _gla(z.reshape(B, S, -1), gq.reshape(B, S, -1), gk.reshape(B, S, -1), gv.reshape(B, S, -1),
                   gr.reshape(B, S, -1), upw, gbias, gla_norm[l].reshape(1, -1))

        h1, xn, aff, aff_t = _mix_out(h, att.reshape(T, -1), gla.reshape(T, -1), sga, sgg,
                               w_branch_attn[l].astype(BF16), w_branch_gla[l].astype(BF16),
                               w_out[l].astype(BF16), norm_ffn[l].reshape(1, D), wr2, tm=512)

        aff3 = aff.reshape(B, S, N_EXPERTS)
        idx = _route(aff_t, B, cap)
        idx = jnp.swapaxes(idx, 1, 2)
        xg, wg = _gather(idx, xn.reshape(B, S * SUBLANES, LANES), aff3, cap)
        y = _ffn(xg, wg, w_exp_gate[l], w_exp_up[l], w_exp_down[l])
        moe = _scatter(idx, y, S)

        last = l == depth - 1
        gfin = norm_final.reshape(1, D)
        assert last, "the final norm is fused into the last layer's PLE kernel"
        h = _ple_out(h1, moe.reshape(T * ROW_TILES, LANES), p[l].reshape(T, PLE_DIM), norm_ple[l].reshape(1, D),
                     w_ple_gate[l].astype(BF16), w_ple[l].astype(BF16), gfin, tm=1024)
    return h.reshape(B, S, D)
```

```python
import functools
import math

import jax
import jax.numpy as jnp
from jax import lax
from jax.experimental import pallas as pl
from jax.experimental.pallas import tpu as pltpu

D_MODEL = 1024
ATT_HEADS = 8
ATT_KV_HEADS = 2
ATT_HEAD_DIM = 64
ATT_GROUP = ATT_HEADS // ATT_KV_HEADS
ATT_Q_DIM = ATT_HEADS * ATT_HEAD_DIM
ATT_KV_DIM = ATT_KV_HEADS * ATT_HEAD_DIM
WINDOW = 128
ROPE_DIM = ATT_HEAD_DIM // 4
ROPE_THETA = 500000.0
GLA_HEADS = 4
GLA_KEY_DIM = D_MODEL // 2
GLA_VAL_DIM = D_MODEL
GLA_DK = GLA_KEY_DIM // GLA_HEADS
GLA_DV = GLA_VAL_DIM // GLA_HEADS
GLA_GATE_RANK = 16
GLA_GATE_NORM = 16.0
GLA_CHUNK = 64
N_EXPERTS = 16
EXPERT_FF = D_MODEL
CAPACITY_FACTOR = 2
PLE_DIM = 256
EPS = 1e-6

LANES = 128
MIB = 1024 * 1024
BF16 = jnp.bfloat16
F32 = jnp.float32
LOG2E = math.log2(math.e)

NT_DIMS = (((1,), (1,)), ((), ()))
TN_DIMS = (((0,), (0,)), ((), ()))


def _cparams(sem, vmem_mib):
    return pltpu.CompilerParams(dimension_semantics=sem, vmem_limit_bytes=vmem_mib * MIB)


def _full(shape):
    n = len(shape)
    return pl.BlockSpec(shape, lambda *_: (0,) * n)


def _rms(x, gain):
    ms = jnp.mean(x * x, axis=-1, keepdims=True)
    return x * lax.rsqrt(ms + EPS) * gain


def _sigmoid(x):
    return 0.5 * jnp.tanh(0.5 * x) + 0.5


SUBLANES = 8
ROW_TILES = D_MODEL // LANES


def _store_token_tiles(ref2d, x, first_row=0):
    rows = x.shape[0]
    for j in range(ROW_TILES):
        ref2d[pl.ds(first_row * ROW_TILES + j, rows, stride=ROW_TILES), :] = x[:, j * LANES:(j + 1) * LANES]


def _load_token_tiles(ref2d, rows, first_row=0):
    return jnp.concatenate([ref2d[pl.ds(first_row * ROW_TILES + j, rows, stride=ROW_TILES), :]
                            for j in range(ROW_TILES)], axis=1)


SUB_ROWS = 256


def _sub_blocks(tile_rows):
    return SUB_ROWS, list(range(0, tile_rows, SUB_ROWS))


def _rope_rows(t, cos_r, sin_r, heads):
    half = ROPE_DIM // 2
    rows = []
    for h in range(heads):
        r0 = h * ATT_HEAD_DIM
        t1, t2 = t[r0:r0 + half], t[r0 + half:r0 + ROPE_DIM]
        rows += [t1 * cos_r - t2 * sin_r, t2 * cos_r + t1 * sin_r, t[r0 + ROPE_DIM:r0 + ATT_HEAD_DIM]]
    return jnp.concatenate(rows, axis=0)


def _in_proj_kernel(x_ref, posr_ref, invfc_ref, gain_ref, wqt_ref, wkz_ref, wvt_ref,
                    wgqk_ref, wgv_ref, wgr_ref, wgate_ref,
                    qt_ref, k0_ref, k1_ref, vt_ref, gq_ref, gk_ref, gv_ref, gr_ref, z_ref, sga_ref, sgg_ref):
    a = _rms(x_ref[...], gain_ref[...]).astype(BF16)
    ang_t = invfc_ref[...] * posr_ref[...].astype(F32)
    cos_r, sin_r = jnp.cos(ang_t), jnp.sin(ang_t)

    qt = lax.dot_general(wqt_ref[...], a, NT_DIMS, preferred_element_type=F32)
    qt_ref[...] = (_rope_rows(qt, cos_r, sin_r, ATT_HEADS) * (ATT_HEAD_DIM ** -0.5 * LOG2E)).astype(BF16)

    kz = jnp.dot(a, wkz_ref[...], preferred_element_type=F32)
    z_ref[...] = kz[:, ATT_KV_DIM:]
    k = _rope_rows(kz[:, :ATT_KV_DIM].T, cos_r, sin_r, ATT_KV_HEADS).T.astype(BF16)
    k0_ref[...] = k[:, :ATT_HEAD_DIM]
    k1_ref[...] = k[:, ATT_HEAD_DIM:]
    vt_ref[...] = lax.dot_general(wvt_ref[...], a, NT_DIMS, preferred_element_type=F32).astype(BF16)

    gqk = jnp.dot(a, wgqk_ref[...], preferred_element_type=F32)
    gq_ref[...] = (gqk[:, :GLA_KEY_DIM] * (GLA_DK ** -0.5)).astype(BF16)
    gk_ref[...] = gqk[:, GLA_KEY_DIM:].astype(BF16)
    gv_ref[...] = jnp.dot(a, wgv_ref[...], preferred_element_type=F32).astype(BF16)
    gr = jnp.dot(a, wgr_ref[...], preferred_element_type=F32)
    gr_ref[...] = (gr * _sigmoid(gr)).astype(BF16)
    gates = jnp.dot(a, wgate_ref[...], preferred_element_type=F32)
    sga_ref[...] = _sigmoid(gates[:, :D_MODEL]).astype(BF16)
    sgg_ref[...] = _sigmoid(gates[:, D_MODEL:]).astype(BF16)


def _in_proj(x2, posr, invfc, gain, wqt, wkz, wvt, wgqk, wgv, wgr, wgate, tm):
    T = x2.shape[0]
    row = lambda n: pl.BlockSpec((tm, n), lambda i: (i, 0))
    col = lambda n: pl.BlockSpec((n, tm), lambda i: (0, i))
    row_widths = (ATT_HEAD_DIM, ATT_HEAD_DIM, None, GLA_KEY_DIM, GLA_KEY_DIM, GLA_VAL_DIM,
                  GLA_VAL_DIM, wkz.shape[1] - ATT_KV_DIM, D_MODEL, D_MODEL)
    row_dtypes = (BF16,) * 7 + (F32, BF16, BF16)
    out_specs = [col(ATT_Q_DIM)]
    out_shape = [jax.ShapeDtypeStruct((ATT_Q_DIM, T), BF16)]
    for n, dt in zip(row_widths, row_dtypes):
        if n is None:
            out_specs.append(col(ATT_KV_DIM))
            out_shape.append(jax.ShapeDtypeStruct((ATT_KV_DIM, T), BF16))
        else:
            out_specs.append(row(n))
            out_shape.append(jax.ShapeDtypeStruct((T, n), dt))
    consts = (invfc, gain, wqt, wkz, wvt, wgqk, wgv, wgr, wgate)
    return pl.pallas_call(
        _in_proj_kernel,
        grid=(T // tm,),
        in_specs=[row(D_MODEL), col(1)] + [_full(c.shape) for c in consts],
        out_specs=out_specs,
        out_shape=out_shape,
        compiler_params=_cparams(("parallel",), 56),
        name="in_proj",
    )(x2, posr, *consts)


def _swa_kernel(sink_ref, qt_ref, k0_ref, k1_ref, vt_ref, o_ref, *, tq, seq):
    blk = WINDOW
    span = 3 * blk
    hd = ATT_HEAD_DIM
    n = pl.program_id(1)
    ones = jnp.ones((16, span), BF16)
    kv_refs = (k0_ref, k1_ref)

    def window_start(sb):
        return pl.multiple_of(jnp.clip(n * tq + (sb - 1) * blk, 0, seq - span), blk)

    def scores(sb, g):
        kw = kv_refs[g][0, pl.ds(window_start(sb), span), :]
        heads = range(g * ATT_GROUP, (g + 1) * ATT_GROUP)
        qs = jnp.concatenate([qt_ref[h * hd:(h + 1) * hd, sb * blk:(sb + 1) * blk] for h in heads], axis=1)
        return jnp.dot(kw, qs, preferred_element_type=F32)

    work = [(sb, g) for sb in range(tq // blk) for g in range(ATT_KV_HEADS)]
    s_next = scores(*work[0])
    outs = []
    for step, (sb, g) in enumerate(work):
        s_all = s_next
        if step + 1 < len(work):
            s_next = scores(*work[step + 1])
        q0 = n * tq + sb * blk
        start = window_start(sb)
        kj = start + lax.broadcasted_iota(jnp.int32, (span, blk), 0)
        qi = q0 + lax.broadcasted_iota(jnp.int32, (span, blk), 1)
        valid = jnp.abs(qi - kj) <= WINDOW
        vaug = jnp.concatenate([vt_ref[g * hd:(g + 1) * hd, pl.ds(start, span)], ones], axis=0)
        sinks = [sink_ref[g * ATT_GROUP + i] * LOG2E for i in range(ATT_GROUP)]
        ss = [jnp.where(valid, s_all[:, i * blk:(i + 1) * blk], -jnp.inf) for i in range(ATT_GROUP)]
        ms = [jnp.maximum(jnp.max(s, axis=0, keepdims=True), sink) for s, sink in zip(ss, sinks)]
        rs = [jnp.dot(vaug, jnp.exp2(s - m).astype(BF16), preferred_element_type=F32) for s, m in zip(ss, ms)]
        outs += [r[:hd] / (r[hd:hd + 1] + jnp.exp2(sink - m)) for r, m, sink in zip(rs, ms, sinks)]
        if g == ATT_KV_HEADS - 1:
            for pr in range(ATT_HEADS // 2):
                pair = jnp.concatenate([outs[2 * pr], outs[2 * pr + 1]], axis=0)
                o_ref[0, sb * blk:(sb + 1) * blk, pr * 2 * hd:(pr + 1) * 2 * hd] = pair.T.astype(BF16)
            outs = []


def _swa(sink, qt, k0, k1, vt, batch, tq):
    S = k0.shape[1]
    nq = S // tq
    kspec = pl.BlockSpec((1, S, ATT_HEAD_DIM), lambda b, n: (b, 0, 0))
    return pl.pallas_call(
        functools.partial(_swa_kernel, tq=tq, seq=S),
        grid=(batch, nq),
        in_specs=[pl.BlockSpec(memory_space=pltpu.SMEM),
                  pl.BlockSpec((ATT_Q_DIM, tq), lambda b, n: (0, b * nq + n)),
                  kspec, kspec,
                  pl.BlockSpec((ATT_KV_DIM, S), lambda b, n: (0, b))],
        out_specs=pl.BlockSpec((1, tq, ATT_Q_DIM), lambda b, n: (b, n, 0)),
        out_shape=jax.ShapeDtypeStruct((batch, S, ATT_Q_DIM), BF16),
        compiler_params=_cparams(("parallel", "parallel"), 32),
        name="swa",
    )(sink, qt, k0, k1, vt)


def _log2_sigmoid(u):
    return jnp.minimum(u, 0.0) * LOG2E - jnp.log2(1.0 + jnp.exp2(jnp.abs(u) * -LOG2E))


def _split2(x):
    hi = x.astype(BF16)
    return hi, (x - hi.astype(F32)).astype(BF16)


GLA_WAYS = 4


def _gla_kernel(z_ref, q_ref, k_ref, v_ref, r_ref, upw_ref, bias_ref, gain_ref,
                o_ref, cf_ref, cb_ref, kef_ref, keb_ref, st_ref, s_ref, *, seq):
    L = GLA_CHUNK
    R2 = 2 * GLA_GATE_RANK
    nc = seq // L
    grp = 4 * L
    cpg = grp // L
    dk = GLA_DK
    mm = functools.partial(jnp.dot, preferred_element_type=F32)
    nt = functools.partial(lax.dot_general, dimension_numbers=NT_DIMS, preferred_element_type=F32)

    row = lax.broadcasted_iota(jnp.int32, (grp, grp), 0)
    col = lax.broadcasted_iota(jnp.int32, (grp, grp), 1)
    same = (row // L) == (col // L)
    fwd_mask = same & (col <= row)
    bwd_mask = same & (col > row)
    tri_lo = jnp.where(fwd_mask, 1.0, 0.0).astype(BF16)
    tri_up = jnp.where(same & (col >= row), 1.0, 0.0).astype(BF16)
    lane = lax.broadcasted_iota(jnp.int32, (1, LANES), 1)
    use_lo = (lane >= R2) & (lane < 2 * R2)

    def group_starts(i):
        return [pl.multiple_of((i * GLA_WAYS + w) * grp, grp) for w in range(GLA_WAYS)]

    def cum_body(i, carry):
        r0s = group_starts(i)
        zs = [_split2(z_ref[0, pl.ds(r0, grp), :]) for r0 in r0s]
        us = [mm(jnp.where(use_lo, zl, zh), upw_ref[...]) + bias_ref[...] for zh, zl in zs]
        las = [_split2(_log2_sigmoid(u) * (1.0 / GLA_GATE_NORM)) for u in us]
        cfxs = [mm(tri_lo, jnp.concatenate([lh[:, :dk], ll[:, :dk]], axis=1)) for lh, ll in las]
        cbxs = [mm(tri_up, jnp.concatenate([lh[:, dk:], ll[:, dk:]], axis=1)) for lh, ll in las]
        for r0, cfx, cbx in zip(r0s, cfxs, cbxs):
            cf = cfx[:, :dk] + cfx[:, dk:]
            cb = cbx[:, :dk] + cbx[:, dk:]
            cf_ref[pl.ds(r0, grp), :] = cf
            cb_ref[pl.ds(r0, grp), :] = cb
            k = k_ref[0, pl.ds(r0, grp), :].astype(F32)
            for c in range(cpg):
                sl = slice(c * L, (c + 1) * L)
                gf = cf[(c + 1) * L - 1:(c + 1) * L]
                gb = cb[c * L:c * L + 1]
                kef_ref[pl.ds(r0 + c * L, L), :] = (k[sl] * jnp.exp2(gf - cf[sl])).astype(BF16)
                keb_ref[pl.ds(r0 + c * L, L), :] = (k[sl] * jnp.exp2(gb - cb[sl])).astype(BF16)
        return carry

    lax.fori_loop(0, seq // (grp * GLA_WAYS), cum_body, 0)

    s_ref[...] = jnp.zeros_like(s_ref)
    zero_k = jnp.zeros((L, dk), BF16)

    def state_body(i, carry):
        j = nc - 1 - i
        rf = pl.multiple_of(i * L, L)
        rb = pl.multiple_of(j * L, L)
        vcat = jnp.concatenate([v_ref[0, pl.ds(rf, L), :], v_ref[0, pl.ds(rb, L), :]], axis=0)
        kblk = jnp.concatenate([jnp.concatenate([kef_ref[pl.ds(rf, L), :], zero_k], axis=1),
                                jnp.concatenate([zero_k, keb_ref[pl.ds(rb, L), :]], axis=1)], axis=0)
        kv = lax.dot_general(vcat, kblk, TN_DIMS, preferred_element_type=F32)
        decay = jnp.exp2(jnp.concatenate([cf_ref[pl.ds(rf + L - 1, 1), :], cb_ref[pl.ds(rb, 1), :]], axis=1))
        s = s_ref[...]
        st_ref[i, :, 0:dk] = s[:, :dk].astype(BF16)
        st_ref[j, :, dk:2 * dk] = s[:, dk:].astype(BF16)
        s_ref[...] = s * decay + kv
        return carry

    lax.fori_loop(0, nc, state_body, 0, unroll=True)

    def out_body(i, carry):
        r0s = group_starts(i)
        ops = []
        for r0 in r0s:
            q = q_ref[0, pl.ds(r0, grp), :].astype(F32)
            k = k_ref[0, pl.ds(r0, grp), :].astype(F32)
            cf = cf_ref[pl.ds(r0, grp), :]
            cb = cb_ref[pl.ds(r0, grp), :]
            ops.append(((q * jnp.exp2(cf)).astype(BF16), (k * jnp.exp2(-cf)).astype(BF16),
                        (q * jnp.exp2(cb)).astype(BF16), (k * jnp.exp2(-cb)).astype(BF16)))
        scores = [(nt(qf, kf), nt(qb, kb)) for qf, kf, qb, kb in ops]
        attns = [jnp.where(fwd_mask, af, jnp.where(bwd_mask, ab, 0.0)).astype(BF16) for af, ab in scores]
        outs = []
        for r0, attn, (qf, _, qb, _) in zip(r0s, attns, ops):
            c0 = r0 // L
            qcat = jnp.concatenate([qf, qb], axis=1)
            inter = jnp.concatenate([nt(qcat[c * L:(c + 1) * L], st_ref[c0 + c]) for c in range(cpg)], axis=0)
            outs.append(mm(attn, v_ref[0, pl.ds(r0, grp), :]) + inter)
        for r0, o in zip(r0s, outs):
            o_ref[0, pl.ds(r0, grp), :] = (_rms(o, gain_ref[...]) * r_ref[0, pl.ds(r0, grp), :].astype(F32)).astype(BF16)
        return carry

    lax.fori_loop(0, seq // (grp * GLA_WAYS), out_body, 0)


def _gla(z, gq, gk, gv, gr, upw, bias, gain):
    B, S, _ = gq.shape
    nc = S // GLA_CHUNK
    seq_blk = lambda n: pl.BlockSpec((1, S, n), lambda b, h: (b, 0, h))
    head_blk = lambda r, n: pl.BlockSpec((None, r, n), lambda b, h: (h, 0, 0))
    return pl.pallas_call(
        functools.partial(_gla_kernel, seq=S),
        grid=(B, GLA_HEADS),
        in_specs=[pl.BlockSpec((1, S, LANES), lambda b, h: (b, 0, 0)),
                  seq_blk(GLA_DK), seq_blk(GLA_DK), seq_blk(GLA_DV), seq_blk(GLA_DV),
                  head_blk(LANES, 2 * GLA_DK), head_blk(1, 2 * GLA_DK),
                  pl.BlockSpec((1, GLA_DV), lambda b, h: (0, h))],
        out_specs=seq_blk(GLA_DV),
        out_shape=jax.ShapeDtypeStruct((B, S, GLA_VAL_DIM), BF16),
        scratch_shapes=[pltpu.VMEM((S, GLA_DK), F32), pltpu.VMEM((S, GLA_DK), F32),
                        pltpu.VMEM((S, GLA_DK), BF16), pltpu.VMEM((S, GLA_DK), BF16),
                        pltpu.VMEM((nc, GLA_DV, 2 * GLA_DK), BF16),
                        pltpu.VMEM((GLA_DV, 2 * GLA_DK), F32)],
        compiler_params=_cparams(("parallel", "parallel"), 48),
        name="gla",
    )(z, gq, gk, gv, gr, upw, bias, gain)


def _mix_out_kernel(x_ref, a_ref, g_ref, sga_ref, sgg_ref, wa_ref, wb_ref, wo_ref, gain_ref, wr_ref,
                    h_ref, xn_ref, aff_ref, afft_ref):
    sub, starts = _sub_blocks(x_ref.shape[0])
    mm = functools.partial(jnp.dot, preferred_element_type=F32)
    blk = lambda ref, r0: ref[r0:r0 + sub, :]
    pad = jnp.zeros((LANES - N_EXPERTS, sub), F32)

    y_att = [mm(blk(a_ref, r0), wa_ref[...]) for r0 in starts]
    y_gla = [mm(blk(g_ref, r0), wb_ref[...]) for r0 in starts]
    merged = [(blk(sga_ref, r0).astype(F32) * ya + blk(sgg_ref, r0).astype(F32) * yg).astype(BF16)
              for r0, ya, yg in zip(starts, y_att, y_gla)]
    hs = [blk(x_ref, r0) + mm(m, wo_ref[...]) for r0, m in zip(starts, merged)]
    xns = [_rms(h, gain_ref[...]) for h in hs]
    for r0, h, xn in zip(starts, hs, xns):
        h_ref[r0:r0 + sub, :] = h
        _store_token_tiles(xn_ref, xn, r0)
    his = [xn.astype(BF16) for xn in xns]
    parts = [lax.dot_general(wr_ref[...], jnp.concatenate([hi, (xn - hi.astype(F32)).astype(BF16)], axis=1),
                             NT_DIMS, preferred_element_type=F32) for xn, hi in zip(xns, his)]
    for r0, part in zip(starts, parts):
        logits = part[:N_EXPERTS] + part[N_EXPERTS:]
        e = jnp.exp(logits - jnp.max(logits, axis=0, keepdims=True))
        aff_t = e / jnp.sum(e, axis=0, keepdims=True)
        afft_ref[:, r0:r0 + sub] = aff_t
        aff_ref[r0:r0 + sub, :] = jnp.concatenate([aff_t, pad], axis=0).T[:, :N_EXPERTS]


def _mix_out(x2, a, g, sga, sgg, wa, wb, wo, gain, wr, tm):
    T = x2.shape[0]
    row = lambda n: pl.BlockSpec((tm, n), lambda i: (i, 0))
    return pl.pallas_call(
        _mix_out_kernel,
        grid=(T // tm,),
        in_specs=[row(D_MODEL), row(ATT_Q_DIM), row(GLA_VAL_DIM), row(D_MODEL), row(D_MODEL),
                  _full(wa.shape), _full(wb.shape), _full(wo.shape), _full(gain.shape), _full(wr.shape)],
        out_specs=[row(D_MODEL), pl.BlockSpec((tm * ROW_TILES, LANES), lambda i: (i, 0)), row(N_EXPERTS),
                   pl.BlockSpec((N_EXPERTS, tm), lambda i: (0, i))],
        out_shape=[jax.ShapeDtypeStruct((T, D_MODEL), F32), jax.ShapeDtypeStruct((T * ROW_TILES, LANES), F32),
                   jax.ShapeDtypeStruct((T, N_EXPERTS), F32), jax.ShapeDtypeStruct((N_EXPERTS, T), F32)],
        compiler_params=_cparams(("parallel",), 48),
        name="mix_out",
    )(x2, a, g, sga, sgg, wa, wb, wo, gain, wr)


ROUTE_WAYS = 4


def _route_kernel(aff_ref, idx_ref, cum_ref, *, cap, seq):
    E = N_EXPERTS
    aff = aff_ref[...]
    count = lambda mask: jnp.sum(mask.astype(jnp.int32), axis=1, keepdims=True)
    as_float = lambda pattern: lax.bitcast_convert_type(pattern, F32)

    def thr_body(t, pattern):
        cand = pattern | jnp.left_shift(jnp.int32(1), 30 - t)
        return jnp.where(count(aff >= as_float(cand)) >= cap, cand, pattern)

    thr = as_float(lax.fori_loop(0, 31, thr_body, jnp.zeros((E, 1), jnp.int32)))
    above = aff > thr
    tie = aff == thr
    need = cap - count(above)

    pos = lax.broadcasted_iota(jnp.int32, (E, seq), 1)

    def tie_body(t, last):
        cand = last | jnp.left_shift(jnp.int32(1), (seq.bit_length() - 2) - t)
        return jnp.where(count(tie & (pos < cand)) < need, cand, last)

    last = lax.fori_loop(0, seq.bit_length() - 1, tie_body, jnp.zeros((E, 1), jnp.int32))
    sel = (above | (tie & (pos <= last))).astype(BF16)

    nt = seq // LANES
    lrow = lax.broadcasted_iota(jnp.int32, (LANES, LANES), 0)
    lcol = lax.broadcasted_iota(jnp.int32, (LANES, LANES), 1)
    tri = (lrow <= lcol).astype(BF16)
    mm = functools.partial(jnp.dot, preferred_element_type=F32)
    for t in range(nt):
        cum_ref[t * E:(t + 1) * E, :] = mm(sel[:, t * LANES:(t + 1) * LANES], tri)
    tile_of = (lax.broadcasted_iota(jnp.int32, (seq, LANES), 0) // LANES
               == lax.broadcasted_iota(jnp.int32, (seq, LANES), 1)).astype(BF16)
    per_tile = mm(sel, tile_of)
    lane = lax.broadcasted_iota(jnp.int32, (1, LANES), 1)
    far = jnp.float32(2 * seq)
    t_end = jnp.where(lane < nt, mm(per_tile.astype(BF16), tri), far)
    t_start = jnp.where(lane < nt, t_end - per_tile, far)
    pad = jnp.zeros((LANES - E, LANES), F32)
    t_start_cols = jnp.concatenate([jnp.where(lane < nt, t_start, 0.0), pad], axis=0).T

    slot = lax.broadcasted_iota(jnp.int32, (cap, LANES), 0).astype(F32)
    ones = jnp.ones((LANES, LANES), BF16)
    zrows = jnp.zeros((LANES - nt, 2 * LANES), F32)
    for e0 in range(0, E, ROUTE_WAYS):
        es = range(e0, e0 + ROUTE_WAYS)
        tiles, picks, whole = [], [], []
        for e in es:
            absc = cum_ref[pl.ds(e, nt, stride=E), :] + t_start_cols[0:nt, e:e + 1]
            hi = jnp.where(absc >= 256.0, 1.0, 0.0) + jnp.where(absc >= 512.0, 1.0, 0.0)
            lo = absc - 256.0 * hi
            tiles.append(jnp.concatenate([jnp.concatenate([lo, hi], axis=1), zrows], axis=0).astype(BF16))
            done = jnp.where(t_end[e:e + 1] <= slot, 1.0, 0.0)
            whole.append(done)
            picks.append((jnp.where(t_start[e:e + 1] <= slot, 1.0, 0.0) - done).astype(BF16))
        rows = [mm(p, w) for p, w in zip(picks, tiles)]
        votes = [(jnp.where(r[:, :LANES] + 256.0 * r[:, LANES:] <= slot, 1.0, 0.0) + float(LANES) * d).astype(BF16)
                 for r, d in zip(rows, whole)]
        for e, v in zip(es, votes):
            idx_ref[0, :, e:e + 1] = mm(v, ones)[:, e:e + 1].astype(jnp.int32)


def _route(aff_t, batch, cap):
    E, T = aff_t.shape
    B, S = batch, T // batch
    return pl.pallas_call(
        functools.partial(_route_kernel, cap=cap, seq=S),
        grid=(B,),
        in_specs=[pl.BlockSpec((E, S), lambda b: (0, b))],
        out_specs=pl.BlockSpec((1, cap, E), lambda b: (b, 0, 0)),
        out_shape=jax.ShapeDtypeStruct((B, cap, E), jnp.int32),
        scratch_shapes=[pltpu.VMEM((S // LANES * E, LANES), F32)],
        compiler_params=_cparams(("parallel",), 32),
        name="route",
    )(aff_t)


def _gather_kernel(idx_ref, xn_ref, aff_ref, xg_ref, wg_ref, ws_ref, *, cap):
    e = pl.program_id(1)

    for i in range(cap):
        t = idx_ref[0, 0, i]
        src = pl.multiple_of(t * SUBLANES, SUBLANES)
        xg_ref[0, 0, i * SUBLANES:(i + 1) * SUBLANES, :] = xn_ref[0, pl.ds(src, SUBLANES), :]
        ws_ref[i:i + 1, :] = aff_ref[0, pl.ds(t, 1), :]
    lane = lax.broadcasted_iota(jnp.int32, (cap, N_EXPERTS), 1)
    wg_ref[0, 0] = jnp.sum(jnp.where(lane == e, ws_ref[...], 0.0), axis=1, keepdims=True)


def _gather(idx, xn_tiles, aff, cap):
    B, S, E = aff.shape
    return pl.pallas_call(
        functools.partial(_gather_kernel, cap=cap),
        grid=(B, E),
        in_specs=[pl.BlockSpec((1, 1, cap), lambda b, e: (b * E + e, 0, 0), memory_space=pltpu.SMEM),
                  pl.BlockSpec((1, S * SUBLANES, LANES), lambda b, e: (b, 0, 0)),
                  pl.BlockSpec((1, S, E), lambda b, e: (b, 0, 0))],
        out_specs=[pl.BlockSpec((1, 1, cap * SUBLANES, LANES), lambda b, e: (b, e, 0, 0)),
                   pl.BlockSpec((1, 1, cap, 1), lambda b, e: (b, e, 0, 0))],
        out_shape=[jax.ShapeDtypeStruct((B, E, cap * SUBLANES, LANES), F32),
                   jax.ShapeDtypeStruct((B, E, cap, 1), F32)],
        scratch_shapes=[pltpu.VMEM((cap, E), F32)],
        compiler_params=_cparams(("arbitrary", "arbitrary"), 48),
        name="gather",
    )(idx.reshape(B * E, 1, cap), xn_tiles, aff)


def _ffn_kernel(xg_ref, wg_ref, w1_ref, w2_ref, w3_ref, y_ref, b1_ref, b2_ref, b3_ref):
    @pl.when(pl.program_id(1) == 0)
    def _():
        b1_ref[...] = w1_ref[0].astype(BF16)
        b2_ref[...] = w2_ref[0].astype(BF16)
        b3_ref[...] = w3_ref[0].astype(BF16)

    rows = wg_ref.shape[2]
    seqs = range(xg_ref.shape[0])
    mm = functools.partial(jnp.dot, preferred_element_type=F32)
    xgs = [_load_token_tiles(xg_ref.at[i, 0], rows).astype(BF16) for i in seqs]
    gates = [mm(xg, b1_ref[...]) for xg in xgs]
    ups = [mm(xg, b2_ref[...]) for xg in xgs]
    hids = [(gate * _sigmoid(gate) * up).astype(BF16) for gate, up in zip(gates, ups)]
    ys = [mm(hid, b3_ref[...]) * wg_ref[i, 0] for i, hid in zip(seqs, hids)]
    for i, y in zip(seqs, ys):
        _store_token_tiles(y_ref.at[i, 0], y)


FFN_SEQS_PER_STEP = 2


def _ffn(xg, wg, w1, w2, w3):
    B, E, C, _ = wg.shape
    _, D, F = w1.shape
    n = FFN_SEQS_PER_STEP
    return pl.pallas_call(
        _ffn_kernel,
        grid=(E, B // n),
        in_specs=[pl.BlockSpec((n, 1, C * ROW_TILES, LANES), lambda e, b: (b, e, 0, 0)),
                  pl.BlockSpec((n, 1, C, 1), lambda e, b: (b, e, 0, 0)),
                  pl.BlockSpec((1, D, F), lambda e, b: (e, 0, 0)),
                  pl.BlockSpec((1, D, F), lambda e, b: (e, 0, 0)),
                  pl.BlockSpec((1, F, D), lambda e, b: (e, 0, 0))],
        out_specs=pl.BlockSpec((n, 1, C * ROW_TILES, LANES), lambda e, b: (b, e, 0, 0)),
        out_shape=jax.ShapeDtypeStruct((B, E, C * ROW_TILES, LANES), F32),
        scratch_shapes=[pltpu.VMEM((D, F), BF16), pltpu.VMEM((D, F), BF16), pltpu.VMEM((F, D), BF16)],
        compiler_params=_cparams(("arbitrary", "arbitrary"), 60),
        name="ffn",
    )(xg, wg, w1, w2, w3)


SCATTER_BATCH = 16


def _scatter_kernel(idx_ref, y_ref, o_ref, *, cap):
    @pl.when(pl.program_id(1) == 0)
    def _():
        o_ref[...] = jnp.zeros_like(o_ref)

    for i0 in range(0, cap, SCATTER_BATCH):
        slots = range(i0, i0 + SCATTER_BATCH)
        rows = [pl.ds(pl.multiple_of(idx_ref[0, 0, i] * SUBLANES, SUBLANES), SUBLANES) for i in slots]
        new = [o_ref[0, r, :] + y_ref[0, 0, i * SUBLANES:(i + 1) * SUBLANES, :] for r, i in zip(rows, slots)]
        for r, v in zip(rows, new):
            o_ref[0, r, :] = v


def _scatter(idx, y_tiles, seq):
    B, E, cap = idx.shape
    return pl.pallas_call(
        functools.partial(_scatter_kernel, cap=cap),
        grid=(B, E),
        in_specs=[pl.BlockSpec((1, 1, cap), lambda b, e: (b * E + e, 0, 0), memory_space=pltpu.SMEM),
                  pl.BlockSpec((1, 1, cap * SUBLANES, LANES), lambda b, e: (b, e, 0, 0))],
        out_specs=pl.BlockSpec((1, seq * SUBLANES, LANES), lambda b, e: (b, 0, 0)),
        out_shape=jax.ShapeDtypeStruct((B, seq * SUBLANES, LANES), F32),
        compiler_params=_cparams(("arbitrary", "arbitrary"), 48),
        name="scatter",
    )(idx.reshape(B * E, 1, cap), y_tiles)


def _ple_out_kernel(h_ref, moe_ref, p_ref, gple_ref, wpg_ref, wple_ref, gfin_ref, o_ref):
    sub, starts = _sub_blocks(h_ref.shape[0])
    mm = functools.partial(jnp.dot, preferred_element_type=F32)
    hs = [h_ref[r0:r0 + sub, :] + _load_token_tiles(moe_ref, sub, r0) for r0 in starts]
    ns = [_rms(h, gple_ref[...]).astype(BF16) for h in hs]
    gates = [_sigmoid(mm(n, wpg_ref[...])) for n in ns]
    embs = [mm(p_ref[r0:r0 + sub, :].astype(BF16), wple_ref[...]) for r0 in starts]
    for r0, h, gate, emb in zip(starts, hs, gates, embs):
        o_ref[r0:r0 + sub, :] = _rms(h + gate * emb, gfin_ref[...])


def _ple_out(h, moe, p2, gple, wpg, wple, gfin, tm):
    T = h.shape[0]
    row = lambda n: pl.BlockSpec((tm, n), lambda i: (i, 0))
    return pl.pallas_call(
        _ple_out_kernel,
        grid=(T // tm,),
        in_specs=[row(D_MODEL), pl.BlockSpec((tm * ROW_TILES, LANES), lambda i: (i, 0)), row(PLE_DIM),
                  _full(gple.shape), _full(wpg.shape), _full(wple.shape), _full(gfin.shape)],
        out_specs=row(D_MODEL),
        out_shape=jax.ShapeDtypeStruct((T, D_MODEL), F32),
        compiler_params=_cparams(("parallel",), 48),
        name="ple_out",
    )(h, moe, p2, gple, wpg, wple, gfin)


def kernel(x, p, positions, norm_mix, w_in, gla_gate_up_fwd, gla_gate_bias_fwd, gla_gate_up_bwd, gla_gate_bias_bwd, attn_sink, gla_norm, w_branch_attn, w_branch_gla, w_out, norm_ffn, w_router, w_exp_gate, w_exp_up, w_exp_down, norm_ple, w_ple_gate, w_ple, norm_final):
    B, S, D = x.shape
    T = B * S
    depth = w_in.shape[0]
    cap = CAPACITY_FACTOR * S // N_EXPERTS
    R = GLA_GATE_RANK

    posr = positions.reshape(1, T)
    inv_freq = ROPE_THETA ** (-jnp.arange(0, ROPE_DIM, 2, dtype=F32) / ROPE_DIM)
    invfc = inv_freq.reshape(ROPE_DIM // 2, 1)

    h = x.reshape(T, D)
    for l in range(depth):
        o = 0
        cols = {}
        for name, n in (("q", ATT_Q_DIM), ("k", ATT_KV_DIM), ("v", ATT_KV_DIM), ("gqk", 2 * GLA_KEY_DIM),
                        ("gv", GLA_VAL_DIM), ("gr", GLA_VAL_DIM), ("z", 2 * R), ("gate", 2 * D_MODEL)):
            cols[name] = w_in[l][:, o:o + n].astype(BF16)
            o += n
        per_head = lambda w: w.reshape(-1, GLA_HEADS, GLA_DK).swapaxes(0, 1)
        upf, upb = per_head(gla_gate_up_fwd[l]), per_head(gla_gate_up_bwd[l])
        up = jnp.concatenate([jnp.concatenate([upf, jnp.zeros_like(upf)], axis=2),
                              jnp.concatenate([jnp.zeros_like(upb), upb], axis=2)], axis=1)
        up_hi = up.astype(BF16)
        up_lo = (up - up_hi.astype(F32)).astype(BF16)
        upw = jnp.concatenate([up_hi, up_hi, up_lo, jnp.zeros_like(up_lo)], axis=1)
        wz4 = jnp.tile(cols["z"], (1, LANES // (2 * R)))
        gbias = jnp.concatenate([per_head(gla_gate_bias_fwd[l]), per_head(gla_gate_bias_bwd[l])], axis=2)
        wr = w_router[l]
        wr_hi = wr.astype(BF16)
        wr_lo = (wr - wr_hi.astype(F32)).astype(BF16)
        wr2 = jnp.concatenate([jnp.concatenate([wr_hi, wr_lo], axis=1),
                               jnp.concatenate([wr_hi, jnp.zeros_like(wr_lo)], axis=1)], axis=0).T

        qt, k0, k1, vt, gq, gk, gv, gr, z, sga, sgg = _in_proj(
            h, posr, invfc, norm_mix[l].reshape(1, D), cols["q"].T,
            jnp.concatenate([cols["k"], wz4], axis=1), cols["v"].T,
            cols["gqk"], cols["gv"], cols["gr"], cols["gate"], tm=512)

        att = _swa(attn_sink[l], qt, k0.reshape(B, S, -1), k1.reshape(B, S, -1), vt, batch=B, tq=1024)
        gla = _gla(z.reshape(B, S, -1), gq.reshape(B, S, -1), gk.reshape(B, S, -1), gv.reshape(B, S, -1),
                   gr.reshape(B, S, -1), upw, gbias, gla_norm[l].reshape(1, -1))

        h1, xn, aff, aff_t = _mix_out(h, att.reshape(T, -1), gla.reshape(T, -1), sga, sgg,
                               w_branch_attn[l].astype(BF16), w_branch_gla[l].astype(BF16),
                               w_out[l].astype(BF16), norm_ffn[l].reshape(1, D), wr2, tm=512)

        aff3 = aff.reshape(B, S, N_EXPERTS)
        idx = _route(aff_t, B, cap)
        idx = jnp.swapaxes(idx, 1, 2)
        xg, wg = _gather(idx, xn.reshape(B, S * SUBLANES, LANES), aff3, cap)
        y = _ffn(xg, wg, w_exp_gate[l], w_exp_up[l], w_exp_down[l])
        moe = _scatter(idx, y, S)

        last = l == depth - 1
        gfin = norm_final.reshape(1, D)
        assert last, "the final norm is fused into the last layer's PLE kernel"
        h = _ple_out(h1, moe.reshape(T * ROW_TILES, LANES), p[l].reshape(T, PLE_DIM), norm_ple[l].reshape(1, D),
                     w_ple_gate[l].astype(BF16), w_ple[l].astype(BF16), gfin, tm=1024)
    return h.reshape(B, S, D)
```

```python
import functools
import math

import jax
import jax.numpy as jnp
from jax import lax
from jax.experimental import pallas as pl
from jax.experimental.pallas import tpu as pltpu

D_MODEL = 1024
ATT_HEADS = 8
ATT_KV_HEADS = 2
ATT_HEAD_DIM = 64
ATT_GROUP = ATT_HEADS // ATT_KV_HEADS
ATT_Q_DIM = ATT_HEADS * ATT_HEAD_DIM
ATT_KV_DIM = ATT_KV_HEADS * ATT_HEAD_DIM
WINDOW = 128
ROPE_DIM = ATT_HEAD_DIM // 4
ROPE_THETA = 500000.0
GLA_HEADS = 4
GLA_KEY_DIM = D_MODEL // 2
GLA_VAL_DIM = D_MODEL
GLA_DK = GLA_KEY_DIM // GLA_HEADS
GLA_DV = GLA_VAL_DIM // GLA_HEADS
GLA_GATE_RANK = 16
GLA_GATE_NORM = 16.0
GLA_CHUNK = 64
N_EXPERTS = 16
EXPERT_FF = D_MODEL
CAPACITY_FACTOR = 2
PLE_DIM = 256
EPS = 1e-6

LANES = 128
MIB = 1024 * 1024
BF16 = jnp.bfloat16
F32 = jnp.float32
LOG2E = math.log2(math.e)

NT_DIMS = (((1,), (1,)), ((), ()))
TN_DIMS = (((0,), (0,)), ((), ()))


def _cparams(sem, vmem_mib):
    return pltpu.CompilerParams(dimension_semantics=sem, vmem_limit_bytes=vmem_mib * MIB)


def _full(shape):
    n = len(shape)
    return pl.BlockSpec(shape, lambda *_: (0,) * n)


def _rms(x, gain):
    ms = jnp.mean(x * x, axis=-1, keepdims=True)
    return x * lax.rsqrt(ms + EPS) * gain


def _sigmoid(x):
    return 0.5 * jnp.tanh(0.5 * x) + 0.5


SUBLANES = 8
ROW_TILES = D_MODEL // LANES


def _store_token_tiles(ref2d, x, first_row=0):
    rows = x.shape[0]
    for j in range(ROW_TILES):
        ref2d[pl.ds(first_row * ROW_TILES + j, rows, stride=ROW_TILES), :] = x[:, j * LANES:(j + 1) * LANES]


def _load_token_tiles(ref2d, rows, first_row=0):
    return jnp.concatenate([ref2d[pl.ds(first_row * ROW_TILES + j, rows, stride=ROW_TILES), :]
                            for j in range(ROW_TILES)], axis=1)


SUB_ROWS = 256


def _sub_blocks(tile_rows):
    return SUB_ROWS, list(range(0, tile_rows, SUB_ROWS))


def _rope_rows(t, cos_r, sin_r, heads):
    half = ROPE_DIM // 2
    rows = []
    for h in range(heads):
        r0 = h * ATT_HEAD_DIM
        t1, t2 = t[r0:r0 + half], t[r0 + half:r0 + ROPE_DIM]
        rows += [t1 * cos_r - t2 * sin_r, t2 * cos_r + t1 * sin_r, t[r0 + ROPE_DIM:r0 + ATT_HEAD_DIM]]
    return jnp.concatenate(rows, axis=0)


def _in_proj_kernel(x_ref, posr_ref, invfc_ref, gain_ref, wqt_ref, wkz_ref, wvt_ref,
                    wgqk_ref, wgv_ref, wgr_ref, wgate_ref,
                    qt_ref, k0_ref, k1_ref, vt_ref, gq_ref, gk_ref, gv_ref, gr_ref, z_ref, sga_ref, sgg_ref):
    a = _rms(x_ref[...], gain_ref[...]).astype(BF16)
    ang_t = invfc_ref[...] * posr_ref[...].astype(F32)
    cos_r, sin_r = jnp.cos(ang_t), jnp.sin(ang_t)

    qt = lax.dot_general(wqt_ref[...], a, NT_DIMS, preferred_element_type=F32)
    qt_ref[...] = (_rope_rows(qt, cos_r, sin_r, ATT_HEADS) * (ATT_HEAD_DIM ** -0.5 * LOG2E)).astype(BF16)

    kz = jnp.dot(a, wkz_ref[...], preferred_element_type=F32)
    z_ref[...] = kz[:, ATT_KV_DIM:]
    k = _rope_rows(kz[:, :ATT_KV_DIM].T, cos_r, sin_r, ATT_KV_HEADS).T.astype(BF16)
    k0_ref[...] = k[:, :ATT_HEAD_DIM]
    k1_ref[...] = k[:, ATT_HEAD_DIM:]
    vt_ref[...] = lax.dot_general(wvt_ref[...], a, NT_DIMS, preferred_element_type=F32).astype(BF16)

    gqk = jnp.dot(a, wgqk_ref[...], preferred_element_type=F32)
    gq_ref[...] = (gqk[:, :GLA_KEY_DIM] * (GLA_DK ** -0.5)).astype(BF16)
    gk_ref[...] = gqk[:, GLA_KEY_DIM:].astype(BF16)
    gv_ref[...] = jnp.dot(a, wgv_ref[...], preferred_element_type=F32).astype(BF16)
    gr = jnp.dot(a, wgr_ref[...], preferred_element_type=F32)
    gr_ref[...] = (gr * _sigmoid(gr)).astype(BF16)
    gates = jnp.dot(a, wgate_ref[...], preferred_element_type=F32)
    sga_ref[...] = _sigmoid(gates[:, :D_MODEL]).astype(BF16)
    sgg_ref[...] = _sigmoid(gates[:, D_MODEL:]).astype(BF16)


def _in_proj(x2, posr, invfc, gain, wqt, wkz, wvt, wgqk, wgv, wgr, wgate, tm):
    T = x2.shape[0]
    row = lambda n: pl.BlockSpec((tm, n), lambda i: (i, 0))
    col = lambda n: pl.BlockSpec((n, tm), lambda i: (0, i))
    row_widths = (ATT_HEAD_DIM, ATT_HEAD_DIM, None, GLA_KEY_DIM, GLA_KEY_DIM, GLA_VAL_DIM,
                  GLA_VAL_DIM, wkz.shape[1] - ATT_KV_DIM, D_MODEL, D_MODEL)
    row_dtypes = (BF16,) * 7 + (F32, BF16, BF16)
    out_specs = [col(ATT_Q_DIM)]
    out_shape = [jax.ShapeDtypeStruct((ATT_Q_DIM, T), BF16)]
    for n, dt in zip(row_widths, row_dtypes):
        if n is None:
            out_specs.append(col(ATT_KV_DIM))
            out_shape.append(jax.ShapeDtypeStruct((ATT_KV_DIM, T), BF16))
        else:
            out_specs.append(row(n))
            out_shape.append(jax.ShapeDtypeStruct((T, n), dt))
    consts = (invfc, gain, wqt, wkz, wvt, wgqk, wgv, wgr, wgate)
    return pl.pallas_call(
        _in_proj_kernel,
        grid=(T // tm,),
        in_specs=[row(D_MODEL), col(1)] + [_full(c.shape) for c in consts],
        out_specs=out_specs,
        out_shape=out_shape,
        compiler_params=_cparams(("parallel",), 56),
        name="in_proj",
    )(x2, posr, *consts)


def _swa_kernel(sink_ref, qt_ref, k0_ref, k1_ref, vt_ref, o_ref, *, tq, seq):
    blk = WINDOW
    span = 3 * blk
    hd = ATT_HEAD_DIM
    n = pl.program_id(1)
    ones = jnp.ones((16, span), BF16)
    kv_refs = (k0_ref, k1_ref)

    def window_start(sb):
        return pl.multiple_of(jnp.clip(n * tq + (sb - 1) * blk, 0, seq - span), blk)

    def scores(sb, g):
        kw = kv_refs[g][0, pl.ds(window_start(sb), span), :]
        heads = range(g * ATT_GROUP, (g + 1) * ATT_GROUP)
        qs = jnp.concatenate([qt_ref[h * hd:(h + 1) * hd, sb * blk:(sb + 1) * blk] for h in heads], axis=1)
        return jnp.dot(kw, qs, preferred_element_type=F32)

    work = [(sb, g) for sb in range(tq // blk) for g in range(ATT_KV_HEADS)]
    s_next = scores(*work[0])
    outs = []
    for step, (sb, g) in enumerate(work):
        s_all = s_next
        if step + 1 < len(work):
            s_next = scores(*work[step + 1])
        q0 = n * tq + sb * blk
        start = window_start(sb)
        kj = start + lax.broadcasted_iota(jnp.int32, (span, blk), 0)
        qi = q0 + lax.broadcasted_iota(jnp.int32, (span, blk), 1)
        valid = jnp.abs(qi - kj) <= WINDOW
        vaug = jnp.concatenate([vt_ref[g * hd:(g + 1) * hd, pl.ds(start, span)], ones], axis=0)
        sinks = [sink_ref[g * ATT_GROUP + i] * LOG2E for i in range(ATT_GROUP)]
        ss = [jnp.where(valid, s_all[:, i * blk:(i + 1) * blk], -jnp.inf) for i in range(ATT_GROUP)]
        ms = [jnp.maximum(jnp.max(s, axis=0, keepdims=True), sink) for s, sink in zip(ss, sinks)]
        rs = [jnp.dot(vaug, jnp.exp2(s - m).astype(BF16), preferred_element_type=F32) for s, m in zip(ss, ms)]
        outs += [r[:hd] / (r[hd:hd + 1] + jnp.exp2(sink - m)) for r, m, sink in zip(rs, ms, sinks)]
        if g == ATT_KV_HEADS - 1:
            for pr in range(ATT_HEADS // 2):
                pair = jnp.concatenate([outs[2 * pr], outs[2 * pr + 1]], axis=0)
                o_ref[0, sb * blk:(sb + 1) * blk, pr * 2 * hd:(pr + 1) * 2 * hd] = pair.T.astype(BF16)
            outs = []


def _swa(sink, qt, k0, k1, vt, batch, tq):
    S = k0.shape[1]
    nq = S // tq
    kspec = pl.BlockSpec((1, S, ATT_HEAD_DIM), lambda b, n: (b, 0, 0))
    return pl.pallas_call(
        functools.partial(_swa_kernel, tq=tq, seq=S),
        grid=(batch, nq),
        in_specs=[pl.BlockSpec(memory_space=pltpu.SMEM),
                  pl.BlockSpec((ATT_Q_DIM, tq), lambda b, n: (0, b * nq + n)),
                  kspec, kspec,
                  pl.BlockSpec((ATT_KV_DIM, S), lambda b, n: (0, b))],
        out_specs=pl.BlockSpec((1, tq, ATT_Q_DIM), lambda b, n: (b, n, 0)),
        out_shape=jax.ShapeDtypeStruct((batch, S, ATT_Q_DIM), BF16),
        compiler_params=_cparams(("parallel", "parallel"), 32),
        name="swa",
    )(sink, qt, k0, k1, vt)


def _log2_sigmoid(u):
    return jnp.minimum(u, 0.0) * LOG2E - jnp.log2(1.0 + jnp.exp2(jnp.abs(u) * -LOG2E))


def _split2(x):
    hi = x.astype(BF16)
    return hi, (x - hi.astype(F32)).astype(BF16)


GLA_WAYS = 4


def _gla_kernel(z_ref, q_ref, k_ref, v_ref, r_ref, upw_ref, bias_ref, gain_ref,
                o_ref, cf_ref, cb_ref, kef_ref, keb_ref, st_ref, s_ref, *, seq):
    L = GLA_CHUNK
    R2 = 2 * GLA_GATE_RANK
    nc = seq // L
    grp = 4 * L
    cpg = grp // L
    dk = GLA_DK
    mm = functools.partial(jnp.dot, preferred_element_type=F32)
    nt = functools.partial(lax.dot_general, dimension_numbers=NT_DIMS, preferred_element_type=F32)

    row = lax.broadcasted_iota(jnp.int32, (grp, grp), 0)
    col = lax.broadcasted_iota(jnp.int32, (grp, grp), 1)
    same = (row // L) == (col // L)
    fwd_mask = same & (col <= row)
    bwd_mask = same & (col > row)
    tri_lo = jnp.where(fwd_mask, 1.0, 0.0).astype(BF16)
    tri_up = jnp.where(same & (col >= row), 1.0, 0.0).astype(BF16)
    lane = lax.broadcasted_iota(jnp.int32, (1, LANES), 1)
    use_lo = (lane >= R2) & (lane < 2 * R2)

    def group_starts(i):
        return [pl.multiple_of((i * GLA_WAYS + w) * grp, grp) for w in range(GLA_WAYS)]

    def cum_body(i, carry):
        r0s = group_starts(i)
        zs = [_split2(z_ref[0, pl.ds(r0, grp), :]) for r0 in r0s]
        us = [mm(jnp.where(use_lo, zl, zh), upw_ref[...]) + bias_ref[...] for zh, zl in zs]
        las = [_split2(_log2_sigmoid(u) * (1.0 / GLA_GATE_NORM)) for u in us]
        cfxs = [mm(tri_lo, jnp.concatenate([lh[:, :dk], ll[:, :dk]], axis=1)) for lh, ll in las]
        cbxs = [mm(tri_up, jnp.concatenate([lh[:, dk:], ll[:, dk:]], axis=1)) for lh, ll in las]
        for r0, cfx, cbx in zip(r0s, cfxs, cbxs):
            cf = cfx[:, :dk] + cfx[:, dk:]
            cb = cbx[:, :dk] + cbx[:, dk:]
            cf_ref[pl.ds(r0, grp), :] = cf
            cb_ref[pl.ds(r0, grp), :] = cb
            k = k_ref[0, pl.ds(r0, grp), :].astype(F32)
            for c in range(cpg):
                sl = slice(c * L, (c + 1) * L)
                gf = cf[(c + 1) * L - 1:(c + 1) * L]
                gb = cb[c * L:c * L + 1]
                kef_ref[pl.ds(r0 + c * L, L), :] = (k[sl] * jnp.exp2(gf - cf[sl])).astype(BF16)
                keb_ref[pl.ds(r0 + c * L, L), :] = (k[sl] * jnp.exp2(gb - cb[sl])).astype(BF16)
        return carry

    lax.fori_loop(0, seq // (grp * GLA_WAYS), cum_body, 0, unroll=True)

    s_ref[...] = jnp.zeros_like(s_ref)
    zero_k = jnp.zeros((L, dk), BF16)

    def state_body(i, carry):
        j = nc - 1 - i
        rf = pl.multiple_of(i * L, L)
        rb = pl.multiple_of(j * L, L)
        vcat = jnp.concatenate([v_ref[0, pl.ds(rf, L), :], v_ref[0, pl.ds(rb, L), :]], axis=0)
        kblk = jnp.concatenate([jnp.concatenate([kef_ref[pl.ds(rf, L), :], zero_k], axis=1),
                                jnp.concatenate([zero_k, keb_ref[pl.ds(rb, L), :]], axis=1)], axis=0)
        kv = lax.dot_general(vcat, kblk, TN_DIMS, preferred_element_type=F32)
        decay = jnp.exp2(jnp.concatenate([cf_ref[pl.ds(rf + L - 1, 1), :], cb_ref[pl.ds(rb, 1), :]], axis=1))
        s = s_ref[...]
        st_ref[i, :, 0:dk] = s[:, :dk].astype(BF16)
        st_ref[j, :, dk:2 * dk] = s[:, dk:].astype(BF16)
        s_ref[...] = s * decay + kv
        return carry

    lax.fori_loop(0, nc, state_body, 0, unroll=True)

    def out_body(i, carry):
        r0s = group_starts(i)
        ops = []
        for r0 in r0s:
            q = q_ref[0, pl.ds(r0, grp), :].astype(F32)
            k = k_ref[0, pl.ds(r0, grp), :].astype(F32)
            cf = cf_ref[pl.ds(r0, grp), :]
            cb = cb_ref[pl.ds(r0, grp), :]
            ops.append(((q * jnp.exp2(cf)).astype(BF16), (k * jnp.exp2(-cf)).astype(BF16),
                        (q * jnp.exp2(cb)).astype(BF16), (k * jnp.exp2(-cb)).astype(BF16)))
        scores = [(nt(qf, kf), nt(qb, kb)) for qf, kf, qb, kb in ops]
        attns = [jnp.where(fwd_mask, af, jnp.where(bwd_mask, ab, 0.0)).astype(BF16) for af, ab in scores]
        outs = []
        for r0, attn, (qf, _, qb, _) in zip(r0s, attns, ops):
            c0 = r0 // L
            qcat = jnp.concatenate([qf, qb], axis=1)
            inter = jnp.concatenate([nt(qcat[c * L:(c + 1) * L], st_ref[c0 + c]) for c in range(cpg)], axis=0)
            outs.append(mm(attn, v_ref[0, pl.ds(r0, grp), :]) + inter)
        for r0, o in zip(r0s, outs):
            o_ref[0, pl.ds(r0, grp), :] = (_rms(o, gain_ref[...]) * r_ref[0, pl.ds(r0, grp), :].astype(F32)).astype(BF16)
        return carry

    lax.fori_loop(0, seq // (grp * GLA_WAYS), out_body, 0, unroll=True)


def _gla(z, gq, gk, gv, gr, upw, bias, gain):
    B, S, _ = gq.shape
    nc = S // GLA_CHUNK
    seq_blk = lambda n: pl.BlockSpec((1, S, n), lambda b, h: (b, 0, h))
    head_blk = lambda r, n: pl.BlockSpec((None, r, n), lambda b, h: (h, 0, 0))
    return pl.pallas_call(
        functools.partial(_gla_kernel, seq=S),
        grid=(B, GLA_HEADS),
        in_specs=[pl.BlockSpec((1, S, LANES), lambda b, h: (b, 0, 0)),
                  seq_blk(GLA_DK), seq_blk(GLA_DK), seq_blk(GLA_DV), seq_blk(GLA_DV),
                  head_blk(LANES, 2 * GLA_DK), head_blk(1, 2 * GLA_DK),
                  pl.BlockSpec((1, GLA_DV), lambda b, h: (0, h))],
        out_specs=seq_blk(GLA_DV),
        out_shape=jax.ShapeDtypeStruct((B, S, GLA_VAL_DIM), BF16),
        scratch_shapes=[pltpu.VMEM((S, GLA_DK), F32), pltpu.VMEM((S, GLA_DK), F32),
                        pltpu.VMEM((S, GLA_DK), BF16), pltpu.VMEM((S, GLA_DK), BF16),
                        pltpu.VMEM((nc, GLA_DV, 2 * GLA_DK), BF16),
                        pltpu.VMEM((GLA_DV, 2 * GLA_DK), F32)],
        compiler_params=_cparams(("parallel", "parallel"), 48),
        name="gla",
    )(z, gq, gk, gv, gr, upw, bias, gain)


def _mix_out_kernel(x_ref, a_ref, g_ref, sga_ref, sgg_ref, wa_ref, wb_ref, wo_ref, gain_ref, wr_ref,
                    h_ref, xn_ref, aff_ref, afft_ref):
    sub, starts = _sub_blocks(x_ref.shape[0])
    mm = functools.partial(jnp.dot, preferred_element_type=F32)
    blk = lambda ref, r0: ref[r0:r0 + sub, :]
    pad = jnp.zeros((LANES - N_EXPERTS, sub), F32)

    y_att = [mm(blk(a_ref, r0), wa_ref[...]) for r0 in starts]
    y_gla = [mm(blk(g_ref, r0), wb_ref[...]) for r0 in starts]
    merged = [(blk(sga_ref, r0).astype(F32) * ya + blk(sgg_ref, r0).astype(F32) * yg).astype(BF16)
              for r0, ya, yg in zip(starts, y_att, y_gla)]
    hs = [blk(x_ref, r0) + mm(m, wo_ref[...]) for r0, m in zip(starts, merged)]
    xns = [_rms(h, gain_ref[...]) for h in hs]
    for r0, h, xn in zip(starts, hs, xns):
        h_ref[r0:r0 + sub, :] = h
        _store_token_tiles(xn_ref, xn, r0)
    his = [xn.astype(BF16) for xn in xns]
    parts = [lax.dot_general(wr_ref[...], jnp.concatenate([hi, (xn - hi.astype(F32)).astype(BF16)], axis=1),
                             NT_DIMS, preferred_element_type=F32) for xn, hi in zip(xns, his)]
    for r0, part in zip(starts, parts):
        logits = part[:N_EXPERTS] + part[N_EXPERTS:]
        e = jnp.exp(logits - jnp.max(logits, axis=0, keepdims=True))
        aff_t = e / jnp.sum(e, axis=0, keepdims=True)
        afft_ref[:, r0:r0 + sub] = aff_t
        aff_ref[r0:r0 + sub, :] = jnp.concatenate([aff_t, pad], axis=0).T[:, :N_EXPERTS]


def _mix_out(x2, a, g, sga, sgg, wa, wb, wo, gain, wr, tm):
    T = x2.shape[0]
    row = lambda n: pl.BlockSpec((tm, n), lambda i: (i, 0))
    return pl.pallas_call(
        _mix_out_kernel,
        grid=(T // tm,),
        in_specs=[row(D_MODEL), row(ATT_Q_DIM), row(GLA_VAL_DIM), row(D_MODEL), row(D_MODEL),
                  _full(wa.shape), _full(wb.shape), _full(wo.shape), _full(gain.shape), _full(wr.shape)],
        out_specs=[row(D_MODEL), pl.BlockSpec((tm * ROW_TILES, LANES), lambda i: (i, 0)), row(N_EXPERTS),
                   pl.BlockSpec((N_EXPERTS, tm), lambda i: (0, i))],
        out_shape=[jax.ShapeDtypeStruct((T, D_MODEL), F32), jax.ShapeDtypeStruct((T * ROW_TILES, LANES), F32),
                   jax.ShapeDtypeStruct((T, N_EXPERTS), F32), jax.ShapeDtypeStruct((N_EXPERTS, T), F32)],
        compiler_params=_cparams(("parallel",), 48),
        name="mix_out",
    )(x2, a, g, sga, sgg, wa, wb, wo, gain, wr)


ROUTE_WAYS = 4


def _route_kernel(aff_ref, idx_ref, cum_ref, *, cap, seq):
    E = N_EXPERTS
    aff = aff_ref[...]
    count = lambda mask: jnp.sum(mask.astype(jnp.int32), axis=1, keepdims=True)
    as_float = lambda pattern: lax.bitcast_convert_type(pattern, F32)

    def thr_body(t, pattern):
        cand = pattern | jnp.left_shift(jnp.int32(1), 30 - t)
        return jnp.where(count(aff >= as_float(cand)) >= cap, cand, pattern)

    thr = as_float(lax.fori_loop(0, 31, thr_body, jnp.zeros((E, 1), jnp.int32)))
    above = aff > thr
    tie = aff == thr
    need = cap - count(above)

    pos = lax.broadcasted_iota(jnp.int32, (E, seq), 1)

    def tie_body(t, last):
        cand = last | jnp.left_shift(jnp.int32(1), (seq.bit_length() - 2) - t)
        return jnp.where(count(tie & (pos < cand)) < need, cand, last)

    last = lax.fori_loop(0, seq.bit_length() - 1, tie_body, jnp.zeros((E, 1), jnp.int32))
    sel = (above | (tie & (pos <= last))).astype(BF16)

    nt = seq // LANES
    lrow = lax.broadcasted_iota(jnp.int32, (LANES, LANES), 0)
    lcol = lax.broadcasted_iota(jnp.int32, (LANES, LANES), 1)
    tri = (lrow <= lcol).astype(BF16)
    mm = functools.partial(jnp.dot, preferred_element_type=F32)
    for t in range(nt):
        cum_ref[t * E:(t + 1) * E, :] = mm(sel[:, t * LANES:(t + 1) * LANES], tri)
    tile_of = (lax.broadcasted_iota(jnp.int32, (seq, LANES), 0) // LANES
               == lax.broadcasted_iota(jnp.int32, (seq, LANES), 1)).astype(BF16)
    per_tile = mm(sel, tile_of)
    lane = lax.broadcasted_iota(jnp.int32, (1, LANES), 1)
    far = jnp.float32(2 * seq)
    t_end = jnp.where(lane < nt, mm(per_tile.astype(BF16), tri), far)
    t_start = jnp.where(lane < nt, t_end - per_tile, far)
    pad = jnp.zeros((LANES - E, LANES), F32)
    t_start_cols = jnp.concatenate([jnp.where(lane < nt, t_start, 0.0), pad], axis=0).T

    slot = lax.broadcasted_iota(jnp.int32, (cap, LANES), 0).astype(F32)
    ones = jnp.ones((LANES, LANES), BF16)
    zrows = jnp.zeros((LANES - nt, 2 * LANES), F32)
    for e0 in range(0, E, ROUTE_WAYS):
        es = range(e0, e0 + ROUTE_WAYS)
        tiles, picks, whole = [], [], []
        for e in es:
            absc = cum_ref[pl.ds(e, nt, stride=E), :] + t_start_cols[0:nt, e:e + 1]
            hi = jnp.where(absc >= 256.0, 1.0, 0.0) + jnp.where(absc >= 512.0, 1.0, 0.0)
            lo = absc - 256.0 * hi
            tiles.append(jnp.concatenate([jnp.concatenate([lo, hi], axis=1), zrows], axis=0).astype(BF16))
            done = jnp.where(t_end[e:e + 1] <= slot, 1.0, 0.0)
            whole.append(done)
            picks.append((jnp.where(t_start[e:e + 1] <= slot, 1.0, 0.0) - done).astype(BF16))
        rows = [mm(p, w) for p, w in zip(picks, tiles)]
        votes = [(jnp.where(r[:, :LANES] + 256.0 * r[:, LANES:] <= slot, 1.0, 0.0) + float(LANES) * d).astype(BF16)
                 for r, d in zip(rows, whole)]
        for e, v in zip(es, votes):
            idx_ref[0, :, e:e + 1] = mm(v, ones)[:, e:e + 1].astype(jnp.int32)


def _route(aff_t, batch, cap):
    E, T = aff_t.shape
    B, S = batch, T // batch
    return pl.pallas_call(
        functools.partial(_route_kernel, cap=cap, seq=S),
        grid=(B,),
        in_specs=[pl.BlockSpec((E, S), lambda b: (0, b))],
        out_specs=pl.BlockSpec((1, cap, E), lambda b: (b, 0, 0)),
        out_shape=jax.ShapeDtypeStruct((B, cap, E), jnp.int32),
        scratch_shapes=[pltpu.VMEM((S // LANES * E, LANES), F32)],
        compiler_params=_cparams(("parallel",), 32),
        name="route",
    )(aff_t)


def _gather_kernel(idx_ref, xn_ref, aff_ref, xg_ref, wg_ref, ws_ref, *, cap):
    e = pl.program_id(1)

    for i in range(cap):
        t = idx_ref[0, 0, i]
        src = pl.multiple_of(t * SUBLANES, SUBLANES)
        xg_ref[0, 0, i * SUBLANES:(i + 1) * SUBLANES, :] = xn_ref[0, pl.ds(src, SUBLANES), :]
        ws_ref[i:i + 1, :] = aff_ref[0, pl.ds(t, 1), :]
    lane = lax.broadcasted_iota(jnp.int32, (cap, N_EXPERTS), 1)
    wg_ref[0, 0] = jnp.sum(jnp.where(lane == e, ws_ref[...], 0.0), axis=1, keepdims=True)


def _gather(idx, xn_tiles, aff, cap):
    B, S, E = aff.shape
    return pl.pallas_call(
        functools.partial(_gather_kernel, cap=cap),
        grid=(B, E),
        in_specs=[pl.BlockSpec((1, 1, cap), lambda b, e: (b * E + e, 0, 0), memory_space=pltpu.SMEM),
                  pl.BlockSpec((1, S * SUBLANES, LANES), lambda b, e: (b, 0, 0)),
                  pl.BlockSpec((1, S, E), lambda b, e: (b, 0, 0))],
        out_specs=[pl.BlockSpec((1, 1, cap * SUBLANES, LANES), lambda b, e: (b, e, 0, 0)),
                   pl.BlockSpec((1, 1, cap, 1), lambda b, e: (b, e, 0, 0))],
        out_shape=[jax.ShapeDtypeStruct((B, E, cap * SUBLANES, LANES), F32),
                   jax.ShapeDtypeStruct((B, E, cap, 1), F32)],
        scratch_shapes=[pltpu.VMEM((cap, E), F32)],
        compiler_params=_cparams(("arbitrary", "arbitrary"), 48),
        name="gather",
    )(idx.reshape(B * E, 1, cap), xn_tiles, aff)


def _ffn_kernel(xg_ref, wg_ref, w1_ref, w2_ref, w3_ref, y_ref, b1_ref, b2_ref, b3_ref):
    @pl.when(pl.program_id(1) == 0)
    def _():
        b1_ref[...] = w1_ref[0].astype(BF16)
        b2_ref[...] = w2_ref[0].astype(BF16)
        b3_ref[...] = w3_ref[0].astype(BF16)

    rows = wg_ref.shape[2]
    seqs = range(xg_ref.shape[0])
    mm = functools.partial(jnp.dot, preferred_element_type=F32)
    xgs = [_load_token_tiles(xg_ref.at[i, 0], rows).astype(BF16) for i in seqs]
    gates = [mm(xg, b1_ref[...]) for xg in xgs]
    ups = [mm(xg, b2_ref[...]) for xg in xgs]
    hids = [(gate * _sigmoid(gate) * up).astype(BF16) for gate, up in zip(gates, ups)]
    ys = [mm(hid, b3_ref[...]) * wg_ref[i, 0] for i, hid in zip(seqs, hids)]
    for i, y in zip(seqs, ys):
        _store_token_tiles(y_ref.at[i, 0], y)


FFN_SEQS_PER_STEP = 2


def _ffn(xg, wg, w1, w2, w3):
    B, E, C, _ = wg.shape
    _, D, F = w1.shape
    n = FFN_SEQS_PER_STEP
    return pl.pallas_call(
        _ffn_kernel,
        grid=(E, B // n),
        in_specs=[pl.BlockSpec((n, 1, C * ROW_TILES, LANES), lambda e, b: (b, e, 0, 0)),
                  pl.BlockSpec((n, 1, C, 1), lambda e, b: (b, e, 0, 0)),
                  pl.BlockSpec((1, D, F), lambda e, b: (e, 0, 0)),
                  pl.BlockSpec((1, D, F), lambda e, b: (e, 0, 0)),
                  pl.BlockSpec((1, F, D), lambda e, b: (e, 0, 0))],
        out_specs=pl.BlockSpec((n, 1, C * ROW_TILES, LANES), lambda e, b: (b, e, 0, 0)),
        out_shape=jax.ShapeDtypeStruct((B, E, C * ROW_TILES, LANES), F32),
        scratch_shapes=[pltpu.VMEM((D, F), BF16), pltpu.VMEM((D, F), BF16), pltpu.VMEM((F, D), BF16)],
        compiler_params=_cparams(("arbitrary", "arbitrary"), 60),
        name="ffn",
    )(xg, wg, w1, w2, w3)


SCATTER_BATCH = 16


def _scatter_kernel(idx_ref, y_ref, o_ref, *, cap):
    @pl.when(pl.program_id(1) == 0)
    def _():
        o_ref[...] = jnp.zeros_like(o_ref)

    for i0 in range(0, cap, SCATTER_BATCH):
        slots = range(i0, i0 + SCATTER_BATCH)
        rows = [pl.ds(pl.multiple_of(idx_ref[0, 0, i] * SUBLANES, SUBLANES), SUBLANES) for i in slots]
        new = [o_ref[0, r, :] + y_ref[0, 0, i * SUBLANES:(i + 1) * SUBLANES, :] for r, i in zip(rows, slots)]
        for r, v in zip(rows, new):
            o_ref[0, r, :] = v


def _scatter(idx, y_tiles, seq):
    B, E, cap = idx.shape
    return pl.pallas_call(
        functools.partial(_scatter_kernel, cap=cap),
        grid=(B, E),
        in_specs=[pl.BlockSpec((1, 1, cap), lambda b, e: (b * E + e, 0, 0), memory_space=pltpu.SMEM),
                  pl.BlockSpec((1, 1, cap * SUBLANES, LANES), lambda b, e: (b, e, 0, 0))],
        out_specs=pl.BlockSpec((1, seq * SUBLANES, LANES), lambda b, e: (b, 0, 0)),
        out_shape=jax.ShapeDtypeStruct((B, seq * SUBLANES, LANES), F32),
        compiler_params=_cparams(("arbitrary", "arbitrary"), 48),
        name="scatter",
    )(idx.reshape(B * E, 1, cap), y_tiles)


def _ple_out_kernel(h_ref, moe_ref, p_ref, gple_ref, wpg_ref, wple_ref, gfin_ref, o_ref):
    sub, starts = _sub_blocks(h_ref.shape[0])
    mm = functools.partial(jnp.dot, preferred_element_type=F32)
    hs = [h_ref[r0:r0 + sub, :] + _load_token_tiles(moe_ref, sub, r0) for r0 in starts]
    ns = [_rms(h, gple_ref[...]).astype(BF16) for h in hs]
    gates = [_sigmoid(mm(n, wpg_ref[...])) for n in ns]
    embs = [mm(p_ref[r0:r0 + sub, :].astype(BF16), wple_ref[...]) for r0 in starts]
    for r0, h, gate, emb in zip(starts, hs, gates, embs):
        o_ref[r0:r0 + sub, :] = _rms(h + gate * emb, gfin_ref[...])


def _ple_out(h, moe, p2, gple, wpg, wple, gfin, tm):
    T = h.shape[0]
    row = lambda n: pl.BlockSpec((tm, n), lambda i: (i, 0))
    return pl.pallas_call(
        _ple_out_kernel,
        grid=(T // tm,),
        in_specs=[row(D_MODEL), pl.BlockSpec((tm * ROW_TILES, LANES), lambda i: (i, 0)), row(PLE_DIM),
                  _full(gple.shape), _full(wpg.shape), _full(wple.shape), _full(gfin.shape)],
        out_specs=row(D_MODEL),
        out_shape=jax.ShapeDtypeStruct((T, D_MODEL), F32),
        compiler_params=_cparams(("parallel",), 48),
        name="ple_out",
    )(h, moe, p2, gple, wpg, wple, gfin)


def kernel(x, p, positions, norm_mix, w_in, gla_gate_up_fwd, gla_gate_bias_fwd, gla_gate_up_bwd, gla_gate_bias_bwd, attn_sink, gla_norm, w_branch_attn, w_branch_gla, w_out, norm_ffn, w_router, w_exp_gate, w_exp_up, w_exp_down, norm_ple, w_ple_gate, w_ple, norm_final):
    B, S, D = x.shape
    T = B * S
    depth = w_in.shape[0]
    cap = CAPACITY_FACTOR * S // N_EXPERTS
    R = GLA_GATE_RANK

    posr = positions.reshape(1, T)
    inv_freq = ROPE_THETA ** (-jnp.arange(0, ROPE_DIM, 2, dtype=F32) / ROPE_DIM)
    invfc = inv_freq.reshape(ROPE_DIM // 2, 1)

    h = x.reshape(T, D)
    for l in range(depth):
        o = 0
        cols = {}
        for name, n in (("q", ATT_Q_DIM), ("k", ATT_KV_DIM), ("v", ATT_KV_DIM), ("gqk", 2 * GLA_KEY_DIM),
                        ("gv", GLA_VAL_DIM), ("gr", GLA_VAL_DIM), ("z", 2 * R), ("gate", 2 * D_MODEL)):
            cols[name] = w_in[l][:, o:o + n].astype(BF16)
            o += n
        per_head = lambda w: w.reshape(-1, GLA_HEADS, GLA_DK).swapaxes(0, 1)
        upf, upb = per_head(gla_gate_up_fwd[l]), per_head(gla_gate_up_bwd[l])
        up = jnp.concatenate([jnp.concatenate([upf, jnp.zeros_like(upf)], axis=2),
                              jnp.concatenate([jnp.zeros_like(upb), upb], axis=2)], axis=1)
        up_hi = up.astype(BF16)
        up_lo = (up - up_hi.astype(F32)).astype(BF16)
        upw = jnp.concatenate([up_hi, up_hi, up_lo, jnp.zeros_like(up_lo)], axis=1)
        wz4 = jnp.tile(cols["z"], (1, LANES // (2 * R)))
        gbias = jnp.concatenate([per_head(gla_gate_bias_fwd[l]), per_head(gla_gate_bias_bwd[l])], axis=2)
        wr = w_router[l]
        wr_hi = wr.astype(BF16)
        wr_lo = (wr - wr_hi.astype(F32)).astype(BF16)
        wr2 = jnp.concatenate([jnp.concatenate([wr_hi, wr_lo], axis=1),
                               jnp.concatenate([wr_hi, jnp.zeros_like(wr_lo)], axis=1)], axis=0).T

        qt, k0, k1, vt, gq, gk, gv, gr, z, sga, sgg = _in_proj(
            h, posr, invfc, norm_mix[l].reshape(1, D), cols["q"].T,
            jnp.concatenate([cols["k"], wz4], axis=1), cols["v"].T,
            cols["gqk"], cols["gv"], cols["gr"], cols["gate"], tm=512)

        att = _swa(attn_sink[l], qt, k0.reshape(B, S, -1), k1.reshape(B, S, -1), vt, batch=B, tq=1024)
        gla = _gla(z.reshape(B, S, -1), gq.reshape(B, S, -1), gk.reshape(B, S, -1), gv.reshape(B, S, -1),
                   gr.reshape(B, S, -1), upw, gbias, gla_norm[l].reshape(1, -1))

        h1, xn, aff, aff_t = _mix_out(h, att.reshape(T, -1), gla.reshape(T, -1), sga, sgg,
                               w_branch_attn[l].astype(BF16), w_branch_gla[l].astype(BF16),
                               w_out[l].astype(BF16), norm_ffn[l].reshape(1, D), wr2, tm=512)

        aff3 = aff.reshape(B, S, N_EXPERTS)
        idx = _route(aff_t, B, cap)
        idx = jnp.swapaxes(idx, 1, 2)
        xg, wg = _gather(idx, xn.reshape(B, S * SUBLANES, LANES), aff3, cap)
        y = _ffn(xg, wg, w_exp_gate[l], w_exp_up[l], w_exp_down[l])
        moe = _scatter(idx, y, S)

        last = l == depth - 1
        gfin = norm_final.reshape(1, D)
        assert last, "the final norm is fused into the last layer's PLE kernel"
        h = _ple_out(h1, moe.reshape(T * ROW_TILES, LANES), p[l].reshape(T, PLE_DIM), norm_ple[l].reshape(1, D),
                     w_ple_gate[l].astype(BF16), w_ple[l].astype(BF16), gfin, tm=1024)
    return h.reshape(B, S, D)
```

```python
import functools
import math

import jax
import jax.numpy as jnp
from jax import lax
from jax.experimental import pallas as pl
from jax.experimental.pallas import tpu as pltpu

D_MODEL = 1024
ATT_HEADS = 8
ATT_KV_HEADS = 2
ATT_HEAD_DIM = 64
ATT_GROUP = ATT_HEADS // ATT_KV_HEADS
ATT_Q_DIM = ATT_HEADS * ATT_HEAD_DIM
ATT_KV_DIM = ATT_KV_HEADS * ATT_HEAD_DIM
WINDOW = 128
ROPE_DIM = ATT_HEAD_DIM // 4
ROPE_THETA = 500000.0
GLA_HEADS = 4
GLA_KEY_DIM = D_MODEL // 2
GLA_VAL_DIM = D_MODEL
GLA_DK = GLA_KEY_DIM // GLA_HEADS
GLA_DV = GLA_VAL_DIM // GLA_HEADS
GLA_GATE_RANK = 16
GLA_GATE_NORM = 16.0
GLA_CHUNK = 64
N_EXPERTS = 16
EXPERT_FF = D_MODEL
CAPACITY_FACTOR = 2
PLE_DIM = 256
EPS = 1e-6

LANES = 128
MIB = 1024 * 1024
BF16 = jnp.bfloat16
F32 = jnp.float32
LOG2E = math.log2(math.e)

NT_DIMS = (((1,), (1,)), ((), ()))
TN_DIMS = (((0,), (0,)), ((), ()))


def _cparams(sem, vmem_mib):
    return pltpu.CompilerParams(dimension_semantics=sem, vmem_limit_bytes=vmem_mib * MIB)


def _full(shape):
    n = len(shape)
    return pl.BlockSpec(shape, lambda *_: (0,) * n)


def _rms(x, gain):
    ms = jnp.mean(x * x, axis=-1, keepdims=True)
    return x * lax.rsqrt(ms + EPS) * gain


def _sigmoid(x):
    return 0.5 * jnp.tanh(0.5 * x) + 0.5


SUBLANES = 8
ROW_TILES = D_MODEL // LANES


def _store_token_tiles(ref2d, x, first_row=0):
    rows = x.shape[0]
    for j in range(ROW_TILES):
        ref2d[pl.ds(first_row * ROW_TILES + j, rows, stride=ROW_TILES), :] = x[:, j * LANES:(j + 1) * LANES]


def _load_token_tiles(ref2d, rows, first_row=0):
    return jnp.concatenate([ref2d[pl.ds(first_row * ROW_TILES + j, rows, stride=ROW_TILES), :]
                            for j in range(ROW_TILES)], axis=1)


SUB_ROWS = 256


def _sub_blocks(tile_rows):
    return SUB_ROWS, list(range(0, tile_rows, SUB_ROWS))


def _rope_rows(t, cos_r, sin_r, heads):
    half = ROPE_DIM // 2
    rows = []
    for h in range(heads):
        r0 = h * ATT_HEAD_DIM
        t1, t2 = t[r0:r0 + half], t[r0 + half:r0 + ROPE_DIM]
        rows += [t1 * cos_r - t2 * sin_r, t2 * cos_r + t1 * sin_r, t[r0 + ROPE_DIM:r0 + ATT_HEAD_DIM]]
    return jnp.concatenate(rows, axis=0)


def _in_proj_kernel(x_ref, posr_ref, invfc_ref, gain_ref, wqt_ref, wkz_ref, wvt_ref,
                    wgqk_ref, wgv_ref, wgr_ref, wgate_ref,
                    qt_ref, k0_ref, k1_ref, vt_ref, gq_ref, gk_ref, gv_ref, gr_ref, z_ref, sga_ref, sgg_ref):
    a = _rms(x_ref[...], gain_ref[...]).astype(BF16)
    ang_t = invfc_ref[...] * posr_ref[...].astype(F32)
    cos_r, sin_r = jnp.cos(ang_t), jnp.sin(ang_t)

    qt = lax.dot_general(wqt_ref[...], a, NT_DIMS, preferred_element_type=F32)
    qt_ref[...] = (_rope_rows(qt, cos_r, sin_r, ATT_HEADS) * (ATT_HEAD_DIM ** -0.5 * LOG2E)).astype(BF16)

    kz = jnp.dot(a, wkz_ref[...], preferred_element_type=F32)
    z_ref[...] = kz[:, ATT_KV_DIM:]
    k = _rope_rows(kz[:, :ATT_KV_DIM].T, cos_r, sin_r, ATT_KV_HEADS).T.astype(BF16)
    k0_ref[...] = k[:, :ATT_HEAD_DIM]
    k1_ref[...] = k[:, ATT_HEAD_DIM:]
    vt_ref[...] = lax.dot_general(wvt_ref[...], a, NT_DIMS, preferred_element_type=F32).astype(BF16)

    gqk = jnp.dot(a, wgqk_ref[...], preferred_element_type=F32)
    gq_ref[...] = (gqk[:, :GLA_KEY_DIM] * (GLA_DK ** -0.5)).astype(BF16)
    gk_ref[...] = gqk[:, GLA_KEY_DIM:].astype(BF16)
    gv_ref[...] = jnp.dot(a, wgv_ref[...], preferred_element_type=F32).astype(BF16)
    gr = jnp.dot(a, wgr_ref[...], preferred_element_type=F32)
    gr_ref[...] = (gr * _sigmoid(gr)).astype(BF16)
    gates = jnp.dot(a, wgate_ref[...], preferred_element_type=F32)
    sga_ref[...] = _sigmoid(gates[:, :D_MODEL]).astype(BF16)
    sgg_ref[...] = _sigmoid(gates[:, D_MODEL:]).astype(BF16)


def _in_proj(x2, posr, invfc, gain, wqt, wkz, wvt, wgqk, wgv, wgr, wgate, tm):
    T = x2.shape[0]
    row = lambda n: pl.BlockSpec((tm, n), lambda i: (i, 0))
    col = lambda n: pl.BlockSpec((n, tm), lambda i: (0, i))
    row_widths = (ATT_HEAD_DIM, ATT_HEAD_DIM, None, GLA_KEY_DIM, GLA_KEY_DIM, GLA_VAL_DIM,
                  GLA_VAL_DIM, wkz.shape[1] - ATT_KV_DIM, D_MODEL, D_MODEL)
    row_dtypes = (BF16,) * 7 + (F32, BF16, BF16)
    out_specs = [col(ATT_Q_DIM)]
    out_shape = [jax.ShapeDtypeStruct((ATT_Q_DIM, T), BF16)]
    for n, dt in zip(row_widths, row_dtypes):
        if n is None:
            out_specs.append(col(ATT_KV_DIM))
            out_shape.append(jax.ShapeDtypeStruct((ATT_KV_DIM, T), BF16))
        else:
            out_specs.append(row(n))
            out_shape.append(jax.ShapeDtypeStruct((T, n), dt))
    consts = (invfc, gain, wqt, wkz, wvt, wgqk, wgv, wgr, wgate)
    return pl.pallas_call(
        _in_proj_kernel,
        grid=(T // tm,),
        in_specs=[row(D_MODEL), col(1)] + [_full(c.shape) for c in consts],
        out_specs=out_specs,
        out_shape=out_shape,
        compiler_params=_cparams(("parallel",), 56),
        name="in_proj",
    )(x2, posr, *consts)


def _swa_kernel(sink_ref, qt_ref, k0_ref, k1_ref, vt_ref, o_ref, *, tq, seq):
    blk = WINDOW
    span = 3 * blk
    hd = ATT_HEAD_DIM
    n = pl.program_id(1)
    ones = jnp.ones((16, span), BF16)
    kv_refs = (k0_ref, k1_ref)

    def window_start(sb):
        return pl.multiple_of(jnp.clip(n * tq + (sb - 1) * blk, 0, seq - span), blk)

    def scores(sb, g):
        kw = kv_refs[g][0, pl.ds(window_start(sb), span), :]
        heads = range(g * ATT_GROUP, (g + 1) * ATT_GROUP)
        qs = jnp.concatenate([qt_ref[h * hd:(h + 1) * hd, sb * blk:(sb + 1) * blk] for h in heads], axis=1)
        return jnp.dot(kw, qs, preferred_element_type=F32)

    work = [(sb, g) for sb in range(tq // blk) for g in range(ATT_KV_HEADS)]
    s_next = scores(*work[0])
    outs = []
    for step, (sb, g) in enumerate(work):
        s_all = s_next
        if step + 1 < len(work):
            s_next = scores(*work[step + 1])
        q0 = n * tq + sb * blk
        start = window_start(sb)
        kj = start + lax.broadcasted_iota(jnp.int32, (span, blk), 0)
        qi = q0 + lax.broadcasted_iota(jnp.int32, (span, blk), 1)
        valid = jnp.abs(qi - kj) <= WINDOW
        vaug = jnp.concatenate([vt_ref[g * hd:(g + 1) * hd, pl.ds(start, span)], ones], axis=0)
        sinks = [sink_ref[g * ATT_GROUP + i] * LOG2E for i in range(ATT_GROUP)]
        ss = [jnp.where(valid, s_all[:, i * blk:(i + 1) * blk], -jnp.inf) for i in range(ATT_GROUP)]
        ms = [jnp.maximum(jnp.max(s, axis=0, keepdims=True), sink) for s, sink in zip(ss, sinks)]
        rs = [jnp.dot(vaug, jnp.exp2(s - m).astype(BF16), preferred_element_type=F32) for s, m in zip(ss, ms)]
        outs += [r[:hd] / (r[hd:hd + 1] + jnp.exp2(sink - m)) for r, m, sink in zip(rs, ms, sinks)]
        if g == ATT_KV_HEADS - 1:
            for pr in range(ATT_HEADS // 2):
                pair = jnp.concatenate([outs[2 * pr], outs[2 * pr + 1]], axis=0)
                o_ref[0, sb * blk:(sb + 1) * blk, pr * 2 * hd:(pr + 1) * 2 * hd] = pair.T.astype(BF16)
            outs = []


def _swa(sink, qt, k0, k1, vt, batch, tq):
    S = k0.shape[1]
    nq = S // tq
    kspec = pl.BlockSpec((1, S, ATT_HEAD_DIM), lambda b, n: (b, 0, 0))
    return pl.pallas_call(
        functools.partial(_swa_kernel, tq=tq, seq=S),
        grid=(batch, nq),
        in_specs=[pl.BlockSpec(memory_space=pltpu.SMEM),
                  pl.BlockSpec((ATT_Q_DIM, tq), lambda b, n: (0, b * nq + n)),
                  kspec, kspec,
                  pl.BlockSpec((ATT_KV_DIM, S), lambda b, n: (0, b))],
        out_specs=pl.BlockSpec((1, tq, ATT_Q_DIM), lambda b, n: (b, n, 0)),
        out_shape=jax.ShapeDtypeStruct((batch, S, ATT_Q_DIM), BF16),
        compiler_params=_cparams(("parallel", "parallel"), 32),
        name="swa",
    )(sink, qt, k0, k1, vt)


def _log2_sigmoid(u):
    return jnp.minimum(u, 0.0) * LOG2E - jnp.log2(1.0 + jnp.exp2(jnp.abs(u) * -LOG2E))


def _split2(x):
    hi = x.astype(BF16)
    return hi, (x - hi.astype(F32)).astype(BF16)


GLA_WAYS = 4


def _gla_kernel(z_ref, q_ref, k_ref, v_ref, r_ref, upw_ref, bias_ref, gain_ref,
                o_ref, cf_ref, cb_ref, kef_ref, keb_ref, st_ref, s_ref, *, seq):
    L = GLA_CHUNK
    R2 = 2 * GLA_GATE_RANK
    nc = seq // L
    grp = 4 * L
    cpg = grp // L
    dk = GLA_DK
    mm = functools.partial(jnp.dot, preferred_element_type=F32)
    nt = functools.partial(lax.dot_general, dimension_numbers=NT_DIMS, preferred_element_type=F32)

    row = lax.broadcasted_iota(jnp.int32, (grp, grp), 0)
    col = lax.broadcasted_iota(jnp.int32, (grp, grp), 1)
    same = (row // L) == (col // L)
    fwd_mask = same & (col <= row)
    bwd_mask = same & (col > row)
    tri_lo = jnp.where(fwd_mask, 1.0, 0.0).astype(BF16)
    tri_up = jnp.where(same & (col >= row), 1.0, 0.0).astype(BF16)
    lane = lax.broadcasted_iota(jnp.int32, (1, LANES), 1)
    use_lo = (lane >= R2) & (lane < 2 * R2)

    def group_starts(i):
        return [pl.multiple_of((i * GLA_WAYS + w) * grp, grp) for w in range(GLA_WAYS)]

    def cum_body(i, carry):
        r0s = group_starts(i)
        zs = [_split2(z_ref[0, pl.ds(r0, grp), :]) for r0 in r0s]
        us = [mm(jnp.where(use_lo, zl, zh), upw_ref[...]) + bias_ref[...] for zh, zl in zs]
        las = [_split2(_log2_sigmoid(u) * (1.0 / GLA_GATE_NORM)) for u in us]
        cfxs = [mm(tri_lo, jnp.concatenate([lh[:, :dk], ll[:, :dk]], axis=1)) for lh, ll in las]
        cbxs = [mm(tri_up, jnp.concatenate([lh[:, dk:], ll[:, dk:]], axis=1)) for lh, ll in las]
        for r0, cfx, cbx in zip(r0s, cfxs, cbxs):
            cf = cfx[:, :dk] + cfx[:, dk:]
            cb = cbx[:, :dk] + cbx[:, dk:]
            cf_ref[pl.ds(r0, grp), :] = cf
            cb_ref[pl.ds(r0, grp), :] = cb
            k = k_ref[0, pl.ds(r0, grp), :].astype(F32)
            for c in range(cpg):
                sl = slice(c * L, (c + 1) * L)
                gf = cf[(c + 1) * L - 1:(c + 1) * L]
                gb = cb[c * L:c * L + 1]
                kef_ref[pl.ds(r0 + c * L, L), :] = (k[sl] * jnp.exp2(gf - cf[sl])).astype(BF16)
                keb_ref[pl.ds(r0 + c * L, L), :] = (k[sl] * jnp.exp2(gb - cb[sl])).astype(BF16)
        return carry

    lax.fori_loop(0, seq // (grp * GLA_WAYS), cum_body, 0, unroll=True)

    s_ref[...] = jnp.zeros_like(s_ref)
    zero_k = jnp.zeros((L, dk), BF16)

    def state_body(i, carry):
        j = nc - 1 - i
        rf = pl.multiple_of(i * L, L)
        rb = pl.multiple_of(j * L, L)
        vcat = jnp.concatenate([v_ref[0, pl.ds(rf, L), :], v_ref[0, pl.ds(rb, L), :]], axis=0)
        kblk = jnp.concatenate([jnp.concatenate([kef_ref[pl.ds(rf, L), :], zero_k], axis=1),
                                jnp.concatenate([zero_k, keb_ref[pl.ds(rb, L), :]], axis=1)], axis=0)
        kv = lax.dot_general(vcat, kblk, TN_DIMS, preferred_element_type=F32)
        decay = jnp.exp2(jnp.concatenate([cf_ref[pl.ds(rf + L - 1, 1), :], cb_ref[pl.ds(rb, 1), :]], axis=1))
        s = s_ref[...]
        st_ref[i, :, 0:dk] = s[:, :dk].astype(BF16)
        st_ref[j, :, dk:2 * dk] = s[:, dk:].astype(BF16)
        s_ref[...] = s * decay + kv
        return carry

    lax.fori_loop(0, nc, state_body, 0, unroll=True)

    def out_body(i, carry):
        r0s = group_starts(i)
        ops = []
        for r0 in r0s:
            q = q_ref[0, pl.ds(r0, grp), :].astype(F32)
            k = k_ref[0, pl.ds(r0, grp), :].astype(F32)
            cf = cf_ref[pl.ds(r0, grp), :]
            cb = cb_ref[pl.ds(r0, grp), :]
            ops.append(((q * jnp.exp2(cf)).astype(BF16), (k * jnp.exp2(-cf)).astype(BF16),
                        (q * jnp.exp2(cb)).astype(BF16), (k * jnp.exp2(-cb)).astype(BF16)))
        scores = [(nt(qf, kf), nt(qb, kb)) for qf, kf, qb, kb in ops]
        attns = [jnp.where(fwd_mask, af, jnp.where(bwd_mask, ab, 0.0)).astype(BF16) for af, ab in scores]
        outs = []
        for r0, attn, (qf, _, qb, _) in zip(r0s, attns, ops):
            c0 = r0 // L
            qcat = jnp.concatenate([qf, qb], axis=1)
            inter = jnp.concatenate([nt(qcat[c * L:(c + 1) * L], st_ref[c0 + c]) for c in range(cpg)], axis=0)
            outs.append(mm(attn, v_ref[0, pl.ds(r0, grp), :]) + inter)
        for r0, o in zip(r0s, outs):
            o_ref[0, pl.ds(r0, grp), :] = (_rms(o, gain_ref[...]) * r_ref[0, pl.ds(r0, grp), :].astype(F32)).astype(BF16)
        return carry

    lax.fori_loop(0, seq // (grp * GLA_WAYS), out_body, 0, unroll=True)


def _gla(z, gq, gk, gv, gr, upw, bias, gain):
    B, S, _ = gq.shape
    nc = S // GLA_CHUNK
    seq_blk = lambda n: pl.BlockSpec((1, S, n), lambda b, h: (b, 0, h))
    head_blk = lambda r, n: pl.BlockSpec((None, r, n), lambda b, h: (h, 0, 0))
    return pl.pallas_call(
        functools.partial(_gla_kernel, seq=S),
        grid=(B, GLA_HEADS),
        in_specs=[pl.BlockSpec((1, S, LANES), lambda b, h: (b, 0, 0)),
                  seq_blk(GLA_DK), seq_blk(GLA_DK), seq_blk(GLA_DV), seq_blk(GLA_DV),
                  head_blk(LANES, 2 * GLA_DK), head_blk(1, 2 * GLA_DK),
                  pl.BlockSpec((1, GLA_DV), lambda b, h: (0, h))],
        out_specs=seq_blk(GLA_DV),
        out_shape=jax.ShapeDtypeStruct((B, S, GLA_VAL_DIM), BF16),
        scratch_shapes=[pltpu.VMEM((S, GLA_DK), F32), pltpu.VMEM((S, GLA_DK), F32),
                        pltpu.VMEM((S, GLA_DK), BF16), pltpu.VMEM((S, GLA_DK), BF16),
                        pltpu.VMEM((nc, GLA_DV, 2 * GLA_DK), BF16),
                        pltpu.VMEM((GLA_DV, 2 * GLA_DK), F32)],
        compiler_params=_cparams(("parallel", "parallel"), 48),
        name="gla",
    )(z, gq, gk, gv, gr, upw, bias, gain)


def _mix_out_kernel(x_ref, a_ref, g_ref, sga_ref, sgg_ref, wa_ref, wb_ref, wo_ref, gain_ref, wr_ref,
                    h_ref, xn_ref, aff_ref, afft_ref):
    sub, starts = _sub_blocks(x_ref.shape[0])
    mm = functools.partial(jnp.dot, preferred_element_type=F32)
    blk = lambda ref, r0: ref[r0:r0 + sub, :]
    pad = jnp.zeros((LANES - N_EXPERTS, sub), F32)

    y_att = [mm(blk(a_ref, r0), wa_ref[...]) for r0 in starts]
    y_gla = [mm(blk(g_ref, r0), wb_ref[...]) for r0 in starts]
    merged = [(blk(sga_ref, r0).astype(F32) * ya + blk(sgg_ref, r0).astype(F32) * yg).astype(BF16)
              for r0, ya, yg in zip(starts, y_att, y_gla)]
    hs = [blk(x_ref, r0) + mm(m, wo_ref[...]) for r0, m in zip(starts, merged)]
    xns = [_rms(h, gain_ref[...]) for h in hs]
    for r0, h, xn in zip(starts, hs, xns):
        h_ref[r0:r0 + sub, :] = h
        _store_token_tiles(xn_ref, xn, r0)
    his = [xn.astype(BF16) for xn in xns]
    parts = [lax.dot_general(wr_ref[...], jnp.concatenate([hi, (xn - hi.astype(F32)).astype(BF16)], axis=1),
                             NT_DIMS, preferred_element_type=F32) for xn, hi in zip(xns, his)]
    for r0, part in zip(starts, parts):
        logits = part[:N_EXPERTS] + part[N_EXPERTS:]
        e = jnp.exp(logits - jnp.max(logits, axis=0, keepdims=True))
        aff_t = e / jnp.sum(e, axis=0, keepdims=True)
        afft_ref[:, r0:r0 + sub] = aff_t
        aff_ref[r0:r0 + sub, :] = jnp.concatenate([aff_t, pad], axis=0).T[:, :N_EXPERTS]


def _mix_out(x2, a, g, sga, sgg, wa, wb, wo, gain, wr, tm):
    T = x2.shape[0]
    row = lambda n: pl.BlockSpec((tm, n), lambda i: (i, 0))
    return pl.pallas_call(
        _mix_out_kernel,
        grid=(T // tm,),
        in_specs=[row(D_MODEL), row(ATT_Q_DIM), row(GLA_VAL_DIM), row(D_MODEL), row(D_MODEL),
                  _full(wa.shape), _full(wb.shape), _full(wo.shape), _full(gain.shape), _full(wr.shape)],
        out_specs=[row(D_MODEL), pl.BlockSpec((tm * ROW_TILES, LANES), lambda i: (i, 0)), row(N_EXPERTS),
                   pl.BlockSpec((N_EXPERTS, tm), lambda i: (0, i))],
        out_shape=[jax.ShapeDtypeStruct((T, D_MODEL), F32), jax.ShapeDtypeStruct((T * ROW_TILES, LANES), F32),
                   jax.ShapeDtypeStruct((T, N_EXPERTS), F32), jax.ShapeDtypeStruct((N_EXPERTS, T), F32)],
        compiler_params=_cparams(("parallel",), 48),
        name="mix_out",
    )(x2, a, g, sga, sgg, wa, wb, wo, gain, wr)


ROUTE_WAYS = 4


def _route_kernel(aff_ref, idx_ref, cum_ref, *, cap, seq):
    E = N_EXPERTS
    aff = aff_ref[...]
    count = lambda mask: jnp.sum(mask.astype(jnp.int32), axis=1, keepdims=True)
    as_float = lambda pattern: lax.bitcast_convert_type(pattern, F32)

    def thr_body(t, pattern):
        cand = pattern | jnp.left_shift(jnp.int32(1), 30 - t)
        return jnp.where(count(aff >= as_float(cand)) >= cap, cand, pattern)

    thr = as_float(lax.fori_loop(0, 31, thr_body, jnp.zeros((E, 1), jnp.int32)))
    above = aff > thr
    tie = aff == thr
    need = cap - count(above)

    pos = lax.broadcasted_iota(jnp.int32, (E, seq), 1)

    def tie_body(t, last):
        cand = last | jnp.left_shift(jnp.int32(1), (seq.bit_length() - 2) - t)
        return jnp.where(count(tie & (pos < cand)) < need, cand, last)

    last = lax.fori_loop(0, seq.bit_length() - 1, tie_body, jnp.zeros((E, 1), jnp.int32))
    sel = (above | (tie & (pos <= last))).astype(BF16)

    nt = seq // LANES
    lrow = lax.broadcasted_iota(jnp.int32, (LANES, LANES), 0)
    lcol = lax.broadcasted_iota(jnp.int32, (LANES, LANES), 1)
    tri = (lrow <= lcol).astype(BF16)
    mm = functools.partial(jnp.dot, preferred_element_type=F32)
    for t in range(nt):
        cum_ref[t * E:(t + 1) * E, :] = mm(sel[:, t * LANES:(t + 1) * LANES], tri)
    tile_of = (lax.broadcasted_iota(jnp.int32, (seq, LANES), 0) // LANES
               == lax.broadcasted_iota(jnp.int32, (seq, LANES), 1)).astype(BF16)
    per_tile = mm(sel, tile_of)
    lane = lax.broadcasted_iota(jnp.int32, (1, LANES), 1)
    far = jnp.float32(2 * seq)
    t_end = jnp.where(lane < nt, mm(per_tile.astype(BF16), tri), far)
    t_start = jnp.where(lane < nt, t_end - per_tile, far)
    pad = jnp.zeros((LANES - E, LANES), F32)
    t_start_cols = jnp.concatenate([jnp.where(lane < nt, t_start, 0.0), pad], axis=0).T

    slot = lax.broadcasted_iota(jnp.int32, (cap, LANES), 0).astype(F32)
    ones = jnp.ones((LANES, LANES), BF16)
    zrows = jnp.zeros((LANES - nt, 2 * LANES), F32)
    for e0 in range(0, E, ROUTE_WAYS):
        es = range(e0, e0 + ROUTE_WAYS)
        tiles, picks, whole = [], [], []
        for e in es:
            absc = cum_ref[pl.ds(e, nt, stride=E), :] + t_start_cols[0:nt, e:e + 1]
            hi = jnp.where(absc >= 256.0, 1.0, 0.0) + jnp.where(absc >= 512.0, 1.0, 0.0)
            lo = absc - 256.0 * hi
            tiles.append(jnp.concatenate([jnp.concatenate([lo, hi], axis=1), zrows], axis=0).astype(BF16))
            done = jnp.where(t_end[e:e + 1] <= slot, 1.0, 0.0)
            whole.append(done)
            picks.append((jnp.where(t_start[e:e + 1] <= slot, 1.0, 0.0) - done).astype(BF16))
        rows = [mm(p, w) for p, w in zip(picks, tiles)]
        votes = [(jnp.where(r[:, :LANES] + 256.0 * r[:, LANES:] <= slot, 1.0, 0.0) + float(LANES) * d).astype(BF16)
                 for r, d in zip(rows, whole)]
        for e, v in zip(es, votes):
            idx_ref[0, :, e:e + 1] = mm(v, ones)[:, e:e + 1].astype(jnp.int32)


def _route(aff_t, batch, cap):
    E, T = aff_t.shape
    B, S = batch, T // batch
    return pl.pallas_call(
        functools.partial(_route_kernel, cap=cap, seq=S),
        grid=(B,),
        in_specs=[pl.BlockSpec((E, S), lambda b: (0, b))],
        out_specs=pl.BlockSpec((1, cap, E), lambda b: (b, 0, 0)),
        out_shape=jax.ShapeDtypeStruct((B, cap, E), jnp.int32),
        scratch_shapes=[pltpu.VMEM((S // LANES * E, LANES), F32)],
        compiler_params=_cparams(("parallel",), 32),
        name="route",
    )(aff_t)


MOE_EXPERTS_PER_STEP = 4


def _gather_kernel(idx_ref, xn_ref, aff_ref, xg_ref, wg_ref, ws_ref, *, cap):
    n = MOE_EXPERTS_PER_STEP
    lane = lax.broadcasted_iota(jnp.int32, (cap, N_EXPERTS), 1)
    for j in range(n):
        for i in range(cap):
            t = idx_ref[j, 0, i]
            src = pl.multiple_of(t * SUBLANES, SUBLANES)
            xg_ref[0, j, i * SUBLANES:(i + 1) * SUBLANES, :] = xn_ref[0, pl.ds(src, SUBLANES), :]
            ws_ref[j, i:i + 1, :] = aff_ref[0, pl.ds(t, 1), :]
        e = pl.program_id(1) * n + j
        wg_ref[0, j] = jnp.sum(jnp.where(lane == e, ws_ref[j], 0.0), axis=1, keepdims=True)


def _gather(idx, xn_tiles, aff, cap):
    B, S, E = aff.shape
    n = MOE_EXPERTS_PER_STEP
    return pl.pallas_call(
        functools.partial(_gather_kernel, cap=cap),
        grid=(B, E // n),
        in_specs=[pl.BlockSpec((n, 1, cap), lambda b, e: (b * (E // n) + e, 0, 0), memory_space=pltpu.SMEM),
                  pl.BlockSpec((1, S * SUBLANES, LANES), lambda b, e: (b, 0, 0)),
                  pl.BlockSpec((1, S, E), lambda b, e: (b, 0, 0))],
        out_specs=[pl.BlockSpec((1, n, cap * SUBLANES, LANES), lambda b, e: (b, e, 0, 0)),
                   pl.BlockSpec((1, n, cap, 1), lambda b, e: (b, e, 0, 0))],
        out_shape=[jax.ShapeDtypeStruct((B, E, cap * SUBLANES, LANES), F32),
                   jax.ShapeDtypeStruct((B, E, cap, 1), F32)],
        scratch_shapes=[pltpu.VMEM((n, cap, E), F32)],
        compiler_params=_cparams(("arbitrary", "arbitrary"), 60),
        name="gather",
    )(idx.reshape(B * E, 1, cap), xn_tiles, aff)


def _ffn_kernel(xg_ref, wg_ref, w1_ref, w2_ref, w3_ref, y_ref, b1_ref, b2_ref, b3_ref):
    @pl.when(pl.program_id(1) == 0)
    def _():
        b1_ref[...] = w1_ref[0].astype(BF16)
        b2_ref[...] = w2_ref[0].astype(BF16)
        b3_ref[...] = w3_ref[0].astype(BF16)

    rows = wg_ref.shape[2]
    seqs = range(xg_ref.shape[0])
    mm = functools.partial(jnp.dot, preferred_element_type=F32)
    xgs = [_load_token_tiles(xg_ref.at[i, 0], rows).astype(BF16) for i in seqs]
    gates = [mm(xg, b1_ref[...]) for xg in xgs]
    ups = [mm(xg, b2_ref[...]) for xg in xgs]
    hids = [(gate * _sigmoid(gate) * up).astype(BF16) for gate, up in zip(gates, ups)]
    ys = [mm(hid, b3_ref[...]) * wg_ref[i, 0] for i, hid in zip(seqs, hids)]
    for i, y in zip(seqs, ys):
        _store_token_tiles(y_ref.at[i, 0], y)


FFN_SEQS_PER_STEP = 2


def _ffn(xg, wg, w1, w2, w3):
    B, E, C, _ = wg.shape
    _, D, F = w1.shape
    n = FFN_SEQS_PER_STEP
    return pl.pallas_call(
        _ffn_kernel,
        grid=(E, B // n),
        in_specs=[pl.BlockSpec((n, 1, C * ROW_TILES, LANES), lambda e, b: (b, e, 0, 0)),
                  pl.BlockSpec((n, 1, C, 1), lambda e, b: (b, e, 0, 0)),
                  pl.BlockSpec((1, D, F), lambda e, b: (e, 0, 0)),
                  pl.BlockSpec((1, D, F), lambda e, b: (e, 0, 0)),
                  pl.BlockSpec((1, F, D), lambda e, b: (e, 0, 0))],
        out_specs=pl.BlockSpec((n, 1, C * ROW_TILES, LANES), lambda e, b: (b, e, 0, 0)),
        out_shape=jax.ShapeDtypeStruct((B, E, C * ROW_TILES, LANES), F32),
        scratch_shapes=[pltpu.VMEM((D, F), BF16), pltpu.VMEM((D, F), BF16), pltpu.VMEM((F, D), BF16)],
        compiler_params=_cparams(("arbitrary", "arbitrary"), 60),
        name="ffn",
    )(xg, wg, w1, w2, w3)


SCATTER_BATCH = 16


def _scatter_kernel(idx_ref, y_ref, o_ref, *, cap):
    @pl.when(pl.program_id(1) == 0)
    def _():
        o_ref[...] = jnp.zeros_like(o_ref)

    for j in range(MOE_EXPERTS_PER_STEP):
        for i0 in range(0, cap, SCATTER_BATCH):
            slots = range(i0, i0 + SCATTER_BATCH)
            rows = [pl.ds(pl.multiple_of(idx_ref[j, 0, i] * SUBLANES, SUBLANES), SUBLANES) for i in slots]
            new = [o_ref[0, r, :] + y_ref[0, j, i * SUBLANES:(i + 1) * SUBLANES, :] for r, i in zip(rows, slots)]
            for r, v in zip(rows, new):
                o_ref[0, r, :] = v


def _scatter(idx, y_tiles, seq):
    B, E, cap = idx.shape
    n = MOE_EXPERTS_PER_STEP
    return pl.pallas_call(
        functools.partial(_scatter_kernel, cap=cap),
        grid=(B, E // n),
        in_specs=[pl.BlockSpec((n, 1, cap), lambda b, e: (b * (E // n) + e, 0, 0), memory_space=pltpu.SMEM),
                  pl.BlockSpec((1, n, cap * SUBLANES, LANES), lambda b, e: (b, e, 0, 0))],
        out_specs=pl.BlockSpec((1, seq * SUBLANES, LANES), lambda b, e: (b, 0, 0)),
        out_shape=jax.ShapeDtypeStruct((B, seq * SUBLANES, LANES), F32),
        compiler_params=_cparams(("arbitrary", "arbitrary"), 60),
        name="scatter",
    )(idx.reshape(B * E, 1, cap), y_tiles)


def _ple_out_kernel(h_ref, moe_ref, p_ref, gple_ref, wpg_ref, wple_ref, gfin_ref, o_ref):
    sub, starts = _sub_blocks(h_ref.shape[0])
    mm = functools.partial(jnp.dot, preferred_element_type=F32)
    hs = [h_ref[r0:r0 + sub, :] + _load_token_tiles(moe_ref, sub, r0) for r0 in starts]
    ns = [_rms(h, gple_ref[...]).astype(BF16) for h in hs]
    gates = [_sigmoid(mm(n, wpg_ref[...])) for n in ns]
    embs = [mm(p_ref[r0:r0 + sub, :].astype(BF16), wple_ref[...]) for r0 in starts]
    for r0, h, gate, emb in zip(starts, hs, gates, embs):
        o_ref[r0:r0 + sub, :] = _rms(h + gate * emb, gfin_ref[...])


def _ple_out(h, moe, p2, gple, wpg, wple, gfin, tm):
    T = h.shape[0]
    row = lambda n: pl.BlockSpec((tm, n), lambda i: (i, 0))
    return pl.pallas_call(
        _ple_out_kernel,
        grid=(T // tm,),
        in_specs=[row(D_MODEL), pl.BlockSpec((tm * ROW_TILES, LANES), lambda i: (i, 0)), row(PLE_DIM),
                  _full(gple.shape), _full(wpg.shape), _full(wple.shape), _full(gfin.shape)],
        out_specs=row(D_MODEL),
        out_shape=jax.ShapeDtypeStruct((T, D_MODEL), F32),
        compiler_params=_cparams(("parallel",), 48),
        name="ple_out",
    )(h, moe, p2, gple, wpg, wple, gfin)


def kernel(x, p, positions, norm_mix, w_in, gla_gate_up_fwd, gla_gate_bias_fwd, gla_gate_up_bwd, gla_gate_bias_bwd, attn_sink, gla_norm, w_branch_attn, w_branch_gla, w_out, norm_ffn, w_router, w_exp_gate, w_exp_up, w_exp_down, norm_ple, w_ple_gate, w_ple, norm_final):
    B, S, D = x.shape
    T = B * S
    depth = w_in.shape[0]
    cap = CAPACITY_FACTOR * S // N_EXPERTS
    R = GLA_GATE_RANK

    posr = positions.reshape(1, T)
    inv_freq = ROPE_THETA ** (-jnp.arange(0, ROPE_DIM, 2, dtype=F32) / ROPE_DIM)
    invfc = inv_freq.reshape(ROPE_DIM // 2, 1)

    h = x.reshape(T, D)
    for l in range(depth):
        o = 0
        cols = {}
        for name, n in (("q", ATT_Q_DIM), ("k", ATT_KV_DIM), ("v", ATT_KV_DIM), ("gqk", 2 * GLA_KEY_DIM),
                        ("gv", GLA_VAL_DIM), ("gr", GLA_VAL_DIM), ("z", 2 * R), ("gate", 2 * D_MODEL)):
            cols[name] = w_in[l][:, o:o + n].astype(BF16)
            o += n
        per_head = lambda w: w.reshape(-1, GLA_HEADS, GLA_DK).swapaxes(0, 1)
        upf, upb = per_head(gla_gate_up_fwd[l]), per_head(gla_gate_up_bwd[l])
        up = jnp.concatenate([jnp.concatenate([upf, jnp.zeros_like(upf)], axis=2),
                              jnp.concatenate([jnp.zeros_like(upb), upb], axis=2)], axis=1)
        up_hi = up.astype(BF16)
        up_lo = (up - up_hi.astype(F32)).astype(BF16)
        upw = jnp.concatenate([up_hi, up_hi, up_lo, jnp.zeros_like(up_lo)], axis=1)
        wz4 = jnp.tile(cols["z"], (1, LANES // (2 * R)))
        gbias = jnp.concatenate([per_head(gla_gate_bias_fwd[l]), per_head(gla_gate_bias_bwd[l])], axis=2)
        wr = w_router[l]
        wr_hi = wr.astype(BF16)
        wr_lo = (wr - wr_hi.astype(F32)).astype(BF16)
        wr2 = jnp.concatenate([jnp.concatenate([wr_hi, wr_lo], axis=1),
                               jnp.concatenate([wr_hi, jnp.zeros_like(wr_lo)], axis=1)], axis=0).T

        qt, k0, k1, vt, gq, gk, gv, gr, z, sga, sgg = _in_proj(
            h, posr, invfc, norm_mix[l].reshape(1, D), cols["q"].T,
            jnp.concatenate([cols["k"], wz4], axis=1), cols["v"].T,
            cols["gqk"], cols["gv"], cols["gr"], cols["gate"], tm=512)

        att = _swa(attn_sink[l], qt, k0.reshape(B, S, -1), k1.reshape(B, S, -1), vt, batch=B, tq=1024)
        gla = _gla(z.reshape(B, S, -1), gq.reshape(B, S, -1), gk.reshape(B, S, -1), gv.reshape(B, S, -1),
                   gr.reshape(B, S, -1), upw, gbias, gla_norm[l].reshape(1, -1))

        h1, xn, aff, aff_t = _mix_out(h, att.reshape(T, -1), gla.reshape(T, -1), sga, sgg,
                               w_branch_attn[l].astype(BF16), w_branch_gla[l].astype(BF16),
                               w_out[l].astype(BF16), norm_ffn[l].reshape(1, D), wr2, tm=512)

        aff3 = aff.reshape(B, S, N_EXPERTS)
        idx = _route(aff_t, B, cap)
        idx = jnp.swapaxes(idx, 1, 2)
        xg, wg = _gather(idx, xn.reshape(B, S * SUBLANES, LANES), aff3, cap)
        y = _ffn(xg, wg, w_exp_gate[l], w_exp_up[l], w_exp_down[l])
        moe = _scatter(idx, y, S)

        last = l == depth - 1
        gfin = norm_final.reshape(1, D)
        assert last, "the final norm is fused into the last layer's PLE kernel"
        h = _ple_out(h1, moe.reshape(T * ROW_TILES, LANES), p[l].reshape(T, PLE_DIM), norm_ple[l].reshape(1, D),
                     w_ple_gate[l].astype(BF16), w_ple[l].astype(BF16), gfin, tm=1024)
    return h.reshape(B, S, D)
```

```python
import functools
import math

import jax
import jax.numpy as jnp
from jax import lax
from jax.experimental import pallas as pl
from jax.experimental.pallas import tpu as pltpu

D_MODEL = 1024
ATT_HEADS = 8
ATT_KV_HEADS = 2
ATT_HEAD_DIM = 64
ATT_GROUP = ATT_HEADS // ATT_KV_HEADS
ATT_Q_DIM = ATT_HEADS * ATT_HEAD_DIM
ATT_KV_DIM = ATT_KV_HEADS * ATT_HEAD_DIM
WINDOW = 128
ROPE_DIM = ATT_HEAD_DIM // 4
ROPE_THETA = 500000.0
GLA_HEADS = 4
GLA_KEY_DIM = D_MODEL // 2
GLA_VAL_DIM = D_MODEL
GLA_DK = GLA_KEY_DIM // GLA_HEADS
GLA_DV = GLA_VAL_DIM // GLA_HEADS
GLA_GATE_RANK = 16
GLA_GATE_NORM = 16.0
GLA_CHUNK = 64
N_EXPERTS = 16
EXPERT_FF = D_MODEL
CAPACITY_FACTOR = 2
PLE_DIM = 256
EPS = 1e-6

LANES = 128
MIB = 1024 * 1024
BF16 = jnp.bfloat16
F32 = jnp.float32
LOG2E = math.log2(math.e)

NT_DIMS = (((1,), (1,)), ((), ()))
TN_DIMS = (((0,), (0,)), ((), ()))

TILE_ROWS = {"in_proj": 512, "swa": 1024, "mix_out": 512, "ple_out": 1024}
VMEM_LIMIT_MIB = {"in_proj": 56, "swa": 32, "gla": 48, "mix_out": 48, "route": 32, "gather": 60,
                  "ffn": 60, "scatter": 60, "ple_out": 48}


def _cparams(name, sem):
    return pltpu.CompilerParams(dimension_semantics=sem, vmem_limit_bytes=VMEM_LIMIT_MIB[name] * MIB)


def _full(shape):
    n = len(shape)
    return pl.BlockSpec(shape, lambda *_: (0,) * n)


def _rms(x, gain):
    ms = jnp.mean(x * x, axis=-1, keepdims=True)
    return x * lax.rsqrt(ms + EPS) * gain


def _sigmoid(x):
    return 0.5 * jnp.tanh(0.5 * x) + 0.5


SUBLANES = 8
ROW_TILES = D_MODEL // LANES


def _store_token_tiles(ref2d, x, first_row=0):
    rows = x.shape[0]
    for j in range(ROW_TILES):
        ref2d[pl.ds(first_row * ROW_TILES + j, rows, stride=ROW_TILES), :] = x[:, j * LANES:(j + 1) * LANES]


def _load_token_tiles(ref2d, rows, first_row=0):
    return jnp.concatenate([ref2d[pl.ds(first_row * ROW_TILES + j, rows, stride=ROW_TILES), :]
                            for j in range(ROW_TILES)], axis=1)


SUB_ROWS = 256


def _sub_blocks(tile_rows):
    return SUB_ROWS, list(range(0, tile_rows, SUB_ROWS))


def _rope_rows(t, cos_r, sin_r, heads):
    half = ROPE_DIM // 2
    rows = []
    for h in range(heads):
        r0 = h * ATT_HEAD_DIM
        t1, t2 = t[r0:r0 + half], t[r0 + half:r0 + ROPE_DIM]
        rows += [t1 * cos_r - t2 * sin_r, t2 * cos_r + t1 * sin_r, t[r0 + ROPE_DIM:r0 + ATT_HEAD_DIM]]
    return jnp.concatenate(rows, axis=0)


def _in_proj_kernel(x_ref, posr_ref, invfc_ref, gain_ref, wqt_ref, wkz_ref, wvt_ref,
                    wgqk_ref, wgv_ref, wgr_ref, wgate_ref,
                    qt_ref, k0_ref, k1_ref, vt_ref, gq_ref, gk_ref, gv_ref, gr_ref, z_ref, sga_ref, sgg_ref):
    a = _rms(x_ref[...], gain_ref[...]).astype(BF16)
    ang_t = invfc_ref[...] * posr_ref[...].astype(F32)
    cos_r, sin_r = jnp.cos(ang_t), jnp.sin(ang_t)

    qt = lax.dot_general(wqt_ref[...], a, NT_DIMS, preferred_element_type=F32)
    qt_ref[...] = (_rope_rows(qt, cos_r, sin_r, ATT_HEADS) * (ATT_HEAD_DIM ** -0.5 * LOG2E)).astype(BF16)

    kz = jnp.dot(a, wkz_ref[...], preferred_element_type=F32)
    z_ref[...] = kz[:, ATT_KV_DIM:]
    k = _rope_rows(kz[:, :ATT_KV_DIM].T, cos_r, sin_r, ATT_KV_HEADS).T.astype(BF16)
    k0_ref[...] = k[:, :ATT_HEAD_DIM]
    k1_ref[...] = k[:, ATT_HEAD_DIM:]
    vt_ref[...] = lax.dot_general(wvt_ref[...], a, NT_DIMS, preferred_element_type=F32).astype(BF16)

    gqk = jnp.dot(a, wgqk_ref[...], preferred_element_type=F32)
    gq_ref[...] = (gqk[:, :GLA_KEY_DIM] * (GLA_DK ** -0.5)).astype(BF16)
    gk_ref[...] = gqk[:, GLA_KEY_DIM:].astype(BF16)
    gv_ref[...] = jnp.dot(a, wgv_ref[...], preferred_element_type=F32).astype(BF16)
    gr = jnp.dot(a, wgr_ref[...], preferred_element_type=F32)
    gr_ref[...] = (gr * _sigmoid(gr)).astype(BF16)
    gates = jnp.dot(a, wgate_ref[...], preferred_element_type=F32)
    sga_ref[...] = _sigmoid(gates[:, :D_MODEL]).astype(BF16)
    sgg_ref[...] = _sigmoid(gates[:, D_MODEL:]).astype(BF16)


def _in_proj(x2, posr, invfc, gain, wqt, wkz, wvt, wgqk, wgv, wgr, wgate, tm):
    T = x2.shape[0]
    row = lambda n: pl.BlockSpec((tm, n), lambda i: (i, 0))
    col = lambda n: pl.BlockSpec((n, tm), lambda i: (0, i))
    row_widths = (ATT_HEAD_DIM, ATT_HEAD_DIM, None, GLA_KEY_DIM, GLA_KEY_DIM, GLA_VAL_DIM,
                  GLA_VAL_DIM, wkz.shape[1] - ATT_KV_DIM, D_MODEL, D_MODEL)
    row_dtypes = (BF16,) * 7 + (F32, BF16, BF16)
    out_specs = [col(ATT_Q_DIM)]
    out_shape = [jax.ShapeDtypeStruct((ATT_Q_DIM, T), BF16)]
    for n, dt in zip(row_widths, row_dtypes):
        if n is None:
            out_specs.append(col(ATT_KV_DIM))
            out_shape.append(jax.ShapeDtypeStruct((ATT_KV_DIM, T), BF16))
        else:
            out_specs.append(row(n))
            out_shape.append(jax.ShapeDtypeStruct((T, n), dt))
    consts = (invfc, gain, wqt, wkz, wvt, wgqk, wgv, wgr, wgate)
    return pl.pallas_call(
        _in_proj_kernel,
        grid=(T // tm,),
        in_specs=[row(D_MODEL), col(1)] + [_full(c.shape) for c in consts],
        out_specs=out_specs,
        out_shape=out_shape,
        compiler_params=_cparams("in_proj", ("parallel",)),
        name="in_proj",
    )(x2, posr, *consts)


def _swa_kernel(sink_ref, qt_ref, k0_ref, k1_ref, vt_ref, o_ref, *, tq, seq):
    blk = WINDOW
    span = 3 * blk
    hd = ATT_HEAD_DIM
    n = pl.program_id(1)
    ones = jnp.ones((16, span), BF16)
    kv_refs = (k0_ref, k1_ref)

    def window_start(sb):
        return pl.multiple_of(jnp.clip(n * tq + (sb - 1) * blk, 0, seq - span), blk)

    def scores(sb, g):
        kw = kv_refs[g][0, pl.ds(window_start(sb), span), :]
        heads = range(g * ATT_GROUP, (g + 1) * ATT_GROUP)
        qs = jnp.concatenate([qt_ref[h * hd:(h + 1) * hd, sb * blk:(sb + 1) * blk] for h in heads], axis=1)
        return jnp.dot(kw, qs, preferred_element_type=F32)

    work = [(sb, g) for sb in range(tq // blk) for g in range(ATT_KV_HEADS)]
    s_next = scores(*work[0])
    outs = []
    for step, (sb, g) in enumerate(work):
        s_all = s_next
        if step + 1 < len(work):
            s_next = scores(*work[step + 1])
        q0 = n * tq + sb * blk
        start = window_start(sb)
        kj = start + lax.broadcasted_iota(jnp.int32, (span, blk), 0)
        qi = q0 + lax.broadcasted_iota(jnp.int32, (span, blk), 1)
        valid = jnp.abs(qi - kj) <= WINDOW
        vaug = jnp.concatenate([vt_ref[g * hd:(g + 1) * hd, pl.ds(start, span)], ones], axis=0)
        sinks = [sink_ref[g * ATT_GROUP + i] * LOG2E for i in range(ATT_GROUP)]
        ss = [jnp.where(valid, s_all[:, i * blk:(i + 1) * blk], -jnp.inf) for i in range(ATT_GROUP)]
        ms = [jnp.maximum(jnp.max(s, axis=0, keepdims=True), sink) for s, sink in zip(ss, sinks)]
        rs = [jnp.dot(vaug, jnp.exp2(s - m).astype(BF16), preferred_element_type=F32) for s, m in zip(ss, ms)]
        outs += [r[:hd] / (r[hd:hd + 1] + jnp.exp2(sink - m)) for r, m, sink in zip(rs, ms, sinks)]
        if g == ATT_KV_HEADS - 1:
            for pr in range(ATT_HEADS // 2):
                pair = jnp.concatenate([outs[2 * pr], outs[2 * pr + 1]], axis=0)
                o_ref[0, sb * blk:(sb + 1) * blk, pr * 2 * hd:(pr + 1) * 2 * hd] = pair.T.astype(BF16)
            outs = []


def _swa(sink, qt, k0, k1, vt, batch, tq):
    S = k0.shape[1]
    nq = S // tq
    kspec = pl.BlockSpec((1, S, ATT_HEAD_DIM), lambda b, n: (b, 0, 0))
    return pl.pallas_call(
        functools.partial(_swa_kernel, tq=tq, seq=S),
        grid=(batch, nq),
        in_specs=[pl.BlockSpec(memory_space=pltpu.SMEM),
                  pl.BlockSpec((ATT_Q_DIM, tq), lambda b, n: (0, b * nq + n)),
                  kspec, kspec,
                  pl.BlockSpec((ATT_KV_DIM, S), lambda b, n: (0, b))],
        out_specs=pl.BlockSpec((1, tq, ATT_Q_DIM), lambda b, n: (b, n, 0)),
        out_shape=jax.ShapeDtypeStruct((batch, S, ATT_Q_DIM), BF16),
        compiler_params=_cparams("swa", ("parallel", "parallel")),
        name="swa",
    )(sink, qt, k0, k1, vt)


def _log2_sigmoid(u):
    return jnp.minimum(u, 0.0) * LOG2E - jnp.log2(1.0 + jnp.exp2(jnp.abs(u) * -LOG2E))


def _split2(x):
    hi = x.astype(BF16)
    return hi, (x - hi.astype(F32)).astype(BF16)


GLA_WAYS = 4


def _gla_kernel(z_ref, q_ref, k_ref, v_ref, r_ref, upw_ref, bias_ref, gain_ref,
                o_ref, cf_ref, cb_ref, kef_ref, keb_ref, st_ref, s_ref, *, seq):
    L = GLA_CHUNK
    R2 = 2 * GLA_GATE_RANK
    nc = seq // L
    grp = 4 * L
    cpg = grp // L
    dk = GLA_DK
    mm = functools.partial(jnp.dot, preferred_element_type=F32)
    nt = functools.partial(lax.dot_general, dimension_numbers=NT_DIMS, preferred_element_type=F32)

    row = lax.broadcasted_iota(jnp.int32, (grp, grp), 0)
    col = lax.broadcasted_iota(jnp.int32, (grp, grp), 1)
    same = (row // L) == (col // L)
    fwd_mask = same & (col <= row)
    bwd_mask = same & (col > row)
    tri_lo = jnp.where(fwd_mask, 1.0, 0.0).astype(BF16)
    tri_up = jnp.where(same & (col >= row), 1.0, 0.0).astype(BF16)
    lane = lax.broadcasted_iota(jnp.int32, (1, LANES), 1)
    use_lo = (lane >= R2) & (lane < 2 * R2)

    def group_starts(i):
        return [pl.multiple_of((i * GLA_WAYS + w) * grp, grp) for w in range(GLA_WAYS)]

    def cum_body(i, carry):
        r0s = group_starts(i)
        zs = [_split2(z_ref[0, pl.ds(r0, grp), :]) for r0 in r0s]
        us = [mm(jnp.where(use_lo, zl, zh), upw_ref[...]) + bias_ref[...] for zh, zl in zs]
        las = [_split2(_log2_sigmoid(u) * (1.0 / GLA_GATE_NORM)) for u in us]
        cfxs = [mm(tri_lo, jnp.concatenate([lh[:, :dk], ll[:, :dk]], axis=1)) for lh, ll in las]
        cbxs = [mm(tri_up, jnp.concatenate([lh[:, dk:], ll[:, dk:]], axis=1)) for lh, ll in las]
        for r0, cfx, cbx in zip(r0s, cfxs, cbxs):
            cf = cfx[:, :dk] + cfx[:, dk:]
            cb = cbx[:, :dk] + cbx[:, dk:]
            cf_ref[pl.ds(r0, grp), :] = cf
            cb_ref[pl.ds(r0, grp), :] = cb
            k = k_ref[0, pl.ds(r0, grp), :].astype(F32)
            for c in range(cpg):
                sl = slice(c * L, (c + 1) * L)
                gf = cf[(c + 1) * L - 1:(c + 1) * L]
                gb = cb[c * L:c * L + 1]
                kef_ref[pl.ds(r0 + c * L, L), :] = (k[sl] * jnp.exp2(gf - cf[sl])).astype(BF16)
                keb_ref[pl.ds(r0 + c * L, L), :] = (k[sl] * jnp.exp2(gb - cb[sl])).astype(BF16)
        return carry

    lax.fori_loop(0, seq // (grp * GLA_WAYS), cum_body, 0, unroll=True)

    s_ref[...] = jnp.zeros_like(s_ref)
    zero_k = jnp.zeros((L, dk), BF16)

    def state_body(i, carry):
        j = nc - 1 - i
        rf = pl.multiple_of(i * L, L)
        rb = pl.multiple_of(j * L, L)
        vcat = jnp.concatenate([v_ref[0, pl.ds(rf, L), :], v_ref[0, pl.ds(rb, L), :]], axis=0)
        kblk = jnp.concatenate([jnp.concatenate([kef_ref[pl.ds(rf, L), :], zero_k], axis=1),
                                jnp.concatenate([zero_k, keb_ref[pl.ds(rb, L), :]], axis=1)], axis=0)
        kv = lax.dot_general(vcat, kblk, TN_DIMS, preferred_element_type=F32)
        decay = jnp.exp2(jnp.concatenate([cf_ref[pl.ds(rf + L - 1, 1), :], cb_ref[pl.ds(rb, 1), :]], axis=1))
        s = s_ref[...]
        st_ref[i, :, 0:dk] = s[:, :dk].astype(BF16)
        st_ref[j, :, dk:2 * dk] = s[:, dk:].astype(BF16)
        s_ref[...] = s * decay + kv
        return carry

    lax.fori_loop(0, nc, state_body, 0, unroll=True)

    def out_body(i, carry):
        r0s = group_starts(i)
        ops = []
        for r0 in r0s:
            q = q_ref[0, pl.ds(r0, grp), :].astype(F32)
            k = k_ref[0, pl.ds(r0, grp), :].astype(F32)
            cf = cf_ref[pl.ds(r0, grp), :]
            cb = cb_ref[pl.ds(r0, grp), :]
            ops.append(((q * jnp.exp2(cf)).astype(BF16), (k * jnp.exp2(-cf)).astype(BF16),
                        (q * jnp.exp2(cb)).astype(BF16), (k * jnp.exp2(-cb)).astype(BF16)))
        scores = [(nt(qf, kf), nt(qb, kb)) for qf, kf, qb, kb in ops]
        attns = [jnp.where(fwd_mask, af, jnp.where(bwd_mask, ab, 0.0)).astype(BF16) for af, ab in scores]
        outs = []
        for r0, attn, (qf, _, qb, _) in zip(r0s, attns, ops):
            c0 = r0 // L
            qcat = jnp.concatenate([qf, qb], axis=1)
            inter = jnp.concatenate([nt(qcat[c * L:(c + 1) * L], st_ref[c0 + c]) for c in range(cpg)], axis=0)
            outs.append(mm(attn, v_ref[0, pl.ds(r0, grp), :]) + inter)
        for r0, o in zip(r0s, outs):
            o_ref[0, pl.ds(r0, grp), :] = (_rms(o, gain_ref[...]) * r_ref[0, pl.ds(r0, grp), :].astype(F32)).astype(BF16)
        return carry

    lax.fori_loop(0, seq // (grp * GLA_WAYS), out_body, 0, unroll=True)


def _gla(z, gq, gk, gv, gr, upw, bias, gain):
    B, S, _ = gq.shape
    nc = S // GLA_CHUNK
    seq_blk = lambda n: pl.BlockSpec((1, S, n), lambda b, h: (b, 0, h))
    head_blk = lambda r, n: pl.BlockSpec((None, r, n), lambda b, h: (h, 0, 0))
    return pl.pallas_call(
        functools.partial(_gla_kernel, seq=S),
        grid=(B, GLA_HEADS),
        in_specs=[pl.BlockSpec((1, S, LANES), lambda b, h: (b, 0, 0)),
                  seq_blk(GLA_DK), seq_blk(GLA_DK), seq_blk(GLA_DV), seq_blk(GLA_DV),
                  head_blk(LANES, 2 * GLA_DK), head_blk(1, 2 * GLA_DK),
                  pl.BlockSpec((1, GLA_DV), lambda b, h: (0, h))],
        out_specs=seq_blk(GLA_DV),
        out_shape=jax.ShapeDtypeStruct((B, S, GLA_VAL_DIM), BF16),
        scratch_shapes=[pltpu.VMEM((S, GLA_DK), F32), pltpu.VMEM((S, GLA_DK), F32),
                        pltpu.VMEM((S, GLA_DK), BF16), pltpu.VMEM((S, GLA_DK), BF16),
                        pltpu.VMEM((nc, GLA_DV, 2 * GLA_DK), BF16),
                        pltpu.VMEM((GLA_DV, 2 * GLA_DK), F32)],
        compiler_params=_cparams("gla", ("parallel", "parallel")),
        name="gla",
    )(z, gq, gk, gv, gr, upw, bias, gain)


def _mix_out_kernel(x_ref, a_ref, g_ref, sga_ref, sgg_ref, wa_ref, wb_ref, wo_ref, gain_ref, wr_ref,
                    h_ref, xn_ref, aff_ref, afft_ref):
    sub, starts = _sub_blocks(x_ref.shape[0])
    mm = functools.partial(jnp.dot, preferred_element_type=F32)
    blk = lambda ref, r0: ref[r0:r0 + sub, :]
    pad = jnp.zeros((LANES - N_EXPERTS, sub), F32)

    y_att = [mm(blk(a_ref, r0), wa_ref[...]) for r0 in starts]
    y_gla = [mm(blk(g_ref, r0), wb_ref[...]) for r0 in starts]
    merged = [(blk(sga_ref, r0).astype(F32) * ya + blk(sgg_ref, r0).astype(F32) * yg).astype(BF16)
              for r0, ya, yg in zip(starts, y_att, y_gla)]
    hs = [blk(x_ref, r0) + mm(m, wo_ref[...]) for r0, m in zip(starts, merged)]
    xns = [_rms(h, gain_ref[...]) for h in hs]
    for r0, h, xn in zip(starts, hs, xns):
        h_ref[r0:r0 + sub, :] = h
        _store_token_tiles(xn_ref, xn, r0)
    his = [xn.astype(BF16) for xn in xns]
    parts = [lax.dot_general(wr_ref[...], jnp.concatenate([hi, (xn - hi.astype(F32)).astype(BF16)], axis=1),
                             NT_DIMS, preferred_element_type=F32) for xn, hi in zip(xns, his)]
    for r0, part in zip(starts, parts):
        logits = part[:N_EXPERTS] + part[N_EXPERTS:]
        e = jnp.exp(logits - jnp.max(logits, axis=0, keepdims=True))
        aff_t = e / jnp.sum(e, axis=0, keepdims=True)
        afft_ref[:, r0:r0 + sub] = aff_t
        aff_ref[r0:r0 + sub, :] = jnp.concatenate([aff_t, pad], axis=0).T[:, :N_EXPERTS]


def _mix_out(x2, a, g, sga, sgg, wa, wb, wo, gain, wr, tm):
    T = x2.shape[0]
    row = lambda n: pl.BlockSpec((tm, n), lambda i: (i, 0))
    return pl.pallas_call(
        _mix_out_kernel,
        grid=(T // tm,),
        in_specs=[row(D_MODEL), row(ATT_Q_DIM), row(GLA_VAL_DIM), row(D_MODEL), row(D_MODEL),
                  _full(wa.shape), _full(wb.shape), _full(wo.shape), _full(gain.shape), _full(wr.shape)],
        out_specs=[row(D_MODEL), pl.BlockSpec((tm * ROW_TILES, LANES), lambda i: (i, 0)), row(N_EXPERTS),
                   pl.BlockSpec((N_EXPERTS, tm), lambda i: (0, i))],
        out_shape=[jax.ShapeDtypeStruct((T, D_MODEL), F32), jax.ShapeDtypeStruct((T * ROW_TILES, LANES), F32),
                   jax.ShapeDtypeStruct((T, N_EXPERTS), F32), jax.ShapeDtypeStruct((N_EXPERTS, T), F32)],
        compiler_params=_cparams("mix_out", ("parallel",)),
        name="mix_out",
    )(x2, a, g, sga, sgg, wa, wb, wo, gain, wr)


ROUTE_WAYS = 4


def _route_kernel(aff_ref, idx_ref, cum_ref, *, cap, seq):
    E = N_EXPERTS
    aff = aff_ref[...]
    count = lambda mask: jnp.sum(mask.astype(jnp.int32), axis=1, keepdims=True)
    as_float = lambda pattern: lax.bitcast_convert_type(pattern, F32)

    def thr_body(t, pattern):
        cand = pattern | jnp.left_shift(jnp.int32(1), 30 - t)
        return jnp.where(count(aff >= as_float(cand)) >= cap, cand, pattern)

    thr = as_float(lax.fori_loop(0, 31, thr_body, jnp.zeros((E, 1), jnp.int32)))
    above = aff > thr
    tie = aff == thr
    need = cap - count(above)

    pos = lax.broadcasted_iota(jnp.int32, (E, seq), 1)

    def tie_body(t, last):
        cand = last | jnp.left_shift(jnp.int32(1), (seq.bit_length() - 2) - t)
        return jnp.where(count(tie & (pos < cand)) < need, cand, last)

    last = lax.fori_loop(0, seq.bit_length() - 1, tie_body, jnp.zeros((E, 1), jnp.int32))
    sel = (above | (tie & (pos <= last))).astype(BF16)

    nt = seq // LANES
    lrow = lax.broadcasted_iota(jnp.int32, (LANES, LANES), 0)
    lcol = lax.broadcasted_iota(jnp.int32, (LANES, LANES), 1)
    tri = (lrow <= lcol).astype(BF16)
    mm = functools.partial(jnp.dot, preferred_element_type=F32)
    for t in range(nt):
        cum_ref[t * E:(t + 1) * E, :] = mm(sel[:, t * LANES:(t + 1) * LANES], tri)
    tile_of = (lax.broadcasted_iota(jnp.int32, (seq, LANES), 0) // LANES
               == lax.broadcasted_iota(jnp.int32, (seq, LANES), 1)).astype(BF16)
    per_tile = mm(sel, tile_of)
    lane = lax.broadcasted_iota(jnp.int32, (1, LANES), 1)
    far = jnp.float32(2 * seq)
    t_end = jnp.where(lane < nt, mm(per_tile.astype(BF16), tri), far)
    t_start = jnp.where(lane < nt, t_end - per_tile, far)
    pad = jnp.zeros((LANES - E, LANES), F32)
    t_start_cols = jnp.concatenate([jnp.where(lane < nt, t_start, 0.0), pad], axis=0).T

    slot = lax.broadcasted_iota(jnp.int32, (cap, LANES), 0).astype(F32)
    ones = jnp.ones((LANES, LANES), BF16)
    zrows = jnp.zeros((LANES - nt, 2 * LANES), F32)
    for e0 in range(0, E, ROUTE_WAYS):
        es = range(e0, e0 + ROUTE_WAYS)
        tiles, picks, whole = [], [], []
        for e in es:
            absc = cum_ref[pl.ds(e, nt, stride=E), :] + t_start_cols[0:nt, e:e + 1]
            hi = jnp.where(absc >= 256.0, 1.0, 0.0) + jnp.where(absc >= 512.0, 1.0, 0.0)
            lo = absc - 256.0 * hi
            tiles.append(jnp.concatenate([jnp.concatenate([lo, hi], axis=1), zrows], axis=0).astype(BF16))
            done = jnp.where(t_end[e:e + 1] <= slot, 1.0, 0.0)
            whole.append(done)
            picks.append((jnp.where(t_start[e:e + 1] <= slot, 1.0, 0.0) - done).astype(BF16))
        rows = [mm(p, w) for p, w in zip(picks, tiles)]
        votes = [(jnp.where(r[:, :LANES] + 256.0 * r[:, LANES:] <= slot, 1.0, 0.0) + float(LANES) * d).astype(BF16)
                 for r, d in zip(rows, whole)]
        for e, v in zip(es, votes):
            idx_ref[0, :, e:e + 1] = mm(v, ones)[:, e:e + 1].astype(jnp.int32)


def _route(aff_t, batch, cap):
    E, T = aff_t.shape
    B, S = batch, T // batch
    return pl.pallas_call(
        functools.partial(_route_kernel, cap=cap, seq=S),
        grid=(B,),
        in_specs=[pl.BlockSpec((E, S), lambda b: (0, b))],
        out_specs=pl.BlockSpec((1, cap, E), lambda b: (b, 0, 0)),
        out_shape=jax.ShapeDtypeStruct((B, cap, E), jnp.int32),
        scratch_shapes=[pltpu.VMEM((S // LANES * E, LANES), F32)],
        compiler_params=_cparams("route", ("parallel",)),
        name="route",
    )(aff_t)


MOE_EXPERTS_PER_STEP = 4


def _gather_kernel(idx_ref, xn_ref, aff_ref, xg_ref, wg_ref, ws_ref, *, cap):
    n = MOE_EXPERTS_PER_STEP
    lane = lax.broadcasted_iota(jnp.int32, (cap, N_EXPERTS), 1)
    for j in range(n):
        for i in range(cap):
            t = idx_ref[j, 0, i]
            src = pl.multiple_of(t * SUBLANES, SUBLANES)
            xg_ref[0, j, i * SUBLANES:(i + 1) * SUBLANES, :] = xn_ref[0, pl.ds(src, SUBLANES), :]
            ws_ref[j, i:i + 1, :] = aff_ref[0, pl.ds(t, 1), :]
        e = pl.program_id(1) * n + j
        wg_ref[0, j] = jnp.sum(jnp.where(lane == e, ws_ref[j], 0.0), axis=1, keepdims=True)


def _gather(idx, xn_tiles, aff, cap):
    B, S, E = aff.shape
    n = MOE_EXPERTS_PER_STEP
    return pl.pallas_call(
        functools.partial(_gather_kernel, cap=cap),
        grid=(B, E // n),
        in_specs=[pl.BlockSpec((n, 1, cap), lambda b, e: (b * (E // n) + e, 0, 0), memory_space=pltpu.SMEM),
                  pl.BlockSpec((1, S * SUBLANES, LANES), lambda b, e: (b, 0, 0)),
                  pl.BlockSpec((1, S, E), lambda b, e: (b, 0, 0))],
        out_specs=[pl.BlockSpec((1, n, cap * SUBLANES, LANES), lambda b, e: (b, e, 0, 0)),
                   pl.BlockSpec((1, n, cap, 1), lambda b, e: (b, e, 0, 0))],
        out_shape=[jax.ShapeDtypeStruct((B, E, cap * SUBLANES, LANES), F32),
                   jax.ShapeDtypeStruct((B, E, cap, 1), F32)],
        scratch_shapes=[pltpu.VMEM((n, cap, E), F32)],
        compiler_params=_cparams("gather", ("arbitrary", "arbitrary")),
        name="gather",
    )(idx.reshape(B * E, 1, cap), xn_tiles, aff)


def _ffn_kernel(xg_ref, wg_ref, w1_ref, w2_ref, w3_ref, y_ref, b1_ref, b2_ref, b3_ref):
    @pl.when(pl.program_id(1) == 0)
    def _():
        b1_ref[...] = w1_ref[0].astype(BF16)
        b2_ref[...] = w2_ref[0].astype(BF16)
        b3_ref[...] = w3_ref[0].astype(BF16)

    rows = wg_ref.shape[2]
    seqs = range(xg_ref.shape[0])
    mm = functools.partial(jnp.dot, preferred_element_type=F32)
    xgs = [_load_token_tiles(xg_ref.at[i, 0], rows).astype(BF16) for i in seqs]
    gates = [mm(xg, b1_ref[...]) for xg in xgs]
    ups = [mm(xg, b2_ref[...]) for xg in xgs]
    hids = [(gate * _sigmoid(gate) * up).astype(BF16) for gate, up in zip(gates, ups)]
    ys = [mm(hid, b3_ref[...]) * wg_ref[i, 0] for i, hid in zip(seqs, hids)]
    for i, y in zip(seqs, ys):
        _store_token_tiles(y_ref.at[i, 0], y)


FFN_SEQS_PER_STEP = 2


def _ffn(xg, wg, w1, w2, w3):
    B, E, C, _ = wg.shape
    _, D, F = w1.shape
    n = FFN_SEQS_PER_STEP
    return pl.pallas_call(
        _ffn_kernel,
        grid=(E, B // n),
        in_specs=[pl.BlockSpec((n, 1, C * ROW_TILES, LANES), lambda e, b: (b, e, 0, 0)),
                  pl.BlockSpec((n, 1, C, 1), lambda e, b: (b, e, 0, 0)),
                  pl.BlockSpec((1, D, F), lambda e, b: (e, 0, 0)),
                  pl.BlockSpec((1, D, F), lambda e, b: (e, 0, 0)),
                  pl.BlockSpec((1, F, D), lambda e, b: (e, 0, 0))],
        out_specs=pl.BlockSpec((n, 1, C * ROW_TILES, LANES), lambda e, b: (b, e, 0, 0)),
        out_shape=jax.ShapeDtypeStruct((B, E, C * ROW_TILES, LANES), F32),
        scratch_shapes=[pltpu.VMEM((D, F), BF16), pltpu.VMEM((D, F), BF16), pltpu.VMEM((F, D), BF16)],
        compiler_params=_cparams("ffn", ("arbitrary", "arbitrary")),
        name="ffn",
    )(xg, wg, w1, w2, w3)


SCATTER_BATCH = 16


def _scatter_kernel(idx_ref, y_ref, o_ref, *, cap):
    @pl.when(pl.program_id(1) == 0)
    def _():
        o_ref[...] = jnp.zeros_like(o_ref)

    for j in range(MOE_EXPERTS_PER_STEP):
        for i0 in range(0, cap, SCATTER_BATCH):
            slots = range(i0, i0 + SCATTER_BATCH)
            rows = [pl.ds(pl.multiple_of(idx_ref[j, 0, i] * SUBLANES, SUBLANES), SUBLANES) for i in slots]
            new = [o_ref[0, r, :] + y_ref[0, j, i * SUBLANES:(i + 1) * SUBLANES, :] for r, i in zip(rows, slots)]
            for r, v in zip(rows, new):
                o_ref[0, r, :] = v


def _scatter(idx, y_tiles, seq):
    B, E, cap = idx.shape
    n = MOE_EXPERTS_PER_STEP
    return pl.pallas_call(
        functools.partial(_scatter_kernel, cap=cap),
        grid=(B, E // n),
        in_specs=[pl.BlockSpec((n, 1, cap), lambda b, e: (b * (E // n) + e, 0, 0), memory_space=pltpu.SMEM),
                  pl.BlockSpec((1, n, cap * SUBLANES, LANES), lambda b, e: (b, e, 0, 0))],
        out_specs=pl.BlockSpec((1, seq * SUBLANES, LANES), lambda b, e: (b, 0, 0)),
        out_shape=jax.ShapeDtypeStruct((B, seq * SUBLANES, LANES), F32),
        compiler_params=_cparams("scatter", ("arbitrary", "arbitrary")),
        name="scatter",
    )(idx.reshape(B * E, 1, cap), y_tiles)


def _ple_out_kernel(h_ref, moe_ref, p_ref, gple_ref, wpg_ref, wple_ref, gfin_ref, o_ref):
    sub, starts = _sub_blocks(h_ref.shape[0])
    mm = functools.partial(jnp.dot, preferred_element_type=F32)
    hs = [h_ref[r0:r0 + sub, :] + _load_token_tiles(moe_ref, sub, r0) for r0 in starts]
    ns = [_rms(h, gple_ref[...]).astype(BF16) for h in hs]
    gates = [_sigmoid(mm(n, wpg_ref[...])) for n in ns]
    embs = [mm(p_ref[r0:r0 + sub, :].astype(BF16), wple_ref[...]) for r0 in starts]
    for r0, h, gate, emb in zip(starts, hs, gates, embs):
        o_ref[r0:r0 + sub, :] = _rms(h + gate * emb, gfin_ref[...])


def _ple_out(h, moe, p2, gple, wpg, wple, gfin, tm):
    T = h.shape[0]
    row = lambda n: pl.BlockSpec((tm, n), lambda i: (i, 0))
    return pl.pallas_call(
        _ple_out_kernel,
        grid=(T // tm,),
        in_specs=[row(D_MODEL), pl.BlockSpec((tm * ROW_TILES, LANES), lambda i: (i, 0)), row(PLE_DIM),
                  _full(gple.shape), _full(wpg.shape), _full(wple.shape), _full(gfin.shape)],
        out_specs=row(D_MODEL),
        out_shape=jax.ShapeDtypeStruct((T, D_MODEL), F32),
        compiler_params=_cparams("ple_out", ("parallel",)),
        name="ple_out",
    )(h, moe, p2, gple, wpg, wple, gfin)


def kernel(x, p, positions, norm_mix, w_in, gla_gate_up_fwd, gla_gate_bias_fwd, gla_gate_up_bwd, gla_gate_bias_bwd, attn_sink, gla_norm, w_branch_attn, w_branch_gla, w_out, norm_ffn, w_router, w_exp_gate, w_exp_up, w_exp_down, norm_ple, w_ple_gate, w_ple, norm_final):
    B, S, D = x.shape
    T = B * S
    depth = w_in.shape[0]
    assert depth == 1, "the final norm is fused into the (single) layer's PLE kernel"
    cap = CAPACITY_FACTOR * S // N_EXPERTS
    R = GLA_GATE_RANK

    posr = positions.reshape(1, T)
    inv_freq = ROPE_THETA ** (-jnp.arange(0, ROPE_DIM, 2, dtype=F32) / ROPE_DIM)
    invfc = inv_freq.reshape(ROPE_DIM // 2, 1)

    h = x.reshape(T, D)
    for l in range(depth):
        o = 0
        cols = {}
        for name, n in (("q", ATT_Q_DIM), ("k", ATT_KV_DIM), ("v", ATT_KV_DIM), ("gqk", 2 * GLA_KEY_DIM),
                        ("gv", GLA_VAL_DIM), ("gr", GLA_VAL_DIM), ("z", 2 * R), ("gate", 2 * D_MODEL)):
            cols[name] = w_in[l][:, o:o + n].astype(BF16)
            o += n
        per_head = lambda w: w.reshape(-1, GLA_HEADS, GLA_DK).swapaxes(0, 1)
        upf, upb = per_head(gla_gate_up_fwd[l]), per_head(gla_gate_up_bwd[l])
        up = jnp.concatenate([jnp.concatenate([upf, jnp.zeros_like(upf)], axis=2),
                              jnp.concatenate([jnp.zeros_like(upb), upb], axis=2)], axis=1)
        up_hi = up.astype(BF16)
        up_lo = (up - up_hi.astype(F32)).astype(BF16)
        upw = jnp.concatenate([up_hi, up_hi, up_lo, jnp.zeros_like(up_lo)], axis=1)
        wz4 = jnp.tile(cols["z"], (1, LANES // (2 * R)))
        gbias = jnp.concatenate([per_head(gla_gate_bias_fwd[l]), per_head(gla_gate_bias_bwd[l])], axis=2)
        wr = w_router[l]
        wr_hi = wr.astype(BF16)
        wr_lo = (wr - wr_hi.astype(F32)).astype(BF16)
        wr2 = jnp.concatenate([jnp.concatenate([wr_hi, wr_lo], axis=1),
                               jnp.concatenate([wr_hi, jnp.zeros_like(wr_lo)], axis=1)], axis=0).T

        qt, k0, k1, vt, gq, gk, gv, gr, z, sga, sgg = _in_proj(
            h, posr, invfc, norm_mix[l].reshape(1, D), cols["q"].T,
            jnp.concatenate([cols["k"], wz4], axis=1), cols["v"].T,
            cols["gqk"], cols["gv"], cols["gr"], cols["gate"], tm=TILE_ROWS["in_proj"])

        att = _swa(attn_sink[l], qt, k0.reshape(B, S, -1), k1.reshape(B, S, -1), vt, batch=B, tq=TILE_ROWS["swa"])
        gla = _gla(z.reshape(B, S, -1), gq.reshape(B, S, -1), gk.reshape(B, S, -1), gv.reshape(B, S, -1),
                   gr.reshape(B, S, -1), upw, gbias, gla_norm[l].reshape(1, -1))

        h1, xn, aff, aff_t = _mix_out(h, att.reshape(T, -1), gla.reshape(T, -1), sga, sgg,
                                      w_branch_attn[l].astype(BF16), w_branch_gla[l].astype(BF16),
                                      w_out[l].astype(BF16), norm_ffn[l].reshape(1, D), wr2, tm=TILE_ROWS["mix_out"])

        aff3 = aff.reshape(B, S, N_EXPERTS)
        idx = _route(aff_t, B, cap)
        idx = jnp.swapaxes(idx, 1, 2)
        xg, wg = _gather(idx, xn.reshape(B, S * SUBLANES, LANES), aff3, cap)
        y = _ffn(xg, wg, w_exp_gate[l], w_exp_up[l], w_exp_down[l])
        moe = _scatter(idx, y, S)

        h = _ple_out(h1, moe.reshape(T * ROW_TILES, LANES), p[l].reshape(T, PLE_DIM), norm_ple[l].reshape(1, D),
                     w_ple_gate[l].astype(BF16), w_ple[l].astype(BF16), norm_final.reshape(1, D),
                     tm=TILE_ROWS["ple_out"])
    return h.reshape(B, S, D)
```

```python
import functools
import math

import jax
import jax.numpy as jnp
from jax import lax
from jax.experimental import pallas as pl
from jax.experimental.pallas import tpu as pltpu

D_MODEL = 1024
ATT_HEADS = 8
ATT_KV_HEADS = 2
ATT_HEAD_DIM = 64
ATT_GROUP = ATT_HEADS // ATT_KV_HEADS
ATT_Q_DIM = ATT_HEADS * ATT_HEAD_DIM
ATT_KV_DIM = ATT_KV_HEADS * ATT_HEAD_DIM
WINDOW = 128
ROPE_DIM = ATT_HEAD_DIM // 4
ROPE_THETA = 500000.0
GLA_HEADS = 4
GLA_KEY_DIM = D_MODEL // 2
GLA_VAL_DIM = D_MODEL
GLA_DK = GLA_KEY_DIM // GLA_HEADS
GLA_DV = GLA_VAL_DIM // GLA_HEADS
GLA_GATE_RANK = 16
GLA_GATE_NORM = 16.0
GLA_CHUNK = 64
N_EXPERTS = 16
EXPERT_FF = D_MODEL
CAPACITY_FACTOR = 2
PLE_DIM = 256
EPS = 1e-6

LANES = 128
MIB = 1024 * 1024
BF16 = jnp.bfloat16
F32 = jnp.float32
LOG2E = math.log2(math.e)

NT_DIMS = (((1,), (1,)), ((), ()))
TN_DIMS = (((0,), (0,)), ((), ()))

TILE_ROWS = {"w_prep": 256, "in_proj": 512, "swa": 1024, "mix_out": 512, "ple_out": 1024}
VMEM_LIMIT_MIB = {"w_prep": 40, "in_proj": 56, "swa": 32, "gla": 48, "mix_out": 48, "route": 32, "gather": 60,
                  "ffn": 60, "scatter": 60, "ple_out": 48}


def _cparams(name, sem):
    return pltpu.CompilerParams(dimension_semantics=sem, vmem_limit_bytes=VMEM_LIMIT_MIB[name] * MIB)


def _full(shape):
    n = len(shape)
    return pl.BlockSpec(shape, lambda *_: (0,) * n)


def _rms(x, gain):
    ms = jnp.mean(x * x, axis=-1, keepdims=True)
    return x * lax.rsqrt(ms + EPS) * gain


def _sigmoid(x):
    return 0.5 * jnp.tanh(0.5 * x) + 0.5


SUBLANES = 8
ROW_TILES = D_MODEL // LANES


def _store_token_tiles(ref2d, x, first_row=0):
    rows = x.shape[0]
    for j in range(ROW_TILES):
        ref2d[pl.ds(first_row * ROW_TILES + j, rows, stride=ROW_TILES), :] = x[:, j * LANES:(j + 1) * LANES]


def _load_token_tiles(ref2d, rows, first_row=0):
    return jnp.concatenate([ref2d[pl.ds(first_row * ROW_TILES + j, rows, stride=ROW_TILES), :]
                            for j in range(ROW_TILES)], axis=1)


SUB_ROWS = 256


def _sub_blocks(tile_rows):
    return SUB_ROWS, list(range(0, tile_rows, SUB_ROWS))


def _rope_rows(t, cos_r, sin_r, heads):
    half = ROPE_DIM // 2
    rows = []
    for h in range(heads):
        r0 = h * ATT_HEAD_DIM
        t1, t2 = t[r0:r0 + half], t[r0 + half:r0 + ROPE_DIM]
        rows += [t1 * cos_r - t2 * sin_r, t2 * cos_r + t1 * sin_r, t[r0 + ROPE_DIM:r0 + ATT_HEAD_DIM]]
    return jnp.concatenate(rows, axis=0)


def _in_proj_kernel(x_ref, posr_ref, invfc_ref, gain_ref, wqt_ref, wkz_ref, wvt_ref,
                    wgqk_ref, wgv_ref, wgr_ref, wgate_ref,
                    qt_ref, k0_ref, k1_ref, vt_ref, gq_ref, gk_ref, gv_ref, gr_ref, z_ref, sga_ref, sgg_ref):
    a = _rms(x_ref[...], gain_ref[...]).astype(BF16)
    ang_t = invfc_ref[...] * posr_ref[...].astype(F32)
    cos_r, sin_r = jnp.cos(ang_t), jnp.sin(ang_t)

    qt = lax.dot_general(wqt_ref[...], a, NT_DIMS, preferred_element_type=F32)
    qt_ref[...] = (_rope_rows(qt, cos_r, sin_r, ATT_HEADS) * (ATT_HEAD_DIM ** -0.5 * LOG2E)).astype(BF16)

    kz = jnp.dot(a, wkz_ref[...], preferred_element_type=F32)
    z_ref[...] = kz[:, ATT_KV_DIM:]
    k = _rope_rows(kz[:, :ATT_KV_DIM].T, cos_r, sin_r, ATT_KV_HEADS).T.astype(BF16)
    k0_ref[...] = k[:, :ATT_HEAD_DIM]
    k1_ref[...] = k[:, ATT_HEAD_DIM:]
    vt_ref[...] = lax.dot_general(wvt_ref[...], a, NT_DIMS, preferred_element_type=F32).astype(BF16)

    gqk = jnp.dot(a, wgqk_ref[...], preferred_element_type=F32)
    gq_ref[...] = (gqk[:, :GLA_KEY_DIM] * (GLA_DK ** -0.5)).astype(BF16)
    gk_ref[...] = gqk[:, GLA_KEY_DIM:].astype(BF16)
    gv_ref[...] = jnp.dot(a, wgv_ref[...], preferred_element_type=F32).astype(BF16)
    gr = jnp.dot(a, wgr_ref[...], preferred_element_type=F32)
    gr_ref[...] = (gr * _sigmoid(gr)).astype(BF16)
    gates = jnp.dot(a, wgate_ref[...], preferred_element_type=F32)
    sga_ref[...] = _sigmoid(gates[:, :D_MODEL]).astype(BF16)
    sgg_ref[...] = _sigmoid(gates[:, D_MODEL:]).astype(BF16)


def _in_proj(x2, posr, invfc, gain, wqt, wkz, wvt, wgqk, wgv, wgr, wgate, tm):
    T = x2.shape[0]
    row = lambda n: pl.BlockSpec((tm, n), lambda i: (i, 0))
    col = lambda n: pl.BlockSpec((n, tm), lambda i: (0, i))
    row_widths = (ATT_HEAD_DIM, ATT_HEAD_DIM, None, GLA_KEY_DIM, GLA_KEY_DIM, GLA_VAL_DIM,
                  GLA_VAL_DIM, wkz.shape[1] - ATT_KV_DIM, D_MODEL, D_MODEL)
    row_dtypes = (BF16,) * 7 + (F32, BF16, BF16)
    out_specs = [col(ATT_Q_DIM)]
    out_shape = [jax.ShapeDtypeStruct((ATT_Q_DIM, T), BF16)]
    for n, dt in zip(row_widths, row_dtypes):
        if n is None:
            out_specs.append(col(ATT_KV_DIM))
            out_shape.append(jax.ShapeDtypeStruct((ATT_KV_DIM, T), BF16))
        else:
            out_specs.append(row(n))
            out_shape.append(jax.ShapeDtypeStruct((T, n), dt))
    consts = (invfc, gain, wqt, wkz, wvt, wgqk, wgv, wgr, wgate)
    return pl.pallas_call(
        _in_proj_kernel,
        grid=(T // tm,),
        in_specs=[row(D_MODEL), col(1)] + [_full(c.shape) for c in consts],
        out_specs=out_specs,
        out_shape=out_shape,
        compiler_params=_cparams("in_proj", ("parallel",)),
        name="in_proj",
    )(x2, posr, *consts)


def _swa_kernel(sink_ref, qt_ref, k0_ref, k1_ref, vt_ref, o_ref, *, tq, seq):
    blk = WINDOW
    span = 3 * blk
    hd = ATT_HEAD_DIM
    n = pl.program_id(1)
    ones = jnp.ones((16, span), BF16)
    kv_refs = (k0_ref, k1_ref)

    def window_start(sb):
        return pl.multiple_of(jnp.clip(n * tq + (sb - 1) * blk, 0, seq - span), blk)

    def scores(sb, g):
        kw = kv_refs[g][0, pl.ds(window_start(sb), span), :]
        heads = range(g * ATT_GROUP, (g + 1) * ATT_GROUP)
        qs = jnp.concatenate([qt_ref[h * hd:(h + 1) * hd, sb * blk:(sb + 1) * blk] for h in heads], axis=1)
        return jnp.dot(kw, qs, preferred_element_type=F32)

    work = [(sb, g) for sb in range(tq // blk) for g in range(ATT_KV_HEADS)]
    s_next = scores(*work[0])
    outs = []
    for step, (sb, g) in enumerate(work):
        s_all = s_next
        if step + 1 < len(work):
            s_next = scores(*work[step + 1])
        q0 = n * tq + sb * blk
        start = window_start(sb)
        kj = start + lax.broadcasted_iota(jnp.int32, (span, blk), 0)
        qi = q0 + lax.broadcasted_iota(jnp.int32, (span, blk), 1)
        valid = jnp.abs(qi - kj) <= WINDOW
        vaug = jnp.concatenate([vt_ref[g * hd:(g + 1) * hd, pl.ds(start, span)], ones], axis=0)
        sinks = [sink_ref[g * ATT_GROUP + i] * LOG2E for i in range(ATT_GROUP)]
        ss = [jnp.where(valid, s_all[:, i * blk:(i + 1) * blk], -jnp.inf) for i in range(ATT_GROUP)]
        ms = [jnp.maximum(jnp.max(s, axis=0, keepdims=True), sink) for s, sink in zip(ss, sinks)]
        rs = [jnp.dot(vaug, jnp.exp2(s - m).astype(BF16), preferred_element_type=F32) for s, m in zip(ss, ms)]
        outs += [r[:hd] / (r[hd:hd + 1] + jnp.exp2(sink - m)) for r, m, sink in zip(rs, ms, sinks)]
        if g == ATT_KV_HEADS - 1:
            for pr in range(ATT_HEADS // 2):
                pair = jnp.concatenate([outs[2 * pr], outs[2 * pr + 1]], axis=0)
                o_ref[0, sb * blk:(sb + 1) * blk, pr * 2 * hd:(pr + 1) * 2 * hd] = pair.T.astype(BF16)
            outs = []


def _swa(sink, qt, k0, k1, vt, batch, tq):
    S = k0.shape[1]
    nq = S // tq
    kspec = pl.BlockSpec((1, S, ATT_HEAD_DIM), lambda b, n: (b, 0, 0))
    return pl.pallas_call(
        functools.partial(_swa_kernel, tq=tq, seq=S),
        grid=(batch, nq),
        in_specs=[pl.BlockSpec(memory_space=pltpu.SMEM),
                  pl.BlockSpec((ATT_Q_DIM, tq), lambda b, n: (0, b * nq + n)),
                  kspec, kspec,
                  pl.BlockSpec((ATT_KV_DIM, S), lambda b, n: (0, b))],
        out_specs=pl.BlockSpec((1, tq, ATT_Q_DIM), lambda b, n: (b, n, 0)),
        out_shape=jax.ShapeDtypeStruct((batch, S, ATT_Q_DIM), BF16),
        compiler_params=_cparams("swa", ("parallel", "parallel")),
        name="swa",
    )(sink, qt, k0, k1, vt)


def _log2_sigmoid(u):
    return jnp.minimum(u, 0.0) * LOG2E - jnp.log2(1.0 + jnp.exp2(jnp.abs(u) * -LOG2E))


def _split2(x):
    hi = x.astype(BF16)
    return hi, (x - hi.astype(F32)).astype(BF16)


GLA_WAYS = 4


def _gla_kernel(z_ref, q_ref, k_ref, v_ref, r_ref, upw_ref, bias_ref, gain_ref,
                o_ref, cf_ref, cb_ref, kef_ref, keb_ref, st_ref, s_ref, *, seq):
    L = GLA_CHUNK
    R2 = 2 * GLA_GATE_RANK
    nc = seq // L
    grp = 4 * L
    cpg = grp // L
    dk = GLA_DK
    mm = functools.partial(jnp.dot, preferred_element_type=F32)
    nt = functools.partial(lax.dot_general, dimension_numbers=NT_DIMS, preferred_element_type=F32)

    row = lax.broadcasted_iota(jnp.int32, (grp, grp), 0)
    col = lax.broadcasted_iota(jnp.int32, (grp, grp), 1)
    same = (row // L) == (col // L)
    fwd_mask = same & (col <= row)
    bwd_mask = same & (col > row)
    tri_lo = jnp.where(fwd_mask, 1.0, 0.0).astype(BF16)
    tri_up = jnp.where(same & (col >= row), 1.0, 0.0).astype(BF16)
    lane = lax.broadcasted_iota(jnp.int32, (1, LANES), 1)
    use_lo = (lane >= R2) & (lane < 2 * R2)

    def group_starts(i):
        return [pl.multiple_of((i * GLA_WAYS + w) * grp, grp) for w in range(GLA_WAYS)]

    def cum_body(i, carry):
        r0s = group_starts(i)
        zs = [_split2(z_ref[0, pl.ds(r0, grp), :]) for r0 in r0s]
        us = [mm(jnp.where(use_lo, zl, zh), upw_ref[...]) + bias_ref[...] for zh, zl in zs]
        las = [_split2(_log2_sigmoid(u) * (1.0 / GLA_GATE_NORM)) for u in us]
        cfxs = [mm(tri_lo, jnp.concatenate([lh[:, :dk], ll[:, :dk]], axis=1)) for lh, ll in las]
        cbxs = [mm(tri_up, jnp.concatenate([lh[:, dk:], ll[:, dk:]], axis=1)) for lh, ll in las]
        for r0, cfx, cbx in zip(r0s, cfxs, cbxs):
            cf = cfx[:, :dk] + cfx[:, dk:]
            cb = cbx[:, :dk] + cbx[:, dk:]
            cf_ref[pl.ds(r0, grp), :] = cf
            cb_ref[pl.ds(r0, grp), :] = cb
            k = k_ref[0, pl.ds(r0, grp), :].astype(F32)
            for c in range(cpg):
                sl = slice(c * L, (c + 1) * L)
                gf = cf[(c + 1) * L - 1:(c + 1) * L]
                gb = cb[c * L:c * L + 1]
                kef_ref[pl.ds(r0 + c * L, L), :] = (k[sl] * jnp.exp2(gf - cf[sl])).astype(BF16)
                keb_ref[pl.ds(r0 + c * L, L), :] = (k[sl] * jnp.exp2(gb - cb[sl])).astype(BF16)
        return carry

    lax.fori_loop(0, seq // (grp * GLA_WAYS), cum_body, 0, unroll=True)

    s_ref[...] = jnp.zeros_like(s_ref)
    zero_k = jnp.zeros((L, dk), BF16)

    def state_body(i, carry):
        j = nc - 1 - i
        rf = pl.multiple_of(i * L, L)
        rb = pl.multiple_of(j * L, L)
        vcat = jnp.concatenate([v_ref[0, pl.ds(rf, L), :], v_ref[0, pl.ds(rb, L), :]], axis=0)
        kblk = jnp.concatenate([jnp.concatenate([kef_ref[pl.ds(rf, L), :], zero_k], axis=1),
                                jnp.concatenate([zero_k, keb_ref[pl.ds(rb, L), :]], axis=1)], axis=0)
        kv = lax.dot_general(vcat, kblk, TN_DIMS, preferred_element_type=F32)
        decay = jnp.exp2(jnp.concatenate([cf_ref[pl.ds(rf + L - 1, 1), :], cb_ref[pl.ds(rb, 1), :]], axis=1))
        s = s_ref[...]
        st_ref[i, :, 0:dk] = s[:, :dk].astype(BF16)
        st_ref[j, :, dk:2 * dk] = s[:, dk:].astype(BF16)
        s_ref[...] = s * decay + kv
        return carry

    lax.fori_loop(0, nc, state_body, 0, unroll=True)

    def out_body(i, carry):
        r0s = group_starts(i)
        ops = []
        for r0 in r0s:
            q = q_ref[0, pl.ds(r0, grp), :].astype(F32)
            k = k_ref[0, pl.ds(r0, grp), :].astype(F32)
            cf = cf_ref[pl.ds(r0, grp), :]
            cb = cb_ref[pl.ds(r0, grp), :]
            ops.append(((q * jnp.exp2(cf)).astype(BF16), (k * jnp.exp2(-cf)).astype(BF16),
                        (q * jnp.exp2(cb)).astype(BF16), (k * jnp.exp2(-cb)).astype(BF16)))
        scores = [(nt(qf, kf), nt(qb, kb)) for qf, kf, qb, kb in ops]
        attns = [jnp.where(fwd_mask, af, jnp.where(bwd_mask, ab, 0.0)).astype(BF16) for af, ab in scores]
        outs = []
        for r0, attn, (qf, _, qb, _) in zip(r0s, attns, ops):
            c0 = r0 // L
            qcat = jnp.concatenate([qf, qb], axis=1)
            inter = jnp.concatenate([nt(qcat[c * L:(c + 1) * L], st_ref[c0 + c]) for c in range(cpg)], axis=0)
            outs.append(mm(attn, v_ref[0, pl.ds(r0, grp), :]) + inter)
        for r0, o in zip(r0s, outs):
            o_ref[0, pl.ds(r0, grp), :] = (_rms(o, gain_ref[...]) * r_ref[0, pl.ds(r0, grp), :].astype(F32)).astype(BF16)
        return carry

    lax.fori_loop(0, seq // (grp * GLA_WAYS), out_body, 0, unroll=True)


def _gla(z, gq, gk, gv, gr, upw, bias, gain):
    B, S, _ = gq.shape
    nc = S // GLA_CHUNK
    seq_blk = lambda n: pl.BlockSpec((1, S, n), lambda b, h: (b, 0, h))
    head_blk = lambda r, n: pl.BlockSpec((None, r, n), lambda b, h: (h, 0, 0))
    return pl.pallas_call(
        functools.partial(_gla_kernel, seq=S),
        grid=(B, GLA_HEADS),
        in_specs=[pl.BlockSpec((1, S, LANES), lambda b, h: (b, 0, 0)),
                  seq_blk(GLA_DK), seq_blk(GLA_DK), seq_blk(GLA_DV), seq_blk(GLA_DV),
                  head_blk(LANES, 2 * GLA_DK), head_blk(1, 2 * GLA_DK),
                  pl.BlockSpec((1, GLA_DV), lambda b, h: (0, h))],
        out_specs=seq_blk(GLA_DV),
        out_shape=jax.ShapeDtypeStruct((B, S, GLA_VAL_DIM), BF16),
        scratch_shapes=[pltpu.VMEM((S, GLA_DK), F32), pltpu.VMEM((S, GLA_DK), F32),
                        pltpu.VMEM((S, GLA_DK), BF16), pltpu.VMEM((S, GLA_DK), BF16),
                        pltpu.VMEM((nc, GLA_DV, 2 * GLA_DK), BF16),
                        pltpu.VMEM((GLA_DV, 2 * GLA_DK), F32)],
        compiler_params=_cparams("gla", ("parallel", "parallel")),
        name="gla",
    )(z, gq, gk, gv, gr, upw, bias, gain)


def _mix_out_kernel(x_ref, a_ref, g_ref, sga_ref, sgg_ref, wa_ref, wb_ref, wo_ref, gain_ref, wr_ref,
                    h_ref, xn_ref, aff_ref, afft_ref):
    sub, starts = _sub_blocks(x_ref.shape[0])
    mm = functools.partial(jnp.dot, preferred_element_type=F32)
    blk = lambda ref, r0: ref[r0:r0 + sub, :]
    pad = jnp.zeros((LANES - N_EXPERTS, sub), F32)

    y_att = [mm(blk(a_ref, r0), wa_ref[...]) for r0 in starts]
    y_gla = [mm(blk(g_ref, r0), wb_ref[...]) for r0 in starts]
    merged = [(blk(sga_ref, r0).astype(F32) * ya + blk(sgg_ref, r0).astype(F32) * yg).astype(BF16)
              for r0, ya, yg in zip(starts, y_att, y_gla)]
    hs = [blk(x_ref, r0) + mm(m, wo_ref[...]) for r0, m in zip(starts, merged)]
    xns = [_rms(h, gain_ref[...]) for h in hs]
    for r0, h, xn in zip(starts, hs, xns):
        h_ref[r0:r0 + sub, :] = h
        _store_token_tiles(xn_ref, xn, r0)
    his = [xn.astype(BF16) for xn in xns]
    parts = [lax.dot_general(wr_ref[...], jnp.concatenate([hi, (xn - hi.astype(F32)).astype(BF16)], axis=1),
                             NT_DIMS, preferred_element_type=F32) for xn, hi in zip(xns, his)]
    for r0, part in zip(starts, parts):
        logits = part[:N_EXPERTS] + part[N_EXPERTS:]
        e = jnp.exp(logits - jnp.max(logits, axis=0, keepdims=True))
        aff_t = e / jnp.sum(e, axis=0, keepdims=True)
        afft_ref[:, r0:r0 + sub] = aff_t
        aff_ref[r0:r0 + sub, :] = jnp.concatenate([aff_t, pad], axis=0).T[:, :N_EXPERTS]


def _mix_out(x2, a, g, sga, sgg, wa, wb, wo, gain, wr, tm):
    T = x2.shape[0]
    row = lambda n: pl.BlockSpec((tm, n), lambda i: (i, 0))
    return pl.pallas_call(
        _mix_out_kernel,
        grid=(T // tm,),
        in_specs=[row(D_MODEL), row(ATT_Q_DIM), row(GLA_VAL_DIM), row(D_MODEL), row(D_MODEL),
                  _full(wa.shape), _full(wb.shape), _full(wo.shape), _full(gain.shape), _full(wr.shape)],
        out_specs=[row(D_MODEL), pl.BlockSpec((tm * ROW_TILES, LANES), lambda i: (i, 0)), row(N_EXPERTS),
                   pl.BlockSpec((N_EXPERTS, tm), lambda i: (0, i))],
        out_shape=[jax.ShapeDtypeStruct((T, D_MODEL), F32), jax.ShapeDtypeStruct((T * ROW_TILES, LANES), F32),
                   jax.ShapeDtypeStruct((T, N_EXPERTS), F32), jax.ShapeDtypeStruct((N_EXPERTS, T), F32)],
        compiler_params=_cparams("mix_out", ("parallel",)),
        name="mix_out",
    )(x2, a, g, sga, sgg, wa, wb, wo, gain, wr)


ROUTE_WAYS = 4


def _route_kernel(aff_ref, idx_ref, cum_ref, *, cap, seq):
    E = N_EXPERTS
    aff = aff_ref[...]
    count = lambda mask: jnp.sum(mask.astype(jnp.int32), axis=1, keepdims=True)
    as_float = lambda pattern: lax.bitcast_convert_type(pattern, F32)

    def thr_body(t, pattern):
        cand = pattern | jnp.left_shift(jnp.int32(1), 30 - t)
        return jnp.where(count(aff >= as_float(cand)) >= cap, cand, pattern)

    thr = as_float(lax.fori_loop(0, 31, thr_body, jnp.zeros((E, 1), jnp.int32)))
    above = aff > thr
    tie = aff == thr
    need = cap - count(above)

    pos = lax.broadcasted_iota(jnp.int32, (E, seq), 1)

    def tie_body(t, last):
        cand = last | jnp.left_shift(jnp.int32(1), (seq.bit_length() - 2) - t)
        return jnp.where(count(tie & (pos < cand)) < need, cand, last)

    last = lax.fori_loop(0, seq.bit_length() - 1, tie_body, jnp.zeros((E, 1), jnp.int32))
    sel = (above | (tie & (pos <= last))).astype(BF16)

    nt = seq // LANES
    lrow = lax.broadcasted_iota(jnp.int32, (LANES, LANES), 0)
    lcol = lax.broadcasted_iota(jnp.int32, (LANES, LANES), 1)
    tri = (lrow <= lcol).astype(BF16)
    mm = functools.partial(jnp.dot, preferred_element_type=F32)
    for t in range(nt):
        cum_ref[t * E:(t + 1) * E, :] = mm(sel[:, t * LANES:(t + 1) * LANES], tri)
    tile_of = (lax.broadcasted_iota(jnp.int32, (seq, LANES), 0) // LANES
               == lax.broadcasted_iota(jnp.int32, (seq, LANES), 1)).astype(BF16)
    per_tile = mm(sel, tile_of)
    lane = lax.broadcasted_iota(jnp.int32, (1, LANES), 1)
    far = jnp.float32(2 * seq)
    t_end = jnp.where(lane < nt, mm(per_tile.astype(BF16), tri), far)
    t_start = jnp.where(lane < nt, t_end - per_tile, far)
    pad = jnp.zeros((LANES - E, LANES), F32)
    t_start_cols = jnp.concatenate([jnp.where(lane < nt, t_start, 0.0), pad], axis=0).T

    slot = lax.broadcasted_iota(jnp.int32, (cap, LANES), 0).astype(F32)
    ones = jnp.ones((LANES, LANES), BF16)
    zrows = jnp.zeros((LANES - nt, 2 * LANES), F32)
    for e0 in range(0, E, ROUTE_WAYS):
        es = range(e0, e0 + ROUTE_WAYS)
        tiles, picks, whole = [], [], []
        for e in es:
            absc = cum_ref[pl.ds(e, nt, stride=E), :] + t_start_cols[0:nt, e:e + 1]
            hi = jnp.where(absc >= 256.0, 1.0, 0.0) + jnp.where(absc >= 512.0, 1.0, 0.0)
            lo = absc - 256.0 * hi
            tiles.append(jnp.concatenate([jnp.concatenate([lo, hi], axis=1), zrows], axis=0).astype(BF16))
            done = jnp.where(t_end[e:e + 1] <= slot, 1.0, 0.0)
            whole.append(done)
            picks.append((jnp.where(t_start[e:e + 1] <= slot, 1.0, 0.0) - done).astype(BF16))
        rows = [mm(p, w) for p, w in zip(picks, tiles)]
        votes = [(jnp.where(r[:, :LANES] + 256.0 * r[:, LANES:] <= slot, 1.0, 0.0) + float(LANES) * d).astype(BF16)
                 for r, d in zip(rows, whole)]
        for e, v in zip(es, votes):
            idx_ref[0, :, e:e + 1] = mm(v, ones)[:, e:e + 1].astype(jnp.int32)


def _route(aff_t, batch, cap):
    E, T = aff_t.shape
    B, S = batch, T // batch
    return pl.pallas_call(
        functools.partial(_route_kernel, cap=cap, seq=S),
        grid=(B,),
        in_specs=[pl.BlockSpec((E, S), lambda b: (0, b))],
        out_specs=pl.BlockSpec((1, cap, E), lambda b: (b, 0, 0)),
        out_shape=jax.ShapeDtypeStruct((B, cap, E), jnp.int32),
        scratch_shapes=[pltpu.VMEM((S // LANES * E, LANES), F32)],
        compiler_params=_cparams("route", ("parallel",)),
        name="route",
    )(aff_t)


MOE_EXPERTS_PER_STEP = 4


def _gather_kernel(idx_ref, xn_ref, aff_ref, xg_ref, wg_ref, ws_ref, *, cap):
    n = MOE_EXPERTS_PER_STEP
    lane = lax.broadcasted_iota(jnp.int32, (cap, N_EXPERTS), 1)
    for j in range(n):
        for i in range(cap):
            t = idx_ref[j, 0, i]
            src = pl.multiple_of(t * SUBLANES, SUBLANES)
            xg_ref[0, j, i * SUBLANES:(i + 1) * SUBLANES, :] = xn_ref[0, pl.ds(src, SUBLANES), :]
            ws_ref[j, i:i + 1, :] = aff_ref[0, pl.ds(t, 1), :]
        e = pl.program_id(1) * n + j
        wg_ref[0, j] = jnp.sum(jnp.where(lane == e, ws_ref[j], 0.0), axis=1, keepdims=True)


def _gather(idx, xn_tiles, aff, cap):
    B, S, E = aff.shape
    n = MOE_EXPERTS_PER_STEP
    return pl.pallas_call(
        functools.partial(_gather_kernel, cap=cap),
        grid=(B, E // n),
        in_specs=[pl.BlockSpec((n, 1, cap), lambda b, e: (b * (E // n) + e, 0, 0), memory_space=pltpu.SMEM),
                  pl.BlockSpec((1, S * SUBLANES, LANES), lambda b, e: (b, 0, 0)),
                  pl.BlockSpec((1, S, E), lambda b, e: (b, 0, 0))],
        out_specs=[pl.BlockSpec((1, n, cap * SUBLANES, LANES), lambda b, e: (b, e, 0, 0)),
                   pl.BlockSpec((1, n, cap, 1), lambda b, e: (b, e, 0, 0))],
        out_shape=[jax.ShapeDtypeStruct((B, E, cap * SUBLANES, LANES), F32),
                   jax.ShapeDtypeStruct((B, E, cap, 1), F32)],
        scratch_shapes=[pltpu.VMEM((n, cap, E), F32)],
        compiler_params=_cparams("gather", ("arbitrary", "arbitrary")),
        name="gather",
    )(idx.reshape(B * E, 1, cap), xn_tiles, aff)


def _ffn_kernel(xg_ref, wg_ref, w1_ref, w2_ref, w3_ref, y_ref, b1_ref, b2_ref, b3_ref):
    @pl.when(pl.program_id(1) == 0)
    def _():
        b1_ref[...] = w1_ref[0].astype(BF16)
        b2_ref[...] = w2_ref[0].astype(BF16)
        b3_ref[...] = w3_ref[0].astype(BF16)

    rows = wg_ref.shape[2]
    seqs = range(xg_ref.shape[0])
    mm = functools.partial(jnp.dot, preferred_element_type=F32)
    xgs = [_load_token_tiles(xg_ref.at[i, 0], rows).astype(BF16) for i in seqs]
    gates = [mm(xg, b1_ref[...]) for xg in xgs]
    ups = [mm(xg, b2_ref[...]) for xg in xgs]
    hids = [(gate * _sigmoid(gate) * up).astype(BF16) for gate, up in zip(gates, ups)]
    ys = [mm(hid, b3_ref[...]) * wg_ref[i, 0] for i, hid in zip(seqs, hids)]
    for i, y in zip(seqs, ys):
        _store_token_tiles(y_ref.at[i, 0], y)


FFN_SEQS_PER_STEP = 2


def _ffn(xg, wg, w1, w2, w3):
    B, E, C, _ = wg.shape
    _, D, F = w1.shape
    n = FFN_SEQS_PER_STEP
    return pl.pallas_call(
        _ffn_kernel,
        grid=(E, B // n),
        in_specs=[pl.BlockSpec((n, 1, C * ROW_TILES, LANES), lambda e, b: (b, e, 0, 0)),
                  pl.BlockSpec((n, 1, C, 1), lambda e, b: (b, e, 0, 0)),
                  pl.BlockSpec((1, D, F), lambda e, b: (e, 0, 0)),
                  pl.BlockSpec((1, D, F), lambda e, b: (e, 0, 0)),
                  pl.BlockSpec((1, F, D), lambda e, b: (e, 0, 0))],
        out_specs=pl.BlockSpec((n, 1, C * ROW_TILES, LANES), lambda e, b: (b, e, 0, 0)),
        out_shape=jax.ShapeDtypeStruct((B, E, C * ROW_TILES, LANES), F32),
        scratch_shapes=[pltpu.VMEM((D, F), BF16), pltpu.VMEM((D, F), BF16), pltpu.VMEM((F, D), BF16)],
        compiler_params=_cparams("ffn", ("arbitrary", "arbitrary")),
        name="ffn",
    )(xg, wg, w1, w2, w3)


SCATTER_BATCH = 16


def _scatter_kernel(idx_ref, y_ref, o_ref, *, cap):
    @pl.when(pl.program_id(1) == 0)
    def _():
        o_ref[...] = jnp.zeros_like(o_ref)

    for j in range(MOE_EXPERTS_PER_STEP):
        for i0 in range(0, cap, SCATTER_BATCH):
            slots = range(i0, i0 + SCATTER_BATCH)
            rows = [pl.ds(pl.multiple_of(idx_ref[j, 0, i] * SUBLANES, SUBLANES), SUBLANES) for i in slots]
            new = [o_ref[0, r, :] + y_ref[0, j, i * SUBLANES:(i + 1) * SUBLANES, :] for r, i in zip(rows, slots)]
            for r, v in zip(rows, new):
                o_ref[0, r, :] = v


def _scatter(idx, y_tiles, seq):
    B, E, cap = idx.shape
    n = MOE_EXPERTS_PER_STEP
    return pl.pallas_call(
        functools.partial(_scatter_kernel, cap=cap),
        grid=(B, E // n),
        in_specs=[pl.BlockSpec((n, 1, cap), lambda b, e: (b * (E // n) + e, 0, 0), memory_space=pltpu.SMEM),
                  pl.BlockSpec((1, n, cap * SUBLANES, LANES), lambda b, e: (b, e, 0, 0))],
        out_specs=pl.BlockSpec((1, seq * SUBLANES, LANES), lambda b, e: (b, 0, 0)),
        out_shape=jax.ShapeDtypeStruct((B, seq * SUBLANES, LANES), F32),
        compiler_params=_cparams("scatter", ("arbitrary", "arbitrary")),
        name="scatter",
    )(idx.reshape(B * E, 1, cap), y_tiles)


def _ple_out_kernel(h_ref, moe_ref, p_ref, gple_ref, wpg_ref, wple_ref, gfin_ref, o_ref):
    sub, starts = _sub_blocks(h_ref.shape[0])
    mm = functools.partial(jnp.dot, preferred_element_type=F32)
    hs = [h_ref[r0:r0 + sub, :] + _load_token_tiles(moe_ref, sub, r0) for r0 in starts]
    ns = [_rms(h, gple_ref[...]).astype(BF16) for h in hs]
    gates = [_sigmoid(mm(n, wpg_ref[...])) for n in ns]
    embs = [mm(p_ref[r0:r0 + sub, :].astype(BF16), wple_ref[...]) for r0 in starts]
    for r0, h, gate, emb in zip(starts, hs, gates, embs):
        o_ref[r0:r0 + sub, :] = _rms(h + gate * emb, gfin_ref[...])


def _ple_out(h, moe, p2, gple, wpg, wple, gfin, tm):
    T = h.shape[0]
    row = lambda n: pl.BlockSpec((tm, n), lambda i: (i, 0))
    return pl.pallas_call(
        _ple_out_kernel,
        grid=(T // tm,),
        in_specs=[row(D_MODEL), pl.BlockSpec((tm * ROW_TILES, LANES), lambda i: (i, 0)), row(PLE_DIM),
                  _full(gple.shape), _full(wpg.shape), _full(wple.shape), _full(gfin.shape)],
        out_specs=row(D_MODEL),
        out_shape=jax.ShapeDtypeStruct((T, D_MODEL), F32),
        compiler_params=_cparams("ple_out", ("parallel",)),
        name="ple_out",
    )(h, moe, p2, gple, wpg, wple, gfin)


IN_EDGES = {}
_o = 0
for _name, _n in (("q", ATT_Q_DIM), ("k", ATT_KV_DIM), ("v", ATT_KV_DIM), ("gqk", 2 * GLA_KEY_DIM),
                  ("gv", GLA_VAL_DIM), ("gr", GLA_VAL_DIM), ("z", 2 * GLA_GATE_RANK), ("gate", 2 * D_MODEL)):
    IN_EDGES[_name] = (_o, _o + _n)
    _o += _n
IN_DIM = _o


def _w_prep_kernel(w_ref, wqt_ref, wkz_ref, wvt_ref, wgqk_ref, wgv_ref, wgr_ref, wgate_ref):
    piece = lambda name: w_ref[:, IN_EDGES[name][0]:IN_EDGES[name][1]]
    wqt_ref[...] = piece("q").T.astype(BF16)
    wvt_ref[...] = piece("v").T.astype(BF16)
    z = piece("z")
    wkz_ref[...] = jnp.concatenate([piece("k")] + [z] * (LANES // (2 * GLA_GATE_RANK)), axis=1).astype(BF16)
    wgqk_ref[...] = piece("gqk").astype(BF16)
    wgv_ref[...] = piece("gv").astype(BF16)
    wgr_ref[...] = piece("gr").astype(BF16)
    wgate_ref[...] = piece("gate").astype(BF16)


def _w_prep(w, rows):
    D = w.shape[0]
    row = lambda n: pl.BlockSpec((rows, n), lambda i: (i, 0))
    col = lambda n: pl.BlockSpec((n, rows), lambda i: (0, i))
    widths = (ATT_KV_DIM + LANES, 2 * GLA_KEY_DIM, GLA_VAL_DIM, GLA_VAL_DIM, 2 * D_MODEL)
    out_specs = [col(ATT_Q_DIM), row(widths[0]), col(ATT_KV_DIM)] + [row(n) for n in widths[1:]]
    out_shape = ([jax.ShapeDtypeStruct((ATT_Q_DIM, D), BF16), jax.ShapeDtypeStruct((D, widths[0]), BF16),
                  jax.ShapeDtypeStruct((ATT_KV_DIM, D), BF16)]
                 + [jax.ShapeDtypeStruct((D, n), BF16) for n in widths[1:]])
    return pl.pallas_call(
        _w_prep_kernel,
        grid=(D // rows,),
        in_specs=[pl.BlockSpec((rows, IN_DIM), lambda i: (i, 0))],
        out_specs=out_specs,
        out_shape=out_shape,
        compiler_params=_cparams("w_prep", ("parallel",)),
        name="w_prep",
    )(w)


def kernel(x, p, positions, norm_mix, w_in, gla_gate_up_fwd, gla_gate_bias_fwd, gla_gate_up_bwd, gla_gate_bias_bwd, attn_sink, gla_norm, w_branch_attn, w_branch_gla, w_out, norm_ffn, w_router, w_exp_gate, w_exp_up, w_exp_down, norm_ple, w_ple_gate, w_ple, norm_final):
    B, S, D = x.shape
    T = B * S
    depth = w_in.shape[0]
    assert depth == 1, "the final norm is fused into the (single) layer's PLE kernel"
    cap = CAPACITY_FACTOR * S // N_EXPERTS
    R = GLA_GATE_RANK

    posr = positions.reshape(1, T)
    inv_freq = ROPE_THETA ** (-jnp.arange(0, ROPE_DIM, 2, dtype=F32) / ROPE_DIM)
    invfc = inv_freq.reshape(ROPE_DIM // 2, 1)

    h = x.reshape(T, D)
    for l in range(depth):
        w_pieces = _w_prep(w_in[l], rows=TILE_ROWS["w_prep"])
        per_head = lambda w: w.reshape(-1, GLA_HEADS, GLA_DK).swapaxes(0, 1)
        upf, upb = per_head(gla_gate_up_fwd[l]), per_head(gla_gate_up_bwd[l])
        up = jnp.concatenate([jnp.concatenate([upf, jnp.zeros_like(upf)], axis=2),
                              jnp.concatenate([jnp.zeros_like(upb), upb], axis=2)], axis=1)
        up_hi = up.astype(BF16)
        up_lo = (up - up_hi.astype(F32)).astype(BF16)
        upw = jnp.concatenate([up_hi, up_hi, up_lo, jnp.zeros_like(up_lo)], axis=1)
        gbias = jnp.concatenate([per_head(gla_gate_bias_fwd[l]), per_head(gla_gate_bias_bwd[l])], axis=2)
        wr = w_router[l]
        wr_hi = wr.astype(BF16)
        wr_lo = (wr - wr_hi.astype(F32)).astype(BF16)
        wr2 = jnp.concatenate([jnp.concatenate([wr_hi, wr_lo], axis=1),
                               jnp.concatenate([wr_hi, jnp.zeros_like(wr_lo)], axis=1)], axis=0).T

        qt, k0, k1, vt, gq, gk, gv, gr, z, sga, sgg = _in_proj(
            h, posr, invfc, norm_mix[l].reshape(1, D), *w_pieces, tm=TILE_ROWS["in_proj"])

        att = _swa(attn_sink[l], qt, k0.reshape(B, S, -1), k1.reshape(B, S, -1), vt, batch=B, tq=TILE_ROWS["swa"])
        gla = _gla(z.reshape(B, S, -1), gq.reshape(B, S, -1), gk.reshape(B, S, -1), gv.reshape(B, S, -1),
                   gr.reshape(B, S, -1), upw, gbias, gla_norm[l].reshape(1, -1))

        h1, xn, aff, aff_t = _mix_out(h, att.reshape(T, -1), gla.reshape(T, -1), sga, sgg,
                                      w_branch_attn[l].astype(BF16), w_branch_gla[l].astype(BF16),
                                      w_out[l].astype(BF16), norm_ffn[l].reshape(1, D), wr2, tm=TILE_ROWS["mix_out"])

        aff3 = aff.reshape(B, S, N_EXPERTS)
        idx = _route(aff_t, B, cap)
        idx = jnp.swapaxes(idx, 1, 2)
        xg, wg = _gather(idx, xn.reshape(B, S * SUBLANES, LANES), aff3, cap)
        y = _ffn(xg, wg, w_exp_gate[l], w_exp_up[l], w_exp_down[l])
        moe = _scatter(idx, y, S)

        h = _ple_out(h1, moe.reshape(T * ROW_TILES, LANES), p[l].reshape(T, PLE_DIM), norm_ple[l].reshape(1, D),
                     w_ple_gate[l].astype(BF16), w_ple[l].astype(BF16), norm_final.reshape(1, D),
                     tm=TILE_ROWS["ple_out"])
    return h.reshape(B, S, D)
```

```python
import functools
import math

import jax
import jax.numpy as jnp
from jax import lax
from jax.experimental import pallas as pl
from jax.experimental.pallas import tpu as pltpu

D_MODEL = 1024
ATT_HEADS = 8
ATT_KV_HEADS = 2
ATT_HEAD_DIM = 64
ATT_GROUP = ATT_HEADS // ATT_KV_HEADS
ATT_Q_DIM = ATT_HEADS * ATT_HEAD_DIM
ATT_KV_DIM = ATT_KV_HEADS * ATT_HEAD_DIM
WINDOW = 128
ROPE_DIM = ATT_HEAD_DIM // 4
ROPE_THETA = 500000.0
GLA_HEADS = 4
GLA_KEY_DIM = D_MODEL // 2
GLA_VAL_DIM = D_MODEL
GLA_DK = GLA_KEY_DIM // GLA_HEADS
GLA_DV = GLA_VAL_DIM // GLA_HEADS
GLA_GATE_RANK = 16
GLA_GATE_NORM = 16.0
GLA_CHUNK = 64
N_EXPERTS = 16
EXPERT_FF = D_MODEL
CAPACITY_FACTOR = 2
PLE_DIM = 256
EPS = 1e-6

LANES = 128
MIB = 1024 * 1024
BF16 = jnp.bfloat16
F32 = jnp.float32
LOG2E = math.log2(math.e)

NT_DIMS = (((1,), (1,)), ((), ()))
TN_DIMS = (((0,), (0,)), ((), ()))

TILE_ROWS = {"w_prep": 256, "in_proj": 512, "swa": 1024, "mix_out": 512, "ple_out": 1024}
VMEM_LIMIT_MIB = {"w_prep": 40, "in_proj": 56, "swa": 32, "gla": 48, "mix_out": 48, "route": 32, "gather": 60,
                  "ffn": 60, "scatter": 60, "ple_out": 48}


def _cparams(name, sem):
    return pltpu.CompilerParams(dimension_semantics=sem, vmem_limit_bytes=VMEM_LIMIT_MIB[name] * MIB)


def _full(shape):
    n = len(shape)
    return pl.BlockSpec(shape, lambda *_: (0,) * n)


def _rms(x, gain):
    ms = jnp.mean(x * x, axis=-1, keepdims=True)
    return x * lax.rsqrt(ms + EPS) * gain


def _sigmoid(x):
    return 0.5 * jnp.tanh(0.5 * x) + 0.5


SUBLANES = 8
ROW_TILES = D_MODEL // LANES


def _store_token_tiles(ref2d, x, first_row=0):
    rows = x.shape[0]
    for j in range(ROW_TILES):
        ref2d[pl.ds(first_row * ROW_TILES + j, rows, stride=ROW_TILES), :] = x[:, j * LANES:(j + 1) * LANES]


def _load_token_tiles(ref2d, rows, first_row=0):
    return jnp.concatenate([ref2d[pl.ds(first_row * ROW_TILES + j, rows, stride=ROW_TILES), :]
                            for j in range(ROW_TILES)], axis=1)


SUB_ROWS = 256


def _sub_blocks(tile_rows):
    return SUB_ROWS, list(range(0, tile_rows, SUB_ROWS))


def _rope_rows(t, cos_r, sin_r, heads):
    half = ROPE_DIM // 2
    rows = []
    for h in range(heads):
        r0 = h * ATT_HEAD_DIM
        t1, t2 = t[r0:r0 + half], t[r0 + half:r0 + ROPE_DIM]
        rows += [t1 * cos_r - t2 * sin_r, t2 * cos_r + t1 * sin_r, t[r0 + ROPE_DIM:r0 + ATT_HEAD_DIM]]
    return jnp.concatenate(rows, axis=0)


def _in_proj_kernel(x_ref, posr_ref, invfc_ref, gain_ref, wqt_ref, wkz_ref, wvt_ref,
                    wgqk_ref, wgv_ref, wgr_ref, wgate_ref,
                    qt_ref, k0_ref, k1_ref, vt_ref, gq_ref, gk_ref, gv_ref, gr_ref, z_ref, sga_ref, sgg_ref):
    a = _rms(x_ref[...], gain_ref[...]).astype(BF16)
    ang_t = invfc_ref[...] * posr_ref[...].astype(F32)
    cos_r, sin_r = jnp.cos(ang_t), jnp.sin(ang_t)

    qt = lax.dot_general(wqt_ref[...], a, NT_DIMS, preferred_element_type=F32)
    qt_ref[...] = (_rope_rows(qt, cos_r, sin_r, ATT_HEADS) * (ATT_HEAD_DIM ** -0.5 * LOG2E)).astype(BF16)

    kz = jnp.dot(a, wkz_ref[...], preferred_element_type=F32)
    z_ref[...] = kz[:, ATT_KV_DIM:]
    k = _rope_rows(kz[:, :ATT_KV_DIM].T, cos_r, sin_r, ATT_KV_HEADS).T.astype(BF16)
    k0_ref[...] = k[:, :ATT_HEAD_DIM]
    k1_ref[...] = k[:, ATT_HEAD_DIM:]
    vt_ref[...] = lax.dot_general(wvt_ref[...], a, NT_DIMS, preferred_element_type=F32).astype(BF16)

    gqk = jnp.dot(a, wgqk_ref[...], preferred_element_type=F32)
    gq_ref[...] = (gqk[:, :GLA_KEY_DIM] * (GLA_DK ** -0.5)).astype(BF16)
    gk_ref[...] = gqk[:, GLA_KEY_DIM:].astype(BF16)
    gv_ref[...] = jnp.dot(a, wgv_ref[...], preferred_element_type=F32).astype(BF16)
    gr = jnp.dot(a, wgr_ref[...], preferred_element_type=F32)
    gr_ref[...] = (gr * _sigmoid(gr)).astype(BF16)
    gates = jnp.dot(a, wgate_ref[...], preferred_element_type=F32)
    sga_ref[...] = _sigmoid(gates[:, :D_MODEL]).astype(BF16)
    sgg_ref[...] = _sigmoid(gates[:, D_MODEL:]).astype(BF16)


def _in_proj(x2, posr, invfc, gain, wqt, wkz, wvt, wgqk, wgv, wgr, wgate, tm):
    T = x2.shape[0]
    row = lambda n: pl.BlockSpec((tm, n), lambda i: (i, 0))
    col = lambda n: pl.BlockSpec((n, tm), lambda i: (0, i))
    row_widths = (ATT_HEAD_DIM, ATT_HEAD_DIM, None, GLA_KEY_DIM, GLA_KEY_DIM, GLA_VAL_DIM,
                  GLA_VAL_DIM, wkz.shape[1] - ATT_KV_DIM, D_MODEL, D_MODEL)
    row_dtypes = (BF16,) * 7 + (F32, BF16, BF16)
    out_specs = [col(ATT_Q_DIM)]
    out_shape = [jax.ShapeDtypeStruct((ATT_Q_DIM, T), BF16)]
    for n, dt in zip(row_widths, row_dtypes):
        if n is None:
            out_specs.append(col(ATT_KV_DIM))
            out_shape.append(jax.ShapeDtypeStruct((ATT_KV_DIM, T), BF16))
        else:
            out_specs.append(row(n))
            out_shape.append(jax.ShapeDtypeStruct((T, n), dt))
    consts = (invfc, gain, wqt, wkz, wvt, wgqk, wgv, wgr, wgate)
    return pl.pallas_call(
        _in_proj_kernel,
        grid=(T // tm,),
        in_specs=[row(D_MODEL), col(1)] + [_full(c.shape) for c in consts],
        out_specs=out_specs,
        out_shape=out_shape,
        compiler_params=_cparams("in_proj", ("parallel",)),
        name="in_proj",
    )(x2, posr, *consts)


def _swa_kernel(sink_ref, qt_ref, k0_ref, k1_ref, vt_ref, o_ref, *, tq, seq):
    blk = WINDOW
    span = 3 * blk
    hd = ATT_HEAD_DIM
    n = pl.program_id(1)
    ones = jnp.ones((16, span), BF16)
    kv_refs = (k0_ref, k1_ref)

    def window_start(sb):
        return pl.multiple_of(jnp.clip(n * tq + (sb - 1) * blk, 0, seq - span), blk)

    def scores(sb, g):
        kw = kv_refs[g][0, pl.ds(window_start(sb), span), :]
        heads = range(g * ATT_GROUP, (g + 1) * ATT_GROUP)
        qs = jnp.concatenate([qt_ref[h * hd:(h + 1) * hd, sb * blk:(sb + 1) * blk] for h in heads], axis=1)
        return jnp.dot(kw, qs, preferred_element_type=F32)

    work = [(sb, g) for sb in range(tq // blk) for g in range(ATT_KV_HEADS)]
    s_next = scores(*work[0])
    outs = []
    for step, (sb, g) in enumerate(work):
        s_all = s_next
        if step + 1 < len(work):
            s_next = scores(*work[step + 1])
        q0 = n * tq + sb * blk
        start = window_start(sb)
        kj = start + lax.broadcasted_iota(jnp.int32, (span, blk), 0)
        qi = q0 + lax.broadcasted_iota(jnp.int32, (span, blk), 1)
        valid = jnp.abs(qi - kj) <= WINDOW
        vaug = jnp.concatenate([vt_ref[g * hd:(g + 1) * hd, pl.ds(start, span)], ones], axis=0)
        sinks = [sink_ref[g * ATT_GROUP + i] * LOG2E for i in range(ATT_GROUP)]
        ss = [jnp.where(valid, s_all[:, i * blk:(i + 1) * blk], -jnp.inf) for i in range(ATT_GROUP)]
        ms = [jnp.maximum(jnp.max(s, axis=0, keepdims=True), sink) for s, sink in zip(ss, sinks)]
        rs = [jnp.dot(vaug, jnp.exp2(s - m).astype(BF16), preferred_element_type=F32) for s, m in zip(ss, ms)]
        outs += [r[:hd] / (r[hd:hd + 1] + jnp.exp2(sink - m)) for r, m, sink in zip(rs, ms, sinks)]
        if g == ATT_KV_HEADS - 1:
            for pr in range(ATT_HEADS // 2):
                pair = jnp.concatenate([outs[2 * pr], outs[2 * pr + 1]], axis=0)
                o_ref[0, sb * blk:(sb + 1) * blk, pr * 2 * hd:(pr + 1) * 2 * hd] = pair.T.astype(BF16)
            outs = []


def _swa(sink, qt, k0, k1, vt, batch, tq):
    S = k0.shape[1]
    nq = S // tq
    kspec = pl.BlockSpec((1, S, ATT_HEAD_DIM), lambda b, n: (b, 0, 0))
    return pl.pallas_call(
        functools.partial(_swa_kernel, tq=tq, seq=S),
        grid=(batch, nq),
        in_specs=[pl.BlockSpec(memory_space=pltpu.SMEM),
                  pl.BlockSpec((ATT_Q_DIM, tq), lambda b, n: (0, b * nq + n)),
                  kspec, kspec,
                  pl.BlockSpec((ATT_KV_DIM, S), lambda b, n: (0, b))],
        out_specs=pl.BlockSpec((1, tq, ATT_Q_DIM), lambda b, n: (b, n, 0)),
        out_shape=jax.ShapeDtypeStruct((batch, S, ATT_Q_DIM), BF16),
        compiler_params=_cparams("swa", ("parallel", "parallel")),
        name="swa",
    )(sink, qt, k0, k1, vt)


def _log2_sigmoid(u):
    return jnp.minimum(u, 0.0) * LOG2E - jnp.log2(1.0 + jnp.exp2(jnp.abs(u) * -LOG2E))


def _split2(x):
    hi = x.astype(BF16)
    return hi, (x - hi.astype(F32)).astype(BF16)


GLA_WAYS = 4


def _gla_kernel(z_ref, q_ref, k_ref, v_ref, r_ref, upw_ref, bias_ref, gain_ref,
                o_ref, cf_ref, cb_ref, kef_ref, keb_ref, st_ref, s_ref, *, seq):
    L = GLA_CHUNK
    R2 = 2 * GLA_GATE_RANK
    nc = seq // L
    grp = 4 * L
    cpg = grp // L
    dk = GLA_DK
    mm = functools.partial(jnp.dot, preferred_element_type=F32)
    nt = functools.partial(lax.dot_general, dimension_numbers=NT_DIMS, preferred_element_type=F32)

    row = lax.broadcasted_iota(jnp.int32, (grp, grp), 0)
    col = lax.broadcasted_iota(jnp.int32, (grp, grp), 1)
    same = (row // L) == (col // L)
    fwd_mask = same & (col <= row)
    bwd_mask = same & (col > row)
    tri_lo = jnp.where(fwd_mask, 1.0, 0.0).astype(BF16)
    tri_up = jnp.where(same & (col >= row), 1.0, 0.0).astype(BF16)
    lane = lax.broadcasted_iota(jnp.int32, (1, LANES), 1)
    use_lo = (lane >= R2) & (lane < 2 * R2)

    def group_starts(i):
        return [pl.multiple_of((i * GLA_WAYS + w) * grp, grp) for w in range(GLA_WAYS)]

    def cum_body(i, carry):
        r0s = group_starts(i)
        zs = [_split2(z_ref[0, pl.ds(r0, grp), :]) for r0 in r0s]
        us = [mm(jnp.where(use_lo, zl, zh), upw_ref[...]) + bias_ref[...] for zh, zl in zs]
        las = [_split2(_log2_sigmoid(u) * (1.0 / GLA_GATE_NORM)) for u in us]
        cfxs = [mm(tri_lo, jnp.concatenate([lh[:, :dk], ll[:, :dk]], axis=1)) for lh, ll in las]
        cbxs = [mm(tri_up, jnp.concatenate([lh[:, dk:], ll[:, dk:]], axis=1)) for lh, ll in las]
        for r0, cfx, cbx in zip(r0s, cfxs, cbxs):
            cf = cfx[:, :dk] + cfx[:, dk:]
            cb = cbx[:, :dk] + cbx[:, dk:]
            cf_ref[pl.ds(r0, grp), :] = cf
            cb_ref[pl.ds(r0, grp), :] = cb
            k = k_ref[0, pl.ds(r0, grp), :].astype(F32)
            for c in range(cpg):
                sl = slice(c * L, (c + 1) * L)
                gf = cf[(c + 1) * L - 1:(c + 1) * L]
                gb = cb[c * L:c * L + 1]
                kef_ref[pl.ds(r0 + c * L, L), :] = (k[sl] * jnp.exp2(gf - cf[sl])).astype(BF16)
                keb_ref[pl.ds(r0 + c * L, L), :] = (k[sl] * jnp.exp2(gb - cb[sl])).astype(BF16)
        return carry

    lax.fori_loop(0, seq // (grp * GLA_WAYS), cum_body, 0, unroll=True)

    s_ref[...] = jnp.zeros_like(s_ref)
    zero_k = jnp.zeros((L, dk), BF16)

    def state_body(i, carry):
        j = nc - 1 - i
        rf = pl.multiple_of(i * L, L)
        rb = pl.multiple_of(j * L, L)
        vcat = jnp.concatenate([v_ref[0, pl.ds(rf, L), :], v_ref[0, pl.ds(rb, L), :]], axis=0)
        kblk = jnp.concatenate([jnp.concatenate([kef_ref[pl.ds(rf, L), :], zero_k], axis=1),
                                jnp.concatenate([zero_k, keb_ref[pl.ds(rb, L), :]], axis=1)], axis=0)
        kv = lax.dot_general(vcat, kblk, TN_DIMS, preferred_element_type=F32)
        decay = jnp.exp2(jnp.concatenate([cf_ref[pl.ds(rf + L - 1, 1), :], cb_ref[pl.ds(rb, 1), :]], axis=1))
        s = s_ref[...]
        st_ref[i, :, 0:dk] = s[:, :dk].astype(BF16)
        st_ref[j, :, dk:2 * dk] = s[:, dk:].astype(BF16)
        s_ref[...] = s * decay + kv
        return carry

    lax.fori_loop(0, nc, state_body, 0, unroll=True)

    def out_body(i, carry):
        r0s = group_starts(i)
        ops = []
        for r0 in r0s:
            q = q_ref[0, pl.ds(r0, grp), :].astype(F32)
            k = k_ref[0, pl.ds(r0, grp), :].astype(F32)
            cf = cf_ref[pl.ds(r0, grp), :]
            cb = cb_ref[pl.ds(r0, grp), :]
            ops.append(((q * jnp.exp2(cf)).astype(BF16), (k * jnp.exp2(-cf)).astype(BF16),
                        (q * jnp.exp2(cb)).astype(BF16), (k * jnp.exp2(-cb)).astype(BF16)))
        scores = [(nt(qf, kf), nt(qb, kb)) for qf, kf, qb, kb in ops]
        attns = [jnp.where(fwd_mask, af, jnp.where(bwd_mask, ab, 0.0)).astype(BF16) for af, ab in scores]
        outs = []
        for r0, attn, (qf, _, qb, _) in zip(r0s, attns, ops):
            c0 = r0 // L
            qcat = jnp.concatenate([qf, qb], axis=1)
            inter = jnp.concatenate([nt(qcat[c * L:(c + 1) * L], st_ref[c0 + c]) for c in range(cpg)], axis=0)
            outs.append(mm(attn, v_ref[0, pl.ds(r0, grp), :]) + inter)
        for r0, o in zip(r0s, outs):
            o_ref[0, pl.ds(r0, grp), :] = (_rms(o, gain_ref[...]) * r_ref[0, pl.ds(r0, grp), :].astype(F32)).astype(BF16)
        return carry

    lax.fori_loop(0, seq // (grp * GLA_WAYS), out_body, 0, unroll=True)


def _gla(z, gq, gk, gv, gr, upw, bias, gain):
    B, S, _ = gq.shape
    nc = S // GLA_CHUNK
    seq_blk = lambda n: pl.BlockSpec((1, S, n), lambda b, h: (b, 0, h))
    head_blk = lambda r, n: pl.BlockSpec((None, r, n), lambda b, h: (h, 0, 0))
    return pl.pallas_call(
        functools.partial(_gla_kernel, seq=S),
        grid=(B, GLA_HEADS),
        in_specs=[pl.BlockSpec((1, S, LANES), lambda b, h: (b, 0, 0)),
                  seq_blk(GLA_DK), seq_blk(GLA_DK), seq_blk(GLA_DV), seq_blk(GLA_DV),
                  head_blk(LANES, 2 * GLA_DK), head_blk(1, 2 * GLA_DK),
                  pl.BlockSpec((1, GLA_DV), lambda b, h: (0, h))],
        out_specs=seq_blk(GLA_DV),
        out_shape=jax.ShapeDtypeStruct((B, S, GLA_VAL_DIM), BF16),
        scratch_shapes=[pltpu.VMEM((S, GLA_DK), F32), pltpu.VMEM((S, GLA_DK), F32),
                        pltpu.VMEM((S, GLA_DK), BF16), pltpu.VMEM((S, GLA_DK), BF16),
                        pltpu.VMEM((nc, GLA_DV, 2 * GLA_DK), BF16),
                        pltpu.VMEM((GLA_DV, 2 * GLA_DK), F32)],
        compiler_params=_cparams("gla", ("parallel", "parallel")),
        name="gla",
    )(z, gq, gk, gv, gr, upw, bias, gain)


def _mix_out_kernel(x_ref, a_ref, g_ref, sga_ref, sgg_ref, wa_ref, wb_ref, wo_ref, gain_ref, wr_ref,
                    h_ref, xn_ref, aff_ref, afft_ref):
    sub, starts = _sub_blocks(x_ref.shape[0])
    mm = functools.partial(jnp.dot, preferred_element_type=F32)
    blk = lambda ref, r0: ref[r0:r0 + sub, :]
    pad = jnp.zeros((LANES - N_EXPERTS, sub), F32)

    y_att = [mm(blk(a_ref, r0), wa_ref[...]) for r0 in starts]
    y_gla = [mm(blk(g_ref, r0), wb_ref[...]) for r0 in starts]
    merged = [(blk(sga_ref, r0).astype(F32) * ya + blk(sgg_ref, r0).astype(F32) * yg).astype(BF16)
              for r0, ya, yg in zip(starts, y_att, y_gla)]
    hs = [blk(x_ref, r0) + mm(m, wo_ref[...]) for r0, m in zip(starts, merged)]
    xns = [_rms(h, gain_ref[...]) for h in hs]
    for r0, h, xn in zip(starts, hs, xns):
        h_ref[r0:r0 + sub, :] = h
        _store_token_tiles(xn_ref, xn, r0)
    his = [xn.astype(BF16) for xn in xns]
    parts = [lax.dot_general(wr_ref[...], jnp.concatenate([hi, (xn - hi.astype(F32)).astype(BF16)], axis=1),
                             NT_DIMS, preferred_element_type=F32) for xn, hi in zip(xns, his)]
    for r0, part in zip(starts, parts):
        logits = part[:N_EXPERTS] + part[N_EXPERTS:]
        e = jnp.exp(logits - jnp.max(logits, axis=0, keepdims=True))
        aff_t = e / jnp.sum(e, axis=0, keepdims=True)
        afft_ref[:, r0:r0 + sub] = aff_t
        aff_ref[r0:r0 + sub, :] = jnp.concatenate([aff_t, pad], axis=0).T[:, :N_EXPERTS]


def _mix_out(x2, a, g, sga, sgg, wa, wb, wo, gain, wr, tm):
    T = x2.shape[0]
    row = lambda n: pl.BlockSpec((tm, n), lambda i: (i, 0))
    return pl.pallas_call(
        _mix_out_kernel,
        grid=(T // tm,),
        in_specs=[row(D_MODEL), row(ATT_Q_DIM), row(GLA_VAL_DIM), row(D_MODEL), row(D_MODEL),
                  _full(wa.shape), _full(wb.shape), _full(wo.shape), _full(gain.shape), _full(wr.shape)],
        out_specs=[row(D_MODEL), pl.BlockSpec((tm * ROW_TILES, LANES), lambda i: (i, 0)), row(N_EXPERTS),
                   pl.BlockSpec((N_EXPERTS, tm), lambda i: (0, i))],
        out_shape=[jax.ShapeDtypeStruct((T, D_MODEL), F32), jax.ShapeDtypeStruct((T * ROW_TILES, LANES), F32),
                   jax.ShapeDtypeStruct((T, N_EXPERTS), F32), jax.ShapeDtypeStruct((N_EXPERTS, T), F32)],
        compiler_params=_cparams("mix_out", ("parallel",)),
        name="mix_out",
    )(x2, a, g, sga, sgg, wa, wb, wo, gain, wr)


ROUTE_WAYS = 4


def _route_kernel(aff_ref, idx_ref, cum_ref, *, cap, seq):
    E = N_EXPERTS
    aff = aff_ref[...]
    count = lambda mask: jnp.sum(mask.astype(jnp.int32), axis=1, keepdims=True)
    as_float = lambda pattern: lax.bitcast_convert_type(pattern, F32)

    def thr_body(t, pattern):
        cand = pattern | jnp.left_shift(jnp.int32(1), 30 - t)
        return jnp.where(count(aff >= as_float(cand)) >= cap, cand, pattern)

    thr = as_float(lax.fori_loop(0, 31, thr_body, jnp.zeros((E, 1), jnp.int32)))
    above = aff > thr
    tie = aff == thr
    need = cap - count(above)

    pos = lax.broadcasted_iota(jnp.int32, (E, seq), 1)

    def tie_body(t, last):
        cand = last | jnp.left_shift(jnp.int32(1), (seq.bit_length() - 2) - t)
        return jnp.where(count(tie & (pos < cand)) < need, cand, last)

    last = lax.fori_loop(0, seq.bit_length() - 1, tie_body, jnp.zeros((E, 1), jnp.int32))
    sel = (above | (tie & (pos <= last))).astype(BF16)

    nt = seq // LANES
    lrow = lax.broadcasted_iota(jnp.int32, (LANES, LANES), 0)
    lcol = lax.broadcasted_iota(jnp.int32, (LANES, LANES), 1)
    tri = (lrow <= lcol).astype(BF16)
    mm = functools.partial(jnp.dot, preferred_element_type=F32)
    for t in range(nt):
        cum_ref[t * E:(t + 1) * E, :] = mm(sel[:, t * LANES:(t + 1) * LANES], tri)
    tile_of = (lax.broadcasted_iota(jnp.int32, (seq, LANES), 0) // LANES
               == lax.broadcasted_iota(jnp.int32, (seq, LANES), 1)).astype(BF16)
    per_tile = mm(sel, tile_of)
    lane = lax.broadcasted_iota(jnp.int32, (1, LANES), 1)
    far = jnp.float32(2 * seq)
    t_end = jnp.where(lane < nt, mm(per_tile.astype(BF16), tri), far)
    t_start = jnp.where(lane < nt, t_end - per_tile, far)
    pad = jnp.zeros((LANES - E, LANES), F32)
    t_start_cols = jnp.concatenate([jnp.where(lane < nt, t_start, 0.0), pad], axis=0).T

    slot = lax.broadcasted_iota(jnp.int32, (cap, LANES), 0).astype(F32)
    ones = jnp.ones((LANES, LANES), BF16)
    zrows = jnp.zeros((LANES - nt, 2 * LANES), F32)
    for e0 in range(0, E, ROUTE_WAYS):
        es = range(e0, e0 + ROUTE_WAYS)
        tiles, picks, whole = [], [], []
        for e in es:
            absc = cum_ref[pl.ds(e, nt, stride=E), :] + t_start_cols[0:nt, e:e + 1]
            hi = jnp.where(absc >= 256.0, 1.0, 0.0) + jnp.where(absc >= 512.0, 1.0, 0.0)
            lo = absc - 256.0 * hi
            tiles.append(jnp.concatenate([jnp.concatenate([lo, hi], axis=1), zrows], axis=0).astype(BF16))
            done = jnp.where(t_end[e:e + 1] <= slot, 1.0, 0.0)
            whole.append(done)
            picks.append((jnp.where(t_start[e:e + 1] <= slot, 1.0, 0.0) - done).astype(BF16))
        rows = [mm(p, w) for p, w in zip(picks, tiles)]
        votes = [(jnp.where(r[:, :LANES] + 256.0 * r[:, LANES:] <= slot, 1.0, 0.0) + float(LANES) * d).astype(BF16)
                 for r, d in zip(rows, whole)]
        for e, v in zip(es, votes):
            idx_ref[0, :, e:e + 1] = mm(v, ones)[:, e:e + 1].astype(jnp.int32)


def _route(aff_t, batch, cap):
    E, T = aff_t.shape
    B, S = batch, T // batch
    return pl.pallas_call(
        functools.partial(_route_kernel, cap=cap, seq=S),
        grid=(B,),
        in_specs=[pl.BlockSpec((E, S), lambda b: (0, b))],
        out_specs=pl.BlockSpec((1, cap, E), lambda b: (b, 0, 0)),
        out_shape=jax.ShapeDtypeStruct((B, cap, E), jnp.int32),
        scratch_shapes=[pltpu.VMEM((S // LANES * E, LANES), F32)],
        compiler_params=_cparams("route", ("parallel",)),
        name="route",
    )(aff_t)


MOE_EXPERTS_PER_STEP = 4


def _gather_kernel(idx_ref, xn_ref, aff_ref, xg_ref, wg_ref, ws_ref, *, cap):
    n = MOE_EXPERTS_PER_STEP
    lane = lax.broadcasted_iota(jnp.int32, (cap, N_EXPERTS), 1)
    for j in range(n):
        for i in range(cap):
            t = idx_ref[j, 0, i]
            src = pl.multiple_of(t * SUBLANES, SUBLANES)
            xg_ref[0, j, i * SUBLANES:(i + 1) * SUBLANES, :] = xn_ref[0, pl.ds(src, SUBLANES), :]
            ws_ref[j, i:i + 1, :] = aff_ref[0, pl.ds(t, 1), :]
        e = pl.program_id(1) * n + j
        wg_ref[0, j] = jnp.sum(jnp.where(lane == e, ws_ref[j], 0.0), axis=1, keepdims=True)


def _gather(idx, xn_tiles, aff, cap):
    B, S, E = aff.shape
    n = MOE_EXPERTS_PER_STEP
    return pl.pallas_call(
        functools.partial(_gather_kernel, cap=cap),
        grid=(B, E // n),
        in_specs=[pl.BlockSpec((n, 1, cap), lambda b, e: (b * (E // n) + e, 0, 0), memory_space=pltpu.SMEM),
                  pl.BlockSpec((1, S * SUBLANES, LANES), lambda b, e: (b, 0, 0)),
                  pl.BlockSpec((1, S, E), lambda b, e: (b, 0, 0))],
        out_specs=[pl.BlockSpec((1, n, cap * SUBLANES, LANES), lambda b, e: (b, e, 0, 0)),
                   pl.BlockSpec((1, n, cap, 1), lambda b, e: (b, e, 0, 0))],
        out_shape=[jax.ShapeDtypeStruct((B, E, cap * SUBLANES, LANES), F32),
                   jax.ShapeDtypeStruct((B, E, cap, 1), F32)],
        scratch_shapes=[pltpu.VMEM((n, cap, E), F32)],
        compiler_params=_cparams("gather", ("arbitrary", "arbitrary")),
        name="gather",
    )(idx.reshape(B * E, 1, cap), xn_tiles, aff)


def _ffn_kernel(xg_ref, wg_ref, w1_ref, w2_ref, w3_ref, y_ref, b1_ref, b2_ref, b3_ref):
    @pl.when(pl.program_id(1) == 0)
    def _():
        b1_ref[...] = w1_ref[0].astype(BF16)
        b2_ref[...] = w2_ref[0].astype(BF16)
        b3_ref[...] = w3_ref[0].astype(BF16)

    rows = wg_ref.shape[2]
    seqs = range(xg_ref.shape[0])
    mm = functools.partial(jnp.dot, preferred_element_type=F32)
    xgs = [_load_token_tiles(xg_ref.at[i, 0], rows).astype(BF16) for i in seqs]
    gates = [mm(xg, b1_ref[...]) for xg in xgs]
    ups = [mm(xg, b2_ref[...]) for xg in xgs]
    hids = [(gate * _sigmoid(gate) * up).astype(BF16) for gate, up in zip(gates, ups)]
    ys = [mm(hid, b3_ref[...]) * wg_ref[i, 0] for i, hid in zip(seqs, hids)]
    for i, y in zip(seqs, ys):
        _store_token_tiles(y_ref.at[i, 0], y)


FFN_SEQS_PER_STEP = 2


def _ffn(xg, wg, w1, w2, w3):
    B, E, C, _ = wg.shape
    _, D, F = w1.shape
    n = FFN_SEQS_PER_STEP
    return pl.pallas_call(
        _ffn_kernel,
        grid=(E, B // n),
        in_specs=[pl.BlockSpec((n, 1, C * ROW_TILES, LANES), lambda e, b: (b, e, 0, 0)),
                  pl.BlockSpec((n, 1, C, 1), lambda e, b: (b, e, 0, 0)),
                  pl.BlockSpec((1, D, F), lambda e, b: (e, 0, 0)),
                  pl.BlockSpec((1, D, F), lambda e, b: (e, 0, 0)),
                  pl.BlockSpec((1, F, D), lambda e, b: (e, 0, 0))],
        out_specs=pl.BlockSpec((n, 1, C * ROW_TILES, LANES), lambda e, b: (b, e, 0, 0)),
        out_shape=jax.ShapeDtypeStruct((B, E, C * ROW_TILES, LANES), F32),
        scratch_shapes=[pltpu.VMEM((D, F), BF16), pltpu.VMEM((D, F), BF16), pltpu.VMEM((F, D), BF16)],
        compiler_params=_cparams("ffn", ("arbitrary", "arbitrary")),
        name="ffn",
    )(xg, wg, w1, w2, w3)


SCATTER_BATCH = 16


def _scatter_kernel(idx_ref, y_ref, o_ref, *, cap):
    @pl.when(pl.program_id(1) == 0)
    def _():
        o_ref[...] = jnp.zeros_like(o_ref)

    for j in range(MOE_EXPERTS_PER_STEP):
        for i0 in range(0, cap, SCATTER_BATCH):
            slots = range(i0, i0 + SCATTER_BATCH)
            rows = [pl.ds(pl.multiple_of(idx_ref[j, 0, i] * SUBLANES, SUBLANES), SUBLANES) for i in slots]
            new = [o_ref[0, r, :] + y_ref[0, j, i * SUBLANES:(i + 1) * SUBLANES, :] for r, i in zip(rows, slots)]
            for r, v in zip(rows, new):
                o_ref[0, r, :] = v


def _scatter(idx, y_tiles, seq):
    B, E, cap = idx.shape
    n = MOE_EXPERTS_PER_STEP
    return pl.pallas_call(
        functools.partial(_scatter_kernel, cap=cap),
        grid=(B, E // n),
        in_specs=[pl.BlockSpec((n, 1, cap), lambda b, e: (b * (E // n) + e, 0, 0), memory_space=pltpu.SMEM),
                  pl.BlockSpec((1, n, cap * SUBLANES, LANES), lambda b, e: (b, e, 0, 0))],
        out_specs=pl.BlockSpec((1, seq * SUBLANES, LANES), lambda b, e: (b, 0, 0)),
        out_shape=jax.ShapeDtypeStruct((B, seq * SUBLANES, LANES), F32),
        compiler_params=_cparams("scatter", ("arbitrary", "arbitrary")),
        name="scatter",
    )(idx.reshape(B * E, 1, cap), y_tiles)


def _ple_out_kernel(h_ref, moe_ref, p_ref, gple_ref, wpg_ref, wple_ref, gfin_ref, o_ref):
    sub, starts = _sub_blocks(h_ref.shape[0])
    mm = functools.partial(jnp.dot, preferred_element_type=F32)
    hs = [h_ref[r0:r0 + sub, :] + _load_token_tiles(moe_ref, sub, r0) for r0 in starts]
    ns = [_rms(h, gple_ref[...]).astype(BF16) for h in hs]
    gates = [_sigmoid(mm(n, wpg_ref[...])) for n in ns]
    embs = [mm(p_ref[r0:r0 + sub, :].astype(BF16), wple_ref[...]) for r0 in starts]
    for r0, h, gate, emb in zip(starts, hs, gates, embs):
        o_ref[r0:r0 + sub, :] = _rms(h + gate * emb, gfin_ref[...])


def _ple_out(h, moe, p2, gple, wpg, wple, gfin, tm):
    T = h.shape[0]
    row = lambda n: pl.BlockSpec((tm, n), lambda i: (i, 0))
    return pl.pallas_call(
        _ple_out_kernel,
        grid=(T // tm,),
        in_specs=[row(D_MODEL), pl.BlockSpec((tm * ROW_TILES, LANES), lambda i: (i, 0)), row(PLE_DIM),
                  _full(gple.shape), _full(wpg.shape), _full(wple.shape), _full(gfin.shape)],
        out_specs=row(D_MODEL),
        out_shape=jax.ShapeDtypeStruct((T, D_MODEL), F32),
        compiler_params=_cparams("ple_out", ("parallel",)),
        name="ple_out",
    )(h, moe, p2, gple, wpg, wple, gfin)


IN_EDGES = {}
_o = 0
for _name, _n in (("q", ATT_Q_DIM), ("k", ATT_KV_DIM), ("v", ATT_KV_DIM), ("gqk", 2 * GLA_KEY_DIM),
                  ("gv", GLA_VAL_DIM), ("gr", GLA_VAL_DIM), ("z", 2 * GLA_GATE_RANK), ("gate", 2 * D_MODEL)):
    IN_EDGES[_name] = (_o, _o + _n)
    _o += _n
IN_DIM = _o


def _w_prep_kernel(wt_ref, wqt_ref, wkz_ref, wvt_ref, wgqk_ref, wgv_ref, wgr_ref, wgate_ref):
    piece = lambda name: wt_ref[IN_EDGES[name][0]:IN_EDGES[name][1], :]
    wqt_ref[...] = piece("q").astype(BF16)
    wvt_ref[...] = piece("v").astype(BF16)
    z = piece("z")
    wkz_ref[...] = jnp.concatenate([piece("k")] + [z] * (LANES // (2 * GLA_GATE_RANK)), axis=0).T.astype(BF16)
    wgqk_ref[...] = piece("gqk").T.astype(BF16)
    wgv_ref[...] = piece("gv").T.astype(BF16)
    wgr_ref[...] = piece("gr").T.astype(BF16)
    wgate_ref[...] = piece("gate").T.astype(BF16)


def _w_prep(wt, layer, cols):
    D = wt.shape[2]
    row = lambda n: pl.BlockSpec((cols, n), lambda i: (i, 0))
    col = lambda n: pl.BlockSpec((n, cols), lambda i: (0, i))
    widths = (ATT_KV_DIM + LANES, 2 * GLA_KEY_DIM, GLA_VAL_DIM, GLA_VAL_DIM, 2 * D_MODEL)
    out_specs = [col(ATT_Q_DIM), row(widths[0]), col(ATT_KV_DIM)] + [row(n) for n in widths[1:]]
    out_shape = ([jax.ShapeDtypeStruct((ATT_Q_DIM, D), BF16), jax.ShapeDtypeStruct((D, widths[0]), BF16),
                  jax.ShapeDtypeStruct((ATT_KV_DIM, D), BF16)]
                 + [jax.ShapeDtypeStruct((D, n), BF16) for n in widths[1:]])
    return pl.pallas_call(
        _w_prep_kernel,
        grid=(D // cols,),
        in_specs=[pl.BlockSpec((None, IN_DIM, cols), lambda i: (layer, 0, i))],
        out_specs=out_specs,
        out_shape=out_shape,
        compiler_params=_cparams("w_prep", ("parallel",)),
        name="w_prep",
    )(wt)


def kernel(x, p, positions, norm_mix, w_in, gla_gate_up_fwd, gla_gate_bias_fwd, gla_gate_up_bwd, gla_gate_bias_bwd, attn_sink, gla_norm, w_branch_attn, w_branch_gla, w_out, norm_ffn, w_router, w_exp_gate, w_exp_up, w_exp_down, norm_ple, w_ple_gate, w_ple, norm_final):
    B, S, D = x.shape
    T = B * S
    depth = w_in.shape[0]
    assert depth == 1, "the final norm is fused into the (single) layer's PLE kernel"
    cap = CAPACITY_FACTOR * S // N_EXPERTS
    R = GLA_GATE_RANK

    posr = positions.reshape(1, T)
    inv_freq = ROPE_THETA ** (-jnp.arange(0, ROPE_DIM, 2, dtype=F32) / ROPE_DIM)
    invfc = inv_freq.reshape(ROPE_DIM // 2, 1)

    h = x.reshape(T, D)
    for l in range(depth):
        w_pieces = _w_prep(jnp.swapaxes(w_in, 1, 2), l, cols=TILE_ROWS["w_prep"])
        per_head = lambda w: w.reshape(-1, GLA_HEADS, GLA_DK).swapaxes(0, 1)
        upf, upb = per_head(gla_gate_up_fwd[l]), per_head(gla_gate_up_bwd[l])
        up = jnp.concatenate([jnp.concatenate([upf, jnp.zeros_like(upf)], axis=2),
                              jnp.concatenate([jnp.zeros_like(upb), upb], axis=2)], axis=1)
        up_hi = up.astype(BF16)
        up_lo = (up - up_hi.astype(F32)).astype(BF16)
        upw = jnp.concatenate([up_hi, up_hi, up_lo, jnp.zeros_like(up_lo)], axis=1)
        gbias = jnp.concatenate([per_head(gla_gate_bias_fwd[l]), per_head(gla_gate_bias_bwd[l])], axis=2)
        wr = w_router[l]
        wr_hi = wr.astype(BF16)
        wr_lo = (wr - wr_hi.astype(F32)).astype(BF16)
        wr2 = jnp.concatenate([jnp.concatenate([wr_hi, wr_lo], axis=1),
                               jnp.concatenate([wr_hi, jnp.zeros_like(wr_lo)], axis=1)], axis=0).T

        qt, k0, k1, vt, gq, gk, gv, gr, z, sga, sgg = _in_proj(
            h, posr, invfc, norm_mix[l].reshape(1, D), *w_pieces, tm=TILE_ROWS["in_proj"])

        att = _swa(attn_sink[l], qt, k0.reshape(B, S, -1), k1.reshape(B, S, -1), vt, batch=B, tq=TILE_ROWS["swa"])
        gla = _gla(z.reshape(B, S, -1), gq.reshape(B, S, -1), gk.reshape(B, S, -1), gv.reshape(B, S, -1),
                   gr.reshape(B, S, -1), upw, gbias, gla_norm[l].reshape(1, -1))

        h1, xn, aff, aff_t = _mix_out(h, att.reshape(T, -1), gla.reshape(T, -1), sga, sgg,
                                      w_branch_attn[l].astype(BF16), w_branch_gla[l].astype(BF16),
                                      w_out[l].astype(BF16), norm_ffn[l].reshape(1, D), wr2, tm=TILE_ROWS["mix_out"])

        aff3 = aff.reshape(B, S, N_EXPERTS)
        idx = _route(aff_t, B, cap)
        idx = jnp.swapaxes(idx, 1, 2)
        xg, wg = _gather(idx, xn.reshape(B, S * SUBLANES, LANES), aff3, cap)
        y = _ffn(xg, wg, w_exp_gate[l], w_exp_up[l], w_exp_down[l])
        moe = _scatter(idx, y, S)

        h = _ple_out(h1, moe.reshape(T * ROW_TILES, LANES), p[l].reshape(T, PLE_DIM), norm_ple[l].reshape(1, D),
                     w_ple_gate[l].astype(BF16), w_ple[l].astype(BF16), norm_final.reshape(1, D),
                     tm=TILE_ROWS["ple_out"])
    return h.reshape(B, S, D)
```

```python
import functools
import math

import jax
import jax.numpy as jnp
from jax import lax
from jax.experimental import pallas as pl
from jax.experimental.pallas import tpu as pltpu

D_MODEL = 1024
ATT_HEADS = 8
ATT_KV_HEADS = 2
ATT_HEAD_DIM = 64
ATT_GROUP = ATT_HEADS // ATT_KV_HEADS
ATT_Q_DIM = ATT_HEADS * ATT_HEAD_DIM
ATT_KV_DIM = ATT_KV_HEADS * ATT_HEAD_DIM
WINDOW = 128
ROPE_DIM = ATT_HEAD_DIM // 4
ROPE_THETA = 500000.0
GLA_HEADS = 4
GLA_KEY_DIM = D_MODEL // 2
GLA_VAL_DIM = D_MODEL
GLA_DK = GLA_KEY_DIM // GLA_HEADS
GLA_DV = GLA_VAL_DIM // GLA_HEADS
GLA_GATE_RANK = 16
GLA_GATE_NORM = 16.0
GLA_CHUNK = 64
N_EXPERTS = 16
EXPERT_FF = D_MODEL
CAPACITY_FACTOR = 2
PLE_DIM = 256
EPS = 1e-6

LANES = 128
MIB = 1024 * 1024
BF16 = jnp.bfloat16
F32 = jnp.float32
LOG2E = math.log2(math.e)

NT_DIMS = (((1,), (1,)), ((), ()))
TN_DIMS = (((0,), (0,)), ((), ()))

TILE_ROWS = {"w_prep": 256, "in_proj": 512, "swa": 1024, "mix_out": 512, "ple_out": 1024}
VMEM_LIMIT_MIB = {"w_prep": 40, "in_proj": 56, "swa": 32, "gla": 48, "mix_out": 48, "route": 32, "gather": 60,
                  "ffn": 60, "scatter": 60, "ple_out": 48}


def _cparams(name, sem):
    return pltpu.CompilerParams(dimension_semantics=sem, vmem_limit_bytes=VMEM_LIMIT_MIB[name] * MIB)


def _full(shape):
    n = len(shape)
    return pl.BlockSpec(shape, lambda *_: (0,) * n)


def _rms(x, gain):
    ms = jnp.mean(x * x, axis=-1, keepdims=True)
    return x * lax.rsqrt(ms + EPS) * gain


def _sigmoid(x):
    return 0.5 * jnp.tanh(0.5 * x) + 0.5


SUBLANES = 8
ROW_TILES = D_MODEL // LANES


def _store_token_tiles(ref2d, x, first_row=0):
    rows = x.shape[0]
    for j in range(ROW_TILES):
        ref2d[pl.ds(first_row * ROW_TILES + j, rows, stride=ROW_TILES), :] = x[:, j * LANES:(j + 1) * LANES]


def _load_token_tiles(ref2d, rows, first_row=0):
    return jnp.concatenate([ref2d[pl.ds(first_row * ROW_TILES + j, rows, stride=ROW_TILES), :]
                            for j in range(ROW_TILES)], axis=1)


SUB_ROWS = 256


def _sub_blocks(tile_rows):
    return SUB_ROWS, list(range(0, tile_rows, SUB_ROWS))


def _rope_rows(t, cos_r, sin_r, heads):
    half = ROPE_DIM // 2
    rows = []
    for h in range(heads):
        r0 = h * ATT_HEAD_DIM
        t1, t2 = t[r0:r0 + half], t[r0 + half:r0 + ROPE_DIM]
        rows += [t1 * cos_r - t2 * sin_r, t2 * cos_r + t1 * sin_r, t[r0 + ROPE_DIM:r0 + ATT_HEAD_DIM]]
    return jnp.concatenate(rows, axis=0)


def _in_proj_kernel(x_ref, posr_ref, invfc_ref, gain_ref, wqt_ref, wkz_ref, wvt_ref,
                    wgqk_ref, wgv_ref, wgr_ref, wgate_ref,
                    qt_ref, k0_ref, k1_ref, vt_ref, gq_ref, gk_ref, gv_ref, gr_ref, z_ref, sga_ref, sgg_ref):
    a = _rms(x_ref[...], gain_ref[...]).astype(BF16)
    ang_t = invfc_ref[...] * posr_ref[...].astype(F32)
    cos_r, sin_r = jnp.cos(ang_t), jnp.sin(ang_t)

    qt = lax.dot_general(wqt_ref[...], a, NT_DIMS, preferred_element_type=F32)
    qt_ref[...] = (_rope_rows(qt, cos_r, sin_r, ATT_HEADS) * (ATT_HEAD_DIM ** -0.5 * LOG2E)).astype(BF16)

    kz = jnp.dot(a, wkz_ref[...], preferred_element_type=F32)
    z_ref[...] = kz[:, ATT_KV_DIM:]
    k = _rope_rows(kz[:, :ATT_KV_DIM].T, cos_r, sin_r, ATT_KV_HEADS).T.astype(BF16)
    k0_ref[...] = k[:, :ATT_HEAD_DIM]
    k1_ref[...] = k[:, ATT_HEAD_DIM:]
    vt_ref[...] = lax.dot_general(wvt_ref[...], a, NT_DIMS, preferred_element_type=F32).astype(BF16)

    gqk = jnp.dot(a, wgqk_ref[...], preferred_element_type=F32)
    gq_ref[...] = (gqk[:, :GLA_KEY_DIM] * (GLA_DK ** -0.5)).astype(BF16)
    gk_ref[...] = gqk[:, GLA_KEY_DIM:].astype(BF16)
    gv_ref[...] = jnp.dot(a, wgv_ref[...], preferred_element_type=F32).astype(BF16)
    gr = jnp.dot(a, wgr_ref[...], preferred_element_type=F32)
    gr_ref[...] = (gr * _sigmoid(gr)).astype(BF16)
    gates = jnp.dot(a, wgate_ref[...], preferred_element_type=F32)
    sga_ref[...] = _sigmoid(gates[:, :D_MODEL]).astype(BF16)
    sgg_ref[...] = _sigmoid(gates[:, D_MODEL:]).astype(BF16)


def _in_proj(x2, posr, invfc, gain, wqt, wkz, wvt, wgqk, wgv, wgr, wgate, tm):
    T = x2.shape[0]
    row = lambda n: pl.BlockSpec((tm, n), lambda i: (i, 0))
    col = lambda n: pl.BlockSpec((n, tm), lambda i: (0, i))
    row_widths = (ATT_HEAD_DIM, ATT_HEAD_DIM, None, GLA_KEY_DIM, GLA_KEY_DIM, GLA_VAL_DIM,
                  GLA_VAL_DIM, wkz.shape[1] - ATT_KV_DIM, D_MODEL, D_MODEL)
    row_dtypes = (BF16,) * 7 + (F32, BF16, BF16)
    out_specs = [col(ATT_Q_DIM)]
    out_shape = [jax.ShapeDtypeStruct((ATT_Q_DIM, T), BF16)]
    for n, dt in zip(row_widths, row_dtypes):
        if n is None:
            out_specs.append(col(ATT_KV_DIM))
            out_shape.append(jax.ShapeDtypeStruct((ATT_KV_DIM, T), BF16))
        else:
            out_specs.append(row(n))
            out_shape.append(jax.ShapeDtypeStruct((T, n), dt))
    consts = (invfc, gain, wqt, wkz, wvt, wgqk, wgv, wgr, wgate)
    return pl.pallas_call(
        _in_proj_kernel,
        grid=(T // tm,),
        in_specs=[row(D_MODEL), col(1)] + [_full(c.shape) for c in consts],
        out_specs=out_specs,
        out_shape=out_shape,
        compiler_params=_cparams("in_proj", ("parallel",)),
        name="in_proj",
    )(x2, posr, *consts)


def _swa_kernel(sink_ref, qt_ref, k0_ref, k1_ref, vt_ref, o_ref, *, tq, seq):
    blk = WINDOW
    span = 3 * blk
    hd = ATT_HEAD_DIM
    n = pl.program_id(1)
    ones = jnp.ones((16, span), BF16)
    kv_refs = (k0_ref, k1_ref)

    def window_start(sb):
        return pl.multiple_of(jnp.clip(n * tq + (sb - 1) * blk, 0, seq - span), blk)

    def scores(sb, g):
        kw = kv_refs[g][0, pl.ds(window_start(sb), span), :]
        heads = range(g * ATT_GROUP, (g + 1) * ATT_GROUP)
        qs = jnp.concatenate([qt_ref[h * hd:(h + 1) * hd, sb * blk:(sb + 1) * blk] for h in heads], axis=1)
        return jnp.dot(kw, qs, preferred_element_type=F32)

    work = [(sb, g) for sb in range(tq // blk) for g in range(ATT_KV_HEADS)]
    s_next = scores(*work[0])
    outs = []
    for step, (sb, g) in enumerate(work):
        s_all = s_next
        if step + 1 < len(work):
            s_next = scores(*work[step + 1])
        q0 = n * tq + sb * blk
        start = window_start(sb)
        kj = start + lax.broadcasted_iota(jnp.int32, (span, blk), 0)
        qi = q0 + lax.broadcasted_iota(jnp.int32, (span, blk), 1)
        valid = jnp.abs(qi - kj) <= WINDOW
        vaug = jnp.concatenate([vt_ref[g * hd:(g + 1) * hd, pl.ds(start, span)], ones], axis=0)
        sinks = [sink_ref[g * ATT_GROUP + i] * LOG2E for i in range(ATT_GROUP)]
        ss = [jnp.where(valid, s_all[:, i * blk:(i + 1) * blk], -jnp.inf) for i in range(ATT_GROUP)]
        ms = [jnp.maximum(jnp.max(s, axis=0, keepdims=True), sink) for s, sink in zip(ss, sinks)]
        es = jnp.concatenate([jnp.exp2(s - m).astype(BF16) for s, m in zip(ss, ms)], axis=1)
        r_all = jnp.dot(vaug, es, preferred_element_type=F32)
        rs = [r_all[:, i * blk:(i + 1) * blk] for i in range(ATT_GROUP)]
        outs += [r[:hd] / (r[hd:hd + 1] + jnp.exp2(sink - m)) for r, m, sink in zip(rs, ms, sinks)]
        if g == ATT_KV_HEADS - 1:
            for pr in range(ATT_HEADS // 2):
                pair = jnp.concatenate([outs[2 * pr], outs[2 * pr + 1]], axis=0)
                o_ref[0, sb * blk:(sb + 1) * blk, pr * 2 * hd:(pr + 1) * 2 * hd] = pair.T.astype(BF16)
            outs = []


def _swa(sink, qt, k0, k1, vt, batch, tq):
    S = k0.shape[1]
    nq = S // tq
    kspec = pl.BlockSpec((1, S, ATT_HEAD_DIM), lambda b, n: (b, 0, 0))
    return pl.pallas_call(
        functools.partial(_swa_kernel, tq=tq, seq=S),
        grid=(batch, nq),
        in_specs=[pl.BlockSpec(memory_space=pltpu.SMEM),
                  pl.BlockSpec((ATT_Q_DIM, tq), lambda b, n: (0, b * nq + n)),
                  kspec, kspec,
                  pl.BlockSpec((ATT_KV_DIM, S), lambda b, n: (0, b))],
        out_specs=pl.BlockSpec((1, tq, ATT_Q_DIM), lambda b, n: (b, n, 0)),
        out_shape=jax.ShapeDtypeStruct((batch, S, ATT_Q_DIM), BF16),
        compiler_params=_cparams("swa", ("parallel", "parallel")),
        name="swa",
    )(sink, qt, k0, k1, vt)


def _log2_sigmoid(u):
    return jnp.minimum(u, 0.0) * LOG2E - jnp.log2(1.0 + jnp.exp2(jnp.abs(u) * -LOG2E))


def _split2(x):
    hi = x.astype(BF16)
    return hi, (x - hi.astype(F32)).astype(BF16)


GLA_WAYS = 4


def _gla_kernel(z_ref, q_ref, k_ref, v_ref, r_ref, upw_ref, bias_ref, gain_ref,
                o_ref, cf_ref, cb_ref, kef_ref, keb_ref, st_ref, s_ref, *, seq):
    L = GLA_CHUNK
    R2 = 2 * GLA_GATE_RANK
    nc = seq // L
    grp = 4 * L
    cpg = grp // L
    dk = GLA_DK
    mm = functools.partial(jnp.dot, preferred_element_type=F32)
    nt = functools.partial(lax.dot_general, dimension_numbers=NT_DIMS, preferred_element_type=F32)

    row = lax.broadcasted_iota(jnp.int32, (grp, grp), 0)
    col = lax.broadcasted_iota(jnp.int32, (grp, grp), 1)
    same = (row // L) == (col // L)
    fwd_mask = same & (col <= row)
    bwd_mask = same & (col > row)
    tri_lo = jnp.where(fwd_mask, 1.0, 0.0).astype(BF16)
    tri_up = jnp.where(same & (col >= row), 1.0, 0.0).astype(BF16)
    lane = lax.broadcasted_iota(jnp.int32, (1, LANES), 1)
    use_lo = (lane >= R2) & (lane < 2 * R2)

    def group_starts(i):
        return [pl.multiple_of((i * GLA_WAYS + w) * grp, grp) for w in range(GLA_WAYS)]

    def cum_body(i, carry):
        r0s = group_starts(i)
        zs = [_split2(z_ref[0, pl.ds(r0, grp), :]) for r0 in r0s]
        us = [mm(jnp.where(use_lo, zl, zh), upw_ref[...]) + bias_ref[...] for zh, zl in zs]
        las = [_split2(_log2_sigmoid(u) * (1.0 / GLA_GATE_NORM)) for u in us]
        cfxs = [mm(tri_lo, jnp.concatenate([lh[:, :dk], ll[:, :dk]], axis=1)) for lh, ll in las]
        cbxs = [mm(tri_up, jnp.concatenate([lh[:, dk:], ll[:, dk:]], axis=1)) for lh, ll in las]
        for r0, cfx, cbx in zip(r0s, cfxs, cbxs):
            cf = cfx[:, :dk] + cfx[:, dk:]
            cb = cbx[:, :dk] + cbx[:, dk:]
            cf_ref[pl.ds(r0, grp), :] = cf
            cb_ref[pl.ds(r0, grp), :] = cb
            k = k_ref[0, pl.ds(r0, grp), :].astype(F32)
            for c in range(cpg):
                sl = slice(c * L, (c + 1) * L)
                gf = cf[(c + 1) * L - 1:(c + 1) * L]
                gb = cb[c * L:c * L + 1]
                kef_ref[pl.ds(r0 + c * L, L), :] = (k[sl] * jnp.exp2(gf - cf[sl])).astype(BF16)
                keb_ref[pl.ds(r0 + c * L, L), :] = (k[sl] * jnp.exp2(gb - cb[sl])).astype(BF16)
        return carry

    lax.fori_loop(0, seq // (grp * GLA_WAYS), cum_body, 0, unroll=True)

    s_ref[...] = jnp.zeros_like(s_ref)
    zero_k = jnp.zeros((L, dk), BF16)

    def state_body(i, carry):
        j = nc - 1 - i
        rf = pl.multiple_of(i * L, L)
        rb = pl.multiple_of(j * L, L)
        vcat = jnp.concatenate([v_ref[0, pl.ds(rf, L), :], v_ref[0, pl.ds(rb, L), :]], axis=0)
        kblk = jnp.concatenate([jnp.concatenate([kef_ref[pl.ds(rf, L), :], zero_k], axis=1),
                                jnp.concatenate([zero_k, keb_ref[pl.ds(rb, L), :]], axis=1)], axis=0)
        kv = lax.dot_general(vcat, kblk, TN_DIMS, preferred_element_type=F32)
        decay = jnp.exp2(jnp.concatenate([cf_ref[pl.ds(rf + L - 1, 1), :], cb_ref[pl.ds(rb, 1), :]], axis=1))
        s = s_ref[...]
        st_ref[i, :, 0:dk] = s[:, :dk].astype(BF16)
        st_ref[j, :, dk:2 * dk] = s[:, dk:].astype(BF16)
        s_ref[...] = s * decay + kv
        return carry

    lax.fori_loop(0, nc, state_body, 0, unroll=True)

    def out_body(i, carry):
        r0s = group_starts(i)
        ops = []
        for r0 in r0s:
            q = q_ref[0, pl.ds(r0, grp), :].astype(F32)
            k = k_ref[0, pl.ds(r0, grp), :].astype(F32)
            cf = cf_ref[pl.ds(r0, grp), :]
            cb = cb_ref[pl.ds(r0, grp), :]
            ops.append(((q * jnp.exp2(cf)).astype(BF16), (k * jnp.exp2(-cf)).astype(BF16),
                        (q * jnp.exp2(cb)).astype(BF16), (k * jnp.exp2(-cb)).astype(BF16)))
        scores = [(nt(qf, kf), nt(qb, kb)) for qf, kf, qb, kb in ops]
        attns = [jnp.where(fwd_mask, af, jnp.where(bwd_mask, ab, 0.0)).astype(BF16) for af, ab in scores]
        outs = []
        for r0, attn, (qf, _, qb, _) in zip(r0s, attns, ops):
            c0 = r0 // L
            qcat = jnp.concatenate([qf, qb], axis=1)
            inter = jnp.concatenate([nt(qcat[c * L:(c + 1) * L], st_ref[c0 + c]) for c in range(cpg)], axis=0)
            outs.append(mm(attn, v_ref[0, pl.ds(r0, grp), :]) + inter)
        for r0, o in zip(r0s, outs):
            o_ref[0, pl.ds(r0, grp), :] = (_rms(o, gain_ref[...]) * r_ref[0, pl.ds(r0, grp), :].astype(F32)).astype(BF16)
        return carry

    lax.fori_loop(0, seq // (grp * GLA_WAYS), out_body, 0, unroll=True)


def _gla(z, gq, gk, gv, gr, upw, bias, gain):
    B, S, _ = gq.shape
    nc = S // GLA_CHUNK
    seq_blk = lambda n: pl.BlockSpec((1, S, n), lambda b, h: (b, 0, h))
    head_blk = lambda r, n: pl.BlockSpec((None, r, n), lambda b, h: (h, 0, 0))
    return pl.pallas_call(
        functools.partial(_gla_kernel, seq=S),
        grid=(B, GLA_HEADS),
        in_specs=[pl.BlockSpec((1, S, LANES), lambda b, h: (b, 0, 0)),
                  seq_blk(GLA_DK), seq_blk(GLA_DK), seq_blk(GLA_DV), seq_blk(GLA_DV),
                  head_blk(LANES, 2 * GLA_DK), head_blk(1, 2 * GLA_DK),
                  pl.BlockSpec((1, GLA_DV), lambda b, h: (0, h))],
        out_specs=seq_blk(GLA_DV),
        out_shape=jax.ShapeDtypeStruct((B, S, GLA_VAL_DIM), BF16),
        scratch_shapes=[pltpu.VMEM((S, GLA_DK), F32), pltpu.VMEM((S, GLA_DK), F32),
                        pltpu.VMEM((S, GLA_DK), BF16), pltpu.VMEM((S, GLA_DK), BF16),
                        pltpu.VMEM((nc, GLA_DV, 2 * GLA_DK), BF16),
                        pltpu.VMEM((GLA_DV, 2 * GLA_DK), F32)],
        compiler_params=_cparams("gla", ("parallel", "parallel")),
        name="gla",
    )(z, gq, gk, gv, gr, upw, bias, gain)


def _mix_out_kernel(x_ref, a_ref, g_ref, sga_ref, sgg_ref, wa_ref, wb_ref, wo_ref, gain_ref, wr_ref,
                    h_ref, xn_ref, aff_ref, afft_ref):
    sub, starts = _sub_blocks(x_ref.shape[0])
    mm = functools.partial(jnp.dot, preferred_element_type=F32)
    blk = lambda ref, r0: ref[r0:r0 + sub, :]
    pad = jnp.zeros((LANES - N_EXPERTS, sub), F32)

    y_att = [mm(blk(a_ref, r0), wa_ref[...]) for r0 in starts]
    y_gla = [mm(blk(g_ref, r0), wb_ref[...]) for r0 in starts]
    merged = [(blk(sga_ref, r0).astype(F32) * ya + blk(sgg_ref, r0).astype(F32) * yg).astype(BF16)
              for r0, ya, yg in zip(starts, y_att, y_gla)]
    hs = [blk(x_ref, r0) + mm(m, wo_ref[...]) for r0, m in zip(starts, merged)]
    xns = [_rms(h, gain_ref[...]) for h in hs]
    for r0, h, xn in zip(starts, hs, xns):
        h_ref[r0:r0 + sub, :] = h
        _store_token_tiles(xn_ref, xn, r0)
    his = [xn.astype(BF16) for xn in xns]
    parts = [lax.dot_general(wr_ref[...], jnp.concatenate([hi, (xn - hi.astype(F32)).astype(BF16)], axis=1),
                             NT_DIMS, preferred_element_type=F32) for xn, hi in zip(xns, his)]
    for r0, part in zip(starts, parts):
        logits = part[:N_EXPERTS] + part[N_EXPERTS:]
        e = jnp.exp(logits - jnp.max(logits, axis=0, keepdims=True))
        aff_t = e / jnp.sum(e, axis=0, keepdims=True)
        afft_ref[:, r0:r0 + sub] = aff_t
        aff_ref[r0:r0 + sub, :] = jnp.concatenate([aff_t, pad], axis=0).T[:, :N_EXPERTS]


def _mix_out(x2, a, g, sga, sgg, wa, wb, wo, gain, wr, tm):
    T = x2.shape[0]
    row = lambda n: pl.BlockSpec((tm, n), lambda i: (i, 0))
    return pl.pallas_call(
        _mix_out_kernel,
        grid=(T // tm,),
        in_specs=[row(D_MODEL), row(ATT_Q_DIM), row(GLA_VAL_DIM), row(D_MODEL), row(D_MODEL),
                  _full(wa.shape), _full(wb.shape), _full(wo.shape), _full(gain.shape), _full(wr.shape)],
        out_specs=[row(D_MODEL), pl.BlockSpec((tm * ROW_TILES, LANES), lambda i: (i, 0)), row(N_EXPERTS),
                   pl.BlockSpec((N_EXPERTS, tm), lambda i: (0, i))],
        out_shape=[jax.ShapeDtypeStruct((T, D_MODEL), F32), jax.ShapeDtypeStruct((T * ROW_TILES, LANES), F32),
                   jax.ShapeDtypeStruct((T, N_EXPERTS), F32), jax.ShapeDtypeStruct((N_EXPERTS, T), F32)],
        compiler_params=_cparams("mix_out", ("parallel",)),
        name="mix_out",
    )(x2, a, g, sga, sgg, wa, wb, wo, gain, wr)


ROUTE_WAYS = 4


def _route_kernel(aff_ref, idx_ref, cum_ref, *, cap, seq):
    E = N_EXPERTS
    aff = aff_ref[...]
    count = lambda mask: jnp.sum(mask.astype(jnp.int32), axis=1, keepdims=True)
    as_float = lambda pattern: lax.bitcast_convert_type(pattern, F32)

    def thr_body(t, pattern):
        cand = pattern | jnp.left_shift(jnp.int32(1), 30 - t)
        return jnp.where(count(aff >= as_float(cand)) >= cap, cand, pattern)

    thr = as_float(lax.fori_loop(0, 31, thr_body, jnp.zeros((E, 1), jnp.int32)))
    above = aff > thr
    tie = aff == thr
    need = cap - count(above)

    pos = lax.broadcasted_iota(jnp.int32, (E, seq), 1)

    def tie_body(t, last):
        cand = last | jnp.left_shift(jnp.int32(1), (seq.bit_length() - 2) - t)
        return jnp.where(count(tie & (pos < cand)) < need, cand, last)

    last = lax.fori_loop(0, seq.bit_length() - 1, tie_body, jnp.zeros((E, 1), jnp.int32))
    sel = (above | (tie & (pos <= last))).astype(BF16)

    nt = seq // LANES
    lrow = lax.broadcasted_iota(jnp.int32, (LANES, LANES), 0)
    lcol = lax.broadcasted_iota(jnp.int32, (LANES, LANES), 1)
    tri = (lrow <= lcol).astype(BF16)
    mm = functools.partial(jnp.dot, preferred_element_type=F32)
    for t in range(nt):
        cum_ref[t * E:(t + 1) * E, :] = mm(sel[:, t * LANES:(t + 1) * LANES], tri)
    tile_of = (lax.broadcasted_iota(jnp.int32, (seq, LANES), 0) // LANES
               == lax.broadcasted_iota(jnp.int32, (seq, LANES), 1)).astype(BF16)
    per_tile = mm(sel, tile_of)
    lane = lax.broadcasted_iota(jnp.int32, (1, LANES), 1)
    far = jnp.float32(2 * seq)
    t_end = jnp.where(lane < nt, mm(per_tile.astype(BF16), tri), far)
    t_start = jnp.where(lane < nt, t_end - per_tile, far)
    pad = jnp.zeros((LANES - E, LANES), F32)
    t_start_cols = jnp.concatenate([jnp.where(lane < nt, t_start, 0.0), pad], axis=0).T

    slot = lax.broadcasted_iota(jnp.int32, (cap, LANES), 0).astype(F32)
    ones = jnp.ones((LANES, LANES), BF16)
    zrows = jnp.zeros((LANES - nt, 2 * LANES), F32)
    for e0 in range(0, E, ROUTE_WAYS):
        es = range(e0, e0 + ROUTE_WAYS)
        tiles, picks, whole = [], [], []
        for e in es:
            absc = cum_ref[pl.ds(e, nt, stride=E), :] + t_start_cols[0:nt, e:e + 1]
            hi = jnp.where(absc >= 256.0, 1.0, 0.0) + jnp.where(absc >= 512.0, 1.0, 0.0)
            lo = absc - 256.0 * hi
            tiles.append(jnp.concatenate([jnp.concatenate([lo, hi], axis=1), zrows], axis=0).astype(BF16))
            done = jnp.where(t_end[e:e + 1] <= slot, 1.0, 0.0)
            whole.append(done)
            picks.append((jnp.where(t_start[e:e + 1] <= slot, 1.0, 0.0) - done).astype(BF16))
        rows = [mm(p, w) for p, w in zip(picks, tiles)]
        votes = [(jnp.where(r[:, :LANES] + 256.0 * r[:, LANES:] <= slot, 1.0, 0.0) + float(LANES) * d).astype(BF16)
                 for r, d in zip(rows, whole)]
        for e, v in zip(es, votes):
            idx_ref[0, :, e:e + 1] = mm(v, ones)[:, e:e + 1].astype(jnp.int32)


def _route(aff_t, batch, cap):
    E, T = aff_t.shape
    B, S = batch, T // batch
    return pl.pallas_call(
        functools.partial(_route_kernel, cap=cap, seq=S),
        grid=(B,),
        in_specs=[pl.BlockSpec((E, S), lambda b: (0, b))],
        out_specs=pl.BlockSpec((1, cap, E), lambda b: (b, 0, 0)),
        out_shape=jax.ShapeDtypeStruct((B, cap, E), jnp.int32),
        scratch_shapes=[pltpu.VMEM((S // LANES * E, LANES), F32)],
        compiler_params=_cparams("route", ("parallel",)),
        name="route",
    )(aff_t)


MOE_EXPERTS_PER_STEP = 4


def _gather_kernel(idx_ref, xn_ref, aff_ref, xg_ref, wg_ref, ws_ref, *, cap):
    n = MOE_EXPERTS_PER_STEP
    lane = lax.broadcasted_iota(jnp.int32, (cap, N_EXPERTS), 1)
    for j in range(n):
        for i in range(cap):
            t = idx_ref[j, 0, i]
            src = pl.multiple_of(t * SUBLANES, SUBLANES)
            xg_ref[0, j, i * SUBLANES:(i + 1) * SUBLANES, :] = xn_ref[0, pl.ds(src, SUBLANES), :]
            ws_ref[j, i:i + 1, :] = aff_ref[0, pl.ds(t, 1), :]
        e = pl.program_id(1) * n + j
        wg_ref[0, j] = jnp.sum(jnp.where(lane == e, ws_ref[j], 0.0), axis=1, keepdims=True)


def _gather(idx, xn_tiles, aff, cap):
    B, S, E = aff.shape
    n = MOE_EXPERTS_PER_STEP
    return pl.pallas_call(
        functools.partial(_gather_kernel, cap=cap),
        grid=(B, E // n),
        in_specs=[pl.BlockSpec((n, 1, cap), lambda b, e: (b * (E // n) + e, 0, 0), memory_space=pltpu.SMEM),
                  pl.BlockSpec((1, S * SUBLANES, LANES), lambda b, e: (b, 0, 0)),
                  pl.BlockSpec((1, S, E), lambda b, e: (b, 0, 0))],
        out_specs=[pl.BlockSpec((1, n, cap * SUBLANES, LANES), lambda b, e: (b, e, 0, 0)),
                   pl.BlockSpec((1, n, cap, 1), lambda b, e: (b, e, 0, 0))],
        out_shape=[jax.ShapeDtypeStruct((B, E, cap * SUBLANES, LANES), F32),
                   jax.ShapeDtypeStruct((B, E, cap, 1), F32)],
        scratch_shapes=[pltpu.VMEM((n, cap, E), F32)],
        compiler_params=_cparams("gather", ("arbitrary", "arbitrary")),
        name="gather",
    )(idx.reshape(B * E, 1, cap), xn_tiles, aff)


def _ffn_kernel(xg_ref, wg_ref, w1_ref, w2_ref, w3_ref, y_ref, b1_ref, b2_ref, b3_ref):
    @pl.when(pl.program_id(1) == 0)
    def _():
        b1_ref[...] = w1_ref[0].astype(BF16)
        b2_ref[...] = w2_ref[0].astype(BF16)
        b3_ref[...] = w3_ref[0].astype(BF16)

    rows = wg_ref.shape[2]
    seqs = range(xg_ref.shape[0])
    mm = functools.partial(jnp.dot, preferred_element_type=F32)
    xgs = [_load_token_tiles(xg_ref.at[i, 0], rows).astype(BF16) for i in seqs]
    gates = [mm(xg, b1_ref[...]) for xg in xgs]
    ups = [mm(xg, b2_ref[...]) for xg in xgs]
    hids = [(gate * _sigmoid(gate) * up).astype(BF16) for gate, up in zip(gates, ups)]
    ys = [mm(hid, b3_ref[...]) * wg_ref[i, 0] for i, hid in zip(seqs, hids)]
    for i, y in zip(seqs, ys):
        _store_token_tiles(y_ref.at[i, 0], y)


FFN_SEQS_PER_STEP = 2


def _ffn(xg, wg, w1, w2, w3):
    B, E, C, _ = wg.shape
    _, D, F = w1.shape
    n = FFN_SEQS_PER_STEP
    return pl.pallas_call(
        _ffn_kernel,
        grid=(E, B // n),
        in_specs=[pl.BlockSpec((n, 1, C * ROW_TILES, LANES), lambda e, b: (b, e, 0, 0)),
                  pl.BlockSpec((n, 1, C, 1), lambda e, b: (b, e, 0, 0)),
                  pl.BlockSpec((1, D, F), lambda e, b: (e, 0, 0)),
                  pl.BlockSpec((1, D, F), lambda e, b: (e, 0, 0)),
                  pl.BlockSpec((1, F, D), lambda e, b: (e, 0, 0))],
        out_specs=pl.BlockSpec((n, 1, C * ROW_TILES, LANES), lambda e, b: (b, e, 0, 0)),
        out_shape=jax.ShapeDtypeStruct((B, E, C * ROW_TILES, LANES), F32),
        scratch_shapes=[pltpu.VMEM((D, F), BF16), pltpu.VMEM((D, F), BF16), pltpu.VMEM((F, D), BF16)],
        compiler_params=_cparams("ffn", ("arbitrary", "arbitrary")),
        name="ffn",
    )(xg, wg, w1, w2, w3)


SCATTER_BATCH = 16


def _scatter_kernel(idx_ref, y_ref, o_ref, *, cap):
    @pl.when(pl.program_id(1) == 0)
    def _():
        o_ref[...] = jnp.zeros_like(o_ref)

    for j in range(MOE_EXPERTS_PER_STEP):
        for i0 in range(0, cap, SCATTER_BATCH):
            slots = range(i0, i0 + SCATTER_BATCH)
            rows = [pl.ds(pl.multiple_of(idx_ref[j, 0, i] * SUBLANES, SUBLANES), SUBLANES) for i in slots]
            new = [o_ref[0, r, :] + y_ref[0, j, i * SUBLANES:(i + 1) * SUBLANES, :] for r, i in zip(rows, slots)]
            for r, v in zip(rows, new):
                o_ref[0, r, :] = v


def _scatter(idx, y_tiles, seq):
    B, E, cap = idx.shape
    n = MOE_EXPERTS_PER_STEP
    return pl.pallas_call(
        functools.partial(_scatter_kernel, cap=cap),
        grid=(B, E // n),
        in_specs=[pl.BlockSpec((n, 1, cap), lambda b, e: (b * (E // n) + e, 0, 0), memory_space=pltpu.SMEM),
                  pl.BlockSpec((1, n, cap * SUBLANES, LANES), lambda b, e: (b, e, 0, 0))],
        out_specs=pl.BlockSpec((1, seq * SUBLANES, LANES), lambda b, e: (b, 0, 0)),
        out_shape=jax.ShapeDtypeStruct((B, seq * SUBLANES, LANES), F32),
        compiler_params=_cparams("scatter", ("arbitrary", "arbitrary")),
        name="scatter",
    )(idx.reshape(B * E, 1, cap), y_tiles)


def _ple_out_kernel(h_ref, moe_ref, p_ref, gple_ref, wpg_ref, wple_ref, gfin_ref, o_ref):
    sub, starts = _sub_blocks(h_ref.shape[0])
    mm = functools.partial(jnp.dot, preferred_element_type=F32)
    hs = [h_ref[r0:r0 + sub, :] + _load_token_tiles(moe_ref, sub, r0) for r0 in starts]
    ns = [_rms(h, gple_ref[...]).astype(BF16) for h in hs]
    gates = [_sigmoid(mm(n, wpg_ref[...])) for n in ns]
    embs = [mm(p_ref[r0:r0 + sub, :].astype(BF16), wple_ref[...]) for r0 in starts]
    for r0, h, gate, emb in zip(starts, hs, gates, embs):
        o_ref[r0:r0 + sub, :] = _rms(h + gate * emb, gfin_ref[...])


def _ple_out(h, moe, p2, gple, wpg, wple, gfin, tm):
    T = h.shape[0]
    row = lambda n: pl.BlockSpec((tm, n), lambda i: (i, 0))
    return pl.pallas_call(
        _ple_out_kernel,
        grid=(T // tm,),
        in_specs=[row(D_MODEL), pl.BlockSpec((tm * ROW_TILES, LANES), lambda i: (i, 0)), row(PLE_DIM),
                  _full(gple.shape), _full(wpg.shape), _full(wple.shape), _full(gfin.shape)],
        out_specs=row(D_MODEL),
        out_shape=jax.ShapeDtypeStruct((T, D_MODEL), F32),
        compiler_params=_cparams("ple_out", ("parallel",)),
        name="ple_out",
    )(h, moe, p2, gple, wpg, wple, gfin)


IN_EDGES = {}
_o = 0
for _name, _n in (("q", ATT_Q_DIM), ("k", ATT_KV_DIM), ("v", ATT_KV_DIM), ("gqk", 2 * GLA_KEY_DIM),
                  ("gv", GLA_VAL_DIM), ("gr", GLA_VAL_DIM), ("z", 2 * GLA_GATE_RANK), ("gate", 2 * D_MODEL)):
    IN_EDGES[_name] = (_o, _o + _n)
    _o += _n
IN_DIM = _o


def _w_prep_kernel(wt_ref, wqt_ref, wkz_ref, wvt_ref, wgqk_ref, wgv_ref, wgr_ref, wgate_ref):
    piece = lambda name: wt_ref[IN_EDGES[name][0]:IN_EDGES[name][1], :]
    wqt_ref[...] = piece("q").astype(BF16)
    wvt_ref[...] = piece("v").astype(BF16)
    z = piece("z")
    wkz_ref[...] = jnp.concatenate([piece("k")] + [z] * (LANES // (2 * GLA_GATE_RANK)), axis=0).T.astype(BF16)
    wgqk_ref[...] = piece("gqk").T.astype(BF16)
    wgv_ref[...] = piece("gv").T.astype(BF16)
    wgr_ref[...] = piece("gr").T.astype(BF16)
    wgate_ref[...] = piece("gate").T.astype(BF16)


def _w_prep(wt, layer, cols):
    D = wt.shape[2]
    row = lambda n: pl.BlockSpec((cols, n), lambda i: (i, 0))
    col = lambda n: pl.BlockSpec((n, cols), lambda i: (0, i))
    widths = (ATT_KV_DIM + LANES, 2 * GLA_KEY_DIM, GLA_VAL_DIM, GLA_VAL_DIM, 2 * D_MODEL)
    out_specs = [col(ATT_Q_DIM), row(widths[0]), col(ATT_KV_DIM)] + [row(n) for n in widths[1:]]
    out_shape = ([jax.ShapeDtypeStruct((ATT_Q_DIM, D), BF16), jax.ShapeDtypeStruct((D, widths[0]), BF16),
                  jax.ShapeDtypeStruct((ATT_KV_DIM, D), BF16)]
                 + [jax.ShapeDtypeStruct((D, n), BF16) for n in widths[1:]])
    return pl.pallas_call(
        _w_prep_kernel,
        grid=(D // cols,),
        in_specs=[pl.BlockSpec((None, IN_DIM, cols), lambda i: (layer, 0, i))],
        out_specs=out_specs,
        out_shape=out_shape,
        compiler_params=_cparams("w_prep", ("parallel",)),
        name="w_prep",
    )(wt)


def kernel(x, p, positions, norm_mix, w_in, gla_gate_up_fwd, gla_gate_bias_fwd, gla_gate_up_bwd, gla_gate_bias_bwd, attn_sink, gla_norm, w_branch_attn, w_branch_gla, w_out, norm_ffn, w_router, w_exp_gate, w_exp_up, w_exp_down, norm_ple, w_ple_gate, w_ple, norm_final):
    B, S, D = x.shape
    T = B * S
    depth = w_in.shape[0]
    assert depth == 1, "the final norm is fused into the (single) layer's PLE kernel"
    cap = CAPACITY_FACTOR * S // N_EXPERTS
    R = GLA_GATE_RANK

    posr = positions.reshape(1, T)
    inv_freq = ROPE_THETA ** (-jnp.arange(0, ROPE_DIM, 2, dtype=F32) / ROPE_DIM)
    invfc = inv_freq.reshape(ROPE_DIM // 2, 1)

    h = x.reshape(T, D)
    for l in range(depth):
        w_pieces = _w_prep(jnp.swapaxes(w_in, 1, 2), l, cols=TILE_ROWS["w_prep"])
        per_head = lambda w: w.reshape(-1, GLA_HEADS, GLA_DK).swapaxes(0, 1)
        upf, upb = per_head(gla_gate_up_fwd[l]), per_head(gla_gate_up_bwd[l])
        up = jnp.concatenate([jnp.concatenate([upf, jnp.zeros_like(upf)], axis=2),
                              jnp.concatenate([jnp.zeros_like(upb), upb], axis=2)], axis=1)
        up_hi = up.astype(BF16)
        up_lo = (up - up_hi.astype(F32)).astype(BF16)
        upw = jnp.concatenate([up_hi, up_hi, up_lo, jnp.zeros_like(up_lo)], axis=1)
        gbias = jnp.concatenate([per_head(gla_gate_bias_fwd[l]), per_head(gla_gate_bias_bwd[l])], axis=2)
        wr = w_router[l]
        wr_hi = wr.astype(BF16)
        wr_lo = (wr - wr_hi.astype(F32)).astype(BF16)
        wr2 = jnp.concatenate([jnp.concatenate([wr_hi, wr_lo], axis=1),
                               jnp.concatenate([wr_hi, jnp.zeros_like(wr_lo)], axis=1)], axis=0).T

        qt, k0, k1, vt, gq, gk, gv, gr, z, sga, sgg = _in_proj(
            h, posr, invfc, norm_mix[l].reshape(1, D), *w_pieces, tm=TILE_ROWS["in_proj"])

        att = _swa(attn_sink[l], qt, k0.reshape(B, S, -1), k1.reshape(B, S, -1), vt, batch=B, tq=TILE_ROWS["swa"])
        gla = _gla(z.reshape(B, S, -1), gq.reshape(B, S, -1), gk.reshape(B, S, -1), gv.reshape(B, S, -1),
                   gr.reshape(B, S, -1), upw, gbias, gla_norm[l].reshape(1, -1))

        h1, xn, aff, aff_t = _mix_out(h, att.reshape(T, -1), gla.reshape(T, -1), sga, sgg,
                                      w_branch_attn[l].astype(BF16), w_branch_gla[l].astype(BF16),
                                      w_out[l].astype(BF16), norm_ffn[l].reshape(1, D), wr2, tm=TILE_ROWS["mix_out"])

        aff3 = aff.reshape(B, S, N_EXPERTS)
        idx = _route(aff_t, B, cap)
        idx = jnp.swapaxes(idx, 1, 2)
        xg, wg = _gather(idx, xn.reshape(B, S * SUBLANES, LANES), aff3, cap)
        y = _ffn(xg, wg, w_exp_gate[l], w_exp_up[l], w_exp_down[l])
        moe = _scatter(idx, y, S)

        h = _ple_out(h1, moe.reshape(T * ROW_TILES, LANES), p[l].reshape(T, PLE_DIM), norm_ple[l].reshape(1, D),
                     w_ple_gate[l].astype(BF16), w_ple[l].astype(BF16), norm_final.reshape(1, D),
                     tm=TILE_ROWS["ple_out"])
    return h.reshape(B, S, D)
```

```python
import functools
import math

import jax
import jax.numpy as jnp
from jax import lax
from jax.experimental import pallas as pl
from jax.experimental.pallas import tpu as pltpu

D_MODEL = 1024
ATT_HEADS = 8
ATT_KV_HEADS = 2
ATT_HEAD_DIM = 64
ATT_GROUP = ATT_HEADS // ATT_KV_HEADS
ATT_Q_DIM = ATT_HEADS * ATT_HEAD_DIM
ATT_KV_DIM = ATT_KV_HEADS * ATT_HEAD_DIM
WINDOW = 128
ROPE_DIM = ATT_HEAD_DIM // 4
ROPE_THETA = 500000.0
GLA_HEADS = 4
GLA_KEY_DIM = D_MODEL // 2
GLA_VAL_DIM = D_MODEL
GLA_DK = GLA_KEY_DIM // GLA_HEADS
GLA_DV = GLA_VAL_DIM // GLA_HEADS
GLA_GATE_RANK = 16
GLA_GATE_NORM = 16.0
GLA_CHUNK = 64
N_EXPERTS = 16
EXPERT_FF = D_MODEL
CAPACITY_FACTOR = 2
PLE_DIM = 256
EPS = 1e-6

LANES = 128
MIB = 1024 * 1024
BF16 = jnp.bfloat16
F32 = jnp.float32
LOG2E = math.log2(math.e)

NT_DIMS = (((1,), (1,)), ((), ()))
TN_DIMS = (((0,), (0,)), ((), ()))

TILE_ROWS = {"w_prep": 256, "in_proj": 512, "swa": 1024, "mix_out": 512, "ple_out": 1024}
VMEM_LIMIT_MIB = {"w_prep": 40, "in_proj": 56, "swa": 32, "gla": 48, "mix_out": 48, "route": 32, "gather": 60,
                  "ffn": 60, "scatter": 60, "ple_out": 48}


def _cparams(name, sem):
    return pltpu.CompilerParams(dimension_semantics=sem, vmem_limit_bytes=VMEM_LIMIT_MIB[name] * MIB)


def _full(shape):
    n = len(shape)
    return pl.BlockSpec(shape, lambda *_: (0,) * n)


def _rms(x, gain):
    ms = jnp.mean(x * x, axis=-1, keepdims=True)
    return x * lax.rsqrt(ms + EPS) * gain


def _sigmoid(x):
    return 0.5 * jnp.tanh(0.5 * x) + 0.5


SUBLANES = 8
ROW_TILES = D_MODEL // LANES


def _store_token_tiles(ref2d, x, first_row=0):
    rows = x.shape[0]
    for j in range(ROW_TILES):
        ref2d[pl.ds(first_row * ROW_TILES + j, rows, stride=ROW_TILES), :] = x[:, j * LANES:(j + 1) * LANES]


def _load_token_tiles(ref2d, rows, first_row=0):
    return jnp.concatenate([ref2d[pl.ds(first_row * ROW_TILES + j, rows, stride=ROW_TILES), :]
                            for j in range(ROW_TILES)], axis=1)


SUB_ROWS = 256


def _sub_blocks(tile_rows):
    return SUB_ROWS, list(range(0, tile_rows, SUB_ROWS))


def _rope_rows(t, cos_r, sin_r, heads):
    half = ROPE_DIM // 2
    rows = []
    for h in range(heads):
        r0 = h * ATT_HEAD_DIM
        t1, t2 = t[r0:r0 + half], t[r0 + half:r0 + ROPE_DIM]
        rows += [t1 * cos_r - t2 * sin_r, t2 * cos_r + t1 * sin_r, t[r0 + ROPE_DIM:r0 + ATT_HEAD_DIM]]
    return jnp.concatenate(rows, axis=0)


def _in_proj_kernel(x_ref, posr_ref, invfc_ref, gain_ref, wqt_ref, wkz_ref, wvt_ref,
                    wgqk_ref, wgv_ref, wgr_ref, wgate_ref,
                    qt_ref, k0_ref, k1_ref, vt_ref, gq_ref, gk_ref, gv_ref, gr_ref, z_ref, sga_ref, sgg_ref):
    a = _rms(x_ref[...], gain_ref[...]).astype(BF16)
    ang_t = invfc_ref[...] * posr_ref[...].astype(F32)
    cos_r, sin_r = jnp.cos(ang_t), jnp.sin(ang_t)

    qt = lax.dot_general(wqt_ref[...], a, NT_DIMS, preferred_element_type=F32)
    qt_ref[...] = (_rope_rows(qt, cos_r, sin_r, ATT_HEADS) * (ATT_HEAD_DIM ** -0.5 * LOG2E)).astype(BF16)

    kz = jnp.dot(a, wkz_ref[...], preferred_element_type=F32)
    z_ref[...] = kz[:, ATT_KV_DIM:]
    k = _rope_rows(kz[:, :ATT_KV_DIM].T, cos_r, sin_r, ATT_KV_HEADS).T.astype(BF16)
    k0_ref[...] = k[:, :ATT_HEAD_DIM]
    k1_ref[...] = k[:, ATT_HEAD_DIM:]
    vt_ref[...] = lax.dot_general(wvt_ref[...], a, NT_DIMS, preferred_element_type=F32).astype(BF16)

    gqk = jnp.dot(a, wgqk_ref[...], preferred_element_type=F32)
    gq_ref[...] = (gqk[:, :GLA_KEY_DIM] * (GLA_DK ** -0.5)).astype(BF16)
    gk_ref[...] = gqk[:, GLA_KEY_DIM:].astype(BF16)
    gv_ref[...] = jnp.dot(a, wgv_ref[...], preferred_element_type=F32).astype(BF16)
    gr = jnp.dot(a, wgr_ref[...], preferred_element_type=F32)
    gr_ref[...] = (gr * _sigmoid(gr)).astype(BF16)
    gates = jnp.dot(a, wgate_ref[...], preferred_element_type=F32)
    sga_ref[...] = _sigmoid(gates[:, :D_MODEL]).astype(BF16)
    sgg_ref[...] = _sigmoid(gates[:, D_MODEL:]).astype(BF16)


def _in_proj(x2, posr, invfc, gain, wqt, wkz, wvt, wgqk, wgv, wgr, wgate, tm):
    T = x2.shape[0]
    row = lambda n: pl.BlockSpec((tm, n), lambda i: (i, 0))
    col = lambda n: pl.BlockSpec((n, tm), lambda i: (0, i))
    row_widths = (ATT_HEAD_DIM, ATT_HEAD_DIM, None, GLA_KEY_DIM, GLA_KEY_DIM, GLA_VAL_DIM,
                  GLA_VAL_DIM, wkz.shape[1] - ATT_KV_DIM, D_MODEL, D_MODEL)
    row_dtypes = (BF16,) * 7 + (F32, BF16, BF16)
    out_specs = [col(ATT_Q_DIM)]
    out_shape = [jax.ShapeDtypeStruct((ATT_Q_DIM, T), BF16)]
    for n, dt in zip(row_widths, row_dtypes):
        if n is None:
            out_specs.append(col(ATT_KV_DIM))
            out_shape.append(jax.ShapeDtypeStruct((ATT_KV_DIM, T), BF16))
        else:
            out_specs.append(row(n))
            out_shape.append(jax.ShapeDtypeStruct((T, n), dt))
    consts = (invfc, gain, wqt, wkz, wvt, wgqk, wgv, wgr, wgate)
    return pl.pallas_call(
        _in_proj_kernel,
        grid=(T // tm,),
        in_specs=[row(D_MODEL), col(1)] + [_full(c.shape) for c in consts],
        out_specs=out_specs,
        out_shape=out_shape,
        compiler_params=_cparams("in_proj", ("parallel",)),
        name="in_proj",
    )(x2, posr, *consts)


def _swa_kernel(sink_ref, qt_ref, k0_ref, k1_ref, vt_ref, o_ref, *, tq, seq):
    blk = WINDOW
    span = 3 * blk
    hd = ATT_HEAD_DIM
    n = pl.program_id(1)
    ones = jnp.ones((16, span), BF16)
    kv_refs = (k0_ref, k1_ref)

    def window_start(sb):
        return pl.multiple_of(jnp.clip(n * tq + (sb - 1) * blk, 0, seq - span), blk)

    def scores(sb, g):
        kw = kv_refs[g][0, pl.ds(window_start(sb), span), :]
        heads = range(g * ATT_GROUP, (g + 1) * ATT_GROUP)
        qs = jnp.concatenate([qt_ref[h * hd:(h + 1) * hd, sb * blk:(sb + 1) * blk] for h in heads], axis=1)
        return jnp.dot(kw, qs, preferred_element_type=F32)

    work = [(sb, g) for sb in range(tq // blk) for g in range(ATT_KV_HEADS)]
    s_next = scores(*work[0])
    outs = []
    for step, (sb, g) in enumerate(work):
        s_all = s_next
        if step + 1 < len(work):
            s_next = scores(*work[step + 1])
        q0 = n * tq + sb * blk
        start = window_start(sb)
        kj = start + lax.broadcasted_iota(jnp.int32, (span, blk), 0)
        qi = q0 + lax.broadcasted_iota(jnp.int32, (span, blk), 1)
        valid = jnp.abs(qi - kj) <= WINDOW
        vaug = jnp.concatenate([vt_ref[g * hd:(g + 1) * hd, pl.ds(start, span)], ones], axis=0)
        sinks = [sink_ref[g * ATT_GROUP + i] * LOG2E for i in range(ATT_GROUP)]
        ss = [jnp.where(valid, s_all[:, i * blk:(i + 1) * blk], -jnp.inf) for i in range(ATT_GROUP)]
        ms = [jnp.maximum(jnp.max(s, axis=0, keepdims=True), sink) for s, sink in zip(ss, sinks)]
        es = jnp.concatenate([jnp.exp2(s - m).astype(BF16) for s, m in zip(ss, ms)], axis=1)
        r_all = jnp.dot(vaug, es, preferred_element_type=F32)
        rs = [r_all[:, i * blk:(i + 1) * blk] for i in range(ATT_GROUP)]
        outs += [r[:hd] / (r[hd:hd + 1] + jnp.exp2(sink - m)) for r, m, sink in zip(rs, ms, sinks)]
        if g == ATT_KV_HEADS - 1:
            for pr in range(ATT_HEADS // 2):
                pair = jnp.concatenate([outs[2 * pr], outs[2 * pr + 1]], axis=0)
                o_ref[0, sb * blk:(sb + 1) * blk, pr * 2 * hd:(pr + 1) * 2 * hd] = pair.T.astype(BF16)
            outs = []


def _swa(sink, qt, k0, k1, vt, batch, tq):
    S = k0.shape[1]
    nq = S // tq
    kspec = pl.BlockSpec((1, S, ATT_HEAD_DIM), lambda b, n: (b, 0, 0))
    return pl.pallas_call(
        functools.partial(_swa_kernel, tq=tq, seq=S),
        grid=(batch, nq),
        in_specs=[pl.BlockSpec(memory_space=pltpu.SMEM),
                  pl.BlockSpec((ATT_Q_DIM, tq), lambda b, n: (0, b * nq + n)),
                  kspec, kspec,
                  pl.BlockSpec((ATT_KV_DIM, S), lambda b, n: (0, b))],
        out_specs=pl.BlockSpec((1, tq, ATT_Q_DIM), lambda b, n: (b, n, 0)),
        out_shape=jax.ShapeDtypeStruct((batch, S, ATT_Q_DIM), BF16),
        compiler_params=_cparams("swa", ("parallel", "parallel")),
        name="swa",
    )(sink, qt, k0, k1, vt)


def _log2_sigmoid(u):
    return jnp.minimum(u, 0.0) * LOG2E - jnp.log2(1.0 + jnp.exp2(jnp.abs(u) * -LOG2E))


def _split2(x):
    hi = x.astype(BF16)
    return hi, (x - hi.astype(F32)).astype(BF16)


GLA_WAYS = 4


def _gla_kernel(z_ref, q_ref, k_ref, v_ref, r_ref, upw_ref, bias_ref, gain_ref,
                o_ref, cf_ref, cb_ref, kef_ref, keb_ref, st_ref, s_ref, *, seq):
    L = GLA_CHUNK
    R2 = 2 * GLA_GATE_RANK
    nc = seq // L
    grp = 4 * L
    cpg = grp // L
    dk = GLA_DK
    mm = functools.partial(jnp.dot, preferred_element_type=F32)
    nt = functools.partial(lax.dot_general, dimension_numbers=NT_DIMS, preferred_element_type=F32)

    row = lax.broadcasted_iota(jnp.int32, (grp, grp), 0)
    col = lax.broadcasted_iota(jnp.int32, (grp, grp), 1)
    same = (row // L) == (col // L)
    fwd_mask = same & (col <= row)
    bwd_mask = same & (col > row)
    tri_lo = jnp.where(fwd_mask, 1.0, 0.0).astype(BF16)
    tri_up = jnp.where(same & (col >= row), 1.0, 0.0).astype(BF16)
    lane = lax.broadcasted_iota(jnp.int32, (1, LANES), 1)
    use_lo = (lane >= R2) & (lane < 2 * R2)

    def group_starts(i):
        return [pl.multiple_of((i * GLA_WAYS + w) * grp, grp) for w in range(GLA_WAYS)]

    def cum_body(i, carry):
        r0s = group_starts(i)
        zs = [_split2(z_ref[0, pl.ds(r0, grp), :]) for r0 in r0s]
        us = [mm(jnp.where(use_lo, zl, zh), upw_ref[...]) + bias_ref[...] for zh, zl in zs]
        las = [_split2(_log2_sigmoid(u) * (1.0 / GLA_GATE_NORM)) for u in us]
        cfxs = [mm(tri_lo, jnp.concatenate([lh[:, :dk], ll[:, :dk]], axis=1)) for lh, ll in las]
        cbxs = [mm(tri_up, jnp.concatenate([lh[:, dk:], ll[:, dk:]], axis=1)) for lh, ll in las]
        for r0, cfx, cbx in zip(r0s, cfxs, cbxs):
            cf = cfx[:, :dk] + cfx[:, dk:]
            cb = cbx[:, :dk] + cbx[:, dk:]
            cf_ref[pl.ds(r0, grp), :] = cf
            cb_ref[pl.ds(r0, grp), :] = cb
            k = k_ref[0, pl.ds(r0, grp), :].astype(F32)
            for c in range(cpg):
                sl = slice(c * L, (c + 1) * L)
                gf = cf[(c + 1) * L - 1:(c + 1) * L]
                gb = cb[c * L:c * L + 1]
                kef_ref[pl.ds(r0 + c * L, L), :] = (k[sl] * jnp.exp2(gf - cf[sl])).astype(BF16)
                keb_ref[pl.ds(r0 + c * L, L), :] = (k[sl] * jnp.exp2(gb - cb[sl])).astype(BF16)
        return carry

    lax.fori_loop(0, seq // (grp * GLA_WAYS), cum_body, 0, unroll=True)

    s_ref[...] = jnp.zeros_like(s_ref)
    zero_k = jnp.zeros((L, dk), BF16)

    def state_body(i, carry):
        j = nc - 1 - i
        rf = pl.multiple_of(i * L, L)
        rb = pl.multiple_of(j * L, L)
        vcat = jnp.concatenate([v_ref[0, pl.ds(rf, L), :], v_ref[0, pl.ds(rb, L), :]], axis=0)
        kblk = jnp.concatenate([jnp.concatenate([kef_ref[pl.ds(rf, L), :], zero_k], axis=1),
                                jnp.concatenate([zero_k, keb_ref[pl.ds(rb, L), :]], axis=1)], axis=0)
        kv = lax.dot_general(vcat, kblk, TN_DIMS, preferred_element_type=F32)
        decay = jnp.exp2(jnp.concatenate([cf_ref[pl.ds(rf + L - 1, 1), :], cb_ref[pl.ds(rb, 1), :]], axis=1))
        s = s_ref[...]
        st_ref[i, :, 0:dk] = s[:, :dk].astype(BF16)
        st_ref[j, :, dk:2 * dk] = s[:, dk:].astype(BF16)
        s_ref[...] = s * decay + kv
        return carry

    lax.fori_loop(0, nc, state_body, 0, unroll=True)

    def out_body(i, carry):
        r0s = group_starts(i)
        ops = []
        for r0 in r0s:
            q = q_ref[0, pl.ds(r0, grp), :].astype(F32)
            k = k_ref[0, pl.ds(r0, grp), :].astype(F32)
            cf = cf_ref[pl.ds(r0, grp), :]
            cb = cb_ref[pl.ds(r0, grp), :]
            ops.append(((q * jnp.exp2(cf)).astype(BF16), (k * jnp.exp2(-cf)).astype(BF16),
                        (q * jnp.exp2(cb)).astype(BF16), (k * jnp.exp2(-cb)).astype(BF16)))
        scores = [(nt(qf, kf), nt(qb, kb)) for qf, kf, qb, kb in ops]
        attns = [jnp.where(fwd_mask, af, jnp.where(bwd_mask, ab, 0.0)).astype(BF16) for af, ab in scores]
        outs = []
        for r0, attn, (qf, _, qb, _) in zip(r0s, attns, ops):
            c0 = r0 // L
            qcat = jnp.concatenate([qf, qb], axis=1)
            inter = jnp.concatenate([nt(qcat[c * L:(c + 1) * L], st_ref[c0 + c]) for c in range(cpg)], axis=0)
            outs.append(mm(attn, v_ref[0, pl.ds(r0, grp), :]) + inter)
        for r0, o in zip(r0s, outs):
            o_ref[0, pl.ds(r0, grp), :] = (_rms(o, gain_ref[...]) * r_ref[0, pl.ds(r0, grp), :].astype(F32)).astype(BF16)
        return carry

    lax.fori_loop(0, seq // (grp * GLA_WAYS), out_body, 0, unroll=True)


def _gla(z, gq, gk, gv, gr, upw, bias, gain):
    B, S, _ = gq.shape
    nc = S // GLA_CHUNK
    seq_blk = lambda n: pl.BlockSpec((1, S, n), lambda b, h: (b, 0, h))
    head_blk = lambda r, n: pl.BlockSpec((None, r, n), lambda b, h: (h, 0, 0))
    return pl.pallas_call(
        functools.partial(_gla_kernel, seq=S),
        grid=(B, GLA_HEADS),
        in_specs=[pl.BlockSpec((1, S, LANES), lambda b, h: (b, 0, 0)),
                  seq_blk(GLA_DK), seq_blk(GLA_DK), seq_blk(GLA_DV), seq_blk(GLA_DV),
                  head_blk(LANES, 2 * GLA_DK), head_blk(1, 2 * GLA_DK),
                  pl.BlockSpec((1, GLA_DV), lambda b, h: (0, h))],
        out_specs=seq_blk(GLA_DV),
        out_shape=jax.ShapeDtypeStruct((B, S, GLA_VAL_DIM), BF16),
        scratch_shapes=[pltpu.VMEM((S, GLA_DK), F32), pltpu.VMEM((S, GLA_DK), F32),
                        pltpu.VMEM((S, GLA_DK), BF16), pltpu.VMEM((S, GLA_DK), BF16),
                        pltpu.VMEM((nc, GLA_DV, 2 * GLA_DK), BF16),
                        pltpu.VMEM((GLA_DV, 2 * GLA_DK), F32)],
        compiler_params=_cparams("gla", ("parallel", "parallel")),
        name="gla",
    )(z, gq, gk, gv, gr, upw, bias, gain)


def _mix_out_kernel(x_ref, a_ref, g_ref, sga_ref, sgg_ref, wa_ref, wb_ref, wo_ref, gain_ref, wr_ref,
                    h_ref, xn_ref, aff_ref, afft_ref):
    sub, starts = _sub_blocks(x_ref.shape[0])
    mm = functools.partial(jnp.dot, preferred_element_type=F32)
    blk = lambda ref, r0: ref[r0:r0 + sub, :]
    pad = jnp.zeros((LANES - N_EXPERTS, sub), F32)

    y_att = [mm(blk(a_ref, r0), wa_ref[...]) for r0 in starts]
    y_gla = [mm(blk(g_ref, r0), wb_ref[...]) for r0 in starts]
    merged = [(blk(sga_ref, r0).astype(F32) * ya + blk(sgg_ref, r0).astype(F32) * yg).astype(BF16)
              for r0, ya, yg in zip(starts, y_att, y_gla)]
    hs = [blk(x_ref, r0) + mm(m, wo_ref[...]) for r0, m in zip(starts, merged)]
    xns = [_rms(h, gain_ref[...]) for h in hs]
    for r0, h, xn in zip(starts, hs, xns):
        h_ref[r0:r0 + sub, :] = h
        _store_token_tiles(xn_ref, xn, r0)
    his = [xn.astype(BF16) for xn in xns]
    parts = [lax.dot_general(wr_ref[...], jnp.concatenate([hi, (xn - hi.astype(F32)).astype(BF16)], axis=1),
                             NT_DIMS, preferred_element_type=F32) for xn, hi in zip(xns, his)]
    for r0, part in zip(starts, parts):
        logits = part[:N_EXPERTS] + part[N_EXPERTS:]
        e = jnp.exp(logits - jnp.max(logits, axis=0, keepdims=True))
        aff_t = e / jnp.sum(e, axis=0, keepdims=True)
        afft_ref[:, r0:r0 + sub] = aff_t
        aff_ref[r0:r0 + sub, :] = jnp.concatenate([aff_t, pad], axis=0).T[:, :N_EXPERTS]


def _mix_out(x2, a, g, sga, sgg, wa, wb, wo, gain, wr, tm):
    T = x2.shape[0]
    row = lambda n: pl.BlockSpec((tm, n), lambda i: (i, 0))
    return pl.pallas_call(
        _mix_out_kernel,
        grid=(T // tm,),
        in_specs=[row(D_MODEL), row(ATT_Q_DIM), row(GLA_VAL_DIM), row(D_MODEL), row(D_MODEL),
                  _full(wa.shape), _full(wb.shape), _full(wo.shape), _full(gain.shape), _full(wr.shape)],
        out_specs=[row(D_MODEL), pl.BlockSpec((tm * ROW_TILES, LANES), lambda i: (i, 0)), row(N_EXPERTS),
                   pl.BlockSpec((N_EXPERTS, tm), lambda i: (0, i))],
        out_shape=[jax.ShapeDtypeStruct((T, D_MODEL), F32), jax.ShapeDtypeStruct((T * ROW_TILES, LANES), F32),
                   jax.ShapeDtypeStruct((T, N_EXPERTS), F32), jax.ShapeDtypeStruct((N_EXPERTS, T), F32)],
        compiler_params=_cparams("mix_out", ("parallel",)),
        name="mix_out",
    )(x2, a, g, sga, sgg, wa, wb, wo, gain, wr)


ROUTE_WAYS = 4


def _route_kernel(aff_ref, idx_ref, cum_ref, *, cap, seq):
    E = N_EXPERTS
    aff = aff_ref[...]
    count = lambda mask: jnp.sum(mask.astype(jnp.int32), axis=1, keepdims=True)
    as_float = lambda pattern: lax.bitcast_convert_type(pattern, F32)

    def largest(nbits, ok):
        def pair(t, p):
            lo = (nbits - nbits % 2) - 2 * (t + 1)
            c = [p | jnp.left_shift(jnp.int32(v), lo) for v in (1, 2, 3)]
            return jnp.where(ok(c[2]), c[2], jnp.where(ok(c[1]), c[1], jnp.where(ok(c[0]), c[0], p)))

        p = jnp.zeros((E, 1), jnp.int32)
        if nbits % 2:
            top = p | jnp.int32(1 << (nbits - 1))
            p = jnp.where(ok(top), top, p)
        return lax.fori_loop(0, nbits // 2, pair, p)

    thr = as_float(largest(31, lambda c: count(aff >= as_float(c)) >= cap))
    above = aff > thr
    tie = aff == thr
    need = cap - count(above)

    pos = lax.broadcasted_iota(jnp.int32, (E, seq), 1)
    last = largest(seq.bit_length() - 1, lambda c: count(tie & (pos < c)) < need)
    sel = (above | (tie & (pos <= last))).astype(BF16)

    nt = seq // LANES
    lrow = lax.broadcasted_iota(jnp.int32, (LANES, LANES), 0)
    lcol = lax.broadcasted_iota(jnp.int32, (LANES, LANES), 1)
    tri = (lrow <= lcol).astype(BF16)
    mm = functools.partial(jnp.dot, preferred_element_type=F32)
    for t in range(nt):
        cum_ref[t * E:(t + 1) * E, :] = mm(sel[:, t * LANES:(t + 1) * LANES], tri)
    tile_of = (lax.broadcasted_iota(jnp.int32, (seq, LANES), 0) // LANES
               == lax.broadcasted_iota(jnp.int32, (seq, LANES), 1)).astype(BF16)
    per_tile = mm(sel, tile_of)
    lane = lax.broadcasted_iota(jnp.int32, (1, LANES), 1)
    far = jnp.float32(2 * seq)
    t_end = jnp.where(lane < nt, mm(per_tile.astype(BF16), tri), far)
    t_start = jnp.where(lane < nt, t_end - per_tile, far)
    pad = jnp.zeros((LANES - E, LANES), F32)
    t_start_cols = jnp.concatenate([jnp.where(lane < nt, t_start, 0.0), pad], axis=0).T

    slot = lax.broadcasted_iota(jnp.int32, (cap, LANES), 0).astype(F32)
    ones = jnp.ones((LANES, LANES), BF16)
    zrows = jnp.zeros((LANES - nt, 2 * LANES), F32)
    for e0 in range(0, E, ROUTE_WAYS):
        es = range(e0, e0 + ROUTE_WAYS)
        tiles, picks, whole = [], [], []
        for e in es:
            absc = cum_ref[pl.ds(e, nt, stride=E), :] + t_start_cols[0:nt, e:e + 1]
            hi = jnp.where(absc >= 256.0, 1.0, 0.0) + jnp.where(absc >= 512.0, 1.0, 0.0)
            lo = absc - 256.0 * hi
            tiles.append(jnp.concatenate([jnp.concatenate([lo, hi], axis=1), zrows], axis=0).astype(BF16))
            done = jnp.where(t_end[e:e + 1] <= slot, 1.0, 0.0)
            whole.append(done)
            picks.append((jnp.where(t_start[e:e + 1] <= slot, 1.0, 0.0) - done).astype(BF16))
        rows = [mm(p, w) for p, w in zip(picks, tiles)]
        votes = [(jnp.where(r[:, :LANES] + 256.0 * r[:, LANES:] <= slot, 1.0, 0.0) + float(LANES) * d).astype(BF16)
                 for r, d in zip(rows, whole)]
        for e, v in zip(es, votes):
            idx_ref[0, :, e:e + 1] = mm(v, ones)[:, e:e + 1].astype(jnp.int32)


def _route(aff_t, batch, cap):
    E, T = aff_t.shape
    B, S = batch, T // batch
    return pl.pallas_call(
        functools.partial(_route_kernel, cap=cap, seq=S),
        grid=(B,),
        in_specs=[pl.BlockSpec((E, S), lambda b: (0, b))],
        out_specs=pl.BlockSpec((1, cap, E), lambda b: (b, 0, 0)),
        out_shape=jax.ShapeDtypeStruct((B, cap, E), jnp.int32),
        scratch_shapes=[pltpu.VMEM((S // LANES * E, LANES), F32)],
        compiler_params=_cparams("route", ("parallel",)),
        name="route",
    )(aff_t)


MOE_EXPERTS_PER_STEP = 4


def _gather_kernel(idx_ref, xn_ref, aff_ref, xg_ref, wg_ref, ws_ref, *, cap):
    n = MOE_EXPERTS_PER_STEP
    lane = lax.broadcasted_iota(jnp.int32, (cap, N_EXPERTS), 1)
    for j in range(n):
        for i in range(cap):
            t = idx_ref[j, 0, i]
            src = pl.multiple_of(t * SUBLANES, SUBLANES)
            xg_ref[0, j, i * SUBLANES:(i + 1) * SUBLANES, :] = xn_ref[0, pl.ds(src, SUBLANES), :]
            ws_ref[j, i:i + 1, :] = aff_ref[0, pl.ds(t, 1), :]
        e = pl.program_id(1) * n + j
        wg_ref[0, j] = jnp.sum(jnp.where(lane == e, ws_ref[j], 0.0), axis=1, keepdims=True)


def _gather(idx, xn_tiles, aff, cap):
    B, S, E = aff.shape
    n = MOE_EXPERTS_PER_STEP
    return pl.pallas_call(
        functools.partial(_gather_kernel, cap=cap),
        grid=(B, E // n),
        in_specs=[pl.BlockSpec((n, 1, cap), lambda b, e: (b * (E // n) + e, 0, 0), memory_space=pltpu.SMEM),
                  pl.BlockSpec((1, S * SUBLANES, LANES), lambda b, e: (b, 0, 0)),
                  pl.BlockSpec((1, S, E), lambda b, e: (b, 0, 0))],
        out_specs=[pl.BlockSpec((1, n, cap * SUBLANES, LANES), lambda b, e: (b, e, 0, 0)),
                   pl.BlockSpec((1, n, cap, 1), lambda b, e: (b, e, 0, 0))],
        out_shape=[jax.ShapeDtypeStruct((B, E, cap * SUBLANES, LANES), F32),
                   jax.ShapeDtypeStruct((B, E, cap, 1), F32)],
        scratch_shapes=[pltpu.VMEM((n, cap, E), F32)],
        compiler_params=_cparams("gather", ("arbitrary", "arbitrary")),
        name="gather",
    )(idx.reshape(B * E, 1, cap), xn_tiles, aff)


def _ffn_kernel(xg_ref, wg_ref, w1_ref, w2_ref, w3_ref, y_ref, b1_ref, b2_ref, b3_ref):
    rows = wg_ref.shape[2]
    seqs = range(xg_ref.shape[0])
    mm = functools.partial(jnp.dot, preferred_element_type=F32)

    def weights(w_ref, b_ref, fresh):
        if fresh:
            b_ref[...] = w_ref[0].astype(BF16)
        return b_ref[...]

    def step(fresh):
        xgs = [_load_token_tiles(xg_ref.at[i, 0], rows).astype(BF16) for i in seqs]
        w1 = weights(w1_ref, b1_ref, fresh)
        gates = [mm(xg, w1) for xg in xgs]
        w2 = weights(w2_ref, b2_ref, fresh)
        ups = [mm(xg, w2) for xg in xgs]
        hids = [(gate * _sigmoid(gate) * up).astype(BF16) for gate, up in zip(gates, ups)]
        w3 = weights(w3_ref, b3_ref, fresh)
        ys = [mm(hid, w3) * wg_ref[i, 0] for i, hid in zip(seqs, hids)]
        for i, y in zip(seqs, ys):
            _store_token_tiles(y_ref.at[i, 0], y)

    first = pl.program_id(1) == 0
    pl.when(first)(functools.partial(step, True))
    pl.when(jnp.logical_not(first))(functools.partial(step, False))


FFN_SEQS_PER_STEP = 2


def _ffn(xg, wg, w1, w2, w3):
    B, E, C, _ = wg.shape
    _, D, F = w1.shape
    n = FFN_SEQS_PER_STEP
    return pl.pallas_call(
        _ffn_kernel,
        grid=(E, B // n),
        in_specs=[pl.BlockSpec((n, 1, C * ROW_TILES, LANES), lambda e, b: (b, e, 0, 0)),
                  pl.BlockSpec((n, 1, C, 1), lambda e, b: (b, e, 0, 0)),
                  pl.BlockSpec((1, D, F), lambda e, b: (e, 0, 0)),
                  pl.BlockSpec((1, D, F), lambda e, b: (e, 0, 0)),
                  pl.BlockSpec((1, F, D), lambda e, b: (e, 0, 0))],
        out_specs=pl.BlockSpec((n, 1, C * ROW_TILES, LANES), lambda e, b: (b, e, 0, 0)),
        out_shape=jax.ShapeDtypeStruct((B, E, C * ROW_TILES, LANES), F32),
        scratch_shapes=[pltpu.VMEM((D, F), BF16), pltpu.VMEM((D, F), BF16), pltpu.VMEM((F, D), BF16)],
        compiler_params=_cparams("ffn", ("arbitrary", "arbitrary")),
        name="ffn",
    )(xg, wg, w1, w2, w3)


SCATTER_BATCH = 16


def _scatter_kernel(idx_ref, y_ref, o_ref, *, cap):
    @pl.when(pl.program_id(1) == 0)
    def _():
        o_ref[...] = jnp.zeros_like(o_ref)

    for j in range(MOE_EXPERTS_PER_STEP):
        for i0 in range(0, cap, SCATTER_BATCH):
            slots = range(i0, i0 + SCATTER_BATCH)
            rows = [pl.ds(pl.multiple_of(idx_ref[j, 0, i] * SUBLANES, SUBLANES), SUBLANES) for i in slots]
            new = [o_ref[0, r, :] + y_ref[0, j, i * SUBLANES:(i + 1) * SUBLANES, :] for r, i in zip(rows, slots)]
            for r, v in zip(rows, new):
                o_ref[0, r, :] = v


def _scatter(idx, y_tiles, seq):
    B, E, cap = idx.shape
    n = MOE_EXPERTS_PER_STEP
    return pl.pallas_call(
        functools.partial(_scatter_kernel, cap=cap),
        grid=(B, E // n),
        in_specs=[pl.BlockSpec((n, 1, cap), lambda b, e: (b * (E // n) + e, 0, 0), memory_space=pltpu.SMEM),
                  pl.BlockSpec((1, n, cap * SUBLANES, LANES), lambda b, e: (b, e, 0, 0))],
        out_specs=pl.BlockSpec((1, seq * SUBLANES, LANES), lambda b, e: (b, 0, 0)),
        out_shape=jax.ShapeDtypeStruct((B, seq * SUBLANES, LANES), F32),
        compiler_params=_cparams("scatter", ("arbitrary", "arbitrary")),
        name="scatter",
    )(idx.reshape(B * E, 1, cap), y_tiles)


def _ple_out_kernel(h_ref, moe_ref, p_ref, gple_ref, wpg_ref, wple_ref, gfin_ref, o_ref):
    sub, starts = _sub_blocks(h_ref.shape[0])
    mm = functools.partial(jnp.dot, preferred_element_type=F32)
    hs = [h_ref[r0:r0 + sub, :] + _load_token_tiles(moe_ref, sub, r0) for r0 in starts]
    ns = [_rms(h, gple_ref[...]).astype(BF16) for h in hs]
    gates = [_sigmoid(mm(n, wpg_ref[...])) for n in ns]
    embs = [mm(p_ref[r0:r0 + sub, :].astype(BF16), wple_ref[...]) for r0 in starts]
    for r0, h, gate, emb in zip(starts, hs, gates, embs):
        o_ref[r0:r0 + sub, :] = _rms(h + gate * emb, gfin_ref[...])


def _ple_out(h, moe, p2, gple, wpg, wple, gfin, tm):
    T = h.shape[0]
    row = lambda n: pl.BlockSpec((tm, n), lambda i: (i, 0))
    return pl.pallas_call(
        _ple_out_kernel,
        grid=(T // tm,),
        in_specs=[row(D_MODEL), pl.BlockSpec((tm * ROW_TILES, LANES), lambda i: (i, 0)), row(PLE_DIM),
                  _full(gple.shape), _full(wpg.shape), _full(wple.shape), _full(gfin.shape)],
        out_specs=row(D_MODEL),
        out_shape=jax.ShapeDtypeStruct((T, D_MODEL), F32),
        compiler_params=_cparams("ple_out", ("parallel",)),
        name="ple_out",
    )(h, moe, p2, gple, wpg, wple, gfin)


IN_EDGES = {}
_o = 0
for _name, _n in (("q", ATT_Q_DIM), ("k", ATT_KV_DIM), ("v", ATT_KV_DIM), ("gqk", 2 * GLA_KEY_DIM),
                  ("gv", GLA_VAL_DIM), ("gr", GLA_VAL_DIM), ("z", 2 * GLA_GATE_RANK), ("gate", 2 * D_MODEL)):
    IN_EDGES[_name] = (_o, _o + _n)
    _o += _n
IN_DIM = _o


def _w_prep_kernel(wt_ref, wqt_ref, wkz_ref, wvt_ref, wgqk_ref, wgv_ref, wgr_ref, wgate_ref):
    piece = lambda name: wt_ref[IN_EDGES[name][0]:IN_EDGES[name][1], :]
    wqt_ref[...] = piece("q").astype(BF16)
    wvt_ref[...] = piece("v").astype(BF16)
    z = piece("z")
    wkz_ref[...] = jnp.concatenate([piece("k")] + [z] * (LANES // (2 * GLA_GATE_RANK)), axis=0).T.astype(BF16)
    wgqk_ref[...] = piece("gqk").T.astype(BF16)
    wgv_ref[...] = piece("gv").T.astype(BF16)
    wgr_ref[...] = piece("gr").T.astype(BF16)
    wgate_ref[...] = piece("gate").T.astype(BF16)


def _w_prep(wt, layer, cols):
    D = wt.shape[2]
    row = lambda n: pl.BlockSpec((cols, n), lambda i: (i, 0))
    col = lambda n: pl.BlockSpec((n, cols), lambda i: (0, i))
    widths = (ATT_KV_DIM + LANES, 2 * GLA_KEY_DIM, GLA_VAL_DIM, GLA_VAL_DIM, 2 * D_MODEL)
    out_specs = [col(ATT_Q_DIM), row(widths[0]), col(ATT_KV_DIM)] + [row(n) for n in widths[1:]]
    out_shape = ([jax.ShapeDtypeStruct((ATT_Q_DIM, D), BF16), jax.ShapeDtypeStruct((D, widths[0]), BF16),
                  jax.ShapeDtypeStruct((ATT_KV_DIM, D), BF16)]
                 + [jax.ShapeDtypeStruct((D, n), BF16) for n in widths[1:]])
    return pl.pallas_call(
        _w_prep_kernel,
        grid=(D // cols,),
        in_specs=[pl.BlockSpec((None, IN_DIM, cols), lambda i: (layer, 0, i))],
        out_specs=out_specs,
        out_shape=out_shape,
        compiler_params=_cparams("w_prep", ("parallel",)),
        name="w_prep",
    )(wt)


def kernel(x, p, positions, norm_mix, w_in, gla_gate_up_fwd, gla_gate_bias_fwd, gla_gate_up_bwd, gla_gate_bias_bwd, attn_sink, gla_norm, w_branch_attn, w_branch_gla, w_out, norm_ffn, w_router, w_exp_gate, w_exp_up, w_exp_down, norm_ple, w_ple_gate, w_ple, norm_final):
    B, S, D = x.shape
    T = B * S
    depth = w_in.shape[0]
    assert depth == 1, "the final norm is fused into the (single) layer's PLE kernel"
    cap = CAPACITY_FACTOR * S // N_EXPERTS
    R = GLA_GATE_RANK

    posr = positions.reshape(1, T)
    inv_freq = ROPE_THETA ** (-jnp.arange(0, ROPE_DIM, 2, dtype=F32) / ROPE_DIM)
    invfc = inv_freq.reshape(ROPE_DIM // 2, 1)

    h = x.reshape(T, D)
    for l in range(depth):
        w_pieces = _w_prep(jnp.swapaxes(w_in, 1, 2), l, cols=TILE_ROWS["w_prep"])
        per_head = lambda w: w.reshape(-1, GLA_HEADS, GLA_DK).swapaxes(0, 1)
        upf, upb = per_head(gla_gate_up_fwd[l]), per_head(gla_gate_up_bwd[l])
        up = jnp.concatenate([jnp.concatenate([upf, jnp.zeros_like(upf)], axis=2),
                              jnp.concatenate([jnp.zeros_like(upb), upb], axis=2)], axis=1)
        up_hi = up.astype(BF16)
        up_lo = (up - up_hi.astype(F32)).astype(BF16)
        upw = jnp.concatenate([up_hi, up_hi, up_lo, jnp.zeros_like(up_lo)], axis=1)
        gbias = jnp.concatenate([per_head(gla_gate_bias_fwd[l]), per_head(gla_gate_bias_bwd[l])], axis=2)
        wr = w_router[l]
        wr_hi = wr.astype(BF16)
        wr_lo = (wr - wr_hi.astype(F32)).astype(BF16)
        wr2 = jnp.concatenate([jnp.concatenate([wr_hi, wr_lo], axis=1),
                               jnp.concatenate([wr_hi, jnp.zeros_like(wr_lo)], axis=1)], axis=0).T

        qt, k0, k1, vt, gq, gk, gv, gr, z, sga, sgg = _in_proj(
            h, posr, invfc, norm_mix[l].reshape(1, D), *w_pieces, tm=TILE_ROWS["in_proj"])

        att = _swa(attn_sink[l], qt, k0.reshape(B, S, -1), k1.reshape(B, S, -1), vt, batch=B, tq=TILE_ROWS["swa"])
        gla = _gla(z.reshape(B, S, -1), gq.reshape(B, S, -1), gk.reshape(B, S, -1), gv.reshape(B, S, -1),
                   gr.reshape(B, S, -1), upw, gbias, gla_norm[l].reshape(1, -1))

        h1, xn, aff, aff_t = _mix_out(h, att.reshape(T, -1), gla.reshape(T, -1), sga, sgg,
                                      w_branch_attn[l].astype(BF16), w_branch_gla[l].astype(BF16),
                                      w_out[l].astype(BF16), norm_ffn[l].reshape(1, D), wr2, tm=TILE_ROWS["mix_out"])

        aff3 = aff.reshape(B, S, N_EXPERTS)
        idx = _route(aff_t, B, cap)
        idx = jnp.swapaxes(idx, 1, 2)
        xg, wg = _gather(idx, xn.reshape(B, S * SUBLANES, LANES), aff3, cap)
        y = _ffn(xg, wg, w_exp_gate[l], w_exp_up[l], w_exp_down[l])
        moe = _scatter(idx, y, S)

        h = _ple_out(h1, moe.reshape(T * ROW_TILES, LANES), p[l].reshape(T, PLE_DIM), norm_ple[l].reshape(1, D),
                     w_ple_gate[l].astype(BF16), w_ple[l].astype(BF16), norm_final.reshape(1, D),
                     tm=TILE_ROWS["ple_out"])
    return h.reshape(B, S, D)
```

```python
import functools
import math

import jax
import jax.numpy as jnp
from jax import lax
from jax.experimental import pallas as pl
from jax.experimental.pallas import tpu as pltpu

D_MODEL = 1024
ATT_HEADS = 8
ATT_KV_HEADS = 2
ATT_HEAD_DIM = 64
ATT_GROUP = ATT_HEADS // ATT_KV_HEADS
ATT_Q_DIM = ATT_HEADS * ATT_HEAD_DIM
ATT_KV_DIM = ATT_KV_HEADS * ATT_HEAD_DIM
WINDOW = 128
ROPE_DIM = ATT_HEAD_DIM // 4
ROPE_THETA = 500000.0
GLA_HEADS = 4
GLA_KEY_DIM = D_MODEL // 2
GLA_VAL_DIM = D_MODEL
GLA_DK = GLA_KEY_DIM // GLA_HEADS
GLA_DV = GLA_VAL_DIM // GLA_HEADS
GLA_GATE_RANK = 16
GLA_GATE_NORM = 16.0
GLA_CHUNK = 64
N_EXPERTS = 16
EXPERT_FF = D_MODEL
CAPACITY_FACTOR = 2
PLE_DIM = 256
EPS = 1e-6

LANES = 128
MIB = 1024 * 1024
BF16 = jnp.bfloat16
F32 = jnp.float32
LOG2E = math.log2(math.e)

NT_DIMS = (((1,), (1,)), ((), ()))
TN_DIMS = (((0,), (0,)), ((), ()))

TILE_ROWS = {"w_prep": 256, "in_proj": 512, "swa": 1024, "mix_out": 512, "ple_out": 1024}
VMEM_LIMIT_MIB = {"w_prep": 40, "in_proj": 56, "swa": 32, "gla": 48, "mix_out": 48, "route": 32, "gather": 60,
                  "ffn": 60, "scatter": 60, "ple_out": 48}


def _cparams(name, sem):
    return pltpu.CompilerParams(dimension_semantics=sem, vmem_limit_bytes=VMEM_LIMIT_MIB[name] * MIB)


def _full(shape):
    n = len(shape)
    return pl.BlockSpec(shape, lambda *_: (0,) * n)


def _rms(x, gain):
    ms = jnp.mean(x * x, axis=-1, keepdims=True)
    return x * lax.rsqrt(ms + EPS) * gain


def _sigmoid(x):
    return 0.5 * jnp.tanh(0.5 * x) + 0.5


SUBLANES = 8
ROW_TILES = D_MODEL // LANES


def _store_token_tiles(ref2d, x, first_row=0):
    rows = x.shape[0]
    for j in range(ROW_TILES):
        ref2d[pl.ds(first_row * ROW_TILES + j, rows, stride=ROW_TILES), :] = x[:, j * LANES:(j + 1) * LANES]


def _load_token_tiles(ref2d, rows, first_row=0):
    return jnp.concatenate([ref2d[pl.ds(first_row * ROW_TILES + j, rows, stride=ROW_TILES), :]
                            for j in range(ROW_TILES)], axis=1)


SUB_ROWS = 256


def _sub_blocks(tile_rows):
    return SUB_ROWS, list(range(0, tile_rows, SUB_ROWS))


def _rope_rows(t, cos_r, sin_r, heads):
    half = ROPE_DIM // 2
    rows = []
    for h in range(heads):
        r0 = h * ATT_HEAD_DIM
        t1, t2 = t[r0:r0 + half], t[r0 + half:r0 + ROPE_DIM]
        rows += [t1 * cos_r - t2 * sin_r, t2 * cos_r + t1 * sin_r, t[r0 + ROPE_DIM:r0 + ATT_HEAD_DIM]]
    return jnp.concatenate(rows, axis=0)


def _in_proj_kernel(x_ref, posr_ref, invfc_ref, gain_ref, wqt_ref, wkz_ref, wvt_ref,
                    wgqk_ref, wgv_ref, wgr_ref, wgate_ref,
                    qt_ref, k0_ref, k1_ref, vt_ref, gq_ref, gk_ref, gv_ref, gr_ref, z_ref, sga_ref, sgg_ref):
    a = _rms(x_ref[...], gain_ref[...]).astype(BF16)
    ang_t = invfc_ref[...] * posr_ref[...].astype(F32)
    cos_r, sin_r = jnp.cos(ang_t), jnp.sin(ang_t)

    qt = lax.dot_general(wqt_ref[...], a, NT_DIMS, preferred_element_type=F32)
    qt_ref[...] = (_rope_rows(qt, cos_r, sin_r, ATT_HEADS) * (ATT_HEAD_DIM ** -0.5 * LOG2E)).astype(BF16)

    kz = jnp.dot(a, wkz_ref[...], preferred_element_type=F32)
    z_ref[...] = kz[:, ATT_KV_DIM:]
    k = _rope_rows(kz[:, :ATT_KV_DIM].T, cos_r, sin_r, ATT_KV_HEADS).T.astype(BF16)
    k0_ref[...] = k[:, :ATT_HEAD_DIM]
    k1_ref[...] = k[:, ATT_HEAD_DIM:]
    vt_ref[...] = lax.dot_general(wvt_ref[...], a, NT_DIMS, preferred_element_type=F32).astype(BF16)

    gqk = jnp.dot(a, wgqk_ref[...], preferred_element_type=F32)
    gq_ref[...] = (gqk[:, :GLA_KEY_DIM] * (GLA_DK ** -0.5)).astype(BF16)
    gk_ref[...] = gqk[:, GLA_KEY_DIM:].astype(BF16)
    gv_ref[...] = jnp.dot(a, wgv_ref[...], preferred_element_type=F32).astype(BF16)
    gr = jnp.dot(a, wgr_ref[...], preferred_element_type=F32)
    gr_ref[...] = (gr * _sigmoid(gr)).astype(BF16)
    gates = jnp.dot(a, wgate_ref[...], preferred_element_type=F32)
    sga_ref[...] = _sigmoid(gates[:, :D_MODEL]).astype(BF16)
    sgg_ref[...] = _sigmoid(gates[:, D_MODEL:]).astype(BF16)


def _in_proj(x2, posr, invfc, gain, wqt, wkz, wvt, wgqk, wgv, wgr, wgate, tm):
    T = x2.shape[0]
    row = lambda n: pl.BlockSpec((tm, n), lambda i: (i, 0))
    col = lambda n: pl.BlockSpec((n, tm), lambda i: (0, i))
    row_widths = (ATT_HEAD_DIM, ATT_HEAD_DIM, None, GLA_KEY_DIM, GLA_KEY_DIM, GLA_VAL_DIM,
                  GLA_VAL_DIM, wkz.shape[1] - ATT_KV_DIM, D_MODEL, D_MODEL)
    row_dtypes = (BF16,) * 7 + (F32, BF16, BF16)
    out_specs = [col(ATT_Q_DIM)]
    out_shape = [jax.ShapeDtypeStruct((ATT_Q_DIM, T), BF16)]
    for n, dt in zip(row_widths, row_dtypes):
        if n is None:
            out_specs.append(col(ATT_KV_DIM))
            out_shape.append(jax.ShapeDtypeStruct((ATT_KV_DIM, T), BF16))
        else:
            out_specs.append(row(n))
            out_shape.append(jax.ShapeDtypeStruct((T, n), dt))
    consts = (invfc, gain, wqt, wkz, wvt, wgqk, wgv, wgr, wgate)
    return pl.pallas_call(
        _in_proj_kernel,
        grid=(T // tm,),
        in_specs=[row(D_MODEL), col(1)] + [_full(c.shape) for c in consts],
        out_specs=out_specs,
        out_shape=out_shape,
        compiler_params=_cparams("in_proj", ("parallel",)),
        name="in_proj",
    )(x2, posr, *consts)


def _swa_kernel(sink_ref, qt_ref, k0_ref, k1_ref, vt_ref, o_ref, *, tq, seq):
    blk = WINDOW
    span = 3 * blk
    hd = ATT_HEAD_DIM
    n = pl.program_id(1)
    ones = jnp.ones((16, span), BF16)
    kv_refs = (k0_ref, k1_ref)

    def window_start(sb):
        return pl.multiple_of(jnp.clip(n * tq + (sb - 1) * blk, 0, seq - span), blk)

    def scores(sb, g):
        kw = kv_refs[g][0, pl.ds(window_start(sb), span), :]
        heads = range(g * ATT_GROUP, (g + 1) * ATT_GROUP)
        qs = jnp.concatenate([qt_ref[h * hd:(h + 1) * hd, sb * blk:(sb + 1) * blk] for h in heads], axis=1)
        return jnp.dot(kw, qs, preferred_element_type=F32)

    work = [(sb, g) for sb in range(tq // blk) for g in range(ATT_KV_HEADS)]
    s_next = scores(*work[0])
    outs = []
    for step, (sb, g) in enumerate(work):
        s_all = s_next
        if step + 1 < len(work):
            s_next = scores(*work[step + 1])
        q0 = n * tq + sb * blk
        start = window_start(sb)
        kj = start + lax.broadcasted_iota(jnp.int32, (span, blk), 0)
        qi = q0 + lax.broadcasted_iota(jnp.int32, (span, blk), 1)
        valid = jnp.abs(qi - kj) <= WINDOW
        vaug = jnp.concatenate([vt_ref[g * hd:(g + 1) * hd, pl.ds(start, span)], ones], axis=0)
        sinks = [sink_ref[g * ATT_GROUP + i] * LOG2E for i in range(ATT_GROUP)]
        ss = [jnp.where(valid, s_all[:, i * blk:(i + 1) * blk], -jnp.inf) for i in range(ATT_GROUP)]
        ms = [jnp.maximum(jnp.max(s, axis=0, keepdims=True), sink) for s, sink in zip(ss, sinks)]
        es = jnp.concatenate([jnp.exp2(s - m).astype(BF16) for s, m in zip(ss, ms)], axis=1)
        r_all = jnp.dot(vaug, es, preferred_element_type=F32)
        rs = [r_all[:, i * blk:(i + 1) * blk] for i in range(ATT_GROUP)]
        outs += [r[:hd] / (r[hd:hd + 1] + jnp.exp2(sink - m)) for r, m, sink in zip(rs, ms, sinks)]
        if g == ATT_KV_HEADS - 1:
            for pr in range(ATT_HEADS // 2):
                pair = jnp.concatenate([outs[2 * pr], outs[2 * pr + 1]], axis=0)
                o_ref[0, sb * blk:(sb + 1) * blk, pr * 2 * hd:(pr + 1) * 2 * hd] = pair.T.astype(BF16)
            outs = []


def _swa(sink, qt, k0, k1, vt, batch, tq):
    S = k0.shape[1]
    nq = S // tq
    kspec = pl.BlockSpec((1, S, ATT_HEAD_DIM), lambda b, n: (b, 0, 0))
    return pl.pallas_call(
        functools.partial(_swa_kernel, tq=tq, seq=S),
        grid=(batch, nq),
        in_specs=[pl.BlockSpec(memory_space=pltpu.SMEM),
                  pl.BlockSpec((ATT_Q_DIM, tq), lambda b, n: (0, b * nq + n)),
                  kspec, kspec,
                  pl.BlockSpec((ATT_KV_DIM, S), lambda b, n: (0, b))],
        out_specs=pl.BlockSpec((1, tq, ATT_Q_DIM), lambda b, n: (b, n, 0)),
        out_shape=jax.ShapeDtypeStruct((batch, S, ATT_Q_DIM), BF16),
        compiler_params=_cparams("swa", ("parallel", "parallel")),
        name="swa",
    )(sink, qt, k0, k1, vt)


def _log2_sigmoid(u):
    return jnp.minimum(u, 0.0) * LOG2E - jnp.log2(1.0 + jnp.exp2(jnp.abs(u) * -LOG2E))


def _split2(x):
    hi = x.astype(BF16)
    return hi, (x - hi.astype(F32)).astype(BF16)


GLA_WAYS = 4


def _gla_kernel(z_ref, q_ref, k_ref, v_ref, r_ref, upw_ref, bias_ref, gain_ref,
                o_ref, cf_ref, cb_ref, kef_ref, keb_ref, st_ref, s_ref, *, seq):
    L = GLA_CHUNK
    R2 = 2 * GLA_GATE_RANK
    nc = seq // L
    grp = 4 * L
    cpg = grp // L
    dk = GLA_DK
    mm = functools.partial(jnp.dot, preferred_element_type=F32)
    nt = functools.partial(lax.dot_general, dimension_numbers=NT_DIMS, preferred_element_type=F32)

    row = lax.broadcasted_iota(jnp.int32, (grp, grp), 0)
    col = lax.broadcasted_iota(jnp.int32, (grp, grp), 1)
    same = (row // L) == (col // L)
    fwd_mask = same & (col <= row)
    bwd_mask = same & (col > row)
    tri_lo = jnp.where(fwd_mask, 1.0, 0.0).astype(BF16)
    tri_up = jnp.where(same & (col >= row), 1.0, 0.0).astype(BF16)
    lane = lax.broadcasted_iota(jnp.int32, (1, LANES), 1)
    use_lo = (lane >= R2) & (lane < 2 * R2)

    def group_starts(i):
        return [pl.multiple_of((i * GLA_WAYS + w) * grp, grp) for w in range(GLA_WAYS)]

    def cum_body(i, carry):
        r0s = group_starts(i)
        zs = [_split2(z_ref[0, pl.ds(r0, grp), :]) for r0 in r0s]
        us = [mm(jnp.where(use_lo, zl, zh), upw_ref[...]) + bias_ref[...] for zh, zl in zs]
        las = [_split2(_log2_sigmoid(u) * (1.0 / GLA_GATE_NORM)) for u in us]
        cfxs = [mm(tri_lo, jnp.concatenate([lh[:, :dk], ll[:, :dk]], axis=1)) for lh, ll in las]
        cbxs = [mm(tri_up, jnp.concatenate([lh[:, dk:], ll[:, dk:]], axis=1)) for lh, ll in las]
        for r0, cfx, cbx in zip(r0s, cfxs, cbxs):
            cf = cfx[:, :dk] + cfx[:, dk:]
            cb = cbx[:, :dk] + cbx[:, dk:]
            cf_ref[pl.ds(r0, grp), :] = cf
            cb_ref[pl.ds(r0, grp), :] = cb
            k = k_ref[0, pl.ds(r0, grp), :].astype(F32)
            for c in range(cpg):
                sl = slice(c * L, (c + 1) * L)
                gf = cf[(c + 1) * L - 1:(c + 1) * L]
                gb = cb[c * L:c * L + 1]
                kef_ref[pl.ds(r0 + c * L, L), :] = (k[sl] * jnp.exp2(gf - cf[sl])).astype(BF16)
                keb_ref[pl.ds(r0 + c * L, L), :] = (k[sl] * jnp.exp2(gb - cb[sl])).astype(BF16)
        return carry

    lax.fori_loop(0, seq // (grp * GLA_WAYS), cum_body, 0, unroll=True)

    s_ref[...] = jnp.zeros_like(s_ref)
    zero_k = jnp.zeros((L, dk), BF16)

    def state_body(i, carry):
        j = nc - 1 - i
        rf = pl.multiple_of(i * L, L)
        rb = pl.multiple_of(j * L, L)
        vcat = jnp.concatenate([v_ref[0, pl.ds(rf, L), :], v_ref[0, pl.ds(rb, L), :]], axis=0)
        kblk = jnp.concatenate([jnp.concatenate([kef_ref[pl.ds(rf, L), :], zero_k], axis=1),
                                jnp.concatenate([zero_k, keb_ref[pl.ds(rb, L), :]], axis=1)], axis=0)
        kv = lax.dot_general(vcat, kblk, TN_DIMS, preferred_element_type=F32)
        decay = jnp.exp2(jnp.concatenate([cf_ref[pl.ds(rf + L - 1, 1), :], cb_ref[pl.ds(rb, 1), :]], axis=1))
        s = s_ref[...]
        st_ref[i, :, 0:dk] = s[:, :dk].astype(BF16)
        st_ref[j, :, dk:2 * dk] = s[:, dk:].astype(BF16)
        s_ref[...] = s * decay + kv
        return carry

    lax.fori_loop(0, nc, state_body, 0, unroll=True)

    def out_body(i, carry):
        r0s = group_starts(i)
        ops = []
        for r0 in r0s:
            q = q_ref[0, pl.ds(r0, grp), :].astype(F32)
            k = k_ref[0, pl.ds(r0, grp), :].astype(F32)
            cf = cf_ref[pl.ds(r0, grp), :]
            cb = cb_ref[pl.ds(r0, grp), :]
            ops.append(((q * jnp.exp2(cf)).astype(BF16), (k * jnp.exp2(-cf)).astype(BF16),
                        (q * jnp.exp2(cb)).astype(BF16), (k * jnp.exp2(-cb)).astype(BF16)))
        scores = [(nt(qf, kf), nt(qb, kb)) for qf, kf, qb, kb in ops]
        attns = [jnp.where(fwd_mask, af, jnp.where(bwd_mask, ab, 0.0)).astype(BF16) for af, ab in scores]
        outs = []
        for r0, attn, (qf, _, qb, _) in zip(r0s, attns, ops):
            c0 = r0 // L
            qcat = jnp.concatenate([qf, qb], axis=1)
            inter = jnp.concatenate([nt(qcat[c * L:(c + 1) * L], st_ref[c0 + c]) for c in range(cpg)], axis=0)
            outs.append(mm(attn, v_ref[0, pl.ds(r0, grp), :]) + inter)
        for r0, o in zip(r0s, outs):
            o_ref[0, pl.ds(r0, grp), :] = (_rms(o, gain_ref[...]) * r_ref[0, pl.ds(r0, grp), :].astype(F32)).astype(BF16)
        return carry

    lax.fori_loop(0, seq // (grp * GLA_WAYS), out_body, 0, unroll=True)


def _gla(z, gq, gk, gv, gr, upw, bias, gain):
    B, S, _ = gq.shape
    nc = S // GLA_CHUNK
    seq_blk = lambda n: pl.BlockSpec((1, S, n), lambda b, h: (b, 0, h))
    head_blk = lambda r, n: pl.BlockSpec((None, r, n), lambda b, h: (h, 0, 0))
    return pl.pallas_call(
        functools.partial(_gla_kernel, seq=S),
        grid=(B, GLA_HEADS),
        in_specs=[pl.BlockSpec((1, S, LANES), lambda b, h: (b, 0, 0)),
                  seq_blk(GLA_DK), seq_blk(GLA_DK), seq_blk(GLA_DV), seq_blk(GLA_DV),
                  head_blk(LANES, 2 * GLA_DK), head_blk(1, 2 * GLA_DK),
                  pl.BlockSpec((1, GLA_DV), lambda b, h: (0, h))],
        out_specs=seq_blk(GLA_DV),
        out_shape=jax.ShapeDtypeStruct((B, S, GLA_VAL_DIM), BF16),
        scratch_shapes=[pltpu.VMEM((S, GLA_DK), F32), pltpu.VMEM((S, GLA_DK), F32),
                        pltpu.VMEM((S, GLA_DK), BF16), pltpu.VMEM((S, GLA_DK), BF16),
                        pltpu.VMEM((nc, GLA_DV, 2 * GLA_DK), BF16),
                        pltpu.VMEM((GLA_DV, 2 * GLA_DK), F32)],
        compiler_params=_cparams("gla", ("parallel", "parallel")),
        name="gla",
    )(z, gq, gk, gv, gr, upw, bias, gain)


def _mix_out_kernel(x_ref, a_ref, g_ref, sga_ref, sgg_ref, wa_ref, wb_ref, wo_ref, gain_ref, wr_ref,
                    h_ref, xn_ref, aff_ref, afft_ref):
    sub, starts = _sub_blocks(x_ref.shape[0])
    mm = functools.partial(jnp.dot, preferred_element_type=F32)
    blk = lambda ref, r0: ref[r0:r0 + sub, :]
    pad = jnp.zeros((LANES - N_EXPERTS, sub), F32)

    y_att = [mm(blk(a_ref, r0), wa_ref[...]) for r0 in starts]
    y_gla = [mm(blk(g_ref, r0), wb_ref[...]) for r0 in starts]
    merged = [(blk(sga_ref, r0).astype(F32) * ya + blk(sgg_ref, r0).astype(F32) * yg).astype(BF16)
              for r0, ya, yg in zip(starts, y_att, y_gla)]
    hs = [blk(x_ref, r0) + mm(m, wo_ref[...]) for r0, m in zip(starts, merged)]
    xns = [_rms(h, gain_ref[...]) for h in hs]
    for r0, h, xn in zip(starts, hs, xns):
        h_ref[r0:r0 + sub, :] = h
        _store_token_tiles(xn_ref, xn, r0)
    his = [xn.astype(BF16) for xn in xns]
    parts = [lax.dot_general(wr_ref[...], jnp.concatenate([hi, (xn - hi.astype(F32)).astype(BF16)], axis=1),
                             NT_DIMS, preferred_element_type=F32) for xn, hi in zip(xns, his)]
    for r0, part in zip(starts, parts):
        logits = part[:N_EXPERTS] + part[N_EXPERTS:]
        e = jnp.exp(logits - jnp.max(logits, axis=0, keepdims=True))
        aff_t = e / jnp.sum(e, axis=0, keepdims=True)
        afft_ref[:, r0:r0 + sub] = aff_t
        aff_ref[r0:r0 + sub, :] = jnp.concatenate([aff_t, pad], axis=0).T[:, :N_EXPERTS]


def _mix_out(x2, a, g, sga, sgg, wa, wb, wo, gain, wr, tm):
    T = x2.shape[0]
    row = lambda n: pl.BlockSpec((tm, n), lambda i: (i, 0))
    return pl.pallas_call(
        _mix_out_kernel,
        grid=(T // tm,),
        in_specs=[row(D_MODEL), row(ATT_Q_DIM), row(GLA_VAL_DIM), row(D_MODEL), row(D_MODEL),
                  _full(wa.shape), _full(wb.shape), _full(wo.shape), _full(gain.shape), _full(wr.shape)],
        out_specs=[row(D_MODEL), pl.BlockSpec((tm * ROW_TILES, LANES), lambda i: (i, 0)), row(N_EXPERTS),
                   pl.BlockSpec((N_EXPERTS, tm), lambda i: (0, i))],
        out_shape=[jax.ShapeDtypeStruct((T, D_MODEL), F32), jax.ShapeDtypeStruct((T * ROW_TILES, LANES), F32),
                   jax.ShapeDtypeStruct((T, N_EXPERTS), F32), jax.ShapeDtypeStruct((N_EXPERTS, T), F32)],
        compiler_params=_cparams("mix_out", ("parallel",)),
        name="mix_out",
    )(x2, a, g, sga, sgg, wa, wb, wo, gain, wr)


ROUTE_WAYS = 4


def _route_kernel(aff_ref, idx_ref, cum_ref, *, cap, seq):
    E = N_EXPERTS
    aff = aff_ref[...]
    count = lambda mask: jnp.sum(mask.astype(jnp.int32), axis=1, keepdims=True)
    as_float = lambda pattern: lax.bitcast_convert_type(pattern, F32)

    def largest(nbits, ok):
        def pair(t, p):
            lo = (nbits - nbits % 2) - 2 * (t + 1)
            c = [p | jnp.left_shift(jnp.int32(v), lo) for v in (1, 2, 3)]
            return jnp.where(ok(c[2]), c[2], jnp.where(ok(c[1]), c[1], jnp.where(ok(c[0]), c[0], p)))

        p = jnp.zeros((E, 1), jnp.int32)
        if nbits % 2:
            top = p | jnp.int32(1 << (nbits - 1))
            p = jnp.where(ok(top), top, p)
        return lax.fori_loop(0, nbits // 2, pair, p)

    thr = as_float(largest(31, lambda c: count(aff >= as_float(c)) >= cap))
    above = aff > thr
    tie = aff == thr
    need = cap - count(above)

    pos = lax.broadcasted_iota(jnp.int32, (E, seq), 1)
    last = largest(seq.bit_length() - 1, lambda c: count(tie & (pos < c)) < need)
    sel = (above | (tie & (pos <= last))).astype(BF16)

    nt = seq // LANES
    lrow = lax.broadcasted_iota(jnp.int32, (LANES, LANES), 0)
    lcol = lax.broadcasted_iota(jnp.int32, (LANES, LANES), 1)
    tri = (lrow <= lcol).astype(BF16)
    mm = functools.partial(jnp.dot, preferred_element_type=F32)
    for t in range(nt):
        cum_ref[t * E:(t + 1) * E, :] = mm(sel[:, t * LANES:(t + 1) * LANES], tri)
    tile_of = (lax.broadcasted_iota(jnp.int32, (seq, LANES), 0) // LANES
               == lax.broadcasted_iota(jnp.int32, (seq, LANES), 1)).astype(BF16)
    per_tile = mm(sel, tile_of)
    lane = lax.broadcasted_iota(jnp.int32, (1, LANES), 1)
    far = jnp.float32(2 * seq)
    t_end = jnp.where(lane < nt, mm(per_tile.astype(BF16), tri), far)
    t_start = jnp.where(lane < nt, t_end - per_tile, far)
    pad = jnp.zeros((LANES - E, LANES), F32)
    t_start_cols = jnp.concatenate([jnp.where(lane < nt, t_start, 0.0), pad], axis=0).T

    slot = lax.broadcasted_iota(jnp.int32, (cap, LANES), 0).astype(F32)
    ones = jnp.ones((LANES, LANES), BF16)
    zrows = jnp.zeros((LANES - nt, 2 * LANES), F32)
    for e0 in range(0, E, ROUTE_WAYS):
        es = range(e0, e0 + ROUTE_WAYS)
        tiles, picks, whole = [], [], []
        for e in es:
            absc = cum_ref[pl.ds(e, nt, stride=E), :] + t_start_cols[0:nt, e:e + 1]
            hi = jnp.where(absc >= 256.0, 1.0, 0.0) + jnp.where(absc >= 512.0, 1.0, 0.0)
            lo = absc - 256.0 * hi
            tiles.append(jnp.concatenate([jnp.concatenate([lo, hi], axis=1), zrows], axis=0).astype(BF16))
            done = jnp.where(t_end[e:e + 1] <= slot, 1.0, 0.0)
            whole.append(done)
            picks.append((jnp.where(t_start[e:e + 1] <= slot, 1.0, 0.0) - done).astype(BF16))
        rows = [mm(p, w) for p, w in zip(picks, tiles)]
        votes = [(jnp.where(r[:, :LANES] + 256.0 * r[:, LANES:] <= slot, 1.0, 0.0) + float(LANES) * d).astype(BF16)
                 for r, d in zip(rows, whole)]
        for e, v in zip(es, votes):
            idx_ref[0, :, e:e + 1] = mm(v, ones)[:, e:e + 1].astype(jnp.int32)


def _route(aff_t, batch, cap):
    E, T = aff_t.shape
    B, S = batch, T // batch
    return pl.pallas_call(
        functools.partial(_route_kernel, cap=cap, seq=S),
        grid=(B,),
        in_specs=[pl.BlockSpec((E, S), lambda b: (0, b))],
        out_specs=pl.BlockSpec((1, cap, E), lambda b: (b, 0, 0)),
        out_shape=jax.ShapeDtypeStruct((B, cap, E), jnp.int32),
        scratch_shapes=[pltpu.VMEM((S // LANES * E, LANES), F32)],
        compiler_params=_cparams("route", ("parallel",)),
        name="route",
    )(aff_t)


MOE_EXPERTS_PER_STEP = 4


def _gather_kernel(idx_ref, xn_ref, aff_ref, xg_ref, wg_ref, xs_ref, ws_ref, *, cap):
    n = MOE_EXPERTS_PER_STEP
    lane = lax.broadcasted_iota(jnp.int32, (cap, N_EXPERTS), 1)
    for j in range(n):
        for i in range(cap):
            t = idx_ref[j, 0, i]
            src = pl.multiple_of(t * SUBLANES, SUBLANES)
            xs_ref[j, i * SUBLANES:(i + 1) * SUBLANES, :] = xn_ref[0, pl.ds(src, SUBLANES), :]
            ws_ref[j, i:i + 1, :] = aff_ref[0, pl.ds(t, 1), :]
        xg_ref[0, j] = _load_token_tiles(xs_ref.at[j], cap).astype(BF16)
        e = pl.program_id(1) * n + j
        wg_ref[0, j] = jnp.sum(jnp.where(lane == e, ws_ref[j], 0.0), axis=1, keepdims=True)


def _gather(idx, xn_tiles, aff, cap):
    B, S, E = aff.shape
    n = MOE_EXPERTS_PER_STEP
    return pl.pallas_call(
        functools.partial(_gather_kernel, cap=cap),
        grid=(B, E // n),
        in_specs=[pl.BlockSpec((n, 1, cap), lambda b, e: (b * (E // n) + e, 0, 0), memory_space=pltpu.SMEM),
                  pl.BlockSpec((1, S * SUBLANES, LANES), lambda b, e: (b, 0, 0)),
                  pl.BlockSpec((1, S, E), lambda b, e: (b, 0, 0))],
        out_specs=[pl.BlockSpec((1, n, cap, D_MODEL), lambda b, e: (b, e, 0, 0)),
                   pl.BlockSpec((1, n, cap, 1), lambda b, e: (b, e, 0, 0))],
        out_shape=[jax.ShapeDtypeStruct((B, E, cap, D_MODEL), BF16),
                   jax.ShapeDtypeStruct((B, E, cap, 1), F32)],
        scratch_shapes=[pltpu.VMEM((n, cap * SUBLANES, LANES), F32), pltpu.VMEM((n, cap, E), F32)],
        compiler_params=_cparams("gather", ("arbitrary", "arbitrary")),
        name="gather",
    )(idx.reshape(B * E, 1, cap), xn_tiles, aff)


def _ffn_kernel(xg_ref, wg_ref, w1_ref, w2_ref, w3_ref, y_ref, b1_ref, b2_ref, b3_ref):
    seqs = range(xg_ref.shape[0])
    mm = functools.partial(jnp.dot, preferred_element_type=F32)

    def weights(w_ref, b_ref, fresh):
        if fresh:
            b_ref[...] = w_ref[0].astype(BF16)
        return b_ref[...]

    def step(fresh):
        xgs = [xg_ref[i, 0] for i in seqs]
        w1 = weights(w1_ref, b1_ref, fresh)
        gates = [mm(xg, w1) for xg in xgs]
        w2 = weights(w2_ref, b2_ref, fresh)
        ups = [mm(xg, w2) for xg in xgs]
        hids = [(gate * _sigmoid(gate) * up).astype(BF16) for gate, up in zip(gates, ups)]
        w3 = weights(w3_ref, b3_ref, fresh)
        ys = [mm(hid, w3) * wg_ref[i, 0] for i, hid in zip(seqs, hids)]
        for i, y in zip(seqs, ys):
            _store_token_tiles(y_ref.at[i, 0], y)

    first = pl.program_id(1) == 0
    pl.when(first)(functools.partial(step, True))
    pl.when(jnp.logical_not(first))(functools.partial(step, False))


FFN_SEQS_PER_STEP = 2


def _ffn(xg, wg, w1, w2, w3):
    B, E, C, _ = wg.shape
    _, D, F = w1.shape
    n = FFN_SEQS_PER_STEP
    return pl.pallas_call(
        _ffn_kernel,
        grid=(E, B // n),
        in_specs=[pl.BlockSpec((n, 1, C, D), lambda e, b: (b, e, 0, 0)),
                  pl.BlockSpec((n, 1, C, 1), lambda e, b: (b, e, 0, 0)),
                  pl.BlockSpec((1, D, F), lambda e, b: (e, 0, 0)),
                  pl.BlockSpec((1, D, F), lambda e, b: (e, 0, 0)),
                  pl.BlockSpec((1, F, D), lambda e, b: (e, 0, 0))],
        out_specs=pl.BlockSpec((n, 1, C * ROW_TILES, LANES), lambda e, b: (b, e, 0, 0)),
        out_shape=jax.ShapeDtypeStruct((B, E, C * ROW_TILES, LANES), F32),
        scratch_shapes=[pltpu.VMEM((D, F), BF16), pltpu.VMEM((D, F), BF16), pltpu.VMEM((F, D), BF16)],
        compiler_params=_cparams("ffn", ("arbitrary", "arbitrary")),
        name="ffn",
    )(xg, wg, w1, w2, w3)


SCATTER_BATCH = 16


def _scatter_kernel(idx_ref, y_ref, o_ref, *, cap):
    @pl.when(pl.program_id(1) == 0)
    def _():
        o_ref[...] = jnp.zeros_like(o_ref)

    for j in range(MOE_EXPERTS_PER_STEP):
        for i0 in range(0, cap, SCATTER_BATCH):
            slots = range(i0, i0 + SCATTER_BATCH)
            rows = [pl.ds(pl.multiple_of(idx_ref[j, 0, i] * SUBLANES, SUBLANES), SUBLANES) for i in slots]
            new = [o_ref[0, r, :] + y_ref[0, j, i * SUBLANES:(i + 1) * SUBLANES, :] for r, i in zip(rows, slots)]
            for r, v in zip(rows, new):
                o_ref[0, r, :] = v


def _scatter(idx, y_tiles, seq):
    B, E, cap = idx.shape
    n = MOE_EXPERTS_PER_STEP
    return pl.pallas_call(
        functools.partial(_scatter_kernel, cap=cap),
        grid=(B, E // n),
        in_specs=[pl.BlockSpec((n, 1, cap), lambda b, e: (b * (E // n) + e, 0, 0), memory_space=pltpu.SMEM),
                  pl.BlockSpec((1, n, cap * SUBLANES, LANES), lambda b, e: (b, e, 0, 0))],
        out_specs=pl.BlockSpec((1, seq * SUBLANES, LANES), lambda b, e: (b, 0, 0)),
        out_shape=jax.ShapeDtypeStruct((B, seq * SUBLANES, LANES), F32),
        compiler_params=_cparams("scatter", ("arbitrary", "arbitrary")),
        name="scatter",
    )(idx.reshape(B * E, 1, cap), y_tiles)


def _ple_out_kernel(h_ref, moe_ref, p_ref, gple_ref, wpg_ref, wple_ref, gfin_ref, o_ref):
    sub, starts = _sub_blocks(h_ref.shape[0])
    mm = functools.partial(jnp.dot, preferred_element_type=F32)
    hs = [h_ref[r0:r0 + sub, :] + _load_token_tiles(moe_ref, sub, r0) for r0 in starts]
    ns = [_rms(h, gple_ref[...]).astype(BF16) for h in hs]
    gates = [_sigmoid(mm(n, wpg_ref[...])) for n in ns]
    embs = [mm(p_ref[r0:r0 + sub, :].astype(BF16), wple_ref[...]) for r0 in starts]
    for r0, h, gate, emb in zip(starts, hs, gates, embs):
        o_ref[r0:r0 + sub, :] = _rms(h + gate * emb, gfin_ref[...])


def _ple_out(h, moe, p2, gple, wpg, wple, gfin, tm):
    T = h.shape[0]
    row = lambda n: pl.BlockSpec((tm, n), lambda i: (i, 0))
    return pl.pallas_call(
        _ple_out_kernel,
        grid=(T // tm,),
        in_specs=[row(D_MODEL), pl.BlockSpec((tm * ROW_TILES, LANES), lambda i: (i, 0)), row(PLE_DIM),
                  _full(gple.shape), _full(wpg.shape), _full(wple.shape), _full(gfin.shape)],
        out_specs=row(D_MODEL),
        out_shape=jax.ShapeDtypeStruct((T, D_MODEL), F32),
        compiler_params=_cparams("ple_out", ("parallel",)),
        name="ple_out",
    )(h, moe, p2, gple, wpg, wple, gfin)


IN_EDGES = {}
_o = 0
for _name, _n in (("q", ATT_Q_DIM), ("k", ATT_KV_DIM), ("v", ATT_KV_DIM), ("gqk", 2 * GLA_KEY_DIM),
                  ("gv", GLA_VAL_DIM), ("gr", GLA_VAL_DIM), ("z", 2 * GLA_GATE_RANK), ("gate", 2 * D_MODEL)):
    IN_EDGES[_name] = (_o, _o + _n)
    _o += _n
IN_DIM = _o


def _w_prep_kernel(wt_ref, wqt_ref, wkz_ref, wvt_ref, wgqk_ref, wgv_ref, wgr_ref, wgate_ref):
    piece = lambda name: wt_ref[IN_EDGES[name][0]:IN_EDGES[name][1], :]
    wqt_ref[...] = piece("q").astype(BF16)
    wvt_ref[...] = piece("v").astype(BF16)
    z = piece("z")
    wkz_ref[...] = jnp.concatenate([piece("k")] + [z] * (LANES // (2 * GLA_GATE_RANK)), axis=0).T.astype(BF16)
    wgqk_ref[...] = piece("gqk").T.astype(BF16)
    wgv_ref[...] = piece("gv").T.astype(BF16)
    wgr_ref[...] = piece("gr").T.astype(BF16)
    wgate_ref[...] = piece("gate").T.astype(BF16)


def _w_prep(wt, layer, cols):
    D = wt.shape[2]
    row = lambda n: pl.BlockSpec((cols, n), lambda i: (i, 0))
    col = lambda n: pl.BlockSpec((n, cols), lambda i: (0, i))
    widths = (ATT_KV_DIM + LANES, 2 * GLA_KEY_DIM, GLA_VAL_DIM, GLA_VAL_DIM, 2 * D_MODEL)
    out_specs = [col(ATT_Q_DIM), row(widths[0]), col(ATT_KV_DIM)] + [row(n) for n in widths[1:]]
    out_shape = ([jax.ShapeDtypeStruct((ATT_Q_DIM, D), BF16), jax.ShapeDtypeStruct((D, widths[0]), BF16),
                  jax.ShapeDtypeStruct((ATT_KV_DIM, D), BF16)]
                 + [jax.ShapeDtypeStruct((D, n), BF16) for n in widths[1:]])
    return pl.pallas_call(
        _w_prep_kernel,
        grid=(D // cols,),
        in_specs=[pl.BlockSpec((None, IN_DIM, cols), lambda i: (layer, 0, i))],
        out_specs=out_specs,
        out_shape=out_shape,
        compiler_params=_cparams("w_prep", ("parallel",)),
        name="w_prep",
    )(wt)


def kernel(x, p, positions, norm_mix, w_in, gla_gate_up_fwd, gla_gate_bias_fwd, gla_gate_up_bwd, gla_gate_bias_bwd, attn_sink, gla_norm, w_branch_attn, w_branch_gla, w_out, norm_ffn, w_router, w_exp_gate, w_exp_up, w_exp_down, norm_ple, w_ple_gate, w_ple, norm_final):
    B, S, D = x.shape
    T = B * S
    depth = w_in.shape[0]
    assert depth == 1, "the final norm is fused into the (single) layer's PLE kernel"
    cap = CAPACITY_FACTOR * S // N_EXPERTS
    R = GLA_GATE_RANK

    posr = positions.reshape(1, T)
    inv_freq = ROPE_THETA ** (-jnp.arange(0, ROPE_DIM, 2, dtype=F32) / ROPE_DIM)
    invfc = inv_freq.reshape(ROPE_DIM // 2, 1)

    h = x.reshape(T, D)
    for l in range(depth):
        w_pieces = _w_prep(jnp.swapaxes(w_in, 1, 2), l, cols=TILE_ROWS["w_prep"])
        per_head = lambda w: w.reshape(-1, GLA_HEADS, GLA_DK).swapaxes(0, 1)
        upf, upb = per_head(gla_gate_up_fwd[l]), per_head(gla_gate_up_bwd[l])
        up = jnp.concatenate([jnp.concatenate([upf, jnp.zeros_like(upf)], axis=2),
                              jnp.concatenate([jnp.zeros_like(upb), upb], axis=2)], axis=1)
        up_hi = up.astype(BF16)
        up_lo = (up - up_hi.astype(F32)).astype(BF16)
        upw = jnp.concatenate([up_hi, up_hi, up_lo, jnp.zeros_like(up_lo)], axis=1)
        gbias = jnp.concatenate([per_head(gla_gate_bias_fwd[l]), per_head(gla_gate_bias_bwd[l])], axis=2)
        wr = w_router[l]
        wr_hi = wr.astype(BF16)
        wr_lo = (wr - wr_hi.astype(F32)).astype(BF16)
        wr2 = jnp.concatenate([jnp.concatenate([wr_hi, wr_lo], axis=1),
                               jnp.concatenate([wr_hi, jnp.zeros_like(wr_lo)], axis=1)], axis=0).T

        qt, k0, k1, vt, gq, gk, gv, gr, z, sga, sgg = _in_proj(
            h, posr, invfc, norm_mix[l].reshape(1, D), *w_pieces, tm=TILE_ROWS["in_proj"])

        att = _swa(attn_sink[l], qt, k0.reshape(B, S, -1), k1.reshape(B, S, -1), vt, batch=B, tq=TILE_ROWS["swa"])
        gla = _gla(z.reshape(B, S, -1), gq.reshape(B, S, -1), gk.reshape(B, S, -1), gv.reshape(B, S, -1),
                   gr.reshape(B, S, -1), upw, gbias, gla_norm[l].reshape(1, -1))

        h1, xn, aff, aff_t = _mix_out(h, att.reshape(T, -1), gla.reshape(T, -1), sga, sgg,
                                      w_branch_attn[l].astype(BF16), w_branch_gla[l].astype(BF16),
                                      w_out[l].astype(BF16), norm_ffn[l].reshape(1, D), wr2, tm=TILE_ROWS["mix_out"])

        aff3 = aff.reshape(B, S, N_EXPERTS)
        idx = _route(aff_t, B, cap)
        idx = jnp.swapaxes(idx, 1, 2)
        xg, wg = _gather(idx, xn.reshape(B, S * SUBLANES, LANES), aff3, cap)
        y = _ffn(xg, wg, w_exp_gate[l], w_exp_up[l], w_exp_down[l])
        moe = _scatter(idx, y, S)

        h = _ple_out(h1, moe.reshape(T * ROW_TILES, LANES), p[l].reshape(T, PLE_DIM), norm_ple[l].reshape(1, D),
                     w_ple_gate[l].astype(BF16), w_ple[l].astype(BF16), norm_final.reshape(1, D),
                     tm=TILE_ROWS["ple_out"])
    return h.reshape(B, S, D)
```

```python
import functools
import math

import jax
import jax.numpy as jnp
from jax import lax
from jax.experimental import pallas as pl
from jax.experimental.pallas import tpu as pltpu

D_MODEL = 1024
ATT_HEADS = 8
ATT_KV_HEADS = 2
ATT_HEAD_DIM = 64
ATT_GROUP = ATT_HEADS // ATT_KV_HEADS
ATT_Q_DIM = ATT_HEADS * ATT_HEAD_DIM
ATT_KV_DIM = ATT_KV_HEADS * ATT_HEAD_DIM
WINDOW = 128
ROPE_DIM = ATT_HEAD_DIM // 4
ROPE_THETA = 500000.0
GLA_HEADS = 4
GLA_KEY_DIM = D_MODEL // 2
GLA_VAL_DIM = D_MODEL
GLA_DK = GLA_KEY_DIM // GLA_HEADS
GLA_DV = GLA_VAL_DIM // GLA_HEADS
GLA_GATE_RANK = 16
GLA_GATE_NORM = 16.0
GLA_CHUNK = 64
N_EXPERTS = 16
EXPERT_FF = D_MODEL
CAPACITY_FACTOR = 2
PLE_DIM = 256
EPS = 1e-6

LANES = 128
MIB = 1024 * 1024
BF16 = jnp.bfloat16
F32 = jnp.float32
LOG2E = math.log2(math.e)

NT_DIMS = (((1,), (1,)), ((), ()))
TN_DIMS = (((0,), (0,)), ((), ()))

TILE_ROWS = {"w_prep": 256, "in_proj": 512, "swa": 1024, "mix_out": 512, "ple_out": 1024}
VMEM_LIMIT_MIB = {"w_prep": 40, "in_proj": 56, "swa": 32, "gla": 48, "mix_out": 48, "route": 32, "gather": 60,
                  "ffn": 60, "scatter": 60, "ple_out": 48}


def _cparams(name, sem):
    return pltpu.CompilerParams(dimension_semantics=sem, vmem_limit_bytes=VMEM_LIMIT_MIB[name] * MIB)


def _full(shape):
    n = len(shape)
    return pl.BlockSpec(shape, lambda *_: (0,) * n)


def _rms(x, gain):
    ms = jnp.mean(x * x, axis=-1, keepdims=True)
    return x * lax.rsqrt(ms + EPS) * gain


def _sigmoid(x):
    return 0.5 * jnp.tanh(0.5 * x) + 0.5


SUBLANES = 8
ROW_TILES = D_MODEL // LANES


def _store_token_tiles(ref2d, x, first_row=0):
    rows = x.shape[0]
    for j in range(ROW_TILES):
        ref2d[pl.ds(first_row * ROW_TILES + j, rows, stride=ROW_TILES), :] = x[:, j * LANES:(j + 1) * LANES]


def _load_token_tiles(ref2d, rows, first_row=0):
    return jnp.concatenate([ref2d[pl.ds(first_row * ROW_TILES + j, rows, stride=ROW_TILES), :]
                            for j in range(ROW_TILES)], axis=1)


SUB_ROWS = 256


def _sub_blocks(tile_rows):
    return SUB_ROWS, list(range(0, tile_rows, SUB_ROWS))


def _rope_rows(t, cos_r, sin_r, heads):
    half = ROPE_DIM // 2
    rows = []
    for h in range(heads):
        r0 = h * ATT_HEAD_DIM
        t1, t2 = t[r0:r0 + half], t[r0 + half:r0 + ROPE_DIM]
        rows += [t1 * cos_r - t2 * sin_r, t2 * cos_r + t1 * sin_r, t[r0 + ROPE_DIM:r0 + ATT_HEAD_DIM]]
    return jnp.concatenate(rows, axis=0)


def _in_proj_kernel(x_ref, posr_ref, invfc_ref, gain_ref, wqt_ref, wkz_ref, wvt_ref,
                    wgqk_ref, wgv_ref, wgr_ref, wgate_ref,
                    qt_ref, k0_ref, k1_ref, vt_ref, gq_ref, gk_ref, gv_ref, gr_ref, z_ref, sga_ref, sgg_ref):
    a = _rms(x_ref[...], gain_ref[...]).astype(BF16)
    ang_t = invfc_ref[...] * posr_ref[...].astype(F32)
    cos_r, sin_r = jnp.cos(ang_t), jnp.sin(ang_t)

    qt = lax.dot_general(wqt_ref[...], a, NT_DIMS, preferred_element_type=F32)
    qt_ref[...] = (_rope_rows(qt, cos_r, sin_r, ATT_HEADS) * (ATT_HEAD_DIM ** -0.5 * LOG2E)).astype(BF16)

    kz = jnp.dot(a, wkz_ref[...], preferred_element_type=F32)
    z_ref[...] = kz[:, ATT_KV_DIM:]
    k = _rope_rows(kz[:, :ATT_KV_DIM].T, cos_r, sin_r, ATT_KV_HEADS).T.astype(BF16)
    k0_ref[...] = k[:, :ATT_HEAD_DIM]
    k1_ref[...] = k[:, ATT_HEAD_DIM:]
    vt_ref[...] = lax.dot_general(wvt_ref[...], a, NT_DIMS, preferred_element_type=F32).astype(BF16)

    gqk = jnp.dot(a, wgqk_ref[...], preferred_element_type=F32)
    gq_ref[...] = (gqk[:, :GLA_KEY_DIM] * (GLA_DK ** -0.5)).astype(BF16)
    gk_ref[...] = gqk[:, GLA_KEY_DIM:].astype(BF16)
    gv_ref[...] = jnp.dot(a, wgv_ref[...], preferred_element_type=F32).astype(BF16)
    gr = jnp.dot(a, wgr_ref[...], preferred_element_type=F32)
    gr_ref[...] = (gr * _sigmoid(gr)).astype(BF16)
    gates = jnp.dot(a, wgate_ref[...], preferred_element_type=F32)
    sga_ref[...] = _sigmoid(gates[:, :D_MODEL]).astype(BF16)
    sgg_ref[...] = _sigmoid(gates[:, D_MODEL:]).astype(BF16)


def _in_proj(x2, posr, invfc, gain, wqt, wkz, wvt, wgqk, wgv, wgr, wgate, tm):
    T = x2.shape[0]
    row = lambda n: pl.BlockSpec((tm, n), lambda i: (i, 0))
    col = lambda n: pl.BlockSpec((n, tm), lambda i: (0, i))
    row_widths = (ATT_HEAD_DIM, ATT_HEAD_DIM, None, GLA_KEY_DIM, GLA_KEY_DIM, GLA_VAL_DIM,
                  GLA_VAL_DIM, wkz.shape[1] - ATT_KV_DIM, D_MODEL, D_MODEL)
    row_dtypes = (BF16,) * 7 + (F32, BF16, BF16)
    out_specs = [col(ATT_Q_DIM)]
    out_shape = [jax.ShapeDtypeStruct((ATT_Q_DIM, T), BF16)]
    for n, dt in zip(row_widths, row_dtypes):
        if n is None:
            out_specs.append(col(ATT_KV_DIM))
            out_shape.append(jax.ShapeDtypeStruct((ATT_KV_DIM, T), BF16))
        else:
            out_specs.append(row(n))
            out_shape.append(jax.ShapeDtypeStruct((T, n), dt))
    consts = (invfc, gain, wqt, wkz, wvt, wgqk, wgv, wgr, wgate)
    return pl.pallas_call(
        _in_proj_kernel,
        grid=(T // tm,),
        in_specs=[row(D_MODEL), col(1)] + [_full(c.shape) for c in consts],
        out_specs=out_specs,
        out_shape=out_shape,
        compiler_params=_cparams("in_proj", ("parallel",)),
        name="in_proj",
    )(x2, posr, *consts)


def _swa_kernel(sink_ref, qt_ref, k0_ref, k1_ref, vt_ref, o_ref, *, tq, seq):
    blk = WINDOW
    span = 3 * blk
    hd = ATT_HEAD_DIM
    n = pl.program_id(1)
    ones = jnp.ones((16, span), BF16)
    kv_refs = (k0_ref, k1_ref)

    def window_start(sb):
        return pl.multiple_of(jnp.clip(n * tq + (sb - 1) * blk, 0, seq - span), blk)

    def scores(sb, g):
        kw = kv_refs[g][0, pl.ds(window_start(sb), span), :]
        heads = range(g * ATT_GROUP, (g + 1) * ATT_GROUP)
        qs = jnp.concatenate([qt_ref[h * hd:(h + 1) * hd, sb * blk:(sb + 1) * blk] for h in heads], axis=1)
        return jnp.dot(kw, qs, preferred_element_type=F32)

    work = [(sb, g) for sb in range(tq // blk) for g in range(ATT_KV_HEADS)]
    s_next = scores(*work[0])
    outs = []
    for step, (sb, g) in enumerate(work):
        s_all = s_next
        if step + 1 < len(work):
            s_next = scores(*work[step + 1])
        q0 = n * tq + sb * blk
        start = window_start(sb)
        kj = start + lax.broadcasted_iota(jnp.int32, (span, blk), 0)
        qi = q0 + lax.broadcasted_iota(jnp.int32, (span, blk), 1)
        valid = jnp.abs(qi - kj) <= WINDOW
        vaug = jnp.concatenate([vt_ref[g * hd:(g + 1) * hd, pl.ds(start, span)], ones], axis=0)
        sinks = [sink_ref[g * ATT_GROUP + i] * LOG2E for i in range(ATT_GROUP)]
        ss = [jnp.where(valid, s_all[:, i * blk:(i + 1) * blk], -jnp.inf) for i in range(ATT_GROUP)]
        ms = [jnp.maximum(jnp.max(s, axis=0, keepdims=True), sink) for s, sink in zip(ss, sinks)]
        es = jnp.concatenate([jnp.exp2(s - m).astype(BF16) for s, m in zip(ss, ms)], axis=1)
        r_all = jnp.dot(vaug, es, preferred_element_type=F32)
        rs = [r_all[:, i * blk:(i + 1) * blk] for i in range(ATT_GROUP)]
        outs += [r[:hd] / (r[hd:hd + 1] + jnp.exp2(sink - m)) for r, m, sink in zip(rs, ms, sinks)]
        if g == ATT_KV_HEADS - 1:
            for pr in range(ATT_HEADS // 2):
                pair = jnp.concatenate([outs[2 * pr], outs[2 * pr + 1]], axis=0)
                o_ref[0, sb * blk:(sb + 1) * blk, pr * 2 * hd:(pr + 1) * 2 * hd] = pair.T.astype(BF16)
            outs = []


def _swa(sink, qt, k0, k1, vt, batch, tq):
    S = k0.shape[1]
    nq = S // tq
    kspec = pl.BlockSpec((1, S, ATT_HEAD_DIM), lambda b, n: (b, 0, 0))
    return pl.pallas_call(
        functools.partial(_swa_kernel, tq=tq, seq=S),
        grid=(batch, nq),
        in_specs=[pl.BlockSpec(memory_space=pltpu.SMEM),
                  pl.BlockSpec((ATT_Q_DIM, tq), lambda b, n: (0, b * nq + n)),
                  kspec, kspec,
                  pl.BlockSpec((ATT_KV_DIM, S), lambda b, n: (0, b))],
        out_specs=pl.BlockSpec((1, tq, ATT_Q_DIM), lambda b, n: (b, n, 0)),
        out_shape=jax.ShapeDtypeStruct((batch, S, ATT_Q_DIM), BF16),
        compiler_params=_cparams("swa", ("parallel", "parallel")),
        name="swa",
    )(sink, qt, k0, k1, vt)


def _log2_sigmoid(u):
    return jnp.minimum(u, 0.0) * LOG2E - jnp.log2(1.0 + jnp.exp2(jnp.abs(u) * -LOG2E))


def _split2(x):
    hi = x.astype(BF16)
    return hi, (x - hi.astype(F32)).astype(BF16)


GLA_WAYS = 4


def _gla_kernel(z_ref, q_ref, k_ref, v_ref, r_ref, upw_ref, bias_ref, gain_ref,
                o_ref, cf_ref, cb_ref, kef_ref, keb_ref, st_ref, s_ref, *, seq):
    L = GLA_CHUNK
    R2 = 2 * GLA_GATE_RANK
    nc = seq // L
    grp = 4 * L
    cpg = grp // L
    dk = GLA_DK
    mm = functools.partial(jnp.dot, preferred_element_type=F32)
    nt = functools.partial(lax.dot_general, dimension_numbers=NT_DIMS, preferred_element_type=F32)

    row = lax.broadcasted_iota(jnp.int32, (grp, grp), 0)
    col = lax.broadcasted_iota(jnp.int32, (grp, grp), 1)
    same = (row // L) == (col // L)
    fwd_mask = same & (col <= row)
    bwd_mask = same & (col > row)
    tri_lo = jnp.where(fwd_mask, 1.0, 0.0).astype(BF16)
    tri_up = jnp.where(same & (col >= row), 1.0, 0.0).astype(BF16)
    lane = lax.broadcasted_iota(jnp.int32, (1, LANES), 1)
    use_lo = (lane >= R2) & (lane < 2 * R2)

    def group_starts(i):
        return [pl.multiple_of((i * GLA_WAYS + w) * grp, grp) for w in range(GLA_WAYS)]

    def cum_body(i, carry):
        r0s = group_starts(i)
        zs = [_split2(z_ref[0, pl.ds(r0, grp), :]) for r0 in r0s]
        us = [mm(jnp.where(use_lo, zl, zh), upw_ref[...]) + bias_ref[...] for zh, zl in zs]
        las = [_split2(_log2_sigmoid(u) * (1.0 / GLA_GATE_NORM)) for u in us]
        cfxs = [mm(tri_lo, jnp.concatenate([lh[:, :dk], ll[:, :dk]], axis=1)) for lh, ll in las]
        cbxs = [mm(tri_up, jnp.concatenate([lh[:, dk:], ll[:, dk:]], axis=1)) for lh, ll in las]
        for r0, cfx, cbx in zip(r0s, cfxs, cbxs):
            cf = cfx[:, :dk] + cfx[:, dk:]
            cb = cbx[:, :dk] + cbx[:, dk:]
            cf_ref[pl.ds(r0, grp), :] = cf
            cb_ref[pl.ds(r0, grp), :] = cb
            k = k_ref[0, pl.ds(r0, grp), :].astype(F32)
            for c in range(cpg):
                sl = slice(c * L, (c + 1) * L)
                gf = cf[(c + 1) * L - 1:(c + 1) * L]
                gb = cb[c * L:c * L + 1]
                kef_ref[pl.ds(r0 + c * L, L), :] = (k[sl] * jnp.exp2(gf - cf[sl])).astype(BF16)
                keb_ref[pl.ds(r0 + c * L, L), :] = (k[sl] * jnp.exp2(gb - cb[sl])).astype(BF16)
        return carry

    lax.fori_loop(0, seq // (grp * GLA_WAYS), cum_body, 0, unroll=True)

    s_ref[...] = jnp.zeros_like(s_ref)
    zero_k = jnp.zeros((L, dk), BF16)

    def state_body(i, carry):
        j = nc - 1 - i
        rf = pl.multiple_of(i * L, L)
        rb = pl.multiple_of(j * L, L)
        vcat = jnp.concatenate([v_ref[0, pl.ds(rf, L), :], v_ref[0, pl.ds(rb, L), :]], axis=0)
        kblk = jnp.concatenate([jnp.concatenate([kef_ref[pl.ds(rf, L), :], zero_k], axis=1),
                                jnp.concatenate([zero_k, keb_ref[pl.ds(rb, L), :]], axis=1)], axis=0)
        kv = lax.dot_general(vcat, kblk, TN_DIMS, preferred_element_type=F32)
        decay = jnp.exp2(jnp.concatenate([cf_ref[pl.ds(rf + L - 1, 1), :], cb_ref[pl.ds(rb, 1), :]], axis=1))
        s = s_ref[...]
        st_ref[i, :, 0:dk] = s[:, :dk].astype(BF16)
        st_ref[j, :, dk:2 * dk] = s[:, dk:].astype(BF16)
        s_ref[...] = s * decay + kv
        return carry

    lax.fori_loop(0, nc, state_body, 0, unroll=True)

    def out_body(i, carry):
        r0s = group_starts(i)
        ops = []
        for r0 in r0s:
            q = q_ref[0, pl.ds(r0, grp), :].astype(F32)
            k = k_ref[0, pl.ds(r0, grp), :].astype(F32)
            cf = cf_ref[pl.ds(r0, grp), :]
            cb = cb_ref[pl.ds(r0, grp), :]
            ops.append(((q * jnp.exp2(cf)).astype(BF16), (k * jnp.exp2(-cf)).astype(BF16),
                        (q * jnp.exp2(cb)).astype(BF16), (k * jnp.exp2(-cb)).astype(BF16)))
        scores = [(nt(qf, kf), nt(qb, kb)) for qf, kf, qb, kb in ops]
        attns = [jnp.where(fwd_mask, af, jnp.where(bwd_mask, ab, 0.0)).astype(BF16) for af, ab in scores]
        outs = []
        for r0, attn, (qf, _, qb, _) in zip(r0s, attns, ops):
            c0 = r0 // L
            qcat = jnp.concatenate([qf, qb], axis=1)
            inter = jnp.concatenate([nt(qcat[c * L:(c + 1) * L], st_ref[c0 + c]) for c in range(cpg)], axis=0)
            outs.append(mm(attn, v_ref[0, pl.ds(r0, grp), :]) + inter)
        for r0, o in zip(r0s, outs):
            o_ref[0, pl.ds(r0, grp), :] = (_rms(o, gain_ref[...]) * r_ref[0, pl.ds(r0, grp), :].astype(F32)).astype(BF16)
        return carry

    lax.fori_loop(0, seq // (grp * GLA_WAYS), out_body, 0, unroll=True)


def _gla(z, gq, gk, gv, gr, upw, bias, gain):
    B, S, _ = gq.shape
    nc = S // GLA_CHUNK
    seq_blk = lambda n: pl.BlockSpec((1, S, n), lambda b, h: (b, 0, h))
    head_blk = lambda r, n: pl.BlockSpec((None, r, n), lambda b, h: (h, 0, 0))
    return pl.pallas_call(
        functools.partial(_gla_kernel, seq=S),
        grid=(B, GLA_HEADS),
        in_specs=[pl.BlockSpec((1, S, LANES), lambda b, h: (b, 0, 0)),
                  seq_blk(GLA_DK), seq_blk(GLA_DK), seq_blk(GLA_DV), seq_blk(GLA_DV),
                  head_blk(LANES, 2 * GLA_DK), head_blk(1, 2 * GLA_DK),
                  pl.BlockSpec((1, GLA_DV), lambda b, h: (0, h))],
        out_specs=seq_blk(GLA_DV),
        out_shape=jax.ShapeDtypeStruct((B, S, GLA_VAL_DIM), BF16),
        scratch_shapes=[pltpu.VMEM((S, GLA_DK), F32), pltpu.VMEM((S, GLA_DK), F32),
                        pltpu.VMEM((S, GLA_DK), BF16), pltpu.VMEM((S, GLA_DK), BF16),
                        pltpu.VMEM((nc, GLA_DV, 2 * GLA_DK), BF16),
                        pltpu.VMEM((GLA_DV, 2 * GLA_DK), F32)],
        compiler_params=_cparams("gla", ("parallel", "parallel")),
        name="gla",
    )(z, gq, gk, gv, gr, upw, bias, gain)


def _mix_out_kernel(x_ref, a_ref, g_ref, sga_ref, sgg_ref, wa_ref, wb_ref, wo_ref, gain_ref, wr_ref,
                    h_ref, xn_ref, aff_ref, afft_ref):
    sub, starts = _sub_blocks(x_ref.shape[0])
    mm = functools.partial(jnp.dot, preferred_element_type=F32)
    blk = lambda ref, r0: ref[r0:r0 + sub, :]
    pad = jnp.zeros((LANES - N_EXPERTS, sub), F32)

    y_att = [mm(blk(a_ref, r0), wa_ref[...]) for r0 in starts]
    y_gla = [mm(blk(g_ref, r0), wb_ref[...]) for r0 in starts]
    merged = [(blk(sga_ref, r0).astype(F32) * ya + blk(sgg_ref, r0).astype(F32) * yg).astype(BF16)
              for r0, ya, yg in zip(starts, y_att, y_gla)]
    hs = [blk(x_ref, r0) + mm(m, wo_ref[...]) for r0, m in zip(starts, merged)]
    xns = [_rms(h, gain_ref[...]) for h in hs]
    for r0, h, xn in zip(starts, hs, xns):
        h_ref[r0:r0 + sub, :] = h
        _store_token_tiles(xn_ref, xn, r0)
    his = [xn.astype(BF16) for xn in xns]
    parts = [lax.dot_general(wr_ref[...], jnp.concatenate([hi, (xn - hi.astype(F32)).astype(BF16)], axis=1),
                             NT_DIMS, preferred_element_type=F32) for xn, hi in zip(xns, his)]
    for r0, part in zip(starts, parts):
        logits = part[:N_EXPERTS] + part[N_EXPERTS:]
        e = jnp.exp(logits - jnp.max(logits, axis=0, keepdims=True))
        aff_t = e / jnp.sum(e, axis=0, keepdims=True)
        afft_ref[:, r0:r0 + sub] = aff_t
        aff_ref[r0:r0 + sub, :] = jnp.concatenate([aff_t, pad], axis=0).T[:, :N_EXPERTS]


def _mix_out(x2, a, g, sga, sgg, wa, wb, wo, gain, wr, tm):
    T = x2.shape[0]
    row = lambda n: pl.BlockSpec((tm, n), lambda i: (i, 0))
    return pl.pallas_call(
        _mix_out_kernel,
        grid=(T // tm,),
        in_specs=[row(D_MODEL), row(ATT_Q_DIM), row(GLA_VAL_DIM), row(D_MODEL), row(D_MODEL),
                  _full(wa.shape), _full(wb.shape), _full(wo.shape), _full(gain.shape), _full(wr.shape)],
        out_specs=[row(D_MODEL), pl.BlockSpec((tm * ROW_TILES, LANES), lambda i: (i, 0)), row(N_EXPERTS),
                   pl.BlockSpec((N_EXPERTS, tm), lambda i: (0, i))],
        out_shape=[jax.ShapeDtypeStruct((T, D_MODEL), F32), jax.ShapeDtypeStruct((T * ROW_TILES, LANES), F32),
                   jax.ShapeDtypeStruct((T, N_EXPERTS), F32), jax.ShapeDtypeStruct((N_EXPERTS, T), F32)],
        compiler_params=_cparams("mix_out", ("parallel",)),
        name="mix_out",
    )(x2, a, g, sga, sgg, wa, wb, wo, gain, wr)


ROUTE_WAYS = 4


def _route_kernel(aff_ref, idx_ref, cum_ref, *, cap, seq):
    E = N_EXPERTS
    aff = aff_ref[...]
    count = lambda mask: jnp.sum(mask.astype(jnp.int32), axis=1, keepdims=True)
    as_float = lambda pattern: lax.bitcast_convert_type(pattern, F32)

    def largest(nbits, ok):
        def pair(t, p):
            lo = (nbits - nbits % 2) - 2 * (t + 1)
            c = [p | jnp.left_shift(jnp.int32(v), lo) for v in (1, 2, 3)]
            return jnp.where(ok(c[2]), c[2], jnp.where(ok(c[1]), c[1], jnp.where(ok(c[0]), c[0], p)))

        p = jnp.zeros((E, 1), jnp.int32)
        if nbits % 2:
            top = p | jnp.int32(1 << (nbits - 1))
            p = jnp.where(ok(top), top, p)
        return lax.fori_loop(0, nbits // 2, pair, p)

    thr = as_float(largest(31, lambda c: count(aff >= as_float(c)) >= cap))
    above = aff > thr
    tie = aff == thr
    need = cap - count(above)

    pos = lax.broadcasted_iota(jnp.int32, (E, seq), 1)
    last = largest(seq.bit_length() - 1, lambda c: count(tie & (pos < c)) < need)
    sel = (above | (tie & (pos <= last))).astype(BF16)

    nt = seq // LANES
    lrow = lax.broadcasted_iota(jnp.int32, (LANES, LANES), 0)
    lcol = lax.broadcasted_iota(jnp.int32, (LANES, LANES), 1)
    tri = (lrow <= lcol).astype(BF16)
    mm = functools.partial(jnp.dot, preferred_element_type=F32)
    for t in range(nt):
        cum_ref[t * E:(t + 1) * E, :] = mm(sel[:, t * LANES:(t + 1) * LANES], tri)
    tile_of = (lax.broadcasted_iota(jnp.int32, (seq, LANES), 0) // LANES
               == lax.broadcasted_iota(jnp.int32, (seq, LANES), 1)).astype(BF16)
    per_tile = mm(sel, tile_of)
    lane = lax.broadcasted_iota(jnp.int32, (1, LANES), 1)
    far = jnp.float32(2 * seq)
    t_end = jnp.where(lane < nt, mm(per_tile.astype(BF16), tri), far)
    t_start = jnp.where(lane < nt, t_end - per_tile, far)
    pad = jnp.zeros((LANES - E, LANES), F32)
    t_start_cols = jnp.concatenate([jnp.where(lane < nt, t_start, 0.0), pad], axis=0).T

    slot = lax.broadcasted_iota(jnp.int32, (cap, LANES), 0).astype(F32)
    ones = jnp.ones((LANES, LANES), BF16)
    zrows = jnp.zeros((LANES - nt, 2 * LANES), F32)
    for e0 in range(0, E, ROUTE_WAYS):
        es = range(e0, e0 + ROUTE_WAYS)
        tiles, picks, whole = [], [], []
        for e in es:
            absc = cum_ref[pl.ds(e, nt, stride=E), :] + t_start_cols[0:nt, e:e + 1]
            hi = jnp.where(absc >= 256.0, 1.0, 0.0) + jnp.where(absc >= 512.0, 1.0, 0.0)
            lo = absc - 256.0 * hi
            tiles.append(jnp.concatenate([jnp.concatenate([lo, hi], axis=1), zrows], axis=0).astype(BF16))
            done = jnp.where(t_end[e:e + 1] <= slot, 1.0, 0.0)
            whole.append(done)
            picks.append((jnp.where(t_start[e:e + 1] <= slot, 1.0, 0.0) - done).astype(BF16))
        rows = [mm(p, w) for p, w in zip(picks, tiles)]
        votes = [(jnp.where(r[:, :LANES] + 256.0 * r[:, LANES:] <= slot, 1.0, 0.0) + float(LANES) * d).astype(BF16)
                 for r, d in zip(rows, whole)]
        for e, v in zip(es, votes):
            idx_ref[0, :, e:e + 1] = mm(v, ones)[:, e:e + 1].astype(jnp.int32)


def _route(aff_t, batch, cap):
    E, T = aff_t.shape
    B, S = batch, T // batch
    return pl.pallas_call(
        functools.partial(_route_kernel, cap=cap, seq=S),
        grid=(B,),
        in_specs=[pl.BlockSpec((E, S), lambda b: (0, b))],
        out_specs=pl.BlockSpec((1, cap, E), lambda b: (b, 0, 0)),
        out_shape=jax.ShapeDtypeStruct((B, cap, E), jnp.int32),
        scratch_shapes=[pltpu.VMEM((S // LANES * E, LANES), F32)],
        compiler_params=_cparams("route", ("parallel",)),
        name="route",
    )(aff_t)


MOE_EXPERTS_PER_STEP = 4


def _gather_kernel(idx_ref, xn_ref, aff_ref, xg_ref, wg_ref, xs_ref, ws_ref, *, cap):
    n = MOE_EXPERTS_PER_STEP
    lane = lax.broadcasted_iota(jnp.int32, (cap, N_EXPERTS), 1)
    for j in range(n):
        for i in range(cap):
            t = idx_ref[j, 0, i]
            src = pl.multiple_of(t * SUBLANES, SUBLANES)
            xs_ref[j, i * SUBLANES:(i + 1) * SUBLANES, :] = xn_ref[0, pl.ds(src, SUBLANES), :]
            ws_ref[j, i:i + 1, :] = aff_ref[0, pl.ds(t, 1), :]
        xg_ref[0, j] = _load_token_tiles(xs_ref.at[j], cap).astype(BF16)
        e = pl.program_id(1) * n + j
        w = jnp.sum(jnp.where(lane == e, ws_ref[j], 0.0), axis=1, keepdims=True)
        wg_ref[0, j] = jnp.broadcast_to(w, (cap, LANES))


def _gather(idx, xn_tiles, aff, cap):
    B, S, E = aff.shape
    n = MOE_EXPERTS_PER_STEP
    return pl.pallas_call(
        functools.partial(_gather_kernel, cap=cap),
        grid=(B, E // n),
        in_specs=[pl.BlockSpec((n, 1, cap), lambda b, e: (b * (E // n) + e, 0, 0), memory_space=pltpu.SMEM),
                  pl.BlockSpec((1, S * SUBLANES, LANES), lambda b, e: (b, 0, 0)),
                  pl.BlockSpec((1, S, E), lambda b, e: (b, 0, 0))],
        out_specs=[pl.BlockSpec((1, n, cap, D_MODEL), lambda b, e: (b, e, 0, 0)),
                   pl.BlockSpec((1, n, cap, LANES), lambda b, e: (b, e, 0, 0))],
        out_shape=[jax.ShapeDtypeStruct((B, E, cap, D_MODEL), BF16),
                   jax.ShapeDtypeStruct((B, E, cap, LANES), F32)],
        scratch_shapes=[pltpu.VMEM((n, cap * SUBLANES, LANES), F32), pltpu.VMEM((n, cap, E), F32)],
        compiler_params=_cparams("gather", ("arbitrary", "arbitrary")),
        name="gather",
    )(idx.reshape(B * E, 1, cap), xn_tiles, aff)


def _ffn_kernel(xg_ref, wg_ref, w1_ref, w2_ref, w3_ref, y_ref, b1_ref, b2_ref, b3_ref):
    seqs = range(xg_ref.shape[0])
    mm = functools.partial(jnp.dot, preferred_element_type=F32)

    def weights(w_ref, b_ref, fresh):
        if fresh:
            b_ref[...] = w_ref[0].astype(BF16)
        return b_ref[...]

    def step(fresh):
        xgs = [xg_ref[i, 0] for i in seqs]
        w1 = weights(w1_ref, b1_ref, fresh)
        gates = [mm(xg, w1) for xg in xgs]
        w2 = weights(w2_ref, b2_ref, fresh)
        ups = [mm(xg, w2) for xg in xgs]
        hids = [(gate * _sigmoid(gate) * up).astype(BF16) for gate, up in zip(gates, ups)]
        w3 = weights(w3_ref, b3_ref, fresh)
        ys = [mm(hid, w3) * wg_ref[i, 0, :, 0:1] for i, hid in zip(seqs, hids)]
        for i, y in zip(seqs, ys):
            _store_token_tiles(y_ref.at[i, 0], y)

    first = pl.program_id(1) == 0
    pl.when(first)(functools.partial(step, True))
    pl.when(jnp.logical_not(first))(functools.partial(step, False))


FFN_SEQS_PER_STEP = 2


def _ffn(xg, wg, w1, w2, w3):
    B, E, C, _ = wg.shape
    _, D, F = w1.shape
    n = FFN_SEQS_PER_STEP
    return pl.pallas_call(
        _ffn_kernel,
        grid=(E, B // n),
        in_specs=[pl.BlockSpec((n, 1, C, D), lambda e, b: (b, e, 0, 0)),
                  pl.BlockSpec((n, 1, C, LANES), lambda e, b: (b, e, 0, 0)),
                  pl.BlockSpec((1, D, F), lambda e, b: (e, 0, 0)),
                  pl.BlockSpec((1, D, F), lambda e, b: (e, 0, 0)),
                  pl.BlockSpec((1, F, D), lambda e, b: (e, 0, 0))],
        out_specs=pl.BlockSpec((n, 1, C * ROW_TILES, LANES), lambda e, b: (b, e, 0, 0)),
        out_shape=jax.ShapeDtypeStruct((B, E, C * ROW_TILES, LANES), F32),
        scratch_shapes=[pltpu.VMEM((D, F), BF16), pltpu.VMEM((D, F), BF16), pltpu.VMEM((F, D), BF16)],
        compiler_params=_cparams("ffn", ("arbitrary", "arbitrary")),
        name="ffn",
    )(xg, wg, w1, w2, w3)


SCATTER_BATCH = 16


def _scatter_kernel(idx_ref, y_ref, o_ref, *, cap):
    @pl.when(pl.program_id(1) == 0)
    def _():
        o_ref[...] = jnp.zeros_like(o_ref)

    for j in range(MOE_EXPERTS_PER_STEP):
        for i0 in range(0, cap, SCATTER_BATCH):
            slots = range(i0, i0 + SCATTER_BATCH)
            rows = [pl.ds(pl.multiple_of(idx_ref[j, 0, i] * SUBLANES, SUBLANES), SUBLANES) for i in slots]
            new = [o_ref[0, r, :] + y_ref[0, j, i * SUBLANES:(i + 1) * SUBLANES, :] for r, i in zip(rows, slots)]
            for r, v in zip(rows, new):
                o_ref[0, r, :] = v


def _scatter(idx, y_tiles, seq):
    B, E, cap = idx.shape
    n = MOE_EXPERTS_PER_STEP
    return pl.pallas_call(
        functools.partial(_scatter_kernel, cap=cap),
        grid=(B, E // n),
        in_specs=[pl.BlockSpec((n, 1, cap), lambda b, e: (b * (E // n) + e, 0, 0), memory_space=pltpu.SMEM),
                  pl.BlockSpec((1, n, cap * SUBLANES, LANES), lambda b, e: (b, e, 0, 0))],
        out_specs=pl.BlockSpec((1, seq * SUBLANES, LANES), lambda b, e: (b, 0, 0)),
        out_shape=jax.ShapeDtypeStruct((B, seq * SUBLANES, LANES), F32),
        compiler_params=_cparams("scatter", ("arbitrary", "arbitrary")),
        name="scatter",
    )(idx.reshape(B * E, 1, cap), y_tiles)


def _ple_out_kernel(h_ref, moe_ref, p_ref, gple_ref, wpg_ref, wple_ref, gfin_ref, o_ref):
    sub, starts = _sub_blocks(h_ref.shape[0])
    mm = functools.partial(jnp.dot, preferred_element_type=F32)
    hs = [h_ref[r0:r0 + sub, :] + _load_token_tiles(moe_ref, sub, r0) for r0 in starts]
    ns = [_rms(h, gple_ref[...]).astype(BF16) for h in hs]
    gates = [_sigmoid(mm(n, wpg_ref[...])) for n in ns]
    embs = [mm(p_ref[r0:r0 + sub, :].astype(BF16), wple_ref[...]) for r0 in starts]
    for r0, h, gate, emb in zip(starts, hs, gates, embs):
        o_ref[r0:r0 + sub, :] = _rms(h + gate * emb, gfin_ref[...])


def _ple_out(h, moe, p2, gple, wpg, wple, gfin, tm):
    T = h.shape[0]
    row = lambda n: pl.BlockSpec((tm, n), lambda i: (i, 0))
    return pl.pallas_call(
        _ple_out_kernel,
        grid=(T // tm,),
        in_specs=[row(D_MODEL), pl.BlockSpec((tm * ROW_TILES, LANES), lambda i: (i, 0)), row(PLE_DIM),
                  _full(gple.shape), _full(wpg.shape), _full(wple.shape), _full(gfin.shape)],
        out_specs=row(D_MODEL),
        out_shape=jax.ShapeDtypeStruct((T, D_MODEL), F32),
        compiler_params=_cparams("ple_out", ("parallel",)),
        name="ple_out",
    )(h, moe, p2, gple, wpg, wple, gfin)


IN_EDGES = {}
_o = 0
for _name, _n in (("q", ATT_Q_DIM), ("k", ATT_KV_DIM), ("v", ATT_KV_DIM), ("gqk", 2 * GLA_KEY_DIM),
                  ("gv", GLA_VAL_DIM), ("gr", GLA_VAL_DIM), ("z", 2 * GLA_GATE_RANK), ("gate", 2 * D_MODEL)):
    IN_EDGES[_name] = (_o, _o + _n)
    _o += _n
IN_DIM = _o


def _w_prep_kernel(wt_ref, wqt_ref, wkz_ref, wvt_ref, wgqk_ref, wgv_ref, wgr_ref, wgate_ref):
    piece = lambda name: wt_ref[IN_EDGES[name][0]:IN_EDGES[name][1], :]
    wqt_ref[...] = piece("q").astype(BF16)
    wvt_ref[...] = piece("v").astype(BF16)
    z = piece("z")
    wkz_ref[...] = jnp.concatenate([piece("k")] + [z] * (LANES // (2 * GLA_GATE_RANK)), axis=0).T.astype(BF16)
    wgqk_ref[...] = piece("gqk").T.astype(BF16)
    wgv_ref[...] = piece("gv").T.astype(BF16)
    wgr_ref[...] = piece("gr").T.astype(BF16)
    wgate_ref[...] = piece("gate").T.astype(BF16)


def _w_prep(wt, layer, cols):
    D = wt.shape[2]
    row = lambda n: pl.BlockSpec((cols, n), lambda i: (i, 0))
    col = lambda n: pl.BlockSpec((n, cols), lambda i: (0, i))
    widths = (ATT_KV_DIM + LANES, 2 * GLA_KEY_DIM, GLA_VAL_DIM, GLA_VAL_DIM, 2 * D_MODEL)
    out_specs = [col(ATT_Q_DIM), row(widths[0]), col(ATT_KV_DIM)] + [row(n) for n in widths[1:]]
    out_shape = ([jax.ShapeDtypeStruct((ATT_Q_DIM, D), BF16), jax.ShapeDtypeStruct((D, widths[0]), BF16),
                  jax.ShapeDtypeStruct((ATT_KV_DIM, D), BF16)]
                 + [jax.ShapeDtypeStruct((D, n), BF16) for n in widths[1:]])
    return pl.pallas_call(
        _w_prep_kernel,
        grid=(D // cols,),
        in_specs=[pl.BlockSpec((None, IN_DIM, cols), lambda i: (layer, 0, i))],
        out_specs=out_specs,
        out_shape=out_shape,
        compiler_params=_cparams("w_prep", ("parallel",)),
        name="w_prep",
    )(wt)


def kernel(x, p, positions, norm_mix, w_in, gla_gate_up_fwd, gla_gate_bias_fwd, gla_gate_up_bwd, gla_gate_bias_bwd, attn_sink, gla_norm, w_branch_attn, w_branch_gla, w_out, norm_ffn, w_router, w_exp_gate, w_exp_up, w_exp_down, norm_ple, w_ple_gate, w_ple, norm_final):
    B, S, D = x.shape
    T = B * S
    depth = w_in.shape[0]
    assert depth == 1, "the final norm is fused into the (single) layer's PLE kernel"
    cap = CAPACITY_FACTOR * S // N_EXPERTS
    R = GLA_GATE_RANK

    posr = positions.reshape(1, T)
    inv_freq = ROPE_THETA ** (-jnp.arange(0, ROPE_DIM, 2, dtype=F32) / ROPE_DIM)
    invfc = inv_freq.reshape(ROPE_DIM // 2, 1)

    h = x.reshape(T, D)
    for l in range(depth):
        w_pieces = _w_prep(jnp.swapaxes(w_in, 1, 2), l, cols=TILE_ROWS["w_prep"])
        per_head = lambda w: w.reshape(-1, GLA_HEADS, GLA_DK).swapaxes(0, 1)
        upf, upb = per_head(gla_gate_up_fwd[l]), per_head(gla_gate_up_bwd[l])
        up = jnp.concatenate([jnp.concatenate([upf, jnp.zeros_like(upf)], axis=2),
                              jnp.concatenate([jnp.zeros_like(upb), upb], axis=2)], axis=1)
        up_hi = up.astype(BF16)
        up_lo = (up - up_hi.astype(F32)).astype(BF16)
        upw = jnp.concatenate([up_hi, up_hi, up_lo, jnp.zeros_like(up_lo)], axis=1)
        gbias = jnp.concatenate([per_head(gla_gate_bias_fwd[l]), per_head(gla_gate_bias_bwd[l])], axis=2)
        wr = w_router[l]
        wr_hi = wr.astype(BF16)
        wr_lo = (wr - wr_hi.astype(F32)).astype(BF16)
        wr2 = jnp.concatenate([jnp.concatenate([wr_hi, wr_lo], axis=1),
                               jnp.concatenate([wr_hi, jnp.zeros_like(wr_lo)], axis=1)], axis=0).T

        qt, k0, k1, vt, gq, gk, gv, gr, z, sga, sgg = _in_proj(
            h, posr, invfc, norm_mix[l].reshape(1, D), *w_pieces, tm=TILE_ROWS["in_proj"])

        att = _swa(attn_sink[l], qt, k0.reshape(B, S, -1), k1.reshape(B, S, -1), vt, batch=B, tq=TILE_ROWS["swa"])
        gla = _gla(z.reshape(B, S, -1), gq.reshape(B, S, -1), gk.reshape(B, S, -1), gv.reshape(B, S, -1),
                   gr.reshape(B, S, -1), upw, gbias, gla_norm[l].reshape(1, -1))

        h1, xn, aff, aff_t = _mix_out(h, att.reshape(T, -1), gla.reshape(T, -1), sga, sgg,
                                      w_branch_attn[l].astype(BF16), w_branch_gla[l].astype(BF16),
                                      w_out[l].astype(BF16), norm_ffn[l].reshape(1, D), wr2, tm=TILE_ROWS["mix_out"])

        aff3 = aff.reshape(B, S, N_EXPERTS)
        idx = _route(aff_t, B, cap)
        idx = jnp.swapaxes(idx, 1, 2)
        xg, wg = _gather(idx, xn.reshape(B, S * SUBLANES, LANES), aff3, cap)
        y = _ffn(xg, wg, w_exp_gate[l], w_exp_up[l], w_exp_down[l])
        moe = _scatter(idx, y, S)

        h = _ple_out(h1, moe.reshape(T * ROW_TILES, LANES), p[l].reshape(T, PLE_DIM), norm_ple[l].reshape(1, D),
                     w_ple_gate[l].astype(BF16), w_ple[l].astype(BF16), norm_final.reshape(1, D),
                     tm=TILE_ROWS["ple_out"])
    return h.reshape(B, S, D)
```

```python
import functools
import math

import jax
import jax.numpy as jnp
from jax import lax
from jax.experimental import pallas as pl
from jax.experimental.pallas import tpu as pltpu

D_MODEL = 1024
ATT_HEADS = 8
ATT_KV_HEADS = 2
ATT_HEAD_DIM = 64
ATT_GROUP = ATT_HEADS // ATT_KV_HEADS
ATT_Q_DIM = ATT_HEADS * ATT_HEAD_DIM
ATT_KV_DIM = ATT_KV_HEADS * ATT_HEAD_DIM
WINDOW = 128
ROPE_DIM = ATT_HEAD_DIM // 4
ROPE_THETA = 500000.0
GLA_HEADS = 4
GLA_KEY_DIM = D_MODEL // 2
GLA_VAL_DIM = D_MODEL
GLA_DK = GLA_KEY_DIM // GLA_HEADS
GLA_DV = GLA_VAL_DIM // GLA_HEADS
GLA_GATE_RANK = 16
GLA_GATE_NORM = 16.0
GLA_CHUNK = 64
N_EXPERTS = 16
EXPERT_FF = D_MODEL
CAPACITY_FACTOR = 2
PLE_DIM = 256
EPS = 1e-6

LANES = 128
MIB = 1024 * 1024
BF16 = jnp.bfloat16
F32 = jnp.float32
LOG2E = math.log2(math.e)

NT_DIMS = (((1,), (1,)), ((), ()))
TN_DIMS = (((0,), (0,)), ((), ()))

TILE_ROWS = {"w_prep": 256, "in_proj": 512, "swa": 1024, "mix_out": 512, "ple_out": 1024}
VMEM_LIMIT_MIB = {"w_prep": 40, "in_proj": 56, "swa": 32, "gla": 48, "mix_out": 48, "route": 32, "gather": 60,
                  "ffn": 60, "scatter": 60, "ple_out": 48}


def _cparams(name, sem):
    return pltpu.CompilerParams(dimension_semantics=sem, vmem_limit_bytes=VMEM_LIMIT_MIB[name] * MIB)


def _full(shape):
    n = len(shape)
    return pl.BlockSpec(shape, lambda *_: (0,) * n)


def _rms(x, gain):
    ms = jnp.mean(x * x, axis=-1, keepdims=True)
    return x * lax.rsqrt(ms + EPS) * gain


def _sigmoid(x):
    return 0.5 * jnp.tanh(0.5 * x) + 0.5


SUBLANES = 8
ROW_TILES = D_MODEL // LANES


def _store_token_tiles(ref2d, x, first_row=0):
    rows = x.shape[0]
    for j in range(ROW_TILES):
        ref2d[pl.ds(first_row * ROW_TILES + j, rows, stride=ROW_TILES), :] = x[:, j * LANES:(j + 1) * LANES]


def _load_token_tiles(ref2d, rows, first_row=0):
    return jnp.concatenate([ref2d[pl.ds(first_row * ROW_TILES + j, rows, stride=ROW_TILES), :]
                            for j in range(ROW_TILES)], axis=1)


SUB_ROWS = 256


def _sub_blocks(tile_rows):
    return SUB_ROWS, list(range(0, tile_rows, SUB_ROWS))


def _rope_rows(t, cos_r, sin_r, heads):
    half = ROPE_DIM // 2
    rows = []
    for h in range(heads):
        r0 = h * ATT_HEAD_DIM
        t1, t2 = t[r0:r0 + half], t[r0 + half:r0 + ROPE_DIM]
        rows += [t1 * cos_r - t2 * sin_r, t2 * cos_r + t1 * sin_r, t[r0 + ROPE_DIM:r0 + ATT_HEAD_DIM]]
    return jnp.concatenate(rows, axis=0)


def _in_proj_kernel(x_ref, posr_ref, invfc_ref, gain_ref, wqt_ref, wkz_ref, wvt_ref,
                    wgqk_ref, wgv_ref, wgr_ref, wgate_ref,
                    qt_ref, k0_ref, k1_ref, vt_ref, gq_ref, gk_ref, gv_ref, gr_ref, z_ref, sga_ref, sgg_ref):
    a = _rms(x_ref[...], gain_ref[...]).astype(BF16)
    ang_t = invfc_ref[...] * posr_ref[...].astype(F32)
    cos_r, sin_r = jnp.cos(ang_t), jnp.sin(ang_t)

    qt = lax.dot_general(wqt_ref[...], a, NT_DIMS, preferred_element_type=F32)
    qt_ref[...] = (_rope_rows(qt, cos_r, sin_r, ATT_HEADS) * (ATT_HEAD_DIM ** -0.5 * LOG2E)).astype(BF16)

    kz = jnp.dot(a, wkz_ref[...], preferred_element_type=F32)
    z_ref[...] = kz[:, ATT_KV_DIM:]
    k = _rope_rows(kz[:, :ATT_KV_DIM].T, cos_r, sin_r, ATT_KV_HEADS).T.astype(BF16)
    k0_ref[...] = k[:, :ATT_HEAD_DIM]
    k1_ref[...] = k[:, ATT_HEAD_DIM:]
    vt_ref[...] = lax.dot_general(wvt_ref[...], a, NT_DIMS, preferred_element_type=F32).astype(BF16)

    gqk = jnp.dot(a, wgqk_ref[...], preferred_element_type=F32)
    gq_ref[...] = (gqk[:, :GLA_KEY_DIM] * (GLA_DK ** -0.5)).astype(BF16)
    gk_ref[...] = gqk[:, GLA_KEY_DIM:].astype(BF16)
    gv_ref[...] = jnp.dot(a, wgv_ref[...], preferred_element_type=F32).astype(BF16)
    gr = jnp.dot(a, wgr_ref[...], preferred_element_type=F32)
    gr_ref[...] = (gr * _sigmoid(gr)).astype(BF16)
    gates = jnp.dot(a, wgate_ref[...], preferred_element_type=F32)
    sga_ref[...] = _sigmoid(gates[:, :D_MODEL]).astype(BF16)
    sgg_ref[...] = _sigmoid(gates[:, D_MODEL:]).astype(BF16)


def _in_proj(x2, posr, invfc, gain, wqt, wkz, wvt, wgqk, wgv, wgr, wgate, tm):
    T = x2.shape[0]
    row = lambda n: pl.BlockSpec((tm, n), lambda i: (i, 0))
    col = lambda n: pl.BlockSpec((n, tm), lambda i: (0, i))
    row_widths = (ATT_HEAD_DIM, ATT_HEAD_DIM, None, GLA_KEY_DIM, GLA_KEY_DIM, GLA_VAL_DIM,
                  GLA_VAL_DIM, wkz.shape[1] - ATT_KV_DIM, D_MODEL, D_MODEL)
    row_dtypes = (BF16,) * 7 + (F32, BF16, BF16)
    out_specs = [col(ATT_Q_DIM)]
    out_shape = [jax.ShapeDtypeStruct((ATT_Q_DIM, T), BF16)]
    for n, dt in zip(row_widths, row_dtypes):
        if n is None:
            out_specs.append(col(ATT_KV_DIM))
            out_shape.append(jax.ShapeDtypeStruct((ATT_KV_DIM, T), BF16))
        else:
            out_specs.append(row(n))
            out_shape.append(jax.ShapeDtypeStruct((T, n), dt))
    consts = (invfc, gain, wqt, wkz, wvt, wgqk, wgv, wgr, wgate)
    return pl.pallas_call(
        _in_proj_kernel,
        grid=(T // tm,),
        in_specs=[row(D_MODEL), col(1)] + [_full(c.shape) for c in consts],
        out_specs=out_specs,
        out_shape=out_shape,
        compiler_params=_cparams("in_proj", ("parallel",)),
        name="in_proj",
    )(x2, posr, *consts)


def _swa_kernel(sink_ref, qt_ref, k0_ref, k1_ref, vt_ref, o_ref, *, tq, seq):
    blk = WINDOW
    span = 3 * blk
    hd = ATT_HEAD_DIM
    n = pl.program_id(1)
    ones = jnp.ones((16, span), BF16)
    kv_refs = (k0_ref, k1_ref)

    def window_start(sb):
        return pl.multiple_of(jnp.clip(n * tq + (sb - 1) * blk, 0, seq - span), blk)

    def scores(sb, g):
        kw = kv_refs[g][0, pl.ds(window_start(sb), span), :]
        heads = range(g * ATT_GROUP, (g + 1) * ATT_GROUP)
        qs = jnp.concatenate([qt_ref[h * hd:(h + 1) * hd, sb * blk:(sb + 1) * blk] for h in heads], axis=1)
        return jnp.dot(kw, qs, preferred_element_type=F32)

    work = [(sb, g) for sb in range(tq // blk) for g in range(ATT_KV_HEADS)]
    s_next = scores(*work[0])
    outs = []
    for step, (sb, g) in enumerate(work):
        s_all = s_next
        if step + 1 < len(work):
            s_next = scores(*work[step + 1])
        q0 = n * tq + sb * blk
        start = window_start(sb)
        kj = start + lax.broadcasted_iota(jnp.int32, (span, blk), 0)
        qi = q0 + lax.broadcasted_iota(jnp.int32, (span, blk), 1)
        valid = jnp.abs(qi - kj) <= WINDOW
        vaug = jnp.concatenate([vt_ref[g * hd:(g + 1) * hd, pl.ds(start, span)], ones], axis=0)
        sinks = [sink_ref[g * ATT_GROUP + i] * LOG2E for i in range(ATT_GROUP)]
        ss = [jnp.where(valid, s_all[:, i * blk:(i + 1) * blk], -jnp.inf) for i in range(ATT_GROUP)]
        ms = [jnp.maximum(jnp.max(s, axis=0, keepdims=True), sink) for s, sink in zip(ss, sinks)]
        es = jnp.concatenate([jnp.exp2(s - m).astype(BF16) for s, m in zip(ss, ms)], axis=1)
        r_all = jnp.dot(vaug, es, preferred_element_type=F32)
        rs = [r_all[:, i * blk:(i + 1) * blk] for i in range(ATT_GROUP)]
        outs += [r[:hd] / (r[hd:hd + 1] + jnp.exp2(sink - m)) for r, m, sink in zip(rs, ms, sinks)]
        if g == ATT_KV_HEADS - 1:
            for pr in range(ATT_HEADS // 2):
                pair = jnp.concatenate([outs[2 * pr], outs[2 * pr + 1]], axis=0)
                o_ref[0, sb * blk:(sb + 1) * blk, pr * 2 * hd:(pr + 1) * 2 * hd] = pair.T.astype(BF16)
            outs = []


def _swa(sink, qt, k0, k1, vt, batch, tq):
    S = k0.shape[1]
    nq = S // tq
    kspec = pl.BlockSpec((1, S, ATT_HEAD_DIM), lambda b, n: (b, 0, 0))
    return pl.pallas_call(
        functools.partial(_swa_kernel, tq=tq, seq=S),
        grid=(batch, nq),
        in_specs=[pl.BlockSpec(memory_space=pltpu.SMEM),
                  pl.BlockSpec((ATT_Q_DIM, tq), lambda b, n: (0, b * nq + n)),
                  kspec, kspec,
                  pl.BlockSpec((ATT_KV_DIM, S), lambda b, n: (0, b))],
        out_specs=pl.BlockSpec((1, tq, ATT_Q_DIM), lambda b, n: (b, n, 0)),
        out_shape=jax.ShapeDtypeStruct((batch, S, ATT_Q_DIM), BF16),
        compiler_params=_cparams("swa", ("parallel", "parallel")),
        name="swa",
    )(sink, qt, k0, k1, vt)


def _log2_sigmoid(u):
    return jnp.minimum(u, 0.0) * LOG2E - jnp.log2(1.0 + jnp.exp2(jnp.abs(u) * -LOG2E))


def _split2(x):
    hi = x.astype(BF16)
    return hi, (x - hi.astype(F32)).astype(BF16)


GLA_WAYS = 4


def _gla_kernel(z_ref, q_ref, k_ref, v_ref, r_ref, upw_ref, bias_ref, gain_ref,
                o_ref, cf_ref, cb_ref, kef_ref, keb_ref, st_ref, s_ref, *, seq):
    L = GLA_CHUNK
    R2 = 2 * GLA_GATE_RANK
    nc = seq // L
    grp = 4 * L
    cpg = grp // L
    dk = GLA_DK
    mm = functools.partial(jnp.dot, preferred_element_type=F32)
    nt = functools.partial(lax.dot_general, dimension_numbers=NT_DIMS, preferred_element_type=F32)

    row = lax.broadcasted_iota(jnp.int32, (grp, grp), 0)
    col = lax.broadcasted_iota(jnp.int32, (grp, grp), 1)
    same = (row // L) == (col // L)
    fwd_mask = same & (col <= row)
    bwd_mask = same & (col > row)
    tri_lo = jnp.where(fwd_mask, 1.0, 0.0).astype(BF16)
    tri_up = jnp.where(same & (col >= row), 1.0, 0.0).astype(BF16)
    lane = lax.broadcasted_iota(jnp.int32, (1, LANES), 1)
    use_lo = (lane >= R2) & (lane < 2 * R2)

    def group_starts(i):
        return [pl.multiple_of((i * GLA_WAYS + w) * grp, grp) for w in range(GLA_WAYS)]

    def cum_body(i, carry):
        r0s = group_starts(i)
        zs = [_split2(z_ref[0, pl.ds(r0, grp), :]) for r0 in r0s]
        us = [mm(jnp.where(use_lo, zl, zh), upw_ref[...]) + bias_ref[...] for zh, zl in zs]
        las = [_split2(_log2_sigmoid(u) * (1.0 / GLA_GATE_NORM)) for u in us]
        cfxs = [mm(tri_lo, jnp.concatenate([lh[:, :dk], ll[:, :dk]], axis=1)) for lh, ll in las]
        cbxs = [mm(tri_up, jnp.concatenate([lh[:, dk:], ll[:, dk:]], axis=1)) for lh, ll in las]
        for r0, cfx, cbx in zip(r0s, cfxs, cbxs):
            cf = cfx[:, :dk] + cfx[:, dk:]
            cb = cbx[:, :dk] + cbx[:, dk:]
            cf_ref[pl.ds(r0, grp), :] = cf
            cb_ref[pl.ds(r0, grp), :] = cb
            k = k_ref[0, pl.ds(r0, grp), :].astype(F32)
            for c in range(cpg):
                sl = slice(c * L, (c + 1) * L)
                gf = cf[(c + 1) * L - 1:(c + 1) * L]
                gb = cb[c * L:c * L + 1]
                kef_ref[pl.ds(r0 + c * L, L), :] = (k[sl] * jnp.exp2(gf - cf[sl])).astype(BF16)
                keb_ref[pl.ds(r0 + c * L, L), :] = (k[sl] * jnp.exp2(gb - cb[sl])).astype(BF16)
        return carry

    lax.fori_loop(0, seq // (grp * GLA_WAYS), cum_body, 0, unroll=True)

    s_ref[...] = jnp.zeros_like(s_ref)
    zero_k = jnp.zeros((L, dk), BF16)

    def state_body(i, carry):
        j = nc - 1 - i
        rf = pl.multiple_of(i * L, L)
        rb = pl.multiple_of(j * L, L)
        vcat = jnp.concatenate([v_ref[0, pl.ds(rf, L), :], v_ref[0, pl.ds(rb, L), :]], axis=0)
        kblk = jnp.concatenate([jnp.concatenate([kef_ref[pl.ds(rf, L), :], zero_k], axis=1),
                                jnp.concatenate([zero_k, keb_ref[pl.ds(rb, L), :]], axis=1)], axis=0)
        kv = lax.dot_general(vcat, kblk, TN_DIMS, preferred_element_type=F32)
        decay = jnp.exp2(jnp.concatenate([cf_ref[pl.ds(rf + L - 1, 1), :], cb_ref[pl.ds(rb, 1), :]], axis=1))
        s = s_ref[...]
        st_ref[i, :, 0:dk] = s[:, :dk].astype(BF16)
        st_ref[j, :, dk:2 * dk] = s[:, dk:].astype(BF16)
        s_ref[...] = s * decay + kv
        return carry

    lax.fori_loop(0, nc, state_body, 0, unroll=True)

    def out_body(i, carry):
        r0s = group_starts(i)
        ops = []
        for r0 in r0s:
            q = q_ref[0, pl.ds(r0, grp), :].astype(F32)
            k = k_ref[0, pl.ds(r0, grp), :].astype(F32)
            cf = cf_ref[pl.ds(r0, grp), :]
            cb = cb_ref[pl.ds(r0, grp), :]
            ops.append(((q * jnp.exp2(cf)).astype(BF16), (k * jnp.exp2(-cf)).astype(BF16),
                        (q * jnp.exp2(cb)).astype(BF16), (k * jnp.exp2(-cb)).astype(BF16)))
        scores = [(nt(qf, kf), nt(qb, kb)) for qf, kf, qb, kb in ops]
        attns = [jnp.where(fwd_mask, af, jnp.where(bwd_mask, ab, 0.0)).astype(BF16) for af, ab in scores]
        outs = []
        for r0, attn, (qf, _, qb, _) in zip(r0s, attns, ops):
            c0 = r0 // L
            qcat = jnp.concatenate([qf, qb], axis=1)
            inter = jnp.concatenate([nt(qcat[c * L:(c + 1) * L], st_ref[c0 + c]) for c in range(cpg)], axis=0)
            outs.append(mm(attn, v_ref[0, pl.ds(r0, grp), :]) + inter)
        for r0, o in zip(r0s, outs):
            o_ref[0, pl.ds(r0, grp), :] = (_rms(o, gain_ref[...]) * r_ref[0, pl.ds(r0, grp), :].astype(F32)).astype(BF16)
        return carry

    lax.fori_loop(0, seq // (grp * GLA_WAYS), out_body, 0, unroll=True)


def _gla(z, gq, gk, gv, gr, upw, bias, gain):
    B, S, _ = gq.shape
    nc = S // GLA_CHUNK
    seq_blk = lambda n: pl.BlockSpec((1, S, n), lambda b, h: (b, 0, h))
    head_blk = lambda r, n: pl.BlockSpec((None, r, n), lambda b, h: (h, 0, 0))
    return pl.pallas_call(
        functools.partial(_gla_kernel, seq=S),
        grid=(B, GLA_HEADS),
        in_specs=[pl.BlockSpec((1, S, LANES), lambda b, h: (b, 0, 0)),
                  seq_blk(GLA_DK), seq_blk(GLA_DK), seq_blk(GLA_DV), seq_blk(GLA_DV),
                  head_blk(LANES, 2 * GLA_DK), head_blk(1, 2 * GLA_DK),
                  pl.BlockSpec((1, GLA_DV), lambda b, h: (0, h))],
        out_specs=seq_blk(GLA_DV),
        out_shape=jax.ShapeDtypeStruct((B, S, GLA_VAL_DIM), BF16),
        scratch_shapes=[pltpu.VMEM((S, GLA_DK), F32), pltpu.VMEM((S, GLA_DK), F32),
                        pltpu.VMEM((S, GLA_DK), BF16), pltpu.VMEM((S, GLA_DK), BF16),
                        pltpu.VMEM((nc, GLA_DV, 2 * GLA_DK), BF16),
                        pltpu.VMEM((GLA_DV, 2 * GLA_DK), F32)],
        compiler_params=_cparams("gla", ("parallel", "parallel")),
        name="gla",
    )(z, gq, gk, gv, gr, upw, bias, gain)


def _mix_out_kernel(x_ref, a_ref, g_ref, sga_ref, sgg_ref, wa_ref, wb_ref, wo_ref, gain_ref, wr_ref,
                    h_ref, xn_ref, aff_ref, afft_ref):
    sub, starts = _sub_blocks(x_ref.shape[0])
    mm = functools.partial(jnp.dot, preferred_element_type=F32)
    blk = lambda ref, r0: ref[r0:r0 + sub, :]
    pad = jnp.zeros((LANES - N_EXPERTS, sub), F32)

    y_att = [mm(blk(a_ref, r0), wa_ref[...]) for r0 in starts]
    y_gla = [mm(blk(g_ref, r0), wb_ref[...]) for r0 in starts]
    merged = [(blk(sga_ref, r0).astype(F32) * ya + blk(sgg_ref, r0).astype(F32) * yg).astype(BF16)
              for r0, ya, yg in zip(starts, y_att, y_gla)]
    hs = [blk(x_ref, r0) + mm(m, wo_ref[...]) for r0, m in zip(starts, merged)]
    xns = [_rms(h, gain_ref[...]) for h in hs]
    for r0, h, xn in zip(starts, hs, xns):
        h_ref[r0:r0 + sub, :] = h
        _store_token_tiles(xn_ref, xn, r0)
    his = [xn.astype(BF16) for xn in xns]
    parts = [lax.dot_general(wr_ref[...], jnp.concatenate([hi, (xn - hi.astype(F32)).astype(BF16)], axis=1),
                             NT_DIMS, preferred_element_type=F32) for xn, hi in zip(xns, his)]
    for r0, part in zip(starts, parts):
        logits = part[:N_EXPERTS] + part[N_EXPERTS:]
        e = jnp.exp(logits - jnp.max(logits, axis=0, keepdims=True))
        aff_t = e / jnp.sum(e, axis=0, keepdims=True)
        afft_ref[:, r0:r0 + sub] = aff_t
        aff_ref[r0:r0 + sub, :] = jnp.concatenate([aff_t, pad], axis=0).T


def _mix_out(x2, a, g, sga, sgg, wa, wb, wo, gain, wr, tm):
    T = x2.shape[0]
    row = lambda n: pl.BlockSpec((tm, n), lambda i: (i, 0))
    return pl.pallas_call(
        _mix_out_kernel,
        grid=(T // tm,),
        in_specs=[row(D_MODEL), row(ATT_Q_DIM), row(GLA_VAL_DIM), row(D_MODEL), row(D_MODEL),
                  _full(wa.shape), _full(wb.shape), _full(wo.shape), _full(gain.shape), _full(wr.shape)],
        out_specs=[row(D_MODEL), pl.BlockSpec((tm * ROW_TILES, LANES), lambda i: (i, 0)), row(LANES),
                   pl.BlockSpec((N_EXPERTS, tm), lambda i: (0, i))],
        out_shape=[jax.ShapeDtypeStruct((T, D_MODEL), F32), jax.ShapeDtypeStruct((T * ROW_TILES, LANES), F32),
                   jax.ShapeDtypeStruct((T, LANES), F32), jax.ShapeDtypeStruct((N_EXPERTS, T), F32)],
        compiler_params=_cparams("mix_out", ("parallel",)),
        name="mix_out",
    )(x2, a, g, sga, sgg, wa, wb, wo, gain, wr)


ROUTE_WAYS = 4


def _route_kernel(aff_ref, idx_ref, cum_ref, *, cap, seq):
    E = N_EXPERTS
    aff = aff_ref[...]
    count = lambda mask: jnp.sum(mask.astype(jnp.int32), axis=1, keepdims=True)
    as_float = lambda pattern: lax.bitcast_convert_type(pattern, F32)

    def largest(nbits, ok):
        def pair(t, p):
            lo = (nbits - nbits % 2) - 2 * (t + 1)
            c = [p | jnp.left_shift(jnp.int32(v), lo) for v in (1, 2, 3)]
            return jnp.where(ok(c[2]), c[2], jnp.where(ok(c[1]), c[1], jnp.where(ok(c[0]), c[0], p)))

        p = jnp.zeros((E, 1), jnp.int32)
        if nbits % 2:
            top = p | jnp.int32(1 << (nbits - 1))
            p = jnp.where(ok(top), top, p)
        return lax.fori_loop(0, nbits // 2, pair, p)

    thr = as_float(largest(31, lambda c: count(aff >= as_float(c)) >= cap))
    above = aff > thr
    tie = aff == thr
    need = cap - count(above)

    pos = lax.broadcasted_iota(jnp.int32, (E, seq), 1)
    last = largest(seq.bit_length() - 1, lambda c: count(tie & (pos < c)) < need)
    sel = (above | (tie & (pos <= last))).astype(BF16)

    nt = seq // LANES
    lrow = lax.broadcasted_iota(jnp.int32, (LANES, LANES), 0)
    lcol = lax.broadcasted_iota(jnp.int32, (LANES, LANES), 1)
    tri = (lrow <= lcol).astype(BF16)
    mm = functools.partial(jnp.dot, preferred_element_type=F32)
    for t in range(nt):
        cum_ref[t * E:(t + 1) * E, :] = mm(sel[:, t * LANES:(t + 1) * LANES], tri)
    tile_of = (lax.broadcasted_iota(jnp.int32, (seq, LANES), 0) // LANES
               == lax.broadcasted_iota(jnp.int32, (seq, LANES), 1)).astype(BF16)
    per_tile = mm(sel, tile_of)
    lane = lax.broadcasted_iota(jnp.int32, (1, LANES), 1)
    far = jnp.float32(2 * seq)
    t_end = jnp.where(lane < nt, mm(per_tile.astype(BF16), tri), far)
    t_start = jnp.where(lane < nt, t_end - per_tile, far)
    pad = jnp.zeros((LANES - E, LANES), F32)
    t_start_cols = jnp.concatenate([jnp.where(lane < nt, t_start, 0.0), pad], axis=0).T

    slot = lax.broadcasted_iota(jnp.int32, (cap, LANES), 0).astype(F32)
    ones = jnp.ones((LANES, LANES), BF16)
    zrows = jnp.zeros((LANES - nt, 2 * LANES), F32)
    for e0 in range(0, E, ROUTE_WAYS):
        es = range(e0, e0 + ROUTE_WAYS)
        tiles, picks, whole = [], [], []
        for e in es:
            absc = cum_ref[pl.ds(e, nt, stride=E), :] + t_start_cols[0:nt, e:e + 1]
            hi = jnp.where(absc >= 256.0, 1.0, 0.0) + jnp.where(absc >= 512.0, 1.0, 0.0)
            lo = absc - 256.0 * hi
            tiles.append(jnp.concatenate([jnp.concatenate([lo, hi], axis=1), zrows], axis=0).astype(BF16))
            done = jnp.where(t_end[e:e + 1] <= slot, 1.0, 0.0)
            whole.append(done)
            picks.append((jnp.where(t_start[e:e + 1] <= slot, 1.0, 0.0) - done).astype(BF16))
        rows = [mm(p, w) for p, w in zip(picks, tiles)]
        votes = [(jnp.where(r[:, :LANES] + 256.0 * r[:, LANES:] <= slot, 1.0, 0.0) + float(LANES) * d).astype(BF16)
                 for r, d in zip(rows, whole)]
        for e, v in zip(es, votes):
            idx_ref[0, :, e:e + 1] = mm(v, ones)[:, e:e + 1].astype(jnp.int32)


def _route(aff_t, batch, cap):
    E, T = aff_t.shape
    B, S = batch, T // batch
    return pl.pallas_call(
        functools.partial(_route_kernel, cap=cap, seq=S),
        grid=(B,),
        in_specs=[pl.BlockSpec((E, S), lambda b: (0, b))],
        out_specs=pl.BlockSpec((1, cap, E), lambda b: (b, 0, 0)),
        out_shape=jax.ShapeDtypeStruct((B, cap, E), jnp.int32),
        scratch_shapes=[pltpu.VMEM((S // LANES * E, LANES), F32)],
        compiler_params=_cparams("route", ("parallel",)),
        name="route",
    )(aff_t)


MOE_EXPERTS_PER_STEP = 4


def _gather_kernel(idx_ref, xn_ref, aff_ref, xg_ref, wg_ref, xs_ref, ws_ref, *, cap):
    n = MOE_EXPERTS_PER_STEP
    lane = lax.broadcasted_iota(jnp.int32, (cap, LANES), 1)
    for j in range(n):
        for i in range(cap):
            t = idx_ref[j, 0, i]
            src = pl.multiple_of(t * SUBLANES, SUBLANES)
            xs_ref[j, i * SUBLANES:(i + 1) * SUBLANES, :] = xn_ref[0, pl.ds(src, SUBLANES), :]
            ws_ref[j, i:i + 1, :] = aff_ref[0, pl.ds(t, 1), :]
        xg_ref[0, j] = _load_token_tiles(xs_ref.at[j], cap).astype(BF16)
        e = pl.program_id(1) * n + j
        wg_ref[0, j] = jnp.sum(jnp.where(lane == e, ws_ref[j], 0.0), axis=1, keepdims=True)


def _gather(idx, xn_tiles, aff, cap):
    B, E, _ = idx.shape
    S = aff.shape[1]
    n = MOE_EXPERTS_PER_STEP
    return pl.pallas_call(
        functools.partial(_gather_kernel, cap=cap),
        grid=(B, E // n),
        in_specs=[pl.BlockSpec((n, 1, cap), lambda b, e: (b * (E // n) + e, 0, 0), memory_space=pltpu.SMEM),
                  pl.BlockSpec((1, S * SUBLANES, LANES), lambda b, e: (b, 0, 0)),
                  pl.BlockSpec((1, S, LANES), lambda b, e: (b, 0, 0))],
        out_specs=[pl.BlockSpec((1, n, cap, D_MODEL), lambda b, e: (b, e, 0, 0)),
                   pl.BlockSpec((1, n, cap, 1), lambda b, e: (b, e, 0, 0))],
        out_shape=[jax.ShapeDtypeStruct((B, E, cap, D_MODEL), BF16),
                   jax.ShapeDtypeStruct((B, E, cap, 1), F32)],
        scratch_shapes=[pltpu.VMEM((n, cap * SUBLANES, LANES), F32), pltpu.VMEM((n, cap, LANES), F32)],
        compiler_params=_cparams("gather", ("arbitrary", "arbitrary")),
        name="gather",
    )(idx.reshape(B * E, 1, cap), xn_tiles, aff)


def _ffn_kernel(xg_ref, wg_ref, w1_ref, w2_ref, w3_ref, y_ref, b1_ref, b2_ref, b3_ref):
    seqs = range(xg_ref.shape[0])
    mm = functools.partial(jnp.dot, preferred_element_type=F32)

    def weights(w_ref, b_ref, fresh):
        if fresh:
            b_ref[...] = w_ref[0].astype(BF16)
        return b_ref[...]

    def step(fresh):
        xgs = [xg_ref[i, 0] for i in seqs]
        w1 = weights(w1_ref, b1_ref, fresh)
        gates = [mm(xg, w1) for xg in xgs]
        w2 = weights(w2_ref, b2_ref, fresh)
        ups = [mm(xg, w2) for xg in xgs]
        hids = [(gate * _sigmoid(gate) * up).astype(BF16) for gate, up in zip(gates, ups)]
        w3 = weights(w3_ref, b3_ref, fresh)
        ys = [mm(hid, w3) * wg_ref[i, 0] for i, hid in zip(seqs, hids)]
        for i, y in zip(seqs, ys):
            _store_token_tiles(y_ref.at[i, 0], y)

    first = pl.program_id(1) == 0
    pl.when(first)(functools.partial(step, True))
    pl.when(jnp.logical_not(first))(functools.partial(step, False))


FFN_SEQS_PER_STEP = 2


def _ffn(xg, wg, w1, w2, w3):
    B, E, C, _ = wg.shape
    _, D, F = w1.shape
    n = FFN_SEQS_PER_STEP
    return pl.pallas_call(
        _ffn_kernel,
        grid=(E, B // n),
        in_specs=[pl.BlockSpec((n, 1, C, D), lambda e, b: (b, e, 0, 0)),
                  pl.BlockSpec((n, 1, C, 1), lambda e, b: (b, e, 0, 0)),
                  pl.BlockSpec((1, D, F), lambda e, b: (e, 0, 0)),
                  pl.BlockSpec((1, D, F), lambda e, b: (e, 0, 0)),
                  pl.BlockSpec((1, F, D), lambda e, b: (e, 0, 0))],
        out_specs=pl.BlockSpec((n, 1, C * ROW_TILES, LANES), lambda e, b: (b, e, 0, 0)),
        out_shape=jax.ShapeDtypeStruct((B, E, C * ROW_TILES, LANES), F32),
        scratch_shapes=[pltpu.VMEM((D, F), BF16), pltpu.VMEM((D, F), BF16), pltpu.VMEM((F, D), BF16)],
        compiler_params=_cparams("ffn", ("arbitrary", "arbitrary")),
        name="ffn",
    )(xg, wg, w1, w2, w3)


SCATTER_BATCH = 16


def _scatter_kernel(idx_ref, y_ref, o_ref, *, cap):
    @pl.when(pl.program_id(1) == 0)
    def _():
        o_ref[...] = jnp.zeros_like(o_ref)

    for j in range(MOE_EXPERTS_PER_STEP):
        for i0 in range(0, cap, SCATTER_BATCH):
            slots = range(i0, i0 + SCATTER_BATCH)
            rows = [pl.ds(pl.multiple_of(idx_ref[j, 0, i] * SUBLANES, SUBLANES), SUBLANES) for i in slots]
            new = [o_ref[0, r, :] + y_ref[0, j, i * SUBLANES:(i + 1) * SUBLANES, :] for r, i in zip(rows, slots)]
            for r, v in zip(rows, new):
                o_ref[0, r, :] = v


def _scatter(idx, y_tiles, seq):
    B, E, cap = idx.shape
    n = MOE_EXPERTS_PER_STEP
    return pl.pallas_call(
        functools.partial(_scatter_kernel, cap=cap),
        grid=(B, E // n),
        in_specs=[pl.BlockSpec((n, 1, cap), lambda b, e: (b * (E // n) + e, 0, 0), memory_space=pltpu.SMEM),
                  pl.BlockSpec((1, n, cap * SUBLANES, LANES), lambda b, e: (b, e, 0, 0))],
        out_specs=pl.BlockSpec((1, seq * SUBLANES, LANES), lambda b, e: (b, 0, 0)),
        out_shape=jax.ShapeDtypeStruct((B, seq * SUBLANES, LANES), F32),
        compiler_params=_cparams("scatter", ("arbitrary", "arbitrary")),
        name="scatter",
    )(idx.reshape(B * E, 1, cap), y_tiles)


def _ple_out_kernel(h_ref, moe_ref, p_ref, gple_ref, wpg_ref, wple_ref, gfin_ref, o_ref):
    sub, starts = _sub_blocks(h_ref.shape[0])
    mm = functools.partial(jnp.dot, preferred_element_type=F32)
    hs = [h_ref[r0:r0 + sub, :] + _load_token_tiles(moe_ref, sub, r0) for r0 in starts]
    ns = [_rms(h, gple_ref[...]).astype(BF16) for h in hs]
    gates = [_sigmoid(mm(n, wpg_ref[...])) for n in ns]
    embs = [mm(p_ref[r0:r0 + sub, :].astype(BF16), wple_ref[...]) for r0 in starts]
    for r0, h, gate, emb in zip(starts, hs, gates, embs):
        o_ref[r0:r0 + sub, :] = _rms(h + gate * emb, gfin_ref[...])


def _ple_out(h, moe, p2, gple, wpg, wple, gfin, tm):
    T = h.shape[0]
    row = lambda n: pl.BlockSpec((tm, n), lambda i: (i, 0))
    return pl.pallas_call(
        _ple_out_kernel,
        grid=(T // tm,),
        in_specs=[row(D_MODEL), pl.BlockSpec((tm * ROW_TILES, LANES), lambda i: (i, 0)), row(PLE_DIM),
                  _full(gple.shape), _full(wpg.shape), _full(wple.shape), _full(gfin.shape)],
        out_specs=row(D_MODEL),
        out_shape=jax.ShapeDtypeStruct((T, D_MODEL), F32),
        compiler_params=_cparams("ple_out", ("parallel",)),
        name="ple_out",
    )(h, moe, p2, gple, wpg, wple, gfin)


IN_EDGES = {}
_o = 0
for _name, _n in (("q", ATT_Q_DIM), ("k", ATT_KV_DIM), ("v", ATT_KV_DIM), ("gqk", 2 * GLA_KEY_DIM),
                  ("gv", GLA_VAL_DIM), ("gr", GLA_VAL_DIM), ("z", 2 * GLA_GATE_RANK), ("gate", 2 * D_MODEL)):
    IN_EDGES[_name] = (_o, _o + _n)
    _o += _n
IN_DIM = _o


def _w_prep_kernel(wt_ref, wqt_ref, wkz_ref, wvt_ref, wgqk_ref, wgv_ref, wgr_ref, wgate_ref):
    piece = lambda name: wt_ref[IN_EDGES[name][0]:IN_EDGES[name][1], :]
    wqt_ref[...] = piece("q").astype(BF16)
    wvt_ref[...] = piece("v").astype(BF16)
    z = piece("z")
    wkz_ref[...] = jnp.concatenate([piece("k")] + [z] * (LANES // (2 * GLA_GATE_RANK)), axis=0).T.astype(BF16)
    wgqk_ref[...] = piece("gqk").T.astype(BF16)
    wgv_ref[...] = piece("gv").T.astype(BF16)
    wgr_ref[...] = piece("gr").T.astype(BF16)
    wgate_ref[...] = piece("gate").T.astype(BF16)


def _w_prep(wt, layer, cols):
    D = wt.shape[2]
    row = lambda n: pl.BlockSpec((cols, n), lambda i: (i, 0))
    col = lambda n: pl.BlockSpec((n, cols), lambda i: (0, i))
    widths = (ATT_KV_DIM + LANES, 2 * GLA_KEY_DIM, GLA_VAL_DIM, GLA_VAL_DIM, 2 * D_MODEL)
    out_specs = [col(ATT_Q_DIM), row(widths[0]), col(ATT_KV_DIM)] + [row(n) for n in widths[1:]]
    out_shape = ([jax.ShapeDtypeStruct((ATT_Q_DIM, D), BF16), jax.ShapeDtypeStruct((D, widths[0]), BF16),
                  jax.ShapeDtypeStruct((ATT_KV_DIM, D), BF16)]
                 + [jax.ShapeDtypeStruct((D, n), BF16) for n in widths[1:]])
    return pl.pallas_call(
        _w_prep_kernel,
        grid=(D // cols,),
        in_specs=[pl.BlockSpec((None, IN_DIM, cols), lambda i: (layer, 0, i))],
        out_specs=out_specs,
        out_shape=out_shape,
        compiler_params=_cparams("w_prep", ("parallel",)),
        name="w_prep",
    )(wt)


def kernel(x, p, positions, norm_mix, w_in, gla_gate_up_fwd, gla_gate_bias_fwd, gla_gate_up_bwd, gla_gate_bias_bwd, attn_sink, gla_norm, w_branch_attn, w_branch_gla, w_out, norm_ffn, w_router, w_exp_gate, w_exp_up, w_exp_down, norm_ple, w_ple_gate, w_ple, norm_final):
    B, S, D = x.shape
    T = B * S
    depth = w_in.shape[0]
    assert depth == 1, "the final norm is fused into the (single) layer's PLE kernel"
    cap = CAPACITY_FACTOR * S // N_EXPERTS
    R = GLA_GATE_RANK

    posr = positions.reshape(1, T)
    inv_freq = ROPE_THETA ** (-jnp.arange(0, ROPE_DIM, 2, dtype=F32) / ROPE_DIM)
    invfc = inv_freq.reshape(ROPE_DIM // 2, 1)

    h = x.reshape(T, D)
    for l in range(depth):
        w_pieces = _w_prep(jnp.swapaxes(w_in, 1, 2), l, cols=TILE_ROWS["w_prep"])
        per_head = lambda w: w.reshape(-1, GLA_HEADS, GLA_DK).swapaxes(0, 1)
        upf, upb = per_head(gla_gate_up_fwd[l]), per_head(gla_gate_up_bwd[l])
        up = jnp.concatenate([jnp.concatenate([upf, jnp.zeros_like(upf)], axis=2),
                              jnp.concatenate([jnp.zeros_like(upb), upb], axis=2)], axis=1)
        up_hi = up.astype(BF16)
        up_lo = (up - up_hi.astype(F32)).astype(BF16)
        upw = jnp.concatenate([up_hi, up_hi, up_lo, jnp.zeros_like(up_lo)], axis=1)
        gbias = jnp.concatenate([per_head(gla_gate_bias_fwd[l]), per_head(gla_gate_bias_bwd[l])], axis=2)
        wr = w_router[l]
        wr_hi = wr.astype(BF16)
        wr_lo = (wr - wr_hi.astype(F32)).astype(BF16)
        wr2 = jnp.concatenate([jnp.concatenate([wr_hi, wr_lo], axis=1),
                               jnp.concatenate([wr_hi, jnp.zeros_like(wr_lo)], axis=1)], axis=0).T

        qt, k0, k1, vt, gq, gk, gv, gr, z, sga, sgg = _in_proj(
            h, posr, invfc, norm_mix[l].reshape(1, D), *w_pieces, tm=TILE_ROWS["in_proj"])

        att = _swa(attn_sink[l], qt, k0.reshape(B, S, -1), k1.reshape(B, S, -1), vt, batch=B, tq=TILE_ROWS["swa"])
        gla = _gla(z.reshape(B, S, -1), gq.reshape(B, S, -1), gk.reshape(B, S, -1), gv.reshape(B, S, -1),
                   gr.reshape(B, S, -1), upw, gbias, gla_norm[l].reshape(1, -1))

        h1, xn, aff, aff_t = _mix_out(h, att.reshape(T, -1), gla.reshape(T, -1), sga, sgg,
                                      w_branch_attn[l].astype(BF16), w_branch_gla[l].astype(BF16),
                                      w_out[l].astype(BF16), norm_ffn[l].reshape(1, D), wr2, tm=TILE_ROWS["mix_out"])

        aff3 = aff.reshape(B, S, LANES)
        idx = _route(aff_t, B, cap)
        idx = jnp.swapaxes(idx, 1, 2)
        xg, wg = _gather(idx, xn.reshape(B, S * SUBLANES, LANES), aff3, cap)
        y = _ffn(xg, wg, w_exp_gate[l], w_exp_up[l], w_exp_down[l])
        moe = _scatter(idx, y, S)

        h = _ple_out(h1, moe.reshape(T * ROW_TILES, LANES), p[l].reshape(T, PLE_DIM), norm_ple[l].reshape(1, D),
                     w_ple_gate[l].astype(BF16), w_ple[l].astype(BF16), norm_final.reshape(1, D),
                     tm=TILE_ROWS["ple_out"])
    return h.reshape(B, S, D)
```

```python
import functools
import math

import jax
import jax.numpy as jnp
from jax import lax
from jax.experimental import pallas as pl
from jax.experimental.pallas import tpu as pltpu

D_MODEL = 1024
ATT_HEADS = 8
ATT_KV_HEADS = 2
ATT_HEAD_DIM = 64
ATT_GROUP = ATT_HEADS // ATT_KV_HEADS
ATT_Q_DIM = ATT_HEADS * ATT_HEAD_DIM
ATT_KV_DIM = ATT_KV_HEADS * ATT_HEAD_DIM
WINDOW = 128
ROPE_DIM = ATT_HEAD_DIM // 4
ROPE_THETA = 500000.0
GLA_HEADS = 4
GLA_KEY_DIM = D_MODEL // 2
GLA_VAL_DIM = D_MODEL
GLA_DK = GLA_KEY_DIM // GLA_HEADS
GLA_DV = GLA_VAL_DIM // GLA_HEADS
GLA_GATE_RANK = 16
GLA_GATE_NORM = 16.0
GLA_CHUNK = 64
N_EXPERTS = 16
EXPERT_FF = D_MODEL
CAPACITY_FACTOR = 2
PLE_DIM = 256
EPS = 1e-6

LANES = 128
MIB = 1024 * 1024
BF16 = jnp.bfloat16
F32 = jnp.float32
LOG2E = math.log2(math.e)

NT_DIMS = (((1,), (1,)), ((), ()))
TN_DIMS = (((0,), (0,)), ((), ()))

TILE_ROWS = {"w_prep": 256, "in_proj": 512, "swa": 1024, "mix_out": 512, "ple_out": 1024}
VMEM_LIMIT_MIB = {"w_prep": 40, "in_proj": 56, "swa": 32, "gla": 48, "mix_out": 48, "route": 32, "gather": 60,
                  "ffn": 60, "scatter": 60, "ple_out": 48}


def _cparams(name, sem):
    return pltpu.CompilerParams(dimension_semantics=sem, vmem_limit_bytes=VMEM_LIMIT_MIB[name] * MIB)


def _full(shape):
    n = len(shape)
    return pl.BlockSpec(shape, lambda *_: (0,) * n)


def _unit_rms(x):
    return x * lax.rsqrt(jnp.mean(x * x, axis=-1, keepdims=True) + EPS)


def _rms(x, gain):
    return _unit_rms(x) * gain


def _sigmoid(x):
    return 0.5 * jnp.tanh(0.5 * x) + 0.5


SUBLANES = 8
ROW_TILES = D_MODEL // LANES


def _store_token_tiles(ref2d, x, first_row=0):
    rows = x.shape[0]
    for j in range(ROW_TILES):
        ref2d[pl.ds(first_row * ROW_TILES + j, rows, stride=ROW_TILES), :] = x[:, j * LANES:(j + 1) * LANES]


def _load_token_tiles(ref2d, rows, first_row=0):
    return jnp.concatenate([ref2d[pl.ds(first_row * ROW_TILES + j, rows, stride=ROW_TILES), :]
                            for j in range(ROW_TILES)], axis=1)


SUB_ROWS = 256


def _sub_blocks(tile_rows):
    return SUB_ROWS, list(range(0, tile_rows, SUB_ROWS))


def _rope_rows(t, cos_r, sin_r, heads):
    half = ROPE_DIM // 2
    rows = []
    for h in range(heads):
        r0 = h * ATT_HEAD_DIM
        t1, t2 = t[r0:r0 + half], t[r0 + half:r0 + ROPE_DIM]
        rows += [t1 * cos_r - t2 * sin_r, t2 * cos_r + t1 * sin_r, t[r0 + ROPE_DIM:r0 + ATT_HEAD_DIM]]
    return jnp.concatenate(rows, axis=0)


def _in_proj_kernel(x_ref, posr_ref, invfc_ref, gain_ref, ggain_ref, wqt_ref, wkz_ref, wvt_ref,
                    wgqk_ref, wgv_ref, wgr_ref, wgate_ref,
                    qt_ref, k0_ref, k1_ref, vt_ref, gq_ref, gk_ref, gv_ref, gr_ref, z_ref, sga_ref, sgg_ref):
    a = _rms(x_ref[...], gain_ref[...]).astype(BF16)
    ang_t = invfc_ref[...] * posr_ref[...].astype(F32)
    cos_r, sin_r = jnp.cos(ang_t), jnp.sin(ang_t)

    qt = lax.dot_general(wqt_ref[...], a, NT_DIMS, preferred_element_type=F32)
    qt_ref[...] = (_rope_rows(qt, cos_r, sin_r, ATT_HEADS) * (ATT_HEAD_DIM ** -0.5 * LOG2E)).astype(BF16)

    kz = jnp.dot(a, wkz_ref[...], preferred_element_type=F32)
    z_ref[...] = kz[:, ATT_KV_DIM:]
    k = _rope_rows(kz[:, :ATT_KV_DIM].T, cos_r, sin_r, ATT_KV_HEADS).T.astype(BF16)
    k0_ref[...] = k[:, :ATT_HEAD_DIM]
    k1_ref[...] = k[:, ATT_HEAD_DIM:]
    vt_ref[...] = lax.dot_general(wvt_ref[...], a, NT_DIMS, preferred_element_type=F32).astype(BF16)

    gqk = jnp.dot(a, wgqk_ref[...], preferred_element_type=F32)
    gq_ref[...] = (gqk[:, :GLA_KEY_DIM] * (GLA_DK ** -0.5)).astype(BF16)
    gk_ref[...] = gqk[:, GLA_KEY_DIM:].astype(BF16)
    gv_ref[...] = jnp.dot(a, wgv_ref[...], preferred_element_type=F32).astype(BF16)
    gr = jnp.dot(a, wgr_ref[...], preferred_element_type=F32)
    gr_ref[...] = (gr * _sigmoid(gr) * ggain_ref[...]).astype(BF16)
    gates = jnp.dot(a, wgate_ref[...], preferred_element_type=F32)
    sga_ref[...] = _sigmoid(gates[:, :D_MODEL]).astype(BF16)
    sgg_ref[...] = _sigmoid(gates[:, D_MODEL:]).astype(BF16)


def _in_proj(x2, posr, invfc, gain, ggain, wqt, wkz, wvt, wgqk, wgv, wgr, wgate, tm):
    T = x2.shape[0]
    row = lambda n: pl.BlockSpec((tm, n), lambda i: (i, 0))
    col = lambda n: pl.BlockSpec((n, tm), lambda i: (0, i))
    row_widths = (ATT_HEAD_DIM, ATT_HEAD_DIM, None, GLA_KEY_DIM, GLA_KEY_DIM, GLA_VAL_DIM,
                  GLA_VAL_DIM, wkz.shape[1] - ATT_KV_DIM, D_MODEL, D_MODEL)
    row_dtypes = (BF16,) * 7 + (F32, BF16, BF16)
    out_specs = [col(ATT_Q_DIM)]
    out_shape = [jax.ShapeDtypeStruct((ATT_Q_DIM, T), BF16)]
    for n, dt in zip(row_widths, row_dtypes):
        if n is None:
            out_specs.append(col(ATT_KV_DIM))
            out_shape.append(jax.ShapeDtypeStruct((ATT_KV_DIM, T), BF16))
        else:
            out_specs.append(row(n))
            out_shape.append(jax.ShapeDtypeStruct((T, n), dt))
    consts = (invfc, gain, ggain, wqt, wkz, wvt, wgqk, wgv, wgr, wgate)
    return pl.pallas_call(
        _in_proj_kernel,
        grid=(T // tm,),
        in_specs=[row(D_MODEL), col(1)] + [_full(c.shape) for c in consts],
        out_specs=out_specs,
        out_shape=out_shape,
        compiler_params=_cparams("in_proj", ("parallel",)),
        name="in_proj",
    )(x2, posr, *consts)


def _swa_kernel(sink_ref, qt_ref, k0_ref, k1_ref, vt_ref, o_ref, *, tq, seq):
    blk = WINDOW
    span = 3 * blk
    hd = ATT_HEAD_DIM
    n = pl.program_id(1)
    ones = jnp.ones((16, span), BF16)
    kv_refs = (k0_ref, k1_ref)

    def window_start(sb):
        return pl.multiple_of(jnp.clip(n * tq + (sb - 1) * blk, 0, seq - span), blk)

    def scores(sb, g):
        kw = kv_refs[g][0, pl.ds(window_start(sb), span), :]
        heads = range(g * ATT_GROUP, (g + 1) * ATT_GROUP)
        qs = jnp.concatenate([qt_ref[h * hd:(h + 1) * hd, sb * blk:(sb + 1) * blk] for h in heads], axis=1)
        return jnp.dot(kw, qs, preferred_element_type=F32)

    work = [(sb, g) for sb in range(tq // blk) for g in range(ATT_KV_HEADS)]
    s_next = scores(*work[0])
    outs = []
    for step, (sb, g) in enumerate(work):
        s_all = s_next
        if step + 1 < len(work):
            s_next = scores(*work[step + 1])
        q0 = n * tq + sb * blk
        start = window_start(sb)
        kj = start + lax.broadcasted_iota(jnp.int32, (span, blk), 0)
        qi = q0 + lax.broadcasted_iota(jnp.int32, (span, blk), 1)
        valid = jnp.abs(qi - kj) <= WINDOW
        vaug = jnp.concatenate([vt_ref[g * hd:(g + 1) * hd, pl.ds(start, span)], ones], axis=0)
        sinks = [sink_ref[g * ATT_GROUP + i] * LOG2E for i in range(ATT_GROUP)]
        ss = [jnp.where(valid, s_all[:, i * blk:(i + 1) * blk], -jnp.inf) for i in range(ATT_GROUP)]
        ms = [jnp.maximum(jnp.max(s, axis=0, keepdims=True), sink) for s, sink in zip(ss, sinks)]
        es = jnp.concatenate([jnp.exp2(s - m).astype(BF16) for s, m in zip(ss, ms)], axis=1)
        r_all = jnp.dot(vaug, es, preferred_element_type=F32)
        rs = [r_all[:, i * blk:(i + 1) * blk] for i in range(ATT_GROUP)]
        outs += [r[:hd] / (r[hd:hd + 1] + jnp.exp2(sink - m)) for r, m, sink in zip(rs, ms, sinks)]
        if g == ATT_KV_HEADS - 1:
            for pr in range(ATT_HEADS // 2):
                pair = jnp.concatenate([outs[2 * pr], outs[2 * pr + 1]], axis=0)
                o_ref[0, sb * blk:(sb + 1) * blk, pr * 2 * hd:(pr + 1) * 2 * hd] = pair.T.astype(BF16)
            outs = []


def _swa(sink, qt, k0, k1, vt, batch, tq):
    S = k0.shape[1]
    nq = S // tq
    kspec = pl.BlockSpec((1, S, ATT_HEAD_DIM), lambda b, n: (b, 0, 0))
    return pl.pallas_call(
        functools.partial(_swa_kernel, tq=tq, seq=S),
        grid=(batch, nq),
        in_specs=[pl.BlockSpec(memory_space=pltpu.SMEM),
                  pl.BlockSpec((ATT_Q_DIM, tq), lambda b, n: (0, b * nq + n)),
                  kspec, kspec,
                  pl.BlockSpec((ATT_KV_DIM, S), lambda b, n: (0, b))],
        out_specs=pl.BlockSpec((1, tq, ATT_Q_DIM), lambda b, n: (b, n, 0)),
        out_shape=jax.ShapeDtypeStruct((batch, S, ATT_Q_DIM), BF16),
        compiler_params=_cparams("swa", ("parallel", "parallel")),
        name="swa",
    )(sink, qt, k0, k1, vt)


def _log2_sigmoid(u):
    return jnp.minimum(u, 0.0) * LOG2E - jnp.log2(1.0 + jnp.exp2(jnp.abs(u) * -LOG2E))


def _split2(x):
    hi = x.astype(BF16)
    return hi, (x - hi.astype(F32)).astype(BF16)


GLA_WAYS = 4


def _gla_kernel(z_ref, q_ref, k_ref, v_ref, r_ref, upw_ref, bias_ref,
                o_ref, cf_ref, cb_ref, kef_ref, keb_ref, st_ref, s_ref, *, seq):
    L = GLA_CHUNK
    R2 = 2 * GLA_GATE_RANK
    nc = seq // L
    grp = 4 * L
    cpg = grp // L
    dk = GLA_DK
    mm = functools.partial(jnp.dot, preferred_element_type=F32)
    nt = functools.partial(lax.dot_general, dimension_numbers=NT_DIMS, preferred_element_type=F32)

    row = lax.broadcasted_iota(jnp.int32, (grp, grp), 0)
    col = lax.broadcasted_iota(jnp.int32, (grp, grp), 1)
    same = (row // L) == (col // L)
    fwd_mask = same & (col <= row)
    bwd_mask = same & (col > row)
    tri_lo = jnp.where(fwd_mask, 1.0, 0.0).astype(BF16)
    tri_up = jnp.where(same & (col >= row), 1.0, 0.0).astype(BF16)
    lane = lax.broadcasted_iota(jnp.int32, (1, LANES), 1)
    use_lo = (lane >= R2) & (lane < 2 * R2)

    def group_starts(i):
        return [pl.multiple_of((i * GLA_WAYS + w) * grp, grp) for w in range(GLA_WAYS)]

    def cum_body(i, carry):
        r0s = group_starts(i)
        zs = [_split2(z_ref[0, pl.ds(r0, grp), :]) for r0 in r0s]
        us = [mm(jnp.where(use_lo, zl, zh), upw_ref[...]) + bias_ref[...] for zh, zl in zs]
        las = [_split2(_log2_sigmoid(u) * (1.0 / GLA_GATE_NORM)) for u in us]
        cfxs = [mm(tri_lo, jnp.concatenate([lh[:, :dk], ll[:, :dk]], axis=1)) for lh, ll in las]
        cbxs = [mm(tri_up, jnp.concatenate([lh[:, dk:], ll[:, dk:]], axis=1)) for lh, ll in las]
        for r0, cfx, cbx in zip(r0s, cfxs, cbxs):
            cf = cfx[:, :dk] + cfx[:, dk:]
            cb = cbx[:, :dk] + cbx[:, dk:]
            cf_ref[pl.ds(r0, grp), :] = cf
            cb_ref[pl.ds(r0, grp), :] = cb
            k = k_ref[0, pl.ds(r0, grp), :].astype(F32)
            for c in range(cpg):
                sl = slice(c * L, (c + 1) * L)
                gf = cf[(c + 1) * L - 1:(c + 1) * L]
                gb = cb[c * L:c * L + 1]
                kef_ref[pl.ds(r0 + c * L, L), :] = (k[sl] * jnp.exp2(gf - cf[sl])).astype(BF16)
                keb_ref[pl.ds(r0 + c * L, L), :] = (k[sl] * jnp.exp2(gb - cb[sl])).astype(BF16)
        return carry

    lax.fori_loop(0, seq // (grp * GLA_WAYS), cum_body, 0, unroll=True)

    s_ref[...] = jnp.zeros_like(s_ref)
    zero_k = jnp.zeros((L, dk), BF16)

    def state_body(i, carry):
        j = nc - 1 - i
        rf = pl.multiple_of(i * L, L)
        rb = pl.multiple_of(j * L, L)
        vcat = jnp.concatenate([v_ref[0, pl.ds(rf, L), :], v_ref[0, pl.ds(rb, L), :]], axis=0)
        kblk = jnp.concatenate([jnp.concatenate([kef_ref[pl.ds(rf, L), :], zero_k], axis=1),
                                jnp.concatenate([zero_k, keb_ref[pl.ds(rb, L), :]], axis=1)], axis=0)
        kv = lax.dot_general(vcat, kblk, TN_DIMS, preferred_element_type=F32)
        decay = jnp.exp2(jnp.concatenate([cf_ref[pl.ds(rf + L - 1, 1), :], cb_ref[pl.ds(rb, 1), :]], axis=1))
        s = s_ref[...]
        st_ref[i, :, 0:dk] = s[:, :dk].astype(BF16)
        st_ref[j, :, dk:2 * dk] = s[:, dk:].astype(BF16)
        s_ref[...] = s * decay + kv
        return carry

    lax.fori_loop(0, nc, state_body, 0, unroll=True)

    def out_body(i, carry):
        r0s = group_starts(i)
        ops = []
        for r0 in r0s:
            q = q_ref[0, pl.ds(r0, grp), :].astype(F32)
            k = k_ref[0, pl.ds(r0, grp), :].astype(F32)
            cf = cf_ref[pl.ds(r0, grp), :]
            cb = cb_ref[pl.ds(r0, grp), :]
            ops.append(((q * jnp.exp2(cf)).astype(BF16), (k * jnp.exp2(-cf)).astype(BF16),
                        (q * jnp.exp2(cb)).astype(BF16), (k * jnp.exp2(-cb)).astype(BF16)))
        scores = [(nt(qf, kf), nt(qb, kb)) for qf, kf, qb, kb in ops]
        attns = [jnp.where(fwd_mask, af, jnp.where(bwd_mask, ab, 0.0)).astype(BF16) for af, ab in scores]
        outs = []
        for r0, attn, (qf, _, qb, _) in zip(r0s, attns, ops):
            c0 = r0 // L
            qcat = jnp.concatenate([qf, qb], axis=1)
            inter = jnp.concatenate([nt(qcat[c * L:(c + 1) * L], st_ref[c0 + c]) for c in range(cpg)], axis=0)
            outs.append(mm(attn, v_ref[0, pl.ds(r0, grp), :]) + inter)
        for r0, o in zip(r0s, outs):
            o_ref[0, pl.ds(r0, grp), :] = (_unit_rms(o) * r_ref[0, pl.ds(r0, grp), :].astype(F32)).astype(BF16)
        return carry

    lax.fori_loop(0, seq // (grp * GLA_WAYS), out_body, 0, unroll=True)


def _gla(z, gq, gk, gv, gr, upw, bias):
    B, S, _ = gq.shape
    nc = S // GLA_CHUNK
    seq_blk = lambda n: pl.BlockSpec((1, S, n), lambda b, h: (b, 0, h))
    head_blk = lambda r, n: pl.BlockSpec((None, r, n), lambda b, h: (h, 0, 0))
    return pl.pallas_call(
        functools.partial(_gla_kernel, seq=S),
        grid=(B, GLA_HEADS),
        in_specs=[pl.BlockSpec((1, S, LANES), lambda b, h: (b, 0, 0)),
                  seq_blk(GLA_DK), seq_blk(GLA_DK), seq_blk(GLA_DV), seq_blk(GLA_DV),
                  head_blk(LANES, 2 * GLA_DK), head_blk(1, 2 * GLA_DK)],
        out_specs=seq_blk(GLA_DV),
        out_shape=jax.ShapeDtypeStruct((B, S, GLA_VAL_DIM), BF16),
        scratch_shapes=[pltpu.VMEM((S, GLA_DK), F32), pltpu.VMEM((S, GLA_DK), F32),
                        pltpu.VMEM((S, GLA_DK), BF16), pltpu.VMEM((S, GLA_DK), BF16),
                        pltpu.VMEM((nc, GLA_DV, 2 * GLA_DK), BF16),
                        pltpu.VMEM((GLA_DV, 2 * GLA_DK), F32)],
        compiler_params=_cparams("gla", ("parallel", "parallel")),
        name="gla",
    )(z, gq, gk, gv, gr, upw, bias)


def _mix_out_kernel(x_ref, a_ref, g_ref, sga_ref, sgg_ref, wa_ref, wb_ref, wo_ref, gain_ref, wr_ref,
                    h_ref, xn_ref, aff_ref, afft_ref):
    sub, starts = _sub_blocks(x_ref.shape[0])
    mm = functools.partial(jnp.dot, preferred_element_type=F32)
    blk = lambda ref, r0: ref[r0:r0 + sub, :]
    pad = jnp.zeros((LANES - N_EXPERTS, sub), F32)

    y_att = [mm(blk(a_ref, r0), wa_ref[...]) for r0 in starts]
    y_gla = [mm(blk(g_ref, r0), wb_ref[...]) for r0 in starts]
    merged = [(blk(sga_ref, r0).astype(F32) * ya + blk(sgg_ref, r0).astype(F32) * yg).astype(BF16)
              for r0, ya, yg in zip(starts, y_att, y_gla)]
    hs = [blk(x_ref, r0) + mm(m, wo_ref[...]) for r0, m in zip(starts, merged)]
    xns = [_rms(h, gain_ref[...]) for h in hs]
    for r0, h, xn in zip(starts, hs, xns):
        h_ref[r0:r0 + sub, :] = h
        _store_token_tiles(xn_ref, xn, r0)
    his = [xn.astype(BF16) for xn in xns]
    parts = [lax.dot_general(wr_ref[...], jnp.concatenate([hi, (xn - hi.astype(F32)).astype(BF16)], axis=1),
                             NT_DIMS, preferred_element_type=F32) for xn, hi in zip(xns, his)]
    for r0, part in zip(starts, parts):
        logits = part[:N_EXPERTS] + part[N_EXPERTS:]
        e = jnp.exp(logits - jnp.max(logits, axis=0, keepdims=True))
        aff_t = e / jnp.sum(e, axis=0, keepdims=True)
        afft_ref[:, r0:r0 + sub] = aff_t
        aff_ref[r0:r0 + sub, :] = jnp.concatenate([aff_t, pad], axis=0).T


def _mix_out(x2, a, g, sga, sgg, wa, wb, wo, gain, wr, tm):
    T = x2.shape[0]
    row = lambda n: pl.BlockSpec((tm, n), lambda i: (i, 0))
    return pl.pallas_call(
        _mix_out_kernel,
        grid=(T // tm,),
        in_specs=[row(D_MODEL), row(ATT_Q_DIM), row(GLA_VAL_DIM), row(D_MODEL), row(D_MODEL),
                  _full(wa.shape), _full(wb.shape), _full(wo.shape), _full(gain.shape), _full(wr.shape)],
        out_specs=[row(D_MODEL), pl.BlockSpec((tm * ROW_TILES, LANES), lambda i: (i, 0)), row(LANES),
                   pl.BlockSpec((N_EXPERTS, tm), lambda i: (0, i))],
        out_shape=[jax.ShapeDtypeStruct((T, D_MODEL), F32), jax.ShapeDtypeStruct((T * ROW_TILES, LANES), F32),
                   jax.ShapeDtypeStruct((T, LANES), F32), jax.ShapeDtypeStruct((N_EXPERTS, T), F32)],
        compiler_params=_cparams("mix_out", ("parallel",)),
        name="mix_out",
    )(x2, a, g, sga, sgg, wa, wb, wo, gain, wr)


ROUTE_WAYS = 4


def _route_kernel(aff_ref, idx_ref, cum_ref, *, cap, seq):
    E = N_EXPERTS
    aff = aff_ref[...]
    count = lambda mask: jnp.sum(mask.astype(jnp.int32), axis=1, keepdims=True)
    as_float = lambda pattern: lax.bitcast_convert_type(pattern, F32)

    def largest(nbits, ok):
        def pair(t, p):
            lo = (nbits - nbits % 2) - 2 * (t + 1)
            c = [p | jnp.left_shift(jnp.int32(v), lo) for v in (1, 2, 3)]
            return jnp.where(ok(c[2]), c[2], jnp.where(ok(c[1]), c[1], jnp.where(ok(c[0]), c[0], p)))

        p = jnp.zeros((E, 1), jnp.int32)
        if nbits % 2:
            top = p | jnp.int32(1 << (nbits - 1))
            p = jnp.where(ok(top), top, p)
        return lax.fori_loop(0, nbits // 2, pair, p)

    thr = as_float(largest(31, lambda c: count(aff >= as_float(c)) >= cap))
    above = aff > thr
    tie = aff == thr
    need = cap - count(above)

    pos = lax.broadcasted_iota(jnp.int32, (E, seq), 1)
    last = largest(seq.bit_length() - 1, lambda c: count(tie & (pos < c)) < need)
    sel = (above | (tie & (pos <= last))).astype(BF16)

    nt = seq // LANES
    lrow = lax.broadcasted_iota(jnp.int32, (LANES, LANES), 0)
    lcol = lax.broadcasted_iota(jnp.int32, (LANES, LANES), 1)
    tri = (lrow <= lcol).astype(BF16)
    mm = functools.partial(jnp.dot, preferred_element_type=F32)
    for t in range(nt):
        cum_ref[t * E:(t + 1) * E, :] = mm(sel[:, t * LANES:(t + 1) * LANES], tri)
    tile_of = (lax.broadcasted_iota(jnp.int32, (seq, LANES), 0) // LANES
               == lax.broadcasted_iota(jnp.int32, (seq, LANES), 1)).astype(BF16)
    per_tile = mm(sel, tile_of)
    lane = lax.broadcasted_iota(jnp.int32, (1, LANES), 1)
    far = jnp.float32(2 * seq)
    t_end = jnp.where(lane < nt, mm(per_tile.astype(BF16), tri), far)
    t_start = jnp.where(lane < nt, t_end - per_tile, far)
    pad = jnp.zeros((LANES - E, LANES), F32)
    t_start_cols = jnp.concatenate([jnp.where(lane < nt, t_start, 0.0), pad], axis=0).T

    slot = lax.broadcasted_iota(jnp.int32, (cap, LANES), 0).astype(F32)
    ones = jnp.ones((LANES, LANES), BF16)
    zrows = jnp.zeros((LANES - nt, 2 * LANES), F32)
    for e0 in range(0, E, ROUTE_WAYS):
        es = range(e0, e0 + ROUTE_WAYS)
        tiles, picks, whole = [], [], []
        for e in es:
            absc = cum_ref[pl.ds(e, nt, stride=E), :] + t_start_cols[0:nt, e:e + 1]
            hi = jnp.where(absc >= 256.0, 1.0, 0.0) + jnp.where(absc >= 512.0, 1.0, 0.0)
            lo = absc - 256.0 * hi
            tiles.append(jnp.concatenate([jnp.concatenate([lo, hi], axis=1), zrows], axis=0).astype(BF16))
            done = jnp.where(t_end[e:e + 1] <= slot, 1.0, 0.0)
            whole.append(done)
            picks.append((jnp.where(t_start[e:e + 1] <= slot, 1.0, 0.0) - done).astype(BF16))
        rows = [mm(p, w) for p, w in zip(picks, tiles)]
        votes = [(jnp.where(r[:, :LANES] + 256.0 * r[:, LANES:] <= slot, 1.0, 0.0) + float(LANES) * d).astype(BF16)
                 for r, d in zip(rows, whole)]
        for e, v in zip(es, votes):
            idx_ref[0, :, e:e + 1] = mm(v, ones)[:, e:e + 1].astype(jnp.int32)


def _route(aff_t, batch, cap):
    E, T = aff_t.shape
    B, S = batch, T // batch
    return pl.pallas_call(
        functools.partial(_route_kernel, cap=cap, seq=S),
        grid=(B,),
        in_specs=[pl.BlockSpec((E, S), lambda b: (0, b))],
        out_specs=pl.BlockSpec((1, cap, E), lambda b: (b, 0, 0)),
        out_shape=jax.ShapeDtypeStruct((B, cap, E), jnp.int32),
        scratch_shapes=[pltpu.VMEM((S // LANES * E, LANES), F32)],
        compiler_params=_cparams("route", ("parallel",)),
        name="route",
    )(aff_t)


MOE_EXPERTS_PER_STEP = 4


def _gather_kernel(idx_ref, xn_ref, aff_ref, xg_ref, wg_ref, xs_ref, ws_ref, *, cap):
    n = MOE_EXPERTS_PER_STEP
    lane = lax.broadcasted_iota(jnp.int32, (cap, LANES), 1)
    for j in range(n):
        for i in range(cap):
            t = idx_ref[j, 0, i]
            src = pl.multiple_of(t * SUBLANES, SUBLANES)
            xs_ref[j, i * SUBLANES:(i + 1) * SUBLANES, :] = xn_ref[0, pl.ds(src, SUBLANES), :]
            ws_ref[j, i:i + 1, :] = aff_ref[0, pl.ds(t, 1), :]
        xg_ref[0, j] = _load_token_tiles(xs_ref.at[j], cap).astype(BF16)
        e = pl.program_id(1) * n + j
        wg_ref[0, j] = jnp.sum(jnp.where(lane == e, ws_ref[j], 0.0), axis=1, keepdims=True)


def _gather(idx, xn_tiles, aff, cap):
    B, E, _ = idx.shape
    S = aff.shape[1]
    n = MOE_EXPERTS_PER_STEP
    return pl.pallas_call(
        functools.partial(_gather_kernel, cap=cap),
        grid=(B, E // n),
        in_specs=[pl.BlockSpec((n, 1, cap), lambda b, e: (b * (E // n) + e, 0, 0), memory_space=pltpu.SMEM),
                  pl.BlockSpec((1, S * SUBLANES, LANES), lambda b, e: (b, 0, 0)),
                  pl.BlockSpec((1, S, LANES), lambda b, e: (b, 0, 0))],
        out_specs=[pl.BlockSpec((1, n, cap, D_MODEL), lambda b, e: (b, e, 0, 0)),
                   pl.BlockSpec((1, n, cap, 1), lambda b, e: (b, e, 0, 0))],
        out_shape=[jax.ShapeDtypeStruct((B, E, cap, D_MODEL), BF16),
                   jax.ShapeDtypeStruct((B, E, cap, 1), F32)],
        scratch_shapes=[pltpu.VMEM((n, cap * SUBLANES, LANES), F32), pltpu.VMEM((n, cap, LANES), F32)],
        compiler_params=_cparams("gather", ("arbitrary", "arbitrary")),
        name="gather",
    )(idx.reshape(B * E, 1, cap), xn_tiles, aff)


def _ffn_kernel(xg_ref, wg_ref, w1_ref, w2_ref, w3_ref, y_ref, b1_ref, b2_ref, b3_ref):
    seqs = range(xg_ref.shape[0])
    mm = functools.partial(jnp.dot, preferred_element_type=F32)

    def weights(w_ref, b_ref, fresh):
        if fresh:
            b_ref[...] = w_ref[0].astype(BF16)
        return b_ref[...]

    def step(fresh):
        xgs = [xg_ref[i, 0] for i in seqs]
        w1 = weights(w1_ref, b1_ref, fresh)
        gates = [mm(xg, w1) for xg in xgs]
        w2 = weights(w2_ref, b2_ref, fresh)
        ups = [mm(xg, w2) for xg in xgs]
        hids = [(gate * _sigmoid(gate) * up).astype(BF16) for gate, up in zip(gates, ups)]
        w3 = weights(w3_ref, b3_ref, fresh)
        ys = [mm(hid, w3) * wg_ref[i, 0] for i, hid in zip(seqs, hids)]
        for i, y in zip(seqs, ys):
            _store_token_tiles(y_ref.at[i, 0], y)

    first = pl.program_id(1) == 0
    pl.when(first)(functools.partial(step, True))
    pl.when(jnp.logical_not(first))(functools.partial(step, False))


FFN_SEQS_PER_STEP = 2


def _ffn(xg, wg, w1, w2, w3):
    B, E, C, _ = wg.shape
    _, D, F = w1.shape
    n = FFN_SEQS_PER_STEP
    return pl.pallas_call(
        _ffn_kernel,
        grid=(E, B // n),
        in_specs=[pl.BlockSpec((n, 1, C, D), lambda e, b: (b, e, 0, 0)),
                  pl.BlockSpec((n, 1, C, 1), lambda e, b: (b, e, 0, 0)),
                  pl.BlockSpec((1, D, F), lambda e, b: (e, 0, 0)),
                  pl.BlockSpec((1, D, F), lambda e, b: (e, 0, 0)),
                  pl.BlockSpec((1, F, D), lambda e, b: (e, 0, 0))],
        out_specs=pl.BlockSpec((n, 1, C * ROW_TILES, LANES), lambda e, b: (b, e, 0, 0)),
        out_shape=jax.ShapeDtypeStruct((B, E, C * ROW_TILES, LANES), F32),
        scratch_shapes=[pltpu.VMEM((D, F), BF16), pltpu.VMEM((D, F), BF16), pltpu.VMEM((F, D), BF16)],
        compiler_params=_cparams("ffn", ("arbitrary", "arbitrary")),
        name="ffn",
    )(xg, wg, w1, w2, w3)


SCATTER_BATCH = 16


SCATTER_RING = 3


def _scatter_kernel(idx_ref, y_hbm, o_ref, ybuf, sems, *, cap):
    n = MOE_EXPERTS_PER_STEP
    per_seq = pl.num_programs(1)
    total = pl.num_programs(0) * per_seq
    s = pl.program_id(0) * per_seq + pl.program_id(1)

    def fetch(step):
        slot = step % SCATTER_RING
        return pltpu.make_async_copy(y_hbm.at[step // per_seq, pl.ds((step % per_seq) * n, n)],
                                     ybuf.at[slot], sems.at[slot])

    @pl.when(s == 0)
    def _():
        for ahead in range(SCATTER_RING - 1):
            fetch(ahead).start()

    @pl.when(s + SCATTER_RING - 1 < total)
    def _():
        fetch(s + SCATTER_RING - 1).start()

    @pl.when(pl.program_id(1) == 0)
    def _():
        o_ref[...] = jnp.zeros_like(o_ref)

    fetch(s).wait()
    y_ref = ybuf.at[s % SCATTER_RING]

    for j in range(n):
        for i0 in range(0, cap, SCATTER_BATCH):
            slots = range(i0, i0 + SCATTER_BATCH)
            rows = [pl.ds(pl.multiple_of(idx_ref[j, 0, i] * SUBLANES, SUBLANES), SUBLANES) for i in slots]
            new = [o_ref[0, r, :] + y_ref[j, i * SUBLANES:(i + 1) * SUBLANES, :] for r, i in zip(rows, slots)]
            for r, v in zip(rows, new):
                o_ref[0, r, :] = v


def _scatter(idx, y_tiles, seq):
    B, E, cap = idx.shape
    n = MOE_EXPERTS_PER_STEP
    return pl.pallas_call(
        functools.partial(_scatter_kernel, cap=cap),
        grid=(B, E // n),
        in_specs=[pl.BlockSpec((n, 1, cap), lambda b, e: (b * (E // n) + e, 0, 0), memory_space=pltpu.SMEM),
                  pl.BlockSpec(memory_space=pl.ANY)],
        out_specs=pl.BlockSpec((1, seq * SUBLANES, LANES), lambda b, e: (b, 0, 0)),
        out_shape=jax.ShapeDtypeStruct((B, seq * SUBLANES, LANES), F32),
        scratch_shapes=[pltpu.VMEM((SCATTER_RING, n, cap * SUBLANES, LANES), F32),
                        pltpu.SemaphoreType.DMA((SCATTER_RING,))],
        compiler_params=_cparams("scatter", ("arbitrary", "arbitrary")),
        name="scatter",
    )(idx.reshape(B * E, 1, cap), y_tiles)


def _ple_out_kernel(h_ref, moe_ref, p_ref, wpg_ref, wple_ref, gfin_ref, o_ref):
    sub, starts = _sub_blocks(h_ref.shape[0])
    mm = functools.partial(jnp.dot, preferred_element_type=F32)
    hs = [h_ref[r0:r0 + sub, :] + _load_token_tiles(moe_ref, sub, r0) for r0 in starts]
    ns = [_unit_rms(h).astype(BF16) for h in hs]
    gates = [_sigmoid(mm(n, wpg_ref[...])) for n in ns]
    embs = [mm(p_ref[r0:r0 + sub, :].astype(BF16), wple_ref[...]) for r0 in starts]
    for r0, h, gate, emb in zip(starts, hs, gates, embs):
        o_ref[r0:r0 + sub, :] = _rms(h + gate * emb, gfin_ref[...])


def _ple_out(h, moe, p2, wpg, wple, gfin, tm):
    T = h.shape[0]
    row = lambda n: pl.BlockSpec((tm, n), lambda i: (i, 0))
    return pl.pallas_call(
        _ple_out_kernel,
        grid=(T // tm,),
        in_specs=[row(D_MODEL), pl.BlockSpec((tm * ROW_TILES, LANES), lambda i: (i, 0)), row(PLE_DIM),
                  _full(wpg.shape), _full(wple.shape), _full(gfin.shape)],
        out_specs=row(D_MODEL),
        out_shape=jax.ShapeDtypeStruct((T, D_MODEL), F32),
        compiler_params=_cparams("ple_out", ("parallel",)),
        name="ple_out",
    )(h, moe, p2, wpg, wple, gfin)


IN_EDGES = {}
_o = 0
for _name, _n in (("q", ATT_Q_DIM), ("k", ATT_KV_DIM), ("v", ATT_KV_DIM), ("gqk", 2 * GLA_KEY_DIM),
                  ("gv", GLA_VAL_DIM), ("gr", GLA_VAL_DIM), ("z", 2 * GLA_GATE_RANK), ("gate", 2 * D_MODEL)):
    IN_EDGES[_name] = (_o, _o + _n)
    _o += _n
IN_DIM = _o


def _w_prep_kernel(wt_ref, wqt_ref, wkz_ref, wvt_ref, wgqk_ref, wgv_ref, wgr_ref, wgate_ref):
    piece = lambda name: wt_ref[IN_EDGES[name][0]:IN_EDGES[name][1], :]
    wqt_ref[...] = piece("q").astype(BF16)
    wvt_ref[...] = piece("v").astype(BF16)
    z = piece("z")
    wkz_ref[...] = jnp.concatenate([piece("k")] + [z] * (LANES // (2 * GLA_GATE_RANK)), axis=0).T.astype(BF16)
    wgqk_ref[...] = piece("gqk").T.astype(BF16)
    wgv_ref[...] = piece("gv").T.astype(BF16)
    wgr_ref[...] = piece("gr").T.astype(BF16)
    wgate_ref[...] = piece("gate").T.astype(BF16)


def _w_prep(wt, layer, cols):
    D = wt.shape[2]
    row = lambda n: pl.BlockSpec((cols, n), lambda i: (i, 0))
    col = lambda n: pl.BlockSpec((n, cols), lambda i: (0, i))
    widths = (ATT_KV_DIM + LANES, 2 * GLA_KEY_DIM, GLA_VAL_DIM, GLA_VAL_DIM, 2 * D_MODEL)
    out_specs = [col(ATT_Q_DIM), row(widths[0]), col(ATT_KV_DIM)] + [row(n) for n in widths[1:]]
    out_shape = ([jax.ShapeDtypeStruct((ATT_Q_DIM, D), BF16), jax.ShapeDtypeStruct((D, widths[0]), BF16),
                  jax.ShapeDtypeStruct((ATT_KV_DIM, D), BF16)]
                 + [jax.ShapeDtypeStruct((D, n), BF16) for n in widths[1:]])
    return pl.pallas_call(
        _w_prep_kernel,
        grid=(D // cols,),
        in_specs=[pl.BlockSpec((None, IN_DIM, cols), lambda i: (layer, 0, i))],
        out_specs=out_specs,
        out_shape=out_shape,
        compiler_params=_cparams("w_prep", ("parallel",)),
        name="w_prep",
    )(wt)


def kernel(x, p, positions, norm_mix, w_in, gla_gate_up_fwd, gla_gate_bias_fwd, gla_gate_up_bwd, gla_gate_bias_bwd, attn_sink, gla_norm, w_branch_attn, w_branch_gla, w_out, norm_ffn, w_router, w_exp_gate, w_exp_up, w_exp_down, norm_ple, w_ple_gate, w_ple, norm_final):
    B, S, D = x.shape
    T = B * S
    depth = w_in.shape[0]
    assert depth == 1, "the final norm is fused into the (single) layer's PLE kernel"
    cap = CAPACITY_FACTOR * S // N_EXPERTS
    R = GLA_GATE_RANK

    posr = positions.reshape(1, T)
    inv_freq = ROPE_THETA ** (-jnp.arange(0, ROPE_DIM, 2, dtype=F32) / ROPE_DIM)
    invfc = inv_freq.reshape(ROPE_DIM // 2, 1)

    h = x.reshape(T, D)
    for l in range(depth):
        w_pieces = _w_prep(jnp.swapaxes(w_in, 1, 2), l, cols=TILE_ROWS["w_prep"])
        per_head = lambda w: w.reshape(-1, GLA_HEADS, GLA_DK).swapaxes(0, 1)
        upf, upb = per_head(gla_gate_up_fwd[l]), per_head(gla_gate_up_bwd[l])
        up = jnp.concatenate([jnp.concatenate([upf, jnp.zeros_like(upf)], axis=2),
                              jnp.concatenate([jnp.zeros_like(upb), upb], axis=2)], axis=1)
        up_hi = up.astype(BF16)
        up_lo = (up - up_hi.astype(F32)).astype(BF16)
        upw = jnp.concatenate([up_hi, up_hi, up_lo, jnp.zeros_like(up_lo)], axis=1)
        gbias = jnp.concatenate([per_head(gla_gate_bias_fwd[l]), per_head(gla_gate_bias_bwd[l])], axis=2)
        wr = w_router[l]
        wr_hi = wr.astype(BF16)
        wr_lo = (wr - wr_hi.astype(F32)).astype(BF16)
        wr2 = jnp.concatenate([jnp.concatenate([wr_hi, wr_lo], axis=1),
                               jnp.concatenate([wr_hi, jnp.zeros_like(wr_lo)], axis=1)], axis=0).T

        qt, k0, k1, vt, gq, gk, gv, gr, z, sga, sgg = _in_proj(
            h, posr, invfc, norm_mix[l].reshape(1, D), gla_norm[l].reshape(1, -1), *w_pieces,
            tm=TILE_ROWS["in_proj"])

        att = _swa(attn_sink[l], qt, k0.reshape(B, S, -1), k1.reshape(B, S, -1), vt, batch=B, tq=TILE_ROWS["swa"])
        gla = _gla(z.reshape(B, S, -1), gq.reshape(B, S, -1), gk.reshape(B, S, -1), gv.reshape(B, S, -1),
                   gr.reshape(B, S, -1), upw, gbias)

        h1, xn, aff, aff_t = _mix_out(h, att.reshape(T, -1), gla.reshape(T, -1), sga, sgg,
                                      w_branch_attn[l].astype(BF16), w_branch_gla[l].astype(BF16),
                                      w_out[l].astype(BF16), norm_ffn[l].reshape(1, D), wr2, tm=TILE_ROWS["mix_out"])

        aff3 = aff.reshape(B, S, LANES)
        idx = _route(aff_t, B, cap)
        idx = jnp.swapaxes(idx, 1, 2)
        xg, wg = _gather(idx, xn.reshape(B, S * SUBLANES, LANES), aff3, cap)
        y = _ffn(xg, wg, w_exp_gate[l], w_exp_up[l], w_exp_down[l])
        moe = _scatter(idx, y, S)

        h = _ple_out(h1, moe.reshape(T * ROW_TILES, LANES), p[l].reshape(T, PLE_DIM),
                     (w_ple_gate[l] * norm_ple[l][:, None]).astype(BF16), w_ple[l].astype(BF16),
                     norm_final.reshape(1, D),
                     tm=TILE_ROWS["ple_out"])
    return h.reshape(B, S, D)
```
